```python
import jax
import jax.numpy as jnp
from jax import lax
import numpy as np

D_MODEL = 1024
BATCH = 32
SEQ = 2048
DEPTH = 2

GRID_W = 64
CTX_LEN = 256
NORM_EPS = 1e-6

MLA_HEADS = 8
MLA_Q_RANK = 256
MLA_KV_RANK = 128
MLA_NOPE = 64
MLA_ROPE = 32
MLA_V = 64
MLA_WIDTH = MLA_HEADS * MLA_V
ROPE_BASE = 10000.0
Q_BLOCK = 128

POOL_WINDOWS = (2, 4, 8, 16)
POOL_WIDTH = 512
POOL_GROUP = POOL_WIDTH // len(POOL_WINDOWS)

GLA_HEADS = 4
GLA_DK = 64
GLA_DV = 128
GLA_KW = GLA_HEADS * GLA_DK
GLA_WIDTH = GLA_HEADS * GLA_DV
GLA_GATE_RANK = 16
GLA_TAU = 16.0
GLA_CHUNK = 64

N_BRANCH = 3

IN_NAMES = ('mla_q', 'mla_kv', 'mla_kr', 'mla_gate', 'pool_x', 'pool_gate',
            'gla_q', 'gla_k', 'gla_v', 'gla_af', 'gla_ab', 'gla_gate', 'merge')
IN_SIZES = (MLA_Q_RANK, MLA_KV_RANK, MLA_ROPE, MLA_WIDTH, POOL_WIDTH, POOL_WIDTH,
            GLA_KW, GLA_KW, GLA_WIDTH, GLA_GATE_RANK, GLA_GATE_RANK, GLA_WIDTH, N_BRANCH * D_MODEL)
D_IN = sum(IN_SIZES)

kernel_name = 'hybrid_mla_pool_gla_prefix_dit'


def rmsnorm(x, g):
    xf = x.astype(jnp.float32)
    y = xf * lax.rsqrt(jnp.mean(xf * xf, axis=-1, keepdims=True) + NORM_EPS)
    return (y * g.astype(jnp.float32)).astype(x.dtype)


def split_columns(z):
    offsets = [int(o) for o in np.cumsum(IN_SIZES)[:-1]]
    return dict(zip(IN_NAMES, jnp.split(z, offsets, axis=-1)))


def flip(t):
    return t[:, ::-1]


def axial_rope_tables(row, col):
    half = MLA_ROPE // 2
    inv = ROPE_BASE ** (-jnp.arange(0, half, 2, dtype=jnp.float32) / half)
    ang_r = row.astype(jnp.float32)[:, None] * inv
    ang_c = col.astype(jnp.float32)[:, None] * inv
    ang = jnp.concatenate([ang_r, ang_r, ang_c, ang_c], axis=-1)
    return jnp.cos(ang), jnp.sin(ang)


def apply_rope(x, cos, sin):
    half = MLA_ROPE // 2
    quarter = half // 2

    def rot(v):
        return jnp.concatenate([-v[..., quarter:], v[..., :quarter]], axis=-1)

    rotated = jnp.concatenate([rot(x[..., :half]), rot(x[..., half:])], axis=-1)
    return (x * cos + rotated * sin).astype(x.dtype)


def softmax_attention(q, k, v, scale):
    s = jnp.einsum('bqhd,bkhd->bhqk', q, k).astype(jnp.float32) * scale
    p = jax.nn.softmax(s, axis=-1).astype(v.dtype)
    return jnp.einsum('bhqk,bkhd->bqhd', p, v)


def blocked_attention(q, k, v, scale):
    B, L, H, dk = q.shape
    nblk = L // Q_BLOCK
    qb = q.reshape(B, nblk, Q_BLOCK, H, dk).transpose(1, 0, 2, 3, 4)
    ob = lax.map(lambda qq: softmax_attention(qq, k, v, scale), qb)
    return ob.transpose(1, 0, 2, 3, 4).reshape(B, L, H, v.shape[-1])


def mla_queries(zz, q_norm, w_uq):
    B, L, _ = zz['mla_q'].shape
    q = (rmsnorm(zz['mla_q'], q_norm) @ w_uq).reshape(B, L, MLA_HEADS, MLA_NOPE + MLA_ROPE)
    return q[..., :MLA_NOPE], q[..., MLA_NOPE:]


def mla_keys_values(zz, kv_norm, w_ukv):
    B, L, _ = zz['mla_kv'].shape
    kv = (rmsnorm(zz['mla_kv'], kv_norm) @ w_ukv).reshape(B, L, MLA_HEADS, MLA_NOPE + MLA_V)
    return kv[..., :MLA_NOPE], kv[..., MLA_NOPE:]


def assemble_keys(k_nope, k_rope):
    B, L, H, _ = k_nope.shape
    return jnp.concatenate([k_nope, jnp.broadcast_to(k_rope[:, :, None, :], (B, L, H, MLA_ROPE))], axis=-1)


def mla_branch(z, zc, cos, sin, q_norm, w_uq, kv_norm, w_ukv, with_ctx_out):
    B, L, _ = z['mla_q'].shape
    scale = (MLA_NOPE + MLA_ROPE) ** -0.5
    q_nope, q_rope = mla_queries(z, q_norm, w_uq)
    q = jnp.concatenate([q_nope, apply_rope(q_rope, cos[:, None, :], sin[:, None, :])], axis=-1)
    k_nope, v = mla_keys_values(z, kv_norm, w_ukv)
    k = assemble_keys(k_nope, apply_rope(z['mla_kr'], cos, sin))
    kc_nope, vc = mla_keys_values(zc, kv_norm, w_ukv)
    kc = assemble_keys(kc_nope, zc['mla_kr'])
    k_all = jnp.concatenate([kc, k], axis=1)
    v_all = jnp.concatenate([vc, v], axis=1)
    y = blocked_attention(q, k_all, v_all, scale).reshape(B, L, MLA_WIDTH)
    y = y * jax.nn.silu(z['mla_gate'])
    if with_ctx_out:
        Bc, Lc, _ = zc['mla_q'].shape
        qc = jnp.concatenate(mla_queries(zc, q_norm, w_uq), axis=-1)
        yc = softmax_attention(qc, kc, vc, scale).reshape(Bc, Lc, MLA_WIDTH) * jax.nn.silu(zc['mla_gate'])
    else:
        yc = None
    return y, yc


def multiscale_pool(u):
    B, L, W = u.shape
    uf = u.astype(jnp.float32)
    csum = jnp.concatenate([jnp.zeros((B, 1, W), jnp.float32), jnp.cumsum(uf, axis=1)], axis=1)
    t = jnp.arange(L)
    outs = []
    for g, w in enumerate(POOL_WINDOWS):
        lo = jnp.clip(t - w // 2, 0, L)
        hi = jnp.clip(t + w // 2, 0, L)
        cs = csum[..., g * POOL_GROUP:(g + 1) * POOL_GROUP]
        count = (hi - lo).astype(jnp.float32)[None, :, None]
        outs.append((jnp.take(cs, hi, axis=1) - jnp.take(cs, lo, axis=1)) / count)
    return (jnp.concatenate(outs, axis=-1) - uf).astype(u.dtype)


def pool_branch(zz, pool_w, pool_scale):
    B, L, _ = zz['pool_x'].shape
    pooled = multiscale_pool(zz['pool_x']).reshape(B, L, len(POOL_WINDOWS), POOL_GROUP)
    mixed = jnp.einsum('blgi,gio->blgo', pooled, pool_w).reshape(B, L, POOL_WIDTH)
    return mixed * pool_scale * jax.nn.silu(zz['pool_gate'])


def gla_scan(q, k, v, log_a, s0, with_out):
    B, L, H, _ = q.shape
    n = L // GLA_CHUNK

    def to_chunks(t):
        return t.reshape(B, n, GLA_CHUNK, H, t.shape[-1]).transpose(1, 0, 3, 2, 4)

    mask = jnp.tril(jnp.ones((GLA_CHUNK, GLA_CHUNK), bool))[:, :, None]

    def step(s, inp):
        qq, kk, vv, aa = inp
        b = jnp.cumsum(aa, axis=2)
        b_last = b[:, :, -1:, :]
        s_new = jnp.exp(b_last)[:, :, 0, :, None] * s + jnp.einsum('bhcd,bhce->bhde', kk * jnp.exp(b_last - b), vv)
        if not with_out:
            return s_new, None
        inter = jnp.einsum('bhcd,bhde->bhce', qq * jnp.exp(b), s)
        decay = jnp.exp(jnp.where(mask, b[:, :, :, None, :] - b[:, :, None, :, :], -jnp.inf))
        attn = jnp.einsum('bhid,bhjd,bhijd->bhij', qq, kk, decay)
        intra = jnp.einsum('bhij,bhje->bhie', attn, vv)
        return s_new, inter + intra

    s_fin, out = lax.scan(step, s0, (to_chunks(q), to_chunks(k), to_chunks(v), to_chunks(log_a)))
    if with_out:
        out = out.transpose(1, 0, 3, 2, 4).reshape(B, L, H, v.shape[-1])
    return s_fin, out


def gla_inputs(zz, af_w2, af_b, ab_w2, ab_b):
    B, L, _ = zz['gla_v'].shape
    f32 = jnp.float32
    q = zz['gla_q'].astype(f32).reshape(B, L, GLA_HEADS, GLA_DK) * GLA_DK ** -0.5
    k = zz['gla_k'].astype(f32).reshape(B, L, GLA_HEADS, GLA_DK)
    v = zz['gla_v'].astype(f32).reshape(B, L, GLA_HEADS, GLA_DV)
    log_a_f = (jax.nn.log_sigmoid((zz['gla_af'] @ af_w2 + af_b).astype(f32)) / GLA_TAU).reshape(B, L, GLA_HEADS, GLA_DK)
    log_a_b = (jax.nn.log_sigmoid((zz['gla_ab'] @ ab_w2 + ab_b).astype(f32)) / GLA_TAU).reshape(B, L, GLA_HEADS, GLA_DK)
    return q, k, v, log_a_f, log_a_b


def gla_output(o, zz, g):
    B, L = o.shape[:2]
    o = rmsnorm(o, g).reshape(B, L, GLA_WIDTH).astype(zz['gla_gate'].dtype)
    return o * jax.nn.silu(zz['gla_gate'])


def gla_branch(z, zc, af_w2, af_b, ab_w2, ab_b, gla_norm, with_ctx_out):
    qc, kc, vc, afc, abc = gla_inputs(zc, af_w2, af_b, ab_w2, ab_b)
    s0 = jnp.zeros((qc.shape[0], GLA_HEADS, GLA_DK, GLA_DV), jnp.float32)
    sc_f, oc_f = gla_scan(qc, kc, vc, afc, s0, with_ctx_out)
    sc_b, oc_b = gla_scan(flip(qc), flip(kc), flip(vc), flip(abc), s0, with_ctx_out)
    q, k, v, af, ab = gla_inputs(z, af_w2, af_b, ab_w2, ab_b)
    _, o_f = gla_scan(q, k, v, af, sc_f, True)
    _, o_b = gla_scan(flip(q), flip(k), flip(v), flip(ab), sc_b, True)
    y = gla_output(o_f + flip(o_b), z, gla_norm)
    yc = gla_output(oc_f + flip(oc_b), zc, gla_norm) if with_ctx_out else None
    return y, yc


def merge_branches(zz, y_mla, y_pool, y_gla, w_bm, w_bp, w_bg, w_out):
    gates = jax.nn.sigmoid(zz['merge'].astype(jnp.float32)).astype(y_mla.dtype)
    g_mla, g_pool, g_gla = jnp.split(gates, N_BRANCH, axis=-1)
    merged = g_mla * (y_mla @ w_bm) + g_pool * (y_pool @ w_bp) + g_gla * (y_gla @ w_bg)
    return merged @ w_out


def trunk_layer(x, xc, mod, mod_c, cos, sin, pre_g, post_g, w_in, mla_q_norm, mla_w_uq, mla_kv_norm,
                mla_w_ukv, pool_w, pool_scale, gla_af_w2, gla_af_b, gla_ab_w2, gla_ab_b, gla_norm,
                w_branch_mla, w_branch_pool, w_branch_gla, w_out, with_ctx_out):
    shift, scale, gate = jnp.split(mod[:, None, :], 3, axis=-1)
    shift_c, scale_c, gate_c = jnp.split(mod_c[None, None, :], 3, axis=-1)
    z = split_columns((rmsnorm(x, pre_g) * (1 + scale) + shift) @ w_in)
    zc = split_columns((rmsnorm(xc, pre_g) * (1 + scale_c) + shift_c) @ w_in)
    y_mla, yc_mla = mla_branch(z, zc, cos, sin, mla_q_norm, mla_w_uq, mla_kv_norm, mla_w_ukv, with_ctx_out)
    y_pool = pool_branch(z, pool_w, pool_scale)
    y_gla, yc_gla = gla_branch(z, zc, gla_af_w2, gla_af_b, gla_ab_w2, gla_ab_b, gla_norm, with_ctx_out)
    out = merge_branches(z, y_mla, y_pool, y_gla, w_branch_mla, w_branch_pool, w_branch_gla, w_out)
    x = x + gate * rmsnorm(out, post_g)
    if with_ctx_out:
        yc_pool = pool_branch(zc, pool_w, pool_scale)
        out_c = merge_branches(zc, yc_mla, yc_pool, yc_gla, w_branch_mla, w_branch_pool, w_branch_gla, w_out)
        xc = xc + gate_c * rmsnorm(out_c, post_g)
    return x, xc


def _fwd_setup_inputs(seed: int = 0) -> dict:
    key = jax.random.key(seed)
    ks = jax.random.split(key, 24)
    f32 = jnp.float32

    def nrm(k, shape, s):
        return jax.random.normal(k, shape, f32) * s

    def gain(k, n):
        return 1.0 + 0.1 * jax.random.normal(k, (DEPTH, n), f32)

    return {
        'x': nrm(ks[0], (BATCH, SEQ, D_MODEL), 1.0),
        'c': nrm(ks[1], (BATCH, D_MODEL), 1.0),
        'ctx': nrm(ks[2], (BATCH, CTX_LEN, D_MODEL), 1.0),
        'c_ctx': nrm(ks[3], (D_MODEL,), 1.0),
        'mod_w': nrm(ks[4], (DEPTH, D_MODEL, 3 * D_MODEL), 0.5 * D_MODEL ** -0.5),
        'mod_b': nrm(ks[5], (DEPTH, 3 * D_MODEL), 0.02),
        'pre_norm': gain(ks[6], D_MODEL),
        'post_norm': gain(ks[7], D_MODEL),
        'w_in': nrm(ks[8], (DEPTH, D_MODEL, D_IN), D_MODEL ** -0.5),
        'mla_q_norm': gain(ks[9], MLA_Q_RANK),
        'mla_w_uq': nrm(ks[10], (DEPTH, MLA_Q_RANK, MLA_HEADS * (MLA_NOPE + MLA_ROPE)), MLA_Q_RANK ** -0.5),
        'mla_kv_norm': gain(ks[11], MLA_KV_RANK),
        'mla_w_ukv': nrm(ks[12], (DEPTH, MLA_KV_RANK, MLA_HEADS * (MLA_NOPE + MLA_V)), MLA_KV_RANK ** -0.5),
        'pool_w': nrm(ks[13], (DEPTH, len(POOL_WINDOWS), POOL_GROUP, POOL_GROUP), POOL_GROUP ** -0.5),
        'pool_scale': gain(ks[14], POOL_WIDTH),
        'gla_af_w2': nrm(ks[15], (DEPTH, GLA_GATE_RANK, GLA_KW), GLA_GATE_RANK ** -0.5),
        'gla_af_b': nrm(ks[16], (DEPTH, GLA_KW), 0.1),
        'gla_ab_w2': nrm(ks[17], (DEPTH, GLA_GATE_RANK, GLA_KW), GLA_GATE_RANK ** -0.5),
        'gla_ab_b': nrm(ks[18], (DEPTH, GLA_KW), 0.1),
        'gla_norm': gain(ks[19], GLA_DV),
        'w_branch_mla': nrm(ks[20], (DEPTH, MLA_WIDTH, D_MODEL), MLA_WIDTH ** -0.5),
        'w_branch_pool': nrm(ks[21], (DEPTH, POOL_WIDTH, D_MODEL), POOL_WIDTH ** -0.5),
        'w_branch_gla': nrm(ks[22], (DEPTH, GLA_WIDTH, D_MODEL), GLA_WIDTH ** -0.5),
        'w_out': nrm(ks[23], (DEPTH, D_MODEL, D_MODEL), D_MODEL ** -0.5),
    }


def _fwd_reference(x, c, ctx, c_ctx, mod_w, mod_b, pre_norm, post_norm, w_in, mla_q_norm, mla_w_uq,
              mla_kv_norm, mla_w_ukv, pool_w, pool_scale, gla_af_w2, gla_af_b, gla_ab_w2, gla_ab_b,
              gla_norm, w_branch_mla, w_branch_pool, w_branch_gla, w_out):
    n_tok = x.shape[1]
    rows = n_tok // GRID_W
    row = jnp.repeat(jnp.arange(rows), GRID_W)
    col = jnp.tile(jnp.arange(GRID_W), rows)
    cos, sin = axial_rope_tables(row, col)
    silu_c = jax.nn.silu(c)
    silu_cc = jax.nn.silu(c_ctx)
    xc = ctx
    for l in range(DEPTH):
        mod = silu_c @ mod_w[l] + mod_b[l]
        mod_c = silu_cc @ mod_w[l] + mod_b[l]
        x, xc = trunk_layer(x, xc, mod, mod_c, cos, sin, pre_norm[l], post_norm[l], w_in[l],
                            mla_q_norm[l], mla_w_uq[l], mla_kv_norm[l], mla_w_ukv[l], pool_w[l],
                            pool_scale[l], gla_af_w2[l], gla_af_b[l], gla_ab_w2[l], gla_ab_b[l],
                            gla_norm[l], w_branch_mla[l], w_branch_pool[l], w_branch_gla[l], w_out[l],
                            l < DEPTH - 1)
    return x


import jax as _jax
import jax.numpy as _jnp

TWIN_FORMAT = 'train_step'
FWD_PARAMS = ['x', 'c', 'ctx', 'c_ctx', 'mod_w', 'mod_b', 'pre_norm', 'post_norm', 'w_in', 'mla_q_norm', 'mla_w_uq', 'mla_kv_norm', 'mla_w_ukv', 'pool_w', 'pool_scale', 'gla_af_w2', 'gla_af_b', 'gla_ab_w2', 'gla_ab_b', 'gla_norm', 'w_branch_mla', 'w_branch_pool', 'w_branch_gla', 'w_out']
TWIN_WEIGHTS = ['c_ctx', 'mod_w', 'mod_b', 'pre_norm', 'post_norm', 'w_in', 'mla_q_norm', 'mla_w_uq', 'mla_kv_norm', 'mla_w_ukv', 'pool_w', 'pool_scale', 'gla_af_w2', 'gla_af_b', 'gla_ab_w2', 'gla_ab_b', 'gla_norm', 'w_branch_mla', 'w_branch_pool', 'w_branch_gla', 'w_out']
TWIN_DIFF_INPUT = 'x'
TWIN_INPUTS = ['x', 'c', 'ctx', 'c_ctx', 'mod_w', 'mod_b', 'pre_norm', 'post_norm', 'w_in', 'mla_q_norm', 'mla_w_uq', 'mla_kv_norm', 'mla_w_ukv', 'pool_w', 'pool_scale', 'gla_af_w2', 'gla_af_b', 'gla_ab_w2', 'gla_ab_b', 'gla_norm', 'w_branch_mla', 'w_branch_pool', 'w_branch_gla', 'w_out', 'loss_target', 'm_c_ctx', 'm_mod_w', 'm_mod_b', 'm_pre_norm', 'm_post_norm', 'm_w_in', 'm_mla_q_norm', 'm_mla_w_uq', 'm_mla_kv_norm', 'm_mla_w_ukv', 'm_pool_w', 'm_pool_scale', 'm_gla_af_w2', 'm_gla_af_b', 'm_gla_ab_w2', 'm_gla_ab_b', 'm_gla_norm', 'm_w_branch_mla', 'm_w_branch_pool', 'm_w_branch_gla', 'm_w_out', 'v_c_ctx', 'v_mod_w', 'v_mod_b', 'v_pre_norm', 'v_post_norm', 'v_w_in', 'v_mla_q_norm', 'v_mla_w_uq', 'v_mla_kv_norm', 'v_mla_w_ukv', 'v_pool_w', 'v_pool_scale', 'v_gla_af_w2', 'v_gla_af_b', 'v_gla_ab_w2', 'v_gla_ab_b', 'v_gla_norm', 'v_w_branch_mla', 'v_w_branch_pool', 'v_w_branch_gla', 'v_w_out']
TWIN_OUTPUTS = ['loss', 'grad_x', 'grad_c_ctx', 'grad_mod_w', 'grad_mod_b', 'grad_pre_norm', 'grad_post_norm', 'grad_w_in', 'grad_mla_q_norm', 'grad_mla_w_uq', 'grad_mla_kv_norm', 'grad_mla_w_ukv', 'grad_pool_w', 'grad_pool_scale', 'grad_gla_af_w2', 'grad_gla_af_b', 'grad_gla_ab_w2', 'grad_gla_ab_b', 'grad_gla_norm', 'grad_w_branch_mla', 'grad_w_branch_pool', 'grad_w_branch_gla', 'grad_w_out', 'delta_c_ctx', 'delta_mod_w', 'delta_mod_b', 'delta_pre_norm', 'delta_post_norm', 'delta_w_in', 'delta_mla_q_norm', 'delta_mla_w_uq', 'delta_mla_kv_norm', 'delta_mla_w_ukv', 'delta_pool_w', 'delta_pool_scale', 'delta_gla_af_w2', 'delta_gla_af_b', 'delta_gla_ab_w2', 'delta_gla_ab_b', 'delta_gla_norm', 'delta_w_branch_mla', 'delta_w_branch_pool', 'delta_w_branch_gla', 'delta_w_out', 'new_m_c_ctx', 'new_m_mod_w', 'new_m_mod_b', 'new_m_pre_norm', 'new_m_post_norm', 'new_m_w_in', 'new_m_mla_q_norm', 'new_m_mla_w_uq', 'new_m_mla_kv_norm', 'new_m_mla_w_ukv', 'new_m_pool_w', 'new_m_pool_scale', 'new_m_gla_af_w2', 'new_m_gla_af_b', 'new_m_gla_ab_w2', 'new_m_gla_ab_b', 'new_m_gla_norm', 'new_m_w_branch_mla', 'new_m_w_branch_pool', 'new_m_w_branch_gla', 'new_m_w_out', 'new_v_c_ctx', 'new_v_mod_w', 'new_v_mod_b', 'new_v_pre_norm', 'new_v_post_norm', 'new_v_w_in', 'new_v_mla_q_norm', 'new_v_mla_w_uq', 'new_v_mla_kv_norm', 'new_v_mla_w_ukv', 'new_v_pool_w', 'new_v_pool_scale', 'new_v_gla_af_w2', 'new_v_gla_af_b', 'new_v_gla_ab_w2', 'new_v_gla_ab_b', 'new_v_gla_norm', 'new_v_w_branch_mla', 'new_v_w_branch_pool', 'new_v_w_branch_gla', 'new_v_w_out']
TWIN_LEAF_KINDS = {'loss': 'loss', 'grad_x': 'grad_x', 'grad_c_ctx': 'grad_w', 'grad_mod_w': 'grad_w', 'grad_mod_b': 'grad_w', 'grad_pre_norm': 'grad_w', 'grad_post_norm': 'grad_w', 'grad_w_in': 'grad_w', 'grad_mla_q_norm': 'grad_w', 'grad_mla_w_uq': 'grad_w', 'grad_mla_kv_norm': 'grad_w', 'grad_mla_w_ukv': 'grad_w', 'grad_pool_w': 'grad_w', 'grad_pool_scale': 'grad_w', 'grad_gla_af_w2': 'grad_w', 'grad_gla_af_b': 'grad_w', 'grad_gla_ab_w2': 'grad_w', 'grad_gla_ab_b': 'grad_w', 'grad_gla_norm': 'grad_w', 'grad_w_branch_mla': 'grad_w', 'grad_w_branch_pool': 'grad_w', 'grad_w_branch_gla': 'grad_w', 'grad_w_out': 'grad_w', 'delta_c_ctx': 'delta_w', 'delta_mod_w': 'delta_w', 'delta_mod_b': 'delta_w', 'delta_pre_norm': 'delta_w', 'delta_post_norm': 'delta_w', 'delta_w_in': 'delta_w', 'delta_mla_q_norm': 'delta_w', 'delta_mla_w_uq': 'delta_w', 'delta_mla_kv_norm': 'delta_w', 'delta_mla_w_ukv': 'delta_w', 'delta_pool_w': 'delta_w', 'delta_pool_scale': 'delta_w', 'delta_gla_af_w2': 'delta_w', 'delta_gla_af_b': 'delta_w', 'delta_gla_ab_w2': 'delta_w', 'delta_gla_ab_b': 'delta_w', 'delta_gla_norm': 'delta_w', 'delta_w_branch_mla': 'delta_w', 'delta_w_branch_pool': 'delta_w', 'delta_w_branch_gla': 'delta_w', 'delta_w_out': 'delta_w', 'new_m_c_ctx': 'new_m', 'new_m_mod_w': 'new_m', 'new_m_mod_b': 'new_m', 'new_m_pre_norm': 'new_m', 'new_m_post_norm': 'new_m', 'new_m_w_in': 'new_m', 'new_m_mla_q_norm': 'new_m', 'new_m_mla_w_uq': 'new_m', 'new_m_mla_kv_norm': 'new_m', 'new_m_mla_w_ukv': 'new_m', 'new_m_pool_w': 'new_m', 'new_m_pool_scale': 'new_m', 'new_m_gla_af_w2': 'new_m', 'new_m_gla_af_b': 'new_m', 'new_m_gla_ab_w2': 'new_m', 'new_m_gla_ab_b': 'new_m', 'new_m_gla_norm': 'new_m', 'new_m_w_branch_mla': 'new_m', 'new_m_w_branch_pool': 'new_m', 'new_m_w_branch_gla': 'new_m', 'new_m_w_out': 'new_m', 'new_v_c_ctx': 'new_v', 'new_v_mod_w': 'new_v', 'new_v_mod_b': 'new_v', 'new_v_pre_norm': 'new_v', 'new_v_post_norm': 'new_v', 'new_v_w_in': 'new_v', 'new_v_mla_q_norm': 'new_v', 'new_v_mla_w_uq': 'new_v', 'new_v_mla_kv_norm': 'new_v', 'new_v_mla_w_ukv': 'new_v', 'new_v_pool_w': 'new_v', 'new_v_pool_scale': 'new_v', 'new_v_gla_af_w2': 'new_v', 'new_v_gla_af_b': 'new_v', 'new_v_gla_ab_w2': 'new_v', 'new_v_gla_ab_b': 'new_v', 'new_v_gla_norm': 'new_v', 'new_v_w_branch_mla': 'new_v', 'new_v_w_branch_pool': 'new_v', 'new_v_w_branch_gla': 'new_v', 'new_v_w_out': 'new_v'}


def _forward(args):
    return _fwd_reference(*[args[k] for k in FWD_PARAMS])


def _output_shape():
    out = _jax.eval_shape(lambda: _forward(_fwd_setup_inputs(0)))
    return out.shape, out.dtype

N_MICROBATCH = 1
ADAM_LR = 0.001
ADAM_B1 = 0.9
ADAM_B2 = 0.999
ADAM_EPS = 1e-08
ADAM_WD = 0.01
ADAM_STEP = 10
PER_EXAMPLE_BATCH_AXIS = {'x': 0, 'c': 0, 'ctx': 0, 'loss_target': 0}
SHARED_INPUTS = []
_WEIGHT_DTYPES = {'c_ctx': _jnp.float32, 'mod_w': _jnp.float32, 'mod_b': _jnp.float32, 'pre_norm': _jnp.float32, 'post_norm': _jnp.float32, 'w_in': _jnp.float32, 'mla_q_norm': _jnp.float32, 'mla_w_uq': _jnp.float32, 'mla_kv_norm': _jnp.float32, 'mla_w_ukv': _jnp.float32, 'pool_w': _jnp.float32, 'pool_scale': _jnp.float32, 'gla_af_w2': _jnp.float32, 'gla_af_b': _jnp.float32, 'gla_ab_w2': _jnp.float32, 'gla_ab_b': _jnp.float32, 'gla_norm': _jnp.float32, 'w_branch_mla': _jnp.float32, 'w_branch_pool': _jnp.float32, 'w_branch_gla': _jnp.float32, 'w_out': _jnp.float32}
MOMENT_SCALE = {'c_ctx': 2.995527e-02, 'mod_w': 3.385503e+00, 'mod_b': 6.435000e+00, 'pre_norm': 2.367898e-01, 'post_norm': 7.403055e+00, 'w_in': 9.680428e-02, 'mla_q_norm': 1.980483e-02, 'mla_w_uq': 1.170827e-02, 'mla_kv_norm': 1.379888e-01, 'mla_w_ukv': 3.795332e-02, 'pool_w': 1.257152e-01, 'pool_scale': 1.549734e-01, 'gla_af_w2': 2.645655e-02, 'gla_af_b': 6.840765e-02, 'gla_ab_w2': 2.434966e-02, 'gla_ab_b': 6.109541e-02, 'gla_norm': 2.956160e-01, 'w_branch_mla': 4.452991e-02, 'w_branch_pool': 9.672350e-02, 'w_branch_gla': 1.043899e-01, 'w_out': 1.613428e-01}


def _to_microbatches(a, axis):
    t = _jnp.moveaxis(a, axis, 0)
    t = t.reshape((N_MICROBATCH, t.shape[0] // N_MICROBATCH) + t.shape[1:])
    return _jnp.moveaxis(t, 1, axis + 1)


def setup_inputs(seed: int = 0) -> dict:
    inp = _fwd_setup_inputs(seed)
    key = _jax.random.fold_in(_jax.random.key(seed), 7919)
    shape, _ = _output_shape()
    out = dict(inp)
    out["loss_target"] = _jax.random.normal(_jax.random.fold_in(key, 0), shape, _jnp.float32)
    for i, name in enumerate(TWIN_WEIGHTS):
        w = inp[name].astype(_jnp.float32)
        if MOMENT_SCALE is None:
            s = _jnp.sqrt(_jnp.mean(_jnp.square(w)) + 1e-30)
        else:
            s = MOMENT_SCALE[name]
        km, kv = _jax.random.split(_jax.random.fold_in(key, i + 1))
        out[name] = w
        out["m_" + name] = s * _jax.random.normal(km, w.shape, _jnp.float32)
        out["v_" + name] = (s * s) * _jax.random.uniform(kv, w.shape, _jnp.float32, 0.5, 1.5)
    if N_MICROBATCH > 1:
        for name, axis in PER_EXAMPLE_BATCH_AXIS.items():
            out[name] = _to_microbatches(out[name], axis)
    return {'x': out['x'], 'c': out['c'], 'ctx': out['ctx'], 'c_ctx': out['c_ctx'], 'mod_w': out['mod_w'], 'mod_b': out['mod_b'], 'pre_norm': out['pre_norm'], 'post_norm': out['post_norm'], 'w_in': out['w_in'], 'mla_q_norm': out['mla_q_norm'], 'mla_w_uq': out['mla_w_uq'], 'mla_kv_norm': out['mla_kv_norm'], 'mla_w_ukv': out['mla_w_ukv'], 'pool_w': out['pool_w'], 'pool_scale': out['pool_scale'], 'gla_af_w2': out['gla_af_w2'], 'gla_af_b': out['gla_af_b'], 'gla_ab_w2': out['gla_ab_w2'], 'gla_ab_b': out['gla_ab_b'], 'gla_norm': out['gla_norm'], 'w_branch_mla': out['w_branch_mla'], 'w_branch_pool': out['w_branch_pool'], 'w_branch_gla': out['w_branch_gla'], 'w_out': out['w_out'], 'loss_target': out['loss_target'], 'm_c_ctx': out['m_c_ctx'], 'm_mod_w': out['m_mod_w'], 'm_mod_b': out['m_mod_b'], 'm_pre_norm': out['m_pre_norm'], 'm_post_norm': out['m_post_norm'], 'm_w_in': out['m_w_in'], 'm_mla_q_norm': out['m_mla_q_norm'], 'm_mla_w_uq': out['m_mla_w_uq'], 'm_mla_kv_norm': out['m_mla_kv_norm'], 'm_mla_w_ukv': out['m_mla_w_ukv'], 'm_pool_w': out['m_pool_w'], 'm_pool_scale': out['m_pool_scale'], 'm_gla_af_w2': out['m_gla_af_w2'], 'm_gla_af_b': out['m_gla_af_b'], 'm_gla_ab_w2': out['m_gla_ab_w2'], 'm_gla_ab_b': out['m_gla_ab_b'], 'm_gla_norm': out['m_gla_norm'], 'm_w_branch_mla': out['m_w_branch_mla'], 'm_w_branch_pool': out['m_w_branch_pool'], 'm_w_branch_gla': out['m_w_branch_gla'], 'm_w_out': out['m_w_out'], 'v_c_ctx': out['v_c_ctx'], 'v_mod_w': out['v_mod_w'], 'v_mod_b': out['v_mod_b'], 'v_pre_norm': out['v_pre_norm'], 'v_post_norm': out['v_post_norm'], 'v_w_in': out['v_w_in'], 'v_mla_q_norm': out['v_mla_q_norm'], 'v_mla_w_uq': out['v_mla_w_uq'], 'v_mla_kv_norm': out['v_mla_kv_norm'], 'v_mla_w_ukv': out['v_mla_w_ukv'], 'v_pool_w': out['v_pool_w'], 'v_pool_scale': out['v_pool_scale'], 'v_gla_af_w2': out['v_gla_af_w2'], 'v_gla_af_b': out['v_gla_af_b'], 'v_gla_ab_w2': out['v_gla_ab_w2'], 'v_gla_ab_b': out['v_gla_ab_b'], 'v_gla_norm': out['v_gla_norm'], 'v_w_branch_mla': out['v_w_branch_mla'], 'v_w_branch_pool': out['v_w_branch_pool'], 'v_w_branch_gla': out['v_w_branch_gla'], 'v_w_out': out['v_w_out']}


def _loss(weights, diff, rest, loss_target):
    with _jax.named_scope("forward"):
        args = {**rest, TWIN_DIFF_INPUT: diff, **{k: w.astype(_WEIGHT_DTYPES[k]) for k, w in weights.items()}}
        y = _forward(args)
    with _jax.named_scope("loss_head"):
        err = _jnp.square(y.astype(_jnp.float32) - loss_target)
        return 0.5 * _jnp.sum(_jnp.mean(err, axis=-1)) if err.ndim else 0.5 * err


def _adamw(w, g, m, v):
    m = ADAM_B1 * m + (1.0 - ADAM_B1) * g
    v = ADAM_B2 * v + (1.0 - ADAM_B2) * _jnp.square(g)
    m_hat = m / (1.0 - ADAM_B1 ** ADAM_STEP)
    v_hat = v / (1.0 - ADAM_B2 ** ADAM_STEP)
    delta = -ADAM_LR * (m_hat / (_jnp.sqrt(v_hat) + ADAM_EPS) + ADAM_WD * w)
    return delta, m, v


def reference(x, c, ctx, c_ctx, mod_w, mod_b, pre_norm, post_norm, w_in, mla_q_norm, mla_w_uq, mla_kv_norm, mla_w_ukv, pool_w, pool_scale, gla_af_w2, gla_af_b, gla_ab_w2, gla_ab_b, gla_norm, w_branch_mla, w_branch_pool, w_branch_gla, w_out, loss_target, m_c_ctx, m_mod_w, m_mod_b, m_pre_norm, m_post_norm, m_w_in, m_mla_q_norm, m_mla_w_uq, m_mla_kv_norm, m_mla_w_ukv, m_pool_w, m_pool_scale, m_gla_af_w2, m_gla_af_b, m_gla_ab_w2, m_gla_ab_b, m_gla_norm, m_w_branch_mla, m_w_branch_pool, m_w_branch_gla, m_w_out, v_c_ctx, v_mod_w, v_mod_b, v_pre_norm, v_post_norm, v_w_in, v_mla_q_norm, v_mla_w_uq, v_mla_kv_norm, v_mla_w_ukv, v_pool_w, v_pool_scale, v_gla_af_w2, v_gla_af_b, v_gla_ab_w2, v_gla_ab_b, v_gla_norm, v_w_branch_mla, v_w_branch_pool, v_w_branch_gla, v_w_out):
    given = dict(x=x, c=c, ctx=ctx, c_ctx=c_ctx, mod_w=mod_w, mod_b=mod_b, pre_norm=pre_norm, post_norm=post_norm, w_in=w_in, mla_q_norm=mla_q_norm, mla_w_uq=mla_w_uq, mla_kv_norm=mla_kv_norm, mla_w_ukv=mla_w_ukv, pool_w=pool_w, pool_scale=pool_scale, gla_af_w2=gla_af_w2, gla_af_b=gla_af_b, gla_ab_w2=gla_ab_w2, gla_ab_b=gla_ab_b, gla_norm=gla_norm, w_branch_mla=w_branch_mla, w_branch_pool=w_branch_pool, w_branch_gla=w_branch_gla, w_out=w_out, loss_target=loss_target, m_c_ctx=m_c_ctx, m_mod_w=m_mod_w, m_mod_b=m_mod_b, m_pre_norm=m_pre_norm, m_post_norm=m_post_norm, m_w_in=m_w_in, m_mla_q_norm=m_mla_q_norm, m_mla_w_uq=m_mla_w_uq, m_mla_kv_norm=m_mla_kv_norm, m_mla_w_ukv=m_mla_w_ukv, m_pool_w=m_pool_w, m_pool_scale=m_pool_scale, m_gla_af_w2=m_gla_af_w2, m_gla_af_b=m_gla_af_b, m_gla_ab_w2=m_gla_ab_w2, m_gla_ab_b=m_gla_ab_b, m_gla_norm=m_gla_norm, m_w_branch_mla=m_w_branch_mla, m_w_branch_pool=m_w_branch_pool, m_w_branch_gla=m_w_branch_gla, m_w_out=m_w_out, v_c_ctx=v_c_ctx, v_mod_w=v_mod_w, v_mod_b=v_mod_b, v_pre_norm=v_pre_norm, v_post_norm=v_post_norm, v_w_in=v_w_in, v_mla_q_norm=v_mla_q_norm, v_mla_w_uq=v_mla_w_uq, v_mla_kv_norm=v_mla_kv_norm, v_mla_w_ukv=v_mla_w_ukv, v_pool_w=v_pool_w, v_pool_scale=v_pool_scale, v_gla_af_w2=v_gla_af_w2, v_gla_af_b=v_gla_af_b, v_gla_ab_w2=v_gla_ab_w2, v_gla_ab_b=v_gla_ab_b, v_gla_norm=v_gla_norm, v_w_branch_mla=v_w_branch_mla, v_w_branch_pool=v_w_branch_pool, v_w_branch_gla=v_w_branch_gla, v_w_out=v_w_out)
    weights = {n: given[n] for n in TWIN_WEIGHTS}
    shared = {n: given[n] for n in SHARED_INPUTS}
    per_example = {n: given[n] for n in ['x', 'c', 'ctx']}
    grad_fn = _jax.value_and_grad(_loss, argnums=(0, 1))

    def one_microbatch(ex, loss_target):
        ex = dict(ex)
        diff = ex.pop(TWIN_DIFF_INPUT)
        return grad_fn(weights, diff, {**shared, **ex}, loss_target)

    if N_MICROBATCH == 1:
        loss, (grad_w, grad_x) = one_microbatch(per_example, given["loss_target"])
    else:
        def body(carry, xs):
            loss_sum, grad_sum = carry
            l_k, (gw_k, gx_k) = one_microbatch(xs[0], xs[1])
            with _jax.named_scope("update"):
                return (loss_sum + l_k, _jax.tree.map(_jnp.add, grad_sum, gw_k)), gx_k

        init = (_jnp.zeros((), _jnp.float32), _jax.tree.map(_jnp.zeros_like, weights))
        (loss, grad_w), grad_x = _jax.lax.scan(body, init, (per_example, given["loss_target"]))
    with _jax.named_scope("update"):
        delta_w, new_m, new_v = {}, {}, {}
        for n in TWIN_WEIGHTS:
            delta_w[n], new_m[n], new_v[n] = _adamw(weights[n], grad_w[n], given["m_" + n], given["v_" + n])
    return (loss, grad_x, *[grad_w[n] for n in TWIN_WEIGHTS], *[delta_w[n] for n in TWIN_WEIGHTS],
            *[new_m[n] for n in TWIN_WEIGHTS], *[new_v[n] for n in TWIN_WEIGHTS])
```

```python
import functools
import math

import jax
import jax.numpy as jnp
import numpy as np
from jax import lax
from jax.experimental import pallas as pl
from jax.experimental.pallas import tpu as pltpu

F32 = jnp.float32
BF16 = jnp.bfloat16

D_MODEL = 1024
NORM_EPS = 1e-6
GRID_W = 64
MLA_HEADS, MLA_Q_RANK, MLA_KV_RANK, MLA_NOPE, MLA_ROPE, MLA_V = 8, 256, 128, 64, 32, 64
MLA_WIDTH = MLA_HEADS * MLA_V
ROPE_BASE = 10000.0
ATT_SCALE = (MLA_NOPE + MLA_ROPE) ** -0.5
POOL_WINDOWS = (2, 4, 8, 16)
POOL_WIDTH, POOL_GROUP = 512, 128
GLA_HEADS, GLA_DK, GLA_DV = 4, 64, 128
GLA_KW, GLA_WIDTH = GLA_HEADS * GLA_DK, GLA_HEADS * GLA_DV
GLA_GATE_RANK, GLA_TAU, GLA_CHUNK = 16, 16.0, 64
IN_SIZES = (256, 128, 32, 512, 512, 512, 256, 256, 512, 16, 16, 512, 3 * D_MODEL)
ADAM_LR, ADAM_B1, ADAM_B2, ADAM_EPS, ADAM_WD, ADAM_STEP = 0.001, 0.9, 0.999, 1e-08, 0.01, 10
N_DEV = 8

LANES = 128
TOKEN_BLOCK = 256
VMEM_LIMIT = 48 * 1024 * 1024
NEG_BIG = -1e30

_NT = (((1,), (1,)), ((), ()))
_TN = (((0,), (0,)), ((), ()))


def _dot(a, b):
    return jnp.dot(a, b, preferred_element_type=F32)


def _dot_nt(a, b):
    return lax.dot_general(a, b, _NT, preferred_element_type=F32)


def _dot_tn(a, b):
    return lax.dot_general(a, b, _TN, preferred_element_type=F32)


def _params(sem=None, vmem=None):
    kw = {}
    if sem is not None:
        kw["dimension_semantics"] = sem
    if vmem is not None:
        kw["vmem_limit_bytes"] = vmem
    return pltpu.CompilerParams(**kw)


def _full(shape):
    n = len(shape)
    return pl.BlockSpec(shape, lambda *_: (0,) * n)


def _sig(x):
    return 1.0 / (1.0 + jnp.exp(-x))


def _silu_and_grad(x):
    s = _sig(x)
    return x * s, s * (1.0 + x * (1.0 - s))


def _acc(ref, val, first):
    @pl.when(first)
    def _():
        ref[...] = val

    @pl.when(jnp.logical_not(first))
    def _():
        ref[...] += val


def mm_multi(a, ws, dtypes, name, tm=TOKEN_BLOCK):
    M, K = a.shape
    nw = len(ws)

    def body(a_ref, *refs):
        av = a_ref[...]
        for w_ref, o_ref in zip(refs[:nw], refs[nw:]):
            o_ref[...] = _dot(av, w_ref[...]).astype(o_ref.dtype)

    return pl.pallas_call(
        body, name=name, grid=(M // tm,),
        in_specs=[pl.BlockSpec((tm, K), lambda i: (i, 0))] + [_full(w.shape) for w in ws],
        out_specs=[pl.BlockSpec((tm, w.shape[1]), lambda i: (i, 0)) for w in ws],
        out_shape=[jax.ShapeDtypeStruct((M, w.shape[1]), dt) for w, dt in zip(ws, dtypes)],
        compiler_params=_params(("parallel",), VMEM_LIMIT),
    )(a, *ws)


def mm_dx(dzs, ws, name, tm=TOKEN_BLOCK):
    M = dzs[0].shape[0]
    K = ws[0].shape[0]
    nw = len(ws)

    def body(*refs):
        o_ref = refs[-1]
        acc = None
        for dz_ref, w_ref in zip(refs[:nw], refs[nw:2 * nw]):
            t = _dot_nt(dz_ref[...], w_ref[...])
            acc = t if acc is None else acc + t
        o_ref[...] = acc

    return pl.pallas_call(
        body, name=name, grid=(M // tm,),
        in_specs=[pl.BlockSpec((tm, dz.shape[1]), lambda i: (i, 0)) for dz in dzs] + [_full(w.shape) for w in ws],
        out_specs=pl.BlockSpec((tm, K), lambda i: (i, 0)),
        out_shape=jax.ShapeDtypeStruct((M, K), F32),
        compiler_params=_params(("parallel",), VMEM_LIMIT),
    )(*dzs, *ws)


def mm_dw(a, dz, name, tn=512, tk=512):
    M, K = a.shape
    n = dz.shape[1]
    tn = min(tn, n)
    tk = tk if M % tk == 0 else TOKEN_BLOCK

    def body(a_ref, dz_ref, o_ref):
        _acc(o_ref, _dot_tn(a_ref[...], dz_ref[...]), pl.program_id(1) == 0)

    return pl.pallas_call(
        body, name=name, grid=(n // tn, M // tk),
        in_specs=[pl.BlockSpec((tk, K), lambda j, k: (k, 0)), pl.BlockSpec((tk, tn), lambda j, k: (k, j))],
        out_specs=pl.BlockSpec((K, tn), lambda j, k: (0, j)),
        out_shape=jax.ShapeDtypeStruct((K, n), F32),
        compiler_params=_params(("parallel", "arbitrary"), VMEM_LIMIT),
    )(a, dz)


def mod_fwd(a8, w, b, name):
    tn = D_MODEL

    def body(a_ref, w_ref, b_ref, o_ref):
        a = a_ref[...]
        o_ref[...] = _dot((a * _sig(a)).astype(BF16), w_ref[...]) + b_ref[...]

    return pl.pallas_call(
        body, name=name, grid=(3,),
        in_specs=[_full(a8.shape), pl.BlockSpec((D_MODEL, tn), lambda j: (0, j)), pl.BlockSpec((1, tn), lambda j: (0, j))],
        out_specs=pl.BlockSpec((8, tn), lambda j: (0, j)),
        out_shape=jax.ShapeDtypeStruct((8, 3 * D_MODEL), F32),
        compiler_params=_params(("parallel",)),
    )(a8, w, b)


def mod_bwd(a8, w, dz8, name):
    tn = D_MODEL

    def body(a_ref, w_ref, dz_ref, dw_ref, db_ref, da_ref):
        a = a_ref[...]
        sa, dsa = _silu_and_grad(a)
        dz = dz_ref[...]
        dw_ref[...] = _dot_tn(sa.astype(BF16), dz.astype(BF16))
        db_ref[...] = jnp.sum(dz, axis=0, keepdims=True)
        _acc(da_ref, _dot_nt(dz.astype(BF16), w_ref[...]) * dsa, pl.program_id(0) == 0)

    return pl.pallas_call(
        body, name=name, grid=(3,),
        in_specs=[_full(a8.shape), pl.BlockSpec((D_MODEL, tn), lambda j: (0, j)), pl.BlockSpec((8, tn), lambda j: (0, j))],
        out_specs=[pl.BlockSpec((D_MODEL, tn), lambda j: (0, j)), pl.BlockSpec((1, tn), lambda j: (0, j)), _full((8, D_MODEL))],
        out_shape=[jax.ShapeDtypeStruct((D_MODEL, 3 * D_MODEL), F32), jax.ShapeDtypeStruct((1, 3 * D_MODEL), F32),
                   jax.ShapeDtypeStruct((8, D_MODEL), F32)],
        compiler_params=_params(("arbitrary",)),
    )(a8, w, dz8)


def _mod_row(nb):
    return lambda i: 2 * (i // nb) + jnp.minimum(i % nb, 1)


def _mod_spec(nb, part):
    row = _mod_row(nb)
    return pl.BlockSpec((1, 1, D_MODEL), lambda i: (row(i), 0, part))


def norm_mod_fwd(x2, g, ms, nb, name):
    T = x2.shape[0]

    def body(x_ref, g_ref, sh_ref, sc_ref, h_ref):
        x = x_ref[...]
        r = lax.rsqrt(jnp.mean(x * x, axis=-1, keepdims=True) + NORM_EPS)
        h_ref[...] = ((x * r) * g_ref[...] * (1.0 + sc_ref[0]) + sh_ref[0]).astype(BF16)

    return pl.pallas_call(
        body, name=name, grid=(T // TOKEN_BLOCK,),
        in_specs=[pl.BlockSpec((TOKEN_BLOCK, D_MODEL), lambda i: (i, 0)), _full((1, D_MODEL)), _mod_spec(nb, 0), _mod_spec(nb, 1)],
        out_specs=pl.BlockSpec((TOKEN_BLOCK, D_MODEL), lambda i: (i, 0)),
        out_shape=jax.ShapeDtypeStruct((T, D_MODEL), BF16),
        compiler_params=_params(("parallel",)),
    )(x2, g, ms, ms)


def norm_mod_bwd(dh, x2, g, ms, dxres, nb, name):
    T = x2.shape[0]
    nrow = ms.shape[0]
    row = _mod_row(nb)

    def body(dh_ref, x_ref, g_ref, sc_ref, dxr_ref, dx_ref, dsh_ref, dsc_ref, dg_ref):
        i = pl.program_id(0)
        t = i % nb
        x = x_ref[...]
        dh = dh_ref[...]
        g = g_ref[...]
        r = lax.rsqrt(jnp.mean(x * x, axis=-1, keepdims=True) + NORM_EPS)
        xn = x * r
        du = dh * (1.0 + sc_ref[0])
        dyg = du * g
        dx_ref[...] = dxr_ref[...] + r * (dyg - xn * jnp.mean(dyg * xn, axis=-1, keepdims=True))
        first = t <= 1
        _acc(dsh_ref.at[0], jnp.sum(dh, axis=0, keepdims=True), first)
        _acc(dsc_ref.at[0], jnp.sum(dh * xn * g, axis=0, keepdims=True), first)
        _acc(dg_ref, jnp.sum(du * xn, axis=0, keepdims=True), i == 0)

    tok = pl.BlockSpec((TOKEN_BLOCK, D_MODEL), lambda i: (i, 0))
    acc = pl.BlockSpec((1, 1, D_MODEL), lambda i: (row(i), 0, 0))
    return pl.pallas_call(
        body, name=name, grid=(T // TOKEN_BLOCK,),
        in_specs=[tok, tok, _full((1, D_MODEL)), _mod_spec(nb, 1), tok],
        out_specs=[tok, acc, acc, _full((1, D_MODEL))],
        out_shape=[jax.ShapeDtypeStruct((T, D_MODEL), F32), jax.ShapeDtypeStruct((nrow, 1, D_MODEL), F32),
                   jax.ShapeDtypeStruct((nrow, 1, D_MODEL), F32), jax.ShapeDtypeStruct((1, D_MODEL), F32)],
        compiler_params=_params(("arbitrary",)),
    )(dh, x2, g, ms, dxres)


def _rot(x):
    lane = lax.broadcasted_iota(jnp.int32, x.shape, 1)
    return jnp.where((lane % 16) < 8, -pltpu.roll(x, LANES - 8, 1), pltpu.roll(x, 8, 1))


def _rope(x, cos, sin):
    return x * cos + _rot(x) * sin


def _rope_t(dy, cos, sin):
    return dy * cos - _rot(dy * sin)


def _rms_rows(x):
    r = lax.rsqrt(jnp.mean(x * x, axis=-1, keepdims=True) + NORM_EPS)
    return x * r, r


def _rms_rows_bwd(dyg, xn, r):
    return r * (dyg - xn * jnp.mean(dyg * xn, axis=-1, keepdims=True))


def mla_prep_fwd(za, qg, kvg, wqn, wqr, wkv, cos, sin, nb, name):
    T = za.shape[0]
    W = MLA_HEADS * LANES

    def body(z_ref, qg_ref, kvg_ref, wqn_ref, wqr_ref, wkv_ref, cos_ref, sin_ref, qn_ref, qr_ref, kv_ref, kr_ref):
        z = z_ref[...]
        cos = cos_ref[...]
        sin = sin_ref[...]
        xq, _ = _rms_rows(z[:, 0:256])
        qn = (xq * qg_ref[...]).astype(BF16)
        qn_ref[...] = (_dot(qn, wqn_ref[...]) * ATT_SCALE).astype(BF16)
        qr = _dot(qn, wqr_ref[...])
        for h in range(MLA_HEADS):
            sl = slice(LANES * h, LANES * (h + 1))
            qr_ref[:, sl] = (_rope(qr[:, sl], cos, sin) * ATT_SCALE).astype(BF16)
        xkv, _ = _rms_rows(z[:, 256:384])
        kv_ref[...] = _dot((xkv * kvg_ref[...]).astype(BF16), wkv_ref[...]).astype(BF16)
        kr_ref[...] = _rope(z[:, 384:512], cos, sin).astype(BF16)

    tok = lambda n: pl.BlockSpec((TOKEN_BLOCK, n), lambda i: (i, 0))
    pos = pl.BlockSpec((TOKEN_BLOCK, LANES), lambda i: (i % nb, 0))
    return pl.pallas_call(
        body, name=name, grid=(T // TOKEN_BLOCK,),
        in_specs=[tok(512), _full(qg.shape), _full(kvg.shape), _full(wqn.shape), _full(wqr.shape), _full(wkv.shape), pos, pos],
        out_specs=[tok(W), tok(W), tok(W), tok(LANES)],
        out_shape=[jax.ShapeDtypeStruct((T, W), BF16)] * 3 + [jax.ShapeDtypeStruct((T, LANES), BF16)],
        compiler_params=_params(("parallel",)),
    )(za, qg, kvg, wqn, wqr, wkv, cos, sin)


def mla_prep_bwd(dqn, dqr, dkv, dkr, za, qg, kvg, wqn, wqr, wkv, cos, sin, nb, name):
    T = za.shape[0]
    W = MLA_HEADS * LANES

    def body(dqn_ref, dqr_ref, dkv_ref, dkr_ref, z_ref, qg_ref, kvg_ref, wqn_ref, wqr_ref, wkv_ref, cos_ref, sin_ref,
             dz_ref, dwqn_ref, dwqr_ref, dwkv_ref, dqg_ref, dkvg_ref):
        first = pl.program_id(0) == 0
        z = z_ref[...]
        cos = cos_ref[...]
        sin = sin_ref[...]
        qg = qg_ref[...]
        kvg = kvg_ref[...]
        xq, rq = _rms_rows(z[:, 0:256])
        qn = (xq * qg).astype(BF16)
        a1 = (dqn_ref[...].astype(F32) * ATT_SCALE).astype(BF16)
        parts = []
        for h in range(MLA_HEADS):
            sl = slice(LANES * h, LANES * (h + 1))
            parts.append(_rope_t(dqr_ref[:, sl].astype(F32) * ATT_SCALE, cos, sin).astype(BF16))
        a2 = jnp.concatenate(parts, axis=1)
        dq = _dot_nt(a1, wqn_ref[...]) + _dot_nt(a2, wqr_ref[...])
        _acc(dwqn_ref, _dot_tn(qn, a1), first)
        _acc(dwqr_ref, _dot_tn(qn, a2), first)
        _acc(dqg_ref, jnp.sum(dq * xq, axis=0, keepdims=True), first)
        dz_ref[:, 0:256] = _rms_rows_bwd(dq * qg, xq, rq).astype(BF16)
        xkv, rkv = _rms_rows(z[:, 256:384])
        kvn = (xkv * kvg).astype(BF16)
        dkvb = dkv_ref[...].astype(BF16)
        dk = _dot_nt(dkvb, wkv_ref[...])
        _acc(dwkv_ref, _dot_tn(kvn, dkvb), first)
        _acc(dkvg_ref, jnp.sum(dk * xkv, axis=0, keepdims=True), first)
        dz_ref[:, 256:384] = _rms_rows_bwd(dk * kvg, xkv, rkv).astype(BF16)
        dz_ref[:, 384:512] = _rope_t(dkr_ref[...], cos, sin).astype(BF16)

    tok = lambda n: pl.BlockSpec((TOKEN_BLOCK, n), lambda i: (i, 0))
    pos = pl.BlockSpec((TOKEN_BLOCK, LANES), lambda i: (i % nb, 0))
    return pl.pallas_call(
        body, name=name, grid=(T // TOKEN_BLOCK,),
        in_specs=[tok(W), tok(W), tok(W), tok(LANES), tok(512), _full(qg.shape), _full(kvg.shape), _full(wqn.shape),
                  _full(wqr.shape), _full(wkv.shape), pos, pos],
        out_specs=[tok(512), _full(wqn.shape), _full(wqr.shape), _full(wkv.shape), _full(qg.shape), _full(kvg.shape)],
        out_shape=[jax.ShapeDtypeStruct((T, 512), BF16), jax.ShapeDtypeStruct(wqn.shape, F32), jax.ShapeDtypeStruct(wqr.shape, F32),
                   jax.ShapeDtypeStruct(wkv.shape, F32), jax.ShapeDtypeStruct(qg.shape, F32), jax.ShapeDtypeStruct(kvg.shape, F32)],
        compiler_params=_params(("arbitrary",)),
    )(dqn, dqr, dkv, dkr, za, qg, kvg, wqn, wqr, wkv, cos, sin)


def _att_scores(qn_ref, qr_ref, kv_ref, kr, j, qi, n_ctx):
    sl = slice(LANES * j, LANES * (j + 1))
    q = jnp.concatenate([qn_ref[0, :, sl], qr_ref[0, :, sl]], axis=1)
    kvj = kv_ref[0, :, sl]
    k = jnp.concatenate([kvj, kr], axis=1)
    s = _dot_nt(q, k)
    kidx = lax.broadcasted_iota(jnp.int32, s.shape, 1)
    s = jnp.where(jnp.logical_or(qi > 0, kidx < n_ctx), s, NEG_BIG)
    p = jnp.exp(s - jnp.max(s, axis=-1, keepdims=True))
    return q, k, kvj, p, jnp.sum(p, axis=-1, keepdims=True)


def attention_fwd(qn, qr, kv, kr, zg, n_ctx, name):
    B, L, _ = qn.shape
    TQ = TOKEN_BLOCK
    W2 = 2 * LANES

    def body(qn_ref, qr_ref, kv_ref, kr_ref, g_ref, ya_ref, ym_ref):
        qi = pl.program_id(2)
        kr_v = kr_ref[0]
        outs = []
        for j in range(2):
            _, _, kvj, p, l = _att_scores(qn_ref, qr_ref, kv_ref, kr_v, j, qi, n_ctx)
            outs.append(_dot(p.astype(BF16), kvj) / l)
        lane = lax.broadcasted_iota(jnp.int32, outs[0].shape, 1)
        y = jnp.where(lane < MLA_V, pltpu.roll(outs[0], MLA_V, 1), outs[1])
        ya_ref[0] = y
        g = g_ref[0]
        ym_ref[0] = (y * g * _sig(g)).astype(BF16)

    qspec = pl.BlockSpec((1, TQ, W2), lambda b, h, i: (b, i, h))
    return pl.pallas_call(
        body, name=name, grid=(B, MLA_HEADS // 2, L // TQ),
        in_specs=[qspec, qspec, pl.BlockSpec((1, L, W2), lambda b, h, i: (b, 0, h)), pl.BlockSpec((1, L, LANES), lambda b, h, i: (b, 0, 0)),
                  pl.BlockSpec((1, TQ, LANES), lambda b, h, i: (b, i, h))],
        out_specs=[pl.BlockSpec((1, TQ, LANES), lambda b, h, i: (b, i, h))] * 2,
        out_shape=[jax.ShapeDtypeStruct((B, L, MLA_WIDTH), F32), jax.ShapeDtypeStruct((B, L, MLA_WIDTH), BF16)],
        compiler_params=_params(("parallel", "parallel", "parallel"), VMEM_LIMIT),
    )(qn, qr, kv, kr, zg)


def attention_bwd(qn, qr, kv, kr, zg, ya, dym, n_ctx, name):
    B, L, _ = qn.shape
    TQ = TOKEN_BLOCK
    W2 = 2 * LANES

    def body(qn_ref, qr_ref, kv_ref, kr_ref, g_ref, ya_ref, dy_ref, dqn_ref, dqr_ref, dkv_ref, dkr_ref, dzg_ref):
        hp = pl.program_id(1)
        qi = pl.program_id(2)
        g = g_ref[0]
        silu, dsilu = _silu_and_grad(g)
        dy = dy_ref[0]
        ya_v = ya_ref[0]
        dya = dy * silu
        dzg_ref[0] = (dy * ya_v * dsilu).astype(BF16)
        lane = lax.broadcasted_iota(jnp.int32, dya.shape, 1)
        hi = lane >= MLA_V
        d_out = [jnp.where(hi, pltpu.roll(dya, MLA_V, 1), 0.0), jnp.where(hi, dya, 0.0)]
        prod = dya * ya_v
        drow = [jnp.sum(jnp.where(hi, 0.0, prod), axis=-1, keepdims=True), jnp.sum(jnp.where(hi, prod, 0.0), axis=-1, keepdims=True)]

        @pl.when(qi == 0)
        def _():
            dkv_ref[...] = jnp.zeros_like(dkv_ref)

        @pl.when(jnp.logical_and(qi == 0, hp == 0))
        def _():
            dkr_ref[...] = jnp.zeros_like(dkr_ref)

        kr_v = kr_ref[0]
        for j in range(2):
            sl = slice(LANES * j, LANES * (j + 1))
            q, k, kvj, p, l = _att_scores(qn_ref, qr_ref, kv_ref, kr_v, j, qi, n_ctx)
            pn = p / l
            dob = d_out[j].astype(BF16)
            ds = (pn * (_dot_nt(dob, kvj) - drow[j])).astype(BF16)
            dq = _dot(ds, k)
            dqn_ref[0, :, sl] = jnp.where(hi, 0.0, dq[:, :LANES]).astype(BF16)
            dqr_ref[0, :, sl] = dq[:, LANES:].astype(BF16)
            dk = _dot_tn(ds, q)
            dkv_ref[0, :, sl] += dk[:, :LANES] + _dot_tn(pn.astype(BF16), dob)
            dkr_ref[0] += dk[:, LANES:]

    qspec = pl.BlockSpec((1, TQ, W2), lambda b, h, i: (b, i, h))
    kvspec = pl.BlockSpec((1, L, W2), lambda b, h, i: (b, 0, h))
    krspec = pl.BlockSpec((1, L, LANES), lambda b, h, i: (b, 0, 0))
    gspec = pl.BlockSpec((1, TQ, LANES), lambda b, h, i: (b, i, h))
    W = MLA_HEADS * LANES
    return pl.pallas_call(
        body, name=name, grid=(B, MLA_HEADS // 2, L // TQ),
        in_specs=[qspec, qspec, kvspec, krspec, gspec, gspec, gspec],
        out_specs=[qspec, qspec, kvspec, krspec, gspec],
        out_shape=[jax.ShapeDtypeStruct((B, L, W), BF16), jax.ShapeDtypeStruct((B, L, W), BF16), jax.ShapeDtypeStruct((B, L, W), F32),
                   jax.ShapeDtypeStruct((B, L, LANES), F32), jax.ShapeDtypeStruct((B, L, MLA_WIDTH), BF16)],
        compiler_params=_params(("parallel", "arbitrary", "arbitrary"), VMEM_LIMIT),
    )(qn, qr, kv, kr, zg, ya, dym)


def _seg_bounds(rows, n_ctx, L):
    in_ctx = rows < n_ctx
    return jnp.where(in_ctx, 0, n_ctx), jnp.where(in_ctx, n_ctx, L)


def _window_sum(u, w, rows, lo, hi, mirror):
    L = u.shape[0]
    offs = range(-w // 2 + 1, w // 2 + 1) if mirror else range(-w // 2, w // 2)
    acc = None
    for d in offs:
        if d == 0:
            t = u
        else:
            src = rows + d
            t = jnp.where(jnp.logical_and(src >= lo, src < hi), pltpu.roll(u, (-d) % L, 0), 0.0)
        acc = t if acc is None else acc + t
    return acc


def _window_count(w, rows, lo, hi):
    pos = rows - lo
    return (jnp.minimum(pos + w // 2, hi - lo) - jnp.maximum(pos - w // 2, 0)).astype(F32)


def pool_fwd(px, pg, pw, ps, n_ctx, name):
    B, L, _ = px.shape

    def body(px_ref, pg_ref, pw_ref, ps_ref, y_ref):
        rows = lax.broadcasted_iota(jnp.int32, (L, POOL_GROUP), 0)
        lo, hi = _seg_bounds(rows, n_ctx, L)
        for gi, w in enumerate(POOL_WINDOWS):
            sl = slice(POOL_GROUP * gi, POOL_GROUP * (gi + 1))
            u = px_ref[0, :, sl]
            pooled = _window_sum(u, w, rows, lo, hi, False) / _window_count(w, rows, lo, hi) - u
            mixed = _dot(pooled.astype(BF16), pw_ref[gi])
            g = pg_ref[0, :, sl]
            y_ref[0, :, sl] = (mixed * ps_ref[:, sl] * (g * _sig(g))).astype(BF16)

    tok = pl.BlockSpec((1, L, POOL_WIDTH), lambda b: (b, 0, 0))
    return pl.pallas_call(
        body, name=name, grid=(B,),
        in_specs=[tok, tok, _full(pw.shape), _full(ps.shape)],
        out_specs=tok, out_shape=jax.ShapeDtypeStruct((B, L, POOL_WIDTH), BF16),
        compiler_params=_params(("parallel",), VMEM_LIMIT),
    )(px, pg, pw, ps)


def pool_bwd(px, pg, pw, ps, dy, n_ctx, name):
    B, L, _ = px.shape

    def body(px_ref, pg_ref, pw_ref, ps_ref, dy_ref, dpx_ref, dpg_ref, dpw_ref, dps_ref):
        first = pl.program_id(0) == 0
        rows = lax.broadcasted_iota(jnp.int32, (L, POOL_GROUP), 0)
        lo, hi = _seg_bounds(rows, n_ctx, L)
        for gi, w in enumerate(POOL_WINDOWS):
            sl = slice(POOL_GROUP * gi, POOL_GROUP * (gi + 1))
            u = px_ref[0, :, sl]
            cnt = _window_count(w, rows, lo, hi)
            pooled = (_window_sum(u, w, rows, lo, hi, False) / cnt - u).astype(BF16)
            mixed = _dot(pooled, pw_ref[gi])
            silu, dsilu = _silu_and_grad(pg_ref[0, :, sl])
            sc = ps_ref[:, sl]
            dyv = dy_ref[0, :, sl]
            _acc(dps_ref.at[:, sl], jnp.sum(dyv * mixed * silu, axis=0, keepdims=True), first)
            dpg_ref[0, :, sl] = (dyv * mixed * sc * dsilu).astype(BF16)
            dmixed = (dyv * sc * silu).astype(BF16)
            _acc(dpw_ref.at[gi], _dot_tn(pooled, dmixed), first)
            dpooled = _dot_nt(dmixed, pw_ref[gi])
            dpx_ref[0, :, sl] = (_window_sum(dpooled / cnt, w, rows, lo, hi, True) - dpooled).astype(BF16)

    tok = pl.BlockSpec((1, L, POOL_WIDTH), lambda b: (b, 0, 0))
    return pl.pallas_call(
        body, name=name, grid=(B,),
        in_specs=[tok, tok, _full(pw.shape), _full(ps.shape), tok],
        out_specs=[tok, tok, _full(pw.shape), _full(ps.shape)],
        out_shape=[jax.ShapeDtypeStruct((B, L, POOL_WIDTH), BF16)] * 2 + [jax.ShapeDtypeStruct(pw.shape, F32), jax.ShapeDtypeStruct(ps.shape, F32)],
        compiler_params=_params(("arbitrary",), VMEM_LIMIT),
    )(px, pg, pw, ps, dy)


_SCAN_STEPS = (1, 2, 4, 8, 16, 32)


def _cum_fwd(x, r):
    for s in _SCAN_STEPS:
        x = x + jnp.where(r >= s, pltpu.roll(x, s, 0), 0.0)
    return x


def _cum_bwd(x, r):
    n = x.shape[0]
    for s in _SCAN_STEPS:
        x = x + jnp.where(r + s < GLA_CHUNK, pltpu.roll(x, n - s, 0), 0.0)
    return x


def _log_sigmoid(x):
    return jnp.minimum(x, 0.0) - jnp.log(1.0 + jnp.exp(-jnp.abs(x)))


def _gla_decays(lr, w_ref, b_ref, r, reverse):
    pre = _dot(lr, w_ref[...]) + b_ref[...]
    a = _log_sigmoid(pre) / GLA_TAU
    pf = _cum_fwd(a, r)
    sf = _cum_bwd(a, r)
    tot = pf + sf - a
    return pre, a, (sf if reverse else pf), tot


def gla_prep_fwd(zlr, zq, zk, waf, wab, baf, bab, name):
    T = zlr.shape[0]

    def body(lr_ref, q_ref, k_ref, waf_ref, wab_ref, baf_ref, bab_ref, qf_ref, kf_ref, ksf_ref, tf_ref, qb_ref, kb_ref, ksb_ref, tb_ref):
        r = lax.broadcasted_iota(jnp.int32, (TOKEN_BLOCK, GLA_KW), 0) % GLA_CHUNK
        lr = lr_ref[...].astype(BF16)
        q = q_ref[...] * GLA_DK ** -0.5
        k = k_ref[...]
        for rev, w_ref, b_ref, qo, ko, kso, to in ((False, waf_ref, baf_ref, qf_ref, kf_ref, ksf_ref, tf_ref),
                                                   (True, wab_ref, bab_ref, qb_ref, kb_ref, ksb_ref, tb_ref)):
            _, _, b, tot = _gla_decays(lr, w_ref, b_ref, r, rev)
            qo[...] = (q * jnp.exp(b)).astype(BF16)
            ko[...] = (k * jnp.exp(-b)).astype(BF16)
            kso[...] = (k * jnp.exp(tot - b)).astype(BF16)
            to[...] = tot

    tok = lambda n: pl.BlockSpec((TOKEN_BLOCK, n), lambda i: (i, 0))
    outs = [jax.ShapeDtypeStruct((T, GLA_KW), BF16)] * 3 + [jax.ShapeDtypeStruct((T, GLA_KW), F32)]
    return pl.pallas_call(
        body, name=name, grid=(T // TOKEN_BLOCK,),
        in_specs=[tok(LANES), tok(GLA_KW), tok(GLA_KW), _full(waf.shape), _full(wab.shape), _full(baf.shape), _full(bab.shape)],
        out_specs=[tok(GLA_KW)] * 8, out_shape=outs + outs,
        compiler_params=_params(("parallel",)),
    )(zlr, zq, zk, waf, wab, baf, bab)


def gla_prep_bwd(zlr, zq, zk, waf, wab, baf, bab, gf, gb, name):
    T = zlr.shape[0]

    def body(lr_ref, q_ref, k_ref, waf_ref, wab_ref, baf_ref, bab_ref, dqf, dkf, dksf, ddf, dqb, dkb, dksb, ddb,
             dlr_ref, dq_ref, dk_ref, dwaf_ref, dwab_ref, dbaf_ref, dbab_ref):
        first = pl.program_id(0) == 0
        r = lax.broadcasted_iota(jnp.int32, (TOKEN_BLOCK, GLA_KW), 0) % GLA_CHUNK
        lr = lr_ref[...].astype(BF16)
        q = q_ref[...] * GLA_DK ** -0.5
        k = k_ref[...]
        dq_tot = None
        dk_tot = None
        dlr = None
        for rev, w_ref, b_ref, dqt, dkt, dks, ddec, dw_ref, db_ref in (
                (False, waf_ref, baf_ref, dqf, dkf, dksf, ddf, dwaf_ref, dbaf_ref),
                (True, wab_ref, bab_ref, dqb, dkb, dksb, ddb, dwab_ref, dbab_ref)):
            pre, _, b, tot = _gla_decays(lr, w_ref, b_ref, r, rev)
            e1 = jnp.exp(b)
            e2 = jnp.exp(-b)
            e3 = jnp.exp(tot - b)
            dqt_v = dqt[...]
            dkt_v = dkt[...]
            dks_v = dks[...]
            dq = dqt_v * e1
            dk = dkt_v * e2 + dks_v * e3
            g3 = dks_v * (k * e3)
            d_b = dqt_v * (q * e1) - dkt_v * (k * e2) - g3
            d_tot = _cum_fwd(g3, r) + _cum_bwd(g3, r) - g3 + ddec[...] * jnp.exp(tot)
            da = (_cum_fwd(d_b, r) if rev else _cum_bwd(d_b, r)) + d_tot
            dpre = (da * (_sig(-pre) / GLA_TAU)).astype(BF16)
            t = _dot_nt(dpre, w_ref[...])
            dlr = t if dlr is None else dlr + t
            _acc(dw_ref, _dot_tn(lr, dpre), first)
            _acc(db_ref, jnp.sum(dpre.astype(F32), axis=0, keepdims=True), first)
            dq_tot = dq if dq_tot is None else dq_tot + dq
            dk_tot = dk if dk_tot is None else dk_tot + dk
        dlr_ref[...] = dlr.astype(BF16)
        dq_ref[...] = (dq_tot * GLA_DK ** -0.5).astype(BF16)
        dk_ref[...] = dk_tot.astype(BF16)

    tok = lambda n: pl.BlockSpec((TOKEN_BLOCK, n), lambda i: (i, 0))
    return pl.pallas_call(
        body, name=name, grid=(T // TOKEN_BLOCK,),
        in_specs=[tok(LANES), tok(GLA_KW), tok(GLA_KW), _full(waf.shape), _full(wab.shape), _full(baf.shape), _full(bab.shape)] + [tok(GLA_KW)] * 8,
        out_specs=[tok(LANES), tok(GLA_KW), tok(GLA_KW), _full(waf.shape), _full(wab.shape), _full(baf.shape), _full(bab.shape)],
        out_shape=[jax.ShapeDtypeStruct((T, LANES), BF16), jax.ShapeDtypeStruct((T, GLA_KW), BF16), jax.ShapeDtypeStruct((T, GLA_KW), BF16),
                   jax.ShapeDtypeStruct(waf.shape, F32), jax.ShapeDtypeStruct(wab.shape, F32), jax.ShapeDtypeStruct(baf.shape, F32),
                   jax.ShapeDtypeStruct(bab.shape, F32)],
        compiler_params=_params(("arbitrary",)),
    )(zlr, zq, zk, waf, wab, baf, bab, *gf, *gb)


def _chunk_order(nc, n_ctx_chunks, reverse):
    if not reverse:
        return lambda c: c
    return lambda c: jnp.where(c < n_ctx_chunks, n_ctx_chunks - 1 - c, nc + n_ctx_chunks - 1 - c)


def _head_mask(shape, h):
    lane = lax.broadcasted_iota(jnp.int32, shape, 1)
    return jnp.logical_and(lane >= GLA_DK * h, lane < GLA_DK * (h + 1))


def _tri_mask(reverse):
    ri = lax.broadcasted_iota(jnp.int32, (GLA_CHUNK, GLA_CHUNK), 0)
    ci = lax.broadcasted_iota(jnp.int32, (GLA_CHUNK, GLA_CHUNK), 1)
    return (ri <= ci) if reverse else (ri >= ci)


def gla_scan_fwd(qt, kt, ks, v, tot, n_ctx, reverse, name):
    B, L, _ = qt.shape
    C = GLA_CHUNK
    nc = L // C
    order = _chunk_order(nc, n_ctx // C, reverse)

    def body(q_ref, k_ref, ks_ref, v_ref, tot_ref, o_ref, ss_ref, st):
        @pl.when(pl.program_id(1) == 0)
        def _():
            st[...] = jnp.zeros_like(st)

        S = st[...]
        ss_ref[0, 0] = S
        Sb = S.astype(BF16)
        q = q_ref[0]
        k = k_ref[0]
        ksv = ks_ref[0]
        mask = _tri_mask(reverse)
        U = jnp.zeros_like(S)
        for h in range(GLA_HEADS):
            qm = jnp.where(_head_mask(q.shape, h), q, jnp.zeros_like(q))
            A = jnp.where(mask, _dot_nt(qm, k), 0.0)
            vh = v_ref[0, :, GLA_DV * h:GLA_DV * (h + 1)]
            o_ref[0, :, GLA_DV * h:GLA_DV * (h + 1)] = _dot(A.astype(BF16), vh.astype(BF16)) + _dot_nt(qm, Sb)
            U = U + jnp.where(_head_mask(S.shape, h), _dot(vh.T.astype(BF16), ksv), 0.0)
        st[...] = jnp.exp(tot_ref[0, 0:1, :]) * S + U

    tok = lambda n: pl.BlockSpec((1, C, n), lambda b, c: (b, order(c), 0))
    return pl.pallas_call(
        body, name=name, grid=(B, nc),
        in_specs=[tok(GLA_KW), tok(GLA_KW), tok(GLA_KW), tok(GLA_WIDTH), tok(GLA_KW)],
        out_specs=[tok(GLA_WIDTH), pl.BlockSpec((1, 1, GLA_DV, GLA_KW), lambda b, c: (b, order(c), 0, 0))],
        out_shape=[jax.ShapeDtypeStruct((B, L, GLA_WIDTH), F32), jax.ShapeDtypeStruct((B, nc, GLA_DV, GLA_KW), F32)],
        scratch_shapes=[pltpu.VMEM((GLA_DV, GLA_KW), F32)],
        compiler_params=_params(("parallel", "arbitrary")),
    )(qt, kt, ks, v, tot)


def gla_scan_bwd(qt, kt, ks, v, tot, ss, do, n_ctx, reverse, name):
    B, L, _ = qt.shape
    C = GLA_CHUNK
    nc = L // C
    fwd_order = _chunk_order(nc, n_ctx // C, reverse)
    order = lambda c: fwd_order(nc - 1 - c)

    def body(q_ref, k_ref, ks_ref, v_ref, tot_ref, ss_ref, do_ref, dq_ref, dk_ref, dks_ref, dv_ref, dd_ref, dst):
        @pl.when(pl.program_id(1) == 0)
        def _():
            dst[...] = jnp.zeros_like(dst)

        dSn = dst[...]
        dSnb = dSn.astype(BF16)
        S = ss_ref[0, 0]
        Sb = S.astype(BF16)
        dec = jnp.exp(tot_ref[0, 0:1, :])
        q = q_ref[0]
        k = k_ref[0]
        ksv = ks_ref[0]
        mask = _tri_mask(reverse)
        dq = jnp.zeros(q.shape, F32)
        dk = jnp.zeros(q.shape, F32)
        dks = jnp.zeros(q.shape, F32)
        dS = dec * dSn
        for h in range(GLA_HEADS):
            hm = _head_mask(q.shape, h)
            sl = slice(GLA_DV * h, GLA_DV * (h + 1))
            qm = jnp.where(hm, q, jnp.zeros_like(q))
            ksm = jnp.where(hm, ksv, jnp.zeros_like(ksv))
            vh = v_ref[0, :, sl].astype(BF16)
            doh = do_ref[0, :, sl]
            dohb = doh.astype(BF16)
            A = jnp.where(mask, _dot_nt(qm, k), 0.0)
            dA = jnp.where(mask, _dot_nt(dohb, vh), 0.0)
            dq = dq + jnp.where(hm, _dot(dohb, Sb) + _dot(dA.astype(BF16), k), 0.0)
            dk = dk + jnp.where(hm, _dot(dA.T.astype(BF16), q), 0.0)
            dv_ref[0, :, sl] = _dot(A.T.astype(BF16), dohb) + _dot_nt(ksm, dSnb)
            dks = dks + jnp.where(hm, _dot(vh, dSnb), 0.0)
            dS = dS + jnp.where(_head_mask(S.shape, h), _dot(doh.T.astype(BF16), q), 0.0)
        dq_ref[0] = dq
        dk_ref[0] = dk
        dks_ref[0] = dks
        dd_ref[0] = jnp.broadcast_to(jnp.sum(dSn * S, axis=0, keepdims=True), (C, GLA_KW))
        dst[...] = dS

    tok = lambda n: pl.BlockSpec((1, C, n), lambda b, c: (b, order(c), 0))
    return pl.pallas_call(
        body, name=name, grid=(B, nc),
        in_specs=[tok(GLA_KW), tok(GLA_KW), tok(GLA_KW), tok(GLA_WIDTH), tok(GLA_KW),
                  pl.BlockSpec((1, 1, GLA_DV, GLA_KW), lambda b, c: (b, order(c), 0, 0)), tok(GLA_WIDTH)],
        out_specs=[tok(GLA_KW), tok(GLA_KW), tok(GLA_KW), tok(GLA_WIDTH), tok(GLA_KW)],
        out_shape=[jax.ShapeDtypeStruct((B, L, GLA_KW), F32)] * 3 + [jax.ShapeDtypeStruct((B, L, GLA_WIDTH), F32), jax.ShapeDtypeStruct((B, L, GLA_KW), F32)],
        scratch_shapes=[pltpu.VMEM((GLA_DV, GLA_KW), F32)],
        compiler_params=_params(("parallel", "arbitrary")),
    )(qt, kt, ks, v, tot, ss, do)


def gla_out_fwd(of, ob, gn, zg, name):
    T = of.shape[0]

    def body(of_ref, ob_ref, gn_ref, g_ref, y_ref):
        for h in range(GLA_HEADS):
            sl = slice(GLA_DV * h, GLA_DV * (h + 1))
            xn, _ = _rms_rows(of_ref[:, sl] + ob_ref[:, sl])
            g = g_ref[:, sl]
            y_ref[:, sl] = (xn * gn_ref[...] * (g * _sig(g))).astype(BF16)

    tok = pl.BlockSpec((TOKEN_BLOCK, GLA_WIDTH), lambda i: (i, 0))
    return pl.pallas_call(
        body, name=name, grid=(T // TOKEN_BLOCK,),
        in_specs=[tok, tok, _full(gn.shape), tok], out_specs=tok,
        out_shape=jax.ShapeDtypeStruct((T, GLA_WIDTH), BF16),
        compiler_params=_params(("parallel",)),
    )(of, ob, gn, zg)


def gla_out_bwd(of, ob, gn, zg, dy, name):
    T = of.shape[0]

    def body(of_ref, ob_ref, gn_ref, g_ref, dy_ref, do_ref, dzg_ref, dgn_ref):
        first = pl.program_id(0) == 0
        gn_v = gn_ref[...]
        dgn = None
        for h in range(GLA_HEADS):
            sl = slice(GLA_DV * h, GLA_DV * (h + 1))
            xn, r = _rms_rows(of_ref[:, sl] + ob_ref[:, sl])
            silu, dsilu = _silu_and_grad(g_ref[:, sl])
            dyv = dy_ref[:, sl]
            dzg_ref[:, sl] = (dyv * xn * gn_v * dsilu).astype(BF16)
            dn = dyv * silu
            t = jnp.sum(dn * xn, axis=0, keepdims=True)
            dgn = t if dgn is None else dgn + t
            do_ref[:, sl] = _rms_rows_bwd(dn * gn_v, xn, r)
        _acc(dgn_ref, dgn, first)

    tok = pl.BlockSpec((TOKEN_BLOCK, GLA_WIDTH), lambda i: (i, 0))
    return pl.pallas_call(
        body, name=name, grid=(T // TOKEN_BLOCK,),
        in_specs=[tok, tok, _full(gn.shape), tok, tok], out_specs=[tok, tok, _full(gn.shape)],
        out_shape=[jax.ShapeDtypeStruct((T, GLA_WIDTH), F32), jax.ShapeDtypeStruct((T, GLA_WIDTH), BF16), jax.ShapeDtypeStruct(gn.shape, F32)],
        compiler_params=_params(("arbitrary",)),
    )(of, ob, gn, zg, dy)


def merge_post_fwd(ys, zm, wbs, wo, x2, pg, ms, nb, name):
    T = x2.shape[0]

    def body(y0, y1, y2, zm_ref, w0, w1, w2, wo_ref, x_ref, pg_ref, gate_ref, xn_ref, out_ref, mg_ref):
        merged = None
        for i, (y_ref, w_ref) in enumerate(((y0, w0), (y1, w1), (y2, w2))):
            t = _sig(zm_ref[:, D_MODEL * i:D_MODEL * (i + 1)]) * _dot(y_ref[...], w_ref[...])
            merged = t if merged is None else merged + t
        mb = merged.astype(BF16)
        mg_ref[...] = mb
        out = _dot(mb, wo_ref[...])
        out_ref[...] = out
        on, _ = _rms_rows(out)
        xn_ref[...] = x_ref[...] + gate_ref[0] * (on * pg_ref[...])

    tok = lambda n: pl.BlockSpec((TOKEN_BLOCK, n), lambda i: (i, 0))
    return pl.pallas_call(
        body, name=name, grid=(T // TOKEN_BLOCK,),
        in_specs=[tok(512)] * 3 + [tok(3 * D_MODEL)] + [_full(w.shape) for w in wbs] + [_full(wo.shape), tok(D_MODEL), _full(pg.shape), _mod_spec(nb, 2)],
        out_specs=[tok(D_MODEL)] * 3,
        out_shape=[jax.ShapeDtypeStruct((T, D_MODEL), F32), jax.ShapeDtypeStruct((T, D_MODEL), F32), jax.ShapeDtypeStruct((T, D_MODEL), BF16)],
        compiler_params=_params(("parallel",), VMEM_LIMIT),
    )(*ys, zm, *wbs, wo, x2, pg, ms)


def merge_post_bwd(dxn, out, ys, zm, wbs, wo, pg, ms, nb, name):
    T = dxn.shape[0]
    nrow = ms.shape[0]
    row = _mod_row(nb)

    def body(dxn_ref, out_ref, y0, y1, y2, zm_ref, w0, w1, w2, wo_ref, pg_ref, gate_ref,
             dy0, dy1, dy2, dzm_ref, dout_ref, dp0, dp1, dp2, dgate_ref, dpg_ref):
        i = pl.program_id(0)
        dxn_v = dxn_ref[...]
        on, r = _rms_rows(out_ref[...])
        pg_v = pg_ref[...]
        _acc(dgate_ref.at[0], jnp.sum(dxn_v * on * pg_v, axis=0, keepdims=True), (i % nb) <= 1)
        dn = dxn_v * gate_ref[0]
        _acc(dpg_ref, jnp.sum(dn * on, axis=0, keepdims=True), i == 0)
        dout = _rms_rows_bwd(dn * pg_v, on, r).astype(BF16)
        dout_ref[...] = dout
        dmerged = _dot_nt(dout, wo_ref[...])
        for j, (y_ref, w_ref, dy_ref, dp_ref) in enumerate(((y0, w0, dy0, dp0), (y1, w1, dy1, dp1), (y2, w2, dy2, dp2))):
            sl = slice(D_MODEL * j, D_MODEL * (j + 1))
            g = _sig(zm_ref[:, sl])
            p = _dot(y_ref[...], w_ref[...])
            dzm_ref[:, sl] = (dmerged * p * g * (1.0 - g)).astype(BF16)
            dp = (dmerged * g).astype(BF16)
            dp_ref[...] = dp
            dy_ref[...] = _dot_nt(dp, w_ref[...])

    tok = lambda n: pl.BlockSpec((TOKEN_BLOCK, n), lambda i: (i, 0))
    return pl.pallas_call(
        body, name=name, grid=(T // TOKEN_BLOCK,),
        in_specs=[tok(D_MODEL), tok(D_MODEL)] + [tok(512)] * 3 + [tok(3 * D_MODEL)] + [_full(w.shape) for w in wbs] + [_full(wo.shape), _full(pg.shape), _mod_spec(nb, 2)],
        out_specs=[tok(512)] * 3 + [tok(3 * D_MODEL), tok(D_MODEL)] + [tok(D_MODEL)] * 3 + [pl.BlockSpec((1, 1, D_MODEL), lambda i: (row(i), 0, 0)), _full(pg.shape)],
        out_shape=[jax.ShapeDtypeStruct((T, 512), F32)] * 3 + [jax.ShapeDtypeStruct((T, 3 * D_MODEL), BF16), jax.ShapeDtypeStruct((T, D_MODEL), BF16)]
        + [jax.ShapeDtypeStruct((T, D_MODEL), BF16)] * 3 + [jax.ShapeDtypeStruct((nrow, 1, D_MODEL), F32), jax.ShapeDtypeStruct(pg.shape, F32)],
        compiler_params=_params(("arbitrary",), VMEM_LIMIT),
    )(dxn, out, *ys, zm, *wbs, wo, pg, ms)


def loss_head(y2, tgt2, nb, name):
    T = y2.shape[0]
    nlat = nb - 1

    def body(y_ref, t_ref, dy_ref, loss_ref, acc):
        i = pl.program_id(0)
        is_lat = (i % nb) > 0

        @pl.when(i == 0)
        def _():
            acc[...] = jnp.zeros_like(acc)

        @pl.when(is_lat)
        def _():
            e = y_ref[...] - t_ref[...]
            dy_ref[...] = e * (1.0 / D_MODEL)
            acc[...] += jnp.sum(e * e, axis=0, keepdims=True)

        @pl.when(jnp.logical_not(is_lat))
        def _():
            dy_ref[...] = jnp.zeros_like(dy_ref)

        @pl.when(i == pl.num_programs(0) - 1)
        def _():
            loss_ref[...] = jnp.sum(acc[...], axis=1, keepdims=True) * (0.5 / D_MODEL)

    tok = pl.BlockSpec((TOKEN_BLOCK, D_MODEL), lambda i: (i, 0))
    tgt = pl.BlockSpec((TOKEN_BLOCK, D_MODEL), lambda i: ((i // nb) * nlat + jnp.maximum(i % nb - 1, 0), 0))
    return pl.pallas_call(
        body, name=name, grid=(T // TOKEN_BLOCK,),
        in_specs=[tok, tgt], out_specs=[tok, _full((1, 1))],
        out_shape=[jax.ShapeDtypeStruct((T, D_MODEL), F32), jax.ShapeDtypeStruct((1, 1), F32)],
        scratch_shapes=[pltpu.VMEM((1, D_MODEL), F32)],
        compiler_params=_params(("arbitrary",)),
    )(y2, tgt2)


_IN_OFFS = tuple(int(o) for o in np.cumsum((0,) + IN_SIZES))
_IN_GROUPS = (("a", 0, 416, 512), ("mg", 416, 512, 512), ("px", 928, 512, 512), ("pg", 1440, 512, 512), ("gq", 1952, 256, 256),
              ("gk", 2208, 256, 256), ("gv", 2464, 512, 512), ("lr", 2976, 32, 128), ("gg", 3008, 512, 512), ("m", 3520, 3072, 3072))


def _pad_cols(w, n):
    return w if w.shape[1] == n else jnp.pad(w, ((0, 0), (0, n - w.shape[1])))


def layer_weights(w_in, w_uq, w_ukv, af_w2, ab_w2, wbm, wbp, wbg, w_out):
    W = {}
    for nm, off, n, npad in _IN_GROUPS:
        W["in_" + nm] = _pad_cols(w_in[:, off:off + n], npad)
    uq = w_uq.reshape(MLA_Q_RANK, MLA_HEADS, MLA_NOPE + MLA_ROPE)
    W["qn"] = jnp.pad(uq[:, :, :MLA_NOPE], ((0, 0), (0, 0), (0, LANES - MLA_NOPE))).reshape(MLA_Q_RANK, MLA_HEADS * LANES)
    W["qr"] = jnp.pad(uq[:, :, MLA_NOPE:], ((0, 0), (0, 0), (0, LANES - MLA_ROPE))).reshape(MLA_Q_RANK, MLA_HEADS * LANES)
    W["kv"] = w_ukv
    W["af"] = jnp.pad(af_w2, ((0, LANES - GLA_GATE_RANK), (0, 0)))
    W["ab"] = jnp.pad(ab_w2, ((GLA_GATE_RANK, LANES - 2 * GLA_GATE_RANK), (0, 0)))
    W["bm"], W["bp"], W["bg"], W["out"] = wbm, wbp, wbg, w_out
    return W


def rope_tables(L, n_ctx):
    t = np.arange(L - n_ctx)
    half = MLA_ROPE // 2
    inv = ROPE_BASE ** (-np.arange(0, half, 2, dtype=np.float32) / half)
    ang_r = (t // GRID_W).astype(np.float32)[:, None] * inv
    ang_c = (t % GRID_W).astype(np.float32)[:, None] * inv
    ang = jnp.asarray(np.concatenate([ang_r, ang_r, ang_c, ang_c], axis=-1), F32)
    cos = jnp.ones((L, LANES), F32).at[n_ctx:, :MLA_ROPE].set(jnp.cos(ang))
    sin = jnp.zeros((L, LANES), F32).at[n_ctx:, :MLA_ROPE].set(jnp.sin(ang))
    return cos, sin


def layer_fwd(x2, ms, W, P, cos, sin, B, L, n_ctx, tag):
    nb = L // TOKEN_BLOCK
    r3 = lambda a: a.reshape(B, L, a.shape[-1])
    r2 = lambda a: a.reshape(B * L, a.shape[-1])
    h = norm_mod_fwd(x2, P["pre"], ms, nb, tag + "norm_mod")
    names = [g[0] for g in _IN_GROUPS[:-1]]
    z = dict(zip(names, mm_multi(h, [W["in_" + n] for n in names], [F32] * len(names), tag + "in_proj")))
    (z["m"],) = mm_multi(h, [W["in_m"]], [F32], tag + "in_proj_merge")
    qn, qr, kv, kr = mla_prep_fwd(z["a"], P["qg"], P["kvg"], W["qn"], W["qr"], W["kv"], cos, sin, nb, tag + "mla_prep")
    ya, y_mla = attention_fwd(r3(qn), r3(qr), r3(kv), r3(kr), r3(z["mg"]), n_ctx, tag + "attention")
    y_pool = pool_fwd(r3(z["px"]), r3(z["pg"]), P["pw"], P["ps"], n_ctx, tag + "pool")
    qf, kf, ksf, tf, qb, kb, ksb, tb = gla_prep_fwd(z["lr"], z["gq"], z["gk"], W["af"], W["ab"], P["baf"], P["bab"], tag + "gla_prep")
    of, ssf = gla_scan_fwd(r3(qf), r3(kf), r3(ksf), r3(z["gv"]), r3(tf), n_ctx, False, tag + "gla_scan_f")
    ob, ssb = gla_scan_fwd(r3(qb), r3(kb), r3(ksb), r3(z["gv"]), r3(tb), n_ctx, True, tag + "gla_scan_b")
    y_gla = gla_out_fwd(r2(of), r2(ob), P["gn"], z["gg"], tag + "gla_out")
    ys = [r2(y_mla), r2(y_pool), y_gla]
    x_new, out, merged = merge_post_fwd(ys, z["m"], [W["bm"], W["bp"], W["bg"]], W["out"], x2, P["post"], ms, nb, tag + "merge_post")
    res = dict(x2=x2, h=h, z=z, qn=qn, qr=qr, kv=kv, kr=kr, ya=ya, ys=ys, gla_f=(qf, kf, ksf, tf, ssf), gla_b=(qb, kb, ksb, tb, ssb),
               of=of, ob=ob, out=out, merged=merged)
    return x_new, res


def layer_bwd(dxn, res, ms, W, P, cos, sin, B, L, n_ctx, tag):
    nb = L // TOKEN_BLOCK
    r3 = lambda a: a.reshape(B, L, a.shape[-1])
    r2 = lambda a: a.reshape(B * L, a.shape[-1])
    z = res["z"]
    ys = res["ys"]
    wbs = [W["bm"], W["bp"], W["bg"]]
    dy0, dy1, dy2, dzm, dout, dp0, dp1, dp2, dgate, dpost = merge_post_bwd(dxn, res["out"], ys, z["m"], wbs, W["out"], P["post"], ms, nb, tag + "merge_post_bwd")
    G = {"out": mm_dw(res["merged"], dout, tag + "dw_out"), "post": dpost}
    for nm, y, dp in zip(("bm", "bp", "bg"), ys, (dp0, dp1, dp2)):
        G[nm] = mm_dw(y, dp, tag + "dw_" + nm)
    dz = {"m": dzm}
    do, dz["gg"], G["gn"] = gla_out_bwd(r2(res["of"]), r2(res["ob"]), P["gn"], z["gg"], dy2, tag + "gla_out_bwd")
    grads = []
    for (qt, kt, ks, tot, ss), rev, nm in ((res["gla_f"], False, "gla_scan_f_bwd"), (res["gla_b"], True, "gla_scan_b_bwd")):
        grads.append(gla_scan_bwd(r3(qt), r3(kt), r3(ks), r3(z["gv"]), r3(tot), ss, r3(do), n_ctx, rev, tag + nm))
    gf = [r2(a) for a in grads[0]]
    gb = [r2(a) for a in grads[1]]
    dz["lr"], dz["gq"], dz["gk"], G["af"], G["ab"], G["baf"], G["bab"] = gla_prep_bwd(
        z["lr"], z["gq"], z["gk"], W["af"], W["ab"], P["baf"], P["bab"], gf[:3] + gf[4:], gb[:3] + gb[4:], tag + "gla_prep_bwd")
    dz["gv"] = add_cast(gf[3], gb[3], tag + "gla_dv")
    dpx, dpg, G["pw"], G["ps"] = pool_bwd(r3(z["px"]), r3(z["pg"]), P["pw"], P["ps"], r3(dy1), n_ctx, tag + "pool_bwd")
    dz["px"], dz["pg"] = r2(dpx), r2(dpg)
    dqn, dqr, dkv, dkr, dzmg = attention_bwd(r3(res["qn"]), r3(res["qr"]), r3(res["kv"]), r3(res["kr"]), r3(z["mg"]), res["ya"], r3(dy0), n_ctx, tag + "attention_bwd")
    dz["mg"] = r2(dzmg)
    dz["a"], G["qn"], G["qr"], G["kv"], G["qg"], G["kvg"] = mla_prep_bwd(
        r2(dqn), r2(dqr), r2(dkv), r2(dkr), z["a"], P["qg"], P["kvg"], W["qn"], W["qr"], W["kv"], cos, sin, nb, tag + "mla_prep_bwd")
    names = [g[0] for g in _IN_GROUPS]
    dh = mm_dx([dz[n] for n in names], [W["in_" + n] for n in names], tag + "in_proj_dx")
    for n in names:
        G["in_" + n] = mm_dw(res["h"], dz[n], tag + "dw_in_" + n)
    dx, dshift, dscale, G["pre"] = norm_mod_bwd(dh, res["x2"], P["pre"], ms, dxn, nb, tag + "norm_mod_bwd")
    dms = jnp.concatenate([dshift, dscale, dgate], axis=-1)
    return dx, G, dms


def add_cast(a, b, name):
    T, n = a.shape

    def body(a_ref, b_ref, o_ref):
        o_ref[...] = (a_ref[...] + b_ref[...]).astype(BF16)

    tok = pl.BlockSpec((TOKEN_BLOCK, n), lambda i: (i, 0))
    return pl.pallas_call(body, name=name, grid=(T // TOKEN_BLOCK,), in_specs=[tok, tok], out_specs=tok,
                          out_shape=jax.ShapeDtypeStruct((T, n), BF16), compiler_params=_params(("parallel",)))(a, b)


def layer_grads_natural(G):
    parts = {}
    for nm, off, n, npad in _IN_GROUPS:
        parts[off] = G["in_" + nm][:, :n]
    w_in = jnp.concatenate([parts[o] for o in sorted(parts)], axis=1)
    gqn = G["qn"].reshape(MLA_Q_RANK, MLA_HEADS, LANES)[:, :, :MLA_NOPE]
    gqr = G["qr"].reshape(MLA_Q_RANK, MLA_HEADS, LANES)[:, :, :MLA_ROPE]
    w_uq = jnp.concatenate([gqn, gqr], axis=-1).reshape(MLA_Q_RANK, MLA_HEADS * (MLA_NOPE + MLA_ROPE))
    return dict(w_in=w_in, mla_w_uq=w_uq, mla_w_ukv=G["kv"], gla_af_w2=G["af"][:GLA_GATE_RANK], gla_ab_w2=G["ab"][GLA_GATE_RANK:2 * GLA_GATE_RANK],
                w_branch_mla=G["bm"], w_branch_pool=G["bp"], w_branch_gla=G["bg"], w_out=G["out"],
                pre_norm=G["pre"][0], post_norm=G["post"][0], mla_q_norm=G["qg"][0], mla_kv_norm=G["kvg"][0], pool_w=G["pw"], pool_scale=G["ps"][0],
                gla_af_b=G["baf"][0], gla_ab_b=G["bab"][0], gla_norm=G["gn"][0])


def local_step(x, c, ctx, c_ctx, full, loss_target):
    B, S, _ = x.shape
    n_ctx = ctx.shape[1]
    L = n_ctx + S
    nb = L // TOKEN_BLOCK
    depth = full["w_in"].shape[0]
    cos, sin = rope_tables(L, n_ctx)
    x2 = jnp.concatenate([ctx, x], axis=1).reshape(B * L, D_MODEL)
    a8 = jnp.zeros((8, D_MODEL), F32).at[:B].set(c).at[B].set(c_ctx)
    Ws, Ps, mss, ress = [], [], [], []
    for l in range(depth):
        tag = f"l{l}_"
        W = layer_weights(full["w_in"][l], full["mla_w_uq"][l], full["mla_w_ukv"][l], full["gla_af_w2"][l], full["gla_ab_w2"][l],
                          full["w_branch_mla"][l], full["w_branch_pool"][l], full["w_branch_gla"][l], full["w_out"][l])
        P = dict(pre=full["pre_norm"][l][None], post=full["post_norm"][l][None], qg=full["mla_q_norm"][l][None], kvg=full["mla_kv_norm"][l][None],
                 pw=full["pool_w"][l].astype(BF16), ps=full["pool_scale"][l][None], baf=full["gla_af_b"][l][None], bab=full["gla_ab_b"][l][None],
                 gn=full["gla_norm"][l][None])
        mod8 = mod_fwd(a8, full["mod_w"][l], full["mod_b"][l][None], tag + "mod")
        ms = jnp.stack([jnp.broadcast_to(mod8[B], (B, 3 * D_MODEL)), mod8[:B]], axis=1).reshape(2 * B, 1, 3 * D_MODEL)
        x2, res = layer_fwd(x2, ms, W, P, cos, sin, B, L, n_ctx, tag)
        Ws.append(W), Ps.append(P), mss.append(ms), ress.append(res)
    dx, loss = loss_head(x2, loss_target.reshape(B * S, D_MODEL), nb, "loss_head")
    grads = [None] * depth
    da8 = None
    for l in reversed(range(depth)):
        tag = f"l{l}_"
        dx, G, dms = layer_bwd(dx, ress[l], mss[l], Ws[l], Ps[l], cos, sin, B, L, n_ctx, tag)
        dms = dms.reshape(B, 2, 3 * D_MODEL)
        dz8 = jnp.zeros((8, 3 * D_MODEL), F32).at[:B].set(dms[:, 1]).at[B].set(jnp.sum(dms[:, 0], axis=0))
        g_mod_w, g_mod_b, da = mod_bwd(a8, full["mod_w"][l], dz8, tag + "mod_bwd")
        da8 = da if da8 is None else da8 + da
        g = layer_grads_natural(G)
        g["mod_w"], g["mod_b"] = g_mod_w, g_mod_b[0]
        grads[l] = g
    gstack = {k: jnp.stack([grads[l][k] for l in range(depth)]) for k in grads[0]}
    gstack["c_ctx"] = da8[B]
    grad_x = dx.reshape(B, L, D_MODEL)[:, n_ctx:]
    return loss, grad_x, gstack


_MESH_ID = pl.DeviceIdType.MESH
_HBM = pl.BlockSpec(memory_space=pltpu.HBM)


def _me_and_peers():
    mx, my, mc = lax.axis_index("x"), lax.axis_index("y"), lax.axis_index("c")
    peers = []
    for k in range(1, N_DEV):
        px, py, pc = mx ^ ((k >> 2) & 1), my ^ ((k >> 1) & 1), mc ^ (k & 1)
        peers.append(((px, py, pc), 4 * px + 2 * py + pc))
    return 4 * mx + 2 * my + mc, peers


def all_gather(x, name):
    def body(x_ref, o_ref, send_sems, recv_sems, local_sem):
        me, peers = _me_and_peers()
        mine = pltpu.make_async_copy(x_ref, o_ref.at[me], local_sem)
        mine.start()
        sends = []
        for k, (peer, _) in enumerate(peers):
            cp = pltpu.make_async_remote_copy(src_ref=x_ref, dst_ref=o_ref.at[me], send_sem=send_sems.at[k], recv_sem=recv_sems.at[k],
                                              device_id=peer, device_id_type=_MESH_ID)
            cp.start()
            sends.append(cp)
        for k, (peer, slot) in enumerate(peers):
            pltpu.make_async_remote_copy(src_ref=x_ref, dst_ref=o_ref.at[slot], send_sem=send_sems.at[k], recv_sem=recv_sems.at[k],
                                         device_id=peer, device_id_type=_MESH_ID).wait_recv()
        for cp in sends:
            cp.wait_send()
        mine.wait()

    return pl.pallas_call(
        body, name=name, in_specs=[_HBM], out_specs=_HBM,
        out_shape=jax.ShapeDtypeStruct((N_DEV,) + x.shape, x.dtype),
        scratch_shapes=[pltpu.SemaphoreType.DMA((N_DEV - 1,)), pltpu.SemaphoreType.DMA((N_DEV - 1,)), pltpu.SemaphoreType.DMA],
    )(x)


def all_to_all(x, name):
    def body(x_ref, o_ref, send_sems, recv_sems, local_sem):
        me, peers = _me_and_peers()
        mine = pltpu.make_async_copy(x_ref.at[me], o_ref.at[me], local_sem)
        mine.start()
        sends = []
        for k, (peer, slot) in enumerate(peers):
            cp = pltpu.make_async_remote_copy(src_ref=x_ref.at[slot], dst_ref=o_ref.at[me], send_sem=send_sems.at[k], recv_sem=recv_sems.at[k],
                                              device_id=peer, device_id_type=_MESH_ID)
            cp.start()
            sends.append(cp)
        for k, (peer, slot) in enumerate(peers):
            pltpu.make_async_remote_copy(src_ref=x_ref.at[slot], dst_ref=o_ref.at[slot], send_sem=send_sems.at[k], recv_sem=recv_sems.at[k],
                                         device_id=peer, device_id_type=_MESH_ID).wait_recv()
        for cp in sends:
            cp.wait_send()
        mine.wait()

    return pl.pallas_call(
        body, name=name, in_specs=[_HBM], out_specs=_HBM,
        out_shape=jax.ShapeDtypeStruct(x.shape, x.dtype),
        scratch_shapes=[pltpu.SemaphoreType.DMA((N_DEV - 1,)), pltpu.SemaphoreType.DMA((N_DEV - 1,)), pltpu.SemaphoreType.DMA],
    )(x)


def reduce_adamw(slots, w, m, v, name, tr=512):
    R = w.shape[0]
    tr = min(tr, R)
    c1 = 1.0 / (1.0 - ADAM_B1 ** ADAM_STEP)
    c2 = 1.0 / (1.0 - ADAM_B2 ** ADAM_STEP)

    def body(s_ref, w_ref, m_ref, v_ref, g_ref, d_ref, nm_ref, nv_ref):
        g = s_ref[0]
        for s in range(1, N_DEV):
            g = g + s_ref[s]
        nm = ADAM_B1 * m_ref[...] + (1.0 - ADAM_B1) * g
        nv = ADAM_B2 * v_ref[...] + (1.0 - ADAM_B2) * (g * g)
        g_ref[...] = g
        nm_ref[...] = nm
        nv_ref[...] = nv
        d_ref[...] = -ADAM_LR * ((nm * c1) / (jnp.sqrt(nv * c2) + ADAM_EPS) + ADAM_WD * w_ref[...])

    blk = pl.BlockSpec((tr, LANES), lambda i: (i, 0))
    return pl.pallas_call(
        body, name=name, grid=(R // tr,),
        in_specs=[pl.BlockSpec((N_DEV, tr, LANES), lambda i: (0, i, 0)), blk, blk, blk], out_specs=[blk] * 4,
        out_shape=[jax.ShapeDtypeStruct((R, LANES), F32)] * 4,
        compiler_params=_params(("parallel",)),
    )(slots, w, m, v)


ARG_WEIGHTS = ("c_ctx", "mod_w", "mod_b", "pre_norm", "post_norm", "w_in", "mla_q_norm", "mla_w_uq", "mla_kv_norm", "mla_w_ukv", "pool_w",
               "pool_scale", "gla_af_w2", "gla_af_b", "gla_ab_w2", "gla_ab_b", "gla_norm", "w_branch_mla", "w_branch_pool", "w_branch_gla", "w_out")
SHARDED = ("mod_w", "w_in", "mla_w_uq", "mla_w_ukv", "gla_af_w2", "gla_ab_w2", "w_branch_mla", "w_branch_pool", "w_branch_gla", "w_out")
ROW_SHARDED = ("w_out",)
REPLICATED = tuple(n for n in ARG_WEIGHTS if n not in SHARDED)
PACK_ROWS = 512


def _pack(parts, dtype):
    flat = jnp.concatenate([p.astype(dtype).reshape(-1) for p in parts])
    n = flat.shape[0]
    total = -(-n // (PACK_ROWS * LANES)) * (PACK_ROWS * LANES)
    return jnp.pad(flat, (0, total - n)).reshape(total // LANES, LANES)


def _unpack(buf, shapes):
    flat = buf.reshape(-1)
    out, off = [], 0
    for shp in shapes:
        n = math.prod(shp)
        out.append(flat[off:off + n].reshape(shp))
        off += n
    return out


def _gathered_to_full(g, name, shard_shape):
    depth, r, cs = shard_shape
    if name in ROW_SHARDED:
        return g.transpose(1, 0, 2, 3).reshape(depth, N_DEV * r, cs)
    return g.transpose(1, 2, 0, 3).reshape(depth, r, N_DEV * cs)


def _full_to_slots(w, name, shard_shape):
    depth, r, cs = shard_shape
    if name in ROW_SHARDED:
        return w.reshape(depth, N_DEV, r, cs).transpose(1, 0, 2, 3)
    return w.reshape(depth, r, N_DEV, cs).transpose(2, 0, 1, 3)


def kernel(x, c, ctx, c_ctx, mod_w, mod_b, pre_norm, post_norm, w_in, mla_q_norm, mla_w_uq, mla_kv_norm, mla_w_ukv, pool_w, pool_scale, gla_af_w2, gla_af_b, gla_ab_w2, gla_ab_b, gla_norm, w_branch_mla, w_branch_pool, w_branch_gla, w_out, loss_target, m_c_ctx, m_mod_w, m_mod_b, m_pre_norm, m_post_norm, m_w_in, m_mla_q_norm, m_mla_w_uq, m_mla_kv_norm, m_mla_w_ukv, m_pool_w, m_pool_scale, m_gla_af_w2, m_gla_af_b, m_gla_ab_w2, m_gla_ab_b, m_gla_norm, m_w_branch_mla, m_w_branch_pool, m_w_branch_gla, m_w_out, v_c_ctx, v_mod_w, v_mod_b, v_pre_norm, v_post_norm, v_w_in, v_mla_q_norm, v_mla_w_uq, v_mla_kv_norm, v_mla_w_ukv, v_pool_w, v_pool_scale, v_gla_af_w2, v_gla_af_b, v_gla_ab_w2, v_gla_ab_b, v_gla_norm, v_w_branch_mla, v_w_branch_pool, v_w_branch_gla, v_w_out):
    local = dict(locals())
    wts = {n: local[n] for n in ARG_WEIGHTS}
    mom1 = {n: local["m_" + n] for n in ARG_WEIGHTS}
    mom2 = {n: local["v_" + n] for n in ARG_WEIGHTS}
    shard_shapes = [wts[n].shape for n in SHARDED]
    rep_shapes = [wts[n].shape for n in REPLICATED]

    gathered = all_gather(_pack([wts[n] for n in SHARDED], BF16), "gather_weights")
    per_dev = [_unpack(gathered[s], shard_shapes) for s in range(N_DEV)]
    full = {n: wts[n] for n in REPLICATED}
    for i, n in enumerate(SHARDED):
        full[n] = _gathered_to_full(jnp.stack([per_dev[s][i] for s in range(N_DEV)]), n, shard_shapes[i])

    loss, grad_x, g = local_step(x, c, ctx, c_ctx, full, loss_target)

    slots = jnp.stack([_pack([_full_to_slots(g[n], n, shp)[s] for n, shp in zip(SHARDED, shard_shapes)], F32) for s in range(N_DEV)])
    arrived = all_to_all(slots, "exchange_grads")
    outs = reduce_adamw(arrived, _pack([wts[n] for n in SHARDED], F32), _pack([mom1[n] for n in SHARDED], F32),
                        _pack([mom2[n] for n in SHARDED], F32), "adamw_sharded")
    res = {kind: dict(zip(SHARDED, _unpack(o, shard_shapes))) for kind, o in zip(("grad", "delta", "new_m", "new_v"), outs)}

    arrived = all_gather(_pack([g[n] for n in REPLICATED], F32), "gather_small_grads")
    outs = reduce_adamw(arrived, _pack([wts[n] for n in REPLICATED], F32), _pack([mom1[n] for n in REPLICATED], F32),
                        _pack([mom2[n] for n in REPLICATED], F32), "adamw_replicated")
    for kind, o in zip(("grad", "delta", "new_m", "new_v"), outs):
        res[kind].update(zip(REPLICATED, _unpack(o, rep_shapes)))

    loss = lax.psum(loss[0, 0], ("x", "y", "c"))
    return (loss, grad_x, *[res[kind][n] for kind in ("grad", "delta", "new_m", "new_v") for n in ARG_WEIGHTS])
```

```python
import functools
import math

import jax
import jax.numpy as jnp
import numpy as np
from jax import lax
from jax.experimental import pallas as pl
from jax.experimental.pallas import tpu as pltpu

F32 = jnp.float32
BF16 = jnp.bfloat16

D_MODEL = 1024
NORM_EPS = 1e-6
GRID_W = 64
MLA_HEADS, MLA_Q_RANK, MLA_KV_RANK, MLA_NOPE, MLA_ROPE, MLA_V = 8, 256, 128, 64, 32, 64
MLA_WIDTH = MLA_HEADS * MLA_V
ROPE_BASE = 10000.0
ATT_SCALE = (MLA_NOPE + MLA_ROPE) ** -0.5
POOL_WINDOWS = (2, 4, 8, 16)
POOL_WIDTH, POOL_GROUP = 512, 128
GLA_HEADS, GLA_DK, GLA_DV = 4, 64, 128
GLA_KW, GLA_WIDTH = GLA_HEADS * GLA_DK, GLA_HEADS * GLA_DV
GLA_GATE_RANK, GLA_TAU, GLA_CHUNK = 16, 16.0, 64
IN_SIZES = (256, 128, 32, 512, 512, 512, 256, 256, 512, 16, 16, 512, 3 * D_MODEL)
ADAM_LR, ADAM_B1, ADAM_B2, ADAM_EPS, ADAM_WD, ADAM_STEP = 0.001, 0.9, 0.999, 1e-08, 0.01, 10
N_DEV = 8

LANES = 128
TOKEN_BLOCK = 256
VMEM_LIMIT = 48 * 1024 * 1024
NEG_BIG = -1e30

_NT = (((1,), (1,)), ((), ()))
_TN = (((0,), (0,)), ((), ()))


def _dot(a, b):
    return jnp.dot(a, b, preferred_element_type=F32)


def _dot_nt(a, b):
    return lax.dot_general(a, b, _NT, preferred_element_type=F32)


def _dot_tn(a, b):
    return lax.dot_general(a, b, _TN, preferred_element_type=F32)


def _params(sem=None, vmem=None):
    kw = {}
    if sem is not None:
        kw["dimension_semantics"] = sem
    if vmem is not None:
        kw["vmem_limit_bytes"] = vmem
    return pltpu.CompilerParams(**kw)


def _full(shape):
    n = len(shape)
    return pl.BlockSpec(shape, lambda *_: (0,) * n)


def _sig(x):
    return 1.0 / (1.0 + jnp.exp(-x))


def _silu_and_grad(x):
    s = _sig(x)
    return x * s, s * (1.0 + x * (1.0 - s))


def _acc(ref, val, first):
    @pl.when(first)
    def _():
        ref[...] = val

    @pl.when(jnp.logical_not(first))
    def _():
        ref[...] += val


def mm_multi(a, ws, dtypes, name, tm=TOKEN_BLOCK):
    M, K = a.shape
    nw = len(ws)

    def body(a_ref, *refs):
        av = a_ref[...]
        for w_ref, o_ref in zip(refs[:nw], refs[nw:]):
            o_ref[...] = _dot(av, w_ref[...]).astype(o_ref.dtype)

    return pl.pallas_call(
        body, name=name, grid=(M // tm,),
        in_specs=[pl.BlockSpec((tm, K), lambda i: (i, 0))] + [_full(w.shape) for w in ws],
        out_specs=[pl.BlockSpec((tm, w.shape[1]), lambda i: (i, 0)) for w in ws],
        out_shape=[jax.ShapeDtypeStruct((M, w.shape[1]), dt) for w, dt in zip(ws, dtypes)],
        compiler_params=_params(("parallel",), VMEM_LIMIT),
    )(a, *ws)


def mm_dx(dzs, ws, name, tm=TOKEN_BLOCK):
    M = dzs[0].shape[0]
    K = ws[0].shape[0]
    nw = len(ws)

    def body(*refs):
        o_ref = refs[-1]
        acc = None
        for dz_ref, w_ref in zip(refs[:nw], refs[nw:2 * nw]):
            t = _dot_nt(dz_ref[...], w_ref[...])
            acc = t if acc is None else acc + t
        o_ref[...] = acc

    return pl.pallas_call(
        body, name=name, grid=(M // tm,),
        in_specs=[pl.BlockSpec((tm, dz.shape[1]), lambda i: (i, 0)) for dz in dzs] + [_full(w.shape) for w in ws],
        out_specs=pl.BlockSpec((tm, K), lambda i: (i, 0)),
        out_shape=jax.ShapeDtypeStruct((M, K), F32),
        compiler_params=_params(("parallel",), VMEM_LIMIT),
    )(*dzs, *ws)


def mm_dw(a, dz, name, tn=1024):
    M, K = a.shape
    n = dz.shape[1]
    tn = min(tn, n)
    tk = next(t for t in (3072, 1536, 1024, 512, TOKEN_BLOCK) if M % t == 0)

    def body(a_ref, dz_ref, o_ref):
        _acc(o_ref, _dot_tn(a_ref[...], dz_ref[...]), pl.program_id(1) == 0)

    return pl.pallas_call(
        body, name=name, grid=(n // tn, M // tk),
        in_specs=[pl.BlockSpec((tk, K), lambda j, k: (k, 0)), pl.BlockSpec((tk, tn), lambda j, k: (k, j))],
        out_specs=pl.BlockSpec((K, tn), lambda j, k: (0, j)),
        out_shape=jax.ShapeDtypeStruct((K, n), F32),
        compiler_params=_params(("parallel", "arbitrary"), VMEM_LIMIT),
    )(a, dz)


def mod_fwd(a8, w, b, name):
    tn = D_MODEL

    def body(a_ref, w_ref, b_ref, o_ref):
        a = a_ref[...]
        o_ref[...] = _dot((a * _sig(a)).astype(BF16), w_ref[...]) + b_ref[...]

    return pl.pallas_call(
        body, name=name, grid=(3,),
        in_specs=[_full(a8.shape), pl.BlockSpec((D_MODEL, tn), lambda j: (0, j)), pl.BlockSpec((1, tn), lambda j: (0, j))],
        out_specs=pl.BlockSpec((8, tn), lambda j: (0, j)),
        out_shape=jax.ShapeDtypeStruct((8, 3 * D_MODEL), F32),
        compiler_params=_params(("parallel",)),
    )(a8, w, b)


def mod_bwd(a8, w, dz8, name):
    tn = D_MODEL

    def body(a_ref, w_ref, dz_ref, dw_ref, db_ref, da_ref):
        a = a_ref[...]
        sa, dsa = _silu_and_grad(a)
        dz = dz_ref[...]
        dw_ref[...] = _dot_tn(sa.astype(BF16), dz.astype(BF16))
        db_ref[...] = jnp.sum(dz, axis=0, keepdims=True)
        _acc(da_ref, _dot_nt(dz.astype(BF16), w_ref[...]) * dsa, pl.program_id(0) == 0)

    return pl.pallas_call(
        body, name=name, grid=(3,),
        in_specs=[_full(a8.shape), pl.BlockSpec((D_MODEL, tn), lambda j: (0, j)), pl.BlockSpec((8, tn), lambda j: (0, j))],
        out_specs=[pl.BlockSpec((D_MODEL, tn), lambda j: (0, j)), pl.BlockSpec((1, tn), lambda j: (0, j)), _full((8, D_MODEL))],
        out_shape=[jax.ShapeDtypeStruct((D_MODEL, 3 * D_MODEL), F32), jax.ShapeDtypeStruct((1, 3 * D_MODEL), F32),
                   jax.ShapeDtypeStruct((8, D_MODEL), F32)],
        compiler_params=_params(("arbitrary",)),
    )(a8, w, dz8)


def _mod_row(nb):
    return lambda i: 2 * (i // nb) + jnp.minimum(i % nb, 1)


def _mod_spec(nb, part):
    row = _mod_row(nb)
    return pl.BlockSpec((1, 1, D_MODEL), lambda i: (row(i), 0, part))


def norm_mod_fwd(x2, g, ms, nb, name):
    T = x2.shape[0]

    def body(x_ref, g_ref, sh_ref, sc_ref, h_ref):
        x = x_ref[...]
        r = lax.rsqrt(jnp.mean(x * x, axis=-1, keepdims=True) + NORM_EPS)
        h_ref[...] = ((x * r) * g_ref[...] * (1.0 + sc_ref[0]) + sh_ref[0]).astype(BF16)

    return pl.pallas_call(
        body, name=name, grid=(T // TOKEN_BLOCK,),
        in_specs=[pl.BlockSpec((TOKEN_BLOCK, D_MODEL), lambda i: (i, 0)), _full((1, D_MODEL)), _mod_spec(nb, 0), _mod_spec(nb, 1)],
        out_specs=pl.BlockSpec((TOKEN_BLOCK, D_MODEL), lambda i: (i, 0)),
        out_shape=jax.ShapeDtypeStruct((T, D_MODEL), BF16),
        compiler_params=_params(("parallel",)),
    )(x2, g, ms, ms)


def norm_mod_bwd(dh, x2, g, ms, dxres, nb, name):
    T = x2.shape[0]
    nrow = ms.shape[0]
    row = _mod_row(nb)

    def body(dh_ref, x_ref, g_ref, sc_ref, dxr_ref, dx_ref, dsh_ref, dsc_ref, dg_ref):
        i = pl.program_id(0)
        t = i % nb
        x = x_ref[...]
        dh = dh_ref[...]
        g = g_ref[...]
        r = lax.rsqrt(jnp.mean(x * x, axis=-1, keepdims=True) + NORM_EPS)
        xn = x * r
        du = dh * (1.0 + sc_ref[0])
        dyg = du * g
        dx_ref[...] = dxr_ref[...] + r * (dyg - xn * jnp.mean(dyg * xn, axis=-1, keepdims=True))
        first = t <= 1
        _acc(dsh_ref.at[0], jnp.sum(dh, axis=0, keepdims=True), first)
        _acc(dsc_ref.at[0], jnp.sum(dh * xn * g, axis=0, keepdims=True), first)
        _acc(dg_ref, jnp.sum(du * xn, axis=0, keepdims=True), i == 0)

    tok = pl.BlockSpec((TOKEN_BLOCK, D_MODEL), lambda i: (i, 0))
    acc = pl.BlockSpec((1, 1, D_MODEL), lambda i: (row(i), 0, 0))
    return pl.pallas_call(
        body, name=name, grid=(T // TOKEN_BLOCK,),
        in_specs=[tok, tok, _full((1, D_MODEL)), _mod_spec(nb, 1), tok],
        out_specs=[tok, acc, acc, _full((1, D_MODEL))],
        out_shape=[jax.ShapeDtypeStruct((T, D_MODEL), F32), jax.ShapeDtypeStruct((nrow, 1, D_MODEL), F32),
                   jax.ShapeDtypeStruct((nrow, 1, D_MODEL), F32), jax.ShapeDtypeStruct((1, D_MODEL), F32)],
        compiler_params=_params(("arbitrary",)),
    )(dh, x2, g, ms, dxres)


def _rot(x):
    lane = lax.broadcasted_iota(jnp.int32, x.shape, 1)
    return jnp.where((lane % 16) < 8, -pltpu.roll(x, LANES - 8, 1), pltpu.roll(x, 8, 1))


def _rope(x, cos, sin):
    return x * cos + _rot(x) * sin


def _rope_t(dy, cos, sin):
    return dy * cos - _rot(dy * sin)


def _rms_rows(x):
    r = lax.rsqrt(jnp.mean(x * x, axis=-1, keepdims=True) + NORM_EPS)
    return x * r, r


def _rms_rows_bwd(dyg, xn, r):
    return r * (dyg - xn * jnp.mean(dyg * xn, axis=-1, keepdims=True))


def mla_prep_fwd(za, qg, kvg, wqn, wqr, wkv, cos, sin, nb, name):
    T = za.shape[0]
    W = MLA_HEADS * LANES

    def body(z_ref, qg_ref, kvg_ref, wqn_ref, wqr_ref, wkv_ref, cos_ref, sin_ref, qn_ref, qr_ref, kv_ref, kr_ref):
        z = z_ref[...]
        cos = cos_ref[...]
        sin = sin_ref[...]
        xq, _ = _rms_rows(z[:, 0:256])
        qn = (xq * qg_ref[...]).astype(BF16)
        qn_ref[...] = (_dot(qn, wqn_ref[...]) * ATT_SCALE).astype(BF16)
        qr = _dot(qn, wqr_ref[...])
        for h in range(MLA_HEADS):
            sl = slice(LANES * h, LANES * (h + 1))
            qr_ref[:, sl] = (_rope(qr[:, sl], cos, sin) * ATT_SCALE).astype(BF16)
        xkv, _ = _rms_rows(z[:, 256:384])
        kv_ref[...] = _dot((xkv * kvg_ref[...]).astype(BF16), wkv_ref[...]).astype(BF16)
        kr_ref[...] = _rope(z[:, 384:512], cos, sin).astype(BF16)

    tok = lambda n: pl.BlockSpec((TOKEN_BLOCK, n), lambda i: (i, 0))
    pos = pl.BlockSpec((TOKEN_BLOCK, LANES), lambda i: (i % nb, 0))
    return pl.pallas_call(
        body, name=name, grid=(T // TOKEN_BLOCK,),
        in_specs=[tok(512), _full(qg.shape), _full(kvg.shape), _full(wqn.shape), _full(wqr.shape), _full(wkv.shape), pos, pos],
        out_specs=[tok(W), tok(W), tok(W), tok(LANES)],
        out_shape=[jax.ShapeDtypeStruct((T, W), BF16)] * 3 + [jax.ShapeDtypeStruct((T, LANES), BF16)],
        compiler_params=_params(("parallel",)),
    )(za, qg, kvg, wqn, wqr, wkv, cos, sin)


def mla_prep_bwd(dqn, dqr, dkv, dkr, za, qg, kvg, wqn, wqr, wkv, cos, sin, nb, name):
    T = za.shape[0]
    W = MLA_HEADS * LANES

    def body(dqn_ref, dqr_ref, dkv_ref, dkr_ref, z_ref, qg_ref, kvg_ref, wqn_ref, wqr_ref, wkv_ref, cos_ref, sin_ref,
             dz_ref, dwqn_ref, dwqr_ref, dwkv_ref, dqg_ref, dkvg_ref):
        first = pl.program_id(0) == 0
        z = z_ref[...]
        cos = cos_ref[...]
        sin = sin_ref[...]
        qg = qg_ref[...]
        kvg = kvg_ref[...]
        xq, rq = _rms_rows(z[:, 0:256])
        qn = (xq * qg).astype(BF16)
        a1 = (dqn_ref[...].astype(F32) * ATT_SCALE).astype(BF16)
        parts = []
        for h in range(MLA_HEADS):
            sl = slice(LANES * h, LANES * (h + 1))
            parts.append(_rope_t(dqr_ref[:, sl].astype(F32) * ATT_SCALE, cos, sin).astype(BF16))
        a2 = jnp.concatenate(parts, axis=1)
        dq = _dot_nt(a1, wqn_ref[...]) + _dot_nt(a2, wqr_ref[...])
        _acc(dwqn_ref, _dot_tn(qn, a1), first)
        _acc(dwqr_ref, _dot_tn(qn, a2), first)
        _acc(dqg_ref, jnp.sum(dq * xq, axis=0, keepdims=True), first)
        dz_ref[:, 0:256] = _rms_rows_bwd(dq * qg, xq, rq).astype(BF16)
        xkv, rkv = _rms_rows(z[:, 256:384])
        kvn = (xkv * kvg).astype(BF16)
        dkvb = dkv_ref[...].astype(BF16)
        dk = _dot_nt(dkvb, wkv_ref[...])
        _acc(dwkv_ref, _dot_tn(kvn, dkvb), first)
        _acc(dkvg_ref, jnp.sum(dk * xkv, axis=0, keepdims=True), first)
        dz_ref[:, 256:384] = _rms_rows_bwd(dk * kvg, xkv, rkv).astype(BF16)
        dz_ref[:, 384:512] = _rope_t(dkr_ref[...], cos, sin).astype(BF16)

    tok = lambda n: pl.BlockSpec((TOKEN_BLOCK, n), lambda i: (i, 0))
    pos = pl.BlockSpec((TOKEN_BLOCK, LANES), lambda i: (i % nb, 0))
    return pl.pallas_call(
        body, name=name, grid=(T // TOKEN_BLOCK,),
        in_specs=[tok(W), tok(W), tok(W), tok(LANES), tok(512), _full(qg.shape), _full(kvg.shape), _full(wqn.shape),
                  _full(wqr.shape), _full(wkv.shape), pos, pos],
        out_specs=[tok(512), _full(wqn.shape), _full(wqr.shape), _full(wkv.shape), _full(qg.shape), _full(kvg.shape)],
        out_shape=[jax.ShapeDtypeStruct((T, 512), BF16), jax.ShapeDtypeStruct(wqn.shape, F32), jax.ShapeDtypeStruct(wqr.shape, F32),
                   jax.ShapeDtypeStruct(wkv.shape, F32), jax.ShapeDtypeStruct(qg.shape, F32), jax.ShapeDtypeStruct(kvg.shape, F32)],
        compiler_params=_params(("arbitrary",)),
    )(dqn, dqr, dkv, dkr, za, qg, kvg, wqn, wqr, wkv, cos, sin)


def _att_scores(qn_ref, qr_ref, kv_ref, kr, j, qi, n_ctx):
    sl = slice(LANES * j, LANES * (j + 1))
    q = jnp.concatenate([qn_ref[0, :, sl], qr_ref[0, :, sl]], axis=1)
    kvj = kv_ref[0, :, sl]
    k = jnp.concatenate([kvj, kr], axis=1)
    s = _dot_nt(q, k)
    kidx = lax.broadcasted_iota(jnp.int32, s.shape, 1)
    s = jnp.where(jnp.logical_or(qi > 0, kidx < n_ctx), s, NEG_BIG)
    p = jnp.exp(s - jnp.max(s, axis=-1, keepdims=True))
    return q, k, kvj, p, jnp.sum(p, axis=-1, keepdims=True)


def attention_fwd(qn, qr, kv, kr, zg, n_ctx, name):
    B, L, _ = qn.shape
    TQ = TOKEN_BLOCK
    W2 = 2 * LANES

    def body(qn_ref, qr_ref, kv_ref, kr_ref, g_ref, ya_ref, ym_ref):
        qi = pl.program_id(2)
        kr_v = kr_ref[0]
        outs = []
        for j in range(2):
            _, _, kvj, p, l = _att_scores(qn_ref, qr_ref, kv_ref, kr_v, j, qi, n_ctx)
            outs.append(_dot(p.astype(BF16), kvj) / l)
        lane = lax.broadcasted_iota(jnp.int32, outs[0].shape, 1)
        y = jnp.where(lane < MLA_V, pltpu.roll(outs[0], MLA_V, 1), outs[1])
        ya_ref[0] = y
        g = g_ref[0]
        ym_ref[0] = (y * g * _sig(g)).astype(BF16)

    qspec = pl.BlockSpec((1, TQ, W2), lambda b, h, i: (b, i, h))
    return pl.pallas_call(
        body, name=name, grid=(B, MLA_HEADS // 2, L // TQ),
        in_specs=[qspec, qspec, pl.BlockSpec((1, L, W2), lambda b, h, i: (b, 0, h)), pl.BlockSpec((1, L, LANES), lambda b, h, i: (b, 0, 0)),
                  pl.BlockSpec((1, TQ, LANES), lambda b, h, i: (b, i, h))],
        out_specs=[pl.BlockSpec((1, TQ, LANES), lambda b, h, i: (b, i, h))] * 2,
        out_shape=[jax.ShapeDtypeStruct((B, L, MLA_WIDTH), F32), jax.ShapeDtypeStruct((B, L, MLA_WIDTH), BF16)],
        compiler_params=_params(("parallel", "parallel", "parallel"), VMEM_LIMIT),
    )(qn, qr, kv, kr, zg)


def attention_bwd(qn, qr, kv, kr, zg, ya, dym, n_ctx, name):
    B, L, _ = qn.shape
    TQ = TOKEN_BLOCK
    W2 = 2 * LANES

    def body(qn_ref, qr_ref, kv_ref, kr_ref, g_ref, ya_ref, dy_ref, dqn_ref, dqr_ref, dkv_ref, dkr_ref, dzg_ref):
        hp = pl.program_id(1)
        qi = pl.program_id(2)
        g = g_ref[0]
        silu, dsilu = _silu_and_grad(g)
        dy = dy_ref[0]
        ya_v = ya_ref[0]
        dya = dy * silu
        dzg_ref[0] = (dy * ya_v * dsilu).astype(BF16)
        lane = lax.broadcasted_iota(jnp.int32, dya.shape, 1)
        hi = lane >= MLA_V
        d_out = [jnp.where(hi, pltpu.roll(dya, MLA_V, 1), 0.0), jnp.where(hi, dya, 0.0)]
        prod = dya * ya_v
        drow = [jnp.sum(jnp.where(hi, 0.0, prod), axis=-1, keepdims=True), jnp.sum(jnp.where(hi, prod, 0.0), axis=-1, keepdims=True)]

        @pl.when(qi == 0)
        def _():
            dkv_ref[...] = jnp.zeros_like(dkv_ref)

        @pl.when(jnp.logical_and(qi == 0, hp == 0))
        def _():
            dkr_ref[...] = jnp.zeros_like(dkr_ref)

        kr_v = kr_ref[0]
        for j in range(2):
            sl = slice(LANES * j, LANES * (j + 1))
            q, k, kvj, p, l = _att_scores(qn_ref, qr_ref, kv_ref, kr_v, j, qi, n_ctx)
            pn = p / l
            dob = d_out[j].astype(BF16)
            ds = (pn * (_dot_nt(dob, kvj) - drow[j])).astype(BF16)
            dq = _dot(ds, k)
            dqn_ref[0, :, sl] = jnp.where(hi, 0.0, dq[:, :LANES]).astype(BF16)
            dqr_ref[0, :, sl] = dq[:, LANES:].astype(BF16)
            dk = _dot_tn(ds, q)
            dkv_ref[0, :, sl] += dk[:, :LANES] + _dot_tn(pn.astype(BF16), dob)
            dkr_ref[0] += dk[:, LANES:]

    qspec = pl.BlockSpec((1, TQ, W2), lambda b, h, i: (b, i, h))
    kvspec = pl.BlockSpec((1, L, W2), lambda b, h, i: (b, 0, h))
    krspec = pl.BlockSpec((1, L, LANES), lambda b, h, i: (b, 0, 0))
    gspec = pl.BlockSpec((1, TQ, LANES), lambda b, h, i: (b, i, h))
    W = MLA_HEADS * LANES
    return pl.pallas_call(
        body, name=name, grid=(B, MLA_HEADS // 2, L // TQ),
        in_specs=[qspec, qspec, kvspec, krspec, gspec, gspec, gspec],
        out_specs=[qspec, qspec, kvspec, krspec, gspec],
        out_shape=[jax.ShapeDtypeStruct((B, L, W), BF16), jax.ShapeDtypeStruct((B, L, W), BF16), jax.ShapeDtypeStruct((B, L, W), F32),
                   jax.ShapeDtypeStruct((B, L, LANES), F32), jax.ShapeDtypeStruct((B, L, MLA_WIDTH), BF16)],
        compiler_params=_params(("parallel", "arbitrary", "arbitrary"), VMEM_LIMIT),
    )(qn, qr, kv, kr, zg, ya, dym)


def _seg_bounds(rows, n_ctx, L):
    in_ctx = rows < n_ctx
    return jnp.where(in_ctx, 0, n_ctx), jnp.where(in_ctx, n_ctx, L)


def _window_sum(u, w, rows, lo, hi, mirror):
    L = u.shape[0]
    offs = range(-w // 2 + 1, w // 2 + 1) if mirror else range(-w // 2, w // 2)
    acc = None
    for d in offs:
        if d == 0:
            t = u
        else:
            src = rows + d
            t = jnp.where(jnp.logical_and(src >= lo, src < hi), pltpu.roll(u, (-d) % L, 0), 0.0)
        acc = t if acc is None else acc + t
    return acc


def _window_count(w, rows, lo, hi):
    pos = rows - lo
    return (jnp.minimum(pos + w // 2, hi - lo) - jnp.maximum(pos - w // 2, 0)).astype(F32)


def pool_fwd(px, pg, pw, ps, n_ctx, name):
    B, L, _ = px.shape

    def body(px_ref, pg_ref, pw_ref, ps_ref, y_ref):
        rows = lax.broadcasted_iota(jnp.int32, (L, POOL_GROUP), 0)
        lo, hi = _seg_bounds(rows, n_ctx, L)
        for gi, w in enumerate(POOL_WINDOWS):
            sl = slice(POOL_GROUP * gi, POOL_GROUP * (gi + 1))
            u = px_ref[0, :, sl]
            pooled = _window_sum(u, w, rows, lo, hi, False) / _window_count(w, rows, lo, hi) - u
            mixed = _dot(pooled.astype(BF16), pw_ref[gi])
            g = pg_ref[0, :, sl]
            y_ref[0, :, sl] = (mixed * ps_ref[:, sl] * (g * _sig(g))).astype(BF16)

    tok = pl.BlockSpec((1, L, POOL_WIDTH), lambda b: (b, 0, 0))
    return pl.pallas_call(
        body, name=name, grid=(B,),
        in_specs=[tok, tok, _full(pw.shape), _full(ps.shape)],
        out_specs=tok, out_shape=jax.ShapeDtypeStruct((B, L, POOL_WIDTH), BF16),
        compiler_params=_params(("parallel",), VMEM_LIMIT),
    )(px, pg, pw, ps)


def pool_bwd(px, pg, pw, ps, dy, n_ctx, name):
    B, L, _ = px.shape

    def body(px_ref, pg_ref, pw_ref, ps_ref, dy_ref, dpx_ref, dpg_ref, dpw_ref, dps_ref):
        first = pl.program_id(0) == 0
        rows = lax.broadcasted_iota(jnp.int32, (L, POOL_GROUP), 0)
        lo, hi = _seg_bounds(rows, n_ctx, L)
        for gi, w in enumerate(POOL_WINDOWS):
            sl = slice(POOL_GROUP * gi, POOL_GROUP * (gi + 1))
            u = px_ref[0, :, sl]
            cnt = _window_count(w, rows, lo, hi)
            pooled = (_window_sum(u, w, rows, lo, hi, False) / cnt - u).astype(BF16)
            mixed = _dot(pooled, pw_ref[gi])
            silu, dsilu = _silu_and_grad(pg_ref[0, :, sl])
            sc = ps_ref[:, sl]
            dyv = dy_ref[0, :, sl]
            _acc(dps_ref.at[:, sl], jnp.sum(dyv * mixed * silu, axis=0, keepdims=True), first)
            dpg_ref[0, :, sl] = (dyv * mixed * sc * dsilu).astype(BF16)
            dmixed = (dyv * sc * silu).astype(BF16)
            _acc(dpw_ref.at[gi], _dot_tn(pooled, dmixed), first)
            dpooled = _dot_nt(dmixed, pw_ref[gi])
            dpx_ref[0, :, sl] = (_window_sum(dpooled / cnt, w, rows, lo, hi, True) - dpooled).astype(BF16)

    tok = pl.BlockSpec((1, L, POOL_WIDTH), lambda b: (b, 0, 0))
    return pl.pallas_call(
        body, name=name, grid=(B,),
        in_specs=[tok, tok, _full(pw.shape), _full(ps.shape), tok],
        out_specs=[tok, tok, _full(pw.shape), _full(ps.shape)],
        out_shape=[jax.ShapeDtypeStruct((B, L, POOL_WIDTH), BF16)] * 2 + [jax.ShapeDtypeStruct(pw.shape, F32), jax.ShapeDtypeStruct(ps.shape, F32)],
        compiler_params=_params(("arbitrary",), VMEM_LIMIT),
    )(px, pg, pw, ps, dy)


_SCAN_STEPS = (1, 2, 4, 8, 16, 32)


def _cum_fwd(x, r):
    for s in _SCAN_STEPS:
        x = x + jnp.where(r >= s, pltpu.roll(x, s, 0), 0.0)
    return x


def _cum_bwd(x, r):
    n = x.shape[0]
    for s in _SCAN_STEPS:
        x = x + jnp.where(r + s < GLA_CHUNK, pltpu.roll(x, n - s, 0), 0.0)
    return x


def _log_sigmoid(x):
    return jnp.minimum(x, 0.0) - jnp.log(1.0 + jnp.exp(-jnp.abs(x)))


def _gla_decays(lr, w_ref, b_ref, r, reverse):
    pre = _dot(lr, w_ref[...]) + b_ref[...]
    a = _log_sigmoid(pre) / GLA_TAU
    pf = _cum_fwd(a, r)
    sf = _cum_bwd(a, r)
    tot = pf + sf - a
    return pre, a, (sf if reverse else pf), tot


def gla_prep_fwd(zlr, zq, zk, waf, wab, baf, bab, name):
    T = zlr.shape[0]

    def body(lr_ref, q_ref, k_ref, waf_ref, wab_ref, baf_ref, bab_ref, qf_ref, kf_ref, ksf_ref, tf_ref, qb_ref, kb_ref, ksb_ref, tb_ref):
        r = lax.broadcasted_iota(jnp.int32, (TOKEN_BLOCK, GLA_KW), 0) % GLA_CHUNK
        lr = lr_ref[...].astype(BF16)
        q = q_ref[...] * GLA_DK ** -0.5
        k = k_ref[...]
        for rev, w_ref, b_ref, qo, ko, kso, to in ((False, waf_ref, baf_ref, qf_ref, kf_ref, ksf_ref, tf_ref),
                                                   (True, wab_ref, bab_ref, qb_ref, kb_ref, ksb_ref, tb_ref)):
            _, _, b, tot = _gla_decays(lr, w_ref, b_ref, r, rev)
            qo[...] = (q * jnp.exp(b)).astype(BF16)
            ko[...] = (k * jnp.exp(-b)).astype(BF16)
            kso[...] = (k * jnp.exp(tot - b)).astype(BF16)
            to[...] = tot

    tok = lambda n: pl.BlockSpec((TOKEN_BLOCK, n), lambda i: (i, 0))
    outs = [jax.ShapeDtypeStruct((T, GLA_KW), BF16)] * 3 + [jax.ShapeDtypeStruct((T, GLA_KW), F32)]
    return pl.pallas_call(
        body, name=name, grid=(T // TOKEN_BLOCK,),
        in_specs=[tok(LANES), tok(GLA_KW), tok(GLA_KW), _full(waf.shape), _full(wab.shape), _full(baf.shape), _full(bab.shape)],
        out_specs=[tok(GLA_KW)] * 8, out_shape=outs + outs,
        compiler_params=_params(("parallel",)),
    )(zlr, zq, zk, waf, wab, baf, bab)


def gla_prep_bwd(zlr, zq, zk, waf, wab, baf, bab, gf, gb, name):
    T = zlr.shape[0]

    def body(lr_ref, q_ref, k_ref, waf_ref, wab_ref, baf_ref, bab_ref, dqf, dkf, dksf, ddf, dqb, dkb, dksb, ddb,
             dlr_ref, dq_ref, dk_ref, dwaf_ref, dwab_ref, dbaf_ref, dbab_ref):
        first = pl.program_id(0) == 0
        r = lax.broadcasted_iota(jnp.int32, (TOKEN_BLOCK, GLA_KW), 0) % GLA_CHUNK
        lr = lr_ref[...].astype(BF16)
        q = q_ref[...] * GLA_DK ** -0.5
        k = k_ref[...]
        dq_tot = None
        dk_tot = None
        dlr = None
        for rev, w_ref, b_ref, dqt, dkt, dks, ddec, dw_ref, db_ref in (
                (False, waf_ref, baf_ref, dqf, dkf, dksf, ddf, dwaf_ref, dbaf_ref),
                (True, wab_ref, bab_ref, dqb, dkb, dksb, ddb, dwab_ref, dbab_ref)):
            pre, _, b, tot = _gla_decays(lr, w_ref, b_ref, r, rev)
            e1 = jnp.exp(b)
            e2 = jnp.exp(-b)
            e3 = jnp.exp(tot - b)
            dqt_v = dqt[...]
            dkt_v = dkt[...]
            dks_v = dks[...]
            dq = dqt_v * e1
            dk = dkt_v * e2 + dks_v * e3
            g3 = dks_v * (k * e3)
            d_b = dqt_v * (q * e1) - dkt_v * (k * e2) - g3
            d_tot = _cum_fwd(g3, r) + _cum_bwd(g3, r) - g3 + ddec[...] * jnp.exp(tot)
            da = (_cum_fwd(d_b, r) if rev else _cum_bwd(d_b, r)) + d_tot
            dpre = (da * (_sig(-pre) / GLA_TAU)).astype(BF16)
            t = _dot_nt(dpre, w_ref[...])
            dlr = t if dlr is None else dlr + t
            _acc(dw_ref, _dot_tn(lr, dpre), first)
            _acc(db_ref, jnp.sum(dpre.astype(F32), axis=0, keepdims=True), first)
            dq_tot = dq if dq_tot is None else dq_tot + dq
            dk_tot = dk if dk_tot is None else dk_tot + dk
        dlr_ref[...] = dlr.astype(BF16)
        dq_ref[...] = (dq_tot * GLA_DK ** -0.5).astype(BF16)
        dk_ref[...] = dk_tot.astype(BF16)

    tok = lambda n: pl.BlockSpec((TOKEN_BLOCK, n), lambda i: (i, 0))
    return pl.pallas_call(
        body, name=name, grid=(T // TOKEN_BLOCK,),
        in_specs=[tok(LANES), tok(GLA_KW), tok(GLA_KW), _full(waf.shape), _full(wab.shape), _full(baf.shape), _full(bab.shape)] + [tok(GLA_KW)] * 8,
        out_specs=[tok(LANES), tok(GLA_KW), tok(GLA_KW), _full(waf.shape), _full(wab.shape), _full(baf.shape), _full(bab.shape)],
        out_shape=[jax.ShapeDtypeStruct((T, LANES), BF16), jax.ShapeDtypeStruct((T, GLA_KW), BF16), jax.ShapeDtypeStruct((T, GLA_KW), BF16),
                   jax.ShapeDtypeStruct(waf.shape, F32), jax.ShapeDtypeStruct(wab.shape, F32), jax.ShapeDtypeStruct(baf.shape, F32),
                   jax.ShapeDtypeStruct(bab.shape, F32)],
        compiler_params=_params(("arbitrary",)),
    )(zlr, zq, zk, waf, wab, baf, bab, *gf, *gb)


def _chunk_order(nc, n_ctx_chunks, reverse):
    if not reverse:
        return lambda c: c
    return lambda c: jnp.where(c < n_ctx_chunks, n_ctx_chunks - 1 - c, nc + n_ctx_chunks - 1 - c)


def _head_mask(shape, h):
    lane = lax.broadcasted_iota(jnp.int32, shape, 1)
    return jnp.logical_and(lane >= GLA_DK * h, lane < GLA_DK * (h + 1))


def _tri_mask(reverse):
    ri = lax.broadcasted_iota(jnp.int32, (GLA_CHUNK, GLA_CHUNK), 0)
    ci = lax.broadcasted_iota(jnp.int32, (GLA_CHUNK, GLA_CHUNK), 1)
    return (ri <= ci) if reverse else (ri >= ci)


def gla_scan_fwd(qt, kt, ks, v, tot, n_ctx, reverse, name):
    B, L, _ = qt.shape
    C = GLA_CHUNK
    nc = L // C
    order = _chunk_order(nc, n_ctx // C, reverse)

    def body(q_ref, k_ref, ks_ref, v_ref, tot_ref, o_ref, ss_ref, st):
        @pl.when(pl.program_id(1) == 0)
        def _():
            st[...] = jnp.zeros_like(st)

        S = st[...]
        ss_ref[0, 0] = S
        Sb = S.astype(BF16)
        q = q_ref[0]
        k = k_ref[0]
        ksv = ks_ref[0]
        mask = _tri_mask(reverse)
        U = jnp.zeros_like(S)
        for h in range(GLA_HEADS):
            qm = jnp.where(_head_mask(q.shape, h), q, jnp.zeros_like(q))
            A = jnp.where(mask, _dot_nt(qm, k), 0.0)
            vh = v_ref[0, :, GLA_DV * h:GLA_DV * (h + 1)]
            o_ref[0, :, GLA_DV * h:GLA_DV * (h + 1)] = _dot(A.astype(BF16), vh.astype(BF16)) + _dot_nt(qm, Sb)
            U = U + jnp.where(_head_mask(S.shape, h), _dot(vh.T.astype(BF16), ksv), 0.0)
        st[...] = jnp.exp(tot_ref[0, 0:1, :]) * S + U

    tok = lambda n: pl.BlockSpec((1, C, n), lambda b, c: (b, order(c), 0))
    return pl.pallas_call(
        body, name=name, grid=(B, nc),
        in_specs=[tok(GLA_KW), tok(GLA_KW), tok(GLA_KW), tok(GLA_WIDTH), tok(GLA_KW)],
        out_specs=[tok(GLA_WIDTH), pl.BlockSpec((1, 1, GLA_DV, GLA_KW), lambda b, c: (b, order(c), 0, 0))],
        out_shape=[jax.ShapeDtypeStruct((B, L, GLA_WIDTH), F32), jax.ShapeDtypeStruct((B, nc, GLA_DV, GLA_KW), F32)],
        scratch_shapes=[pltpu.VMEM((GLA_DV, GLA_KW), F32)],
        compiler_params=_params(("parallel", "arbitrary")),
    )(qt, kt, ks, v, tot)


def gla_scan_bwd(qt, kt, ks, v, tot, ss, do, n_ctx, reverse, name):
    B, L, _ = qt.shape
    C = GLA_CHUNK
    nc = L // C
    fwd_order = _chunk_order(nc, n_ctx // C, reverse)
    order = lambda c: fwd_order(nc - 1 - c)

    def body(q_ref, k_ref, ks_ref, v_ref, tot_ref, ss_ref, do_ref, dq_ref, dk_ref, dks_ref, dv_ref, dd_ref, dst):
        @pl.when(pl.program_id(1) == 0)
        def _():
            dst[...] = jnp.zeros_like(dst)

        dSn = dst[...]
        dSnb = dSn.astype(BF16)
        S = ss_ref[0, 0]
        Sb = S.astype(BF16)
        dec = jnp.exp(tot_ref[0, 0:1, :])
        q = q_ref[0]
        k = k_ref[0]
        ksv = ks_ref[0]
        mask = _tri_mask(reverse)
        dq = jnp.zeros(q.shape, F32)
        dk = jnp.zeros(q.shape, F32)
        dks = jnp.zeros(q.shape, F32)
        dS = dec * dSn
        for h in range(GLA_HEADS):
            hm = _head_mask(q.shape, h)
            sl = slice(GLA_DV * h, GLA_DV * (h + 1))
            qm = jnp.where(hm, q, jnp.zeros_like(q))
            ksm = jnp.where(hm, ksv, jnp.zeros_like(ksv))
            vh = v_ref[0, :, sl].astype(BF16)
            doh = do_ref[0, :, sl]
            dohb = doh.astype(BF16)
            A = jnp.where(mask, _dot_nt(qm, k), 0.0)
            dA = jnp.where(mask, _dot_nt(dohb, vh), 0.0)
            dq = dq + jnp.where(hm, _dot(dohb, Sb) + _dot(dA.astype(BF16), k), 0.0)
            dk = dk + jnp.where(hm, _dot(dA.T.astype(BF16), q), 0.0)
            dv_ref[0, :, sl] = _dot(A.T.astype(BF16), dohb) + _dot_nt(ksm, dSnb)
            dks = dks + jnp.where(hm, _dot(vh, dSnb), 0.0)
            dS = dS + jnp.where(_head_mask(S.shape, h), _dot(doh.T.astype(BF16), q), 0.0)
        dq_ref[0] = dq
        dk_ref[0] = dk
        dks_ref[0] = dks
        dd_ref[0] = jnp.broadcast_to(jnp.sum(dSn * S, axis=0, keepdims=True), (C, GLA_KW))
        dst[...] = dS

    tok = lambda n: pl.BlockSpec((1, C, n), lambda b, c: (b, order(c), 0))
    return pl.pallas_call(
        body, name=name, grid=(B, nc),
        in_specs=[tok(GLA_KW), tok(GLA_KW), tok(GLA_KW), tok(GLA_WIDTH), tok(GLA_KW),
                  pl.BlockSpec((1, 1, GLA_DV, GLA_KW), lambda b, c: (b, order(c), 0, 0)), tok(GLA_WIDTH)],
        out_specs=[tok(GLA_KW), tok(GLA_KW), tok(GLA_KW), tok(GLA_WIDTH), tok(GLA_KW)],
        out_shape=[jax.ShapeDtypeStruct((B, L, GLA_KW), F32)] * 3 + [jax.ShapeDtypeStruct((B, L, GLA_WIDTH), F32), jax.ShapeDtypeStruct((B, L, GLA_KW), F32)],
        scratch_shapes=[pltpu.VMEM((GLA_DV, GLA_KW), F32)],
        compiler_params=_params(("parallel", "arbitrary")),
    )(qt, kt, ks, v, tot, ss, do)


def gla_out_fwd(of, ob, gn, zg, name):
    T = of.shape[0]

    def body(of_ref, ob_ref, gn_ref, g_ref, y_ref):
        for h in range(GLA_HEADS):
            sl = slice(GLA_DV * h, GLA_DV * (h + 1))
            xn, _ = _rms_rows(of_ref[:, sl] + ob_ref[:, sl])
            g = g_ref[:, sl]
            y_ref[:, sl] = (xn * gn_ref[...] * (g * _sig(g))).astype(BF16)

    tok = pl.BlockSpec((TOKEN_BLOCK, GLA_WIDTH), lambda i: (i, 0))
    return pl.pallas_call(
        body, name=name, grid=(T // TOKEN_BLOCK,),
        in_specs=[tok, tok, _full(gn.shape), tok], out_specs=tok,
        out_shape=jax.ShapeDtypeStruct((T, GLA_WIDTH), BF16),
        compiler_params=_params(("parallel",)),
    )(of, ob, gn, zg)


def gla_out_bwd(of, ob, gn, zg, dy, name):
    T = of.shape[0]

    def body(of_ref, ob_ref, gn_ref, g_ref, dy_ref, do_ref, dzg_ref, dgn_ref):
        first = pl.program_id(0) == 0
        gn_v = gn_ref[...]
        dgn = None
        for h in range(GLA_HEADS):
            sl = slice(GLA_DV * h, GLA_DV * (h + 1))
            xn, r = _rms_rows(of_ref[:, sl] + ob_ref[:, sl])
            silu, dsilu = _silu_and_grad(g_ref[:, sl])
            dyv = dy_ref[:, sl]
            dzg_ref[:, sl] = (dyv * xn * gn_v * dsilu).astype(BF16)
            dn = dyv * silu
            t = jnp.sum(dn * xn, axis=0, keepdims=True)
            dgn = t if dgn is None else dgn + t
            do_ref[:, sl] = _rms_rows_bwd(dn * gn_v, xn, r)
        _acc(dgn_ref, dgn, first)

    tok = pl.BlockSpec((TOKEN_BLOCK, GLA_WIDTH), lambda i: (i, 0))
    return pl.pallas_call(
        body, name=name, grid=(T // TOKEN_BLOCK,),
        in_specs=[tok, tok, _full(gn.shape), tok, tok], out_specs=[tok, tok, _full(gn.shape)],
        out_shape=[jax.ShapeDtypeStruct((T, GLA_WIDTH), F32), jax.ShapeDtypeStruct((T, GLA_WIDTH), BF16), jax.ShapeDtypeStruct(gn.shape, F32)],
        compiler_params=_params(("arbitrary",)),
    )(of, ob, gn, zg, dy)


def merge_post_fwd(ys, zm, wbs, wo, x2, pg, ms, nb, name):
    T = x2.shape[0]

    def body(y0, y1, y2, zm_ref, w0, w1, w2, wo_ref, x_ref, pg_ref, gate_ref, xn_ref, out_ref, mg_ref):
        merged = None
        for i, (y_ref, w_ref) in enumerate(((y0, w0), (y1, w1), (y2, w2))):
            t = _sig(zm_ref[:, D_MODEL * i:D_MODEL * (i + 1)]) * _dot(y_ref[...], w_ref[...])
            merged = t if merged is None else merged + t
        mb = merged.astype(BF16)
        mg_ref[...] = mb
        out = _dot(mb, wo_ref[...])
        out_ref[...] = out
        on, _ = _rms_rows(out)
        xn_ref[...] = x_ref[...] + gate_ref[0] * (on * pg_ref[...])

    tok = lambda n: pl.BlockSpec((TOKEN_BLOCK, n), lambda i: (i, 0))
    return pl.pallas_call(
        body, name=name, grid=(T // TOKEN_BLOCK,),
        in_specs=[tok(512)] * 3 + [tok(3 * D_MODEL)] + [_full(w.shape) for w in wbs] + [_full(wo.shape), tok(D_MODEL), _full(pg.shape), _mod_spec(nb, 2)],
        out_specs=[tok(D_MODEL)] * 3,
        out_shape=[jax.ShapeDtypeStruct((T, D_MODEL), F32), jax.ShapeDtypeStruct((T, D_MODEL), F32), jax.ShapeDtypeStruct((T, D_MODEL), BF16)],
        compiler_params=_params(("parallel",), VMEM_LIMIT),
    )(*ys, zm, *wbs, wo, x2, pg, ms)


def merge_post_bwd(dxn, out, ys, zm, wbs, wo, pg, ms, nb, name):
    T = dxn.shape[0]
    nrow = ms.shape[0]
    row = _mod_row(nb)

    def body(dxn_ref, out_ref, y0, y1, y2, zm_ref, w0, w1, w2, wo_ref, pg_ref, gate_ref,
             dy0, dy1, dy2, dzm_ref, dout_ref, dp0, dp1, dp2, dgate_ref, dpg_ref):
        i = pl.program_id(0)
        dxn_v = dxn_ref[...]
        on, r = _rms_rows(out_ref[...])
        pg_v = pg_ref[...]
        _acc(dgate_ref.at[0], jnp.sum(dxn_v * on * pg_v, axis=0, keepdims=True), (i % nb) <= 1)
        dn = dxn_v * gate_ref[0]
        _acc(dpg_ref, jnp.sum(dn * on, axis=0, keepdims=True), i == 0)
        dout = _rms_rows_bwd(dn * pg_v, on, r).astype(BF16)
        dout_ref[...] = dout
        dmerged = _dot_nt(dout, wo_ref[...])
        for j, (y_ref, w_ref, dy_ref, dp_ref) in enumerate(((y0, w0, dy0, dp0), (y1, w1, dy1, dp1), (y2, w2, dy2, dp2))):
            sl = slice(D_MODEL * j, D_MODEL * (j + 1))
            g = _sig(zm_ref[:, sl])
            p = _dot(y_ref[...], w_ref[...])
            dzm_ref[:, sl] = (dmerged * p * g * (1.0 - g)).astype(BF16)
            dp = (dmerged * g).astype(BF16)
            dp_ref[...] = dp
            dy_ref[...] = _dot_nt(dp, w_ref[...])

    tok = lambda n: pl.BlockSpec((TOKEN_BLOCK, n), lambda i: (i, 0))
    return pl.pallas_call(
        body, name=name, grid=(T // TOKEN_BLOCK,),
        in_specs=[tok(D_MODEL), tok(D_MODEL)] + [tok(512)] * 3 + [tok(3 * D_MODEL)] + [_full(w.shape) for w in wbs] + [_full(wo.shape), _full(pg.shape), _mod_spec(nb, 2)],
        out_specs=[tok(512)] * 3 + [tok(3 * D_MODEL), tok(D_MODEL)] + [tok(D_MODEL)] * 3 + [pl.BlockSpec((1, 1, D_MODEL), lambda i: (row(i), 0, 0)), _full(pg.shape)],
        out_shape=[jax.ShapeDtypeStruct((T, 512), F32)] * 3 + [jax.ShapeDtypeStruct((T, 3 * D_MODEL), BF16), jax.ShapeDtypeStruct((T, D_MODEL), BF16)]
        + [jax.ShapeDtypeStruct((T, D_MODEL), BF16)] * 3 + [jax.ShapeDtypeStruct((nrow, 1, D_MODEL), F32), jax.ShapeDtypeStruct(pg.shape, F32)],
        compiler_params=_params(("arbitrary",), VMEM_LIMIT),
    )(dxn, out, *ys, zm, *wbs, wo, pg, ms)


def loss_head(y2, tgt2, nb, name):
    T = y2.shape[0]
    nlat = nb - 1

    def body(y_ref, t_ref, dy_ref, loss_ref, acc):
        i = pl.program_id(0)
        is_lat = (i % nb) > 0

        @pl.when(i == 0)
        def _():
            acc[...] = jnp.zeros_like(acc)

        @pl.when(is_lat)
        def _():
            e = y_ref[...] - t_ref[...]
            dy_ref[...] = e * (1.0 / D_MODEL)
            acc[...] += jnp.sum(e * e, axis=0, keepdims=True)

        @pl.when(jnp.logical_not(is_lat))
        def _():
            dy_ref[...] = jnp.zeros_like(dy_ref)

        @pl.when(i == pl.num_programs(0) - 1)
        def _():
            loss_ref[...] = jnp.sum(acc[...], axis=1, keepdims=True) * (0.5 / D_MODEL)

    tok = pl.BlockSpec((TOKEN_BLOCK, D_MODEL), lambda i: (i, 0))
    tgt = pl.BlockSpec((TOKEN_BLOCK, D_MODEL), lambda i: ((i // nb) * nlat + jnp.maximum(i % nb - 1, 0), 0))
    return pl.pallas_call(
        body, name=name, grid=(T // TOKEN_BLOCK,),
        in_specs=[tok, tgt], out_specs=[tok, _full((1, 1))],
        out_shape=[jax.ShapeDtypeStruct((T, D_MODEL), F32), jax.ShapeDtypeStruct((1, 1), F32)],
        scratch_shapes=[pltpu.VMEM((1, D_MODEL), F32)],
        compiler_params=_params(("arbitrary",)),
    )(y2, tgt2)


_IN_OFFS = tuple(int(o) for o in np.cumsum((0,) + IN_SIZES))
_IN_GROUPS = (("a", 0, 416, 512), ("mg", 416, 512, 512), ("px", 928, 512, 512), ("pg", 1440, 512, 512), ("gq", 1952, 256, 256),
              ("gk", 2208, 256, 256), ("gv", 2464, 512, 512), ("lr", 2976, 32, 128), ("gg", 3008, 512, 512), ("m", 3520, 3072, 3072))


def _pad_cols(w, n):
    return w if w.shape[1] == n else jnp.pad(w, ((0, 0), (0, n - w.shape[1])))


def layer_weights(w_in, w_uq, w_ukv, af_w2, ab_w2, wbm, wbp, wbg, w_out):
    W = {}
    for nm, off, n, npad in _IN_GROUPS:
        W["in_" + nm] = _pad_cols(w_in[:, off:off + n], npad)
    uq = w_uq.reshape(MLA_Q_RANK, MLA_HEADS, MLA_NOPE + MLA_ROPE)
    W["qn"] = jnp.pad(uq[:, :, :MLA_NOPE], ((0, 0), (0, 0), (0, LANES - MLA_NOPE))).reshape(MLA_Q_RANK, MLA_HEADS * LANES)
    W["qr"] = jnp.pad(uq[:, :, MLA_NOPE:], ((0, 0), (0, 0), (0, LANES - MLA_ROPE))).reshape(MLA_Q_RANK, MLA_HEADS * LANES)
    W["kv"] = w_ukv
    W["af"] = jnp.pad(af_w2, ((0, LANES - GLA_GATE_RANK), (0, 0)))
    W["ab"] = jnp.pad(ab_w2, ((GLA_GATE_RANK, LANES - 2 * GLA_GATE_RANK), (0, 0)))
    W["bm"], W["bp"], W["bg"], W["out"] = wbm, wbp, wbg, w_out
    return W


def rope_tables(L, n_ctx):
    t = np.arange(L - n_ctx)
    half = MLA_ROPE // 2
    inv = ROPE_BASE ** (-np.arange(0, half, 2, dtype=np.float32) / half)
    ang_r = (t // GRID_W).astype(np.float32)[:, None] * inv
    ang_c = (t % GRID_W).astype(np.float32)[:, None] * inv
    ang = jnp.asarray(np.concatenate([ang_r, ang_r, ang_c, ang_c], axis=-1), F32)
    cos = jnp.ones((L, LANES), F32).at[n_ctx:, :MLA_ROPE].set(jnp.cos(ang))
    sin = jnp.zeros((L, LANES), F32).at[n_ctx:, :MLA_ROPE].set(jnp.sin(ang))
    return cos, sin


def layer_fwd(x2, ms, W, P, cos, sin, B, L, n_ctx, tag):
    nb = L // TOKEN_BLOCK
    r3 = lambda a: a.reshape(B, L, a.shape[-1])
    r2 = lambda a: a.reshape(B * L, a.shape[-1])
    h = norm_mod_fwd(x2, P["pre"], ms, nb, tag + "norm_mod")
    names = [g[0] for g in _IN_GROUPS[:-1]]
    z = dict(zip(names, mm_multi(h, [W["in_" + n] for n in names], [F32] * len(names), tag + "in_proj")))
    (z["m"],) = mm_multi(h, [W["in_m"]], [F32], tag + "in_proj_merge")
    qn, qr, kv, kr = mla_prep_fwd(z["a"], P["qg"], P["kvg"], W["qn"], W["qr"], W["kv"], cos, sin, nb, tag + "mla_prep")
    ya, y_mla = attention_fwd(r3(qn), r3(qr), r3(kv), r3(kr), r3(z["mg"]), n_ctx, tag + "attention")
    y_pool = pool_fwd(r3(z["px"]), r3(z["pg"]), P["pw"], P["ps"], n_ctx, tag + "pool")
    qf, kf, ksf, tf, qb, kb, ksb, tb = gla_prep_fwd(z["lr"], z["gq"], z["gk"], W["af"], W["ab"], P["baf"], P["bab"], tag + "gla_prep")
    of, ssf = gla_scan_fwd(r3(qf), r3(kf), r3(ksf), r3(z["gv"]), r3(tf), n_ctx, False, tag + "gla_scan_f")
    ob, ssb = gla_scan_fwd(r3(qb), r3(kb), r3(ksb), r3(z["gv"]), r3(tb), n_ctx, True, tag + "gla_scan_b")
    y_gla = gla_out_fwd(r2(of), r2(ob), P["gn"], z["gg"], tag + "gla_out")
    ys = [r2(y_mla), r2(y_pool), y_gla]
    x_new, out, merged = merge_post_fwd(ys, z["m"], [W["bm"], W["bp"], W["bg"]], W["out"], x2, P["post"], ms, nb, tag + "merge_post")
    res = dict(x2=x2, h=h, z=z, qn=qn, qr=qr, kv=kv, kr=kr, ya=ya, ys=ys, gla_f=(qf, kf, ksf, tf, ssf), gla_b=(qb, kb, ksb, tb, ssb),
               of=of, ob=ob, out=out, merged=merged)
    return x_new, res


def layer_bwd(dxn, res, ms, W, P, cos, sin, B, L, n_ctx, tag):
    nb = L // TOKEN_BLOCK
    r3 = lambda a: a.reshape(B, L, a.shape[-1])
    r2 = lambda a: a.reshape(B * L, a.shape[-1])
    z = res["z"]
    ys = res["ys"]
    wbs = [W["bm"], W["bp"], W["bg"]]
    dy0, dy1, dy2, dzm, dout, dp0, dp1, dp2, dgate, dpost = merge_post_bwd(dxn, res["out"], ys, z["m"], wbs, W["out"], P["post"], ms, nb, tag + "merge_post_bwd")
    G = {"out": mm_dw(res["merged"], dout, tag + "dw_out"), "post": dpost}
    for nm, y, dp in zip(("bm", "bp", "bg"), ys, (dp0, dp1, dp2)):
        G[nm] = mm_dw(y, dp, tag + "dw_" + nm)
    dz = {"m": dzm}
    do, dz["gg"], G["gn"] = gla_out_bwd(r2(res["of"]), r2(res["ob"]), P["gn"], z["gg"], dy2, tag + "gla_out_bwd")
    grads = []
    for (qt, kt, ks, tot, ss), rev, nm in ((res["gla_f"], False, "gla_scan_f_bwd"), (res["gla_b"], True, "gla_scan_b_bwd")):
        grads.append(gla_scan_bwd(r3(qt), r3(kt), r3(ks), r3(z["gv"]), r3(tot), ss, r3(do), n_ctx, rev, tag + nm))
    gf = [r2(a) for a in grads[0]]
    gb = [r2(a) for a in grads[1]]
    dz["lr"], dz["gq"], dz["gk"], G["af"], G["ab"], G["baf"], G["bab"] = gla_prep_bwd(
        z["lr"], z["gq"], z["gk"], W["af"], W["ab"], P["baf"], P["bab"], gf[:3] + gf[4:], gb[:3] + gb[4:], tag + "gla_prep_bwd")
    dz["gv"] = add_cast(gf[3], gb[3], tag + "gla_dv")
    dpx, dpg, G["pw"], G["ps"] = pool_bwd(r3(z["px"]), r3(z["pg"]), P["pw"], P["ps"], r3(dy1), n_ctx, tag + "pool_bwd")
    dz["px"], dz["pg"] = r2(dpx), r2(dpg)
    dqn, dqr, dkv, dkr, dzmg = attention_bwd(r3(res["qn"]), r3(res["qr"]), r3(res["kv"]), r3(res["kr"]), r3(z["mg"]), res["ya"], r3(dy0), n_ctx, tag + "attention_bwd")
    dz["mg"] = r2(dzmg)
    dz["a"], G["qn"], G["qr"], G["kv"], G["qg"], G["kvg"] = mla_prep_bwd(
        r2(dqn), r2(dqr), r2(dkv), r2(dkr), z["a"], P["qg"], P["kvg"], W["qn"], W["qr"], W["kv"], cos, sin, nb, tag + "mla_prep_bwd")
    names = [g[0] for g in _IN_GROUPS]
    dh = mm_dx([dz[n] for n in names], [W["in_" + n] for n in names], tag + "in_proj_dx")
    for n in names:
        G["in_" + n] = mm_dw(res["h"], dz[n], tag + "dw_in_" + n)
    dx, dshift, dscale, G["pre"] = norm_mod_bwd(dh, res["x2"], P["pre"], ms, dxn, nb, tag + "norm_mod_bwd")
    dms = jnp.concatenate([dshift, dscale, dgate], axis=-1)
    return dx, G, dms


def add_cast(a, b, name):
    T, n = a.shape

    def body(a_ref, b_ref, o_ref):
        o_ref[...] = (a_ref[...] + b_ref[...]).astype(BF16)

    tok = pl.BlockSpec((TOKEN_BLOCK, n), lambda i: (i, 0))
    return pl.pallas_call(body, name=name, grid=(T // TOKEN_BLOCK,), in_specs=[tok, tok], out_specs=tok,
                          out_shape=jax.ShapeDtypeStruct((T, n), BF16), compiler_params=_params(("parallel",)))(a, b)


def layer_grads_natural(G):
    parts = {}
    for nm, off, n, npad in _IN_GROUPS:
        parts[off] = G["in_" + nm][:, :n]
    w_in = jnp.concatenate([parts[o] for o in sorted(parts)], axis=1)
    gqn = G["qn"].reshape(MLA_Q_RANK, MLA_HEADS, LANES)[:, :, :MLA_NOPE]
    gqr = G["qr"].reshape(MLA_Q_RANK, MLA_HEADS, LANES)[:, :, :MLA_ROPE]
    w_uq = jnp.concatenate([gqn, gqr], axis=-1).reshape(MLA_Q_RANK, MLA_HEADS * (MLA_NOPE + MLA_ROPE))
    return dict(w_in=w_in, mla_w_uq=w_uq, mla_w_ukv=G["kv"], gla_af_w2=G["af"][:GLA_GATE_RANK], gla_ab_w2=G["ab"][GLA_GATE_RANK:2 * GLA_GATE_RANK],
                w_branch_mla=G["bm"], w_branch_pool=G["bp"], w_branch_gla=G["bg"], w_out=G["out"],
                pre_norm=G["pre"][0], post_norm=G["post"][0], mla_q_norm=G["qg"][0], mla_kv_norm=G["kvg"][0], pool_w=G["pw"], pool_scale=G["ps"][0],
                gla_af_b=G["baf"][0], gla_ab_b=G["bab"][0], gla_norm=G["gn"][0])


def local_step(x, c, ctx, c_ctx, full, loss_target):
    B, S, _ = x.shape
    n_ctx = ctx.shape[1]
    L = n_ctx + S
    nb = L // TOKEN_BLOCK
    depth = full["w_in"].shape[0]
    cos, sin = rope_tables(L, n_ctx)
    x2 = jnp.concatenate([ctx, x], axis=1).reshape(B * L, D_MODEL)
    a8 = jnp.zeros((8, D_MODEL), F32).at[:B].set(c).at[B].set(c_ctx)
    Ws, Ps, mss, ress = [], [], [], []
    for l in range(depth):
        tag = f"l{l}_"
        W = layer_weights(full["w_in"][l], full["mla_w_uq"][l], full["mla_w_ukv"][l], full["gla_af_w2"][l], full["gla_ab_w2"][l],
                          full["w_branch_mla"][l], full["w_branch_pool"][l], full["w_branch_gla"][l], full["w_out"][l])
        P = dict(pre=full["pre_norm"][l][None], post=full["post_norm"][l][None], qg=full["mla_q_norm"][l][None], kvg=full["mla_kv_norm"][l][None],
                 pw=full["pool_w"][l].astype(BF16), ps=full["pool_scale"][l][None], baf=full["gla_af_b"][l][None], bab=full["gla_ab_b"][l][None],
                 gn=full["gla_norm"][l][None])
        mod8 = mod_fwd(a8, full["mod_w"][l], full["mod_b"][l][None], tag + "mod")
        ms = jnp.stack([jnp.broadcast_to(mod8[B], (B, 3 * D_MODEL)), mod8[:B]], axis=1).reshape(2 * B, 1, 3 * D_MODEL)
        x2, res = layer_fwd(x2, ms, W, P, cos, sin, B, L, n_ctx, tag)
        Ws.append(W), Ps.append(P), mss.append(ms), ress.append(res)
    dx, loss = loss_head(x2, loss_target.reshape(B * S, D_MODEL), nb, "loss_head")
    grads = [None] * depth
    da8 = None
    for l in reversed(range(depth)):
        tag = f"l{l}_"
        dx, G, dms = layer_bwd(dx, ress[l], mss[l], Ws[l], Ps[l], cos, sin, B, L, n_ctx, tag)
        dms = dms.reshape(B, 2, 3 * D_MODEL)
        dz8 = jnp.zeros((8, 3 * D_MODEL), F32).at[:B].set(dms[:, 1]).at[B].set(jnp.sum(dms[:, 0], axis=0))
        g_mod_w, g_mod_b, da = mod_bwd(a8, full["mod_w"][l], dz8, tag + "mod_bwd")
        da8 = da if da8 is None else da8 + da
        g = layer_grads_natural(G)
        g["mod_w"], g["mod_b"] = g_mod_w, g_mod_b[0]
        grads[l] = g
    gstack = {k: jnp.stack([grads[l][k] for l in range(depth)]) for k in grads[0]}
    gstack["c_ctx"] = da8[B]
    grad_x = dx.reshape(B, L, D_MODEL)[:, n_ctx:]
    return loss, grad_x, gstack


_MESH_ID = pl.DeviceIdType.MESH
_HBM = pl.BlockSpec(memory_space=pltpu.HBM)


def _me_and_peers():
    mx, my, mc = lax.axis_index("x"), lax.axis_index("y"), lax.axis_index("c")
    peers = []
    for k in range(1, N_DEV):
        px, py, pc = mx ^ ((k >> 2) & 1), my ^ ((k >> 1) & 1), mc ^ (k & 1)
        peers.append(((px, py, pc), 4 * px + 2 * py + pc))
    return 4 * mx + 2 * my + mc, peers


def _comm_scratch(n):
    return [pltpu.SemaphoreType.DMA((n * (N_DEV - 1),)), pltpu.SemaphoreType.DMA((n * (N_DEV - 1),)), pltpu.SemaphoreType.DMA((n,))]


def gather_blocks(xs, name):
    n = len(xs)
    K = N_DEV - 1

    def body(*refs):
        x_refs, o_refs = refs[:n], refs[n:2 * n]
        send_sems, recv_sems, local_sems = refs[2 * n:]
        mx, my, mc = lax.axis_index("x"), lax.axis_index("y"), lax.axis_index("c")
        me, sibling = (mx, my, mc), (mx, my, 1 - mc)
        chips = [(1 - mx, my), (mx, 1 - my), (1 - mx, 1 - my)]

        def slot(px, py, pc):
            return 4 * px + 2 * py + pc

        def copy(i, k, block, to, src=None):
            dst = o_refs[i].at[slot(*block)]
            return pltpu.make_async_remote_copy(src_ref=dst if src is None else src, dst_ref=dst, send_sem=send_sems.at[K * i + k],
                                                recv_sem=recv_sems.at[K * i + k], device_id=to, device_id_type=_MESH_ID)

        mine, first, passed = [], [], []
        for i in range(n):
            mine.append(pltpu.make_async_copy(x_refs[i], o_refs[i].at[slot(*me)], local_sems.at[i]))
            first.append(copy(i, 0, me, sibling, src=x_refs[i]))
            first += [copy(i, 1 + j, me, (*chip, mc), src=x_refs[i]) for j, chip in enumerate(chips)]
        for cp in mine + first:
            cp.start()
        for j, chip in enumerate(chips):
            for i in range(n):
                copy(i, 1 + j, (*chip, mc), me).wait_recv()
                passed.append(copy(i, 4 + j, (*chip, mc), sibling))
                passed[-1].start()
        for i in range(n):
            copy(i, 0, sibling, me).wait_recv()
            for j, chip in enumerate(chips):
                copy(i, 4 + j, (*chip, 1 - mc), me).wait_recv()
        for cp in first + passed:
            cp.wait_send()
        for cp in mine:
            cp.wait()

    return pl.pallas_call(
        body, name=name, in_specs=[_HBM] * n, out_specs=[_HBM] * n,
        out_shape=[jax.ShapeDtypeStruct((N_DEV,) + x.shape, x.dtype) for x in xs],
        scratch_shapes=_comm_scratch(n),
    )(*xs)


def scatter_blocks(xs, name):
    n = len(xs)
    K = N_DEV - 1

    def body(*refs):
        x_refs, o_refs = refs[:n], refs[n:2 * n]
        send_sems, recv_sems, local_sems = refs[2 * n:]
        me, peers = _me_and_peers()

        def copy(i, k, src_slot, dst_slot, to):
            return pltpu.make_async_remote_copy(src_ref=x_refs[i].at[src_slot], dst_ref=o_refs[i].at[dst_slot], send_sem=send_sems.at[K * i + k],
                                                recv_sem=recv_sems.at[K * i + k], device_id=to, device_id_type=_MESH_ID)

        mine = [pltpu.make_async_copy(x_refs[i].at[me], o_refs[i].at[me], local_sems.at[i]) for i in range(n)]
        sends = [copy(i, k, slot, me, peer) for k, (peer, slot) in enumerate(peers) for i in range(n)]
        for cp in mine + sends:
            cp.start()
        for k, (peer, slot) in enumerate(peers):
            for i in range(n):
                copy(i, k, slot, slot, peer).wait_recv()
        for cp in sends:
            cp.wait_send()
        for cp in mine:
            cp.wait()

    return pl.pallas_call(
        body, name=name, in_specs=[_HBM] * n, out_specs=[_HBM] * n,
        out_shape=[jax.ShapeDtypeStruct(x.shape, x.dtype) for x in xs],
        scratch_shapes=_comm_scratch(n),
    )(*xs)


def reduce_adamw(slots, w, m, v, name, tr=256):
    R, C = w.shape
    tr = min(tr, R)
    c1 = 1.0 / (1.0 - ADAM_B1 ** ADAM_STEP)
    c2 = 1.0 / (1.0 - ADAM_B2 ** ADAM_STEP)

    def body(s_ref, w_ref, m_ref, v_ref, g_ref, d_ref, nm_ref, nv_ref):
        g = s_ref[0].astype(F32)
        for s in range(1, N_DEV):
            g = g + s_ref[s].astype(F32)
        nm = ADAM_B1 * m_ref[...] + (1.0 - ADAM_B1) * g
        nv = ADAM_B2 * v_ref[...] + (1.0 - ADAM_B2) * (g * g)
        g_ref[...] = g
        nm_ref[...] = nm
        nv_ref[...] = nv
        d_ref[...] = -ADAM_LR * ((nm * c1) / (jnp.sqrt(nv * c2) + ADAM_EPS) + ADAM_WD * w_ref[...])

    blk = pl.BlockSpec((tr, C), lambda i: (i, 0))
    return pl.pallas_call(
        body, name=name, grid=(R // tr,),
        in_specs=[pl.BlockSpec((N_DEV, tr, C), lambda i: (0, i, 0)), blk, blk, blk], out_specs=[blk] * 4,
        out_shape=[jax.ShapeDtypeStruct((R, C), F32)] * 4,
        compiler_params=_params(("parallel",), VMEM_LIMIT),
    )(slots, w, m, v)


ARG_WEIGHTS = ("c_ctx", "mod_w", "mod_b", "pre_norm", "post_norm", "w_in", "mla_q_norm", "mla_w_uq", "mla_kv_norm", "mla_w_ukv", "pool_w",
               "pool_scale", "gla_af_w2", "gla_af_b", "gla_ab_w2", "gla_ab_b", "gla_norm", "w_branch_mla", "w_branch_pool", "w_branch_gla", "w_out")
SHARDED = ("mod_w", "w_in", "mla_w_uq", "mla_w_ukv", "gla_af_w2", "gla_ab_w2", "w_branch_mla", "w_branch_pool", "w_branch_gla", "w_out")
ROW_SHARDED = ("w_out",)
REPLICATED = tuple(n for n in ARG_WEIGHTS if n not in SHARDED)
PACK_ROWS = 512


def _pack(parts, dtype):
    flat = jnp.concatenate([p.astype(dtype).reshape(-1) for p in parts])
    n = flat.shape[0]
    total = -(-n // (PACK_ROWS * LANES)) * (PACK_ROWS * LANES)
    return jnp.pad(flat, (0, total - n)).reshape(total // LANES, LANES)


def _unpack(buf, shapes):
    flat = buf.reshape(-1)
    out, off = [], 0
    for shp in shapes:
        n = math.prod(shp)
        out.append(flat[off:off + n].reshape(shp))
        off += n
    return out


def _gathered_to_full(g, name, shard_shape):
    depth, r, cs = shard_shape
    if name in ROW_SHARDED:
        return g.transpose(1, 0, 2, 3).reshape(depth, N_DEV * r, cs)
    return g.transpose(1, 2, 0, 3).reshape(depth, r, N_DEV * cs)


def _full_to_slots(w, name, shard_shape):
    depth, r, cs = shard_shape
    if name in ROW_SHARDED:
        return w.reshape(depth, N_DEV, r, cs).transpose(1, 0, 2, 3)
    return w.reshape(depth, r, N_DEV, cs).transpose(2, 0, 1, 3)


def kernel(x, c, ctx, c_ctx, mod_w, mod_b, pre_norm, post_norm, w_in, mla_q_norm, mla_w_uq, mla_kv_norm, mla_w_ukv, pool_w, pool_scale, gla_af_w2, gla_af_b, gla_ab_w2, gla_ab_b, gla_norm, w_branch_mla, w_branch_pool, w_branch_gla, w_out, loss_target, m_c_ctx, m_mod_w, m_mod_b, m_pre_norm, m_post_norm, m_w_in, m_mla_q_norm, m_mla_w_uq, m_mla_kv_norm, m_mla_w_ukv, m_pool_w, m_pool_scale, m_gla_af_w2, m_gla_af_b, m_gla_ab_w2, m_gla_ab_b, m_gla_norm, m_w_branch_mla, m_w_branch_pool, m_w_branch_gla, m_w_out, v_c_ctx, v_mod_w, v_mod_b, v_pre_norm, v_post_norm, v_w_in, v_mla_q_norm, v_mla_w_uq, v_mla_kv_norm, v_mla_w_ukv, v_pool_w, v_pool_scale, v_gla_af_w2, v_gla_af_b, v_gla_ab_w2, v_gla_ab_b, v_gla_norm, v_w_branch_mla, v_w_branch_pool, v_w_branch_gla, v_w_out):
    local = dict(locals())
    wts = {n: local[n] for n in ARG_WEIGHTS}
    mom1 = {n: local["m_" + n] for n in ARG_WEIGHTS}
    mom2 = {n: local["v_" + n] for n in ARG_WEIGHTS}
    shard_shapes = [wts[n].shape for n in SHARDED]
    rep_shapes = [wts[n].shape for n in REPLICATED]
    kinds = ("grad", "delta", "new_m", "new_v")

    gathered = gather_blocks([wts[n].astype(BF16) for n in SHARDED], "gather_weights")
    full = {n: wts[n] for n in REPLICATED}
    for n, gw, shp in zip(SHARDED, gathered, shard_shapes):
        full[n] = _gathered_to_full(gw, n, shp)

    loss, grad_x, g = local_step(x, c, ctx, c_ctx, full, loss_target)

    slots = [_full_to_slots(g[n], n, shp).astype(BF16) for n, shp in zip(SHARDED, shard_shapes)]
    arrived = scatter_blocks(slots, "exchange_grads")
    res = {kind: {} for kind in kinds}
    for n, a, shp in zip(SHARDED, arrived, shard_shapes):
        flat = (shp[0] * shp[1], shp[2])
        outs = reduce_adamw(a.reshape((N_DEV,) + flat), wts[n].reshape(flat), mom1[n].reshape(flat), mom2[n].reshape(flat), "adamw_" + n)
        for kind, o in zip(kinds, outs):
            res[kind][n] = o.reshape(shp)

    (arrived,) = gather_blocks([_pack([g[n] for n in REPLICATED], F32)], "gather_small_grads")
    outs = reduce_adamw(arrived, _pack([wts[n] for n in REPLICATED], F32), _pack([mom1[n] for n in REPLICATED], F32),
                        _pack([mom2[n] for n in REPLICATED], F32), "adamw_replicated")
    for kind, o in zip(("grad", "delta", "new_m", "new_v"), outs):
        res[kind].update(zip(REPLICATED, _unpack(o, rep_shapes)))

    loss = lax.psum(loss[0, 0], ("x", "y", "c"))
    return (loss, grad_x, *[res[kind][n] for kind in ("grad", "delta", "new_m", "new_v") for n in ARG_WEIGHTS])
```

```python
import functools
import math

import jax
import jax.numpy as jnp
import numpy as np
from jax import lax
from jax.experimental import pallas as pl
from jax.experimental.pallas import tpu as pltpu

F32 = jnp.float32
BF16 = jnp.bfloat16

D_MODEL = 1024
NORM_EPS = 1e-6
GRID_W = 64
MLA_HEADS, MLA_Q_RANK, MLA_KV_RANK, MLA_NOPE, MLA_ROPE, MLA_V = 8, 256, 128, 64, 32, 64
MLA_WIDTH = MLA_HEADS * MLA_V
ROPE_BASE = 10000.0
ATT_SCALE = (MLA_NOPE + MLA_ROPE) ** -0.5
POOL_WINDOWS = (2, 4, 8, 16)
POOL_WIDTH, POOL_GROUP = 512, 128
GLA_HEADS, GLA_DK, GLA_DV = 4, 64, 128
GLA_KW, GLA_WIDTH = GLA_HEADS * GLA_DK, GLA_HEADS * GLA_DV
GLA_GATE_RANK, GLA_TAU, GLA_CHUNK = 16, 16.0, 64
IN_SIZES = (256, 128, 32, 512, 512, 512, 256, 256, 512, 16, 16, 512, 3 * D_MODEL)
ADAM_LR, ADAM_B1, ADAM_B2, ADAM_EPS, ADAM_WD, ADAM_STEP = 0.001, 0.9, 0.999, 1e-08, 0.01, 10
N_DEV = 8

LANES = 128
TOKEN_BLOCK = 256
VMEM_LIMIT = 48 * 1024 * 1024
NEG_BIG = -1e30

_NT = (((1,), (1,)), ((), ()))
_TN = (((0,), (0,)), ((), ()))


def _dot(a, b):
    return jnp.dot(a, b, preferred_element_type=F32)


def _dot_nt(a, b):
    return lax.dot_general(a, b, _NT, preferred_element_type=F32)


def _dot_tn(a, b):
    return lax.dot_general(a, b, _TN, preferred_element_type=F32)


def _params(sem=None, vmem=None):
    kw = {}
    if sem is not None:
        kw["dimension_semantics"] = sem
    if vmem is not None:
        kw["vmem_limit_bytes"] = vmem
    return pltpu.CompilerParams(**kw)


def _full(shape):
    n = len(shape)
    return pl.BlockSpec(shape, lambda *_: (0,) * n)


def _sig(x):
    return 1.0 / (1.0 + jnp.exp(-x))


def _silu_and_grad(x):
    s = _sig(x)
    return x * s, s * (1.0 + x * (1.0 - s))


def _acc(ref, val, first):
    @pl.when(first)
    def _():
        ref[...] = val

    @pl.when(jnp.logical_not(first))
    def _():
        ref[...] += val


def mm_multi(a, ws, dtypes, name, tm=TOKEN_BLOCK):
    M, K = a.shape
    nw = len(ws)

    def body(a_ref, *refs):
        av = a_ref[...]
        for w_ref, o_ref in zip(refs[:nw], refs[nw:]):
            o_ref[...] = _dot(av, w_ref[...]).astype(o_ref.dtype)

    return pl.pallas_call(
        body, name=name, grid=(M // tm,),
        in_specs=[pl.BlockSpec((tm, K), lambda i: (i, 0))] + [_full(w.shape) for w in ws],
        out_specs=[pl.BlockSpec((tm, w.shape[1]), lambda i: (i, 0)) for w in ws],
        out_shape=[jax.ShapeDtypeStruct((M, w.shape[1]), dt) for w, dt in zip(ws, dtypes)],
        compiler_params=_params(("parallel",), VMEM_LIMIT),
    )(a, *ws)


def mm_dx(dzs, ws, name, tm=TOKEN_BLOCK):
    M = dzs[0].shape[0]
    K = ws[0].shape[0]
    nw = len(ws)

    def body(*refs):
        o_ref = refs[-1]
        acc = None
        for dz_ref, w_ref in zip(refs[:nw], refs[nw:2 * nw]):
            t = _dot_nt(dz_ref[...], w_ref[...])
            acc = t if acc is None else acc + t
        o_ref[...] = acc

    return pl.pallas_call(
        body, name=name, grid=(M // tm,),
        in_specs=[pl.BlockSpec((tm, dz.shape[1]), lambda i: (i, 0)) for dz in dzs] + [_full(w.shape) for w in ws],
        out_specs=pl.BlockSpec((tm, K), lambda i: (i, 0)),
        out_shape=jax.ShapeDtypeStruct((M, K), F32),
        compiler_params=_params(("parallel",), VMEM_LIMIT),
    )(*dzs, *ws)


def mm_dw(a, dz, name, tn=1024):
    M, K = a.shape
    n = dz.shape[1]
    tn = min(tn, n)
    tk = next(t for t in (3072, 1536, 1024, 512, TOKEN_BLOCK) if M % t == 0)

    def body(a_ref, dz_ref, o_ref):
        _acc(o_ref, _dot_tn(a_ref[...], dz_ref[...]), pl.program_id(1) == 0)

    return pl.pallas_call(
        body, name=name, grid=(n // tn, M // tk),
        in_specs=[pl.BlockSpec((tk, K), lambda j, k: (k, 0)), pl.BlockSpec((tk, tn), lambda j, k: (k, j))],
        out_specs=pl.BlockSpec((K, tn), lambda j, k: (0, j)),
        out_shape=jax.ShapeDtypeStruct((K, n), F32),
        compiler_params=_params(("parallel", "arbitrary"), VMEM_LIMIT),
    )(a, dz)


def mod_fwd(a8, w, b, name):
    tn = D_MODEL

    def body(a_ref, w_ref, b_ref, o_ref):
        a = a_ref[...]
        o_ref[...] = _dot((a * _sig(a)).astype(BF16), w_ref[...]) + b_ref[...]

    return pl.pallas_call(
        body, name=name, grid=(3,),
        in_specs=[_full(a8.shape), pl.BlockSpec((D_MODEL, tn), lambda j: (0, j)), pl.BlockSpec((1, tn), lambda j: (0, j))],
        out_specs=pl.BlockSpec((8, tn), lambda j: (0, j)),
        out_shape=jax.ShapeDtypeStruct((8, 3 * D_MODEL), F32),
        compiler_params=_params(("parallel",)),
    )(a8, w, b)


def mod_bwd(a8, w, dz8, name):
    tn = D_MODEL

    def body(a_ref, w_ref, dz_ref, dw_ref, db_ref, da_ref):
        a = a_ref[...]
        sa, dsa = _silu_and_grad(a)
        dz = dz_ref[...]
        dw_ref[...] = _dot_tn(sa.astype(BF16), dz.astype(BF16))
        db_ref[...] = jnp.sum(dz, axis=0, keepdims=True)
        _acc(da_ref, _dot_nt(dz.astype(BF16), w_ref[...]) * dsa, pl.program_id(0) == 0)

    return pl.pallas_call(
        body, name=name, grid=(3,),
        in_specs=[_full(a8.shape), pl.BlockSpec((D_MODEL, tn), lambda j: (0, j)), pl.BlockSpec((8, tn), lambda j: (0, j))],
        out_specs=[pl.BlockSpec((D_MODEL, tn), lambda j: (0, j)), pl.BlockSpec((1, tn), lambda j: (0, j)), _full((8, D_MODEL))],
        out_shape=[jax.ShapeDtypeStruct((D_MODEL, 3 * D_MODEL), F32), jax.ShapeDtypeStruct((1, 3 * D_MODEL), F32),
                   jax.ShapeDtypeStruct((8, D_MODEL), F32)],
        compiler_params=_params(("arbitrary",)),
    )(a8, w, dz8)


def _mod_row(nb):
    return lambda i: 2 * (i // nb) + jnp.minimum(i % nb, 1)


def _mod_spec(nb, part):
    row = _mod_row(nb)
    return pl.BlockSpec((1, 1, D_MODEL), lambda i: (row(i), 0, part))


def norm_mod_fwd(x2, g, ms, nb, name):
    T = x2.shape[0]

    def body(x_ref, g_ref, sh_ref, sc_ref, h_ref):
        x = x_ref[...]
        r = lax.rsqrt(jnp.mean(x * x, axis=-1, keepdims=True) + NORM_EPS)
        h_ref[...] = ((x * r) * g_ref[...] * (1.0 + sc_ref[0]) + sh_ref[0]).astype(BF16)

    return pl.pallas_call(
        body, name=name, grid=(T // TOKEN_BLOCK,),
        in_specs=[pl.BlockSpec((TOKEN_BLOCK, D_MODEL), lambda i: (i, 0)), _full((1, D_MODEL)), _mod_spec(nb, 0), _mod_spec(nb, 1)],
        out_specs=pl.BlockSpec((TOKEN_BLOCK, D_MODEL), lambda i: (i, 0)),
        out_shape=jax.ShapeDtypeStruct((T, D_MODEL), BF16),
        compiler_params=_params(("parallel",)),
    )(x2, g, ms, ms)


def norm_mod_bwd(dh, x2, g, ms, dxres, nb, name):
    T = x2.shape[0]
    nrow = ms.shape[0]
    row = _mod_row(nb)

    def body(dh_ref, x_ref, g_ref, sc_ref, dxr_ref, dx_ref, dsh_ref, dsc_ref, dg_ref):
        i = pl.program_id(0)
        t = i % nb
        x = x_ref[...]
        dh = dh_ref[...]
        g = g_ref[...]
        r = lax.rsqrt(jnp.mean(x * x, axis=-1, keepdims=True) + NORM_EPS)
        xn = x * r
        du = dh * (1.0 + sc_ref[0])
        dyg = du * g
        dx_ref[...] = dxr_ref[...] + r * (dyg - xn * jnp.mean(dyg * xn, axis=-1, keepdims=True))
        first = t <= 1
        _acc(dsh_ref.at[0], jnp.sum(dh, axis=0, keepdims=True), first)
        _acc(dsc_ref.at[0], jnp.sum(dh * xn * g, axis=0, keepdims=True), first)
        _acc(dg_ref, jnp.sum(du * xn, axis=0, keepdims=True), i == 0)

    tok = pl.BlockSpec((TOKEN_BLOCK, D_MODEL), lambda i: (i, 0))
    acc = pl.BlockSpec((1, 1, D_MODEL), lambda i: (row(i), 0, 0))
    return pl.pallas_call(
        body, name=name, grid=(T // TOKEN_BLOCK,),
        in_specs=[tok, tok, _full((1, D_MODEL)), _mod_spec(nb, 1), tok],
        out_specs=[tok, acc, acc, _full((1, D_MODEL))],
        out_shape=[jax.ShapeDtypeStruct((T, D_MODEL), F32), jax.ShapeDtypeStruct((nrow, 1, D_MODEL), F32),
                   jax.ShapeDtypeStruct((nrow, 1, D_MODEL), F32), jax.ShapeDtypeStruct((1, D_MODEL), F32)],
        compiler_params=_params(("arbitrary",)),
    )(dh, x2, g, ms, dxres)


def _rot(x):
    lane = lax.broadcasted_iota(jnp.int32, x.shape, 1)
    return jnp.where((lane % 16) < 8, -pltpu.roll(x, LANES - 8, 1), pltpu.roll(x, 8, 1))


def _rope(x, cos, sin):
    return x * cos + _rot(x) * sin


def _rope_t(dy, cos, sin):
    return dy * cos - _rot(dy * sin)


def _rms_rows(x):
    r = lax.rsqrt(jnp.mean(x * x, axis=-1, keepdims=True) + NORM_EPS)
    return x * r, r


def _rms_rows_bwd(dyg, xn, r):
    return r * (dyg - xn * jnp.mean(dyg * xn, axis=-1, keepdims=True))


def mla_prep_fwd(za, qg, kvg, wqn, wqr, wkv, cos, sin, nb, name):
    T = za.shape[0]
    W = MLA_HEADS * LANES

    def body(z_ref, qg_ref, kvg_ref, wqn_ref, wqr_ref, wkv_ref, cos_ref, sin_ref, qn_ref, qr_ref, kv_ref, kr_ref):
        z = z_ref[...]
        cos = cos_ref[...]
        sin = sin_ref[...]
        xq, _ = _rms_rows(z[:, 0:256])
        qn = (xq * qg_ref[...]).astype(BF16)
        qn_ref[...] = (_dot(qn, wqn_ref[...]) * ATT_SCALE).astype(BF16)
        qr = _dot(qn, wqr_ref[...])
        for h in range(MLA_HEADS):
            sl = slice(LANES * h, LANES * (h + 1))
            qr_ref[:, sl] = (_rope(qr[:, sl], cos, sin) * ATT_SCALE).astype(BF16)
        xkv, _ = _rms_rows(z[:, 256:384])
        kv_ref[...] = _dot((xkv * kvg_ref[...]).astype(BF16), wkv_ref[...]).astype(BF16)
        kr_ref[...] = _rope(z[:, 384:512], cos, sin).astype(BF16)

    tok = lambda n: pl.BlockSpec((TOKEN_BLOCK, n), lambda i: (i, 0))
    pos = pl.BlockSpec((TOKEN_BLOCK, LANES), lambda i: (i % nb, 0))
    return pl.pallas_call(
        body, name=name, grid=(T // TOKEN_BLOCK,),
        in_specs=[tok(512), _full(qg.shape), _full(kvg.shape), _full(wqn.shape), _full(wqr.shape), _full(wkv.shape), pos, pos],
        out_specs=[tok(W), tok(W), tok(W), tok(LANES)],
        out_shape=[jax.ShapeDtypeStruct((T, W), BF16)] * 3 + [jax.ShapeDtypeStruct((T, LANES), BF16)],
        compiler_params=_params(("parallel",)),
    )(za, qg, kvg, wqn, wqr, wkv, cos, sin)


def mla_prep_bwd(dqn, dqr, dkv, dkr, za, qg, kvg, wqn, wqr, wkv, cos, sin, nb, name):
    T = za.shape[0]
    W = MLA_HEADS * LANES

    def body(dqn_ref, dqr_ref, dkv_ref, dkr_ref, z_ref, qg_ref, kvg_ref, wqn_ref, wqr_ref, wkv_ref, cos_ref, sin_ref,
             dz_ref, dwqn_ref, dwqr_ref, dwkv_ref, dqg_ref, dkvg_ref):
        first = pl.program_id(0) == 0
        z = z_ref[...]
        cos = cos_ref[...]
        sin = sin_ref[...]
        qg = qg_ref[...]
        kvg = kvg_ref[...]
        xq, rq = _rms_rows(z[:, 0:256])
        qn = (xq * qg).astype(BF16)
        a1 = (dqn_ref[...].astype(F32) * ATT_SCALE).astype(BF16)
        parts = []
        for h in range(MLA_HEADS):
            sl = slice(LANES * h, LANES * (h + 1))
            parts.append(_rope_t(dqr_ref[:, sl].astype(F32) * ATT_SCALE, cos, sin).astype(BF16))
        a2 = jnp.concatenate(parts, axis=1)
        dq = _dot_nt(a1, wqn_ref[...]) + _dot_nt(a2, wqr_ref[...])
        _acc(dwqn_ref, _dot_tn(qn, a1), first)
        _acc(dwqr_ref, _dot_tn(qn, a2), first)
        _acc(dqg_ref, jnp.sum(dq * xq, axis=0, keepdims=True), first)
        dz_ref[:, 0:256] = _rms_rows_bwd(dq * qg, xq, rq).astype(BF16)
        xkv, rkv = _rms_rows(z[:, 256:384])
        kvn = (xkv * kvg).astype(BF16)
        dkvb = dkv_ref[...].astype(BF16)
        dk = _dot_nt(dkvb, wkv_ref[...])
        _acc(dwkv_ref, _dot_tn(kvn, dkvb), first)
        _acc(dkvg_ref, jnp.sum(dk * xkv, axis=0, keepdims=True), first)
        dz_ref[:, 256:384] = _rms_rows_bwd(dk * kvg, xkv, rkv).astype(BF16)
        dz_ref[:, 384:512] = _rope_t(dkr_ref[...], cos, sin).astype(BF16)

    tok = lambda n: pl.BlockSpec((TOKEN_BLOCK, n), lambda i: (i, 0))
    pos = pl.BlockSpec((TOKEN_BLOCK, LANES), lambda i: (i % nb, 0))
    return pl.pallas_call(
        body, name=name, grid=(T // TOKEN_BLOCK,),
        in_specs=[tok(W), tok(W), tok(W), tok(LANES), tok(512), _full(qg.shape), _full(kvg.shape), _full(wqn.shape),
                  _full(wqr.shape), _full(wkv.shape), pos, pos],
        out_specs=[tok(512), _full(wqn.shape), _full(wqr.shape), _full(wkv.shape), _full(qg.shape), _full(kvg.shape)],
        out_shape=[jax.ShapeDtypeStruct((T, 512), BF16), jax.ShapeDtypeStruct(wqn.shape, F32), jax.ShapeDtypeStruct(wqr.shape, F32),
                   jax.ShapeDtypeStruct(wkv.shape, F32), jax.ShapeDtypeStruct(qg.shape, F32), jax.ShapeDtypeStruct(kvg.shape, F32)],
        compiler_params=_params(("arbitrary",)),
    )(dqn, dqr, dkv, dkr, za, qg, kvg, wqn, wqr, wkv, cos, sin)


def _att_qk(qn_ref, qr_ref, kv_ref, kr, j):
    sl = slice(LANES * j, LANES * (j + 1))
    q = jnp.concatenate([qn_ref[0, :, sl], qr_ref[0, :, sl]], axis=1)
    kvj = kv_ref[0, :, sl]
    k = jnp.concatenate([kvj, kr], axis=1)
    return q, k, kvj, _dot_nt(q, k)


def _att_specs(L, lk, q0):
    TQ, W2 = TOKEN_BLOCK, 2 * LANES
    qspec = pl.BlockSpec((1, TQ, W2), lambda b, h, i: (b, i + q0, h))
    kvspec = pl.BlockSpec((1, lk, W2), lambda b, h, i: (b, 0, h))
    krspec = pl.BlockSpec((1, lk, LANES), lambda b, h, i: (b, 0, 0))
    gspec = pl.BlockSpec((1, TQ, LANES), lambda b, h, i: (b, i + q0, h))
    lspec = pl.BlockSpec((1, 1, TQ, LANES), lambda b, h, i: (b, h, i + q0, 0))
    return qspec, kvspec, krspec, gspec, lspec


_ANY = pl.BlockSpec(memory_space=pl.ANY)


def attention_fwd(qn, qr, kv, kr, zg, n_ctx, name):
    B, L, _ = qn.shape
    TQ = TOKEN_BLOCK
    HP = MLA_HEADS // 2
    shapes = [jax.ShapeDtypeStruct((B, L, MLA_WIDTH), F32), jax.ShapeDtypeStruct((B, L, MLA_WIDTH), BF16),
              jax.ShapeDtypeStruct((B, HP, L, LANES), F32)]

    def body(qn_ref, qr_ref, kv_ref, kr_ref, g_ref, *rest):
        _fwd_step(qn_ref, qr_ref, kv_ref, kr_ref, g_ref, *rest[-3:])

    def body_ctx(qn_ref, qr_ref, kv_ref, kr_ref, g_ref, *rest):
        _fwd_step(qn_ref, qr_ref, kv_ref, kr_ref, g_ref, *rest[-3:])

    def _fwd_step(qn_ref, qr_ref, kv_ref, kr_ref, g_ref, ya_ref, ym_ref, lse_ref):
        kr_v = kr_ref[0]
        outs, lses = [], []
        for j in range(2):
            _, _, kvj, s = _att_qk(qn_ref, qr_ref, kv_ref, kr_v, j)
            m = jnp.max(s, axis=-1, keepdims=True)
            p = jnp.exp(s - m).astype(BF16)
            lane_k = lax.broadcasted_iota(jnp.int32, kvj.shape, 1)
            o = _dot(p, jnp.where(lane_k < MLA_V, jnp.ones_like(kvj), kvj))
            l = o[:, 0:1]
            outs.append(o / l)
            lses.append(m + jnp.log(l))
        lane = lax.broadcasted_iota(jnp.int32, outs[0].shape, 1)
        y = jnp.where(lane < MLA_V, pltpu.roll(outs[0], MLA_V, 1), outs[1])
        ya_ref[0] = y
        g = g_ref[0]
        ym_ref[0] = (y * g * _sig(g)).astype(BF16)
        lse_ref[0, 0] = jnp.where(lane < MLA_V, lses[0], lses[1])

    qspec, kvspec, krspec, gspec, lspec = _att_specs(L, L, 1)
    main = pl.pallas_call(
        body, name=name, grid=(B, HP, L // TQ - 1),
        in_specs=[qspec, qspec, kvspec, krspec, gspec], out_specs=[gspec, gspec, lspec], out_shape=shapes,
        compiler_params=_params(("parallel", "parallel", "parallel"), VMEM_LIMIT),
    )(qn, qr, kv, kr, zg)
    qspec, kvspec, krspec, gspec, lspec = _att_specs(L, n_ctx, 0)
    return pl.pallas_call(
        body_ctx, name=name + "_ctx", grid=(B, HP, 1),
        in_specs=[qspec, qspec, kvspec, krspec, gspec, _ANY, _ANY, _ANY], out_specs=[gspec, gspec, lspec], out_shape=shapes,
        input_output_aliases={5: 0, 6: 1, 7: 2},
        compiler_params=_params(("parallel", "parallel", "parallel"), VMEM_LIMIT),
    )(qn, qr, kv, kr, zg, *main)


def attention_bwd(qn, qr, kv, kr, zg, ya, lse, dym, n_ctx, name):
    B, L, _ = qn.shape
    TQ = TOKEN_BLOCK
    HP = MLA_HEADS // 2
    W = MLA_HEADS * LANES
    shapes = [jax.ShapeDtypeStruct((B, L, W), BF16), jax.ShapeDtypeStruct((B, L, W), BF16), jax.ShapeDtypeStruct((B, L, W), F32),
              jax.ShapeDtypeStruct((B, L, LANES), F32), jax.ShapeDtypeStruct((B, L, MLA_WIDTH), BF16)]

    def body(qn_ref, qr_ref, kv_ref, kr_ref, g_ref, ya_ref, lse_ref, dy_ref, dqn_ref, dqr_ref, dkv_ref, dkr_ref, dzg_ref):
        @pl.when(pl.program_id(2) == 0)
        def _():
            dkv_ref[...] = jnp.zeros_like(dkv_ref)

        @pl.when(jnp.logical_and(pl.program_id(2) == 0, pl.program_id(1) == 0))
        def _():
            dkr_ref[...] = jnp.zeros_like(dkr_ref)

        _bwd_step(qn_ref, qr_ref, kv_ref, kr_ref, g_ref, ya_ref, lse_ref, dy_ref, dqn_ref, dqr_ref, dkv_ref, dkr_ref, dzg_ref)

    def body_ctx(qn_ref, qr_ref, kv_ref, kr_ref, g_ref, ya_ref, lse_ref, dy_ref, dkv_in, dkr_in, a0, a1, a2,
                 dqn_ref, dqr_ref, dkv_ref, dkr_ref, dzg_ref):
        dkv_ref[...] = dkv_in[...]

        @pl.when(pl.program_id(1) == 0)
        def _():
            dkr_ref[...] = dkr_in[...]

        _bwd_step(qn_ref, qr_ref, kv_ref, kr_ref, g_ref, ya_ref, lse_ref, dy_ref, dqn_ref, dqr_ref, dkv_ref, dkr_ref, dzg_ref)

    def _bwd_step(qn_ref, qr_ref, kv_ref, kr_ref, g_ref, ya_ref, lse_ref, dy_ref, dqn_ref, dqr_ref, dkv_ref, dkr_ref, dzg_ref):
        g = g_ref[0]
        silu, dsilu = _silu_and_grad(g)
        dy = dy_ref[0]
        ya_v = ya_ref[0]
        dya = dy * silu
        dzg_ref[0] = (dy * ya_v * dsilu).astype(BF16)
        lane = lax.broadcasted_iota(jnp.int32, dya.shape, 1)
        hi = lane >= MLA_V
        d_out = [jnp.where(hi, pltpu.roll(dya, MLA_V, 1), 0.0), jnp.where(hi, dya, 0.0)]
        prod = dya * ya_v
        drow = [jnp.sum(jnp.where(hi, 0.0, prod), axis=-1, keepdims=True), jnp.sum(jnp.where(hi, prod, 0.0), axis=-1, keepdims=True)]
        lse_v = lse_ref[0, 0]
        kr_v = kr_ref[0]
        for j in range(2):
            sl = slice(LANES * j, LANES * (j + 1))
            q, k, kvj, s = _att_qk(qn_ref, qr_ref, kv_ref, kr_v, j)
            pn = jnp.exp(s - lse_v[:, MLA_V * j:MLA_V * j + 1])
            dob = d_out[j].astype(BF16)
            ds = (pn * (_dot_nt(dob, kvj) - drow[j])).astype(BF16)
            dq = _dot(ds, k)
            dqn_ref[0, :, sl] = jnp.where(hi, 0.0, dq[:, :LANES]).astype(BF16)
            dqr_ref[0, :, sl] = dq[:, LANES:].astype(BF16)
            dk = _dot_tn(ds, q)
            dkv_ref[0, :, sl] += dk[:, :LANES] + _dot_tn(pn.astype(BF16), dob)
            dkr_ref[0] += dk[:, LANES:]

    sem = _params(("parallel", "arbitrary", "arbitrary"), VMEM_LIMIT)
    qspec, kvspec, krspec, gspec, lspec = _att_specs(L, L, 1)
    main = pl.pallas_call(
        body, name=name, grid=(B, HP, L // TQ - 1),
        in_specs=[qspec, qspec, kvspec, krspec, gspec, gspec, lspec, gspec],
        out_specs=[qspec, qspec, kvspec, krspec, gspec], out_shape=shapes, compiler_params=sem,
    )(qn, qr, kv, kr, zg, ya, lse, dym)
    qspec, kvspec, krspec, gspec, lspec = _att_specs(L, n_ctx, 0)
    return pl.pallas_call(
        body_ctx, name=name + "_ctx", grid=(B, HP, 1),
        in_specs=[qspec, qspec, kvspec, krspec, gspec, gspec, lspec, gspec, kvspec, krspec, _ANY, _ANY, _ANY],
        out_specs=[qspec, qspec, kvspec, krspec, gspec], out_shape=shapes,
        input_output_aliases={8: 2, 9: 3, 10: 0, 11: 1, 12: 4}, compiler_params=sem,
    )(qn, qr, kv, kr, zg, ya, lse, dym, main[2], main[3], main[0], main[1], main[4])


def _seg_bounds(rows, n_ctx, L):
    in_ctx = rows < n_ctx
    return jnp.where(in_ctx, 0, n_ctx), jnp.where(in_ctx, n_ctx, L)


def _window_sum(u, w, rows, lo, hi, mirror):
    L = u.shape[0]
    offs = range(-w // 2 + 1, w // 2 + 1) if mirror else range(-w // 2, w // 2)
    acc = None
    for d in offs:
        if d == 0:
            t = u
        else:
            src = rows + d
            t = jnp.where(jnp.logical_and(src >= lo, src < hi), pltpu.roll(u, (-d) % L, 0), 0.0)
        acc = t if acc is None else acc + t
    return acc


def _window_count(w, rows, lo, hi):
    pos = rows - lo
    return (jnp.minimum(pos + w // 2, hi - lo) - jnp.maximum(pos - w // 2, 0)).astype(F32)


def pool_fwd(px, pg, pw, ps, n_ctx, name):
    B, L, _ = px.shape

    def body(px_ref, pg_ref, pw_ref, ps_ref, y_ref):
        rows = lax.broadcasted_iota(jnp.int32, (L, POOL_GROUP), 0)
        lo, hi = _seg_bounds(rows, n_ctx, L)
        for gi, w in enumerate(POOL_WINDOWS):
            sl = slice(POOL_GROUP * gi, POOL_GROUP * (gi + 1))
            u = px_ref[0, :, sl]
            pooled = _window_sum(u, w, rows, lo, hi, False) / _window_count(w, rows, lo, hi) - u
            mixed = _dot(pooled.astype(BF16), pw_ref[gi])
            g = pg_ref[0, :, sl]
            y_ref[0, :, sl] = (mixed * ps_ref[:, sl] * (g * _sig(g))).astype(BF16)

    tok = pl.BlockSpec((1, L, POOL_WIDTH), lambda b: (b, 0, 0))
    return pl.pallas_call(
        body, name=name, grid=(B,),
        in_specs=[tok, tok, _full(pw.shape), _full(ps.shape)],
        out_specs=tok, out_shape=jax.ShapeDtypeStruct((B, L, POOL_WIDTH), BF16),
        compiler_params=_params(("parallel",), VMEM_LIMIT),
    )(px, pg, pw, ps)


def pool_bwd(px, pg, pw, ps, dy, n_ctx, name):
    B, L, _ = px.shape

    def body(px_ref, pg_ref, pw_ref, ps_ref, dy_ref, dpx_ref, dpg_ref, dpw_ref, dps_ref):
        first = pl.program_id(0) == 0
        rows = lax.broadcasted_iota(jnp.int32, (L, POOL_GROUP), 0)
        lo, hi = _seg_bounds(rows, n_ctx, L)
        for gi, w in enumerate(POOL_WINDOWS):
            sl = slice(POOL_GROUP * gi, POOL_GROUP * (gi + 1))
            u = px_ref[0, :, sl]
            cnt = _window_count(w, rows, lo, hi)
            pooled = (_window_sum(u, w, rows, lo, hi, False) / cnt - u).astype(BF16)
            mixed = _dot(pooled, pw_ref[gi])
            silu, dsilu = _silu_and_grad(pg_ref[0, :, sl])
            sc = ps_ref[:, sl]
            dyv = dy_ref[0, :, sl]
            _acc(dps_ref.at[:, sl], jnp.sum(dyv * mixed * silu, axis=0, keepdims=True), first)
            dpg_ref[0, :, sl] = (dyv * mixed * sc * dsilu).astype(BF16)
            dmixed = (dyv * sc * silu).astype(BF16)
            _acc(dpw_ref.at[gi], _dot_tn(pooled, dmixed), first)
            dpooled = _dot_nt(dmixed, pw_ref[gi])
            dpx_ref[0, :, sl] = (_window_sum(dpooled / cnt, w, rows, lo, hi, True) - dpooled).astype(BF16)

    tok = pl.BlockSpec((1, L, POOL_WIDTH), lambda b: (b, 0, 0))
    return pl.pallas_call(
        body, name=name, grid=(B,),
        in_specs=[tok, tok, _full(pw.shape), _full(ps.shape), tok],
        out_specs=[tok, tok, _full(pw.shape), _full(ps.shape)],
        out_shape=[jax.ShapeDtypeStruct((B, L, POOL_WIDTH), BF16)] * 2 + [jax.ShapeDtypeStruct(pw.shape, F32), jax.ShapeDtypeStruct(ps.shape, F32)],
        compiler_params=_params(("arbitrary",), VMEM_LIMIT),
    )(px, pg, pw, ps, dy)


_SCAN_STEPS = (1, 2, 4, 8, 16, 32)


def _cum_fwd(x, r):
    for s in _SCAN_STEPS:
        x = x + jnp.where(r >= s, pltpu.roll(x, s, 0), 0.0)
    return x


def _cum_bwd(x, r):
    n = x.shape[0]
    for s in _SCAN_STEPS:
        x = x + jnp.where(r + s < GLA_CHUNK, pltpu.roll(x, n - s, 0), 0.0)
    return x


def _log_sigmoid(x):
    return jnp.minimum(x, 0.0) - jnp.log(1.0 + jnp.exp(-jnp.abs(x)))


def _gla_decays(lr, w_ref, b_ref, r, reverse):
    pre = _dot(lr, w_ref[...]) + b_ref[...]
    a = _log_sigmoid(pre) / GLA_TAU
    pf = _cum_fwd(a, r)
    sf = _cum_bwd(a, r)
    tot = pf + sf - a
    return pre, a, (sf if reverse else pf), tot


def gla_prep_fwd(zlr, zq, zk, waf, wab, baf, bab, name):
    T = zlr.shape[0]

    def body(lr_ref, q_ref, k_ref, waf_ref, wab_ref, baf_ref, bab_ref, qf_ref, kf_ref, ksf_ref, tf_ref, qb_ref, kb_ref, ksb_ref, tb_ref):
        r = lax.broadcasted_iota(jnp.int32, (TOKEN_BLOCK, GLA_KW), 0) % GLA_CHUNK
        lr = lr_ref[...].astype(BF16)
        q = q_ref[...] * GLA_DK ** -0.5
        k = k_ref[...]
        for rev, w_ref, b_ref, qo, ko, kso, to in ((False, waf_ref, baf_ref, qf_ref, kf_ref, ksf_ref, tf_ref),
                                                   (True, wab_ref, bab_ref, qb_ref, kb_ref, ksb_ref, tb_ref)):
            _, _, b, tot = _gla_decays(lr, w_ref, b_ref, r, rev)
            qo[...] = (q * jnp.exp(b)).astype(BF16)
            ko[...] = (k * jnp.exp(-b)).astype(BF16)
            kso[...] = (k * jnp.exp(tot - b)).astype(BF16)
            to[...] = tot

    tok = lambda n: pl.BlockSpec((TOKEN_BLOCK, n), lambda i: (i, 0))
    outs = [jax.ShapeDtypeStruct((T, GLA_KW), BF16)] * 3 + [jax.ShapeDtypeStruct((T, GLA_KW), F32)]
    return pl.pallas_call(
        body, name=name, grid=(T // TOKEN_BLOCK,),
        in_specs=[tok(LANES), tok(GLA_KW), tok(GLA_KW), _full(waf.shape), _full(wab.shape), _full(baf.shape), _full(bab.shape)],
        out_specs=[tok(GLA_KW)] * 8, out_shape=outs + outs,
        compiler_params=_params(("parallel",)),
    )(zlr, zq, zk, waf, wab, baf, bab)


def gla_prep_bwd(zlr, zq, zk, waf, wab, baf, bab, gf, gb, name):
    T = zlr.shape[0]

    def body(lr_ref, q_ref, k_ref, waf_ref, wab_ref, baf_ref, bab_ref, dqf, dkf, dksf, ddf, dqb, dkb, dksb, ddb,
             dlr_ref, dq_ref, dk_ref, dwaf_ref, dwab_ref, dbaf_ref, dbab_ref):
        first = pl.program_id(0) == 0
        r = lax.broadcasted_iota(jnp.int32, (TOKEN_BLOCK, GLA_KW), 0) % GLA_CHUNK
        lr = lr_ref[...].astype(BF16)
        q = q_ref[...] * GLA_DK ** -0.5
        k = k_ref[...]
        dq_tot = None
        dk_tot = None
        dlr = None
        for rev, w_ref, b_ref, dqt, dkt, dks, ddec, dw_ref, db_ref in (
                (False, waf_ref, baf_ref, dqf, dkf, dksf, ddf, dwaf_ref, dbaf_ref),
                (True, wab_ref, bab_ref, dqb, dkb, dksb, ddb, dwab_ref, dbab_ref)):
            pre, _, b, tot = _gla_decays(lr, w_ref, b_ref, r, rev)
            e1 = jnp.exp(b)
            e2 = jnp.exp(-b)
            e3 = jnp.exp(tot - b)
            dqt_v = dqt[...]
            dkt_v = dkt[...]
            dks_v = dks[...]
            dq = dqt_v * e1
            dk = dkt_v * e2 + dks_v * e3
            g3 = dks_v * (k * e3)
            d_b = dqt_v * (q * e1) - dkt_v * (k * e2) - g3
            d_tot = _cum_fwd(g3, r) + _cum_bwd(g3, r) - g3 + ddec[...] * jnp.exp(tot)
            da = (_cum_fwd(d_b, r) if rev else _cum_bwd(d_b, r)) + d_tot
            dpre = (da * (_sig(-pre) / GLA_TAU)).astype(BF16)
            t = _dot_nt(dpre, w_ref[...])
            dlr = t if dlr is None else dlr + t
            _acc(dw_ref, _dot_tn(lr, dpre), first)
            _acc(db_ref, jnp.sum(dpre.astype(F32), axis=0, keepdims=True), first)
            dq_tot = dq if dq_tot is None else dq_tot + dq
            dk_tot = dk if dk_tot is None else dk_tot + dk
        dlr_ref[...] = dlr.astype(BF16)
        dq_ref[...] = (dq_tot * GLA_DK ** -0.5).astype(BF16)
        dk_ref[...] = dk_tot.astype(BF16)

    tok = lambda n: pl.BlockSpec((TOKEN_BLOCK, n), lambda i: (i, 0))
    return pl.pallas_call(
        body, name=name, grid=(T // TOKEN_BLOCK,),
        in_specs=[tok(LANES), tok(GLA_KW), tok(GLA_KW), _full(waf.shape), _full(wab.shape), _full(baf.shape), _full(bab.shape)] + [tok(GLA_KW)] * 8,
        out_specs=[tok(LANES), tok(GLA_KW), tok(GLA_KW), _full(waf.shape), _full(wab.shape), _full(baf.shape), _full(bab.shape)],
        out_shape=[jax.ShapeDtypeStruct((T, LANES), BF16), jax.ShapeDtypeStruct((T, GLA_KW), BF16), jax.ShapeDtypeStruct((T, GLA_KW), BF16),
                   jax.ShapeDtypeStruct(waf.shape, F32), jax.ShapeDtypeStruct(wab.shape, F32), jax.ShapeDtypeStruct(baf.shape, F32),
                   jax.ShapeDtypeStruct(bab.shape, F32)],
        compiler_params=_params(("arbitrary",)),
    )(zlr, zq, zk, waf, wab, baf, bab, *gf, *gb)


def _chunk_order(nc, n_ctx_chunks, reverse):
    if not reverse:
        return lambda c: c
    return lambda c: jnp.where(c < n_ctx_chunks, n_ctx_chunks - 1 - c, nc + n_ctx_chunks - 1 - c)


def _head_mask(shape, h):
    lane = lax.broadcasted_iota(jnp.int32, shape, 1)
    return jnp.logical_and(lane >= GLA_DK * h, lane < GLA_DK * (h + 1))


def _tri_mask(reverse):
    ri = lax.broadcasted_iota(jnp.int32, (GLA_CHUNK, GLA_CHUNK), 0)
    ci = lax.broadcasted_iota(jnp.int32, (GLA_CHUNK, GLA_CHUNK), 1)
    return (ri <= ci) if reverse else (ri >= ci)


def gla_scan_fwd(dirs, v, n_ctx, name):
    B, L, _ = v.shape
    C = GLA_CHUNK
    nc = L // C
    orders = [_chunk_order(nc, n_ctx // C, rev) for rev in (False, True)]

    def body(qf, kf, ksf, tf, vf, qb, kb, ksb, tb, vb, of, ssf, ob, ssb, stf, stb):
        @pl.when(pl.program_id(1) == 0)
        def _():
            stf[...] = jnp.zeros_like(stf)
            stb[...] = jnp.zeros_like(stb)

        step(qf, kf, ksf, vf, tf, of, ssf, stf, False)
        step(qb, kb, ksb, vb, tb, ob, ssb, stb, True)

    def step(q_ref, k_ref, ks_ref, v_ref, tot_ref, o_ref, ss_ref, st, reverse):
        S = st[...]
        ss_ref[0, 0] = S
        Sb = S.astype(BF16)
        q = q_ref[0]
        k = k_ref[0]
        ksv = ks_ref[0]
        mask = _tri_mask(reverse)
        U = jnp.zeros_like(S)
        for h in range(GLA_HEADS):
            qm = jnp.where(_head_mask(q.shape, h), q, jnp.zeros_like(q))
            A = jnp.where(mask, _dot_nt(qm, k), 0.0)
            vh = v_ref[0, :, GLA_DV * h:GLA_DV * (h + 1)]
            o_ref[0, :, GLA_DV * h:GLA_DV * (h + 1)] = _dot(A.astype(BF16), vh.astype(BF16)) + _dot_nt(qm, Sb)
            U = U + jnp.where(_head_mask(S.shape, h), _dot(vh.T.astype(BF16), ksv), 0.0)
        st[...] = jnp.exp(tot_ref[0, 0:1, :]) * S + U

    in_specs, out_specs, out_shape = [], [], []
    for order in orders:
        tok = lambda n, order=order: pl.BlockSpec((1, C, n), lambda b, c: (b, order(c), 0))
        in_specs += [tok(GLA_KW), tok(GLA_KW), tok(GLA_KW), tok(GLA_KW), tok(GLA_WIDTH)]
        out_specs += [tok(GLA_WIDTH), pl.BlockSpec((1, 1, GLA_DV, GLA_KW), lambda b, c, order=order: (b, order(c), 0, 0))]
        out_shape += [jax.ShapeDtypeStruct((B, L, GLA_WIDTH), F32), jax.ShapeDtypeStruct((B, nc, GLA_DV, GLA_KW), F32)]
    outs = pl.pallas_call(
        body, name=name, grid=(B, nc), in_specs=in_specs, out_specs=out_specs, out_shape=out_shape,
        scratch_shapes=[pltpu.VMEM((GLA_DV, GLA_KW), F32)] * 2,
        compiler_params=_params(("parallel", "arbitrary")),
    )(*dirs[0], v, *dirs[1], v)
    return outs[:2], outs[2:]


def gla_scan_bwd(dirs, v, do, n_ctx, name):
    B, L, _ = v.shape
    C = GLA_CHUNK
    nc = L // C
    orders = []
    for rev in (False, True):
        fwd_order = _chunk_order(nc, n_ctx // C, rev)
        orders.append(lambda c, fwd_order=fwd_order: fwd_order(nc - 1 - c))

    def body(qf, kf, ksf, tf, ssf, vf, dof, qb, kb, ksb, tb, ssb, vb, dob,
             dqf, dkf, dksf, dvf, ddf, dqb, dkb, dksb, dvb, ddb, dstf, dstb):
        @pl.when(pl.program_id(1) == 0)
        def _():
            dstf[...] = jnp.zeros_like(dstf)
            dstb[...] = jnp.zeros_like(dstb)

        step(qf, kf, ksf, vf, tf, ssf, dof, dqf, dkf, dksf, dvf, ddf, dstf, False)
        step(qb, kb, ksb, vb, tb, ssb, dob, dqb, dkb, dksb, dvb, ddb, dstb, True)

    def step(q_ref, k_ref, ks_ref, v_ref, tot_ref, ss_ref, do_ref, dq_ref, dk_ref, dks_ref, dv_ref, dd_ref, dst, reverse):
        dSn = dst[...]
        dSnb = dSn.astype(BF16)
        S = ss_ref[0, 0]
        Sb = S.astype(BF16)
        dec = jnp.exp(tot_ref[0, 0:1, :])
        q = q_ref[0]
        k = k_ref[0]
        ksv = ks_ref[0]
        mask = _tri_mask(reverse)
        dq = jnp.zeros(q.shape, F32)
        dk = jnp.zeros(q.shape, F32)
        dks = jnp.zeros(q.shape, F32)
        dS = dec * dSn
        for h in range(GLA_HEADS):
            hm = _head_mask(q.shape, h)
            sl = slice(GLA_DV * h, GLA_DV * (h + 1))
            qm = jnp.where(hm, q, jnp.zeros_like(q))
            ksm = jnp.where(hm, ksv, jnp.zeros_like(ksv))
            vh = v_ref[0, :, sl].astype(BF16)
            doh = do_ref[0, :, sl]
            dohb = doh.astype(BF16)
            A = jnp.where(mask, _dot_nt(qm, k), 0.0)
            dA = jnp.where(mask, _dot_nt(dohb, vh), 0.0)
            dq = dq + jnp.where(hm, _dot(dohb, Sb) + _dot(dA.astype(BF16), k), 0.0)
            dk = dk + jnp.where(hm, _dot(dA.T.astype(BF16), q), 0.0)
            dv_ref[0, :, sl] = _dot(A.T.astype(BF16), dohb) + _dot_nt(ksm, dSnb)
            dks = dks + jnp.where(hm, _dot(vh, dSnb), 0.0)
            dS = dS + jnp.where(_head_mask(S.shape, h), _dot(doh.T.astype(BF16), q), 0.0)
        dq_ref[0] = dq
        dk_ref[0] = dk
        dks_ref[0] = dks
        dd_ref[0] = jnp.broadcast_to(jnp.sum(dSn * S, axis=0, keepdims=True), (C, GLA_KW))
        dst[...] = dS

    in_specs, out_specs, out_shape = [], [], []
    for order in orders:
        tok = lambda n, order=order: pl.BlockSpec((1, C, n), lambda b, c: (b, order(c), 0))
        in_specs += [tok(GLA_KW), tok(GLA_KW), tok(GLA_KW), tok(GLA_KW),
                     pl.BlockSpec((1, 1, GLA_DV, GLA_KW), lambda b, c, order=order: (b, order(c), 0, 0)), tok(GLA_WIDTH), tok(GLA_WIDTH)]
        out_specs += [tok(GLA_KW), tok(GLA_KW), tok(GLA_KW), tok(GLA_WIDTH), tok(GLA_KW)]
        out_shape += [jax.ShapeDtypeStruct((B, L, GLA_KW), F32)] * 3 + [jax.ShapeDtypeStruct((B, L, GLA_WIDTH), F32), jax.ShapeDtypeStruct((B, L, GLA_KW), F32)]
    outs = pl.pallas_call(
        body, name=name, grid=(B, nc), in_specs=in_specs, out_specs=out_specs, out_shape=out_shape,
        scratch_shapes=[pltpu.VMEM((GLA_DV, GLA_KW), F32)] * 2,
        compiler_params=_params(("parallel", "arbitrary")),
    )(*dirs[0], v, do, *dirs[1], v, do)
    return outs[:5], outs[5:]


def gla_out_fwd(of, ob, gn, zg, name):
    T = of.shape[0]

    def body(of_ref, ob_ref, gn_ref, g_ref, y_ref):
        for h in range(GLA_HEADS):
            sl = slice(GLA_DV * h, GLA_DV * (h + 1))
            xn, _ = _rms_rows(of_ref[:, sl] + ob_ref[:, sl])
            g = g_ref[:, sl]
            y_ref[:, sl] = (xn * gn_ref[...] * (g * _sig(g))).astype(BF16)

    tok = pl.BlockSpec((TOKEN_BLOCK, GLA_WIDTH), lambda i: (i, 0))
    return pl.pallas_call(
        body, name=name, grid=(T // TOKEN_BLOCK,),
        in_specs=[tok, tok, _full(gn.shape), tok], out_specs=tok,
        out_shape=jax.ShapeDtypeStruct((T, GLA_WIDTH), BF16),
        compiler_params=_params(("parallel",)),
    )(of, ob, gn, zg)


def gla_out_bwd(of, ob, gn, zg, dy, name):
    T = of.shape[0]

    def body(of_ref, ob_ref, gn_ref, g_ref, dy_ref, do_ref, dzg_ref, dgn_ref):
        first = pl.program_id(0) == 0
        gn_v = gn_ref[...]
        dgn = None
        for h in range(GLA_HEADS):
            sl = slice(GLA_DV * h, GLA_DV * (h + 1))
            xn, r = _rms_rows(of_ref[:, sl] + ob_ref[:, sl])
            silu, dsilu = _silu_and_grad(g_ref[:, sl])
            dyv = dy_ref[:, sl]
            dzg_ref[:, sl] = (dyv * xn * gn_v * dsilu).astype(BF16)
            dn = dyv * silu
            t = jnp.sum(dn * xn, axis=0, keepdims=True)
            dgn = t if dgn is None else dgn + t
            do_ref[:, sl] = _rms_rows_bwd(dn * gn_v, xn, r)
        _acc(dgn_ref, dgn, first)

    tok = pl.BlockSpec((TOKEN_BLOCK, GLA_WIDTH), lambda i: (i, 0))
    return pl.pallas_call(
        body, name=name, grid=(T // TOKEN_BLOCK,),
        in_specs=[tok, tok, _full(gn.shape), tok, tok], out_specs=[tok, tok, _full(gn.shape)],
        out_shape=[jax.ShapeDtypeStruct((T, GLA_WIDTH), F32), jax.ShapeDtypeStruct((T, GLA_WIDTH), BF16), jax.ShapeDtypeStruct(gn.shape, F32)],
        compiler_params=_params(("arbitrary",)),
    )(of, ob, gn, zg, dy)


def merge_post_fwd(ys, zm, wbs, wo, x2, pg, ms, nb, name):
    T = x2.shape[0]

    def body(y0, y1, y2, zm_ref, w0, w1, w2, wo_ref, x_ref, pg_ref, gate_ref, xn_ref, out_ref, mg_ref):
        merged = None
        for i, (y_ref, w_ref) in enumerate(((y0, w0), (y1, w1), (y2, w2))):
            t = _sig(zm_ref[:, D_MODEL * i:D_MODEL * (i + 1)]) * _dot(y_ref[...], w_ref[...])
            merged = t if merged is None else merged + t
        mb = merged.astype(BF16)
        mg_ref[...] = mb
        out = _dot(mb, wo_ref[...])
        out_ref[...] = out
        on, _ = _rms_rows(out)
        xn_ref[...] = x_ref[...] + gate_ref[0] * (on * pg_ref[...])

    tok = lambda n: pl.BlockSpec((TOKEN_BLOCK, n), lambda i: (i, 0))
    return pl.pallas_call(
        body, name=name, grid=(T // TOKEN_BLOCK,),
        in_specs=[tok(512)] * 3 + [tok(3 * D_MODEL)] + [_full(w.shape) for w in wbs] + [_full(wo.shape), tok(D_MODEL), _full(pg.shape), _mod_spec(nb, 2)],
        out_specs=[tok(D_MODEL)] * 3,
        out_shape=[jax.ShapeDtypeStruct((T, D_MODEL), F32), jax.ShapeDtypeStruct((T, D_MODEL), F32), jax.ShapeDtypeStruct((T, D_MODEL), BF16)],
        compiler_params=_params(("parallel",), VMEM_LIMIT),
    )(*ys, zm, *wbs, wo, x2, pg, ms)


def merge_post_bwd(dxn, out, ys, zm, wbs, wo, pg, ms, nb, name):
    T = dxn.shape[0]
    nrow = ms.shape[0]
    row = _mod_row(nb)

    def body(dxn_ref, out_ref, y0, y1, y2, zm_ref, w0, w1, w2, wo_ref, pg_ref, gate_ref,
             dy0, dy1, dy2, dzm_ref, dout_ref, dp0, dp1, dp2, dgate_ref, dpg_ref):
        i = pl.program_id(0)
        dxn_v = dxn_ref[...]
        on, r = _rms_rows(out_ref[...])
        pg_v = pg_ref[...]
        _acc(dgate_ref.at[0], jnp.sum(dxn_v * on * pg_v, axis=0, keepdims=True), (i % nb) <= 1)
        dn = dxn_v * gate_ref[0]
        _acc(dpg_ref, jnp.sum(dn * on, axis=0, keepdims=True), i == 0)
        dout = _rms_rows_bwd(dn * pg_v, on, r).astype(BF16)
        dout_ref[...] = dout
        dmerged = _dot_nt(dout, wo_ref[...])
        for j, (y_ref, w_ref, dy_ref, dp_ref) in enumerate(((y0, w0, dy0, dp0), (y1, w1, dy1, dp1), (y2, w2, dy2, dp2))):
            sl = slice(D_MODEL * j, D_MODEL * (j + 1))
            g = _sig(zm_ref[:, sl])
            p = _dot(y_ref[...], w_ref[...])
            dzm_ref[:, sl] = (dmerged * p * g * (1.0 - g)).astype(BF16)
            dp = (dmerged * g).astype(BF16)
            dp_ref[...] = dp
            dy_ref[...] = _dot_nt(dp, w_ref[...])

    tok = lambda n: pl.BlockSpec((TOKEN_BLOCK, n), lambda i: (i, 0))
    return pl.pallas_call(
        body, name=name, grid=(T // TOKEN_BLOCK,),
        in_specs=[tok(D_MODEL), tok(D_MODEL)] + [tok(512)] * 3 + [tok(3 * D_MODEL)] + [_full(w.shape) for w in wbs] + [_full(wo.shape), _full(pg.shape), _mod_spec(nb, 2)],
        out_specs=[tok(512)] * 3 + [tok(3 * D_MODEL), tok(D_MODEL)] + [tok(D_MODEL)] * 3 + [pl.BlockSpec((1, 1, D_MODEL), lambda i: (row(i), 0, 0)), _full(pg.shape)],
        out_shape=[jax.ShapeDtypeStruct((T, 512), F32)] * 3 + [jax.ShapeDtypeStruct((T, 3 * D_MODEL), BF16), jax.ShapeDtypeStruct((T, D_MODEL), BF16)]
        + [jax.ShapeDtypeStruct((T, D_MODEL), BF16)] * 3 + [jax.ShapeDtypeStruct((nrow, 1, D_MODEL), F32), jax.ShapeDtypeStruct(pg.shape, F32)],
        compiler_params=_params(("arbitrary",), VMEM_LIMIT),
    )(dxn, out, *ys, zm, *wbs, wo, pg, ms)


def loss_head(y2, tgt2, nb, name):
    T = y2.shape[0]
    nlat = nb - 1

    def body(y_ref, t_ref, dy_ref, loss_ref, acc):
        i = pl.program_id(0)
        is_lat = (i % nb) > 0

        @pl.when(i == 0)
        def _():
            acc[...] = jnp.zeros_like(acc)

        @pl.when(is_lat)
        def _():
            e = y_ref[...] - t_ref[...]
            dy_ref[...] = e * (1.0 / D_MODEL)
            acc[...] += jnp.sum(e * e, axis=0, keepdims=True)

        @pl.when(jnp.logical_not(is_lat))
        def _():
            dy_ref[...] = jnp.zeros_like(dy_ref)

        @pl.when(i == pl.num_programs(0) - 1)
        def _():
            loss_ref[...] = jnp.sum(acc[...], axis=1, keepdims=True) * (0.5 / D_MODEL)

    tok = pl.BlockSpec((TOKEN_BLOCK, D_MODEL), lambda i: (i, 0))
    tgt = pl.BlockSpec((TOKEN_BLOCK, D_MODEL), lambda i: ((i // nb) * nlat + jnp.maximum(i % nb - 1, 0), 0))
    return pl.pallas_call(
        body, name=name, grid=(T // TOKEN_BLOCK,),
        in_specs=[tok, tgt], out_specs=[tok, _full((1, 1))],
        out_shape=[jax.ShapeDtypeStruct((T, D_MODEL), F32), jax.ShapeDtypeStruct((1, 1), F32)],
        scratch_shapes=[pltpu.VMEM((1, D_MODEL), F32)],
        compiler_params=_params(("arbitrary",)),
    )(y2, tgt2)


_IN_OFFS = tuple(int(o) for o in np.cumsum((0,) + IN_SIZES))
_IN_GROUPS = (("a", 0, 416, 512), ("mg", 416, 512, 512), ("px", 928, 512, 512), ("pg", 1440, 512, 512), ("gq", 1952, 256, 256),
              ("gk", 2208, 256, 256), ("gv", 2464, 512, 512), ("lr", 2976, 32, 128), ("gg", 3008, 512, 512), ("m", 3520, 3072, 3072))


def _pad_cols(w, n):
    return w if w.shape[1] == n else jnp.pad(w, ((0, 0), (0, n - w.shape[1])))


def layer_weights(w_in, w_uq, w_ukv, af_w2, ab_w2, wbm, wbp, wbg, w_out):
    W = {}
    for nm, off, n, npad in _IN_GROUPS:
        W["in_" + nm] = _pad_cols(w_in[:, off:off + n], npad)
    uq = w_uq.reshape(MLA_Q_RANK, MLA_HEADS, MLA_NOPE + MLA_ROPE)
    W["qn"] = jnp.pad(uq[:, :, :MLA_NOPE], ((0, 0), (0, 0), (0, LANES - MLA_NOPE))).reshape(MLA_Q_RANK, MLA_HEADS * LANES)
    W["qr"] = jnp.pad(uq[:, :, MLA_NOPE:], ((0, 0), (0, 0), (0, LANES - MLA_ROPE))).reshape(MLA_Q_RANK, MLA_HEADS * LANES)
    W["kv"] = w_ukv
    W["af"] = jnp.pad(af_w2, ((0, LANES - GLA_GATE_RANK), (0, 0)))
    W["ab"] = jnp.pad(ab_w2, ((GLA_GATE_RANK, LANES - 2 * GLA_GATE_RANK), (0, 0)))
    W["bm"], W["bp"], W["bg"], W["out"] = wbm, wbp, wbg, w_out
    return W


def rope_tables(L, n_ctx):
    t = np.arange(L - n_ctx)
    half = MLA_ROPE // 2
    inv = ROPE_BASE ** (-np.arange(0, half, 2, dtype=np.float32) / half)
    ang_r = (t // GRID_W).astype(np.float32)[:, None] * inv
    ang_c = (t % GRID_W).astype(np.float32)[:, None] * inv
    ang = jnp.asarray(np.concatenate([ang_r, ang_r, ang_c, ang_c], axis=-1), F32)
    cos = jnp.ones((L, LANES), F32).at[n_ctx:, :MLA_ROPE].set(jnp.cos(ang))
    sin = jnp.zeros((L, LANES), F32).at[n_ctx:, :MLA_ROPE].set(jnp.sin(ang))
    return cos, sin


def layer_fwd(x2, ms, W, P, cos, sin, B, L, n_ctx, tag):
    nb = L // TOKEN_BLOCK
    r3 = lambda a: a.reshape(B, L, a.shape[-1])
    r2 = lambda a: a.reshape(B * L, a.shape[-1])
    h = norm_mod_fwd(x2, P["pre"], ms, nb, tag + "norm_mod")
    names = [g[0] for g in _IN_GROUPS[:-1]]
    z = dict(zip(names, mm_multi(h, [W["in_" + n] for n in names], [F32] * len(names), tag + "in_proj")))
    (z["m"],) = mm_multi(h, [W["in_m"]], [F32], tag + "in_proj_merge")
    qn, qr, kv, kr = mla_prep_fwd(z["a"], P["qg"], P["kvg"], W["qn"], W["qr"], W["kv"], cos, sin, nb, tag + "mla_prep")
    ya, y_mla, lse = attention_fwd(r3(qn), r3(qr), r3(kv), r3(kr), r3(z["mg"]), n_ctx, tag + "attention")
    y_pool = pool_fwd(r3(z["px"]), r3(z["pg"]), P["pw"], P["ps"], n_ctx, tag + "pool")
    qf, kf, ksf, tf, qb, kb, ksb, tb = gla_prep_fwd(z["lr"], z["gq"], z["gk"], W["af"], W["ab"], P["baf"], P["bab"], tag + "gla_prep")
    (of, ssf), (ob, ssb) = gla_scan_fwd([(r3(qf), r3(kf), r3(ksf), r3(tf)), (r3(qb), r3(kb), r3(ksb), r3(tb))], r3(z["gv"]), n_ctx, tag + "gla_scan")
    y_gla = gla_out_fwd(r2(of), r2(ob), P["gn"], z["gg"], tag + "gla_out")
    ys = [r2(y_mla), r2(y_pool), y_gla]
    x_new, out, merged = merge_post_fwd(ys, z["m"], [W["bm"], W["bp"], W["bg"]], W["out"], x2, P["post"], ms, nb, tag + "merge_post")
    res = dict(x2=x2, h=h, z=z, qn=qn, qr=qr, kv=kv, kr=kr, ya=ya, lse=lse, ys=ys, gla_f=(qf, kf, ksf, tf, ssf), gla_b=(qb, kb, ksb, tb, ssb),
               of=of, ob=ob, out=out, merged=merged)
    return x_new, res


def layer_bwd(dxn, res, ms, W, P, cos, sin, B, L, n_ctx, tag):
    nb = L // TOKEN_BLOCK
    r3 = lambda a: a.reshape(B, L, a.shape[-1])
    r2 = lambda a: a.reshape(B * L, a.shape[-1])
    z = res["z"]
    ys = res["ys"]
    wbs = [W["bm"], W["bp"], W["bg"]]
    dy0, dy1, dy2, dzm, dout, dp0, dp1, dp2, dgate, dpost = merge_post_bwd(dxn, res["out"], ys, z["m"], wbs, W["out"], P["post"], ms, nb, tag + "merge_post_bwd")
    G = {"out": mm_dw(res["merged"], dout, tag + "dw_out"), "post": dpost}
    for nm, y, dp in zip(("bm", "bp", "bg"), ys, (dp0, dp1, dp2)):
        G[nm] = mm_dw(y, dp, tag + "dw_" + nm)
    dz = {"m": dzm}
    do, dz["gg"], G["gn"] = gla_out_bwd(r2(res["of"]), r2(res["ob"]), P["gn"], z["gg"], dy2, tag + "gla_out_bwd")
    grads = gla_scan_bwd([(r3(qt), r3(kt), r3(ks), r3(tot), ss) for qt, kt, ks, tot, ss in (res["gla_f"], res["gla_b"])],
                         r3(z["gv"]), r3(do), n_ctx, tag + "gla_scan_bwd")
    gf = [r2(a) for a in grads[0]]
    gb = [r2(a) for a in grads[1]]
    dz["lr"], dz["gq"], dz["gk"], G["af"], G["ab"], G["baf"], G["bab"] = gla_prep_bwd(
        z["lr"], z["gq"], z["gk"], W["af"], W["ab"], P["baf"], P["bab"], gf[:3] + gf[4:], gb[:3] + gb[4:], tag + "gla_prep_bwd")
    dz["gv"] = add_cast(gf[3], gb[3], tag + "gla_dv")
    dpx, dpg, G["pw"], G["ps"] = pool_bwd(r3(z["px"]), r3(z["pg"]), P["pw"], P["ps"], r3(dy1), n_ctx, tag + "pool_bwd")
    dz["px"], dz["pg"] = r2(dpx), r2(dpg)
    dqn, dqr, dkv, dkr, dzmg = attention_bwd(r3(res["qn"]), r3(res["qr"]), r3(res["kv"]), r3(res["kr"]), r3(z["mg"]), res["ya"], res["lse"], r3(dy0), n_ctx, tag + "attention_bwd")
    dz["mg"] = r2(dzmg)
    dz["a"], G["qn"], G["qr"], G["kv"], G["qg"], G["kvg"] = mla_prep_bwd(
        r2(dqn), r2(dqr), r2(dkv), r2(dkr), z["a"], P["qg"], P["kvg"], W["qn"], W["qr"], W["kv"], cos, sin, nb, tag + "mla_prep_bwd")
    names = [g[0] for g in _IN_GROUPS]
    dh = mm_dx([dz[n] for n in names], [W["in_" + n] for n in names], tag + "in_proj_dx")
    for n in names:
        G["in_" + n] = mm_dw(res["h"], dz[n], tag + "dw_in_" + n)
    dx, dshift, dscale, G["pre"] = norm_mod_bwd(dh, res["x2"], P["pre"], ms, dxn, nb, tag + "norm_mod_bwd")
    dms = jnp.concatenate([dshift, dscale, dgate], axis=-1)
    return dx, G, dms


def add_cast(a, b, name):
    T, n = a.shape

    def body(a_ref, b_ref, o_ref):
        o_ref[...] = (a_ref[...] + b_ref[...]).astype(BF16)

    tok = pl.BlockSpec((TOKEN_BLOCK, n), lambda i: (i, 0))
    return pl.pallas_call(body, name=name, grid=(T // TOKEN_BLOCK,), in_specs=[tok, tok], out_specs=tok,
                          out_shape=jax.ShapeDtypeStruct((T, n), BF16), compiler_params=_params(("parallel",)))(a, b)


def layer_grads_natural(G):
    parts = {}
    for nm, off, n, npad in _IN_GROUPS:
        parts[off] = G["in_" + nm][:, :n]
    w_in = jnp.concatenate([parts[o] for o in sorted(parts)], axis=1)
    gqn = G["qn"].reshape(MLA_Q_RANK, MLA_HEADS, LANES)[:, :, :MLA_NOPE]
    gqr = G["qr"].reshape(MLA_Q_RANK, MLA_HEADS, LANES)[:, :, :MLA_ROPE]
    w_uq = jnp.concatenate([gqn, gqr], axis=-1).reshape(MLA_Q_RANK, MLA_HEADS * (MLA_NOPE + MLA_ROPE))
    return dict(w_in=w_in, mla_w_uq=w_uq, mla_w_ukv=G["kv"], gla_af_w2=G["af"][:GLA_GATE_RANK], gla_ab_w2=G["ab"][GLA_GATE_RANK:2 * GLA_GATE_RANK],
                w_branch_mla=G["bm"], w_branch_pool=G["bp"], w_branch_gla=G["bg"], w_out=G["out"],
                pre_norm=G["pre"][0], post_norm=G["post"][0], mla_q_norm=G["qg"][0], mla_kv_norm=G["kvg"][0], pool_w=G["pw"], pool_scale=G["ps"][0],
                gla_af_b=G["baf"][0], gla_ab_b=G["bab"][0], gla_norm=G["gn"][0])


def local_step(x, c, ctx, c_ctx, full, loss_target):
    B, S, _ = x.shape
    n_ctx = ctx.shape[1]
    L = n_ctx + S
    nb = L // TOKEN_BLOCK
    depth = full["w_in"].shape[0]
    cos, sin = rope_tables(L, n_ctx)
    x2 = jnp.concatenate([ctx, x], axis=1).reshape(B * L, D_MODEL)
    a8 = jnp.zeros((8, D_MODEL), F32).at[:B].set(c).at[B].set(c_ctx)
    Ws, Ps, mss, ress = [], [], [], []
    for l in range(depth):
        tag = f"l{l}_"
        W = layer_weights(full["w_in"][l], full["mla_w_uq"][l], full["mla_w_ukv"][l], full["gla_af_w2"][l], full["gla_ab_w2"][l],
                          full["w_branch_mla"][l], full["w_branch_pool"][l], full["w_branch_gla"][l], full["w_out"][l])
        P = dict(pre=full["pre_norm"][l][None], post=full["post_norm"][l][None], qg=full["mla_q_norm"][l][None], kvg=full["mla_kv_norm"][l][None],
                 pw=full["pool_w"][l].astype(BF16), ps=full["pool_scale"][l][None], baf=full["gla_af_b"][l][None], bab=full["gla_ab_b"][l][None],
                 gn=full["gla_norm"][l][None])
        mod8 = mod_fwd(a8, full["mod_w"][l], full["mod_b"][l][None], tag + "mod")
        ms = jnp.stack([jnp.broadcast_to(mod8[B], (B, 3 * D_MODEL)), mod8[:B]], axis=1).reshape(2 * B, 1, 3 * D_MODEL)
        x2, res = layer_fwd(x2, ms, W, P, cos, sin, B, L, n_ctx, tag)
        Ws.append(W), Ps.append(P), mss.append(ms), ress.append(res)
    dx, loss = loss_head(x2, loss_target.reshape(B * S, D_MODEL), nb, "loss_head")
    grads = [None] * depth
    da8 = None
    for l in reversed(range(depth)):
        tag = f"l{l}_"
        dx, G, dms = layer_bwd(dx, ress[l], mss[l], Ws[l], Ps[l], cos, sin, B, L, n_ctx, tag)
        dms = dms.reshape(B, 2, 3 * D_MODEL)
        dz8 = jnp.zeros((8, 3 * D_MODEL), F32).at[:B].set(dms[:, 1]).at[B].set(jnp.sum(dms[:, 0], axis=0))
        g_mod_w, g_mod_b, da = mod_bwd(a8, full["mod_w"][l], dz8, tag + "mod_bwd")
        da8 = da if da8 is None else da8 + da
        g = layer_grads_natural(G)
        g["mod_w"], g["mod_b"] = g_mod_w, g_mod_b[0]
        grads[l] = g
    gstack = {k: jnp.stack([grads[l][k] for l in range(depth)]) for k in grads[0]}
    gstack["c_ctx"] = da8[B]
    grad_x = dx.reshape(B, L, D_MODEL)[:, n_ctx:]
    return loss, grad_x, gstack


_MESH_ID = pl.DeviceIdType.MESH
_HBM = pl.BlockSpec(memory_space=pltpu.HBM)


def _me_and_peers():
    mx, my, mc = lax.axis_index("x"), lax.axis_index("y"), lax.axis_index("c")
    peers = []
    for k in range(1, N_DEV):
        px, py, pc = mx ^ ((k >> 2) & 1), my ^ ((k >> 1) & 1), mc ^ (k & 1)
        peers.append(((px, py, pc), 4 * px + 2 * py + pc))
    return 4 * mx + 2 * my + mc, peers


def _comm_scratch(n):
    return [pltpu.SemaphoreType.DMA((n * (N_DEV - 1),)), pltpu.SemaphoreType.DMA((n * (N_DEV - 1),)), pltpu.SemaphoreType.DMA((n,))]


def gather_blocks(xs, name):
    n = len(xs)
    K = N_DEV - 1

    def body(*refs):
        x_refs, o_refs = refs[:n], refs[n:2 * n]
        send_sems, recv_sems, local_sems = refs[2 * n:]
        mx, my, mc = lax.axis_index("x"), lax.axis_index("y"), lax.axis_index("c")
        me, sibling = (mx, my, mc), (mx, my, 1 - mc)
        chips = [(1 - mx, my), (mx, 1 - my), (1 - mx, 1 - my)]

        def slot(px, py, pc):
            return 4 * px + 2 * py + pc

        def copy(i, k, block, to, src=None):
            dst = o_refs[i].at[slot(*block)]
            return pltpu.make_async_remote_copy(src_ref=dst if src is None else src, dst_ref=dst, send_sem=send_sems.at[K * i + k],
                                                recv_sem=recv_sems.at[K * i + k], device_id=to, device_id_type=_MESH_ID)

        mine, first, passed = [], [], []
        for i in range(n):
            mine.append(pltpu.make_async_copy(x_refs[i], o_refs[i].at[slot(*me)], local_sems.at[i]))
            first.append(copy(i, 0, me, sibling, src=x_refs[i]))
            first += [copy(i, 1 + j, me, (*chip, mc), src=x_refs[i]) for j, chip in enumerate(chips)]
        for cp in mine + first:
            cp.start()
        for j, chip in enumerate(chips):
            for i in range(n):
                copy(i, 1 + j, (*chip, mc), me).wait_recv()
                passed.append(copy(i, 4 + j, (*chip, mc), sibling))
                passed[-1].start()
        for i in range(n):
            copy(i, 0, sibling, me).wait_recv()
            for j, chip in enumerate(chips):
                copy(i, 4 + j, (*chip, 1 - mc), me).wait_recv()
        for cp in first + passed:
            cp.wait_send()
        for cp in mine:
            cp.wait()

    return pl.pallas_call(
        body, name=name, in_specs=[_HBM] * n, out_specs=[_HBM] * n,
        out_shape=[jax.ShapeDtypeStruct((N_DEV,) + x.shape, x.dtype) for x in xs],
        scratch_shapes=_comm_scratch(n),
    )(*xs)


def scatter_blocks(xs, name):
    n = len(xs)
    K = N_DEV - 1

    def body(*refs):
        x_refs, o_refs = refs[:n], refs[n:2 * n]
        send_sems, recv_sems, local_sems = refs[2 * n:]
        me, peers = _me_and_peers()

        def copy(i, k, src_slot, dst_slot, to):
            return pltpu.make_async_remote_copy(src_ref=x_refs[i].at[src_slot], dst_ref=o_refs[i].at[dst_slot], send_sem=send_sems.at[K * i + k],
                                                recv_sem=recv_sems.at[K * i + k], device_id=to, device_id_type=_MESH_ID)

        mine = [pltpu.make_async_copy(x_refs[i].at[me], o_refs[i].at[me], local_sems.at[i]) for i in range(n)]
        sends = [copy(i, k, slot, me, peer) for k, (peer, slot) in enumerate(peers) for i in range(n)]
        for cp in mine + sends:
            cp.start()
        for k, (peer, slot) in enumerate(peers):
            for i in range(n):
                copy(i, k, slot, slot, peer).wait_recv()
        for cp in sends:
            cp.wait_send()
        for cp in mine:
            cp.wait()

    return pl.pallas_call(
        body, name=name, in_specs=[_HBM] * n, out_specs=[_HBM] * n,
        out_shape=[jax.ShapeDtypeStruct(x.shape, x.dtype) for x in xs],
        scratch_shapes=_comm_scratch(n),
    )(*xs)


def reduce_adamw(slots, w, m, v, name, tr=256):
    R, C = w.shape
    tr = min(tr, R)
    c1 = 1.0 / (1.0 - ADAM_B1 ** ADAM_STEP)
    c2 = 1.0 / (1.0 - ADAM_B2 ** ADAM_STEP)

    def body(s_ref, w_ref, m_ref, v_ref, g_ref, d_ref, nm_ref, nv_ref):
        g = s_ref[0].astype(F32)
        for s in range(1, N_DEV):
            g = g + s_ref[s].astype(F32)
        nm = ADAM_B1 * m_ref[...] + (1.0 - ADAM_B1) * g
        nv = ADAM_B2 * v_ref[...] + (1.0 - ADAM_B2) * (g * g)
        g_ref[...] = g
        nm_ref[...] = nm
        nv_ref[...] = nv
        d_ref[...] = -ADAM_LR * ((nm * c1) / (jnp.sqrt(nv * c2) + ADAM_EPS) + ADAM_WD * w_ref[...])

    blk = pl.BlockSpec((tr, C), lambda i: (i, 0))
    return pl.pallas_call(
        body, name=name, grid=(R // tr,),
        in_specs=[pl.BlockSpec((N_DEV, tr, C), lambda i: (0, i, 0)), blk, blk, blk], out_specs=[blk] * 4,
        out_shape=[jax.ShapeDtypeStruct((R, C), F32)] * 4,
        compiler_params=_params(("parallel",), VMEM_LIMIT),
    )(slots, w, m, v)


ARG_WEIGHTS = ("c_ctx", "mod_w", "mod_b", "pre_norm", "post_norm", "w_in", "mla_q_norm", "mla_w_uq", "mla_kv_norm", "mla_w_ukv", "pool_w",
               "pool_scale", "gla_af_w2", "gla_af_b", "gla_ab_w2", "gla_ab_b", "gla_norm", "w_branch_mla", "w_branch_pool", "w_branch_gla", "w_out")
SHARDED = ("mod_w", "w_in", "mla_w_uq", "mla_w_ukv", "gla_af_w2", "gla_ab_w2", "w_branch_mla", "w_branch_pool", "w_branch_gla", "w_out")
ROW_SHARDED = ("w_out",)
REPLICATED = tuple(n for n in ARG_WEIGHTS if n not in SHARDED)
PACK_ROWS = 512


def _pack(parts, dtype):
    flat = jnp.concatenate([p.astype(dtype).reshape(-1) for p in parts])
    n = flat.shape[0]
    total = -(-n // (PACK_ROWS * LANES)) * (PACK_ROWS * LANES)
    return jnp.pad(flat, (0, total - n)).reshape(total // LANES, LANES)


def _unpack(buf, shapes):
    flat = buf.reshape(-1)
    out, off = [], 0
    for shp in shapes:
        n = math.prod(shp)
        out.append(flat[off:off + n].reshape(shp))
        off += n
    return out


def _gathered_to_full(g, name, shard_shape):
    depth, r, cs = shard_shape
    if name in ROW_SHARDED:
        return g.transpose(1, 0, 2, 3).reshape(depth, N_DEV * r, cs)
    return g.transpose(1, 2, 0, 3).reshape(depth, r, N_DEV * cs)


def _full_to_slots(w, name, shard_shape):
    depth, r, cs = shard_shape
    if name in ROW_SHARDED:
        return w.reshape(depth, N_DEV, r, cs).transpose(1, 0, 2, 3)
    return w.reshape(depth, r, N_DEV, cs).transpose(2, 0, 1, 3)


def kernel(x, c, ctx, c_ctx, mod_w, mod_b, pre_norm, post_norm, w_in, mla_q_norm, mla_w_uq, mla_kv_norm, mla_w_ukv, pool_w, pool_scale, gla_af_w2, gla_af_b, gla_ab_w2, gla_ab_b, gla_norm, w_branch_mla, w_branch_pool, w_branch_gla, w_out, loss_target, m_c_ctx, m_mod_w, m_mod_b, m_pre_norm, m_post_norm, m_w_in, m_mla_q_norm, m_mla_w_uq, m_mla_kv_norm, m_mla_w_ukv, m_pool_w, m_pool_scale, m_gla_af_w2, m_gla_af_b, m_gla_ab_w2, m_gla_ab_b, m_gla_norm, m_w_branch_mla, m_w_branch_pool, m_w_branch_gla, m_w_out, v_c_ctx, v_mod_w, v_mod_b, v_pre_norm, v_post_norm, v_w_in, v_mla_q_norm, v_mla_w_uq, v_mla_kv_norm, v_mla_w_ukv, v_pool_w, v_pool_scale, v_gla_af_w2, v_gla_af_b, v_gla_ab_w2, v_gla_ab_b, v_gla_norm, v_w_branch_mla, v_w_branch_pool, v_w_branch_gla, v_w_out):
    local = dict(locals())
    wts = {n: local[n] for n in ARG_WEIGHTS}
    mom1 = {n: local["m_" + n] for n in ARG_WEIGHTS}
    mom2 = {n: local["v_" + n] for n in ARG_WEIGHTS}
    shard_shapes = [wts[n].shape for n in SHARDED]
    rep_shapes = [wts[n].shape for n in REPLICATED]
    kinds = ("grad", "delta", "new_m", "new_v")

    gathered = gather_blocks([wts[n].astype(BF16) for n in SHARDED], "gather_weights")
    full = {n: wts[n] for n in REPLICATED}
    for n, gw, shp in zip(SHARDED, gathered, shard_shapes):
        full[n] = _gathered_to_full(gw, n, shp)

    loss, grad_x, g = local_step(x, c, ctx, c_ctx, full, loss_target)

    slots = [_full_to_slots(g[n], n, shp).astype(BF16) for n, shp in zip(SHARDED, shard_shapes)]
    arrived = scatter_blocks(slots, "exchange_grads")
    res = {kind: {} for kind in kinds}
    for n, a, shp in zip(SHARDED, arrived, shard_shapes):
        flat = (shp[0] * shp[1], shp[2])
        outs = reduce_adamw(a.reshape((N_DEV,) + flat), wts[n].reshape(flat), mom1[n].reshape(flat), mom2[n].reshape(flat), "adamw_" + n)
        for kind, o in zip(kinds, outs):
            res[kind][n] = o.reshape(shp)

    (arrived,) = gather_blocks([_pack([g[n] for n in REPLICATED], F32)], "gather_small_grads")
    outs = reduce_adamw(arrived, _pack([wts[n] for n in REPLICATED], F32), _pack([mom1[n] for n in REPLICATED], F32),
                        _pack([mom2[n] for n in REPLICATED], F32), "adamw_replicated")
    for kind, o in zip(("grad", "delta", "new_m", "new_v"), outs):
        res[kind].update(zip(REPLICATED, _unpack(o, rep_shapes)))

    loss = lax.psum(loss[0, 0], ("x", "y", "c"))
    return (loss, grad_x, *[res[kind][n] for kind in ("grad", "delta", "new_m", "new_v") for n in ARG_WEIGHTS])
```

```python
import functools
import math

import jax
import jax.numpy as jnp
import numpy as np
from jax import lax
from jax.experimental import pallas as pl
from jax.experimental.pallas import tpu as pltpu

F32 = jnp.float32
BF16 = jnp.bfloat16

D_MODEL = 1024
NORM_EPS = 1e-6
GRID_W = 64
MLA_HEADS, MLA_Q_RANK, MLA_KV_RANK, MLA_NOPE, MLA_ROPE, MLA_V = 8, 256, 128, 64, 32, 64
MLA_WIDTH = MLA_HEADS * MLA_V
ROPE_BASE = 10000.0
ATT_SCALE = (MLA_NOPE + MLA_ROPE) ** -0.5
POOL_WINDOWS = (2, 4, 8, 16)
POOL_WIDTH, POOL_GROUP = 512, 128
GLA_HEADS, GLA_DK, GLA_DV = 4, 64, 128
GLA_KW, GLA_WIDTH = GLA_HEADS * GLA_DK, GLA_HEADS * GLA_DV
GLA_GATE_RANK, GLA_TAU, GLA_CHUNK = 16, 16.0, 64
IN_SIZES = (256, 128, 32, 512, 512, 512, 256, 256, 512, 16, 16, 512, 3 * D_MODEL)
ADAM_LR, ADAM_B1, ADAM_B2, ADAM_EPS, ADAM_WD, ADAM_STEP = 0.001, 0.9, 0.999, 1e-08, 0.01, 10
N_DEV = 8

LANES = 128
TOKEN_BLOCK = 256
VMEM_LIMIT = 48 * 1024 * 1024
NEG_BIG = -1e30

_NT = (((1,), (1,)), ((), ()))
_TN = (((0,), (0,)), ((), ()))


def _dot(a, b):
    return jnp.dot(a, b, preferred_element_type=F32)


def _dot_nt(a, b):
    return lax.dot_general(a, b, _NT, preferred_element_type=F32)


def _dot_tn(a, b):
    return lax.dot_general(a, b, _TN, preferred_element_type=F32)


def _params(sem=None, vmem=None):
    kw = {}
    if sem is not None:
        kw["dimension_semantics"] = sem
    if vmem is not None:
        kw["vmem_limit_bytes"] = vmem
    return pltpu.CompilerParams(**kw)


def _full(shape):
    n = len(shape)
    return pl.BlockSpec(shape, lambda *_: (0,) * n)


def _sig(x):
    return 1.0 / (1.0 + jnp.exp(-x))


def _silu_and_grad(x):
    s = _sig(x)
    return x * s, s * (1.0 + x * (1.0 - s))


def _acc(ref, val, first):
    @pl.when(first)
    def _():
        ref[...] = val

    @pl.when(jnp.logical_not(first))
    def _():
        ref[...] += val


def mm_multi(a, ws, dtypes, name, tm=TOKEN_BLOCK):
    M, K = a.shape
    nw = len(ws)

    def body(a_ref, *refs):
        av = a_ref[...]
        for w_ref, o_ref in zip(refs[:nw], refs[nw:]):
            o_ref[...] = _dot(av, w_ref[...]).astype(o_ref.dtype)

    return pl.pallas_call(
        body, name=name, grid=(M // tm,),
        in_specs=[pl.BlockSpec((tm, K), lambda i: (i, 0))] + [_full(w.shape) for w in ws],
        out_specs=[pl.BlockSpec((tm, w.shape[1]), lambda i: (i, 0)) for w in ws],
        out_shape=[jax.ShapeDtypeStruct((M, w.shape[1]), dt) for w, dt in zip(ws, dtypes)],
        compiler_params=_params(("parallel",), VMEM_LIMIT),
    )(a, *ws)


def mm_dx(dzs, ws, name, tm=TOKEN_BLOCK):
    M = dzs[0].shape[0]
    K = ws[0].shape[0]
    nw = len(ws)

    def body(*refs):
        o_ref = refs[-1]
        acc = None
        for dz_ref, w_ref in zip(refs[:nw], refs[nw:2 * nw]):
            t = _dot_nt(dz_ref[...], w_ref[...])
            acc = t if acc is None else acc + t
        o_ref[...] = acc

    return pl.pallas_call(
        body, name=name, grid=(M // tm,),
        in_specs=[pl.BlockSpec((tm, dz.shape[1]), lambda i: (i, 0)) for dz in dzs] + [_full(w.shape) for w in ws],
        out_specs=pl.BlockSpec((tm, K), lambda i: (i, 0)),
        out_shape=jax.ShapeDtypeStruct((M, K), F32),
        compiler_params=_params(("parallel",), VMEM_LIMIT),
    )(*dzs, *ws)


def mm_dw(a, dz, name, tn=1024):
    M, K = a.shape
    n = dz.shape[1]
    tn = min(tn, n)
    tk = next(t for t in (3072, 1536, 1024, 512, TOKEN_BLOCK) if M % t == 0)

    def body(a_ref, dz_ref, o_ref):
        _acc(o_ref, _dot_tn(a_ref[...], dz_ref[...]), pl.program_id(1) == 0)

    return pl.pallas_call(
        body, name=name, grid=(n // tn, M // tk),
        in_specs=[pl.BlockSpec((tk, K), lambda j, k: (k, 0)), pl.BlockSpec((tk, tn), lambda j, k: (k, j))],
        out_specs=pl.BlockSpec((K, tn), lambda j, k: (0, j)),
        out_shape=jax.ShapeDtypeStruct((K, n), F32),
        compiler_params=_params(("parallel", "arbitrary"), VMEM_LIMIT),
    )(a, dz)


def mod_fwd(a8, w, b, name):
    tn = D_MODEL

    def body(a_ref, w_ref, b_ref, o_ref):
        a = a_ref[...]
        o_ref[...] = _dot((a * _sig(a)).astype(BF16), w_ref[...]) + b_ref[...]

    return pl.pallas_call(
        body, name=name, grid=(3,),
        in_specs=[_full(a8.shape), pl.BlockSpec((D_MODEL, tn), lambda j: (0, j)), pl.BlockSpec((1, tn), lambda j: (0, j))],
        out_specs=pl.BlockSpec((8, tn), lambda j: (0, j)),
        out_shape=jax.ShapeDtypeStruct((8, 3 * D_MODEL), F32),
        compiler_params=_params(("parallel",)),
    )(a8, w, b)


def mod_bwd(a8, w, dz8, name):
    tn = D_MODEL

    def body(a_ref, w_ref, dz_ref, dw_ref, db_ref, da_ref):
        a = a_ref[...]
        sa, dsa = _silu_and_grad(a)
        dz = dz_ref[...]
        dw_ref[...] = _dot_tn(sa.astype(BF16), dz.astype(BF16))
        db_ref[...] = jnp.sum(dz, axis=0, keepdims=True)
        _acc(da_ref, _dot_nt(dz.astype(BF16), w_ref[...]) * dsa, pl.program_id(0) == 0)

    return pl.pallas_call(
        body, name=name, grid=(3,),
        in_specs=[_full(a8.shape), pl.BlockSpec((D_MODEL, tn), lambda j: (0, j)), pl.BlockSpec((8, tn), lambda j: (0, j))],
        out_specs=[pl.BlockSpec((D_MODEL, tn), lambda j: (0, j)), pl.BlockSpec((1, tn), lambda j: (0, j)), _full((8, D_MODEL))],
        out_shape=[jax.ShapeDtypeStruct((D_MODEL, 3 * D_MODEL), F32), jax.ShapeDtypeStruct((1, 3 * D_MODEL), F32),
                   jax.ShapeDtypeStruct((8, D_MODEL), F32)],
        compiler_params=_params(("arbitrary",)),
    )(a8, w, dz8)


def _mod_row(nb):
    return lambda i: 2 * (i // nb) + jnp.minimum(i % nb, 1)


def _mod_spec(nb, part):
    row = _mod_row(nb)
    return pl.BlockSpec((1, 1, D_MODEL), lambda i: (row(i), 0, part))


def norm_mod_fwd(x2, g, ms, nb, name):
    T = x2.shape[0]

    def body(x_ref, g_ref, sh_ref, sc_ref, h_ref):
        x = x_ref[...]
        r = lax.rsqrt(jnp.mean(x * x, axis=-1, keepdims=True) + NORM_EPS)
        h_ref[...] = ((x * r) * g_ref[...] * (1.0 + sc_ref[0]) + sh_ref[0]).astype(BF16)

    return pl.pallas_call(
        body, name=name, grid=(T // TOKEN_BLOCK,),
        in_specs=[pl.BlockSpec((TOKEN_BLOCK, D_MODEL), lambda i: (i, 0)), _full((1, D_MODEL)), _mod_spec(nb, 0), _mod_spec(nb, 1)],
        out_specs=pl.BlockSpec((TOKEN_BLOCK, D_MODEL), lambda i: (i, 0)),
        out_shape=jax.ShapeDtypeStruct((T, D_MODEL), BF16),
        compiler_params=_params(("parallel",)),
    )(x2, g, ms, ms)


def norm_mod_bwd(dh, x2, g, ms, dxres, nb, name):
    T = x2.shape[0]
    nrow = ms.shape[0]
    row = _mod_row(nb)

    def body(dh_ref, x_ref, g_ref, sc_ref, dxr_ref, dx_ref, dsh_ref, dsc_ref, dg_ref):
        i = pl.program_id(0)
        t = i % nb
        x = x_ref[...]
        dh = dh_ref[...]
        g = g_ref[...]
        r = lax.rsqrt(jnp.mean(x * x, axis=-1, keepdims=True) + NORM_EPS)
        xn = x * r
        du = dh * (1.0 + sc_ref[0])
        dyg = du * g
        dx_ref[...] = dxr_ref[...] + r * (dyg - xn * jnp.mean(dyg * xn, axis=-1, keepdims=True))
        first = t <= 1
        _acc(dsh_ref.at[0], jnp.sum(dh, axis=0, keepdims=True), first)
        _acc(dsc_ref.at[0], jnp.sum(dh * xn * g, axis=0, keepdims=True), first)
        _acc(dg_ref, jnp.sum(du * xn, axis=0, keepdims=True), i == 0)

    tok = pl.BlockSpec((TOKEN_BLOCK, D_MODEL), lambda i: (i, 0))
    acc = pl.BlockSpec((1, 1, D_MODEL), lambda i: (row(i), 0, 0))
    return pl.pallas_call(
        body, name=name, grid=(T // TOKEN_BLOCK,),
        in_specs=[tok, tok, _full((1, D_MODEL)), _mod_spec(nb, 1), tok],
        out_specs=[tok, acc, acc, _full((1, D_MODEL))],
        out_shape=[jax.ShapeDtypeStruct((T, D_MODEL), F32), jax.ShapeDtypeStruct((nrow, 1, D_MODEL), F32),
                   jax.ShapeDtypeStruct((nrow, 1, D_MODEL), F32), jax.ShapeDtypeStruct((1, D_MODEL), F32)],
        compiler_params=_params(("arbitrary",)),
    )(dh, x2, g, ms, dxres)


def _rot(x):
    lane = lax.broadcasted_iota(jnp.int32, x.shape, 1)
    return jnp.where((lane % 16) < 8, -pltpu.roll(x, LANES - 8, 1), pltpu.roll(x, 8, 1))


def _rope(x, cos, sin):
    return x * cos + _rot(x) * sin


def _rope_t(dy, cos, sin):
    return dy * cos - _rot(dy * sin)


def _rms_rows(x):
    r = lax.rsqrt(jnp.mean(x * x, axis=-1, keepdims=True) + NORM_EPS)
    return x * r, r


def _rms_rows_bwd(dyg, xn, r):
    return r * (dyg - xn * jnp.mean(dyg * xn, axis=-1, keepdims=True))


def mla_prep_fwd(za, qg, kvg, wqn, wqr, wkv, cos, sin, nb, name):
    T = za.shape[0]
    W = MLA_HEADS * LANES

    def body(z_ref, qg_ref, kvg_ref, wqn_ref, wqr_ref, wkv_ref, cos_ref, sin_ref, qn_ref, qr_ref, kv_ref, kr_ref):
        z = z_ref[...]
        cos = cos_ref[...]
        sin = sin_ref[...]
        xq, _ = _rms_rows(z[:, 0:256])
        qn = (xq * qg_ref[...]).astype(BF16)
        qn_ref[...] = (_dot(qn, wqn_ref[...]) * ATT_SCALE).astype(BF16)
        qr = _dot(qn, wqr_ref[...])
        for h in range(MLA_HEADS):
            sl = slice(LANES * h, LANES * (h + 1))
            qr_ref[:, sl] = (_rope(qr[:, sl], cos, sin) * ATT_SCALE).astype(BF16)
        xkv, _ = _rms_rows(z[:, 256:384])
        kv_ref[...] = _dot((xkv * kvg_ref[...]).astype(BF16), wkv_ref[...]).astype(BF16)
        kr_ref[...] = _rope(z[:, 384:512], cos, sin).astype(BF16)

    tok = lambda n: pl.BlockSpec((TOKEN_BLOCK, n), lambda i: (i, 0))
    pos = pl.BlockSpec((TOKEN_BLOCK, LANES), lambda i: (i % nb, 0))
    return pl.pallas_call(
        body, name=name, grid=(T // TOKEN_BLOCK,),
        in_specs=[tok(512), _full(qg.shape), _full(kvg.shape), _full(wqn.shape), _full(wqr.shape), _full(wkv.shape), pos, pos],
        out_specs=[tok(W), tok(W), tok(W), tok(LANES)],
        out_shape=[jax.ShapeDtypeStruct((T, W), BF16)] * 3 + [jax.ShapeDtypeStruct((T, LANES), BF16)],
        compiler_params=_params(("parallel",)),
    )(za, qg, kvg, wqn, wqr, wkv, cos, sin)


def mla_prep_bwd(dqn, dqr, dkv, dkr, za, qg, kvg, wqn, wqr, wkv, cos, sin, nb, name):
    T = za.shape[0]
    W = MLA_HEADS * LANES

    def body(dqn_ref, dqr_ref, dkv_ref, dkr_ref, z_ref, qg_ref, kvg_ref, wqn_ref, wqr_ref, wkv_ref, cos_ref, sin_ref,
             dz_ref, dwqn_ref, dwqr_ref, dwkv_ref, dqg_ref, dkvg_ref):
        first = pl.program_id(0) == 0
        z = z_ref[...]
        cos = cos_ref[...]
        sin = sin_ref[...]
        qg = qg_ref[...]
        kvg = kvg_ref[...]
        xq, rq = _rms_rows(z[:, 0:256])
        qn = (xq * qg).astype(BF16)
        a1 = (dqn_ref[...].astype(F32) * ATT_SCALE).astype(BF16)
        parts = []
        for h in range(MLA_HEADS):
            sl = slice(LANES * h, LANES * (h + 1))
            parts.append(_rope_t(dqr_ref[:, sl].astype(F32) * ATT_SCALE, cos, sin).astype(BF16))
        a2 = jnp.concatenate(parts, axis=1)
        dq = _dot_nt(a1, wqn_ref[...]) + _dot_nt(a2, wqr_ref[...])
        _acc(dwqn_ref, _dot_tn(qn, a1), first)
        _acc(dwqr_ref, _dot_tn(qn, a2), first)
        _acc(dqg_ref, jnp.sum(dq * xq, axis=0, keepdims=True), first)
        dz_ref[:, 0:256] = _rms_rows_bwd(dq * qg, xq, rq).astype(BF16)
        xkv, rkv = _rms_rows(z[:, 256:384])
        kvn = (xkv * kvg).astype(BF16)
        dkvb = dkv_ref[...].astype(BF16)
        dk = _dot_nt(dkvb, wkv_ref[...])
        _acc(dwkv_ref, _dot_tn(kvn, dkvb), first)
        _acc(dkvg_ref, jnp.sum(dk * xkv, axis=0, keepdims=True), first)
        dz_ref[:, 256:384] = _rms_rows_bwd(dk * kvg, xkv, rkv).astype(BF16)
        dz_ref[:, 384:512] = _rope_t(dkr_ref[...], cos, sin).astype(BF16)

    tok = lambda n: pl.BlockSpec((TOKEN_BLOCK, n), lambda i: (i, 0))
    pos = pl.BlockSpec((TOKEN_BLOCK, LANES), lambda i: (i % nb, 0))
    return pl.pallas_call(
        body, name=name, grid=(T // TOKEN_BLOCK,),
        in_specs=[tok(W), tok(W), tok(W), tok(LANES), tok(512), _full(qg.shape), _full(kvg.shape), _full(wqn.shape),
                  _full(wqr.shape), _full(wkv.shape), pos, pos],
        out_specs=[tok(512), _full(wqn.shape), _full(wqr.shape), _full(wkv.shape), _full(qg.shape), _full(kvg.shape)],
        out_shape=[jax.ShapeDtypeStruct((T, 512), BF16), jax.ShapeDtypeStruct(wqn.shape, F32), jax.ShapeDtypeStruct(wqr.shape, F32),
                   jax.ShapeDtypeStruct(wkv.shape, F32), jax.ShapeDtypeStruct(qg.shape, F32), jax.ShapeDtypeStruct(kvg.shape, F32)],
        compiler_params=_params(("arbitrary",)),
    )(dqn, dqr, dkv, dkr, za, qg, kvg, wqn, wqr, wkv, cos, sin)


def _att_qk(qn_ref, qr_ref, kv_ref, kr, j):
    sl = slice(LANES * j, LANES * (j + 1))
    q = jnp.concatenate([qn_ref[0, :, sl], qr_ref[0, :, sl]], axis=1)
    kvj = kv_ref[0, :, sl]
    k = jnp.concatenate([kvj, kr], axis=1)
    return q, k, kvj, _dot_nt(q, k)


def _att_specs(L, lk, q0):
    TQ, W2 = TOKEN_BLOCK, 2 * LANES
    qspec = pl.BlockSpec((1, TQ, W2), lambda b, h, i: (b, i + q0, h))
    kvspec = pl.BlockSpec((1, lk, W2), lambda b, h, i: (b, 0, h))
    krspec = pl.BlockSpec((1, lk, LANES), lambda b, h, i: (b, 0, 0))
    gspec = pl.BlockSpec((1, TQ, LANES), lambda b, h, i: (b, i + q0, h))
    lspec = pl.BlockSpec((1, 1, TQ, LANES), lambda b, h, i: (b, h, i + q0, 0))
    return qspec, kvspec, krspec, gspec, lspec


_ANY = pl.BlockSpec(memory_space=pl.ANY)


def attention_fwd(qn, qr, kv, kr, zg, n_ctx, name, hosted=None):
    B, L, _ = qn.shape
    TQ = TOKEN_BLOCK
    HP = MLA_HEADS // 2
    shapes = [jax.ShapeDtypeStruct((B, L, MLA_WIDTH), F32), jax.ShapeDtypeStruct((B, L, MLA_WIDTH), BF16),
              jax.ShapeDtypeStruct((B, HP, L, LANES), F32)]

    nx = hosted.n if hosted else 0
    NQ = L // TQ - 1

    def body(*refs):
        if hosted:
            step = (pl.program_id(0) * HP + pl.program_id(1)) * NQ + pl.program_id(2)
            hosted.run(refs[5:5 + nx], refs[8 + nx:8 + 2 * nx], refs[8 + 2 * nx:], step, B * HP * NQ)
        _fwd_step(*refs[:5], *refs[5 + nx:8 + nx])

    def body_ctx(qn_ref, qr_ref, kv_ref, kr_ref, g_ref, *rest):
        _fwd_step(qn_ref, qr_ref, kv_ref, kr_ref, g_ref, *rest[-3:])

    def _fwd_step(qn_ref, qr_ref, kv_ref, kr_ref, g_ref, ya_ref, ym_ref, lse_ref):
        kr_v = kr_ref[0]
        outs, lses = [], []
        for j in range(2):
            _, _, kvj, s = _att_qk(qn_ref, qr_ref, kv_ref, kr_v, j)
            m = jnp.max(s, axis=-1, keepdims=True)
            p = jnp.exp(s - m).astype(BF16)
            lane_k = lax.broadcasted_iota(jnp.int32, kvj.shape, 1)
            o = _dot(p, jnp.where(lane_k < MLA_V, jnp.ones_like(kvj), kvj))
            l = o[:, 0:1]
            outs.append(o / l)
            lses.append(m + jnp.log(l))
        lane = lax.broadcasted_iota(jnp.int32, outs[0].shape, 1)
        y = jnp.where(lane < MLA_V, pltpu.roll(outs[0], MLA_V, 1), outs[1])
        ya_ref[0] = y
        g = g_ref[0]
        ym_ref[0] = (y * g * _sig(g)).astype(BF16)
        lse_ref[0, 0] = jnp.where(lane < MLA_V, lses[0], lses[1])

    qspec, kvspec, krspec, gspec, lspec = _att_specs(L, L, 1)
    main = pl.pallas_call(
        body, name=name, grid=(B, HP, NQ),
        in_specs=[qspec, qspec, kvspec, krspec, gspec] + (hosted.in_specs if hosted else []),
        out_specs=[gspec, gspec, lspec] + (hosted.out_specs if hosted else []),
        out_shape=shapes + (hosted.out_shape if hosted else []), scratch_shapes=hosted.scratch if hosted else [],
        compiler_params=_params(("arbitrary",) * 3 if hosted else ("parallel",) * 3, VMEM_LIMIT),
    )(qn, qr, kv, kr, zg, *(hosted.xs if hosted else []))
    qspec, kvspec, krspec, gspec, lspec = _att_specs(L, n_ctx, 0)
    outs = pl.pallas_call(
        body_ctx, name=name + "_ctx", grid=(B, HP, 1),
        in_specs=[qspec, qspec, kvspec, krspec, gspec, _ANY, _ANY, _ANY], out_specs=[gspec, gspec, lspec], out_shape=shapes,
        input_output_aliases={5: 0, 6: 1, 7: 2},
        compiler_params=_params(("parallel", "parallel", "parallel"), VMEM_LIMIT),
    )(qn, qr, kv, kr, zg, *main[:3])
    return (*outs, list(main[3:]))


def attention_bwd(qn, qr, kv, kr, zg, ya, lse, dym, n_ctx, name, hosted=None):
    B, L, _ = qn.shape
    TQ = TOKEN_BLOCK
    HP = MLA_HEADS // 2
    W = MLA_HEADS * LANES
    shapes = [jax.ShapeDtypeStruct((B, L, W), BF16), jax.ShapeDtypeStruct((B, L, W), BF16), jax.ShapeDtypeStruct((B, L, W), F32),
              jax.ShapeDtypeStruct((B, L, LANES), F32), jax.ShapeDtypeStruct((B, L, MLA_WIDTH), BF16)]

    nx = hosted.n if hosted else 0
    NQ = L // TQ - 1

    def body(*refs):
        if hosted:
            step = (pl.program_id(0) * HP + pl.program_id(1)) * NQ + pl.program_id(2)
            hosted.run(refs[8:8 + nx], refs[13 + nx:13 + 2 * nx], refs[13 + 2 * nx:], step, B * HP * NQ)
        dkv_ref, dkr_ref = refs[10 + nx], refs[11 + nx]

        @pl.when(pl.program_id(2) == 0)
        def _():
            dkv_ref[...] = jnp.zeros_like(dkv_ref)

        @pl.when(jnp.logical_and(pl.program_id(2) == 0, pl.program_id(1) == 0))
        def _():
            dkr_ref[...] = jnp.zeros_like(dkr_ref)

        _bwd_step(*refs[:8], *refs[8 + nx:13 + nx])

    def body_ctx(qn_ref, qr_ref, kv_ref, kr_ref, g_ref, ya_ref, lse_ref, dy_ref, dkv_in, dkr_in, a0, a1, a2,
                 dqn_ref, dqr_ref, dkv_ref, dkr_ref, dzg_ref):
        dkv_ref[...] = dkv_in[...]

        @pl.when(pl.program_id(1) == 0)
        def _():
            dkr_ref[...] = dkr_in[...]

        _bwd_step(qn_ref, qr_ref, kv_ref, kr_ref, g_ref, ya_ref, lse_ref, dy_ref, dqn_ref, dqr_ref, dkv_ref, dkr_ref, dzg_ref)

    def _bwd_step(qn_ref, qr_ref, kv_ref, kr_ref, g_ref, ya_ref, lse_ref, dy_ref, dqn_ref, dqr_ref, dkv_ref, dkr_ref, dzg_ref):
        g = g_ref[0]
        silu, dsilu = _silu_and_grad(g)
        dy = dy_ref[0]
        ya_v = ya_ref[0]
        dya = dy * silu
        dzg_ref[0] = (dy * ya_v * dsilu).astype(BF16)
        lane = lax.broadcasted_iota(jnp.int32, dya.shape, 1)
        hi = lane >= MLA_V
        d_out = [jnp.where(hi, pltpu.roll(dya, MLA_V, 1), 0.0), jnp.where(hi, dya, 0.0)]
        prod = dya * ya_v
        drow = [jnp.sum(jnp.where(hi, 0.0, prod), axis=-1, keepdims=True), jnp.sum(jnp.where(hi, prod, 0.0), axis=-1, keepdims=True)]
        lse_v = lse_ref[0, 0]
        kr_v = kr_ref[0]
        for j in range(2):
            sl = slice(LANES * j, LANES * (j + 1))
            q, k, kvj, s = _att_qk(qn_ref, qr_ref, kv_ref, kr_v, j)
            pn = jnp.exp(s - lse_v[:, MLA_V * j:MLA_V * j + 1])
            dob = d_out[j].astype(BF16)
            ds = (pn * (_dot_nt(dob, kvj) - drow[j])).astype(BF16)
            dq = _dot(ds, k)
            dqn_ref[0, :, sl] = jnp.where(hi, 0.0, dq[:, :LANES]).astype(BF16)
            dqr_ref[0, :, sl] = dq[:, LANES:].astype(BF16)
            dk = _dot_tn(ds, q)
            dkv_ref[0, :, sl] += dk[:, :LANES] + _dot_tn(pn.astype(BF16), dob)
            dkr_ref[0] += dk[:, LANES:]

    sem = _params(("parallel", "arbitrary", "arbitrary"), VMEM_LIMIT)
    qspec, kvspec, krspec, gspec, lspec = _att_specs(L, L, 1)
    main = pl.pallas_call(
        body, name=name, grid=(B, HP, NQ),
        in_specs=[qspec, qspec, kvspec, krspec, gspec, gspec, lspec, gspec] + (hosted.in_specs if hosted else []),
        out_specs=[qspec, qspec, kvspec, krspec, gspec] + (hosted.out_specs if hosted else []),
        out_shape=shapes + (hosted.out_shape if hosted else []), scratch_shapes=hosted.scratch if hosted else [],
        compiler_params=_params(("arbitrary",) * 3, VMEM_LIMIT) if hosted else sem,
    )(qn, qr, kv, kr, zg, ya, lse, dym, *(hosted.xs if hosted else []))
    qspec, kvspec, krspec, gspec, lspec = _att_specs(L, n_ctx, 0)
    outs = pl.pallas_call(
        body_ctx, name=name + "_ctx", grid=(B, HP, 1),
        in_specs=[qspec, qspec, kvspec, krspec, gspec, gspec, lspec, gspec, kvspec, krspec, _ANY, _ANY, _ANY],
        out_specs=[qspec, qspec, kvspec, krspec, gspec], out_shape=shapes,
        input_output_aliases={8: 2, 9: 3, 10: 0, 11: 1, 12: 4}, compiler_params=sem,
    )(qn, qr, kv, kr, zg, ya, lse, dym, main[2], main[3], main[0], main[1], main[4])
    return (*outs, list(main[5:]))


def _seg_bounds(rows, n_ctx, L):
    in_ctx = rows < n_ctx
    return jnp.where(in_ctx, 0, n_ctx), jnp.where(in_ctx, n_ctx, L)


def _window_sum(u, w, rows, lo, hi, mirror):
    L = u.shape[0]
    offs = range(-w // 2 + 1, w // 2 + 1) if mirror else range(-w // 2, w // 2)
    acc = None
    for d in offs:
        if d == 0:
            t = u
        else:
            src = rows + d
            t = jnp.where(jnp.logical_and(src >= lo, src < hi), pltpu.roll(u, (-d) % L, 0), 0.0)
        acc = t if acc is None else acc + t
    return acc


def _window_count(w, rows, lo, hi):
    pos = rows - lo
    return (jnp.minimum(pos + w // 2, hi - lo) - jnp.maximum(pos - w // 2, 0)).astype(F32)


def pool_fwd(px, pg, pw, ps, n_ctx, name):
    B, L, _ = px.shape

    def body(px_ref, pg_ref, pw_ref, ps_ref, y_ref):
        rows = lax.broadcasted_iota(jnp.int32, (L, POOL_GROUP), 0)
        lo, hi = _seg_bounds(rows, n_ctx, L)
        for gi, w in enumerate(POOL_WINDOWS):
            sl = slice(POOL_GROUP * gi, POOL_GROUP * (gi + 1))
            u = px_ref[0, :, sl]
            pooled = _window_sum(u, w, rows, lo, hi, False) / _window_count(w, rows, lo, hi) - u
            mixed = _dot(pooled.astype(BF16), pw_ref[gi])
            g = pg_ref[0, :, sl]
            y_ref[0, :, sl] = (mixed * ps_ref[:, sl] * (g * _sig(g))).astype(BF16)

    tok = pl.BlockSpec((1, L, POOL_WIDTH), lambda b: (b, 0, 0))
    return pl.pallas_call(
        body, name=name, grid=(B,),
        in_specs=[tok, tok, _full(pw.shape), _full(ps.shape)],
        out_specs=tok, out_shape=jax.ShapeDtypeStruct((B, L, POOL_WIDTH), BF16),
        compiler_params=_params(("parallel",), VMEM_LIMIT),
    )(px, pg, pw, ps)


def pool_bwd(px, pg, pw, ps, dy, n_ctx, name):
    B, L, _ = px.shape

    def body(px_ref, pg_ref, pw_ref, ps_ref, dy_ref, dpx_ref, dpg_ref, dpw_ref, dps_ref):
        first = pl.program_id(0) == 0
        rows = lax.broadcasted_iota(jnp.int32, (L, POOL_GROUP), 0)
        lo, hi = _seg_bounds(rows, n_ctx, L)
        for gi, w in enumerate(POOL_WINDOWS):
            sl = slice(POOL_GROUP * gi, POOL_GROUP * (gi + 1))
            u = px_ref[0, :, sl]
            cnt = _window_count(w, rows, lo, hi)
            pooled = (_window_sum(u, w, rows, lo, hi, False) / cnt - u).astype(BF16)
            mixed = _dot(pooled, pw_ref[gi])
            silu, dsilu = _silu_and_grad(pg_ref[0, :, sl])
            sc = ps_ref[:, sl]
            dyv = dy_ref[0, :, sl]
            _acc(dps_ref.at[:, sl], jnp.sum(dyv * mixed * silu, axis=0, keepdims=True), first)
            dpg_ref[0, :, sl] = (dyv * mixed * sc * dsilu).astype(BF16)
            dmixed = (dyv * sc * silu).astype(BF16)
            _acc(dpw_ref.at[gi], _dot_tn(pooled, dmixed), first)
            dpooled = _dot_nt(dmixed, pw_ref[gi])
            dpx_ref[0, :, sl] = (_window_sum(dpooled / cnt, w, rows, lo, hi, True) - dpooled).astype(BF16)

    tok = pl.BlockSpec((1, L, POOL_WIDTH), lambda b: (b, 0, 0))
    return pl.pallas_call(
        body, name=name, grid=(B,),
        in_specs=[tok, tok, _full(pw.shape), _full(ps.shape), tok],
        out_specs=[tok, tok, _full(pw.shape), _full(ps.shape)],
        out_shape=[jax.ShapeDtypeStruct((B, L, POOL_WIDTH), BF16)] * 2 + [jax.ShapeDtypeStruct(pw.shape, F32), jax.ShapeDtypeStruct(ps.shape, F32)],
        compiler_params=_params(("arbitrary",), VMEM_LIMIT),
    )(px, pg, pw, ps, dy)


_SCAN_STEPS = (1, 2, 4, 8, 16, 32)


def _cum_fwd(x, r):
    for s in _SCAN_STEPS:
        x = x + jnp.where(r >= s, pltpu.roll(x, s, 0), 0.0)
    return x


def _cum_bwd(x, r):
    n = x.shape[0]
    for s in _SCAN_STEPS:
        x = x + jnp.where(r + s < GLA_CHUNK, pltpu.roll(x, n - s, 0), 0.0)
    return x


def _log_sigmoid(x):
    return jnp.minimum(x, 0.0) - jnp.log(1.0 + jnp.exp(-jnp.abs(x)))


def _gla_decays(lr, w_ref, b_ref, r, reverse):
    pre = _dot(lr, w_ref[...]) + b_ref[...]
    a = _log_sigmoid(pre) / GLA_TAU
    pf = _cum_fwd(a, r)
    sf = _cum_bwd(a, r)
    tot = pf + sf - a
    return pre, a, (sf if reverse else pf), tot


def gla_prep_fwd(zlr, zq, zk, waf, wab, baf, bab, name):
    T = zlr.shape[0]

    def body(lr_ref, q_ref, k_ref, waf_ref, wab_ref, baf_ref, bab_ref, qf_ref, kf_ref, ksf_ref, tf_ref, qb_ref, kb_ref, ksb_ref, tb_ref):
        r = lax.broadcasted_iota(jnp.int32, (TOKEN_BLOCK, GLA_KW), 0) % GLA_CHUNK
        lr = lr_ref[...].astype(BF16)
        q = q_ref[...] * GLA_DK ** -0.5
        k = k_ref[...]
        for rev, w_ref, b_ref, qo, ko, kso, to in ((False, waf_ref, baf_ref, qf_ref, kf_ref, ksf_ref, tf_ref),
                                                   (True, wab_ref, bab_ref, qb_ref, kb_ref, ksb_ref, tb_ref)):
            _, _, b, tot = _gla_decays(lr, w_ref, b_ref, r, rev)
            qo[...] = (q * jnp.exp(b)).astype(BF16)
            ko[...] = (k * jnp.exp(-b)).astype(BF16)
            kso[...] = (k * jnp.exp(tot - b)).astype(BF16)
            to[...] = tot

    tok = lambda n: pl.BlockSpec((TOKEN_BLOCK, n), lambda i: (i, 0))
    outs = [jax.ShapeDtypeStruct((T, GLA_KW), BF16)] * 3 + [jax.ShapeDtypeStruct((T, GLA_KW), F32)]
    return pl.pallas_call(
        body, name=name, grid=(T // TOKEN_BLOCK,),
        in_specs=[tok(LANES), tok(GLA_KW), tok(GLA_KW), _full(waf.shape), _full(wab.shape), _full(baf.shape), _full(bab.shape)],
        out_specs=[tok(GLA_KW)] * 8, out_shape=outs + outs,
        compiler_params=_params(("parallel",)),
    )(zlr, zq, zk, waf, wab, baf, bab)


def gla_prep_bwd(zlr, zq, zk, waf, wab, baf, bab, gf, gb, name):
    T = zlr.shape[0]

    def body(lr_ref, q_ref, k_ref, waf_ref, wab_ref, baf_ref, bab_ref, dqf, dkf, dksf, ddf, dqb, dkb, dksb, ddb,
             dlr_ref, dq_ref, dk_ref, dwaf_ref, dwab_ref, dbaf_ref, dbab_ref):
        first = pl.program_id(0) == 0
        r = lax.broadcasted_iota(jnp.int32, (TOKEN_BLOCK, GLA_KW), 0) % GLA_CHUNK
        lr = lr_ref[...].astype(BF16)
        q = q_ref[...] * GLA_DK ** -0.5
        k = k_ref[...]
        dq_tot = None
        dk_tot = None
        dlr = None
        for rev, w_ref, b_ref, dqt, dkt, dks, ddec, dw_ref, db_ref in (
                (False, waf_ref, baf_ref, dqf, dkf, dksf, ddf, dwaf_ref, dbaf_ref),
                (True, wab_ref, bab_ref, dqb, dkb, dksb, ddb, dwab_ref, dbab_ref)):
            pre, _, b, tot = _gla_decays(lr, w_ref, b_ref, r, rev)
            e1 = jnp.exp(b)
            e2 = jnp.exp(-b)
            e3 = jnp.exp(tot - b)
            dqt_v = dqt[...]
            dkt_v = dkt[...]
            dks_v = dks[...]
            dq = dqt_v * e1
            dk = dkt_v * e2 + dks_v * e3
            g3 = dks_v * (k * e3)
            d_b = dqt_v * (q * e1) - dkt_v * (k * e2) - g3
            d_tot = _cum_fwd(g3, r) + _cum_bwd(g3, r) - g3 + ddec[...] * jnp.exp(tot)
            da = (_cum_fwd(d_b, r) if rev else _cum_bwd(d_b, r)) + d_tot
            dpre = (da * (_sig(-pre) / GLA_TAU)).astype(BF16)
            t = _dot_nt(dpre, w_ref[...])
            dlr = t if dlr is None else dlr + t
            _acc(dw_ref, _dot_tn(lr, dpre), first)
            _acc(db_ref, jnp.sum(dpre.astype(F32), axis=0, keepdims=True), first)
            dq_tot = dq if dq_tot is None else dq_tot + dq
            dk_tot = dk if dk_tot is None else dk_tot + dk
        dlr_ref[...] = dlr.astype(BF16)
        dq_ref[...] = (dq_tot * GLA_DK ** -0.5).astype(BF16)
        dk_ref[...] = dk_tot.astype(BF16)

    tok = lambda n: pl.BlockSpec((TOKEN_BLOCK, n), lambda i: (i, 0))
    return pl.pallas_call(
        body, name=name, grid=(T // TOKEN_BLOCK,),
        in_specs=[tok(LANES), tok(GLA_KW), tok(GLA_KW), _full(waf.shape), _full(wab.shape), _full(baf.shape), _full(bab.shape)] + [tok(GLA_KW)] * 8,
        out_specs=[tok(LANES), tok(GLA_KW), tok(GLA_KW), _full(waf.shape), _full(wab.shape), _full(baf.shape), _full(bab.shape)],
        out_shape=[jax.ShapeDtypeStruct((T, LANES), BF16), jax.ShapeDtypeStruct((T, GLA_KW), BF16), jax.ShapeDtypeStruct((T, GLA_KW), BF16),
                   jax.ShapeDtypeStruct(waf.shape, F32), jax.ShapeDtypeStruct(wab.shape, F32), jax.ShapeDtypeStruct(baf.shape, F32),
                   jax.ShapeDtypeStruct(bab.shape, F32)],
        compiler_params=_params(("arbitrary",)),
    )(zlr, zq, zk, waf, wab, baf, bab, *gf, *gb)


def _chunk_order(nc, n_ctx_chunks, reverse):
    if not reverse:
        return lambda c: c
    return lambda c: jnp.where(c < n_ctx_chunks, n_ctx_chunks - 1 - c, nc + n_ctx_chunks - 1 - c)


def _head_mask(shape, h):
    lane = lax.broadcasted_iota(jnp.int32, shape, 1)
    return jnp.logical_and(lane >= GLA_DK * h, lane < GLA_DK * (h + 1))


def _tri_mask4(reverse):
    ri = lax.broadcasted_iota(jnp.int32, (GLA_CHUNK, GLA_HEADS * GLA_CHUNK), 0)
    ci = lax.broadcasted_iota(jnp.int32, (GLA_CHUNK, GLA_HEADS * GLA_CHUNK), 1) % GLA_CHUNK
    return (ri <= ci) if reverse else (ri >= ci)


def _block_diag(x, rb, cb):
    x4 = jnp.concatenate([x] * GLA_HEADS, axis=0)
    r = lax.broadcasted_iota(jnp.int32, x4.shape, 0) // rb
    c = lax.broadcasted_iota(jnp.int32, x4.shape, 1) // cb
    return jnp.where(r == c, x4, jnp.zeros_like(x4))


def _diag_blocks(f, rb, cb):
    c = lax.broadcasted_iota(jnp.int32, (rb, GLA_HEADS * cb), 1) // cb
    out = None
    for h in range(GLA_HEADS):
        t = jnp.where(c == h, f[rb * h:rb * (h + 1)], 0.0)
        out = t if out is None else out + t
    return out


def gla_scan_fwd(dirs, v, n_ctx, name):
    B, L, _ = v.shape
    C = GLA_CHUNK
    nc = L // C
    orders = [_chunk_order(nc, n_ctx // C, rev) for rev in (False, True)]

    def body(qf, kf, ksf, tf, vf, qb, kb, ksb, tb, vb, of, ssf, ob, ssb, stf, stb):
        @pl.when(pl.program_id(1) == 0)
        def _():
            stf[...] = jnp.zeros_like(stf)
            stb[...] = jnp.zeros_like(stb)

        step(qf, kf, ksf, vf, tf, of, ssf, stf, False)
        step(qb, kb, ksb, vb, tb, ob, ssb, stb, True)

    def step(q_ref, k_ref, ks_ref, v_ref, tot_ref, o_ref, ss_ref, st, reverse):
        S = st[...]
        ss_ref[0, 0] = S
        q = q_ref[0]
        v = v_ref[0]
        k4 = _block_diag(k_ref[0], GLA_CHUNK, GLA_DK)
        v4 = _block_diag(v.astype(BF16), GLA_CHUNK, GLA_DV)
        s4 = _block_diag(S.astype(BF16), GLA_DV, GLA_DK)
        P = jnp.where(_tri_mask4(reverse), _dot_nt(q, k4), 0.0)
        o_ref[0] = _dot(P.astype(BF16), v4) + _dot_nt(q, s4)
        st[...] = jnp.exp(tot_ref[0, 0:1, :]) * S + _diag_blocks(_dot(v.T.astype(BF16), ks_ref[0]), GLA_DV, GLA_DK)

    in_specs, out_specs, out_shape = [], [], []
    for order in orders:
        tok = lambda n, order=order: pl.BlockSpec((1, C, n), lambda b, c: (b, order(c), 0))
        in_specs += [tok(GLA_KW), tok(GLA_KW), tok(GLA_KW), tok(GLA_KW), tok(GLA_WIDTH)]
        out_specs += [tok(GLA_WIDTH), pl.BlockSpec((1, 1, GLA_DV, GLA_KW), lambda b, c, order=order: (b, order(c), 0, 0))]
        out_shape += [jax.ShapeDtypeStruct((B, L, GLA_WIDTH), F32), jax.ShapeDtypeStruct((B, nc, GLA_DV, GLA_KW), F32)]
    outs = pl.pallas_call(
        body, name=name, grid=(B, nc), in_specs=in_specs, out_specs=out_specs, out_shape=out_shape,
        scratch_shapes=[pltpu.VMEM((GLA_DV, GLA_KW), F32)] * 2,
        compiler_params=_params(("parallel", "arbitrary")),
    )(*dirs[0], v, *dirs[1], v)
    return outs[:2], outs[2:]


def gla_scan_bwd(dirs, v, do, n_ctx, name):
    B, L, _ = v.shape
    C = GLA_CHUNK
    nc = L // C
    orders = []
    for rev in (False, True):
        fwd_order = _chunk_order(nc, n_ctx // C, rev)
        orders.append(lambda c, fwd_order=fwd_order: fwd_order(nc - 1 - c))

    def body(qf, kf, ksf, tf, ssf, vf, dof, qb, kb, ksb, tb, ssb, vb, dob,
             dqf, dkf, dksf, dvf, ddf, dqb, dkb, dksb, dvb, ddb, dstf, dstb):
        @pl.when(pl.program_id(1) == 0)
        def _():
            dstf[...] = jnp.zeros_like(dstf)
            dstb[...] = jnp.zeros_like(dstb)

        step(qf, kf, ksf, vf, tf, ssf, dof, dqf, dkf, dksf, dvf, ddf, dstf, False)
        step(qb, kb, ksb, vb, tb, ssb, dob, dqb, dkb, dksb, dvb, ddb, dstb, True)

    def step(q_ref, k_ref, ks_ref, v_ref, tot_ref, ss_ref, do_ref, dq_ref, dk_ref, dks_ref, dv_ref, dd_ref, dst, reverse):
        dSn = dst[...]
        S = ss_ref[0, 0]
        q = q_ref[0]
        vb = v_ref[0].astype(BF16)
        dob = do_ref[0].astype(BF16)
        k4 = _block_diag(k_ref[0], GLA_CHUNK, GLA_DK)
        v4 = _block_diag(vb, GLA_CHUNK, GLA_DV)
        s4 = _block_diag(S.astype(BF16), GLA_DV, GLA_DK)
        ds4 = _block_diag(dSn.astype(BF16), GLA_DV, GLA_DK)
        tri = _tri_mask4(reverse)
        P = jnp.where(tri, _dot_nt(q, k4), 0.0).astype(BF16)
        dP = jnp.where(tri, _dot_nt(dob, v4), 0.0).astype(BF16)
        dq_ref[0] = _dot(dob, s4) + _dot(dP, k4)
        dk_ref[0] = _diag_blocks(_dot_tn(dP, q), GLA_CHUNK, GLA_DK)
        dv_ref[0] = _diag_blocks(_dot_tn(P, dob), GLA_CHUNK, GLA_DV) + _dot_nt(ks_ref[0], ds4)
        dks_ref[0] = _dot(vb, ds4)
        dd_ref[0] = jnp.broadcast_to(jnp.sum(dSn * S, axis=0, keepdims=True), (C, GLA_KW))
        dst[...] = jnp.exp(tot_ref[0, 0:1, :]) * dSn + _diag_blocks(_dot_tn(dob, q), GLA_DV, GLA_DK)

    in_specs, out_specs, out_shape = [], [], []
    for order in orders:
        tok = lambda n, order=order: pl.BlockSpec((1, C, n), lambda b, c: (b, order(c), 0))
        in_specs += [tok(GLA_KW), tok(GLA_KW), tok(GLA_KW), tok(GLA_KW),
                     pl.BlockSpec((1, 1, GLA_DV, GLA_KW), lambda b, c, order=order: (b, order(c), 0, 0)), tok(GLA_WIDTH), tok(GLA_WIDTH)]
        out_specs += [tok(GLA_KW), tok(GLA_KW), tok(GLA_KW), tok(GLA_WIDTH), tok(GLA_KW)]
        out_shape += [jax.ShapeDtypeStruct((B, L, GLA_KW), F32)] * 3 + [jax.ShapeDtypeStruct((B, L, GLA_WIDTH), F32), jax.ShapeDtypeStruct((B, L, GLA_KW), F32)]
    outs = pl.pallas_call(
        body, name=name, grid=(B, nc), in_specs=in_specs, out_specs=out_specs, out_shape=out_shape,
        scratch_shapes=[pltpu.VMEM((GLA_DV, GLA_KW), F32)] * 2,
        compiler_params=_params(("parallel", "arbitrary")),
    )(*dirs[0], v, do, *dirs[1], v, do)
    return outs[:5], outs[5:]


def gla_out_fwd(of, ob, gn, zg, name):
    T = of.shape[0]

    def body(of_ref, ob_ref, gn_ref, g_ref, y_ref):
        for h in range(GLA_HEADS):
            sl = slice(GLA_DV * h, GLA_DV * (h + 1))
            xn, _ = _rms_rows(of_ref[:, sl] + ob_ref[:, sl])
            g = g_ref[:, sl]
            y_ref[:, sl] = (xn * gn_ref[...] * (g * _sig(g))).astype(BF16)

    tok = pl.BlockSpec((TOKEN_BLOCK, GLA_WIDTH), lambda i: (i, 0))
    return pl.pallas_call(
        body, name=name, grid=(T // TOKEN_BLOCK,),
        in_specs=[tok, tok, _full(gn.shape), tok], out_specs=tok,
        out_shape=jax.ShapeDtypeStruct((T, GLA_WIDTH), BF16),
        compiler_params=_params(("parallel",)),
    )(of, ob, gn, zg)


def gla_out_bwd(of, ob, gn, zg, dy, name):
    T = of.shape[0]

    def body(of_ref, ob_ref, gn_ref, g_ref, dy_ref, do_ref, dzg_ref, dgn_ref):
        first = pl.program_id(0) == 0
        gn_v = gn_ref[...]
        dgn = None
        for h in range(GLA_HEADS):
            sl = slice(GLA_DV * h, GLA_DV * (h + 1))
            xn, r = _rms_rows(of_ref[:, sl] + ob_ref[:, sl])
            silu, dsilu = _silu_and_grad(g_ref[:, sl])
            dyv = dy_ref[:, sl]
            dzg_ref[:, sl] = (dyv * xn * gn_v * dsilu).astype(BF16)
            dn = dyv * silu
            t = jnp.sum(dn * xn, axis=0, keepdims=True)
            dgn = t if dgn is None else dgn + t
            do_ref[:, sl] = _rms_rows_bwd(dn * gn_v, xn, r)
        _acc(dgn_ref, dgn, first)

    tok = pl.BlockSpec((TOKEN_BLOCK, GLA_WIDTH), lambda i: (i, 0))
    return pl.pallas_call(
        body, name=name, grid=(T // TOKEN_BLOCK,),
        in_specs=[tok, tok, _full(gn.shape), tok, tok], out_specs=[tok, tok, _full(gn.shape)],
        out_shape=[jax.ShapeDtypeStruct((T, GLA_WIDTH), F32), jax.ShapeDtypeStruct((T, GLA_WIDTH), BF16), jax.ShapeDtypeStruct(gn.shape, F32)],
        compiler_params=_params(("arbitrary",)),
    )(of, ob, gn, zg, dy)


def merge_post_fwd(ys, zm, wbs, wo, x2, pg, ms, nb, name):
    T = x2.shape[0]

    def body(y0, y1, y2, zm_ref, w0, w1, w2, wo_ref, x_ref, pg_ref, gate_ref, xn_ref, out_ref, mg_ref):
        merged = None
        for i, (y_ref, w_ref) in enumerate(((y0, w0), (y1, w1), (y2, w2))):
            t = _sig(zm_ref[:, D_MODEL * i:D_MODEL * (i + 1)]) * _dot(y_ref[...], w_ref[...])
            merged = t if merged is None else merged + t
        mb = merged.astype(BF16)
        mg_ref[...] = mb
        out = _dot(mb, wo_ref[...])
        out_ref[...] = out
        on, _ = _rms_rows(out)
        xn_ref[...] = x_ref[...] + gate_ref[0] * (on * pg_ref[...])

    tok = lambda n: pl.BlockSpec((TOKEN_BLOCK, n), lambda i: (i, 0))
    return pl.pallas_call(
        body, name=name, grid=(T // TOKEN_BLOCK,),
        in_specs=[tok(512)] * 3 + [tok(3 * D_MODEL)] + [_full(w.shape) for w in wbs] + [_full(wo.shape), tok(D_MODEL), _full(pg.shape), _mod_spec(nb, 2)],
        out_specs=[tok(D_MODEL)] * 3,
        out_shape=[jax.ShapeDtypeStruct((T, D_MODEL), F32), jax.ShapeDtypeStruct((T, D_MODEL), F32), jax.ShapeDtypeStruct((T, D_MODEL), BF16)],
        compiler_params=_params(("parallel",), VMEM_LIMIT),
    )(*ys, zm, *wbs, wo, x2, pg, ms)


def merge_post_bwd(dxn, out, ys, zm, wbs, wo, pg, ms, nb, name):
    T = dxn.shape[0]
    nrow = ms.shape[0]
    row = _mod_row(nb)

    def body(dxn_ref, out_ref, y0, y1, y2, zm_ref, w0, w1, w2, wo_ref, pg_ref, gate_ref,
             dy0, dy1, dy2, dzm_ref, dout_ref, dp0, dp1, dp2, dgate_ref, dpg_ref):
        i = pl.program_id(0)
        dxn_v = dxn_ref[...]
        on, r = _rms_rows(out_ref[...])
        pg_v = pg_ref[...]
        _acc(dgate_ref.at[0], jnp.sum(dxn_v * on * pg_v, axis=0, keepdims=True), (i % nb) <= 1)
        dn = dxn_v * gate_ref[0]
        _acc(dpg_ref, jnp.sum(dn * on, axis=0, keepdims=True), i == 0)
        dout = _rms_rows_bwd(dn * pg_v, on, r).astype(BF16)
        dout_ref[...] = dout
        dmerged = _dot_nt(dout, wo_ref[...])
        for j, (y_ref, w_ref, dy_ref, dp_ref) in enumerate(((y0, w0, dy0, dp0), (y1, w1, dy1, dp1), (y2, w2, dy2, dp2))):
            sl = slice(D_MODEL * j, D_MODEL * (j + 1))
            g = _sig(zm_ref[:, sl])
            p = _dot(y_ref[...], w_ref[...])
            dzm_ref[:, sl] = (dmerged * p * g * (1.0 - g)).astype(BF16)
            dp = (dmerged * g).astype(BF16)
            dp_ref[...] = dp
            dy_ref[...] = _dot_nt(dp, w_ref[...])

    tok = lambda n: pl.BlockSpec((TOKEN_BLOCK, n), lambda i: (i, 0))
    return pl.pallas_call(
        body, name=name, grid=(T // TOKEN_BLOCK,),
        in_specs=[tok(D_MODEL), tok(D_MODEL)] + [tok(512)] * 3 + [tok(3 * D_MODEL)] + [_full(w.shape) for w in wbs] + [_full(wo.shape), _full(pg.shape), _mod_spec(nb, 2)],
        out_specs=[tok(512)] * 3 + [tok(3 * D_MODEL), tok(D_MODEL)] + [tok(D_MODEL)] * 3 + [pl.BlockSpec((1, 1, D_MODEL), lambda i: (row(i), 0, 0)), _full(pg.shape)],
        out_shape=[jax.ShapeDtypeStruct((T, 512), F32)] * 3 + [jax.ShapeDtypeStruct((T, 3 * D_MODEL), BF16), jax.ShapeDtypeStruct((T, D_MODEL), BF16)]
        + [jax.ShapeDtypeStruct((T, D_MODEL), BF16)] * 3 + [jax.ShapeDtypeStruct((nrow, 1, D_MODEL), F32), jax.ShapeDtypeStruct(pg.shape, F32)],
        compiler_params=_params(("arbitrary",), VMEM_LIMIT),
    )(dxn, out, *ys, zm, *wbs, wo, pg, ms)


def loss_head(y2, tgt2, nb, name):
    T = y2.shape[0]
    nlat = nb - 1

    def body(y_ref, t_ref, dy_ref, loss_ref, acc):
        i = pl.program_id(0)
        is_lat = (i % nb) > 0

        @pl.when(i == 0)
        def _():
            acc[...] = jnp.zeros_like(acc)

        @pl.when(is_lat)
        def _():
            e = y_ref[...] - t_ref[...]
            dy_ref[...] = e * (1.0 / D_MODEL)
            acc[...] += jnp.sum(e * e, axis=0, keepdims=True)

        @pl.when(jnp.logical_not(is_lat))
        def _():
            dy_ref[...] = jnp.zeros_like(dy_ref)

        @pl.when(i == pl.num_programs(0) - 1)
        def _():
            loss_ref[...] = jnp.sum(acc[...], axis=1, keepdims=True) * (0.5 / D_MODEL)

    tok = pl.BlockSpec((TOKEN_BLOCK, D_MODEL), lambda i: (i, 0))
    tgt = pl.BlockSpec((TOKEN_BLOCK, D_MODEL), lambda i: ((i // nb) * nlat + jnp.maximum(i % nb - 1, 0), 0))
    return pl.pallas_call(
        body, name=name, grid=(T // TOKEN_BLOCK,),
        in_specs=[tok, tgt], out_specs=[tok, _full((1, 1))],
        out_shape=[jax.ShapeDtypeStruct((T, D_MODEL), F32), jax.ShapeDtypeStruct((1, 1), F32)],
        scratch_shapes=[pltpu.VMEM((1, D_MODEL), F32)],
        compiler_params=_params(("arbitrary",)),
    )(y2, tgt2)


_IN_OFFS = tuple(int(o) for o in np.cumsum((0,) + IN_SIZES))
_IN_GROUPS = (("a", 0, 416, 512), ("mg", 416, 512, 512), ("px", 928, 512, 512), ("pg", 1440, 512, 512), ("gq", 1952, 256, 256),
              ("gk", 2208, 256, 256), ("gv", 2464, 512, 512), ("lr", 2976, 32, 128), ("gg", 3008, 512, 512), ("m", 3520, 3072, 3072))


def _pad_cols(w, n):
    return w if w.shape[1] == n else jnp.pad(w, ((0, 0), (0, n - w.shape[1])))


def layer_weights(w_in, w_uq, w_ukv, af_w2, ab_w2, wbm, wbp, wbg, w_out):
    W = {}
    for nm, off, n, npad in _IN_GROUPS:
        W["in_" + nm] = _pad_cols(w_in[:, off:off + n], npad)
    uq = w_uq.reshape(MLA_Q_RANK, MLA_HEADS, MLA_NOPE + MLA_ROPE)
    W["qn"] = jnp.pad(uq[:, :, :MLA_NOPE], ((0, 0), (0, 0), (0, LANES - MLA_NOPE))).reshape(MLA_Q_RANK, MLA_HEADS * LANES)
    W["qr"] = jnp.pad(uq[:, :, MLA_NOPE:], ((0, 0), (0, 0), (0, LANES - MLA_ROPE))).reshape(MLA_Q_RANK, MLA_HEADS * LANES)
    W["kv"] = w_ukv
    W["af"] = jnp.pad(af_w2, ((0, LANES - GLA_GATE_RANK), (0, 0)))
    W["ab"] = jnp.pad(ab_w2, ((GLA_GATE_RANK, LANES - 2 * GLA_GATE_RANK), (0, 0)))
    W["bm"], W["bp"], W["bg"], W["out"] = wbm, wbp, wbg, w_out
    return W


def rope_tables(L, n_ctx):
    t = np.arange(L - n_ctx)
    half = MLA_ROPE // 2
    inv = ROPE_BASE ** (-np.arange(0, half, 2, dtype=np.float32) / half)
    ang_r = (t // GRID_W).astype(np.float32)[:, None] * inv
    ang_c = (t % GRID_W).astype(np.float32)[:, None] * inv
    ang = jnp.asarray(np.concatenate([ang_r, ang_r, ang_c, ang_c], axis=-1), F32)
    cos = jnp.ones((L, LANES), F32).at[n_ctx:, :MLA_ROPE].set(jnp.cos(ang))
    sin = jnp.zeros((L, LANES), F32).at[n_ctx:, :MLA_ROPE].set(jnp.sin(ang))
    return cos, sin


def layer_fwd(x2, ms, W, P, cos, sin, B, L, n_ctx, tag, hosted=None):
    nb = L // TOKEN_BLOCK
    r3 = lambda a: a.reshape(B, L, a.shape[-1])
    r2 = lambda a: a.reshape(B * L, a.shape[-1])
    h = norm_mod_fwd(x2, P["pre"], ms, nb, tag + "norm_mod")
    names = [g[0] for g in _IN_GROUPS[:-1]]
    z = dict(zip(names, mm_multi(h, [W["in_" + n] for n in names], [F32] * len(names), tag + "in_proj")))
    (z["m"],) = mm_multi(h, [W["in_m"]], [F32], tag + "in_proj_merge")
    qn, qr, kv, kr = mla_prep_fwd(z["a"], P["qg"], P["kvg"], W["qn"], W["qr"], W["kv"], cos, sin, nb, tag + "mla_prep")
    ya, y_mla, lse, carried = attention_fwd(r3(qn), r3(qr), r3(kv), r3(kr), r3(z["mg"]), n_ctx, tag + "attention", hosted)
    y_pool = pool_fwd(r3(z["px"]), r3(z["pg"]), P["pw"], P["ps"], n_ctx, tag + "pool")
    qf, kf, ksf, tf, qb, kb, ksb, tb = gla_prep_fwd(z["lr"], z["gq"], z["gk"], W["af"], W["ab"], P["baf"], P["bab"], tag + "gla_prep")
    (of, ssf), (ob, ssb) = gla_scan_fwd([(r3(qf), r3(kf), r3(ksf), r3(tf)), (r3(qb), r3(kb), r3(ksb), r3(tb))], r3(z["gv"]), n_ctx, tag + "gla_scan")
    y_gla = gla_out_fwd(r2(of), r2(ob), P["gn"], z["gg"], tag + "gla_out")
    ys = [r2(y_mla), r2(y_pool), y_gla]
    x_new, out, merged = merge_post_fwd(ys, z["m"], [W["bm"], W["bp"], W["bg"]], W["out"], x2, P["post"], ms, nb, tag + "merge_post")
    res = dict(x2=x2, h=h, z=z, qn=qn, qr=qr, kv=kv, kr=kr, ya=ya, lse=lse, ys=ys, gla_f=(qf, kf, ksf, tf, ssf), gla_b=(qb, kb, ksb, tb, ssb),
               of=of, ob=ob, out=out, merged=merged)
    return x_new, res, carried


def layer_bwd(dxn, res, ms, W, P, cos, sin, B, L, n_ctx, tag, hosted=None):
    nb = L // TOKEN_BLOCK
    r3 = lambda a: a.reshape(B, L, a.shape[-1])
    r2 = lambda a: a.reshape(B * L, a.shape[-1])
    z = res["z"]
    ys = res["ys"]
    wbs = [W["bm"], W["bp"], W["bg"]]
    dy0, dy1, dy2, dzm, dout, dp0, dp1, dp2, dgate, dpost = merge_post_bwd(dxn, res["out"], ys, z["m"], wbs, W["out"], P["post"], ms, nb, tag + "merge_post_bwd")
    G = {"out": mm_dw(res["merged"], dout, tag + "dw_out"), "post": dpost}
    for nm, y, dp in zip(("bm", "bp", "bg"), ys, (dp0, dp1, dp2)):
        G[nm] = mm_dw(y, dp, tag + "dw_" + nm)
    dz = {"m": dzm}
    do, dz["gg"], G["gn"] = gla_out_bwd(r2(res["of"]), r2(res["ob"]), P["gn"], z["gg"], dy2, tag + "gla_out_bwd")
    grads = gla_scan_bwd([(r3(qt), r3(kt), r3(ks), r3(tot), ss) for qt, kt, ks, tot, ss in (res["gla_f"], res["gla_b"])],
                         r3(z["gv"]), r3(do), n_ctx, tag + "gla_scan_bwd")
    gf = [r2(a) for a in grads[0]]
    gb = [r2(a) for a in grads[1]]
    dz["lr"], dz["gq"], dz["gk"], G["af"], G["ab"], G["baf"], G["bab"] = gla_prep_bwd(
        z["lr"], z["gq"], z["gk"], W["af"], W["ab"], P["baf"], P["bab"], gf[:3] + gf[4:], gb[:3] + gb[4:], tag + "gla_prep_bwd")
    dz["gv"] = add_cast(gf[3], gb[3], tag + "gla_dv")
    dpx, dpg, G["pw"], G["ps"] = pool_bwd(r3(z["px"]), r3(z["pg"]), P["pw"], P["ps"], r3(dy1), n_ctx, tag + "pool_bwd")
    dz["px"], dz["pg"] = r2(dpx), r2(dpg)
    dqn, dqr, dkv, dkr, dzmg, got = attention_bwd(r3(res["qn"]), r3(res["qr"]), r3(res["kv"]), r3(res["kr"]), r3(z["mg"]), res["ya"], res["lse"],
                                                  r3(dy0), n_ctx, tag + "attention_bwd", hosted)
    dz["mg"] = r2(dzmg)
    dz["a"], G["qn"], G["qr"], G["kv"], G["qg"], G["kvg"] = mla_prep_bwd(
        r2(dqn), r2(dqr), r2(dkv), r2(dkr), z["a"], P["qg"], P["kvg"], W["qn"], W["qr"], W["kv"], cos, sin, nb, tag + "mla_prep_bwd")
    names = [g[0] for g in _IN_GROUPS]
    dh = mm_dx([dz[n] for n in names], [W["in_" + n] for n in names], tag + "in_proj_dx")
    for n in names:
        G["in_" + n] = mm_dw(res["h"], dz[n], tag + "dw_in_" + n)
    dx, dshift, dscale, G["pre"] = norm_mod_bwd(dh, res["x2"], P["pre"], ms, dxn, nb, tag + "norm_mod_bwd")
    dms = jnp.concatenate([dshift, dscale, dgate], axis=-1)
    return dx, G, dms, got


def add_cast(a, b, name):
    T, n = a.shape

    def body(a_ref, b_ref, o_ref):
        o_ref[...] = (a_ref[...] + b_ref[...]).astype(BF16)

    tok = pl.BlockSpec((TOKEN_BLOCK, n), lambda i: (i, 0))
    return pl.pallas_call(body, name=name, grid=(T // TOKEN_BLOCK,), in_specs=[tok, tok], out_specs=tok,
                          out_shape=jax.ShapeDtypeStruct((T, n), BF16), compiler_params=_params(("parallel",)))(a, b)


def layer_grads_natural(G):
    parts = {}
    for nm, off, n, npad in _IN_GROUPS:
        parts[off] = G["in_" + nm][:, :n]
    w_in = jnp.concatenate([parts[o] for o in sorted(parts)], axis=1)
    gqn = G["qn"].reshape(MLA_Q_RANK, MLA_HEADS, LANES)[:, :, :MLA_NOPE]
    gqr = G["qr"].reshape(MLA_Q_RANK, MLA_HEADS, LANES)[:, :, :MLA_ROPE]
    w_uq = jnp.concatenate([gqn, gqr], axis=-1).reshape(MLA_Q_RANK, MLA_HEADS * (MLA_NOPE + MLA_ROPE))
    return dict(w_in=w_in, mla_w_uq=w_uq, mla_w_ukv=G["kv"], gla_af_w2=G["af"][:GLA_GATE_RANK], gla_ab_w2=G["ab"][GLA_GATE_RANK:2 * GLA_GATE_RANK],
                w_branch_mla=G["bm"], w_branch_pool=G["bp"], w_branch_gla=G["bg"], w_out=G["out"],
                pre_norm=G["pre"][0], post_norm=G["post"][0], mla_q_norm=G["qg"][0], mla_kv_norm=G["kvg"][0], pool_w=G["pw"], pool_scale=G["ps"][0],
                gla_af_b=G["baf"][0], gla_ab_b=G["bab"][0], gla_norm=G["gn"][0])


def local_step(x, c, ctx, c_ctx, small, loss_target, depth, layer_full, host_fwd=None, host_bwd=None):
    B, S, _ = x.shape
    n_ctx = ctx.shape[1]
    L = n_ctx + S
    nb = L // TOKEN_BLOCK
    cos, sin = rope_tables(L, n_ctx)
    x2 = jnp.concatenate([ctx, x], axis=1).reshape(B * L, D_MODEL)
    a8 = jnp.zeros((8, D_MODEL), F32).at[:B].set(c).at[B].set(c_ctx)
    Ws, Ps, mss, ress, mod_ws = [], [], [], [], []
    carried = None
    for l in range(depth):
        tag = f"l{l}_"
        full = layer_full(l, carried)
        W = layer_weights(full["w_in"], full["mla_w_uq"], full["mla_w_ukv"], full["gla_af_w2"], full["gla_ab_w2"],
                          full["w_branch_mla"], full["w_branch_pool"], full["w_branch_gla"], full["w_out"])
        P = dict(pre=small["pre_norm"][l][None], post=small["post_norm"][l][None], qg=small["mla_q_norm"][l][None], kvg=small["mla_kv_norm"][l][None],
                 pw=small["pool_w"][l].astype(BF16), ps=small["pool_scale"][l][None], baf=small["gla_af_b"][l][None], bab=small["gla_ab_b"][l][None],
                 gn=small["gla_norm"][l][None])
        mod8 = mod_fwd(a8, full["mod_w"], small["mod_b"][l][None], tag + "mod")
        ms = jnp.stack([jnp.broadcast_to(mod8[B], (B, 3 * D_MODEL)), mod8[:B]], axis=1).reshape(2 * B, 1, 3 * D_MODEL)
        x2, res, carried = layer_fwd(x2, ms, W, P, cos, sin, B, L, n_ctx, tag, host_fwd(l) if host_fwd else None)
        Ws.append(W), Ps.append(P), mss.append(ms), ress.append(res), mod_ws.append(full["mod_w"])
    dx, loss = loss_head(x2, loss_target.reshape(B * S, D_MODEL), nb, "loss_head")
    grads = [None] * depth
    delivered = [None] * depth
    da8 = None
    for l in reversed(range(depth)):
        tag = f"l{l}_"
        hosted = host_bwd(l, grads[l + 1]) if host_bwd and l + 1 < depth else None
        dx, G, dms, got = layer_bwd(dx, ress[l], mss[l], Ws[l], Ps[l], cos, sin, B, L, n_ctx, tag, hosted)
        if hosted:
            delivered[l + 1] = got
        dms = dms.reshape(B, 2, 3 * D_MODEL)
        dz8 = jnp.zeros((8, 3 * D_MODEL), F32).at[:B].set(dms[:, 1]).at[B].set(jnp.sum(dms[:, 0], axis=0))
        g_mod_w, g_mod_b, da = mod_bwd(a8, mod_ws[l], dz8, tag + "mod_bwd")
        da8 = da if da8 is None else da8 + da
        g = layer_grads_natural(G)
        g["mod_w"], g["mod_b"] = g_mod_w, g_mod_b[0]
        grads[l] = g
    grad_x = dx.reshape(B, L, D_MODEL)[:, n_ctx:]
    return loss, grad_x, grads, da8[B], delivered


_MESH_ID = pl.DeviceIdType.MESH
_HBM = pl.BlockSpec(memory_space=pltpu.HBM)


def _me_and_peers():
    mx, my, mc = lax.axis_index("x"), lax.axis_index("y"), lax.axis_index("c")
    peers = []
    for k in range(1, N_DEV):
        px, py, pc = mx ^ ((k >> 2) & 1), my ^ ((k >> 1) & 1), mc ^ (k & 1)
        peers.append(((px, py, pc), 4 * px + 2 * py + pc))
    return 4 * mx + 2 * my + mc, peers


def _comm_scratch(n):
    return [pltpu.SemaphoreType.DMA((n * (N_DEV - 1),)), pltpu.SemaphoreType.DMA((n * (N_DEV - 1),)), pltpu.SemaphoreType.DMA((n,))]


class _Gather:
    def __init__(self, x_refs, o_refs, send_sems, recv_sems, local_sems):
        self.x, self.o, self.send, self.recv, self.local = x_refs, o_refs, send_sems, recv_sems, local_sems
        self.n = len(x_refs)
        mx, my, mc = lax.axis_index("x"), lax.axis_index("y"), lax.axis_index("c")
        self.me, self.sibling, self.mc = (mx, my, mc), (mx, my, 1 - mc), mc
        self.chips = [(1 - mx, my), (mx, 1 - my), (1 - mx, 1 - my)]

    @staticmethod
    def out_shape(xs):
        return [jax.ShapeDtypeStruct((N_DEV,) + x.shape, x.dtype) for x in xs]

    def _copy(self, i, k, block, to, src=None):
        px, py, pc = block
        dst = self.o[i].at[4 * px + 2 * py + pc]
        sem = (N_DEV - 1) * i + k
        return pltpu.make_async_remote_copy(src_ref=dst if src is None else src, dst_ref=dst, send_sem=self.send.at[sem],
                                            recv_sem=self.recv.at[sem], device_id=to, device_id_type=_MESH_ID)

    def _mine(self, i):
        mx, my, mc = self.me
        return pltpu.make_async_copy(self.x[i], self.o[i].at[4 * mx + 2 * my + mc], self.local.at[i])

    def _first(self):
        out = []
        for i in range(self.n):
            out.append(self._copy(i, 0, self.me, self.sibling, src=self.x[i]))
            out += [self._copy(i, 1 + j, self.me, (*chip, self.mc), src=self.x[i]) for j, chip in enumerate(self.chips)]
        return out

    def _passed(self, j, i):
        return self._copy(i, 4 + j, (*self.chips[j], self.mc), self.sibling)

    def start(self):
        for i in range(self.n):
            self._mine(i).start()
        for cp in self._first():
            cp.start()

    def forward(self):
        for j, chip in enumerate(self.chips):
            for i in range(self.n):
                self._copy(i, 1 + j, (*chip, self.mc), self.me).wait_recv()
                self._passed(j, i).start()

    def finish(self):
        for i in range(self.n):
            self._copy(i, 0, self.sibling, self.me).wait_recv()
            for j, chip in enumerate(self.chips):
                self._copy(i, 4 + j, (*chip, 1 - self.mc), self.me).wait_recv()
        for cp in self._first():
            cp.wait_send()
        for j in range(len(self.chips)):
            for i in range(self.n):
                self._passed(j, i).wait_send()
        for i in range(self.n):
            self._mine(i).wait()


class _Scatter:
    def __init__(self, x_refs, o_refs, send_sems, recv_sems, local_sems):
        self.x, self.o, self.send, self.recv, self.local = x_refs, o_refs, send_sems, recv_sems, local_sems
        self.n = len(x_refs)
        self.me, self.peers = _me_and_peers()

    @staticmethod
    def out_shape(xs):
        return [jax.ShapeDtypeStruct(x.shape, x.dtype) for x in xs]

    def _copy(self, i, k, src_slot, dst_slot, to):
        sem = (N_DEV - 1) * i + k
        return pltpu.make_async_remote_copy(src_ref=self.x[i].at[src_slot], dst_ref=self.o[i].at[dst_slot], send_sem=self.send.at[sem],
                                            recv_sem=self.recv.at[sem], device_id=to, device_id_type=_MESH_ID)

    def _mine(self, i):
        return pltpu.make_async_copy(self.x[i].at[self.me], self.o[i].at[self.me], self.local.at[i])

    def _sends(self):
        return [self._copy(i, k, slot, self.me, peer) for k, (peer, slot) in enumerate(self.peers) for i in range(self.n)]

    def start(self):
        for i in range(self.n):
            self._mine(i).start()
        for cp in self._sends():
            cp.start()

    def forward(self):
        pass

    def finish(self):
        for k, (peer, slot) in enumerate(self.peers):
            for i in range(self.n):
                self._copy(i, k, slot, slot, peer).wait_recv()
        for cp in self._sends():
            cp.wait_send()
        for i in range(self.n):
            self._mine(i).wait()


class _Hosted:
    def __init__(self, kind, xs):
        self.kind, self.xs, self.n = kind, list(xs), len(xs)
        self.in_specs = [_HBM] * self.n
        self.out_specs = [_HBM] * self.n
        self.out_shape = kind.out_shape(self.xs)
        self.scratch = _comm_scratch(self.n)

    def run(self, x_refs, o_refs, sems, step, total):
        for when, phase in ((0, "start"), (total // 2, "forward"), (total - 1, "finish")):
            @pl.when(step == when)
            def _(phase=phase):
                getattr(self.kind(x_refs, o_refs, *sems), phase)()


def gather_blocks(xs, name):
    n = len(xs)

    def body(*refs):
        g = _Gather(refs[:n], refs[n:2 * n], *refs[2 * n:])
        g.start()
        g.forward()
        g.finish()

    return pl.pallas_call(
        body, name=name, in_specs=[_HBM] * n, out_specs=[_HBM] * n,
        out_shape=_Gather.out_shape(xs), scratch_shapes=_comm_scratch(n),
    )(*xs)


def scatter_blocks(xs, name):
    n = len(xs)

    def body(*refs):
        s = _Scatter(refs[:n], refs[n:2 * n], *refs[2 * n:])
        s.start()
        s.finish()

    return pl.pallas_call(
        body, name=name, in_specs=[_HBM] * n, out_specs=[_HBM] * n,
        out_shape=_Scatter.out_shape(xs), scratch_shapes=_comm_scratch(n),
    )(*xs)


def reduce_adamw(slots, w, m, v, name, tr=256):
    R, C = w.shape
    nl = len(slots)
    rows = R // nl
    tr = min(tr, rows)
    nbl = rows // tr
    c1 = 1.0 / (1.0 - ADAM_B1 ** ADAM_STEP)
    c2 = 1.0 / (1.0 - ADAM_B2 ** ADAM_STEP)

    def body(*refs):
        w_ref, m_ref, v_ref, g_ref, d_ref, nm_ref, nv_ref = refs[nl:]
        part = pl.program_id(0) // nbl
        g = None
        for l, s_ref in enumerate(refs[:nl]):
            gl = s_ref[0].astype(F32)
            for s in range(1, N_DEV):
                gl = gl + s_ref[s].astype(F32)
            g = gl if g is None else jnp.where(part == l, gl, g)
        nm = ADAM_B1 * m_ref[...] + (1.0 - ADAM_B1) * g
        nv = ADAM_B2 * v_ref[...] + (1.0 - ADAM_B2) * (g * g)
        g_ref[...] = g
        nm_ref[...] = nm
        nv_ref[...] = nv
        d_ref[...] = -ADAM_LR * ((nm * c1) / (jnp.sqrt(nv * c2) + ADAM_EPS) + ADAM_WD * w_ref[...])

    blk = pl.BlockSpec((tr, C), lambda i: (i, 0))
    sspecs = [pl.BlockSpec((N_DEV, tr, C), lambda i, l=l: (0, jnp.clip(i - l * nbl, 0, nbl - 1), 0)) for l in range(nl)]
    return pl.pallas_call(
        body, name=name, grid=(R // tr,),
        in_specs=sspecs + [blk, blk, blk], out_specs=[blk] * 4,
        out_shape=[jax.ShapeDtypeStruct((R, C), F32)] * 4,
        compiler_params=_params(("parallel",), VMEM_LIMIT),
    )(*slots, w, m, v)


ARG_WEIGHTS = ("c_ctx", "mod_w", "mod_b", "pre_norm", "post_norm", "w_in", "mla_q_norm", "mla_w_uq", "mla_kv_norm", "mla_w_ukv", "pool_w",
               "pool_scale", "gla_af_w2", "gla_af_b", "gla_ab_w2", "gla_ab_b", "gla_norm", "w_branch_mla", "w_branch_pool", "w_branch_gla", "w_out")
SHARDED = ("mod_w", "w_in", "mla_w_uq", "mla_w_ukv", "gla_af_w2", "gla_ab_w2", "w_branch_mla", "w_branch_pool", "w_branch_gla", "w_out")
ROW_SHARDED = ("w_out",)
REPLICATED = tuple(n for n in ARG_WEIGHTS if n not in SHARDED)
PACK_ROWS = 512


def _pack(parts, dtype):
    flat = jnp.concatenate([p.astype(dtype).reshape(-1) for p in parts])
    n = flat.shape[0]
    total = -(-n // (PACK_ROWS * LANES)) * (PACK_ROWS * LANES)
    return jnp.pad(flat, (0, total - n)).reshape(total // LANES, LANES)


def _unpack(buf, shapes):
    flat = buf.reshape(-1)
    out, off = [], 0
    for shp in shapes:
        n = math.prod(shp)
        out.append(flat[off:off + n].reshape(shp))
        off += n
    return out


def _gathered_to_full(g, name):
    _, r, cs = g.shape
    if name in ROW_SHARDED:
        return g.reshape(N_DEV * r, cs)
    return g.transpose(1, 0, 2).reshape(r, N_DEV * cs)


def _full_to_slots(w, name):
    if name in ROW_SHARDED:
        return w.reshape(N_DEV, w.shape[0] // N_DEV, w.shape[1])
    return w.reshape(w.shape[0], N_DEV, w.shape[1] // N_DEV).transpose(1, 0, 2)


def kernel(x, c, ctx, c_ctx, mod_w, mod_b, pre_norm, post_norm, w_in, mla_q_norm, mla_w_uq, mla_kv_norm, mla_w_ukv, pool_w, pool_scale, gla_af_w2, gla_af_b, gla_ab_w2, gla_ab_b, gla_norm, w_branch_mla, w_branch_pool, w_branch_gla, w_out, loss_target, m_c_ctx, m_mod_w, m_mod_b, m_pre_norm, m_post_norm, m_w_in, m_mla_q_norm, m_mla_w_uq, m_mla_kv_norm, m_mla_w_ukv, m_pool_w, m_pool_scale, m_gla_af_w2, m_gla_af_b, m_gla_ab_w2, m_gla_ab_b, m_gla_norm, m_w_branch_mla, m_w_branch_pool, m_w_branch_gla, m_w_out, v_c_ctx, v_mod_w, v_mod_b, v_pre_norm, v_post_norm, v_w_in, v_mla_q_norm, v_mla_w_uq, v_mla_kv_norm, v_mla_w_ukv, v_pool_w, v_pool_scale, v_gla_af_w2, v_gla_af_b, v_gla_ab_w2, v_gla_ab_b, v_gla_norm, v_w_branch_mla, v_w_branch_pool, v_w_branch_gla, v_w_out):
    local = dict(locals())
    wts = {n: local[n] for n in ARG_WEIGHTS}
    mom1 = {n: local["m_" + n] for n in ARG_WEIGHTS}
    mom2 = {n: local["v_" + n] for n in ARG_WEIGHTS}
    shard_shapes = [wts[n].shape for n in SHARDED]
    rep_shapes = [wts[n].shape for n in REPLICATED]
    kinds = ("grad", "delta", "new_m", "new_v")

    depth = w_in.shape[0]

    def shards(l):
        return [wts[n][l].astype(BF16) for n in SHARDED]

    first = gather_blocks(shards(0), "gather_weights_l0")

    def layer_full(l, carried):
        return {n: _gathered_to_full(gw, n) for n, gw in zip(SHARDED, first if l == 0 else carried)}

    def host_fwd(l):
        return _Hosted(_Gather, shards(l + 1)) if l + 1 < depth else None

    def slots(g):
        return [_full_to_slots(g[n], n).astype(BF16) for n in SHARDED]

    def host_bwd(l, g_above):
        return _Hosted(_Scatter, slots(g_above))

    small = {n: wts[n] for n in REPLICATED}
    loss, grad_x, grads, g_c_ctx, arrived = local_step(x, c, ctx, c_ctx, small, loss_target, depth, layer_full, host_fwd, host_bwd)
    arrived[0] = scatter_blocks(slots(grads[0]), "exchange_grads_l0")
    res = {kind: {} for kind in kinds}
    for i, (n, shp) in enumerate(zip(SHARDED, shard_shapes)):
        flat = (shp[0] * shp[1], shp[2])
        outs = reduce_adamw([arrived[l][i] for l in range(depth)], wts[n].reshape(flat), mom1[n].reshape(flat), mom2[n].reshape(flat), "adamw_" + n)
        for kind, o in zip(kinds, outs):
            res[kind][n] = o.reshape(shp)

    g = {n: (g_c_ctx if n == "c_ctx" else jnp.stack([grads[l][n] for l in range(depth)])) for n in REPLICATED}
    (gathered,) = gather_blocks([_pack([g[n] for n in REPLICATED], F32)], "gather_small_grads")
    outs = reduce_adamw([gathered], _pack([wts[n] for n in REPLICATED], F32), _pack([mom1[n] for n in REPLICATED], F32),
                        _pack([mom2[n] for n in REPLICATED], F32), "adamw_replicated")
    for kind, o in zip(("grad", "delta", "new_m", "new_v"), outs):
        res[kind].update(zip(REPLICATED, _unpack(o, rep_shapes)))

    loss = lax.psum(loss[0, 0], ("x", "y", "c"))
    return (loss, grad_x, *[res[kind][n] for kind in ("grad", "delta", "new_m", "new_v") for n in ARG_WEIGHTS])
```

```python
import functools
import math

import jax
import jax.numpy as jnp
import numpy as np
from jax import lax
from jax.experimental import pallas as pl
from jax.experimental.pallas import tpu as pltpu

F32 = jnp.float32
BF16 = jnp.bfloat16

D_MODEL = 1024
NORM_EPS = 1e-6
GRID_W = 64
MLA_HEADS, MLA_Q_RANK, MLA_KV_RANK, MLA_NOPE, MLA_ROPE, MLA_V = 8, 256, 128, 64, 32, 64
MLA_WIDTH = MLA_HEADS * MLA_V
ROPE_BASE = 10000.0
ATT_SCALE = (MLA_NOPE + MLA_ROPE) ** -0.5
POOL_WINDOWS = (2, 4, 8, 16)
POOL_WIDTH, POOL_GROUP = 512, 128
GLA_HEADS, GLA_DK, GLA_DV = 4, 64, 128
GLA_KW, GLA_WIDTH = GLA_HEADS * GLA_DK, GLA_HEADS * GLA_DV
GLA_GATE_RANK, GLA_TAU, GLA_CHUNK = 16, 16.0, 64
IN_SIZES = (256, 128, 32, 512, 512, 512, 256, 256, 512, 16, 16, 512, 3 * D_MODEL)
ADAM_LR, ADAM_B1, ADAM_B2, ADAM_EPS, ADAM_WD, ADAM_STEP = 0.001, 0.9, 0.999, 1e-08, 0.01, 10
N_DEV = 8

LANES = 128
TOKEN_BLOCK = 256
VMEM_LIMIT = 48 * 1024 * 1024
NEG_BIG = -1e30

_NT = (((1,), (1,)), ((), ()))
_TN = (((0,), (0,)), ((), ()))


def _dot(a, b):
    return jnp.dot(a, b, preferred_element_type=F32)


def _dot_nt(a, b):
    return lax.dot_general(a, b, _NT, preferred_element_type=F32)


def _dot_tn(a, b):
    return lax.dot_general(a, b, _TN, preferred_element_type=F32)


def _params(sem=None, vmem=None):
    kw = {}
    if sem is not None:
        kw["dimension_semantics"] = sem
    if vmem is not None:
        kw["vmem_limit_bytes"] = vmem
    return pltpu.CompilerParams(**kw)


def _full(shape):
    n = len(shape)
    return pl.BlockSpec(shape, lambda *_: (0,) * n)


def _sig(x):
    return 1.0 / (1.0 + jnp.exp(-x))


def _silu_and_grad(x):
    s = _sig(x)
    return x * s, s * (1.0 + x * (1.0 - s))


def _acc(ref, val, first):
    @pl.when(first)
    def _():
        ref[...] = val

    @pl.when(jnp.logical_not(first))
    def _():
        ref[...] += val


def mm_multi(a, ws, dtypes, name, tm=TOKEN_BLOCK):
    M, K = a.shape
    nw = len(ws)

    def body(a_ref, *refs):
        av = a_ref[...]
        for w_ref, o_ref in zip(refs[:nw], refs[nw:]):
            o_ref[...] = _dot(av, w_ref[...]).astype(o_ref.dtype)

    return pl.pallas_call(
        body, name=name, grid=(M // tm,),
        in_specs=[pl.BlockSpec((tm, K), lambda i: (i, 0))] + [_full(w.shape) for w in ws],
        out_specs=[pl.BlockSpec((tm, w.shape[1]), lambda i: (i, 0)) for w in ws],
        out_shape=[jax.ShapeDtypeStruct((M, w.shape[1]), dt) for w, dt in zip(ws, dtypes)],
        compiler_params=_params(("parallel",), VMEM_LIMIT),
    )(a, *ws)


def mm_dx(dzs, ws, name, tm=TOKEN_BLOCK, hosted=None):
    M = dzs[0].shape[0]
    K = ws[0].shape[0]
    nw = len(ws)
    nx = hosted.n if hosted else 0

    def body(*refs):
        if hosted:
            hosted.run(refs[2 * nw:2 * nw + nx], refs[2 * nw + nx + 1:2 * nw + 2 * nx + 1], refs[2 * nw + 2 * nx + 1:], pl.program_id(0), M // tm)
        acc = None
        for dz_ref, w_ref in zip(refs[:nw], refs[nw:2 * nw]):
            t = _dot_nt(dz_ref[...], w_ref[...])
            acc = t if acc is None else acc + t
        refs[2 * nw + nx][...] = acc

    outs = pl.pallas_call(
        body, name=name, grid=(M // tm,),
        in_specs=[pl.BlockSpec((tm, dz.shape[1]), lambda i: (i, 0)) for dz in dzs] + [_full(w.shape) for w in ws] + (hosted.in_specs if hosted else []),
        out_specs=[pl.BlockSpec((tm, K), lambda i: (i, 0))] + (hosted.out_specs if hosted else []),
        out_shape=[jax.ShapeDtypeStruct((M, K), F32)] + (hosted.out_shape if hosted else []),
        scratch_shapes=hosted.scratch if hosted else [],
        compiler_params=_params(("arbitrary",) if hosted else ("parallel",), VMEM_LIMIT),
    )(*dzs, *ws, *(hosted.xs if hosted else []))
    return outs[0], list(outs[1:])


def mm_dw(a, dz, name, tn=1024):
    M, K = a.shape
    n = dz.shape[1]
    tn = min(tn, n)
    tk = next(t for t in (3072, 1536, 1024, 512, TOKEN_BLOCK) if M % t == 0)

    def body(a_ref, dz_ref, o_ref):
        _acc(o_ref, _dot_tn(a_ref[...], dz_ref[...]), pl.program_id(1) == 0)

    return pl.pallas_call(
        body, name=name, grid=(n // tn, M // tk),
        in_specs=[pl.BlockSpec((tk, K), lambda j, k: (k, 0)), pl.BlockSpec((tk, tn), lambda j, k: (k, j))],
        out_specs=pl.BlockSpec((K, tn), lambda j, k: (0, j)),
        out_shape=jax.ShapeDtypeStruct((K, n), F32),
        compiler_params=_params(("parallel", "arbitrary"), VMEM_LIMIT),
    )(a, dz)


def mod_fwd(a8, w, b, name):
    tn = D_MODEL

    def body(a_ref, w_ref, b_ref, o_ref):
        a = a_ref[...]
        o_ref[...] = _dot((a * _sig(a)).astype(BF16), w_ref[...]) + b_ref[...]

    return pl.pallas_call(
        body, name=name, grid=(3,),
        in_specs=[_full(a8.shape), pl.BlockSpec((D_MODEL, tn), lambda j: (0, j)), pl.BlockSpec((1, tn), lambda j: (0, j))],
        out_specs=pl.BlockSpec((8, tn), lambda j: (0, j)),
        out_shape=jax.ShapeDtypeStruct((8, 3 * D_MODEL), F32),
        compiler_params=_params(("parallel",)),
    )(a8, w, b)


def mod_bwd(a8, w, dz8, name):
    tn = D_MODEL

    def body(a_ref, w_ref, dz_ref, db_ref, da_ref):
        a = a_ref[...]
        _, dsa = _silu_and_grad(a)
        dz = dz_ref[...]
        db_ref[...] = jnp.sum(dz, axis=0, keepdims=True)
        _acc(da_ref, _dot_nt(dz.astype(BF16), w_ref[...]) * dsa, pl.program_id(0) == 0)

    return pl.pallas_call(
        body, name=name, grid=(3,),
        in_specs=[_full(a8.shape), pl.BlockSpec((D_MODEL, tn), lambda j: (0, j)), pl.BlockSpec((8, tn), lambda j: (0, j))],
        out_specs=[pl.BlockSpec((1, tn), lambda j: (0, j)), _full((8, D_MODEL))],
        out_shape=[jax.ShapeDtypeStruct((1, 3 * D_MODEL), F32), jax.ShapeDtypeStruct((8, D_MODEL), F32)],
        compiler_params=_params(("arbitrary",)),
    )(a8, w, dz8)


def mod_dw_columns(a_all, dz_cols, name):
    depth, R, n = dz_cols.shape

    def body(a_ref, dz_ref, dw_ref):
        a = a_ref[...]
        dw_ref[0] = _dot_tn((a * _sig(a)).astype(BF16), dz_ref[0].astype(BF16))

    return pl.pallas_call(
        body, name=name, grid=(depth,),
        in_specs=[_full(a_all.shape), pl.BlockSpec((1, R, n), lambda l: (l, 0, 0))],
        out_specs=pl.BlockSpec((1, D_MODEL, n), lambda l: (l, 0, 0)),
        out_shape=jax.ShapeDtypeStruct((depth, D_MODEL, n), F32),
        compiler_params=_params(("parallel",)),
    )(a_all, dz_cols)


def _mod_row(nb):
    return lambda i: 2 * (i // nb) + jnp.minimum(i % nb, 1)


def _mod_spec(nb, part):
    row = _mod_row(nb)
    return pl.BlockSpec((1, 1, D_MODEL), lambda i: (row(i), 0, part))


def norm_mod_fwd(x2, g, ms, nb, name):
    T = x2.shape[0]

    def body(x_ref, g_ref, sh_ref, sc_ref, h_ref):
        x = x_ref[...]
        r = lax.rsqrt(jnp.mean(x * x, axis=-1, keepdims=True) + NORM_EPS)
        h_ref[...] = ((x * r) * g_ref[...] * (1.0 + sc_ref[0]) + sh_ref[0]).astype(BF16)

    return pl.pallas_call(
        body, name=name, grid=(T // TOKEN_BLOCK,),
        in_specs=[pl.BlockSpec((TOKEN_BLOCK, D_MODEL), lambda i: (i, 0)), _full((1, D_MODEL)), _mod_spec(nb, 0), _mod_spec(nb, 1)],
        out_specs=pl.BlockSpec((TOKEN_BLOCK, D_MODEL), lambda i: (i, 0)),
        out_shape=jax.ShapeDtypeStruct((T, D_MODEL), BF16),
        compiler_params=_params(("parallel",)),
    )(x2, g, ms, ms)


def norm_mod_bwd(dh, x2, g, ms, dxres, nb, name):
    T = x2.shape[0]
    nrow = ms.shape[0]
    row = _mod_row(nb)

    def body(dh_ref, x_ref, g_ref, sc_ref, dxr_ref, dx_ref, dsh_ref, dsc_ref, dg_ref):
        i = pl.program_id(0)
        t = i % nb
        x = x_ref[...]
        dh = dh_ref[...]
        g = g_ref[...]
        r = lax.rsqrt(jnp.mean(x * x, axis=-1, keepdims=True) + NORM_EPS)
        xn = x * r
        du = dh * (1.0 + sc_ref[0])
        dyg = du * g
        dx_ref[...] = dxr_ref[...] + r * (dyg - xn * jnp.mean(dyg * xn, axis=-1, keepdims=True))
        first = t <= 1
        _acc(dsh_ref.at[0], jnp.sum(dh, axis=0, keepdims=True), first)
        _acc(dsc_ref.at[0], jnp.sum(dh * xn * g, axis=0, keepdims=True), first)
        _acc(dg_ref, jnp.sum(du * xn, axis=0, keepdims=True), i == 0)

    tok = pl.BlockSpec((TOKEN_BLOCK, D_MODEL), lambda i: (i, 0))
    acc = pl.BlockSpec((1, 1, D_MODEL), lambda i: (row(i), 0, 0))
    return pl.pallas_call(
        body, name=name, grid=(T // TOKEN_BLOCK,),
        in_specs=[tok, tok, _full((1, D_MODEL)), _mod_spec(nb, 1), tok],
        out_specs=[tok, acc, acc, _full((1, D_MODEL))],
        out_shape=[jax.ShapeDtypeStruct((T, D_MODEL), F32), jax.ShapeDtypeStruct((nrow, 1, D_MODEL), F32),
                   jax.ShapeDtypeStruct((nrow, 1, D_MODEL), F32), jax.ShapeDtypeStruct((1, D_MODEL), F32)],
        compiler_params=_params(("arbitrary",)),
    )(dh, x2, g, ms, dxres)


def _rot(x):
    lane = lax.broadcasted_iota(jnp.int32, x.shape, 1)
    return jnp.where((lane % 16) < 8, -pltpu.roll(x, LANES - 8, 1), pltpu.roll(x, 8, 1))


def _rope(x, cos, sin):
    return x * cos + _rot(x) * sin


def _rope_t(dy, cos, sin):
    return dy * cos - _rot(dy * sin)


def _rms_rows(x):
    r = lax.rsqrt(jnp.mean(x * x, axis=-1, keepdims=True) + NORM_EPS)
    return x * r, r


def _rms_rows_bwd(dyg, xn, r):
    return r * (dyg - xn * jnp.mean(dyg * xn, axis=-1, keepdims=True))


def mla_prep_fwd(za, qg, kvg, wqn, wqr, wkv, cos, sin, nb, name):
    T = za.shape[0]
    W = MLA_HEADS * LANES

    def body(z_ref, qg_ref, kvg_ref, wqn_ref, wqr_ref, wkv_ref, cos_ref, sin_ref, qn_ref, qr_ref, kv_ref, kr_ref):
        z = z_ref[...]
        cos = cos_ref[...]
        sin = sin_ref[...]
        xq, _ = _rms_rows(z[:, 0:256])
        qn = (xq * qg_ref[...]).astype(BF16)
        qn_ref[...] = (_dot(qn, wqn_ref[...]) * ATT_SCALE).astype(BF16)
        qr = _dot(qn, wqr_ref[...])
        for h in range(MLA_HEADS):
            sl = slice(LANES * h, LANES * (h + 1))
            qr_ref[:, sl] = (_rope(qr[:, sl], cos, sin) * ATT_SCALE).astype(BF16)
        xkv, _ = _rms_rows(z[:, 256:384])
        kv_ref[...] = _dot((xkv * kvg_ref[...]).astype(BF16), wkv_ref[...]).astype(BF16)
        kr_ref[...] = _rope(z[:, 384:512], cos, sin).astype(BF16)

    tok = lambda n: pl.BlockSpec((TOKEN_BLOCK, n), lambda i: (i, 0))
    pos = pl.BlockSpec((TOKEN_BLOCK, LANES), lambda i: (i % nb, 0))
    return pl.pallas_call(
        body, name=name, grid=(T // TOKEN_BLOCK,),
        in_specs=[tok(512), _full(qg.shape), _full(kvg.shape), _full(wqn.shape), _full(wqr.shape), _full(wkv.shape), pos, pos],
        out_specs=[tok(W), tok(W), tok(W), tok(LANES)],
        out_shape=[jax.ShapeDtypeStruct((T, W), BF16)] * 3 + [jax.ShapeDtypeStruct((T, LANES), BF16)],
        compiler_params=_params(("parallel",)),
    )(za, qg, kvg, wqn, wqr, wkv, cos, sin)


def mla_prep_bwd(dqn, dqr, dkv, dkr, za, qg, kvg, wqn, wqr, wkv, cos, sin, nb, name):
    T = za.shape[0]
    W = MLA_HEADS * LANES

    def body(dqn_ref, dqr_ref, dkv_ref, dkr_ref, z_ref, qg_ref, kvg_ref, wqn_ref, wqr_ref, wkv_ref, cos_ref, sin_ref,
             dz_ref, dwqn_ref, dwqr_ref, dwkv_ref, dqg_ref, dkvg_ref):
        first = pl.program_id(0) == 0
        z = z_ref[...]
        cos = cos_ref[...]
        sin = sin_ref[...]
        qg = qg_ref[...]
        kvg = kvg_ref[...]
        xq, rq = _rms_rows(z[:, 0:256])
        qn = (xq * qg).astype(BF16)
        a1 = (dqn_ref[...].astype(F32) * ATT_SCALE).astype(BF16)
        parts = []
        for h in range(MLA_HEADS):
            sl = slice(LANES * h, LANES * (h + 1))
            parts.append(_rope_t(dqr_ref[:, sl].astype(F32) * ATT_SCALE, cos, sin).astype(BF16))
        a2 = jnp.concatenate(parts, axis=1)
        dq = _dot_nt(a1, wqn_ref[...]) + _dot_nt(a2, wqr_ref[...])
        _acc(dwqn_ref, _dot_tn(qn, a1), first)
        _acc(dwqr_ref, _dot_tn(qn, a2), first)
        _acc(dqg_ref, jnp.sum(dq * xq, axis=0, keepdims=True), first)
        dz_ref[:, 0:256] = _rms_rows_bwd(dq * qg, xq, rq).astype(BF16)
        xkv, rkv = _rms_rows(z[:, 256:384])
        kvn = (xkv * kvg).astype(BF16)
        dkvb = dkv_ref[...].astype(BF16)
        dk = _dot_nt(dkvb, wkv_ref[...])
        _acc(dwkv_ref, _dot_tn(kvn, dkvb), first)
        _acc(dkvg_ref, jnp.sum(dk * xkv, axis=0, keepdims=True), first)
        dz_ref[:, 256:384] = _rms_rows_bwd(dk * kvg, xkv, rkv).astype(BF16)
        dz_ref[:, 384:512] = _rope_t(dkr_ref[...], cos, sin).astype(BF16)

    tok = lambda n: pl.BlockSpec((TOKEN_BLOCK, n), lambda i: (i, 0))
    pos = pl.BlockSpec((TOKEN_BLOCK, LANES), lambda i: (i % nb, 0))
    return pl.pallas_call(
        body, name=name, grid=(T // TOKEN_BLOCK,),
        in_specs=[tok(W), tok(W), tok(W), tok(LANES), tok(512), _full(qg.shape), _full(kvg.shape), _full(wqn.shape),
                  _full(wqr.shape), _full(wkv.shape), pos, pos],
        out_specs=[tok(512), _full(wqn.shape), _full(wqr.shape), _full(wkv.shape), _full(qg.shape), _full(kvg.shape)],
        out_shape=[jax.ShapeDtypeStruct((T, 512), BF16), jax.ShapeDtypeStruct(wqn.shape, F32), jax.ShapeDtypeStruct(wqr.shape, F32),
                   jax.ShapeDtypeStruct(wkv.shape, F32), jax.ShapeDtypeStruct(qg.shape, F32), jax.ShapeDtypeStruct(kvg.shape, F32)],
        compiler_params=_params(("arbitrary",)),
    )(dqn, dqr, dkv, dkr, za, qg, kvg, wqn, wqr, wkv, cos, sin)


def _att_qk(qn_ref, qr_ref, kv_ref, kr, j):
    sl = slice(LANES * j, LANES * (j + 1))
    q = jnp.concatenate([qn_ref[0, :, sl], qr_ref[0, :, sl]], axis=1)
    kvj = kv_ref[0, :, sl]
    k = jnp.concatenate([kvj, kr], axis=1)
    return q, k, kvj, _dot_nt(q, k)


def _att_specs(L, lk, q0):
    TQ, W2 = TOKEN_BLOCK, 2 * LANES
    qspec = pl.BlockSpec((1, TQ, W2), lambda b, h, i: (b, i + q0, h))
    kvspec = pl.BlockSpec((1, lk, W2), lambda b, h, i: (b, 0, h))
    krspec = pl.BlockSpec((1, lk, LANES), lambda b, h, i: (b, 0, 0))
    gspec = pl.BlockSpec((1, TQ, LANES), lambda b, h, i: (b, i + q0, h))
    lspec = pl.BlockSpec((1, 1, TQ, LANES), lambda b, h, i: (b, h, i + q0, 0))
    return qspec, kvspec, krspec, gspec, lspec


_ANY = pl.BlockSpec(memory_space=pl.ANY)


def attention_fwd(qn, qr, kv, kr, zg, n_ctx, name, hosted=None):
    B, L, _ = qn.shape
    TQ = TOKEN_BLOCK
    HP = MLA_HEADS // 2
    shapes = [jax.ShapeDtypeStruct((B, L, MLA_WIDTH), F32), jax.ShapeDtypeStruct((B, L, MLA_WIDTH), BF16),
              jax.ShapeDtypeStruct((B, HP, L, LANES), F32)]

    nx = hosted.n if hosted else 0
    NQ = L // TQ - 1

    def body(*refs):
        if hosted:
            step = (pl.program_id(0) * HP + pl.program_id(1)) * NQ + pl.program_id(2)
            hosted.run(refs[5:5 + nx], refs[8 + nx:8 + 2 * nx], refs[8 + 2 * nx:], step, B * HP * NQ)
        _fwd_step(*refs[:5], *refs[5 + nx:8 + nx])

    def body_ctx(qn_ref, qr_ref, kv_ref, kr_ref, g_ref, *rest):
        _fwd_step(qn_ref, qr_ref, kv_ref, kr_ref, g_ref, *rest[-3:])

    def _fwd_step(qn_ref, qr_ref, kv_ref, kr_ref, g_ref, ya_ref, ym_ref, lse_ref):
        kr_v = kr_ref[0]
        outs, lses = [], []
        for j in range(2):
            _, _, kvj, s = _att_qk(qn_ref, qr_ref, kv_ref, kr_v, j)
            m = jnp.max(s, axis=-1, keepdims=True)
            p = jnp.exp(s - m).astype(BF16)
            lane_k = lax.broadcasted_iota(jnp.int32, kvj.shape, 1)
            o = _dot(p, jnp.where(lane_k < MLA_V, jnp.ones_like(kvj), kvj))
            l = o[:, 0:1]
            outs.append(o / l)
            lses.append(m + jnp.log(l))
        lane = lax.broadcasted_iota(jnp.int32, outs[0].shape, 1)
        y = jnp.where(lane < MLA_V, pltpu.roll(outs[0], MLA_V, 1), outs[1])
        ya_ref[0] = y
        g = g_ref[0]
        ym_ref[0] = (y * g * _sig(g)).astype(BF16)
        lse_ref[0, 0] = jnp.where(lane < MLA_V, lses[0], lses[1])

    qspec, kvspec, krspec, gspec, lspec = _att_specs(L, L, 1)
    main = pl.pallas_call(
        body, name=name, grid=(B, HP, NQ),
        in_specs=[qspec, qspec, kvspec, krspec, gspec] + (hosted.in_specs if hosted else []),
        out_specs=[gspec, gspec, lspec] + (hosted.out_specs if hosted else []),
        out_shape=shapes + (hosted.out_shape if hosted else []), scratch_shapes=hosted.scratch if hosted else [],
        compiler_params=_params(("arbitrary",) * 3 if hosted else ("parallel",) * 3, VMEM_LIMIT),
    )(qn, qr, kv, kr, zg, *(hosted.xs if hosted else []))
    qspec, kvspec, krspec, gspec, lspec = _att_specs(L, n_ctx, 0)
    outs = pl.pallas_call(
        body_ctx, name=name + "_ctx", grid=(B, HP, 1),
        in_specs=[qspec, qspec, kvspec, krspec, gspec, _ANY, _ANY, _ANY], out_specs=[gspec, gspec, lspec], out_shape=shapes,
        input_output_aliases={5: 0, 6: 1, 7: 2},
        compiler_params=_params(("parallel", "parallel", "parallel"), VMEM_LIMIT),
    )(qn, qr, kv, kr, zg, *main[:3])
    return (*outs, list(main[3:]))


def attention_bwd(qn, qr, kv, kr, zg, ya, lse, dym, n_ctx, name, hosted=None):
    B, L, _ = qn.shape
    TQ = TOKEN_BLOCK
    HP = MLA_HEADS // 2
    W = MLA_HEADS * LANES
    shapes = [jax.ShapeDtypeStruct((B, L, W), BF16), jax.ShapeDtypeStruct((B, L, W), BF16), jax.ShapeDtypeStruct((B, L, W), F32),
              jax.ShapeDtypeStruct((B, L, LANES), F32), jax.ShapeDtypeStruct((B, L, MLA_WIDTH), BF16)]

    nx = hosted.n if hosted else 0
    NQ = L // TQ - 1

    def body(*refs):
        if hosted:
            step = (pl.program_id(0) * HP + pl.program_id(1)) * NQ + pl.program_id(2)
            hosted.run(refs[8:8 + nx], refs[13 + nx:13 + 2 * nx], refs[13 + 2 * nx:], step, B * HP * NQ)
        dkv_ref, dkr_ref = refs[10 + nx], refs[11 + nx]

        @pl.when(pl.program_id(2) == 0)
        def _():
            dkv_ref[...] = jnp.zeros_like(dkv_ref)

        @pl.when(jnp.logical_and(pl.program_id(2) == 0, pl.program_id(1) == 0))
        def _():
            dkr_ref[...] = jnp.zeros_like(dkr_ref)

        _bwd_step(*refs[:8], *refs[8 + nx:13 + nx])

    def body_ctx(qn_ref, qr_ref, kv_ref, kr_ref, g_ref, ya_ref, lse_ref, dy_ref, dkv_in, dkr_in, a0, a1, a2,
                 dqn_ref, dqr_ref, dkv_ref, dkr_ref, dzg_ref):
        dkv_ref[...] = dkv_in[...]

        @pl.when(pl.program_id(1) == 0)
        def _():
            dkr_ref[...] = dkr_in[...]

        _bwd_step(qn_ref, qr_ref, kv_ref, kr_ref, g_ref, ya_ref, lse_ref, dy_ref, dqn_ref, dqr_ref, dkv_ref, dkr_ref, dzg_ref)

    def _bwd_step(qn_ref, qr_ref, kv_ref, kr_ref, g_ref, ya_ref, lse_ref, dy_ref, dqn_ref, dqr_ref, dkv_ref, dkr_ref, dzg_ref):
        g = g_ref[0]
        silu, dsilu = _silu_and_grad(g)
        dy = dy_ref[0]
        ya_v = ya_ref[0]
        dya = dy * silu
        dzg_ref[0] = (dy * ya_v * dsilu).astype(BF16)
        lane = lax.broadcasted_iota(jnp.int32, dya.shape, 1)
        hi = lane >= MLA_V
        d_out = [jnp.where(hi, pltpu.roll(dya, MLA_V, 1), 0.0), jnp.where(hi, dya, 0.0)]
        prod = dya * ya_v
        drow = [jnp.sum(jnp.where(hi, 0.0, prod), axis=-1, keepdims=True), jnp.sum(jnp.where(hi, prod, 0.0), axis=-1, keepdims=True)]
        lse_v = lse_ref[0, 0]
        kr_v = kr_ref[0]
        for j in range(2):
            sl = slice(LANES * j, LANES * (j + 1))
            q, k, kvj, s = _att_qk(qn_ref, qr_ref, kv_ref, kr_v, j)
            pn = jnp.exp(s - lse_v[:, MLA_V * j:MLA_V * j + 1])
            dob = d_out[j].astype(BF16)
            ds = (pn * (_dot_nt(dob, kvj) - drow[j])).astype(BF16)
            dq = _dot(ds, k)
            dqn_ref[0, :, sl] = jnp.where(hi, 0.0, dq[:, :LANES]).astype(BF16)
            dqr_ref[0, :, sl] = dq[:, LANES:].astype(BF16)
            dk = _dot_tn(ds, q)
            dkv_ref[0, :, sl] += dk[:, :LANES] + _dot_tn(pn.astype(BF16), dob)
            dkr_ref[0] += dk[:, LANES:]

    sem = _params(("parallel", "arbitrary", "arbitrary"), VMEM_LIMIT)
    qspec, kvspec, krspec, gspec, lspec = _att_specs(L, L, 1)
    main = pl.pallas_call(
        body, name=name, grid=(B, HP, NQ),
        in_specs=[qspec, qspec, kvspec, krspec, gspec, gspec, lspec, gspec] + (hosted.in_specs if hosted else []),
        out_specs=[qspec, qspec, kvspec, krspec, gspec] + (hosted.out_specs if hosted else []),
        out_shape=shapes + (hosted.out_shape if hosted else []), scratch_shapes=hosted.scratch if hosted else [],
        compiler_params=_params(("arbitrary",) * 3, VMEM_LIMIT) if hosted else sem,
    )(qn, qr, kv, kr, zg, ya, lse, dym, *(hosted.xs if hosted else []))
    qspec, kvspec, krspec, gspec, lspec = _att_specs(L, n_ctx, 0)
    outs = pl.pallas_call(
        body_ctx, name=name + "_ctx", grid=(B, HP, 1),
        in_specs=[qspec, qspec, kvspec, krspec, gspec, gspec, lspec, gspec, kvspec, krspec, _ANY, _ANY, _ANY],
        out_specs=[qspec, qspec, kvspec, krspec, gspec], out_shape=shapes,
        input_output_aliases={8: 2, 9: 3, 10: 0, 11: 1, 12: 4}, compiler_params=sem,
    )(qn, qr, kv, kr, zg, ya, lse, dym, main[2], main[3], main[0], main[1], main[4])
    return (*outs, list(main[5:]))


def _seg_bounds(rows, n_ctx, L):
    in_ctx = rows < n_ctx
    return jnp.where(in_ctx, 0, n_ctx), jnp.where(in_ctx, n_ctx, L)


def _window_sum(u, w, rows, lo, hi, mirror):
    L = u.shape[0]
    offs = range(-w // 2 + 1, w // 2 + 1) if mirror else range(-w // 2, w // 2)
    acc = None
    for d in offs:
        if d == 0:
            t = u
        else:
            src = rows + d
            t = jnp.where(jnp.logical_and(src >= lo, src < hi), pltpu.roll(u, (-d) % L, 0), 0.0)
        acc = t if acc is None else acc + t
    return acc


def _window_count(w, rows, lo, hi):
    pos = rows - lo
    return (jnp.minimum(pos + w // 2, hi - lo) - jnp.maximum(pos - w // 2, 0)).astype(F32)


def pool_fwd(px, pg, pw, ps, n_ctx, name):
    B, L, _ = px.shape

    def body(px_ref, pg_ref, pw_ref, ps_ref, y_ref):
        rows = lax.broadcasted_iota(jnp.int32, (L, POOL_GROUP), 0)
        lo, hi = _seg_bounds(rows, n_ctx, L)
        for gi, w in enumerate(POOL_WINDOWS):
            sl = slice(POOL_GROUP * gi, POOL_GROUP * (gi + 1))
            u = px_ref[0, :, sl]
            pooled = _window_sum(u, w, rows, lo, hi, False) / _window_count(w, rows, lo, hi) - u
            mixed = _dot(pooled.astype(BF16), pw_ref[gi])
            g = pg_ref[0, :, sl]
            y_ref[0, :, sl] = (mixed * ps_ref[:, sl] * (g * _sig(g))).astype(BF16)

    tok = pl.BlockSpec((1, L, POOL_WIDTH), lambda b: (b, 0, 0))
    return pl.pallas_call(
        body, name=name, grid=(B,),
        in_specs=[tok, tok, _full(pw.shape), _full(ps.shape)],
        out_specs=tok, out_shape=jax.ShapeDtypeStruct((B, L, POOL_WIDTH), BF16),
        compiler_params=_params(("parallel",), VMEM_LIMIT),
    )(px, pg, pw, ps)


def pool_bwd(px, pg, pw, ps, dy, n_ctx, name):
    B, L, _ = px.shape

    def body(px_ref, pg_ref, pw_ref, ps_ref, dy_ref, dpx_ref, dpg_ref, dpw_ref, dps_ref):
        first = pl.program_id(0) == 0
        rows = lax.broadcasted_iota(jnp.int32, (L, POOL_GROUP), 0)
        lo, hi = _seg_bounds(rows, n_ctx, L)
        for gi, w in enumerate(POOL_WINDOWS):
            sl = slice(POOL_GROUP * gi, POOL_GROUP * (gi + 1))
            u = px_ref[0, :, sl]
            cnt = _window_count(w, rows, lo, hi)
            pooled = (_window_sum(u, w, rows, lo, hi, False) / cnt - u).astype(BF16)
            mixed = _dot(pooled, pw_ref[gi])
            silu, dsilu = _silu_and_grad(pg_ref[0, :, sl])
            sc = ps_ref[:, sl]
            dyv = dy_ref[0, :, sl]
            _acc(dps_ref.at[:, sl], jnp.sum(dyv * mixed * silu, axis=0, keepdims=True), first)
            dpg_ref[0, :, sl] = (dyv * mixed * sc * dsilu).astype(BF16)
            dmixed = (dyv * sc * silu).astype(BF16)
            _acc(dpw_ref.at[gi], _dot_tn(pooled, dmixed), first)
            dpooled = _dot_nt(dmixed, pw_ref[gi])
            dpx_ref[0, :, sl] = (_window_sum(dpooled / cnt, w, rows, lo, hi, True) - dpooled).astype(BF16)

    tok = pl.BlockSpec((1, L, POOL_WIDTH), lambda b: (b, 0, 0))
    return pl.pallas_call(
        body, name=name, grid=(B,),
        in_specs=[tok, tok, _full(pw.shape), _full(ps.shape), tok],
        out_specs=[tok, tok, _full(pw.shape), _full(ps.shape)],
        out_shape=[jax.ShapeDtypeStruct((B, L, POOL_WIDTH), BF16)] * 2 + [jax.ShapeDtypeStruct(pw.shape, F32), jax.ShapeDtypeStruct(ps.shape, F32)],
        compiler_params=_params(("arbitrary",), VMEM_LIMIT),
    )(px, pg, pw, ps, dy)


_SCAN_STEPS = (1, 2, 4, 8, 16, 32)


def _cum_fwd(x, r):
    for s in _SCAN_STEPS:
        x = x + jnp.where(r >= s, pltpu.roll(x, s, 0), 0.0)
    return x


def _cum_bwd(x, r):
    n = x.shape[0]
    for s in _SCAN_STEPS:
        x = x + jnp.where(r + s < GLA_CHUNK, pltpu.roll(x, n - s, 0), 0.0)
    return x


def _log_sigmoid(x):
    return jnp.minimum(x, 0.0) - jnp.log(1.0 + jnp.exp(-jnp.abs(x)))


def _gla_decays(lr, w_ref, b_ref, r, reverse):
    pre = _dot(lr, w_ref[...]) + b_ref[...]
    a = _log_sigmoid(pre) / GLA_TAU
    pf = _cum_fwd(a, r)
    sf = _cum_bwd(a, r)
    tot = pf + sf - a
    return pre, a, (sf if reverse else pf), tot


def gla_prep_fwd(zlr, zq, zk, waf, wab, baf, bab, name):
    T = zlr.shape[0]

    def body(lr_ref, q_ref, k_ref, waf_ref, wab_ref, baf_ref, bab_ref, qf_ref, kf_ref, ksf_ref, tf_ref, qb_ref, kb_ref, ksb_ref, tb_ref):
        r = lax.broadcasted_iota(jnp.int32, (TOKEN_BLOCK, GLA_KW), 0) % GLA_CHUNK
        lr = lr_ref[...].astype(BF16)
        q = q_ref[...] * GLA_DK ** -0.5
        k = k_ref[...]
        for rev, w_ref, b_ref, qo, ko, kso, to in ((False, waf_ref, baf_ref, qf_ref, kf_ref, ksf_ref, tf_ref),
                                                   (True, wab_ref, bab_ref, qb_ref, kb_ref, ksb_ref, tb_ref)):
            _, _, b, tot = _gla_decays(lr, w_ref, b_ref, r, rev)
            qo[...] = (q * jnp.exp(b)).astype(BF16)
            ko[...] = (k * jnp.exp(-b)).astype(BF16)
            kso[...] = (k * jnp.exp(tot - b)).astype(BF16)
            to[...] = tot

    tok = lambda n: pl.BlockSpec((TOKEN_BLOCK, n), lambda i: (i, 0))
    outs = [jax.ShapeDtypeStruct((T, GLA_KW), BF16)] * 3 + [jax.ShapeDtypeStruct((T, GLA_KW), F32)]
    return pl.pallas_call(
        body, name=name, grid=(T // TOKEN_BLOCK,),
        in_specs=[tok(LANES), tok(GLA_KW), tok(GLA_KW), _full(waf.shape), _full(wab.shape), _full(baf.shape), _full(bab.shape)],
        out_specs=[tok(GLA_KW)] * 8, out_shape=outs + outs,
        compiler_params=_params(("parallel",)),
    )(zlr, zq, zk, waf, wab, baf, bab)


def gla_prep_bwd(zlr, zq, zk, waf, wab, baf, bab, gf, gb, name):
    T = zlr.shape[0]

    def body(lr_ref, q_ref, k_ref, waf_ref, wab_ref, baf_ref, bab_ref, dqf, dkf, dksf, ddf, dqb, dkb, dksb, ddb,
             dlr_ref, dq_ref, dk_ref, dwaf_ref, dwab_ref, dbaf_ref, dbab_ref):
        first = pl.program_id(0) == 0
        r = lax.broadcasted_iota(jnp.int32, (TOKEN_BLOCK, GLA_KW), 0) % GLA_CHUNK
        lr = lr_ref[...].astype(BF16)
        q = q_ref[...] * GLA_DK ** -0.5
        k = k_ref[...]
        dq_tot = None
        dk_tot = None
        dlr = None
        for rev, w_ref, b_ref, dqt, dkt, dks, ddec, dw_ref, db_ref in (
                (False, waf_ref, baf_ref, dqf, dkf, dksf, ddf, dwaf_ref, dbaf_ref),
                (True, wab_ref, bab_ref, dqb, dkb, dksb, ddb, dwab_ref, dbab_ref)):
            pre, _, b, tot = _gla_decays(lr, w_ref, b_ref, r, rev)
            e1 = jnp.exp(b)
            e2 = jnp.exp(-b)
            e3 = jnp.exp(tot - b)
            dqt_v = dqt[...]
            dkt_v = dkt[...]
            dks_v = dks[...]
            dq = dqt_v * e1
            dk = dkt_v * e2 + dks_v * e3
            g3 = dks_v * (k * e3)
            d_b = dqt_v * (q * e1) - dkt_v * (k * e2) - g3
            d_tot = _cum_fwd(g3, r) + _cum_bwd(g3, r) - g3 + ddec[...] * jnp.exp(tot)
            da = (_cum_fwd(d_b, r) if rev else _cum_bwd(d_b, r)) + d_tot
            dpre = (da * (_sig(-pre) / GLA_TAU)).astype(BF16)
            t = _dot_nt(dpre, w_ref[...])
            dlr = t if dlr is None else dlr + t
            _acc(dw_ref, _dot_tn(lr, dpre), first)
            _acc(db_ref, jnp.sum(dpre.astype(F32), axis=0, keepdims=True), first)
            dq_tot = dq if dq_tot is None else dq_tot + dq
            dk_tot = dk if dk_tot is None else dk_tot + dk
        dlr_ref[...] = dlr.astype(BF16)
        dq_ref[...] = (dq_tot * GLA_DK ** -0.5).astype(BF16)
        dk_ref[...] = dk_tot.astype(BF16)

    tok = lambda n: pl.BlockSpec((TOKEN_BLOCK, n), lambda i: (i, 0))
    return pl.pallas_call(
        body, name=name, grid=(T // TOKEN_BLOCK,),
        in_specs=[tok(LANES), tok(GLA_KW), tok(GLA_KW), _full(waf.shape), _full(wab.shape), _full(baf.shape), _full(bab.shape)] + [tok(GLA_KW)] * 8,
        out_specs=[tok(LANES), tok(GLA_KW), tok(GLA_KW), _full(waf.shape), _full(wab.shape), _full(baf.shape), _full(bab.shape)],
        out_shape=[jax.ShapeDtypeStruct((T, LANES), BF16), jax.ShapeDtypeStruct((T, GLA_KW), BF16), jax.ShapeDtypeStruct((T, GLA_KW), BF16),
                   jax.ShapeDtypeStruct(waf.shape, F32), jax.ShapeDtypeStruct(wab.shape, F32), jax.ShapeDtypeStruct(baf.shape, F32),
                   jax.ShapeDtypeStruct(bab.shape, F32)],
        compiler_params=_params(("arbitrary",)),
    )(zlr, zq, zk, waf, wab, baf, bab, *gf, *gb)


def _chunk_order(nc, n_ctx_chunks, reverse):
    if not reverse:
        return lambda c: c
    return lambda c: jnp.where(c < n_ctx_chunks, n_ctx_chunks - 1 - c, nc + n_ctx_chunks - 1 - c)


def _head_mask(shape, h):
    lane = lax.broadcasted_iota(jnp.int32, shape, 1)
    return jnp.logical_and(lane >= GLA_DK * h, lane < GLA_DK * (h + 1))


def _tri_mask4(reverse):
    ri = lax.broadcasted_iota(jnp.int32, (GLA_CHUNK, GLA_HEADS * GLA_CHUNK), 0)
    ci = lax.broadcasted_iota(jnp.int32, (GLA_CHUNK, GLA_HEADS * GLA_CHUNK), 1) % GLA_CHUNK
    return (ri <= ci) if reverse else (ri >= ci)


def _block_diag(x, rb, cb):
    x4 = jnp.concatenate([x] * GLA_HEADS, axis=0)
    r = lax.broadcasted_iota(jnp.int32, x4.shape, 0) // rb
    c = lax.broadcasted_iota(jnp.int32, x4.shape, 1) // cb
    return jnp.where(r == c, x4, jnp.zeros_like(x4))


def _diag_blocks(f, rb, cb):
    c = lax.broadcasted_iota(jnp.int32, (rb, GLA_HEADS * cb), 1) // cb
    out = None
    for h in range(GLA_HEADS):
        t = jnp.where(c == h, f[rb * h:rb * (h + 1)], 0.0)
        out = t if out is None else out + t
    return out


def gla_scan_fwd(dirs, v, n_ctx, name):
    B, L, _ = v.shape
    C = GLA_CHUNK
    nc = L // C
    orders = [_chunk_order(nc, n_ctx // C, rev) for rev in (False, True)]

    def body(qf, kf, ksf, tf, vf, qb, kb, ksb, tb, vb, of, ssf, ob, ssb, stf, stb):
        @pl.when(pl.program_id(1) == 0)
        def _():
            stf[...] = jnp.zeros_like(stf)
            stb[...] = jnp.zeros_like(stb)

        step(qf, kf, ksf, vf, tf, of, ssf, stf, False)
        step(qb, kb, ksb, vb, tb, ob, ssb, stb, True)

    def step(q_ref, k_ref, ks_ref, v_ref, tot_ref, o_ref, ss_ref, st, reverse):
        S = st[...]
        ss_ref[0, 0] = S
        q = q_ref[0]
        v = v_ref[0]
        k4 = _block_diag(k_ref[0], GLA_CHUNK, GLA_DK)
        v4 = _block_diag(v.astype(BF16), GLA_CHUNK, GLA_DV)
        s4 = _block_diag(S.astype(BF16), GLA_DV, GLA_DK)
        P = jnp.where(_tri_mask4(reverse), _dot_nt(q, k4), 0.0)
        o_ref[0] = _dot(P.astype(BF16), v4) + _dot_nt(q, s4)
        st[...] = jnp.exp(tot_ref[0, 0:1, :]) * S + _diag_blocks(_dot(v.T.astype(BF16), ks_ref[0]), GLA_DV, GLA_DK)

    in_specs, out_specs, out_shape = [], [], []
    for order in orders:
        tok = lambda n, order=order: pl.BlockSpec((1, C, n), lambda b, c: (b, order(c), 0))
        in_specs += [tok(GLA_KW), tok(GLA_KW), tok(GLA_KW), tok(GLA_KW), tok(GLA_WIDTH)]
        out_specs += [tok(GLA_WIDTH), pl.BlockSpec((1, 1, GLA_DV, GLA_KW), lambda b, c, order=order: (b, order(c), 0, 0))]
        out_shape += [jax.ShapeDtypeStruct((B, L, GLA_WIDTH), F32), jax.ShapeDtypeStruct((B, nc, GLA_DV, GLA_KW), F32)]
    outs = pl.pallas_call(
        body, name=name, grid=(B, nc), in_specs=in_specs, out_specs=out_specs, out_shape=out_shape,
        scratch_shapes=[pltpu.VMEM((GLA_DV, GLA_KW), F32)] * 2,
        compiler_params=_params(("parallel", "arbitrary")),
    )(*dirs[0], v, *dirs[1], v)
    return outs[:2], outs[2:]


def gla_scan_bwd(dirs, v, do, n_ctx, name, hosted=None):
    B, L, _ = v.shape
    C = GLA_CHUNK
    nc = L // C
    orders = []
    for rev in (False, True):
        fwd_order = _chunk_order(nc, n_ctx // C, rev)
        orders.append(lambda c, fwd_order=fwd_order: fwd_order(nc - 1 - c))

    nx = hosted.n if hosted else 0

    def body(*refs):
        qf, kf, ksf, tf, ssf, vf, dof, qb, kb, ksb, tb, ssb, vb, dob = refs[:14]
        dqf, dkf, dksf, dvf, ddf, dqb, dkb, dksb, dvb, ddb = refs[14 + nx:24 + nx]
        dstf, dstb = refs[24 + 2 * nx:26 + 2 * nx]
        if hosted:
            hosted.run(refs[14:14 + nx], refs[24 + nx:24 + 2 * nx], refs[26 + 2 * nx:], pl.program_id(0) * nc + pl.program_id(1), B * nc)

        @pl.when(pl.program_id(1) == 0)
        def _():
            dstf[...] = jnp.zeros_like(dstf)
            dstb[...] = jnp.zeros_like(dstb)

        step(qf, kf, ksf, vf, tf, ssf, dof, dqf, dkf, dksf, dvf, ddf, dstf, False)
        step(qb, kb, ksb, vb, tb, ssb, dob, dqb, dkb, dksb, dvb, ddb, dstb, True)

    def step(q_ref, k_ref, ks_ref, v_ref, tot_ref, ss_ref, do_ref, dq_ref, dk_ref, dks_ref, dv_ref, dd_ref, dst, reverse):
        dSn = dst[...]
        S = ss_ref[0, 0]
        q = q_ref[0]
        vb = v_ref[0].astype(BF16)
        dob = do_ref[0].astype(BF16)
        k4 = _block_diag(k_ref[0], GLA_CHUNK, GLA_DK)
        v4 = _block_diag(vb, GLA_CHUNK, GLA_DV)
        s4 = _block_diag(S.astype(BF16), GLA_DV, GLA_DK)
        ds4 = _block_diag(dSn.astype(BF16), GLA_DV, GLA_DK)
        tri = _tri_mask4(reverse)
        P = jnp.where(tri, _dot_nt(q, k4), 0.0).astype(BF16)
        dP = jnp.where(tri, _dot_nt(dob, v4), 0.0).astype(BF16)
        dq_ref[0] = _dot(dob, s4) + _dot(dP, k4)
        dk_ref[0] = _diag_blocks(_dot_tn(dP, q), GLA_CHUNK, GLA_DK)
        dv_ref[0] = _diag_blocks(_dot_tn(P, dob), GLA_CHUNK, GLA_DV) + _dot_nt(ks_ref[0], ds4)
        dks_ref[0] = _dot(vb, ds4)
        dd_ref[0] = jnp.broadcast_to(jnp.sum(dSn * S, axis=0, keepdims=True), (C, GLA_KW))
        dst[...] = jnp.exp(tot_ref[0, 0:1, :]) * dSn + _diag_blocks(_dot_tn(dob, q), GLA_DV, GLA_DK)

    in_specs, out_specs, out_shape = [], [], []
    for order in orders:
        tok = lambda n, order=order: pl.BlockSpec((1, C, n), lambda b, c: (b, order(c), 0))
        in_specs += [tok(GLA_KW), tok(GLA_KW), tok(GLA_KW), tok(GLA_KW),
                     pl.BlockSpec((1, 1, GLA_DV, GLA_KW), lambda b, c, order=order: (b, order(c), 0, 0)), tok(GLA_WIDTH), tok(GLA_WIDTH)]
        out_specs += [tok(GLA_KW), tok(GLA_KW), tok(GLA_KW), tok(GLA_WIDTH), tok(GLA_KW)]
        out_shape += [jax.ShapeDtypeStruct((B, L, GLA_KW), F32)] * 3 + [jax.ShapeDtypeStruct((B, L, GLA_WIDTH), F32), jax.ShapeDtypeStruct((B, L, GLA_KW), F32)]
    outs = pl.pallas_call(
        body, name=name, grid=(B, nc), in_specs=in_specs + (hosted.in_specs if hosted else []),
        out_specs=out_specs + (hosted.out_specs if hosted else []), out_shape=out_shape + (hosted.out_shape if hosted else []),
        scratch_shapes=[pltpu.VMEM((GLA_DV, GLA_KW), F32)] * 2 + (hosted.scratch if hosted else []),
        compiler_params=_params(("arbitrary", "arbitrary") if hosted else ("parallel", "arbitrary")),
    )(*dirs[0], v, do, *dirs[1], v, do, *(hosted.xs if hosted else []))
    return outs[:5], outs[5:10], list(outs[10:])


def gla_out_fwd(of, ob, gn, zg, name):
    T = of.shape[0]

    def body(of_ref, ob_ref, gn_ref, g_ref, y_ref):
        for h in range(GLA_HEADS):
            sl = slice(GLA_DV * h, GLA_DV * (h + 1))
            xn, _ = _rms_rows(of_ref[:, sl] + ob_ref[:, sl])
            g = g_ref[:, sl]
            y_ref[:, sl] = (xn * gn_ref[...] * (g * _sig(g))).astype(BF16)

    tok = pl.BlockSpec((TOKEN_BLOCK, GLA_WIDTH), lambda i: (i, 0))
    return pl.pallas_call(
        body, name=name, grid=(T // TOKEN_BLOCK,),
        in_specs=[tok, tok, _full(gn.shape), tok], out_specs=tok,
        out_shape=jax.ShapeDtypeStruct((T, GLA_WIDTH), BF16),
        compiler_params=_params(("parallel",)),
    )(of, ob, gn, zg)


def gla_out_bwd(of, ob, gn, zg, dy, name):
    T = of.shape[0]

    def body(of_ref, ob_ref, gn_ref, g_ref, dy_ref, do_ref, dzg_ref, dgn_ref):
        first = pl.program_id(0) == 0
        gn_v = gn_ref[...]
        dgn = None
        for h in range(GLA_HEADS):
            sl = slice(GLA_DV * h, GLA_DV * (h + 1))
            xn, r = _rms_rows(of_ref[:, sl] + ob_ref[:, sl])
            silu, dsilu = _silu_and_grad(g_ref[:, sl])
            dyv = dy_ref[:, sl]
            dzg_ref[:, sl] = (dyv * xn * gn_v * dsilu).astype(BF16)
            dn = dyv * silu
            t = jnp.sum(dn * xn, axis=0, keepdims=True)
            dgn = t if dgn is None else dgn + t
            do_ref[:, sl] = _rms_rows_bwd(dn * gn_v, xn, r)
        _acc(dgn_ref, dgn, first)

    tok = pl.BlockSpec((TOKEN_BLOCK, GLA_WIDTH), lambda i: (i, 0))
    return pl.pallas_call(
        body, name=name, grid=(T // TOKEN_BLOCK,),
        in_specs=[tok, tok, _full(gn.shape), tok, tok], out_specs=[tok, tok, _full(gn.shape)],
        out_shape=[jax.ShapeDtypeStruct((T, GLA_WIDTH), F32), jax.ShapeDtypeStruct((T, GLA_WIDTH), BF16), jax.ShapeDtypeStruct(gn.shape, F32)],
        compiler_params=_params(("arbitrary",)),
    )(of, ob, gn, zg, dy)


def merge_post_fwd(ys, zm, wbs, wo, x2, pg, ms, nb, name):
    T = x2.shape[0]

    def body(y0, y1, y2, zm_ref, w0, w1, w2, wo_ref, x_ref, pg_ref, gate_ref, xn_ref, out_ref, mg_ref):
        merged = None
        for i, (y_ref, w_ref) in enumerate(((y0, w0), (y1, w1), (y2, w2))):
            t = _sig(zm_ref[:, D_MODEL * i:D_MODEL * (i + 1)]) * _dot(y_ref[...], w_ref[...])
            merged = t if merged is None else merged + t
        mb = merged.astype(BF16)
        mg_ref[...] = mb
        out = _dot(mb, wo_ref[...])
        out_ref[...] = out
        on, _ = _rms_rows(out)
        xn_ref[...] = x_ref[...] + gate_ref[0] * (on * pg_ref[...])

    tok = lambda n: pl.BlockSpec((TOKEN_BLOCK, n), lambda i: (i, 0))
    return pl.pallas_call(
        body, name=name, grid=(T // TOKEN_BLOCK,),
        in_specs=[tok(512)] * 3 + [tok(3 * D_MODEL)] + [_full(w.shape) for w in wbs] + [_full(wo.shape), tok(D_MODEL), _full(pg.shape), _mod_spec(nb, 2)],
        out_specs=[tok(D_MODEL)] * 3,
        out_shape=[jax.ShapeDtypeStruct((T, D_MODEL), F32), jax.ShapeDtypeStruct((T, D_MODEL), F32), jax.ShapeDtypeStruct((T, D_MODEL), BF16)],
        compiler_params=_params(("parallel",), VMEM_LIMIT),
    )(*ys, zm, *wbs, wo, x2, pg, ms)


def merge_post_bwd(dxn, out, ys, zm, wbs, wo, pg, ms, nb, name):
    T = dxn.shape[0]
    nrow = ms.shape[0]
    row = _mod_row(nb)

    def body(dxn_ref, out_ref, y0, y1, y2, zm_ref, w0, w1, w2, wo_ref, pg_ref, gate_ref,
             dy0, dy1, dy2, dzm_ref, dout_ref, dp0, dp1, dp2, dgate_ref, dpg_ref):
        i = pl.program_id(0)
        dxn_v = dxn_ref[...]
        on, r = _rms_rows(out_ref[...])
        pg_v = pg_ref[...]
        _acc(dgate_ref.at[0], jnp.sum(dxn_v * on * pg_v, axis=0, keepdims=True), (i % nb) <= 1)
        dn = dxn_v * gate_ref[0]
        _acc(dpg_ref, jnp.sum(dn * on, axis=0, keepdims=True), i == 0)
        dout = _rms_rows_bwd(dn * pg_v, on, r).astype(BF16)
        dout_ref[...] = dout
        dmerged = _dot_nt(dout, wo_ref[...])
        for j, (y_ref, w_ref, dy_ref, dp_ref) in enumerate(((y0, w0, dy0, dp0), (y1, w1, dy1, dp1), (y2, w2, dy2, dp2))):
            sl = slice(D_MODEL * j, D_MODEL * (j + 1))
            g = _sig(zm_ref[:, sl])
            p = _dot(y_ref[...], w_ref[...])
            dzm_ref[:, sl] = (dmerged * p * g * (1.0 - g)).astype(BF16)
            dp = (dmerged * g).astype(BF16)
            dp_ref[...] = dp
            dy_ref[...] = _dot_nt(dp, w_ref[...])

    tok = lambda n: pl.BlockSpec((TOKEN_BLOCK, n), lambda i: (i, 0))
    return pl.pallas_call(
        body, name=name, grid=(T // TOKEN_BLOCK,),
        in_specs=[tok(D_MODEL), tok(D_MODEL)] + [tok(512)] * 3 + [tok(3 * D_MODEL)] + [_full(w.shape) for w in wbs] + [_full(wo.shape), _full(pg.shape), _mod_spec(nb, 2)],
        out_specs=[tok(512)] * 3 + [tok(3 * D_MODEL), tok(D_MODEL)] + [tok(D_MODEL)] * 3 + [pl.BlockSpec((1, 1, D_MODEL), lambda i: (row(i), 0, 0)), _full(pg.shape)],
        out_shape=[jax.ShapeDtypeStruct((T, 512), F32)] * 3 + [jax.ShapeDtypeStruct((T, 3 * D_MODEL), BF16), jax.ShapeDtypeStruct((T, D_MODEL), BF16)]
        + [jax.ShapeDtypeStruct((T, D_MODEL), BF16)] * 3 + [jax.ShapeDtypeStruct((nrow, 1, D_MODEL), F32), jax.ShapeDtypeStruct(pg.shape, F32)],
        compiler_params=_params(("arbitrary",), VMEM_LIMIT),
    )(dxn, out, *ys, zm, *wbs, wo, pg, ms)


def loss_head(y2, tgt2, nb, name):
    T = y2.shape[0]
    nlat = nb - 1

    def body(y_ref, t_ref, dy_ref, loss_ref, acc):
        i = pl.program_id(0)
        is_lat = (i % nb) > 0

        @pl.when(i == 0)
        def _():
            acc[...] = jnp.zeros_like(acc)

        @pl.when(is_lat)
        def _():
            e = y_ref[...] - t_ref[...]
            dy_ref[...] = e * (1.0 / D_MODEL)
            acc[...] += jnp.sum(e * e, axis=0, keepdims=True)

        @pl.when(jnp.logical_not(is_lat))
        def _():
            dy_ref[...] = jnp.zeros_like(dy_ref)

        @pl.when(i == pl.num_programs(0) - 1)
        def _():
            loss_ref[...] = jnp.sum(acc[...], axis=1, keepdims=True) * (0.5 / D_MODEL)

    tok = pl.BlockSpec((TOKEN_BLOCK, D_MODEL), lambda i: (i, 0))
    tgt = pl.BlockSpec((TOKEN_BLOCK, D_MODEL), lambda i: ((i // nb) * nlat + jnp.maximum(i % nb - 1, 0), 0))
    return pl.pallas_call(
        body, name=name, grid=(T // TOKEN_BLOCK,),
        in_specs=[tok, tgt], out_specs=[tok, _full((1, 1))],
        out_shape=[jax.ShapeDtypeStruct((T, D_MODEL), F32), jax.ShapeDtypeStruct((1, 1), F32)],
        scratch_shapes=[pltpu.VMEM((1, D_MODEL), F32)],
        compiler_params=_params(("arbitrary",)),
    )(y2, tgt2)


_IN_OFFS = tuple(int(o) for o in np.cumsum((0,) + IN_SIZES))
_IN_GROUPS = (("a", 0, 416, 512), ("mg", 416, 512, 512), ("px", 928, 512, 512), ("pg", 1440, 512, 512), ("gq", 1952, 256, 256),
              ("gk", 2208, 256, 256), ("gv", 2464, 512, 512), ("lr", 2976, 32, 128), ("gg", 3008, 512, 512), ("m", 3520, 3072, 3072))


def _pad_cols(w, n):
    return w if w.shape[1] == n else jnp.pad(w, ((0, 0), (0, n - w.shape[1])))


def layer_weights(w_in, w_uq, w_ukv, af_w2, ab_w2, wbm, wbp, wbg, w_out):
    W = {}
    for nm, off, n, npad in _IN_GROUPS:
        W["in_" + nm] = _pad_cols(w_in[:, off:off + n], npad)
    uq = w_uq.reshape(MLA_Q_RANK, MLA_HEADS, MLA_NOPE + MLA_ROPE)
    W["qn"] = jnp.pad(uq[:, :, :MLA_NOPE], ((0, 0), (0, 0), (0, LANES - MLA_NOPE))).reshape(MLA_Q_RANK, MLA_HEADS * LANES)
    W["qr"] = jnp.pad(uq[:, :, MLA_NOPE:], ((0, 0), (0, 0), (0, LANES - MLA_ROPE))).reshape(MLA_Q_RANK, MLA_HEADS * LANES)
    W["kv"] = w_ukv
    W["af"] = jnp.pad(af_w2, ((0, LANES - GLA_GATE_RANK), (0, 0)))
    W["ab"] = jnp.pad(ab_w2, ((GLA_GATE_RANK, LANES - 2 * GLA_GATE_RANK), (0, 0)))
    W["bm"], W["bp"], W["bg"], W["out"] = wbm, wbp, wbg, w_out
    return W


def rope_tables(L, n_ctx):
    t = np.arange(L - n_ctx)
    half = MLA_ROPE // 2
    inv = ROPE_BASE ** (-np.arange(0, half, 2, dtype=np.float32) / half)
    ang_r = (t // GRID_W).astype(np.float32)[:, None] * inv
    ang_c = (t % GRID_W).astype(np.float32)[:, None] * inv
    ang = jnp.asarray(np.concatenate([ang_r, ang_r, ang_c, ang_c], axis=-1), F32)
    cos = jnp.ones((L, LANES), F32).at[n_ctx:, :MLA_ROPE].set(jnp.cos(ang))
    sin = jnp.zeros((L, LANES), F32).at[n_ctx:, :MLA_ROPE].set(jnp.sin(ang))
    return cos, sin


def layer_fwd(x2, ms, W, P, cos, sin, B, L, n_ctx, tag, hosted=None):
    nb = L // TOKEN_BLOCK
    r3 = lambda a: a.reshape(B, L, a.shape[-1])
    r2 = lambda a: a.reshape(B * L, a.shape[-1])
    h = norm_mod_fwd(x2, P["pre"], ms, nb, tag + "norm_mod")
    names = [g[0] for g in _IN_GROUPS[:-1]]
    z = dict(zip(names, mm_multi(h, [W["in_" + n] for n in names], [F32] * len(names), tag + "in_proj")))
    (z["m"],) = mm_multi(h, [W["in_m"]], [F32], tag + "in_proj_merge")
    qn, qr, kv, kr = mla_prep_fwd(z["a"], P["qg"], P["kvg"], W["qn"], W["qr"], W["kv"], cos, sin, nb, tag + "mla_prep")
    ya, y_mla, lse, carried = attention_fwd(r3(qn), r3(qr), r3(kv), r3(kr), r3(z["mg"]), n_ctx, tag + "attention", hosted)
    y_pool = pool_fwd(r3(z["px"]), r3(z["pg"]), P["pw"], P["ps"], n_ctx, tag + "pool")
    qf, kf, ksf, tf, qb, kb, ksb, tb = gla_prep_fwd(z["lr"], z["gq"], z["gk"], W["af"], W["ab"], P["baf"], P["bab"], tag + "gla_prep")
    (of, ssf), (ob, ssb) = gla_scan_fwd([(r3(qf), r3(kf), r3(ksf), r3(tf)), (r3(qb), r3(kb), r3(ksb), r3(tb))], r3(z["gv"]), n_ctx, tag + "gla_scan")
    y_gla = gla_out_fwd(r2(of), r2(ob), P["gn"], z["gg"], tag + "gla_out")
    ys = [r2(y_mla), r2(y_pool), y_gla]
    x_new, out, merged = merge_post_fwd(ys, z["m"], [W["bm"], W["bp"], W["bg"]], W["out"], x2, P["post"], ms, nb, tag + "merge_post")
    res = dict(x2=x2, h=h, z=z, qn=qn, qr=qr, kv=kv, kr=kr, ya=ya, lse=lse, ys=ys, gla_f=(qf, kf, ksf, tf, ssf), gla_b=(qb, kb, ksb, tb, ssb),
               of=of, ob=ob, out=out, merged=merged)
    return x_new, res, carried


def layer_bwd(dxn, res, ms, W, P, cos, sin, B, L, n_ctx, tag, hosted=None, host_own=None):
    nb = L // TOKEN_BLOCK
    r3 = lambda a: a.reshape(B, L, a.shape[-1])
    r2 = lambda a: a.reshape(B * L, a.shape[-1])
    z = res["z"]
    ys = res["ys"]
    wbs = [W["bm"], W["bp"], W["bg"]]
    dy0, dy1, dy2, dzm, dout, dp0, dp1, dp2, dgate, dpost = merge_post_bwd(dxn, res["out"], ys, z["m"], wbs, W["out"], P["post"], ms, nb, tag + "merge_post_bwd")
    G = {"out": mm_dw(res["merged"], dout, tag + "dw_out"), "post": dpost}
    for nm, y, dp in zip(("bm", "bp", "bg"), ys, (dp0, dp1, dp2)):
        G[nm] = mm_dw(y, dp, tag + "dw_" + nm)
    g = {n: _natural_grad(G, n) for n in GRADS_EARLY}
    own = {}
    dz = {"m": dzm}
    do, dz["gg"], G["gn"] = gla_out_bwd(r2(res["of"]), r2(res["ob"]), P["gn"], z["gg"], dy2, tag + "gla_out_bwd")
    carrier = host_own(GRADS_EARLY, g) if host_own else None
    *grads, arrived = gla_scan_bwd([(r3(qt), r3(kt), r3(ks), r3(tot), ss) for qt, kt, ks, tot, ss in (res["gla_f"], res["gla_b"])],
                                   r3(z["gv"]), r3(do), n_ctx, tag + "gla_scan_bwd", carrier)
    own.update(zip(GRADS_EARLY, arrived))
    gf = [r2(a) for a in grads[0]]
    gb = [r2(a) for a in grads[1]]
    dz["lr"], dz["gq"], dz["gk"], G["af"], G["ab"], G["baf"], G["bab"] = gla_prep_bwd(
        z["lr"], z["gq"], z["gk"], W["af"], W["ab"], P["baf"], P["bab"], gf[:3] + gf[4:], gb[:3] + gb[4:], tag + "gla_prep_bwd")
    dz["gv"] = add_cast(gf[3], gb[3], tag + "gla_dv")
    dpx, dpg, G["pw"], G["ps"] = pool_bwd(r3(z["px"]), r3(z["pg"]), P["pw"], P["ps"], r3(dy1), n_ctx, tag + "pool_bwd")
    dz["px"], dz["pg"] = r2(dpx), r2(dpg)
    dqn, dqr, dkv, dkr, dzmg, got = attention_bwd(r3(res["qn"]), r3(res["qr"]), r3(res["kv"]), r3(res["kr"]), r3(z["mg"]), res["ya"], res["lse"],
                                                  r3(dy0), n_ctx, tag + "attention_bwd", hosted)
    dz["mg"] = r2(dzmg)
    dz["a"], G["qn"], G["qr"], G["kv"], G["qg"], G["kvg"] = mla_prep_bwd(
        r2(dqn), r2(dqr), r2(dkv), r2(dkr), z["a"], P["qg"], P["kvg"], W["qn"], W["qr"], W["kv"], cos, sin, nb, tag + "mla_prep_bwd")
    names = [grp[0] for grp in _IN_GROUPS]
    for n in names:
        G["in_" + n] = mm_dw(res["h"], dz[n], tag + "dw_in_" + n)
    g.update({n: _natural_grad(G, n) for n in GRADS_LATE})
    carrier = host_own(GRADS_LATE, g) if host_own else None
    dh, arrived = mm_dx([dz[n] for n in names], [W["in_" + n] for n in names], tag + "in_proj_dx", hosted=carrier)
    own.update(zip(GRADS_LATE, arrived))
    dx, dshift, dscale, G["pre"] = norm_mod_bwd(dh, res["x2"], P["pre"], ms, dxn, nb, tag + "norm_mod_bwd")
    g.update({n: _natural_grad(G, n) for n in GRADS_REPLICATED})
    dms = jnp.concatenate([dshift, dscale, dgate], axis=-1)
    return dx, g, dms, got, own


def add_cast(a, b, name):
    T, n = a.shape

    def body(a_ref, b_ref, o_ref):
        o_ref[...] = (a_ref[...] + b_ref[...]).astype(BF16)

    tok = pl.BlockSpec((TOKEN_BLOCK, n), lambda i: (i, 0))
    return pl.pallas_call(body, name=name, grid=(T // TOKEN_BLOCK,), in_specs=[tok, tok], out_specs=tok,
                          out_shape=jax.ShapeDtypeStruct((T, n), BF16), compiler_params=_params(("parallel",)))(a, b)


GRADS_EARLY = ("w_branch_mla", "w_branch_pool", "w_branch_gla", "w_out")
GRADS_LATE = ("w_in", "mla_w_uq", "mla_w_ukv", "gla_af_w2", "gla_ab_w2")
GRADS_REPLICATED = ("pre_norm", "post_norm", "mla_q_norm", "mla_kv_norm", "pool_w", "pool_scale", "gla_af_b", "gla_ab_b", "gla_norm")
_DIRECT = dict(mla_w_ukv="kv", w_branch_mla="bm", w_branch_pool="bp", w_branch_gla="bg", w_out="out", pool_w="pw")
_ROW = dict(pre_norm="pre", post_norm="post", mla_q_norm="qg", mla_kv_norm="kvg", pool_scale="ps", gla_af_b="baf", gla_ab_b="bab", gla_norm="gn")


def _natural_grad(G, name):
    if name == "w_in":
        parts = {off: G["in_" + nm][:, :n] for nm, off, n, npad in _IN_GROUPS}
        return jnp.concatenate([parts[o] for o in sorted(parts)], axis=1)
    if name == "mla_w_uq":
        gqn = G["qn"].reshape(MLA_Q_RANK, MLA_HEADS, LANES)[:, :, :MLA_NOPE]
        gqr = G["qr"].reshape(MLA_Q_RANK, MLA_HEADS, LANES)[:, :, :MLA_ROPE]
        return jnp.concatenate([gqn, gqr], axis=-1).reshape(MLA_Q_RANK, MLA_HEADS * (MLA_NOPE + MLA_ROPE))
    if name == "gla_af_w2":
        return G["af"][:GLA_GATE_RANK]
    if name == "gla_ab_w2":
        return G["ab"][GLA_GATE_RANK:2 * GLA_GATE_RANK]
    return G[_DIRECT[name]] if name in _DIRECT else G[_ROW[name]][0]


def local_step(x, c, ctx, c_ctx, small, loss_target, depth, layer_full, host_fwd=None, host_bwd=None, host_own=None):
    B, S, _ = x.shape
    n_ctx = ctx.shape[1]
    L = n_ctx + S
    nb = L // TOKEN_BLOCK
    cos, sin = rope_tables(L, n_ctx)
    x2 = jnp.concatenate([ctx, x], axis=1).reshape(B * L, D_MODEL)
    a8 = jnp.zeros((8, D_MODEL), F32).at[:B].set(c).at[B].set(c_ctx)
    Ws, Ps, mss, ress, mod_ws = [], [], [], [], []
    carried = None
    for l in range(depth):
        tag = f"l{l}_"
        full = layer_full(l, carried)
        W = layer_weights(full["w_in"], full["mla_w_uq"], full["mla_w_ukv"], full["gla_af_w2"], full["gla_ab_w2"],
                          full["w_branch_mla"], full["w_branch_pool"], full["w_branch_gla"], full["w_out"])
        P = dict(pre=small["pre_norm"][l][None], post=small["post_norm"][l][None], qg=small["mla_q_norm"][l][None], kvg=small["mla_kv_norm"][l][None],
                 pw=small["pool_w"][l].astype(BF16), ps=small["pool_scale"][l][None], baf=small["gla_af_b"][l][None], bab=small["gla_ab_b"][l][None],
                 gn=small["gla_norm"][l][None])
        mod8 = mod_fwd(a8, full["mod_w"], small["mod_b"][l][None], tag + "mod")
        ms = jnp.stack([jnp.broadcast_to(mod8[B], (B, 3 * D_MODEL)), mod8[:B]], axis=1).reshape(2 * B, 1, 3 * D_MODEL)
        x2, res, carried = layer_fwd(x2, ms, W, P, cos, sin, B, L, n_ctx, tag, host_fwd(l) if host_fwd else None)
        Ws.append(W), Ps.append(P), mss.append(ms), ress.append(res), mod_ws.append(full["mod_w"])
    dx, loss = loss_head(x2, loss_target.reshape(B * S, D_MODEL), nb, "loss_head")
    grads = [None] * depth
    delivered = [None] * depth
    dz8s = [None] * depth
    da8 = None
    for l in reversed(range(depth)):
        tag = f"l{l}_"
        hosted = host_bwd(l, grads[l + 1]) if host_bwd and l + 1 < depth else None
        dx, g, dms, got, own = layer_bwd(dx, ress[l], mss[l], Ws[l], Ps[l], cos, sin, B, L, n_ctx, tag, hosted, host_own(l) if host_own else None)
        if hosted:
            delivered[l + 1] = got
        if own:
            delivered[l] = own
        dms = dms.reshape(B, 2, 3 * D_MODEL)
        dz8s[l] = jnp.zeros((8, 3 * D_MODEL), F32).at[:B].set(dms[:, 1]).at[B].set(jnp.sum(dms[:, 0], axis=0))
        g_mod_b, da = mod_bwd(a8, mod_ws[l], dz8s[l], tag + "mod_bwd")
        da8 = da if da8 is None else da8 + da
        g["mod_b"] = g_mod_b[0]
        grads[l] = g
    grad_x = dx.reshape(B, L, D_MODEL)[:, n_ctx:]
    return loss, grad_x, grads, da8[B], delivered, (a8, dz8s)


_MESH_ID = pl.DeviceIdType.MESH
_HBM = pl.BlockSpec(memory_space=pltpu.HBM)


def _me_and_peers():
    mx, my, mc = lax.axis_index("x"), lax.axis_index("y"), lax.axis_index("c")
    peers = []
    for k in range(1, N_DEV):
        px, py, pc = mx ^ ((k >> 2) & 1), my ^ ((k >> 1) & 1), mc ^ (k & 1)
        peers.append(((px, py, pc), 4 * px + 2 * py + pc))
    return 4 * mx + 2 * my + mc, peers


def _comm_scratch(n):
    return [pltpu.SemaphoreType.DMA((n * (N_DEV - 1),)), pltpu.SemaphoreType.DMA((n * (N_DEV - 1),)), pltpu.SemaphoreType.DMA((n,))]


class _Gather:
    def __init__(self, x_refs, o_refs, send_sems, recv_sems, local_sems):
        self.x, self.o, self.send, self.recv, self.local = x_refs, o_refs, send_sems, recv_sems, local_sems
        self.n = len(x_refs)
        mx, my, mc = lax.axis_index("x"), lax.axis_index("y"), lax.axis_index("c")
        self.me, self.sibling, self.mc = (mx, my, mc), (mx, my, 1 - mc), mc
        self.chips = [(1 - mx, my), (mx, 1 - my), (1 - mx, 1 - my)]

    @staticmethod
    def out_shape(xs):
        return [jax.ShapeDtypeStruct((N_DEV,) + x.shape, x.dtype) for x in xs]

    def _copy(self, i, k, block, to, src=None):
        px, py, pc = block
        dst = self.o[i].at[4 * px + 2 * py + pc]
        sem = (N_DEV - 1) * i + k
        return pltpu.make_async_remote_copy(src_ref=dst if src is None else src, dst_ref=dst, send_sem=self.send.at[sem],
                                            recv_sem=self.recv.at[sem], device_id=to, device_id_type=_MESH_ID)

    def _mine(self, i):
        mx, my, mc = self.me
        return pltpu.make_async_copy(self.x[i], self.o[i].at[4 * mx + 2 * my + mc], self.local.at[i])

    def _first(self):
        out = []
        for i in range(self.n):
            out.append(self._copy(i, 0, self.me, self.sibling, src=self.x[i]))
            out += [self._copy(i, 1 + j, self.me, (*chip, self.mc), src=self.x[i]) for j, chip in enumerate(self.chips)]
        return out

    def _passed(self, j, i):
        return self._copy(i, 4 + j, (*self.chips[j], self.mc), self.sibling)

    def start(self):
        for i in range(self.n):
            self._mine(i).start()
        for cp in self._first():
            cp.start()

    def forward(self):
        for j, chip in enumerate(self.chips):
            for i in range(self.n):
                self._copy(i, 1 + j, (*chip, self.mc), self.me).wait_recv()
                self._passed(j, i).start()

    def finish(self):
        for i in range(self.n):
            self._copy(i, 0, self.sibling, self.me).wait_recv()
            for j, chip in enumerate(self.chips):
                self._copy(i, 4 + j, (*chip, 1 - self.mc), self.me).wait_recv()
        for cp in self._first():
            cp.wait_send()
        for j in range(len(self.chips)):
            for i in range(self.n):
                self._passed(j, i).wait_send()
        for i in range(self.n):
            self._mine(i).wait()


class _Scatter:
    def __init__(self, x_refs, o_refs, send_sems, recv_sems, local_sems):
        self.x, self.o, self.send, self.recv, self.local = x_refs, o_refs, send_sems, recv_sems, local_sems
        self.n = len(x_refs)
        self.me, self.peers = _me_and_peers()

    @staticmethod
    def out_shape(xs):
        return [jax.ShapeDtypeStruct(x.shape, x.dtype) for x in xs]

    def _copy(self, i, k, src_slot, dst_slot, to):
        sem = (N_DEV - 1) * i + k
        return pltpu.make_async_remote_copy(src_ref=self.x[i].at[src_slot], dst_ref=self.o[i].at[dst_slot], send_sem=self.send.at[sem],
                                            recv_sem=self.recv.at[sem], device_id=to, device_id_type=_MESH_ID)

    def _mine(self, i):
        return pltpu.make_async_copy(self.x[i].at[self.me], self.o[i].at[self.me], self.local.at[i])

    def _sends(self):
        return [self._copy(i, k, slot, self.me, peer) for k, (peer, slot) in enumerate(self.peers) for i in range(self.n)]

    def start(self):
        for i in range(self.n):
            self._mine(i).start()
        for cp in self._sends():
            cp.start()

    def forward(self):
        pass

    def finish(self):
        for k, (peer, slot) in enumerate(self.peers):
            for i in range(self.n):
                self._copy(i, k, slot, slot, peer).wait_recv()
        for cp in self._sends():
            cp.wait_send()
        for i in range(self.n):
            self._mine(i).wait()


class _Hosted:
    def __init__(self, kind, xs):
        self.kind, self.xs, self.n = kind, list(xs), len(xs)
        self.in_specs = [_HBM] * self.n
        self.out_specs = [_HBM] * self.n
        self.out_shape = kind.out_shape(self.xs)
        self.scratch = _comm_scratch(self.n)

    def run(self, x_refs, o_refs, sems, step, total):
        for when, phase in ((0, "start"), (total // 2, "forward"), (total - 1, "finish")):
            @pl.when(step == when)
            def _(phase=phase):
                getattr(self.kind(x_refs, o_refs, *sems), phase)()


def gather_blocks(xs, name):
    n = len(xs)

    def body(*refs):
        g = _Gather(refs[:n], refs[n:2 * n], *refs[2 * n:])
        g.start()
        g.forward()
        g.finish()

    return pl.pallas_call(
        body, name=name, in_specs=[_HBM] * n, out_specs=[_HBM] * n,
        out_shape=_Gather.out_shape(xs), scratch_shapes=_comm_scratch(n),
    )(*xs)


def scatter_blocks(xs, name):
    n = len(xs)

    def body(*refs):
        s = _Scatter(refs[:n], refs[n:2 * n], *refs[2 * n:])
        s.start()
        s.finish()

    return pl.pallas_call(
        body, name=name, in_specs=[_HBM] * n, out_specs=[_HBM] * n,
        out_shape=_Scatter.out_shape(xs), scratch_shapes=_comm_scratch(n),
    )(*xs)


def reduce_adamw(slots, w, m, v, name, tr=256):
    R, C = w.shape
    nl = len(slots)
    ns = slots[0].shape[0]
    rows = R // nl
    tr = min(tr, rows)
    nbl = rows // tr
    c1 = 1.0 / (1.0 - ADAM_B1 ** ADAM_STEP)
    c2 = 1.0 / (1.0 - ADAM_B2 ** ADAM_STEP)

    def body(*refs):
        w_ref, m_ref, v_ref, g_ref, d_ref, nm_ref, nv_ref = refs[nl:]
        part = pl.program_id(0) // nbl
        g = None
        for l, s_ref in enumerate(refs[:nl]):
            gl = s_ref[0].astype(F32)
            for s in range(1, ns):
                gl = gl + s_ref[s].astype(F32)
            g = gl if g is None else jnp.where(part == l, gl, g)
        nm = ADAM_B1 * m_ref[...] + (1.0 - ADAM_B1) * g
        nv = ADAM_B2 * v_ref[...] + (1.0 - ADAM_B2) * (g * g)
        g_ref[...] = g
        nm_ref[...] = nm
        nv_ref[...] = nv
        d_ref[...] = -ADAM_LR * ((nm * c1) / (jnp.sqrt(nv * c2) + ADAM_EPS) + ADAM_WD * w_ref[...])

    blk = pl.BlockSpec((tr, C), lambda i: (i, 0))
    sspecs = [pl.BlockSpec((ns, tr, C), lambda i, l=l: (0, jnp.clip(i - l * nbl, 0, nbl - 1), 0)) for l in range(nl)]
    return pl.pallas_call(
        body, name=name, grid=(R // tr,),
        in_specs=sspecs + [blk, blk, blk], out_specs=[blk] * 4,
        out_shape=[jax.ShapeDtypeStruct((R, C), F32)] * 4,
        compiler_params=_params(("parallel",), VMEM_LIMIT),
    )(*slots, w, m, v)


ARG_WEIGHTS = ("c_ctx", "mod_w", "mod_b", "pre_norm", "post_norm", "w_in", "mla_q_norm", "mla_w_uq", "mla_kv_norm", "mla_w_ukv", "pool_w",
               "pool_scale", "gla_af_w2", "gla_af_b", "gla_ab_w2", "gla_ab_b", "gla_norm", "w_branch_mla", "w_branch_pool", "w_branch_gla", "w_out")
SHARDED = ("mod_w", "w_in", "mla_w_uq", "mla_w_ukv", "gla_af_w2", "gla_ab_w2", "w_branch_mla", "w_branch_pool", "w_branch_gla", "w_out")
ROW_SHARDED = ("w_out",)
REPLICATED = tuple(n for n in ARG_WEIGHTS if n not in SHARDED)
PACK_ROWS = 512


def _pack(parts, dtype):
    flat = jnp.concatenate([p.astype(dtype).reshape(-1) for p in parts])
    n = flat.shape[0]
    total = -(-n // (PACK_ROWS * LANES)) * (PACK_ROWS * LANES)
    return jnp.pad(flat, (0, total - n)).reshape(total // LANES, LANES)


def _unpack(buf, shapes):
    flat = buf.reshape(-1)
    out, off = [], 0
    for shp in shapes:
        n = math.prod(shp)
        out.append(flat[off:off + n].reshape(shp))
        off += n
    return out


def _gathered_to_full(g, name):
    _, r, cs = g.shape
    if name in ROW_SHARDED:
        return g.reshape(N_DEV * r, cs)
    return g.transpose(1, 0, 2).reshape(r, N_DEV * cs)


def _full_to_slots(w, name):
    if name in ROW_SHARDED:
        return w.reshape(N_DEV, w.shape[0] // N_DEV, w.shape[1])
    return w.reshape(w.shape[0], N_DEV, w.shape[1] // N_DEV).transpose(1, 0, 2)


def kernel(x, c, ctx, c_ctx, mod_w, mod_b, pre_norm, post_norm, w_in, mla_q_norm, mla_w_uq, mla_kv_norm, mla_w_ukv, pool_w, pool_scale, gla_af_w2, gla_af_b, gla_ab_w2, gla_ab_b, gla_norm, w_branch_mla, w_branch_pool, w_branch_gla, w_out, loss_target, m_c_ctx, m_mod_w, m_mod_b, m_pre_norm, m_post_norm, m_w_in, m_mla_q_norm, m_mla_w_uq, m_mla_kv_norm, m_mla_w_ukv, m_pool_w, m_pool_scale, m_gla_af_w2, m_gla_af_b, m_gla_ab_w2, m_gla_ab_b, m_gla_norm, m_w_branch_mla, m_w_branch_pool, m_w_branch_gla, m_w_out, v_c_ctx, v_mod_w, v_mod_b, v_pre_norm, v_post_norm, v_w_in, v_mla_q_norm, v_mla_w_uq, v_mla_kv_norm, v_mla_w_ukv, v_pool_w, v_pool_scale, v_gla_af_w2, v_gla_af_b, v_gla_ab_w2, v_gla_ab_b, v_gla_norm, v_w_branch_mla, v_w_branch_pool, v_w_branch_gla, v_w_out):
    local = dict(locals())
    wts = {n: local[n] for n in ARG_WEIGHTS}
    mom1 = {n: local["m_" + n] for n in ARG_WEIGHTS}
    mom2 = {n: local["v_" + n] for n in ARG_WEIGHTS}
    shard_shapes = [wts[n].shape for n in SHARDED]
    rep_shapes = [wts[n].shape for n in REPLICATED]
    kinds = ("grad", "delta", "new_m", "new_v")

    depth = w_in.shape[0]

    def shards(l):
        return [wts[n][l].astype(BF16) for n in SHARDED]

    first = gather_blocks(shards(0), "gather_weights_l0")

    def layer_full(l, carried):
        return {n: _gathered_to_full(gw, n) for n, gw in zip(SHARDED, first if l == 0 else carried)}

    def host_fwd(l):
        return _Hosted(_Gather, shards(l + 1)) if l + 1 < depth else None

    exchanged = GRADS_EARLY + GRADS_LATE

    def slots(g, names):
        return [_full_to_slots(g[n], n).astype(BF16) for n in names]

    def host_bwd(l, g_above):
        return _Hosted(_Scatter, slots(g_above, exchanged))

    def host_own(l):
        return (lambda names, g: _Hosted(_Scatter, slots(g, names))) if l == 0 else None

    small = {n: wts[n] for n in REPLICATED}
    loss, grad_x, grads, g_c_ctx, arrived, (a8, dz8s) = local_step(x, c, ctx, c_ctx, small, loss_target, depth, layer_full, host_fwd, host_bwd, host_own)
    arrived = [a if isinstance(a, dict) else dict(zip(exchanged, a)) for a in arrived]

    g = {n: (g_c_ctx if n == "c_ctx" else jnp.stack([grads[l][n] for l in range(depth)])) for n in REPLICATED}
    gathered, a_all, dz_all = gather_blocks([_pack([g[n] for n in REPLICATED], F32), a8, jnp.concatenate(dz8s, axis=0)], "gather_small_grads")
    outs = reduce_adamw([gathered], _pack([wts[n] for n in REPLICATED], F32), _pack([mom1[n] for n in REPLICATED], F32),
                        _pack([mom2[n] for n in REPLICATED], F32), "adamw_replicated")
    me = 4 * lax.axis_index("x") + 2 * lax.axis_index("y") + lax.axis_index("c")
    ncol = mod_w.shape[2]
    dz_cols = lax.dynamic_slice_in_dim(dz_all.reshape(N_DEV, depth, 8, 3 * D_MODEL), me * ncol, ncol, axis=3)
    g_mod_w = mod_dw_columns(a_all.reshape(N_DEV * 8, D_MODEL), dz_cols.transpose(1, 0, 2, 3).reshape(depth, N_DEV * 8, ncol), "mod_dw")

    res = {kind: {} for kind in kinds}
    for n, shp in zip(SHARDED, shard_shapes):
        flat = (shp[0] * shp[1], shp[2])
        parts = [g_mod_w.reshape((1,) + flat)] if n == "mod_w" else [arrived[l][n] for l in range(depth)]
        for kind, o in zip(kinds, reduce_adamw(parts, wts[n].reshape(flat), mom1[n].reshape(flat), mom2[n].reshape(flat), "adamw_" + n)):
            res[kind][n] = o.reshape(shp)
    for kind, o in zip(("grad", "delta", "new_m", "new_v"), outs):
        res[kind].update(zip(REPLICATED, _unpack(o, rep_shapes)))

    loss = lax.psum(loss[0, 0], ("x", "y", "c"))
    return (loss, grad_x, *[res[kind][n] for kind in ("grad", "delta", "new_m", "new_v") for n in ARG_WEIGHTS])
```

```python
import functools
import math

import jax
import jax.numpy as jnp
import numpy as np
from jax import lax
from jax.experimental import pallas as pl
from jax.experimental.pallas import tpu as pltpu

F32 = jnp.float32
BF16 = jnp.bfloat16

D_MODEL = 1024
NORM_EPS = 1e-6
GRID_W = 64
MLA_HEADS, MLA_Q_RANK, MLA_KV_RANK, MLA_NOPE, MLA_ROPE, MLA_V = 8, 256, 128, 64, 32, 64
MLA_WIDTH = MLA_HEADS * MLA_V
ROPE_BASE = 10000.0
ATT_SCALE = (MLA_NOPE + MLA_ROPE) ** -0.5
POOL_WINDOWS = (2, 4, 8, 16)
POOL_WIDTH, POOL_GROUP = 512, 128
GLA_HEADS, GLA_DK, GLA_DV = 4, 64, 128
GLA_KW, GLA_WIDTH = GLA_HEADS * GLA_DK, GLA_HEADS * GLA_DV
GLA_GATE_RANK, GLA_TAU, GLA_CHUNK = 16, 16.0, 64
IN_SIZES = (256, 128, 32, 512, 512, 512, 256, 256, 512, 16, 16, 512, 3 * D_MODEL)
ADAM_LR, ADAM_B1, ADAM_B2, ADAM_EPS, ADAM_WD, ADAM_STEP = 0.001, 0.9, 0.999, 1e-08, 0.01, 10
N_DEV = 8

LANES = 128
TOKEN_BLOCK = 256
VMEM_LIMIT = 48 * 1024 * 1024
NEG_BIG = -1e30

_NT = (((1,), (1,)), ((), ()))
_TN = (((0,), (0,)), ((), ()))


def _dot(a, b):
    return jnp.dot(a, b, preferred_element_type=F32)


def _dot_nt(a, b):
    return lax.dot_general(a, b, _NT, preferred_element_type=F32)


def _dot_tn(a, b):
    return lax.dot_general(a, b, _TN, preferred_element_type=F32)


def _params(sem=None, vmem=None):
    kw = {}
    if sem is not None:
        kw["dimension_semantics"] = sem
    if vmem is not None:
        kw["vmem_limit_bytes"] = vmem
    return pltpu.CompilerParams(**kw)


def _full(shape):
    n = len(shape)
    return pl.BlockSpec(shape, lambda *_: (0,) * n)


def _sig(x):
    return 1.0 / (1.0 + jnp.exp(-x))


def _silu_and_grad(x):
    s = _sig(x)
    return x * s, s * (1.0 + x * (1.0 - s))


def _acc(ref, val, first):
    @pl.when(first)
    def _():
        ref[...] = val

    @pl.when(jnp.logical_not(first))
    def _():
        ref[...] += val


def mm_multi(a, ws, dtypes, name, tm=TOKEN_BLOCK):
    M, K = a.shape
    nw = len(ws)

    def body(a_ref, *refs):
        av = a_ref[...]
        for w_ref, o_ref in zip(refs[:nw], refs[nw:]):
            o_ref[...] = _dot(av, w_ref[...]).astype(o_ref.dtype)

    return pl.pallas_call(
        body, name=name, grid=(M // tm,),
        in_specs=[pl.BlockSpec((tm, K), lambda i: (i, 0))] + [_full(w.shape) for w in ws],
        out_specs=[pl.BlockSpec((tm, w.shape[1]), lambda i: (i, 0)) for w in ws],
        out_shape=[jax.ShapeDtypeStruct((M, w.shape[1]), dt) for w, dt in zip(ws, dtypes)],
        compiler_params=_params(("parallel",), VMEM_LIMIT),
    )(a, *ws)


def mm_dx(dzs, ws, name, tm=TOKEN_BLOCK, hosted=None):
    M = dzs[0].shape[0]
    K = ws[0].shape[0]
    nw = len(ws)
    nx = hosted.n if hosted else 0

    def body(*refs):
        if hosted:
            hosted.run(refs[2 * nw:2 * nw + nx], refs[2 * nw + nx + 1:2 * nw + 2 * nx + 1], refs[2 * nw + 2 * nx + 1:], pl.program_id(0), M // tm)
        acc = None
        for dz_ref, w_ref in zip(refs[:nw], refs[nw:2 * nw]):
            t = _dot_nt(dz_ref[...], w_ref[...])
            acc = t if acc is None else acc + t
        refs[2 * nw + nx][...] = acc

    outs = pl.pallas_call(
        body, name=name, grid=(M // tm,),
        in_specs=[pl.BlockSpec((tm, dz.shape[1]), lambda i: (i, 0)) for dz in dzs] + [_full(w.shape) for w in ws] + (hosted.in_specs if hosted else []),
        out_specs=[pl.BlockSpec((tm, K), lambda i: (i, 0))] + (hosted.out_specs if hosted else []),
        out_shape=[jax.ShapeDtypeStruct((M, K), F32)] + (hosted.out_shape if hosted else []),
        scratch_shapes=hosted.scratch if hosted else [],
        compiler_params=_params(("arbitrary",) if hosted else ("parallel",), VMEM_LIMIT),
    )(*dzs, *ws, *(hosted.xs if hosted else []))
    return outs[0], list(outs[1:])


def mm_dw(a, dz, name, tn=1024):
    M, K = a.shape
    n = dz.shape[1]
    tn = min(tn, n)
    tk = next(t for t in (3072, 1536, 1024, 512, TOKEN_BLOCK) if M % t == 0)

    def body(a_ref, dz_ref, o_ref):
        _acc(o_ref, _dot_tn(a_ref[...], dz_ref[...]), pl.program_id(1) == 0)

    return pl.pallas_call(
        body, name=name, grid=(n // tn, M // tk),
        in_specs=[pl.BlockSpec((tk, K), lambda j, k: (k, 0)), pl.BlockSpec((tk, tn), lambda j, k: (k, j))],
        out_specs=pl.BlockSpec((K, tn), lambda j, k: (0, j)),
        out_shape=jax.ShapeDtypeStruct((K, n), F32),
        compiler_params=_params(("parallel", "arbitrary"), VMEM_LIMIT),
    )(a, dz)


def mod_fwd(a8, w, b, name):
    tn = D_MODEL

    def body(a_ref, w_ref, b_ref, o_ref):
        a = a_ref[...]
        o_ref[...] = _dot((a * _sig(a)).astype(BF16), w_ref[...]) + b_ref[...]

    return pl.pallas_call(
        body, name=name, grid=(3,),
        in_specs=[_full(a8.shape), pl.BlockSpec((D_MODEL, tn), lambda j: (0, j)), pl.BlockSpec((1, tn), lambda j: (0, j))],
        out_specs=pl.BlockSpec((8, tn), lambda j: (0, j)),
        out_shape=jax.ShapeDtypeStruct((8, 3 * D_MODEL), F32),
        compiler_params=_params(("parallel",)),
    )(a8, w, b)


def mod_bwd(a8, w, dz8, name):
    tn = D_MODEL

    def body(a_ref, w_ref, dz_ref, db_ref, da_ref):
        a = a_ref[...]
        _, dsa = _silu_and_grad(a)
        dz = dz_ref[...]
        db_ref[...] = jnp.sum(dz, axis=0, keepdims=True)
        _acc(da_ref, _dot_nt(dz.astype(BF16), w_ref[...]) * dsa, pl.program_id(0) == 0)

    return pl.pallas_call(
        body, name=name, grid=(3,),
        in_specs=[_full(a8.shape), pl.BlockSpec((D_MODEL, tn), lambda j: (0, j)), pl.BlockSpec((8, tn), lambda j: (0, j))],
        out_specs=[pl.BlockSpec((1, tn), lambda j: (0, j)), _full((8, D_MODEL))],
        out_shape=[jax.ShapeDtypeStruct((1, 3 * D_MODEL), F32), jax.ShapeDtypeStruct((8, D_MODEL), F32)],
        compiler_params=_params(("arbitrary",)),
    )(a8, w, dz8)


def mod_dw_columns(a_all, dz_cols, name):
    depth, R, n = dz_cols.shape

    def body(a_ref, dz_ref, dw_ref):
        a = a_ref[...]
        dw_ref[0] = _dot_tn((a * _sig(a)).astype(BF16), dz_ref[0].astype(BF16))

    return pl.pallas_call(
        body, name=name, grid=(depth,),
        in_specs=[_full(a_all.shape), pl.BlockSpec((1, R, n), lambda l: (l, 0, 0))],
        out_specs=pl.BlockSpec((1, D_MODEL, n), lambda l: (l, 0, 0)),
        out_shape=jax.ShapeDtypeStruct((depth, D_MODEL, n), F32),
        compiler_params=_params(("parallel",)),
    )(a_all, dz_cols)


def _mod_row(nb):
    return lambda i: 2 * (i // nb) + jnp.minimum(i % nb, 1)


def _mod_spec(nb, part):
    row = _mod_row(nb)
    return pl.BlockSpec((1, 1, D_MODEL), lambda i: (row(i), 0, part))


def norm_in_proj(x2, g, ms, ws, nb, name):
    T = x2.shape[0]
    nw = len(ws)

    def body(x_ref, g_ref, sh_ref, sc_ref, *refs):
        x = x_ref[...]
        r = lax.rsqrt(jnp.mean(x * x, axis=-1, keepdims=True) + NORM_EPS)
        h = ((x * r) * g_ref[...] * (1.0 + sc_ref[0]) + sh_ref[0]).astype(BF16)
        refs[nw][...] = h
        for w_ref, o_ref in zip(refs[:nw], refs[nw + 1:]):
            o_ref[...] = _dot(h, w_ref[...])

    tok = lambda n: pl.BlockSpec((TOKEN_BLOCK, n), lambda i: (i, 0))
    outs = pl.pallas_call(
        body, name=name, grid=(T // TOKEN_BLOCK,),
        in_specs=[tok(D_MODEL), _full((1, D_MODEL)), _mod_spec(nb, 0), _mod_spec(nb, 1)] + [_full(w.shape) for w in ws],
        out_specs=[tok(D_MODEL)] + [tok(w.shape[1]) for w in ws],
        out_shape=[jax.ShapeDtypeStruct((T, D_MODEL), BF16)] + [jax.ShapeDtypeStruct((T, w.shape[1]), F32) for w in ws],
        compiler_params=_params(("parallel",), VMEM_LIMIT),
    )(x2, g, ms, ms, *ws)
    return outs[0], outs[1:]


def in_proj_norm_bwd(dzs, ws, x2, g, ms, dxres, nb, name, hosted=None):
    T = x2.shape[0]
    nw = len(ws)
    nx = hosted.n if hosted else 0
    nrow = ms.shape[0]
    row = _mod_row(nb)
    n_in = 2 * nw + 4

    def body(*refs):
        x_ref, g_ref, sc_ref, dxr_ref = refs[2 * nw:n_in]
        dx_ref, dsh_ref, dsc_ref, dg_ref = refs[n_in + nx:n_in + nx + 4]
        i = pl.program_id(0)
        if hosted:
            hosted.run(refs[n_in:n_in + nx], refs[n_in + nx + 4:n_in + 2 * nx + 4], refs[n_in + 2 * nx + 4:], i, T // TOKEN_BLOCK)
        dh = None
        for dz_ref, w_ref in zip(refs[:nw], refs[nw:2 * nw]):
            t = _dot_nt(dz_ref[...], w_ref[...])
            dh = t if dh is None else dh + t
        x = x_ref[...]
        g = g_ref[...]
        r = lax.rsqrt(jnp.mean(x * x, axis=-1, keepdims=True) + NORM_EPS)
        xn = x * r
        du = dh * (1.0 + sc_ref[0])
        dyg = du * g
        dx_ref[...] = dxr_ref[...] + r * (dyg - xn * jnp.mean(dyg * xn, axis=-1, keepdims=True))
        first = (i % nb) <= 1
        _acc(dsh_ref.at[0], jnp.sum(dh, axis=0, keepdims=True), first)
        _acc(dsc_ref.at[0], jnp.sum(dh * xn * g, axis=0, keepdims=True), first)
        _acc(dg_ref, jnp.sum(du * xn, axis=0, keepdims=True), i == 0)

    tok = lambda n: pl.BlockSpec((TOKEN_BLOCK, n), lambda i: (i, 0))
    acc = pl.BlockSpec((1, 1, D_MODEL), lambda i: (row(i), 0, 0))
    outs = pl.pallas_call(
        body, name=name, grid=(T // TOKEN_BLOCK,),
        in_specs=[tok(dz.shape[1]) for dz in dzs] + [_full(w.shape) for w in ws] + [tok(D_MODEL), _full((1, D_MODEL)), _mod_spec(nb, 1), tok(D_MODEL)]
        + (hosted.in_specs if hosted else []),
        out_specs=[tok(D_MODEL), acc, acc, _full((1, D_MODEL))] + (hosted.out_specs if hosted else []),
        out_shape=[jax.ShapeDtypeStruct((T, D_MODEL), F32), jax.ShapeDtypeStruct((nrow, 1, D_MODEL), F32),
                   jax.ShapeDtypeStruct((nrow, 1, D_MODEL), F32), jax.ShapeDtypeStruct((1, D_MODEL), F32)] + (hosted.out_shape if hosted else []),
        scratch_shapes=hosted.scratch if hosted else [],
        compiler_params=_params(("arbitrary",), VMEM_LIMIT),
    )(*dzs, *ws, x2, g, ms, dxres, *(hosted.xs if hosted else []))
    return outs[:4], list(outs[4:])


def _rot(x):
    lane = lax.broadcasted_iota(jnp.int32, x.shape, 1)
    return jnp.where((lane % 16) < 8, -pltpu.roll(x, LANES - 8, 1), pltpu.roll(x, 8, 1))


def _rope(x, cos, sin):
    return x * cos + _rot(x) * sin


def _rope_t(dy, cos, sin):
    return dy * cos - _rot(dy * sin)


def _rms_rows(x):
    r = lax.rsqrt(jnp.mean(x * x, axis=-1, keepdims=True) + NORM_EPS)
    return x * r, r


def _rms_rows_bwd(dyg, xn, r):
    return r * (dyg - xn * jnp.mean(dyg * xn, axis=-1, keepdims=True))


def mla_prep_fwd(za, qg, kvg, wqn, wqr, wkv, cos, sin, nb, name):
    T = za.shape[0]
    W = MLA_HEADS * LANES

    def body(z_ref, qg_ref, kvg_ref, wqn_ref, wqr_ref, wkv_ref, cos_ref, sin_ref, qn_ref, qr_ref, kv_ref, kr_ref):
        z = z_ref[...]
        cos = cos_ref[...]
        sin = sin_ref[...]
        xq, _ = _rms_rows(z[:, 0:256])
        qn = (xq * qg_ref[...]).astype(BF16)
        qn_ref[...] = (_dot(qn, wqn_ref[...]) * ATT_SCALE).astype(BF16)
        qr = _dot(qn, wqr_ref[...])
        for h in range(MLA_HEADS):
            sl = slice(LANES * h, LANES * (h + 1))
            qr_ref[:, sl] = (_rope(qr[:, sl], cos, sin) * ATT_SCALE).astype(BF16)
        xkv, _ = _rms_rows(z[:, 256:384])
        kv_ref[...] = _dot((xkv * kvg_ref[...]).astype(BF16), wkv_ref[...]).astype(BF16)
        kr_ref[...] = _rope(z[:, 384:512], cos, sin).astype(BF16)

    tok = lambda n: pl.BlockSpec((TOKEN_BLOCK, n), lambda i: (i, 0))
    pos = pl.BlockSpec((TOKEN_BLOCK, LANES), lambda i: (i % nb, 0))
    return pl.pallas_call(
        body, name=name, grid=(T // TOKEN_BLOCK,),
        in_specs=[tok(512), _full(qg.shape), _full(kvg.shape), _full(wqn.shape), _full(wqr.shape), _full(wkv.shape), pos, pos],
        out_specs=[tok(W), tok(W), tok(W), tok(LANES)],
        out_shape=[jax.ShapeDtypeStruct((T, W), BF16)] * 3 + [jax.ShapeDtypeStruct((T, LANES), BF16)],
        compiler_params=_params(("parallel",)),
    )(za, qg, kvg, wqn, wqr, wkv, cos, sin)


def mla_prep_bwd(dqn, dqr, dkv, dkr, za, qg, kvg, wqn, wqr, wkv, cos, sin, nb, name):
    T = za.shape[0]
    W = MLA_HEADS * LANES

    def body(dqn_ref, dqr_ref, dkv_ref, dkr_ref, z_ref, qg_ref, kvg_ref, wqn_ref, wqr_ref, wkv_ref, cos_ref, sin_ref,
             dz_ref, dwqn_ref, dwqr_ref, dwkv_ref, dqg_ref, dkvg_ref):
        first = pl.program_id(0) == 0
        z = z_ref[...]
        cos = cos_ref[...]
        sin = sin_ref[...]
        qg = qg_ref[...]
        kvg = kvg_ref[...]
        xq, rq = _rms_rows(z[:, 0:256])
        qn = (xq * qg).astype(BF16)
        a1 = (dqn_ref[...].astype(F32) * ATT_SCALE).astype(BF16)
        parts = []
        for h in range(MLA_HEADS):
            sl = slice(LANES * h, LANES * (h + 1))
            parts.append(_rope_t(dqr_ref[:, sl].astype(F32) * ATT_SCALE, cos, sin).astype(BF16))
        a2 = jnp.concatenate(parts, axis=1)
        dq = _dot_nt(a1, wqn_ref[...]) + _dot_nt(a2, wqr_ref[...])
        _acc(dwqn_ref, _dot_tn(qn, a1), first)
        _acc(dwqr_ref, _dot_tn(qn, a2), first)
        _acc(dqg_ref, jnp.sum(dq * xq, axis=0, keepdims=True), first)
        dz_ref[:, 0:256] = _rms_rows_bwd(dq * qg, xq, rq).astype(BF16)
        xkv, rkv = _rms_rows(z[:, 256:384])
        kvn = (xkv * kvg).astype(BF16)
        dkvb = dkv_ref[...].astype(BF16)
        dk = _dot_nt(dkvb, wkv_ref[...])
        _acc(dwkv_ref, _dot_tn(kvn, dkvb), first)
        _acc(dkvg_ref, jnp.sum(dk * xkv, axis=0, keepdims=True), first)
        dz_ref[:, 256:384] = _rms_rows_bwd(dk * kvg, xkv, rkv).astype(BF16)
        dz_ref[:, 384:512] = _rope_t(dkr_ref[...], cos, sin).astype(BF16)

    tok = lambda n: pl.BlockSpec((TOKEN_BLOCK, n), lambda i: (i, 0))
    pos = pl.BlockSpec((TOKEN_BLOCK, LANES), lambda i: (i % nb, 0))
    return pl.pallas_call(
        body, name=name, grid=(T // TOKEN_BLOCK,),
        in_specs=[tok(W), tok(W), tok(W), tok(LANES), tok(512), _full(qg.shape), _full(kvg.shape), _full(wqn.shape),
                  _full(wqr.shape), _full(wkv.shape), pos, pos],
        out_specs=[tok(512), _full(wqn.shape), _full(wqr.shape), _full(wkv.shape), _full(qg.shape), _full(kvg.shape)],
        out_shape=[jax.ShapeDtypeStruct((T, 512), BF16), jax.ShapeDtypeStruct(wqn.shape, F32), jax.ShapeDtypeStruct(wqr.shape, F32),
                   jax.ShapeDtypeStruct(wkv.shape, F32), jax.ShapeDtypeStruct(qg.shape, F32), jax.ShapeDtypeStruct(kvg.shape, F32)],
        compiler_params=_params(("arbitrary",)),
    )(dqn, dqr, dkv, dkr, za, qg, kvg, wqn, wqr, wkv, cos, sin)


def _att_qk(qn_ref, qr_ref, kv_ref, kr, j):
    sl = slice(LANES * j, LANES * (j + 1))
    q = jnp.concatenate([qn_ref[0, :, sl], qr_ref[0, :, sl]], axis=1)
    kvj = kv_ref[0, :, sl]
    k = jnp.concatenate([kvj, kr], axis=1)
    return q, k, kvj, _dot_nt(q, k)


def _att_specs(L, lk, q0, pairs=1):
    TQ, W2 = TOKEN_BLOCK, 2 * LANES * pairs
    qspec = pl.BlockSpec((1, TQ, W2), lambda b, h, i: (b, i + q0, h))
    kvspec = pl.BlockSpec((1, lk, W2), lambda b, h, i: (b, 0, h))
    krspec = pl.BlockSpec((1, lk, LANES), lambda b, h, i: (b, 0, 0))
    gspec = pl.BlockSpec((1, TQ, LANES * pairs), lambda b, h, i: (b, i + q0, h))
    lspec = pl.BlockSpec((1, pairs, TQ, LANES), lambda b, h, i: (b, h, i + q0, 0))
    return qspec, kvspec, krspec, gspec, lspec


_ANY = pl.BlockSpec(memory_space=pl.ANY)


def attention_fwd(qn, qr, kv, kr, zg, n_ctx, name, hosted=None):
    B, L, _ = qn.shape
    TQ = TOKEN_BLOCK
    PAIRS = 2
    HP = MLA_HEADS // 2 // PAIRS
    shapes = [jax.ShapeDtypeStruct((B, L, MLA_WIDTH), F32), jax.ShapeDtypeStruct((B, L, MLA_WIDTH), BF16),
              jax.ShapeDtypeStruct((B, MLA_HEADS // 2, L, LANES), F32)]

    nx = hosted.n if hosted else 0
    NQ = L // TQ - 1

    def body(*refs):
        if hosted:
            step = (pl.program_id(0) * HP + pl.program_id(1)) * NQ + pl.program_id(2)
            hosted.run(refs[5:5 + nx], refs[8 + nx:8 + 2 * nx], refs[8 + 2 * nx:], step, B * HP * NQ)
        _fwd_step(*refs[:5], *refs[5 + nx:8 + nx])

    def body_ctx(qn_ref, qr_ref, kv_ref, kr_ref, g_ref, *rest):
        _fwd_step(qn_ref, qr_ref, kv_ref, kr_ref, g_ref, *rest[-3:])

    def _fwd_step(qn_ref, qr_ref, kv_ref, kr_ref, g_ref, ya_ref, ym_ref, lse_ref):
        kr_v = kr_ref[0]
        for pr in range(PAIRS):
            outs, lses = [], []
            for j in (2 * pr, 2 * pr + 1):
                _, _, kvj, s = _att_qk(qn_ref, qr_ref, kv_ref, kr_v, j)
                m = jnp.max(s, axis=-1, keepdims=True)
                p = jnp.exp(s - m).astype(BF16)
                lane_k = lax.broadcasted_iota(jnp.int32, kvj.shape, 1)
                o = _dot(p, jnp.where(lane_k < MLA_V, jnp.ones_like(kvj), kvj))
                l = o[:, 0:1]
                outs.append(o / l)
                lses.append(m + jnp.log(l))
            lane = lax.broadcasted_iota(jnp.int32, outs[0].shape, 1)
            y = jnp.where(lane < MLA_V, pltpu.roll(outs[0], MLA_V, 1), outs[1])
            sl = slice(LANES * pr, LANES * (pr + 1))
            ya_ref[0, :, sl] = y
            g = g_ref[0, :, sl]
            ym_ref[0, :, sl] = (y * g * _sig(g)).astype(BF16)
            lse_ref[0, pr] = jnp.where(lane < MLA_V, lses[0], lses[1])

    qspec, kvspec, krspec, gspec, lspec = _att_specs(L, L, 1, PAIRS)
    main = pl.pallas_call(
        body, name=name, grid=(B, HP, NQ),
        in_specs=[qspec, qspec, kvspec, krspec, gspec] + (hosted.in_specs if hosted else []),
        out_specs=[gspec, gspec, lspec] + (hosted.out_specs if hosted else []),
        out_shape=shapes + (hosted.out_shape if hosted else []), scratch_shapes=hosted.scratch if hosted else [],
        compiler_params=_params(("arbitrary",) * 3 if hosted else ("parallel",) * 3, VMEM_LIMIT),
    )(qn, qr, kv, kr, zg, *(hosted.xs if hosted else []))
    qspec, kvspec, krspec, gspec, lspec = _att_specs(L, n_ctx, 0, PAIRS)
    outs = pl.pallas_call(
        body_ctx, name=name + "_ctx", grid=(B, HP, 1),
        in_specs=[qspec, qspec, kvspec, krspec, gspec, _ANY, _ANY, _ANY], out_specs=[gspec, gspec, lspec], out_shape=shapes,
        input_output_aliases={5: 0, 6: 1, 7: 2},
        compiler_params=_params(("parallel", "parallel", "parallel"), VMEM_LIMIT),
    )(qn, qr, kv, kr, zg, *main[:3])
    return (*outs, list(main[3:]))


def attention_bwd(qn, qr, kv, kr, zg, ya, lse, dym, n_ctx, name, hosted=None):
    B, L, _ = qn.shape
    TQ = TOKEN_BLOCK
    HP = MLA_HEADS // 2
    W = MLA_HEADS * LANES
    shapes = [jax.ShapeDtypeStruct((B, L, W), BF16), jax.ShapeDtypeStruct((B, L, W), BF16), jax.ShapeDtypeStruct((B, L, W), F32),
              jax.ShapeDtypeStruct((B, L, LANES), F32), jax.ShapeDtypeStruct((B, L, MLA_WIDTH), BF16)]

    nx = hosted.n if hosted else 0
    NQ = L // TQ - 1

    def body(*refs):
        if hosted:
            step = (pl.program_id(0) * HP + pl.program_id(1)) * NQ + pl.program_id(2)
            hosted.run(refs[8:8 + nx], refs[13 + nx:13 + 2 * nx], refs[13 + 2 * nx:], step, B * HP * NQ)
        dkv_ref, dkr_ref = refs[10 + nx], refs[11 + nx]

        @pl.when(pl.program_id(2) == 0)
        def _():
            dkv_ref[...] = jnp.zeros_like(dkv_ref)

        @pl.when(jnp.logical_and(pl.program_id(2) == 0, pl.program_id(1) == 0))
        def _():
            dkr_ref[...] = jnp.zeros_like(dkr_ref)

        _bwd_step(*refs[:8], *refs[8 + nx:13 + nx])

    def body_ctx(qn_ref, qr_ref, kv_ref, kr_ref, g_ref, ya_ref, lse_ref, dy_ref, dkv_in, dkr_in, a0, a1, a2,
                 dqn_ref, dqr_ref, dkv_ref, dkr_ref, dzg_ref):
        dkv_ref[...] = dkv_in[...]

        @pl.when(pl.program_id(1) == 0)
        def _():
            dkr_ref[...] = dkr_in[...]

        _bwd_step(qn_ref, qr_ref, kv_ref, kr_ref, g_ref, ya_ref, lse_ref, dy_ref, dqn_ref, dqr_ref, dkv_ref, dkr_ref, dzg_ref)

    def _bwd_step(qn_ref, qr_ref, kv_ref, kr_ref, g_ref, ya_ref, lse_ref, dy_ref, dqn_ref, dqr_ref, dkv_ref, dkr_ref, dzg_ref):
        g = g_ref[0]
        silu, dsilu = _silu_and_grad(g)
        dy = dy_ref[0]
        ya_v = ya_ref[0]
        dya = dy * silu
        dzg_ref[0] = (dy * ya_v * dsilu).astype(BF16)
        lane = lax.broadcasted_iota(jnp.int32, dya.shape, 1)
        hi = lane >= MLA_V
        d_out = [jnp.where(hi, pltpu.roll(dya, MLA_V, 1), 0.0), jnp.where(hi, dya, 0.0)]
        prod = dya * ya_v
        drow = [jnp.sum(jnp.where(hi, 0.0, prod), axis=-1, keepdims=True), jnp.sum(jnp.where(hi, prod, 0.0), axis=-1, keepdims=True)]
        lse_v = lse_ref[0, 0]
        kr_v = kr_ref[0]
        for j in range(2):
            sl = slice(LANES * j, LANES * (j + 1))
            q, k, kvj, s = _att_qk(qn_ref, qr_ref, kv_ref, kr_v, j)
            pn = jnp.exp(s - lse_v[:, MLA_V * j:MLA_V * j + 1])
            dob = d_out[j].astype(BF16)
            ds = (pn * (_dot_nt(dob, kvj) - drow[j])).astype(BF16)
            dq = _dot(ds, k)
            dqn_ref[0, :, sl] = jnp.where(hi, 0.0, dq[:, :LANES]).astype(BF16)
            dqr_ref[0, :, sl] = dq[:, LANES:].astype(BF16)
            dk = _dot_tn(ds, q)
            dkv_ref[0, :, sl] += dk[:, :LANES] + _dot_tn(pn.astype(BF16), dob)
            dkr_ref[0] += dk[:, LANES:]

    sem = _params(("parallel", "arbitrary", "arbitrary"), VMEM_LIMIT)
    qspec, kvspec, krspec, gspec, lspec = _att_specs(L, L, 1)
    main = pl.pallas_call(
        body, name=name, grid=(B, HP, NQ),
        in_specs=[qspec, qspec, kvspec, krspec, gspec, gspec, lspec, gspec] + (hosted.in_specs if hosted else []),
        out_specs=[qspec, qspec, kvspec, krspec, gspec] + (hosted.out_specs if hosted else []),
        out_shape=shapes + (hosted.out_shape if hosted else []), scratch_shapes=hosted.scratch if hosted else [],
        compiler_params=_params(("arbitrary",) * 3, VMEM_LIMIT) if hosted else sem,
    )(qn, qr, kv, kr, zg, ya, lse, dym, *(hosted.xs if hosted else []))
    qspec, kvspec, krspec, gspec, lspec = _att_specs(L, n_ctx, 0)
    outs = pl.pallas_call(
        body_ctx, name=name + "_ctx", grid=(B, HP, 1),
        in_specs=[qspec, qspec, kvspec, krspec, gspec, gspec, lspec, gspec, kvspec, krspec, _ANY, _ANY, _ANY],
        out_specs=[qspec, qspec, kvspec, krspec, gspec], out_shape=shapes,
        input_output_aliases={8: 2, 9: 3, 10: 0, 11: 1, 12: 4}, compiler_params=sem,
    )(qn, qr, kv, kr, zg, ya, lse, dym, main[2], main[3], main[0], main[1], main[4])
    return (*outs, list(main[5:]))


def _seg_bounds(rows, n_ctx, L):
    in_ctx = rows < n_ctx
    return jnp.where(in_ctx, 0, n_ctx), jnp.where(in_ctx, n_ctx, L)


def _window_sum(u, w, rows, lo, hi, mirror):
    L = u.shape[0]
    offs = range(-w // 2 + 1, w // 2 + 1) if mirror else range(-w // 2, w // 2)
    acc = None
    for d in offs:
        if d == 0:
            t = u
        else:
            src = rows + d
            t = jnp.where(jnp.logical_and(src >= lo, src < hi), pltpu.roll(u, (-d) % L, 0), 0.0)
        acc = t if acc is None else acc + t
    return acc


def _window_count(w, rows, lo, hi):
    pos = rows - lo
    return (jnp.minimum(pos + w // 2, hi - lo) - jnp.maximum(pos - w // 2, 0)).astype(F32)


def pool_fwd(px, pg, pw, ps, n_ctx, name):
    B, L, _ = px.shape

    def body(px_ref, pg_ref, pw_ref, ps_ref, y_ref):
        rows = lax.broadcasted_iota(jnp.int32, (L, POOL_GROUP), 0)
        lo, hi = _seg_bounds(rows, n_ctx, L)
        for gi, w in enumerate(POOL_WINDOWS):
            sl = slice(POOL_GROUP * gi, POOL_GROUP * (gi + 1))
            u = px_ref[0, :, sl]
            pooled = _window_sum(u, w, rows, lo, hi, False) / _window_count(w, rows, lo, hi) - u
            mixed = _dot(pooled.astype(BF16), pw_ref[gi])
            g = pg_ref[0, :, sl]
            y_ref[0, :, sl] = (mixed * ps_ref[:, sl] * (g * _sig(g))).astype(BF16)

    tok = pl.BlockSpec((1, L, POOL_WIDTH), lambda b: (b, 0, 0))
    return pl.pallas_call(
        body, name=name, grid=(B,),
        in_specs=[tok, tok, _full(pw.shape), _full(ps.shape)],
        out_specs=tok, out_shape=jax.ShapeDtypeStruct((B, L, POOL_WIDTH), BF16),
        compiler_params=_params(("parallel",), VMEM_LIMIT),
    )(px, pg, pw, ps)


def pool_bwd(px, pg, pw, ps, dy, n_ctx, name):
    B, L, _ = px.shape

    def body(px_ref, pg_ref, pw_ref, ps_ref, dy_ref, dpx_ref, dpg_ref, dpw_ref, dps_ref):
        first = pl.program_id(0) == 0
        rows = lax.broadcasted_iota(jnp.int32, (L, POOL_GROUP), 0)
        lo, hi = _seg_bounds(rows, n_ctx, L)
        for gi, w in enumerate(POOL_WINDOWS):
            sl = slice(POOL_GROUP * gi, POOL_GROUP * (gi + 1))
            u = px_ref[0, :, sl]
            cnt = _window_count(w, rows, lo, hi)
            pooled = (_window_sum(u, w, rows, lo, hi, False) / cnt - u).astype(BF16)
            mixed = _dot(pooled, pw_ref[gi])
            silu, dsilu = _silu_and_grad(pg_ref[0, :, sl])
            sc = ps_ref[:, sl]
            dyv = dy_ref[0, :, sl]
            _acc(dps_ref.at[:, sl], jnp.sum(dyv * mixed * silu, axis=0, keepdims=True), first)
            dpg_ref[0, :, sl] = (dyv * mixed * sc * dsilu).astype(BF16)
            dmixed = (dyv * sc * silu).astype(BF16)
            _acc(dpw_ref.at[gi], _dot_tn(pooled, dmixed), first)
            dpooled = _dot_nt(dmixed, pw_ref[gi])
            dpx_ref[0, :, sl] = (_window_sum(dpooled / cnt, w, rows, lo, hi, True) - dpooled).astype(BF16)

    tok = pl.BlockSpec((1, L, POOL_WIDTH), lambda b: (b, 0, 0))
    return pl.pallas_call(
        body, name=name, grid=(B,),
        in_specs=[tok, tok, _full(pw.shape), _full(ps.shape), tok],
        out_specs=[tok, tok, _full(pw.shape), _full(ps.shape)],
        out_shape=[jax.ShapeDtypeStruct((B, L, POOL_WIDTH), BF16)] * 2 + [jax.ShapeDtypeStruct(pw.shape, F32), jax.ShapeDtypeStruct(ps.shape, F32)],
        compiler_params=_params(("arbitrary",), VMEM_LIMIT),
    )(px, pg, pw, ps, dy)


_SCAN_STEPS = (1, 2, 4, 8, 16, 32)
SCAN_CHUNKS = 2


def _cum_fwd(x, r):
    for s in _SCAN_STEPS:
        x = x + jnp.where(r >= s, pltpu.roll(x, s, 0), 0.0)
    return x


def _cum_bwd(x, r):
    n = x.shape[0]
    for s in _SCAN_STEPS:
        x = x + jnp.where(r + s < GLA_CHUNK, pltpu.roll(x, n - s, 0), 0.0)
    return x


def _log_sigmoid(x):
    return jnp.minimum(x, 0.0) - jnp.log(1.0 + jnp.exp(-jnp.abs(x)))


def _gla_decays(lr, w_ref, b_ref, r, reverse):
    pre = _dot(lr, w_ref[...]) + b_ref[...]
    a = _log_sigmoid(pre) / GLA_TAU
    return pre, a, (_cum_bwd(a, r) if reverse else _cum_fwd(a, r)), _chunk_total(a)


def _chunk_total(x):
    x3 = x.reshape(x.shape[0] // GLA_CHUNK, GLA_CHUNK, x.shape[1])
    return jnp.broadcast_to(jnp.sum(x3, axis=1, keepdims=True), x3.shape).reshape(x.shape)


def gla_prep_fwd(zlr, zq, zk, waf, wab, baf, bab, name):
    T = zlr.shape[0]

    def body(lr_ref, q_ref, k_ref, waf_ref, wab_ref, baf_ref, bab_ref, qf_ref, kf_ref, ksf_ref, tf_ref, qb_ref, kb_ref, ksb_ref, tb_ref):
        r = lax.broadcasted_iota(jnp.int32, (TOKEN_BLOCK, GLA_KW), 0) % GLA_CHUNK
        lr = lr_ref[...].astype(BF16)
        q = q_ref[...] * GLA_DK ** -0.5
        k = k_ref[...]
        for rev, w_ref, b_ref, qo, ko, kso, to in ((False, waf_ref, baf_ref, qf_ref, kf_ref, ksf_ref, tf_ref),
                                                   (True, wab_ref, bab_ref, qb_ref, kb_ref, ksb_ref, tb_ref)):
            _, _, b, tot = _gla_decays(lr, w_ref, b_ref, r, rev)
            qo[...] = (q * jnp.exp(b)).astype(BF16)
            ko[...] = (k * jnp.exp(-b)).astype(BF16)
            kso[...] = (k * jnp.exp(tot - b)).astype(BF16)
            to[...] = tot

    tok = lambda n: pl.BlockSpec((TOKEN_BLOCK, n), lambda i: (i, 0))
    outs = [jax.ShapeDtypeStruct((T, GLA_KW), BF16)] * 3 + [jax.ShapeDtypeStruct((T, GLA_KW), F32)]
    return pl.pallas_call(
        body, name=name, grid=(T // TOKEN_BLOCK,),
        in_specs=[tok(LANES), tok(GLA_KW), tok(GLA_KW), _full(waf.shape), _full(wab.shape), _full(baf.shape), _full(bab.shape)],
        out_specs=[tok(GLA_KW)] * 8, out_shape=outs + outs,
        compiler_params=_params(("parallel",)),
    )(zlr, zq, zk, waf, wab, baf, bab)


def gla_prep_bwd(zlr, zq, zk, waf, wab, baf, bab, gf, gb, dvs, name):
    T = zlr.shape[0]

    def body(lr_ref, q_ref, k_ref, waf_ref, wab_ref, baf_ref, bab_ref, dqf, dkf, dksf, ddf, dqb, dkb, dksb, ddb, dvf, dvb,
             dlr_ref, dq_ref, dk_ref, dwaf_ref, dwab_ref, dbaf_ref, dbab_ref, dv_ref):
        first = pl.program_id(0) == 0
        dv_ref[...] = (dvf[...] + dvb[...]).astype(BF16)
        r = lax.broadcasted_iota(jnp.int32, (TOKEN_BLOCK, GLA_KW), 0) % GLA_CHUNK
        lr = lr_ref[...].astype(BF16)
        q = q_ref[...] * GLA_DK ** -0.5
        k = k_ref[...]
        dq_tot = None
        dk_tot = None
        dlr = None
        for rev, w_ref, b_ref, dqt, dkt, dks, ddec, dw_ref, db_ref in (
                (False, waf_ref, baf_ref, dqf, dkf, dksf, ddf, dwaf_ref, dbaf_ref),
                (True, wab_ref, bab_ref, dqb, dkb, dksb, ddb, dwab_ref, dbab_ref)):
            pre, _, b, tot = _gla_decays(lr, w_ref, b_ref, r, rev)
            e1 = jnp.exp(b)
            e2 = jnp.exp(-b)
            e3 = jnp.exp(tot - b)
            dqt_v = dqt[...]
            dkt_v = dkt[...]
            dks_v = dks[...]
            dq = dqt_v * e1
            dk = dkt_v * e2 + dks_v * e3
            g3 = dks_v * (k * e3)
            d_b = dqt_v * (q * e1) - dkt_v * (k * e2) - g3
            d_tot = _chunk_total(g3) + ddec[...] * jnp.exp(tot)
            da = (_cum_fwd(d_b, r) if rev else _cum_bwd(d_b, r)) + d_tot
            dpre = (da * (_sig(-pre) / GLA_TAU)).astype(BF16)
            t = _dot_nt(dpre, w_ref[...])
            dlr = t if dlr is None else dlr + t
            _acc(dw_ref, _dot_tn(lr, dpre), first)
            _acc(db_ref, jnp.sum(dpre.astype(F32), axis=0, keepdims=True), first)
            dq_tot = dq if dq_tot is None else dq_tot + dq
            dk_tot = dk if dk_tot is None else dk_tot + dk
        dlr_ref[...] = dlr.astype(BF16)
        dq_ref[...] = (dq_tot * GLA_DK ** -0.5).astype(BF16)
        dk_ref[...] = dk_tot.astype(BF16)

    tok = lambda n: pl.BlockSpec((TOKEN_BLOCK, n), lambda i: (i, 0))
    return pl.pallas_call(
        body, name=name, grid=(T // TOKEN_BLOCK,),
        in_specs=[tok(LANES), tok(GLA_KW), tok(GLA_KW), _full(waf.shape), _full(wab.shape), _full(baf.shape), _full(bab.shape)] + [tok(GLA_KW)] * 8
        + [tok(GLA_WIDTH)] * 2,
        out_specs=[tok(LANES), tok(GLA_KW), tok(GLA_KW), _full(waf.shape), _full(wab.shape), _full(baf.shape), _full(bab.shape), tok(GLA_WIDTH)],
        out_shape=[jax.ShapeDtypeStruct((T, LANES), BF16), jax.ShapeDtypeStruct((T, GLA_KW), BF16), jax.ShapeDtypeStruct((T, GLA_KW), BF16),
                   jax.ShapeDtypeStruct(waf.shape, F32), jax.ShapeDtypeStruct(wab.shape, F32), jax.ShapeDtypeStruct(baf.shape, F32),
                   jax.ShapeDtypeStruct(bab.shape, F32), jax.ShapeDtypeStruct((T, GLA_WIDTH), BF16)],
        compiler_params=_params(("arbitrary",)),
    )(zlr, zq, zk, waf, wab, baf, bab, *gf, *gb, *dvs)


def _chunk_order(nc, n_ctx_chunks, reverse):
    if not reverse:
        return lambda c: c
    return lambda c: jnp.where(c < n_ctx_chunks, n_ctx_chunks - 1 - c, nc + n_ctx_chunks - 1 - c)


def _head_mask(shape, h):
    lane = lax.broadcasted_iota(jnp.int32, shape, 1)
    return jnp.logical_and(lane >= GLA_DK * h, lane < GLA_DK * (h + 1))


def _tri_mask4(reverse):
    ri = lax.broadcasted_iota(jnp.int32, (GLA_CHUNK, GLA_HEADS * GLA_CHUNK), 0)
    ci = lax.broadcasted_iota(jnp.int32, (GLA_CHUNK, GLA_HEADS * GLA_CHUNK), 1) % GLA_CHUNK
    return (ri <= ci) if reverse else (ri >= ci)


def _block_diag(x, rb, cb):
    x4 = jnp.concatenate([x] * GLA_HEADS, axis=0)
    r = lax.broadcasted_iota(jnp.int32, x4.shape, 0) // rb
    c = lax.broadcasted_iota(jnp.int32, x4.shape, 1) // cb
    return jnp.where(r == c, x4, jnp.zeros_like(x4))


def _diag_blocks(f, rb, cb):
    c = lax.broadcasted_iota(jnp.int32, (rb, GLA_HEADS * cb), 1) // cb
    out = None
    for h in range(GLA_HEADS):
        t = jnp.where(c == h, f[rb * h:rb * (h + 1)], 0.0)
        out = t if out is None else out + t
    return out


def gla_scan_fwd(dirs, v, n_ctx, name):
    B, L, _ = v.shape
    C, G = GLA_CHUNK, SCAN_CHUNKS
    nc = L // C
    orders = [_chunk_order(nc // G, n_ctx // C // G, rev) for rev in (False, True)]

    def body(qf, kf, ksf, tf, vf, qb, kb, ksb, tb, vb, of, ssf, ob, ssb, stf, stb):
        @pl.when(pl.program_id(1) == 0)
        def _():
            stf[...] = jnp.zeros_like(stf)
            stb[...] = jnp.zeros_like(stb)

        for sub in range(G):
            step(qf, kf, ksf, vf, tf, of, ssf, stf, False, sub)
            step(qb, kb, ksb, vb, tb, ob, ssb, stb, True, G - 1 - sub)

    def step(q_ref, k_ref, ks_ref, v_ref, tot_ref, o_ref, ss_ref, st, reverse, sub):
        rows = slice(C * sub, C * (sub + 1))
        S = st[...]
        ss_ref[0, sub] = S
        q = q_ref[0, rows]
        v = v_ref[0, rows]
        k4 = _block_diag(k_ref[0, rows], GLA_CHUNK, GLA_DK)
        v4 = _block_diag(v.astype(BF16), GLA_CHUNK, GLA_DV)
        s4 = _block_diag(S.astype(BF16), GLA_DV, GLA_DK)
        P = jnp.where(_tri_mask4(reverse), _dot_nt(q, k4), 0.0)
        o_ref[0, rows] = _dot(P.astype(BF16), v4) + _dot_nt(q, s4)
        st[...] = jnp.exp(tot_ref[0, C * sub:C * sub + 1, :]) * S + _diag_blocks(_dot(v.T.astype(BF16), ks_ref[0, rows]), GLA_DV, GLA_DK)

    in_specs, out_specs, out_shape = [], [], []
    for order in orders:
        tok = lambda n, order=order: pl.BlockSpec((1, G * C, n), lambda b, c: (b, order(c), 0))
        in_specs += [tok(GLA_KW), tok(GLA_KW), tok(GLA_KW), tok(GLA_KW), tok(GLA_WIDTH)]
        out_specs += [tok(GLA_WIDTH), pl.BlockSpec((1, G, GLA_DV, GLA_KW), lambda b, c, order=order: (b, order(c), 0, 0))]
        out_shape += [jax.ShapeDtypeStruct((B, L, GLA_WIDTH), F32), jax.ShapeDtypeStruct((B, nc, GLA_DV, GLA_KW), F32)]
    outs = pl.pallas_call(
        body, name=name, grid=(B, nc // G), in_specs=in_specs, out_specs=out_specs, out_shape=out_shape,
        scratch_shapes=[pltpu.VMEM((GLA_DV, GLA_KW), F32)] * 2,
        compiler_params=_params(("parallel", "arbitrary")),
    )(*dirs[0], v, *dirs[1], v)
    return outs[:2], outs[2:]


def gla_scan_bwd(dirs, v, do, n_ctx, name, hosted=None):
    B, L, _ = v.shape
    C, G = GLA_CHUNK, SCAN_CHUNKS
    nc = L // C
    npair = nc // G
    orders = []
    for rev in (False, True):
        fwd_order = _chunk_order(npair, n_ctx // C // G, rev)
        orders.append(lambda c, fwd_order=fwd_order: fwd_order(npair - 1 - c))

    nx = hosted.n if hosted else 0

    def body(*refs):
        qf, kf, ksf, tf, ssf, vf, dof, qb, kb, ksb, tb, ssb, vb, dob = refs[:14]
        dqf, dkf, dksf, dvf, ddf, dqb, dkb, dksb, dvb, ddb = refs[14 + nx:24 + nx]
        dstf, dstb = refs[24 + 2 * nx:26 + 2 * nx]
        if hosted:
            hosted.run(refs[14:14 + nx], refs[24 + nx:24 + 2 * nx], refs[26 + 2 * nx:], pl.program_id(0) * npair + pl.program_id(1), B * npair)

        @pl.when(pl.program_id(1) == 0)
        def _():
            dstf[...] = jnp.zeros_like(dstf)
            dstb[...] = jnp.zeros_like(dstb)

        for sub in range(G):
            step(qf, kf, ksf, vf, tf, ssf, dof, dqf, dkf, dksf, dvf, ddf, dstf, False, G - 1 - sub)
            step(qb, kb, ksb, vb, tb, ssb, dob, dqb, dkb, dksb, dvb, ddb, dstb, True, sub)

    def step(q_ref, k_ref, ks_ref, v_ref, tot_ref, ss_ref, do_ref, dq_ref, dk_ref, dks_ref, dv_ref, dd_ref, dst, reverse, sub):
        rows = slice(C * sub, C * (sub + 1))
        dSn = dst[...]
        S = ss_ref[0, sub]
        q = q_ref[0, rows]
        vb = v_ref[0, rows].astype(BF16)
        dob = do_ref[0, rows].astype(BF16)
        k4 = _block_diag(k_ref[0, rows], GLA_CHUNK, GLA_DK)
        v4 = _block_diag(vb, GLA_CHUNK, GLA_DV)
        s4 = _block_diag(S.astype(BF16), GLA_DV, GLA_DK)
        ds4 = _block_diag(dSn.astype(BF16), GLA_DV, GLA_DK)
        tri = _tri_mask4(reverse)
        P = jnp.where(tri, _dot_nt(q, k4), 0.0).astype(BF16)
        dP = jnp.where(tri, _dot_nt(dob, v4), 0.0).astype(BF16)
        dq_ref[0, rows] = _dot(dob, s4) + _dot(dP, k4)
        dk_ref[0, rows] = _diag_blocks(_dot_tn(dP, q), GLA_CHUNK, GLA_DK)
        dv_ref[0, rows] = _diag_blocks(_dot_tn(P, dob), GLA_CHUNK, GLA_DV) + _dot_nt(ks_ref[0, rows], ds4)
        dks_ref[0, rows] = _dot(vb, ds4)
        dd_ref[0, rows] = jnp.broadcast_to(jnp.sum(dSn * S, axis=0, keepdims=True), (C, GLA_KW))
        dst[...] = jnp.exp(tot_ref[0, C * sub:C * sub + 1, :]) * dSn + _diag_blocks(_dot_tn(dob, q), GLA_DV, GLA_DK)

    in_specs, out_specs, out_shape = [], [], []
    for order in orders:
        tok = lambda n, order=order: pl.BlockSpec((1, G * C, n), lambda b, c: (b, order(c), 0))
        in_specs += [tok(GLA_KW), tok(GLA_KW), tok(GLA_KW), tok(GLA_KW),
                     pl.BlockSpec((1, G, GLA_DV, GLA_KW), lambda b, c, order=order: (b, order(c), 0, 0)), tok(GLA_WIDTH), tok(GLA_WIDTH)]
        out_specs += [tok(GLA_KW), tok(GLA_KW), tok(GLA_KW), tok(GLA_WIDTH), tok(GLA_KW)]
        out_shape += [jax.ShapeDtypeStruct((B, L, GLA_KW), F32)] * 3 + [jax.ShapeDtypeStruct((B, L, GLA_WIDTH), F32), jax.ShapeDtypeStruct((B, L, GLA_KW), F32)]
    outs = pl.pallas_call(
        body, name=name, grid=(B, npair), in_specs=in_specs + (hosted.in_specs if hosted else []),
        out_specs=out_specs + (hosted.out_specs if hosted else []), out_shape=out_shape + (hosted.out_shape if hosted else []),
        scratch_shapes=[pltpu.VMEM((GLA_DV, GLA_KW), F32)] * 2 + (hosted.scratch if hosted else []),
        compiler_params=_params(("arbitrary", "arbitrary") if hosted else ("parallel", "arbitrary")),
    )(*dirs[0], v, do, *dirs[1], v, do, *(hosted.xs if hosted else []))
    return outs[:5], outs[5:10], list(outs[10:])


def gla_out_fwd(of, ob, gn, zg, name):
    T = of.shape[0]

    def body(of_ref, ob_ref, gn_ref, g_ref, y_ref):
        for h in range(GLA_HEADS):
            sl = slice(GLA_DV * h, GLA_DV * (h + 1))
            xn, _ = _rms_rows(of_ref[:, sl] + ob_ref[:, sl])
            g = g_ref[:, sl]
            y_ref[:, sl] = (xn * gn_ref[...] * (g * _sig(g))).astype(BF16)

    tok = pl.BlockSpec((TOKEN_BLOCK, GLA_WIDTH), lambda i: (i, 0))
    return pl.pallas_call(
        body, name=name, grid=(T // TOKEN_BLOCK,),
        in_specs=[tok, tok, _full(gn.shape), tok], out_specs=tok,
        out_shape=jax.ShapeDtypeStruct((T, GLA_WIDTH), BF16),
        compiler_params=_params(("parallel",)),
    )(of, ob, gn, zg)


def gla_out_bwd(of, ob, gn, zg, dy, name):
    T = of.shape[0]

    def body(of_ref, ob_ref, gn_ref, g_ref, dy_ref, do_ref, dzg_ref, dgn_ref):
        first = pl.program_id(0) == 0
        gn_v = gn_ref[...]
        dgn = None
        for h in range(GLA_HEADS):
            sl = slice(GLA_DV * h, GLA_DV * (h + 1))
            xn, r = _rms_rows(of_ref[:, sl] + ob_ref[:, sl])
            silu, dsilu = _silu_and_grad(g_ref[:, sl])
            dyv = dy_ref[:, sl]
            dzg_ref[:, sl] = (dyv * xn * gn_v * dsilu).astype(BF16)
            dn = dyv * silu
            t = jnp.sum(dn * xn, axis=0, keepdims=True)
            dgn = t if dgn is None else dgn + t
            do_ref[:, sl] = _rms_rows_bwd(dn * gn_v, xn, r)
        _acc(dgn_ref, dgn, first)

    tok = pl.BlockSpec((TOKEN_BLOCK, GLA_WIDTH), lambda i: (i, 0))
    return pl.pallas_call(
        body, name=name, grid=(T // TOKEN_BLOCK,),
        in_specs=[tok, tok, _full(gn.shape), tok, tok], out_specs=[tok, tok, _full(gn.shape)],
        out_shape=[jax.ShapeDtypeStruct((T, GLA_WIDTH), F32), jax.ShapeDtypeStruct((T, GLA_WIDTH), BF16), jax.ShapeDtypeStruct(gn.shape, F32)],
        compiler_params=_params(("arbitrary",)),
    )(of, ob, gn, zg, dy)


def merge_post_fwd(ys, zm, wbs, wo, x2, pg, ms, nb, name):
    T = x2.shape[0]

    def body(y0, y1, y2, zm_ref, w0, w1, w2, wo_ref, x_ref, pg_ref, gate_ref, xn_ref, out_ref, mg_ref):
        merged = None
        for i, (y_ref, w_ref) in enumerate(((y0, w0), (y1, w1), (y2, w2))):
            t = _sig(zm_ref[:, D_MODEL * i:D_MODEL * (i + 1)]) * _dot(y_ref[...], w_ref[...])
            merged = t if merged is None else merged + t
        mb = merged.astype(BF16)
        mg_ref[...] = mb
        out = _dot(mb, wo_ref[...])
        out_ref[...] = out
        on, _ = _rms_rows(out)
        xn_ref[...] = x_ref[...] + gate_ref[0] * (on * pg_ref[...])

    tok = lambda n: pl.BlockSpec((TOKEN_BLOCK, n), lambda i: (i, 0))
    return pl.pallas_call(
        body, name=name, grid=(T // TOKEN_BLOCK,),
        in_specs=[tok(512)] * 3 + [tok(3 * D_MODEL)] + [_full(w.shape) for w in wbs] + [_full(wo.shape), tok(D_MODEL), _full(pg.shape), _mod_spec(nb, 2)],
        out_specs=[tok(D_MODEL)] * 3,
        out_shape=[jax.ShapeDtypeStruct((T, D_MODEL), F32), jax.ShapeDtypeStruct((T, D_MODEL), F32), jax.ShapeDtypeStruct((T, D_MODEL), BF16)],
        compiler_params=_params(("parallel",), VMEM_LIMIT),
    )(*ys, zm, *wbs, wo, x2, pg, ms)


def merge_post_bwd(dxn, out, ys, zm, wbs, wo, pg, ms, nb, name):
    T = dxn.shape[0]
    nrow = ms.shape[0]
    row = _mod_row(nb)

    def body(dxn_ref, out_ref, y0, y1, y2, zm_ref, w0, w1, w2, wo_ref, pg_ref, gate_ref,
             dy0, dy1, dy2, dzm_ref, dout_ref, dp0, dp1, dp2, dgate_ref, dpg_ref):
        i = pl.program_id(0)
        dxn_v = dxn_ref[...]
        on, r = _rms_rows(out_ref[...])
        pg_v = pg_ref[...]
        _acc(dgate_ref.at[0], jnp.sum(dxn_v * on * pg_v, axis=0, keepdims=True), (i % nb) <= 1)
        dn = dxn_v * gate_ref[0]
        _acc(dpg_ref, jnp.sum(dn * on, axis=0, keepdims=True), i == 0)
        dout = _rms_rows_bwd(dn * pg_v, on, r).astype(BF16)
        dout_ref[...] = dout
        dmerged = _dot_nt(dout, wo_ref[...])
        for j, (y_ref, w_ref, dy_ref, dp_ref) in enumerate(((y0, w0, dy0, dp0), (y1, w1, dy1, dp1), (y2, w2, dy2, dp2))):
            sl = slice(D_MODEL * j, D_MODEL * (j + 1))
            g = _sig(zm_ref[:, sl])
            p = _dot(y_ref[...], w_ref[...])
            dzm_ref[:, sl] = (dmerged * p * g * (1.0 - g)).astype(BF16)
            dp = (dmerged * g).astype(BF16)
            dp_ref[...] = dp
            dy_ref[...] = _dot_nt(dp, w_ref[...])

    tok = lambda n: pl.BlockSpec((TOKEN_BLOCK, n), lambda i: (i, 0))
    return pl.pallas_call(
        body, name=name, grid=(T // TOKEN_BLOCK,),
        in_specs=[tok(D_MODEL), tok(D_MODEL)] + [tok(512)] * 3 + [tok(3 * D_MODEL)] + [_full(w.shape) for w in wbs] + [_full(wo.shape), _full(pg.shape), _mod_spec(nb, 2)],
        out_specs=[tok(512)] * 3 + [tok(3 * D_MODEL), tok(D_MODEL)] + [tok(D_MODEL)] * 3 + [pl.BlockSpec((1, 1, D_MODEL), lambda i: (row(i), 0, 0)), _full(pg.shape)],
        out_shape=[jax.ShapeDtypeStruct((T, 512), F32)] * 3 + [jax.ShapeDtypeStruct((T, 3 * D_MODEL), BF16), jax.ShapeDtypeStruct((T, D_MODEL), BF16)]
        + [jax.ShapeDtypeStruct((T, D_MODEL), BF16)] * 3 + [jax.ShapeDtypeStruct((nrow, 1, D_MODEL), F32), jax.ShapeDtypeStruct(pg.shape, F32)],
        compiler_params=_params(("arbitrary",), VMEM_LIMIT),
    )(dxn, out, *ys, zm, *wbs, wo, pg, ms)


def loss_head(y2, tgt2, nb, name):
    T = y2.shape[0]
    nlat = nb - 1

    def body(y_ref, t_ref, dy_ref, loss_ref, acc):
        i = pl.program_id(0)
        is_lat = (i % nb) > 0

        @pl.when(i == 0)
        def _():
            acc[...] = jnp.zeros_like(acc)

        @pl.when(is_lat)
        def _():
            e = y_ref[...] - t_ref[...]
            dy_ref[...] = e * (1.0 / D_MODEL)
            acc[...] += jnp.sum(e * e, axis=0, keepdims=True)

        @pl.when(jnp.logical_not(is_lat))
        def _():
            dy_ref[...] = jnp.zeros_like(dy_ref)

        @pl.when(i == pl.num_programs(0) - 1)
        def _():
            loss_ref[...] = jnp.sum(acc[...], axis=1, keepdims=True) * (0.5 / D_MODEL)

    tok = pl.BlockSpec((TOKEN_BLOCK, D_MODEL), lambda i: (i, 0))
    tgt = pl.BlockSpec((TOKEN_BLOCK, D_MODEL), lambda i: ((i // nb) * nlat + jnp.maximum(i % nb - 1, 0), 0))
    return pl.pallas_call(
        body, name=name, grid=(T // TOKEN_BLOCK,),
        in_specs=[tok, tgt], out_specs=[tok, _full((1, 1))],
        out_shape=[jax.ShapeDtypeStruct((T, D_MODEL), F32), jax.ShapeDtypeStruct((1, 1), F32)],
        scratch_shapes=[pltpu.VMEM((1, D_MODEL), F32)],
        compiler_params=_params(("arbitrary",)),
    )(y2, tgt2)


_IN_OFFS = tuple(int(o) for o in np.cumsum((0,) + IN_SIZES))
_IN_GROUPS = (("a", 0, 416, 512), ("mg", 416, 512, 512), ("px", 928, 512, 512), ("pg", 1440, 512, 512), ("gq", 1952, 256, 256),
              ("gk", 2208, 256, 256), ("gv", 2464, 512, 512), ("lr", 2976, 32, 128), ("gg", 3008, 512, 512), ("m", 3520, 3072, 3072))


def _pad_cols(w, n):
    return w if w.shape[1] == n else jnp.pad(w, ((0, 0), (0, n - w.shape[1])))


def layer_weights(w_in, w_uq, w_ukv, af_w2, ab_w2, wbm, wbp, wbg, w_out):
    W = {}
    for nm, off, n, npad in _IN_GROUPS:
        W["in_" + nm] = _pad_cols(w_in[:, off:off + n], npad)
    uq = w_uq.reshape(MLA_Q_RANK, MLA_HEADS, MLA_NOPE + MLA_ROPE)
    W["qn"] = jnp.pad(uq[:, :, :MLA_NOPE], ((0, 0), (0, 0), (0, LANES - MLA_NOPE))).reshape(MLA_Q_RANK, MLA_HEADS * LANES)
    W["qr"] = jnp.pad(uq[:, :, MLA_NOPE:], ((0, 0), (0, 0), (0, LANES - MLA_ROPE))).reshape(MLA_Q_RANK, MLA_HEADS * LANES)
    W["kv"] = w_ukv
    W["af"] = jnp.pad(af_w2, ((0, LANES - GLA_GATE_RANK), (0, 0)))
    W["ab"] = jnp.pad(ab_w2, ((GLA_GATE_RANK, LANES - 2 * GLA_GATE_RANK), (0, 0)))
    W["bm"], W["bp"], W["bg"], W["out"] = wbm, wbp, wbg, w_out
    return W


def rope_tables(L, n_ctx):
    t = np.arange(L - n_ctx)
    half = MLA_ROPE // 2
    inv = ROPE_BASE ** (-np.arange(0, half, 2, dtype=np.float32) / half)
    ang_r = (t // GRID_W).astype(np.float32)[:, None] * inv
    ang_c = (t % GRID_W).astype(np.float32)[:, None] * inv
    ang = jnp.asarray(np.concatenate([ang_r, ang_r, ang_c, ang_c], axis=-1), F32)
    cos = jnp.ones((L, LANES), F32).at[n_ctx:, :MLA_ROPE].set(jnp.cos(ang))
    sin = jnp.zeros((L, LANES), F32).at[n_ctx:, :MLA_ROPE].set(jnp.sin(ang))
    return cos, sin


def layer_fwd(x2, ms, W, P, cos, sin, B, L, n_ctx, tag, hosted=None):
    nb = L // TOKEN_BLOCK
    r3 = lambda a: a.reshape(B, L, a.shape[-1])
    r2 = lambda a: a.reshape(B * L, a.shape[-1])
    names = [g[0] for g in _IN_GROUPS[:-1]]
    h, zs = norm_in_proj(x2, P["pre"], ms, [W["in_" + n] for n in names], nb, tag + "in_proj")
    z = dict(zip(names, zs))
    (z["m"],) = mm_multi(h, [W["in_m"]], [F32], tag + "in_proj_merge")
    qn, qr, kv, kr = mla_prep_fwd(z["a"], P["qg"], P["kvg"], W["qn"], W["qr"], W["kv"], cos, sin, nb, tag + "mla_prep")
    ya, y_mla, lse, carried = attention_fwd(r3(qn), r3(qr), r3(kv), r3(kr), r3(z["mg"]), n_ctx, tag + "attention", hosted)
    y_pool = pool_fwd(r3(z["px"]), r3(z["pg"]), P["pw"], P["ps"], n_ctx, tag + "pool")
    qf, kf, ksf, tf, qb, kb, ksb, tb = gla_prep_fwd(z["lr"], z["gq"], z["gk"], W["af"], W["ab"], P["baf"], P["bab"], tag + "gla_prep")
    (of, ssf), (ob, ssb) = gla_scan_fwd([(r3(qf), r3(kf), r3(ksf), r3(tf)), (r3(qb), r3(kb), r3(ksb), r3(tb))], r3(z["gv"]), n_ctx, tag + "gla_scan")
    y_gla = gla_out_fwd(r2(of), r2(ob), P["gn"], z["gg"], tag + "gla_out")
    ys = [r2(y_mla), r2(y_pool), y_gla]
    x_new, out, merged = merge_post_fwd(ys, z["m"], [W["bm"], W["bp"], W["bg"]], W["out"], x2, P["post"], ms, nb, tag + "merge_post")
    res = dict(x2=x2, h=h, z=z, qn=qn, qr=qr, kv=kv, kr=kr, ya=ya, lse=lse, ys=ys, gla_f=(qf, kf, ksf, tf, ssf), gla_b=(qb, kb, ksb, tb, ssb),
               of=of, ob=ob, out=out, merged=merged)
    return x_new, res, carried


def layer_bwd(dxn, res, ms, W, P, cos, sin, B, L, n_ctx, tag, hosted=None, host_own=None):
    nb = L // TOKEN_BLOCK
    r3 = lambda a: a.reshape(B, L, a.shape[-1])
    r2 = lambda a: a.reshape(B * L, a.shape[-1])
    z = res["z"]
    ys = res["ys"]
    wbs = [W["bm"], W["bp"], W["bg"]]
    dy0, dy1, dy2, dzm, dout, dp0, dp1, dp2, dgate, dpost = merge_post_bwd(dxn, res["out"], ys, z["m"], wbs, W["out"], P["post"], ms, nb, tag + "merge_post_bwd")
    G = {"out": mm_dw(res["merged"], dout, tag + "dw_out"), "post": dpost}
    for nm, y, dp in zip(("bm", "bp", "bg"), ys, (dp0, dp1, dp2)):
        G[nm] = mm_dw(y, dp, tag + "dw_" + nm)
    g = {n: _natural_grad(G, n) for n in GRADS_EARLY}
    own = {}
    dz = {"m": dzm}
    do, dz["gg"], G["gn"] = gla_out_bwd(r2(res["of"]), r2(res["ob"]), P["gn"], z["gg"], dy2, tag + "gla_out_bwd")
    carrier = host_own(GRADS_EARLY, g) if host_own else None
    *grads, arrived = gla_scan_bwd([(r3(qt), r3(kt), r3(ks), r3(tot), ss) for qt, kt, ks, tot, ss in (res["gla_f"], res["gla_b"])],
                                   r3(z["gv"]), r3(do), n_ctx, tag + "gla_scan_bwd", carrier)
    own.update(zip(GRADS_EARLY, arrived))
    gf = [r2(a) for a in grads[0]]
    gb = [r2(a) for a in grads[1]]
    dz["lr"], dz["gq"], dz["gk"], G["af"], G["ab"], G["baf"], G["bab"], dz["gv"] = gla_prep_bwd(
        z["lr"], z["gq"], z["gk"], W["af"], W["ab"], P["baf"], P["bab"], gf[:3] + gf[4:], gb[:3] + gb[4:], [gf[3], gb[3]], tag + "gla_prep_bwd")
    dpx, dpg, G["pw"], G["ps"] = pool_bwd(r3(z["px"]), r3(z["pg"]), P["pw"], P["ps"], r3(dy1), n_ctx, tag + "pool_bwd")
    dz["px"], dz["pg"] = r2(dpx), r2(dpg)
    dqn, dqr, dkv, dkr, dzmg, got = attention_bwd(r3(res["qn"]), r3(res["qr"]), r3(res["kv"]), r3(res["kr"]), r3(z["mg"]), res["ya"], res["lse"],
                                                  r3(dy0), n_ctx, tag + "attention_bwd", hosted)
    dz["mg"] = r2(dzmg)
    dz["a"], G["qn"], G["qr"], G["kv"], G["qg"], G["kvg"] = mla_prep_bwd(
        r2(dqn), r2(dqr), r2(dkv), r2(dkr), z["a"], P["qg"], P["kvg"], W["qn"], W["qr"], W["kv"], cos, sin, nb, tag + "mla_prep_bwd")
    names = [grp[0] for grp in _IN_GROUPS]
    for n in names:
        G["in_" + n] = mm_dw(res["h"], dz[n], tag + "dw_in_" + n)
    g.update({n: _natural_grad(G, n) for n in GRADS_LATE})
    carrier = host_own(GRADS_LATE, g) if host_own else None
    (dx, dshift, dscale, G["pre"]), arrived = in_proj_norm_bwd([dz[n] for n in names], [W["in_" + n] for n in names], res["x2"], P["pre"], ms, dxn, nb,
                                                               tag + "in_proj_dx", carrier)
    own.update(zip(GRADS_LATE, arrived))
    g.update({n: _natural_grad(G, n) for n in GRADS_REPLICATED})
    dms = jnp.concatenate([dshift, dscale, dgate], axis=-1)
    return dx, g, dms, got, own


def add_cast(a, b, name):
    T, n = a.shape

    def body(a_ref, b_ref, o_ref):
        o_ref[...] = (a_ref[...] + b_ref[...]).astype(BF16)

    tok = pl.BlockSpec((TOKEN_BLOCK, n), lambda i: (i, 0))
    return pl.pallas_call(body, name=name, grid=(T // TOKEN_BLOCK,), in_specs=[tok, tok], out_specs=tok,
                          out_shape=jax.ShapeDtypeStruct((T, n), BF16), compiler_params=_params(("parallel",)))(a, b)


GRADS_EARLY = ("w_branch_mla", "w_branch_pool", "w_branch_gla", "w_out")
GRADS_LATE = ("w_in", "mla_w_uq", "mla_w_ukv", "gla_af_w2", "gla_ab_w2")
GRADS_REPLICATED = ("pre_norm", "post_norm", "mla_q_norm", "mla_kv_norm", "pool_w", "pool_scale", "gla_af_b", "gla_ab_b", "gla_norm")
_DIRECT = dict(mla_w_ukv="kv", w_branch_mla="bm", w_branch_pool="bp", w_branch_gla="bg", w_out="out", pool_w="pw")
_ROW = dict(pre_norm="pre", post_norm="post", mla_q_norm="qg", mla_kv_norm="kvg", pool_scale="ps", gla_af_b="baf", gla_ab_b="bab", gla_norm="gn")


def _natural_grad(G, name):
    if name == "w_in":
        parts = {off: G["in_" + nm][:, :n] for nm, off, n, npad in _IN_GROUPS}
        return jnp.concatenate([parts[o] for o in sorted(parts)], axis=1)
    if name == "mla_w_uq":
        gqn = G["qn"].reshape(MLA_Q_RANK, MLA_HEADS, LANES)[:, :, :MLA_NOPE]
        gqr = G["qr"].reshape(MLA_Q_RANK, MLA_HEADS, LANES)[:, :, :MLA_ROPE]
        return jnp.concatenate([gqn, gqr], axis=-1).reshape(MLA_Q_RANK, MLA_HEADS * (MLA_NOPE + MLA_ROPE))
    if name == "gla_af_w2":
        return G["af"][:GLA_GATE_RANK]
    if name == "gla_ab_w2":
        return G["ab"][GLA_GATE_RANK:2 * GLA_GATE_RANK]
    return G[_DIRECT[name]] if name in _DIRECT else G[_ROW[name]][0]


def local_step(x, c, ctx, c_ctx, small, loss_target, depth, layer_full, host_fwd=None, host_bwd=None, host_own=None):
    B, S, _ = x.shape
    n_ctx = ctx.shape[1]
    L = n_ctx + S
    nb = L // TOKEN_BLOCK
    cos, sin = rope_tables(L, n_ctx)
    x2 = jnp.concatenate([ctx, x], axis=1).reshape(B * L, D_MODEL)
    a8 = jnp.zeros((8, D_MODEL), F32).at[:B].set(c).at[B].set(c_ctx)
    Ws, Ps, mss, ress, mod_ws = [], [], [], [], []
    carried = None
    for l in range(depth):
        tag = f"l{l}_"
        full = layer_full(l, carried)
        W = layer_weights(full["w_in"], full["mla_w_uq"], full["mla_w_ukv"], full["gla_af_w2"], full["gla_ab_w2"],
                          full["w_branch_mla"], full["w_branch_pool"], full["w_branch_gla"], full["w_out"])
        P = dict(pre=small["pre_norm"][l][None], post=small["post_norm"][l][None], qg=small["mla_q_norm"][l][None], kvg=small["mla_kv_norm"][l][None],
                 pw=small["pool_w"][l].astype(BF16), ps=small["pool_scale"][l][None], baf=small["gla_af_b"][l][None], bab=small["gla_ab_b"][l][None],
                 gn=small["gla_norm"][l][None])
        mod8 = mod_fwd(a8, full["mod_w"], small["mod_b"][l][None], tag + "mod")
        ms = jnp.stack([jnp.broadcast_to(mod8[B], (B, 3 * D_MODEL)), mod8[:B]], axis=1).reshape(2 * B, 1, 3 * D_MODEL)
        x2, res, carried = layer_fwd(x2, ms, W, P, cos, sin, B, L, n_ctx, tag, host_fwd(l) if host_fwd else None)
        Ws.append(W), Ps.append(P), mss.append(ms), ress.append(res), mod_ws.append(full["mod_w"])
    dx, loss = loss_head(x2, loss_target.reshape(B * S, D_MODEL), nb, "loss_head")
    grads = [None] * depth
    delivered = [None] * depth
    dz8s = [None] * depth
    da8 = None
    for l in reversed(range(depth)):
        tag = f"l{l}_"
        hosted = host_bwd(l, grads[l + 1]) if host_bwd and l + 1 < depth else None
        dx, g, dms, got, own = layer_bwd(dx, ress[l], mss[l], Ws[l], Ps[l], cos, sin, B, L, n_ctx, tag, hosted, host_own(l) if host_own else None)
        if hosted:
            delivered[l + 1] = got
        if own:
            delivered[l] = own
        dms = dms.reshape(B, 2, 3 * D_MODEL)
        dz8s[l] = jnp.zeros((8, 3 * D_MODEL), F32).at[:B].set(dms[:, 1]).at[B].set(jnp.sum(dms[:, 0], axis=0))
        g_mod_b, da = mod_bwd(a8, mod_ws[l], dz8s[l], tag + "mod_bwd")
        da8 = da if da8 is None else da8 + da
        g["mod_b"] = g_mod_b[0]
        grads[l] = g
    grad_x = dx.reshape(B, L, D_MODEL)[:, n_ctx:]
    return loss, grad_x, grads, da8[B], delivered, (a8, dz8s)


_MESH_ID = pl.DeviceIdType.MESH
_HBM = pl.BlockSpec(memory_space=pltpu.HBM)


def _me_and_peers():
    mx, my, mc = lax.axis_index("x"), lax.axis_index("y"), lax.axis_index("c")
    peers = []
    for k in range(1, N_DEV):
        px, py, pc = mx ^ ((k >> 2) & 1), my ^ ((k >> 1) & 1), mc ^ (k & 1)
        peers.append(((px, py, pc), 4 * px + 2 * py + pc))
    return 4 * mx + 2 * my + mc, peers


def _comm_scratch(n):
    return [pltpu.SemaphoreType.DMA((n * (N_DEV - 1),)), pltpu.SemaphoreType.DMA((n * (N_DEV - 1),)), pltpu.SemaphoreType.DMA((n,))]


class _Gather:
    def __init__(self, x_refs, o_refs, send_sems, recv_sems, local_sems):
        self.x, self.o, self.send, self.recv, self.local = x_refs, o_refs, send_sems, recv_sems, local_sems
        self.n = len(x_refs)
        mx, my, mc = lax.axis_index("x"), lax.axis_index("y"), lax.axis_index("c")
        self.me, self.sibling, self.mc = (mx, my, mc), (mx, my, 1 - mc), mc
        self.chips = [(1 - mx, my), (mx, 1 - my), (1 - mx, 1 - my)]

    @staticmethod
    def out_shape(xs):
        return [jax.ShapeDtypeStruct((N_DEV,) + x.shape, x.dtype) for x in xs]

    def _copy(self, i, k, block, to, src=None):
        px, py, pc = block
        dst = self.o[i].at[4 * px + 2 * py + pc]
        sem = (N_DEV - 1) * i + k
        return pltpu.make_async_remote_copy(src_ref=dst if src is None else src, dst_ref=dst, send_sem=self.send.at[sem],
                                            recv_sem=self.recv.at[sem], device_id=to, device_id_type=_MESH_ID)

    def _mine(self, i):
        mx, my, mc = self.me
        return pltpu.make_async_copy(self.x[i], self.o[i].at[4 * mx + 2 * my + mc], self.local.at[i])

    def _first(self):
        out = []
        for i in range(self.n):
            out.append(self._copy(i, 0, self.me, self.sibling, src=self.x[i]))
            out += [self._copy(i, 1 + j, self.me, (*chip, self.mc), src=self.x[i]) for j, chip in enumerate(self.chips)]
        return out

    def _passed(self, j, i):
        return self._copy(i, 4 + j, (*self.chips[j], self.mc), self.sibling)

    def start(self):
        for i in range(self.n):
            self._mine(i).start()
        for cp in self._first():
            cp.start()

    def forward(self):
        for j, chip in enumerate(self.chips):
            for i in range(self.n):
                self._copy(i, 1 + j, (*chip, self.mc), self.me).wait_recv()
                self._passed(j, i).start()

    def finish(self):
        for i in range(self.n):
            self._copy(i, 0, self.sibling, self.me).wait_recv()
            for j, chip in enumerate(self.chips):
                self._copy(i, 4 + j, (*chip, 1 - self.mc), self.me).wait_recv()
        for cp in self._first():
            cp.wait_send()
        for j in range(len(self.chips)):
            for i in range(self.n):
                self._passed(j, i).wait_send()
        for i in range(self.n):
            self._mine(i).wait()


class _Scatter:
    def __init__(self, x_refs, o_refs, send_sems, recv_sems, local_sems):
        self.x, self.o, self.send, self.recv, self.local = x_refs, o_refs, send_sems, recv_sems, local_sems
        self.n = len(x_refs)
        self.me, self.peers = _me_and_peers()

    @staticmethod
    def out_shape(xs):
        return [jax.ShapeDtypeStruct(x.shape, x.dtype) for x in xs]

    def _copy(self, i, k, src_slot, dst_slot, to):
        sem = (N_DEV - 1) * i + k
        return pltpu.make_async_remote_copy(src_ref=self.x[i].at[src_slot], dst_ref=self.o[i].at[dst_slot], send_sem=self.send.at[sem],
                                            recv_sem=self.recv.at[sem], device_id=to, device_id_type=_MESH_ID)

    def _mine(self, i):
        return pltpu.make_async_copy(self.x[i].at[self.me], self.o[i].at[self.me], self.local.at[i])

    def _sends(self):
        return [self._copy(i, k, slot, self.me, peer) for k, (peer, slot) in enumerate(self.peers) for i in range(self.n)]

    def start(self):
        for i in range(self.n):
            self._mine(i).start()
        for cp in self._sends():
            cp.start()

    def forward(self):
        pass

    def finish(self):
        for k, (peer, slot) in enumerate(self.peers):
            for i in range(self.n):
                self._copy(i, k, slot, slot, peer).wait_recv()
        for cp in self._sends():
            cp.wait_send()
        for i in range(self.n):
            self._mine(i).wait()


class _Hosted:
    def __init__(self, kind, xs):
        self.kind, self.xs, self.n = kind, list(xs), len(xs)
        self.in_specs = [_HBM] * self.n
        self.out_specs = [_HBM] * self.n
        self.out_shape = kind.out_shape(self.xs)
        self.scratch = _comm_scratch(self.n)

    def run(self, x_refs, o_refs, sems, step, total):
        for when, phase in ((0, "start"), (total // 2, "forward"), (total - 1, "finish")):
            @pl.when(step == when)
            def _(phase=phase):
                getattr(self.kind(x_refs, o_refs, *sems), phase)()


def gather_blocks(xs, name):
    n = len(xs)

    def body(*refs):
        g = _Gather(refs[:n], refs[n:2 * n], *refs[2 * n:])
        g.start()
        g.forward()
        g.finish()

    return pl.pallas_call(
        body, name=name, in_specs=[_HBM] * n, out_specs=[_HBM] * n,
        out_shape=_Gather.out_shape(xs), scratch_shapes=_comm_scratch(n),
    )(*xs)


def scatter_blocks(xs, name):
    n = len(xs)

    def body(*refs):
        s = _Scatter(refs[:n], refs[n:2 * n], *refs[2 * n:])
        s.start()
        s.finish()

    return pl.pallas_call(
        body, name=name, in_specs=[_HBM] * n, out_specs=[_HBM] * n,
        out_shape=_Scatter.out_shape(xs), scratch_shapes=_comm_scratch(n),
    )(*xs)


def reduce_adamw(slots, w, m, v, name, tr=256):
    R, C = w.shape
    nl = len(slots)
    ns = slots[0].shape[0]
    rows = R // nl
    tr = min(tr, rows)
    nbl = rows // tr
    c1 = 1.0 / (1.0 - ADAM_B1 ** ADAM_STEP)
    c2 = 1.0 / (1.0 - ADAM_B2 ** ADAM_STEP)

    def body(*refs):
        w_ref, m_ref, v_ref, g_ref, d_ref, nm_ref, nv_ref = refs[nl:]
        part = pl.program_id(0) // nbl
        g = None
        for l, s_ref in enumerate(refs[:nl]):
            gl = s_ref[0].astype(F32)
            for s in range(1, ns):
                gl = gl + s_ref[s].astype(F32)
            g = gl if g is None else jnp.where(part == l, gl, g)
        nm = ADAM_B1 * m_ref[...] + (1.0 - ADAM_B1) * g
        nv = ADAM_B2 * v_ref[...] + (1.0 - ADAM_B2) * (g * g)
        g_ref[...] = g
        nm_ref[...] = nm
        nv_ref[...] = nv
        d_ref[...] = -ADAM_LR * ((nm * c1) / (jnp.sqrt(nv * c2) + ADAM_EPS) + ADAM_WD * w_ref[...])

    blk = pl.BlockSpec((tr, C), lambda i: (i, 0))
    sspecs = [pl.BlockSpec((ns, tr, C), lambda i, l=l: (0, jnp.clip(i - l * nbl, 0, nbl - 1), 0)) for l in range(nl)]
    return pl.pallas_call(
        body, name=name, grid=(R // tr,),
        in_specs=sspecs + [blk, blk, blk], out_specs=[blk] * 4,
        out_shape=[jax.ShapeDtypeStruct((R, C), F32)] * 4,
        compiler_params=_params(("parallel",), VMEM_LIMIT),
    )(*slots, w, m, v)


ARG_WEIGHTS = ("c_ctx", "mod_w", "mod_b", "pre_norm", "post_norm", "w_in", "mla_q_norm", "mla_w_uq", "mla_kv_norm", "mla_w_ukv", "pool_w",
               "pool_scale", "gla_af_w2", "gla_af_b", "gla_ab_w2", "gla_ab_b", "gla_norm", "w_branch_mla", "w_branch_pool", "w_branch_gla", "w_out")
SHARDED = ("mod_w", "w_in", "mla_w_uq", "mla_w_ukv", "gla_af_w2", "gla_ab_w2", "w_branch_mla", "w_branch_pool", "w_branch_gla", "w_out")
ROW_SHARDED = ("w_out",)
REPLICATED = tuple(n for n in ARG_WEIGHTS if n not in SHARDED)
PACK_ROWS = 512


def _pack(parts, dtype):
    flat = jnp.concatenate([p.astype(dtype).reshape(-1) for p in parts])
    n = flat.shape[0]
    total = -(-n // (PACK_ROWS * LANES)) * (PACK_ROWS * LANES)
    return jnp.pad(flat, (0, total - n)).reshape(total // LANES, LANES)


def _unpack(buf, shapes):
    flat = buf.reshape(-1)
    out, off = [], 0
    for shp in shapes:
        n = math.prod(shp)
        out.append(flat[off:off + n].reshape(shp))
        off += n
    return out


def _gathered_to_full(g, name):
    _, r, cs = g.shape
    if name in ROW_SHARDED:
        return g.reshape(N_DEV * r, cs)
    return g.transpose(1, 0, 2).reshape(r, N_DEV * cs)


def _full_to_slots(w, name):
    if name in ROW_SHARDED:
        return w.reshape(N_DEV, w.shape[0] // N_DEV, w.shape[1])
    return w.reshape(w.shape[0], N_DEV, w.shape[1] // N_DEV).transpose(1, 0, 2)


def kernel(x, c, ctx, c_ctx, mod_w, mod_b, pre_norm, post_norm, w_in, mla_q_norm, mla_w_uq, mla_kv_norm, mla_w_ukv, pool_w, pool_scale, gla_af_w2, gla_af_b, gla_ab_w2, gla_ab_b, gla_norm, w_branch_mla, w_branch_pool, w_branch_gla, w_out, loss_target, m_c_ctx, m_mod_w, m_mod_b, m_pre_norm, m_post_norm, m_w_in, m_mla_q_norm, m_mla_w_uq, m_mla_kv_norm, m_mla_w_ukv, m_pool_w, m_pool_scale, m_gla_af_w2, m_gla_af_b, m_gla_ab_w2, m_gla_ab_b, m_gla_norm, m_w_branch_mla, m_w_branch_pool, m_w_branch_gla, m_w_out, v_c_ctx, v_mod_w, v_mod_b, v_pre_norm, v_post_norm, v_w_in, v_mla_q_norm, v_mla_w_uq, v_mla_kv_norm, v_mla_w_ukv, v_pool_w, v_pool_scale, v_gla_af_w2, v_gla_af_b, v_gla_ab_w2, v_gla_ab_b, v_gla_norm, v_w_branch_mla, v_w_branch_pool, v_w_branch_gla, v_w_out):
    local = dict(locals())
    wts = {n: local[n] for n in ARG_WEIGHTS}
    mom1 = {n: local["m_" + n] for n in ARG_WEIGHTS}
    mom2 = {n: local["v_" + n] for n in ARG_WEIGHTS}
    shard_shapes = [wts[n].shape for n in SHARDED]
    rep_shapes = [wts[n].shape for n in REPLICATED]
    kinds = ("grad", "delta", "new_m", "new_v")

    depth = w_in.shape[0]

    def shards(l):
        return [wts[n][l].astype(BF16) for n in SHARDED]

    first = gather_blocks(shards(0), "gather_weights_l0")

    def layer_full(l, carried):
        return {n: _gathered_to_full(gw, n) for n, gw in zip(SHARDED, first if l == 0 else carried)}

    def host_fwd(l):
        return _Hosted(_Gather, shards(l + 1)) if l + 1 < depth else None

    exchanged = GRADS_EARLY + GRADS_LATE

    def slots(g, names):
        return [_full_to_slots(g[n], n).astype(BF16) for n in names]

    def host_bwd(l, g_above):
        return _Hosted(_Scatter, slots(g_above, exchanged))

    def host_own(l):
        return (lambda names, g: _Hosted(_Scatter, slots(g, names))) if l == 0 else None

    small = {n: wts[n] for n in REPLICATED}
    loss, grad_x, grads, g_c_ctx, arrived, (a8, dz8s) = local_step(x, c, ctx, c_ctx, small, loss_target, depth, layer_full, host_fwd, host_bwd, host_own)
    arrived = [a if isinstance(a, dict) else dict(zip(exchanged, a)) for a in arrived]

    g = {n: (g_c_ctx if n == "c_ctx" else jnp.stack([grads[l][n] for l in range(depth)])) for n in REPLICATED}
    gathered, a_all, dz_all = gather_blocks([_pack([g[n] for n in REPLICATED], F32), a8, jnp.concatenate(dz8s, axis=0)], "gather_small_grads")
    outs = reduce_adamw([gathered], _pack([wts[n] for n in REPLICATED], F32), _pack([mom1[n] for n in REPLICATED], F32),
                        _pack([mom2[n] for n in REPLICATED], F32), "adamw_replicated")
    me = 4 * lax.axis_index("x") + 2 * lax.axis_index("y") + lax.axis_index("c")
    ncol = mod_w.shape[2]
    dz_cols = lax.dynamic_slice_in_dim(dz_all.reshape(N_DEV, depth, 8, 3 * D_MODEL), me * ncol, ncol, axis=3)
    g_mod_w = mod_dw_columns(a_all.reshape(N_DEV * 8, D_MODEL), dz_cols.transpose(1, 0, 2, 3).reshape(depth, N_DEV * 8, ncol), "mod_dw")

    res = {kind: {} for kind in kinds}
    for n, shp in zip(SHARDED, shard_shapes):
        flat = (shp[0] * shp[1], shp[2])
        parts = [g_mod_w.reshape((1,) + flat)] if n == "mod_w" else [arrived[l][n] for l in range(depth)]
        for kind, o in zip(kinds, reduce_adamw(parts, wts[n].reshape(flat), mom1[n].reshape(flat), mom2[n].reshape(flat), "adamw_" + n)):
            res[kind][n] = o.reshape(shp)
    for kind, o in zip(("grad", "delta", "new_m", "new_v"), outs):
        res[kind].update(zip(REPLICATED, _unpack(o, rep_shapes)))

    loss = lax.psum(loss[0, 0], ("x", "y", "c"))
    return (loss, grad_x, *[res[kind][n] for kind in ("grad", "delta", "new_m", "new_v") for n in ARG_WEIGHTS])
```

```python
import functools
import math

import jax
import jax.numpy as jnp
import numpy as np
from jax import lax
from jax.experimental import pallas as pl
from jax.experimental.pallas import tpu as pltpu

F32 = jnp.float32
BF16 = jnp.bfloat16

D_MODEL = 1024
NORM_EPS = 1e-6
GRID_W = 64
MLA_HEADS, MLA_Q_RANK, MLA_KV_RANK, MLA_NOPE, MLA_ROPE, MLA_V = 8, 256, 128, 64, 32, 64
MLA_WIDTH = MLA_HEADS * MLA_V
ROPE_BASE = 10000.0
ATT_SCALE = (MLA_NOPE + MLA_ROPE) ** -0.5
POOL_WINDOWS = (2, 4, 8, 16)
POOL_WIDTH, POOL_GROUP = 512, 128
GLA_HEADS, GLA_DK, GLA_DV = 4, 64, 128
GLA_KW, GLA_WIDTH = GLA_HEADS * GLA_DK, GLA_HEADS * GLA_DV
GLA_GATE_RANK, GLA_TAU, GLA_CHUNK = 16, 16.0, 64
IN_SIZES = (256, 128, 32, 512, 512, 512, 256, 256, 512, 16, 16, 512, 3 * D_MODEL)
ADAM_LR, ADAM_B1, ADAM_B2, ADAM_EPS, ADAM_WD, ADAM_STEP = 0.001, 0.9, 0.999, 1e-08, 0.01, 10
N_DEV = 8

LANES = 128
TOKEN_BLOCK = 256
VMEM_LIMIT = 48 * 1024 * 1024
NEG_BIG = -1e30

_NT = (((1,), (1,)), ((), ()))
_TN = (((0,), (0,)), ((), ()))


def _dot(a, b):
    return jnp.dot(a, b, preferred_element_type=F32)


def _dot_nt(a, b):
    return lax.dot_general(a, b, _NT, preferred_element_type=F32)


def _dot_tn(a, b):
    return lax.dot_general(a, b, _TN, preferred_element_type=F32)


def _params(sem=None, vmem=None):
    kw = {}
    if sem is not None:
        kw["dimension_semantics"] = sem
    if vmem is not None:
        kw["vmem_limit_bytes"] = vmem
    return pltpu.CompilerParams(**kw)


def _full(shape):
    n = len(shape)
    return pl.BlockSpec(shape, lambda *_: (0,) * n)


def _sig(x):
    return 1.0 / (1.0 + jnp.exp(-x))


def _silu_and_grad(x):
    s = _sig(x)
    return x * s, s * (1.0 + x * (1.0 - s))


def _acc(ref, val, first):
    @pl.when(first)
    def _():
        ref[...] = val

    @pl.when(jnp.logical_not(first))
    def _():
        ref[...] += val


def mm_multi(a, ws, dtypes, name, tm=TOKEN_BLOCK):
    M, K = a.shape
    nw = len(ws)

    def body(a_ref, *refs):
        av = a_ref[...]
        for w_ref, o_ref in zip(refs[:nw], refs[nw:]):
            o_ref[...] = _dot(av, w_ref[...]).astype(o_ref.dtype)

    return pl.pallas_call(
        body, name=name, grid=(M // tm,),
        in_specs=[pl.BlockSpec((tm, K), lambda i: (i, 0))] + [_full(w.shape) for w in ws],
        out_specs=[pl.BlockSpec((tm, w.shape[1]), lambda i: (i, 0)) for w in ws],
        out_shape=[jax.ShapeDtypeStruct((M, w.shape[1]), dt) for w, dt in zip(ws, dtypes)],
        compiler_params=_params(("parallel",), VMEM_LIMIT),
    )(a, *ws)


def mm_dw(a, dz, name, tn=1024):
    M, K = a.shape
    n = dz.shape[1]
    tn = min(tn, n)
    tk = next(t for t in (3072, 1536, 1024, 512, TOKEN_BLOCK) if M % t == 0)

    def body(a_ref, dz_ref, o_ref):
        _acc(o_ref, _dot_tn(a_ref[...], dz_ref[...]), pl.program_id(1) == 0)

    return pl.pallas_call(
        body, name=name, grid=(n // tn, M // tk),
        in_specs=[pl.BlockSpec((tk, K), lambda j, k: (k, 0)), pl.BlockSpec((tk, tn), lambda j, k: (k, j))],
        out_specs=pl.BlockSpec((K, tn), lambda j, k: (0, j)),
        out_shape=jax.ShapeDtypeStruct((K, n), F32),
        compiler_params=_params(("parallel", "arbitrary"), VMEM_LIMIT),
    )(a, dz)


def mod_fwd(a8, w, b, name):
    tn = D_MODEL

    def body(a_ref, w_ref, b_ref, o_ref):
        a = a_ref[...]
        o_ref[...] = _dot((a * _sig(a)).astype(BF16), w_ref[...]) + b_ref[...]

    return pl.pallas_call(
        body, name=name, grid=(3,),
        in_specs=[_full(a8.shape), pl.BlockSpec((D_MODEL, tn), lambda j: (0, j)), pl.BlockSpec((1, tn), lambda j: (0, j))],
        out_specs=pl.BlockSpec((8, tn), lambda j: (0, j)),
        out_shape=jax.ShapeDtypeStruct((8, 3 * D_MODEL), F32),
        compiler_params=_params(("parallel",)),
    )(a8, w, b)


def mod_bwd(a8, w, dz8, name):
    tn = D_MODEL

    def body(a_ref, w_ref, dz_ref, db_ref, da_ref):
        a = a_ref[...]
        _, dsa = _silu_and_grad(a)
        dz = dz_ref[...]
        db_ref[...] = jnp.sum(dz, axis=0, keepdims=True)
        _acc(da_ref, _dot_nt(dz.astype(BF16), w_ref[...]) * dsa, pl.program_id(0) == 0)

    return pl.pallas_call(
        body, name=name, grid=(3,),
        in_specs=[_full(a8.shape), pl.BlockSpec((D_MODEL, tn), lambda j: (0, j)), pl.BlockSpec((8, tn), lambda j: (0, j))],
        out_specs=[pl.BlockSpec((1, tn), lambda j: (0, j)), _full((8, D_MODEL))],
        out_shape=[jax.ShapeDtypeStruct((1, 3 * D_MODEL), F32), jax.ShapeDtypeStruct((8, D_MODEL), F32)],
        compiler_params=_params(("arbitrary",)),
    )(a8, w, dz8)


def mod_dw_columns(a_all, dz_cols, name):
    depth, R, n = dz_cols.shape

    def body(a_ref, dz_ref, dw_ref):
        a = a_ref[...]
        dw_ref[0] = _dot_tn((a * _sig(a)).astype(BF16), dz_ref[0].astype(BF16))

    return pl.pallas_call(
        body, name=name, grid=(depth,),
        in_specs=[_full(a_all.shape), pl.BlockSpec((1, R, n), lambda l: (l, 0, 0))],
        out_specs=pl.BlockSpec((1, D_MODEL, n), lambda l: (l, 0, 0)),
        out_shape=jax.ShapeDtypeStruct((depth, D_MODEL, n), F32),
        compiler_params=_params(("parallel",)),
    )(a_all, dz_cols)


def _mod_row(nb):
    return lambda i: 2 * (i // nb) + jnp.minimum(i % nb, 1)


def _mod_spec(nb, part):
    row = _mod_row(nb)
    return pl.BlockSpec((1, 1, D_MODEL), lambda i: (row(i), 0, part))


def norm_in_proj(x2, g, ms, ws, nb, name):
    T = x2.shape[0]
    nw = len(ws)

    def body(x_ref, g_ref, sh_ref, sc_ref, *refs):
        x = x_ref[...]
        r = lax.rsqrt(jnp.mean(x * x, axis=-1, keepdims=True) + NORM_EPS)
        h = ((x * r) * g_ref[...] * (1.0 + sc_ref[0]) + sh_ref[0]).astype(BF16)
        refs[nw][...] = h
        for w_ref, o_ref in zip(refs[:nw], refs[nw + 1:]):
            o_ref[...] = _dot(h, w_ref[...])

    tok = lambda n: pl.BlockSpec((TOKEN_BLOCK, n), lambda i: (i, 0))
    outs = pl.pallas_call(
        body, name=name, grid=(T // TOKEN_BLOCK,),
        in_specs=[tok(D_MODEL), _full((1, D_MODEL)), _mod_spec(nb, 0), _mod_spec(nb, 1)] + [_full(w.shape) for w in ws],
        out_specs=[tok(D_MODEL)] + [tok(w.shape[1]) for w in ws],
        out_shape=[jax.ShapeDtypeStruct((T, D_MODEL), BF16)] + [jax.ShapeDtypeStruct((T, w.shape[1]), F32) for w in ws],
        compiler_params=_params(("parallel",), VMEM_LIMIT),
    )(x2, g, ms, ms, *ws)
    return outs[0], outs[1:]


def in_proj_norm_bwd(dzs, ws, x2, g, ms, dxres, nb, name, hosted=None):
    T = x2.shape[0]
    nw = len(ws)
    nx = hosted.n if hosted else 0
    nrow = ms.shape[0]
    row = _mod_row(nb)
    n_in = 2 * nw + 4

    def body(*refs):
        x_ref, g_ref, sc_ref, dxr_ref = refs[2 * nw:n_in]
        dx_ref, dsh_ref, dsc_ref, dg_ref = refs[n_in + nx:n_in + nx + 4]
        i = pl.program_id(0)
        if hosted:
            hosted.run(refs[n_in:n_in + nx], refs[n_in + nx + 4:n_in + 2 * nx + 4], refs[n_in + 2 * nx + 4:], i, T // TOKEN_BLOCK)
        dh = None
        for dz_ref, w_ref in zip(refs[:nw], refs[nw:2 * nw]):
            t = _dot_nt(dz_ref[...], w_ref[...])
            dh = t if dh is None else dh + t
        x = x_ref[...]
        g = g_ref[...]
        r = lax.rsqrt(jnp.mean(x * x, axis=-1, keepdims=True) + NORM_EPS)
        xn = x * r
        du = dh * (1.0 + sc_ref[0])
        dyg = du * g
        dx_ref[...] = dxr_ref[...] + r * (dyg - xn * jnp.mean(dyg * xn, axis=-1, keepdims=True))
        first = (i % nb) <= 1
        _acc(dsh_ref.at[0], jnp.sum(dh, axis=0, keepdims=True), first)
        _acc(dsc_ref.at[0], jnp.sum(dh * xn * g, axis=0, keepdims=True), first)
        _acc(dg_ref, jnp.sum(du * xn, axis=0, keepdims=True), i == 0)

    tok = lambda n: pl.BlockSpec((TOKEN_BLOCK, n), lambda i: (i, 0))
    acc = pl.BlockSpec((1, 1, D_MODEL), lambda i: (row(i), 0, 0))
    outs = pl.pallas_call(
        body, name=name, grid=(T // TOKEN_BLOCK,),
        in_specs=[tok(dz.shape[1]) for dz in dzs] + [_full(w.shape) for w in ws] + [tok(D_MODEL), _full((1, D_MODEL)), _mod_spec(nb, 1), tok(D_MODEL)]
        + (hosted.in_specs if hosted else []),
        out_specs=[tok(D_MODEL), acc, acc, _full((1, D_MODEL))] + (hosted.out_specs if hosted else []),
        out_shape=[jax.ShapeDtypeStruct((T, D_MODEL), F32), jax.ShapeDtypeStruct((nrow, 1, D_MODEL), F32),
                   jax.ShapeDtypeStruct((nrow, 1, D_MODEL), F32), jax.ShapeDtypeStruct((1, D_MODEL), F32)] + (hosted.out_shape if hosted else []),
        scratch_shapes=hosted.scratch if hosted else [],
        compiler_params=_params(("arbitrary",), VMEM_LIMIT),
    )(*dzs, *ws, x2, g, ms, dxres, *(hosted.xs if hosted else []))
    return outs[:4], list(outs[4:])


def _rot(x):
    lane = lax.broadcasted_iota(jnp.int32, x.shape, 1)
    return jnp.where((lane % 16) < 8, -pltpu.roll(x, LANES - 8, 1), pltpu.roll(x, 8, 1))


def _rope(x, cos, sin):
    return x * cos + _rot(x) * sin


def _rope_t(dy, cos, sin):
    return dy * cos - _rot(dy * sin)


def _rms_rows(x):
    r = lax.rsqrt(jnp.mean(x * x, axis=-1, keepdims=True) + NORM_EPS)
    return x * r, r


def _rms_rows_bwd(dyg, xn, r):
    return r * (dyg - xn * jnp.mean(dyg * xn, axis=-1, keepdims=True))


def mla_prep_fwd(za, qg, kvg, wqn, wqr, wkv, cos, sin, nb, name):
    T = za.shape[0]
    W = MLA_HEADS * LANES

    def body(z_ref, qg_ref, kvg_ref, wqn_ref, wqr_ref, wkv_ref, cos_ref, sin_ref, qn_ref, qr_ref, kv_ref, kr_ref):
        z = z_ref[...]
        cos = cos_ref[...]
        sin = sin_ref[...]
        xq, _ = _rms_rows(z[:, 0:256])
        qn = (xq * qg_ref[...]).astype(BF16)
        qn_ref[...] = (_dot(qn, wqn_ref[...]) * ATT_SCALE).astype(BF16)
        qr = _dot(qn, wqr_ref[...])
        for h in range(MLA_HEADS):
            sl = slice(LANES * h, LANES * (h + 1))
            qr_ref[:, sl] = (_rope(qr[:, sl], cos, sin) * ATT_SCALE).astype(BF16)
        xkv, _ = _rms_rows(z[:, 256:384])
        kv_ref[...] = _dot((xkv * kvg_ref[...]).astype(BF16), wkv_ref[...]).astype(BF16)
        kr_ref[...] = _rope(z[:, 384:512], cos, sin).astype(BF16)

    tok = lambda n: pl.BlockSpec((TOKEN_BLOCK, n), lambda i: (i, 0))
    pos = pl.BlockSpec((TOKEN_BLOCK, LANES), lambda i: (i % nb, 0))
    return pl.pallas_call(
        body, name=name, grid=(T // TOKEN_BLOCK,),
        in_specs=[tok(512), _full(qg.shape), _full(kvg.shape), _full(wqn.shape), _full(wqr.shape), _full(wkv.shape), pos, pos],
        out_specs=[tok(W), tok(W), tok(W), tok(LANES)],
        out_shape=[jax.ShapeDtypeStruct((T, W), BF16)] * 3 + [jax.ShapeDtypeStruct((T, LANES), BF16)],
        compiler_params=_params(("parallel",)),
    )(za, qg, kvg, wqn, wqr, wkv, cos, sin)


def mla_prep_bwd(dqn, dqr, dkv, dkr, za, qg, kvg, wqn, wqr, wkv, cos, sin, nb, name):
    T = za.shape[0]
    W = MLA_HEADS * LANES

    def body(dqn_ref, dqr_ref, dkv_ref, dkr_ref, z_ref, qg_ref, kvg_ref, wqn_ref, wqr_ref, wkv_ref, cos_ref, sin_ref,
             dz_ref, dwqn_ref, dwqr_ref, dwkv_ref, dqg_ref, dkvg_ref):
        first = pl.program_id(0) == 0
        z = z_ref[...]
        cos = cos_ref[...]
        sin = sin_ref[...]
        qg = qg_ref[...]
        kvg = kvg_ref[...]
        xq, rq = _rms_rows(z[:, 0:256])
        qn = (xq * qg).astype(BF16)
        a1 = (dqn_ref[...].astype(F32) * ATT_SCALE).astype(BF16)
        parts = []
        for h in range(MLA_HEADS):
            sl = slice(LANES * h, LANES * (h + 1))
            parts.append(_rope_t(dqr_ref[:, sl].astype(F32) * ATT_SCALE, cos, sin).astype(BF16))
        a2 = jnp.concatenate(parts, axis=1)
        dq = _dot_nt(a1, wqn_ref[...]) + _dot_nt(a2, wqr_ref[...])
        _acc(dwqn_ref, _dot_tn(qn, a1), first)
        _acc(dwqr_ref, _dot_tn(qn, a2), first)
        _acc(dqg_ref, jnp.sum(dq * xq, axis=0, keepdims=True), first)
        dz_ref[:, 0:256] = _rms_rows_bwd(dq * qg, xq, rq).astype(BF16)
        xkv, rkv = _rms_rows(z[:, 256:384])
        kvn = (xkv * kvg).astype(BF16)
        dkvb = dkv_ref[...].astype(BF16)
        dk = _dot_nt(dkvb, wkv_ref[...])
        _acc(dwkv_ref, _dot_tn(kvn, dkvb), first)
        _acc(dkvg_ref, jnp.sum(dk * xkv, axis=0, keepdims=True), first)
        dz_ref[:, 256:384] = _rms_rows_bwd(dk * kvg, xkv, rkv).astype(BF16)
        dz_ref[:, 384:512] = _rope_t(dkr_ref[...], cos, sin).astype(BF16)

    tok = lambda n: pl.BlockSpec((TOKEN_BLOCK, n), lambda i: (i, 0))
    pos = pl.BlockSpec((TOKEN_BLOCK, LANES), lambda i: (i % nb, 0))
    return pl.pallas_call(
        body, name=name, grid=(T // TOKEN_BLOCK,),
        in_specs=[tok(W), tok(W), tok(W), tok(LANES), tok(512), _full(qg.shape), _full(kvg.shape), _full(wqn.shape),
                  _full(wqr.shape), _full(wkv.shape), pos, pos],
        out_specs=[tok(512), _full(wqn.shape), _full(wqr.shape), _full(wkv.shape), _full(qg.shape), _full(kvg.shape)],
        out_shape=[jax.ShapeDtypeStruct((T, 512), BF16), jax.ShapeDtypeStruct(wqn.shape, F32), jax.ShapeDtypeStruct(wqr.shape, F32),
                   jax.ShapeDtypeStruct(wkv.shape, F32), jax.ShapeDtypeStruct(qg.shape, F32), jax.ShapeDtypeStruct(kvg.shape, F32)],
        compiler_params=_params(("arbitrary",)),
    )(dqn, dqr, dkv, dkr, za, qg, kvg, wqn, wqr, wkv, cos, sin)


def _att_qk(qn_ref, qr_ref, kv_ref, kr, j):
    sl = slice(LANES * j, LANES * (j + 1))
    q = jnp.concatenate([qn_ref[0, :, sl], qr_ref[0, :, sl]], axis=1)
    kvj = kv_ref[0, :, sl]
    k = jnp.concatenate([kvj, kr], axis=1)
    return q, k, kvj, _dot_nt(q, k)


def _att_specs(L, lk, q0, pairs=1):
    TQ, W2 = TOKEN_BLOCK, 2 * LANES * pairs
    qspec = pl.BlockSpec((1, TQ, W2), lambda b, h, i: (b, i + q0, h))
    kvspec = pl.BlockSpec((1, lk, W2), lambda b, h, i: (b, 0, h))
    krspec = pl.BlockSpec((1, lk, LANES), lambda b, h, i: (b, 0, 0))
    gspec = pl.BlockSpec((1, TQ, LANES * pairs), lambda b, h, i: (b, i + q0, h))
    lspec = pl.BlockSpec((1, pairs, TQ, LANES), lambda b, h, i: (b, h, i + q0, 0))
    return qspec, kvspec, krspec, gspec, lspec


_ANY = pl.BlockSpec(memory_space=pl.ANY)


def attention_fwd(qn, qr, kv, kr, zg, n_ctx, name, hosted=None):
    B, L, _ = qn.shape
    TQ = TOKEN_BLOCK
    PAIRS = 2
    HP = MLA_HEADS // 2 // PAIRS
    shapes = [jax.ShapeDtypeStruct((B, L, MLA_WIDTH), F32), jax.ShapeDtypeStruct((B, L, MLA_WIDTH), BF16),
              jax.ShapeDtypeStruct((B, MLA_HEADS // 2, L, LANES), F32)]

    nx = hosted.n if hosted else 0
    NQ = L // TQ - 1

    def body(*refs):
        if hosted:
            step = (pl.program_id(0) * HP + pl.program_id(1)) * NQ + pl.program_id(2)
            hosted.run(refs[5:5 + nx], refs[8 + nx:8 + 2 * nx], refs[8 + 2 * nx:], step, B * HP * NQ)
        _fwd_step(*refs[:5], *refs[5 + nx:8 + nx])

    def body_ctx(qn_ref, qr_ref, kv_ref, kr_ref, g_ref, *rest):
        _fwd_step(qn_ref, qr_ref, kv_ref, kr_ref, g_ref, *rest[-3:])

    def _fwd_step(qn_ref, qr_ref, kv_ref, kr_ref, g_ref, ya_ref, ym_ref, lse_ref):
        kr_v = kr_ref[0]
        for pr in range(PAIRS):
            outs, lses = [], []
            for j in (2 * pr, 2 * pr + 1):
                _, _, kvj, s = _att_qk(qn_ref, qr_ref, kv_ref, kr_v, j)
                m = jnp.max(s, axis=-1, keepdims=True)
                p = jnp.exp(s - m).astype(BF16)
                lane_k = lax.broadcasted_iota(jnp.int32, kvj.shape, 1)
                o = _dot(p, jnp.where(lane_k < MLA_V, jnp.ones_like(kvj), kvj))
                l = o[:, 0:1]
                outs.append(o / l)
                lses.append(m + jnp.log(l))
            lane = lax.broadcasted_iota(jnp.int32, outs[0].shape, 1)
            y = jnp.where(lane < MLA_V, pltpu.roll(outs[0], MLA_V, 1), outs[1])
            sl = slice(LANES * pr, LANES * (pr + 1))
            ya_ref[0, :, sl] = y
            g = g_ref[0, :, sl]
            ym_ref[0, :, sl] = (y * g * _sig(g)).astype(BF16)
            lse_ref[0, pr] = jnp.where(lane < MLA_V, lses[0], lses[1])

    qspec, kvspec, krspec, gspec, lspec = _att_specs(L, L, 1, PAIRS)
    main = pl.pallas_call(
        body, name=name, grid=(B, HP, NQ),
        in_specs=[qspec, qspec, kvspec, krspec, gspec] + (hosted.in_specs if hosted else []),
        out_specs=[gspec, gspec, lspec] + (hosted.out_specs if hosted else []),
        out_shape=shapes + (hosted.out_shape if hosted else []), scratch_shapes=hosted.scratch if hosted else [],
        compiler_params=_params(("arbitrary",) * 3 if hosted else ("parallel",) * 3, VMEM_LIMIT),
    )(qn, qr, kv, kr, zg, *(hosted.xs if hosted else []))
    qspec, kvspec, krspec, gspec, lspec = _att_specs(L, n_ctx, 0, PAIRS)
    outs = pl.pallas_call(
        body_ctx, name=name + "_ctx", grid=(B, HP, 1),
        in_specs=[qspec, qspec, kvspec, krspec, gspec, _ANY, _ANY, _ANY], out_specs=[gspec, gspec, lspec], out_shape=shapes,
        input_output_aliases={5: 0, 6: 1, 7: 2},
        compiler_params=_params(("parallel", "parallel", "parallel"), VMEM_LIMIT),
    )(qn, qr, kv, kr, zg, *main[:3])
    return (*outs, list(main[3:]))


def attention_bwd(qn, qr, kv, kr, zg, ya, lse, dym, n_ctx, name, hosted=None):
    B, L, _ = qn.shape
    TQ = TOKEN_BLOCK
    PAIRS = 2
    HP = MLA_HEADS // 2 // PAIRS
    W = MLA_HEADS * LANES
    shapes = [jax.ShapeDtypeStruct((B, L, W), BF16), jax.ShapeDtypeStruct((B, L, W), BF16), jax.ShapeDtypeStruct((B, L, W), F32),
              jax.ShapeDtypeStruct((B, L, LANES), F32), jax.ShapeDtypeStruct((B, L, MLA_WIDTH), BF16)]

    nx = hosted.n if hosted else 0
    NQ = L // TQ - 1

    def body(*refs):
        if hosted:
            step = (pl.program_id(0) * HP + pl.program_id(1)) * NQ + pl.program_id(2)
            hosted.run(refs[8:8 + nx], refs[13 + nx:13 + 2 * nx], refs[13 + 2 * nx:], step, B * HP * NQ)
        dkv_ref, dkr_ref = refs[10 + nx], refs[11 + nx]

        @pl.when(pl.program_id(2) == 0)
        def _():
            dkv_ref[...] = jnp.zeros_like(dkv_ref)

        @pl.when(jnp.logical_and(pl.program_id(2) == 0, pl.program_id(1) == 0))
        def _():
            dkr_ref[...] = jnp.zeros_like(dkr_ref)

        _bwd_step(*refs[:8], *refs[8 + nx:13 + nx])

    def body_ctx(qn_ref, qr_ref, kv_ref, kr_ref, g_ref, ya_ref, lse_ref, dy_ref, dkv_in, dkr_in, a0, a1, a2,
                 dqn_ref, dqr_ref, dkv_ref, dkr_ref, dzg_ref):
        dkv_ref[...] = dkv_in[...]

        @pl.when(pl.program_id(1) == 0)
        def _():
            dkr_ref[...] = dkr_in[...]

        _bwd_step(qn_ref, qr_ref, kv_ref, kr_ref, g_ref, ya_ref, lse_ref, dy_ref, dqn_ref, dqr_ref, dkv_ref, dkr_ref, dzg_ref)

    def _bwd_step(qn_ref, qr_ref, kv_ref, kr_ref, g_ref, ya_ref, lse_ref, dy_ref, dqn_ref, dqr_ref, dkv_ref, dkr_ref, dzg_ref):
        kr_v = kr_ref[0]
        for pr in range(PAIRS):
            psl = slice(LANES * pr, LANES * (pr + 1))
            silu, dsilu = _silu_and_grad(g_ref[0, :, psl])
            dy = dy_ref[0, :, psl]
            ya_v = ya_ref[0, :, psl]
            dya = dy * silu
            dzg_ref[0, :, psl] = (dy * ya_v * dsilu).astype(BF16)
            lane = lax.broadcasted_iota(jnp.int32, dya.shape, 1)
            hi = lane >= MLA_V
            d_out = [jnp.where(hi, pltpu.roll(dya, MLA_V, 1), 0.0), jnp.where(hi, dya, 0.0)]
            prod = dya * ya_v
            drow = [jnp.sum(jnp.where(hi, 0.0, prod), axis=-1, keepdims=True), jnp.sum(jnp.where(hi, prod, 0.0), axis=-1, keepdims=True)]
            lse_v = lse_ref[0, pr]
            for jj in range(2):
                j = 2 * pr + jj
                sl = slice(LANES * j, LANES * (j + 1))
                q, k, kvj, s = _att_qk(qn_ref, qr_ref, kv_ref, kr_v, j)
                pn = jnp.exp(s - lse_v[:, MLA_V * jj:MLA_V * jj + 1])
                dob = d_out[jj].astype(BF16)
                ds = (pn * (_dot_nt(dob, kvj) - drow[jj])).astype(BF16)
                dq = _dot(ds, k)
                dqn_ref[0, :, sl] = jnp.where(hi, 0.0, dq[:, :LANES]).astype(BF16)
                dqr_ref[0, :, sl] = dq[:, LANES:].astype(BF16)
                dk = _dot_tn(ds, q)
                dkv_ref[0, :, sl] += dk[:, :LANES] + _dot_tn(pn.astype(BF16), dob)
                dkr_ref[0] += dk[:, LANES:]

    sem = _params(("parallel", "arbitrary", "arbitrary"), VMEM_LIMIT)
    qspec, kvspec, krspec, gspec, lspec = _att_specs(L, L, 1, PAIRS)
    main = pl.pallas_call(
        body, name=name, grid=(B, HP, NQ),
        in_specs=[qspec, qspec, kvspec, krspec, gspec, gspec, lspec, gspec] + (hosted.in_specs if hosted else []),
        out_specs=[qspec, qspec, kvspec, krspec, gspec] + (hosted.out_specs if hosted else []),
        out_shape=shapes + (hosted.out_shape if hosted else []), scratch_shapes=hosted.scratch if hosted else [],
        compiler_params=_params(("arbitrary",) * 3, VMEM_LIMIT) if hosted else sem,
    )(qn, qr, kv, kr, zg, ya, lse, dym, *(hosted.xs if hosted else []))
    qspec, kvspec, krspec, gspec, lspec = _att_specs(L, n_ctx, 0, PAIRS)
    outs = pl.pallas_call(
        body_ctx, name=name + "_ctx", grid=(B, HP, 1),
        in_specs=[qspec, qspec, kvspec, krspec, gspec, gspec, lspec, gspec, kvspec, krspec, _ANY, _ANY, _ANY],
        out_specs=[qspec, qspec, kvspec, krspec, gspec], out_shape=shapes,
        input_output_aliases={8: 2, 9: 3, 10: 0, 11: 1, 12: 4}, compiler_params=sem,
    )(qn, qr, kv, kr, zg, ya, lse, dym, main[2], main[3], main[0], main[1], main[4])
    return (*outs, list(main[5:]))


def _seg_bounds(rows, n_ctx, L):
    in_ctx = rows < n_ctx
    return jnp.where(in_ctx, 0, n_ctx), jnp.where(in_ctx, n_ctx, L)


def _window_sum(u, w, rows, lo, hi, mirror):
    L = u.shape[0]
    offs = range(-w // 2 + 1, w // 2 + 1) if mirror else range(-w // 2, w // 2)
    acc = None
    for d in offs:
        if d == 0:
            t = u
        else:
            src = rows + d
            t = jnp.where(jnp.logical_and(src >= lo, src < hi), pltpu.roll(u, (-d) % L, 0), 0.0)
        acc = t if acc is None else acc + t
    return acc


def _window_count(w, rows, lo, hi):
    pos = rows - lo
    return (jnp.minimum(pos + w // 2, hi - lo) - jnp.maximum(pos - w // 2, 0)).astype(F32)


def pool_fwd(px, pg, pw, ps, n_ctx, name):
    B, L, _ = px.shape

    def body(px_ref, pg_ref, pw_ref, ps_ref, y_ref):
        rows = lax.broadcasted_iota(jnp.int32, (L, POOL_GROUP), 0)
        lo, hi = _seg_bounds(rows, n_ctx, L)
        for gi, w in enumerate(POOL_WINDOWS):
            sl = slice(POOL_GROUP * gi, POOL_GROUP * (gi + 1))
            u = px_ref[0, :, sl]
            pooled = _window_sum(u, w, rows, lo, hi, False) / _window_count(w, rows, lo, hi) - u
            mixed = _dot(pooled.astype(BF16), pw_ref[gi])
            g = pg_ref[0, :, sl]
            y_ref[0, :, sl] = (mixed * ps_ref[:, sl] * (g * _sig(g))).astype(BF16)

    tok = pl.BlockSpec((1, L, POOL_WIDTH), lambda b: (b, 0, 0))
    return pl.pallas_call(
        body, name=name, grid=(B,),
        in_specs=[tok, tok, _full(pw.shape), _full(ps.shape)],
        out_specs=tok, out_shape=jax.ShapeDtypeStruct((B, L, POOL_WIDTH), BF16),
        compiler_params=_params(("parallel",), VMEM_LIMIT),
    )(px, pg, pw, ps)


def pool_bwd(px, pg, pw, ps, dy, n_ctx, name):
    B, L, _ = px.shape

    def body(px_ref, pg_ref, pw_ref, ps_ref, dy_ref, dpx_ref, dpg_ref, dpw_ref, dps_ref):
        first = pl.program_id(0) == 0
        rows = lax.broadcasted_iota(jnp.int32, (L, POOL_GROUP), 0)
        lo, hi = _seg_bounds(rows, n_ctx, L)
        for gi, w in enumerate(POOL_WINDOWS):
            sl = slice(POOL_GROUP * gi, POOL_GROUP * (gi + 1))
            u = px_ref[0, :, sl]
            cnt = _window_count(w, rows, lo, hi)
            pooled = (_window_sum(u, w, rows, lo, hi, False) / cnt - u).astype(BF16)
            mixed = _dot(pooled, pw_ref[gi])
            silu, dsilu = _silu_and_grad(pg_ref[0, :, sl])
            sc = ps_ref[:, sl]
            dyv = dy_ref[0, :, sl]
            _acc(dps_ref.at[:, sl], jnp.sum(dyv * mixed * silu, axis=0, keepdims=True), first)
            dpg_ref[0, :, sl] = (dyv * mixed * sc * dsilu).astype(BF16)
            dmixed = (dyv * sc * silu).astype(BF16)
            _acc(dpw_ref.at[gi], _dot_tn(pooled, dmixed), first)
            dpooled = _dot_nt(dmixed, pw_ref[gi])
            dpx_ref[0, :, sl] = (_window_sum(dpooled / cnt, w, rows, lo, hi, True) - dpooled).astype(BF16)

    tok = pl.BlockSpec((1, L, POOL_WIDTH), lambda b: (b, 0, 0))
    return pl.pallas_call(
        body, name=name, grid=(B,),
        in_specs=[tok, tok, _full(pw.shape), _full(ps.shape), tok],
        out_specs=[tok, tok, _full(pw.shape), _full(ps.shape)],
        out_shape=[jax.ShapeDtypeStruct((B, L, POOL_WIDTH), BF16)] * 2 + [jax.ShapeDtypeStruct(pw.shape, F32), jax.ShapeDtypeStruct(ps.shape, F32)],
        compiler_params=_params(("arbitrary",), VMEM_LIMIT),
    )(px, pg, pw, ps, dy)


_SCAN_STEPS = (1, 2, 4, 8, 16, 32)
SCAN_CHUNKS = 4


def _cum_fwd(x, r):
    for s in _SCAN_STEPS:
        x = x + jnp.where(r >= s, pltpu.roll(x, s, 0), 0.0)
    return x


def _cum_bwd(x, r):
    n = x.shape[0]
    for s in _SCAN_STEPS:
        x = x + jnp.where(r + s < GLA_CHUNK, pltpu.roll(x, n - s, 0), 0.0)
    return x


def _log_sigmoid(x):
    return jnp.minimum(x, 0.0) - jnp.log(1.0 + jnp.exp(-jnp.abs(x)))


def _gla_decays(lr, w_ref, b_ref, r, reverse):
    pre = _dot(lr, w_ref[...]) + b_ref[...]
    a = _log_sigmoid(pre) / GLA_TAU
    return pre, a, (_cum_bwd(a, r) if reverse else _cum_fwd(a, r)), _chunk_total(a)


def _chunk_total(x):
    x3 = x.reshape(x.shape[0] // GLA_CHUNK, GLA_CHUNK, x.shape[1])
    return jnp.broadcast_to(jnp.sum(x3, axis=1, keepdims=True), x3.shape).reshape(x.shape)


def gla_prep_fwd(zlr, zq, zk, waf, wab, baf, bab, name):
    T = zlr.shape[0]

    def body(lr_ref, q_ref, k_ref, waf_ref, wab_ref, baf_ref, bab_ref, qf_ref, kf_ref, ksf_ref, tf_ref, qb_ref, kb_ref, ksb_ref, tb_ref):
        r = lax.broadcasted_iota(jnp.int32, (TOKEN_BLOCK, GLA_KW), 0) % GLA_CHUNK
        lr = lr_ref[...].astype(BF16)
        q = q_ref[...] * GLA_DK ** -0.5
        k = k_ref[...]
        for rev, w_ref, b_ref, qo, ko, kso, to in ((False, waf_ref, baf_ref, qf_ref, kf_ref, ksf_ref, tf_ref),
                                                   (True, wab_ref, bab_ref, qb_ref, kb_ref, ksb_ref, tb_ref)):
            _, _, b, tot = _gla_decays(lr, w_ref, b_ref, r, rev)
            qo[...] = (q * jnp.exp(b)).astype(BF16)
            ko[...] = (k * jnp.exp(-b)).astype(BF16)
            kso[...] = (k * jnp.exp(tot - b)).astype(BF16)
            to[...] = tot

    tok = lambda n: pl.BlockSpec((TOKEN_BLOCK, n), lambda i: (i, 0))
    outs = [jax.ShapeDtypeStruct((T, GLA_KW), BF16)] * 3 + [jax.ShapeDtypeStruct((T, GLA_KW), F32)]
    return pl.pallas_call(
        body, name=name, grid=(T // TOKEN_BLOCK,),
        in_specs=[tok(LANES), tok(GLA_KW), tok(GLA_KW), _full(waf.shape), _full(wab.shape), _full(baf.shape), _full(bab.shape)],
        out_specs=[tok(GLA_KW)] * 8, out_shape=outs + outs,
        compiler_params=_params(("parallel",)),
    )(zlr, zq, zk, waf, wab, baf, bab)


def gla_prep_bwd(zlr, zq, zk, waf, wab, baf, bab, gf, gb, dvs, name):
    T = zlr.shape[0]

    def body(lr_ref, q_ref, k_ref, waf_ref, wab_ref, baf_ref, bab_ref, dqf, dkf, dksf, ddf, dqb, dkb, dksb, ddb, dvf, dvb,
             dlr_ref, dq_ref, dk_ref, dwaf_ref, dwab_ref, dbaf_ref, dbab_ref, dv_ref):
        first = pl.program_id(0) == 0
        dv_ref[...] = (dvf[...] + dvb[...]).astype(BF16)
        r = lax.broadcasted_iota(jnp.int32, (TOKEN_BLOCK, GLA_KW), 0) % GLA_CHUNK
        lr = lr_ref[...].astype(BF16)
        q = q_ref[...] * GLA_DK ** -0.5
        k = k_ref[...]
        dq_tot = None
        dk_tot = None
        dlr = None
        for rev, w_ref, b_ref, dqt, dkt, dks, ddec, dw_ref, db_ref in (
                (False, waf_ref, baf_ref, dqf, dkf, dksf, ddf, dwaf_ref, dbaf_ref),
                (True, wab_ref, bab_ref, dqb, dkb, dksb, ddb, dwab_ref, dbab_ref)):
            pre, _, b, tot = _gla_decays(lr, w_ref, b_ref, r, rev)
            e1 = jnp.exp(b)
            e2 = jnp.exp(-b)
            e3 = jnp.exp(tot - b)
            dqt_v = dqt[...]
            dkt_v = dkt[...]
            dks_v = dks[...]
            dq = dqt_v * e1
            dk = dkt_v * e2 + dks_v * e3
            g3 = dks_v * (k * e3)
            d_b = dqt_v * (q * e1) - dkt_v * (k * e2) - g3
            d_tot = _chunk_total(g3) + ddec[...] * jnp.exp(tot)
            da = (_cum_fwd(d_b, r) if rev else _cum_bwd(d_b, r)) + d_tot
            dpre = (da * (_sig(-pre) / GLA_TAU)).astype(BF16)
            t = _dot_nt(dpre, w_ref[...])
            dlr = t if dlr is None else dlr + t
            _acc(dw_ref, _dot_tn(lr, dpre), first)
            _acc(db_ref, jnp.sum(dpre.astype(F32), axis=0, keepdims=True), first)
            dq_tot = dq if dq_tot is None else dq_tot + dq
            dk_tot = dk if dk_tot is None else dk_tot + dk
        dlr_ref[...] = dlr.astype(BF16)
        dq_ref[...] = (dq_tot * GLA_DK ** -0.5).astype(BF16)
        dk_ref[...] = dk_tot.astype(BF16)

    tok = lambda n: pl.BlockSpec((TOKEN_BLOCK, n), lambda i: (i, 0))
    return pl.pallas_call(
        body, name=name, grid=(T // TOKEN_BLOCK,),
        in_specs=[tok(LANES), tok(GLA_KW), tok(GLA_KW), _full(waf.shape), _full(wab.shape), _full(baf.shape), _full(bab.shape)] + [tok(GLA_KW)] * 8
        + [tok(GLA_WIDTH)] * 2,
        out_specs=[tok(LANES), tok(GLA_KW), tok(GLA_KW), _full(waf.shape), _full(wab.shape), _full(baf.shape), _full(bab.shape), tok(GLA_WIDTH)],
        out_shape=[jax.ShapeDtypeStruct((T, LANES), BF16), jax.ShapeDtypeStruct((T, GLA_KW), BF16), jax.ShapeDtypeStruct((T, GLA_KW), BF16),
                   jax.ShapeDtypeStruct(waf.shape, F32), jax.ShapeDtypeStruct(wab.shape, F32), jax.ShapeDtypeStruct(baf.shape, F32),
                   jax.ShapeDtypeStruct(bab.shape, F32), jax.ShapeDtypeStruct((T, GLA_WIDTH), BF16)],
        compiler_params=_params(("arbitrary",)),
    )(zlr, zq, zk, waf, wab, baf, bab, *gf, *gb, *dvs)


def _chunk_order(nc, n_ctx_chunks, reverse):
    if not reverse:
        return lambda c: c
    return lambda c: jnp.where(c < n_ctx_chunks, n_ctx_chunks - 1 - c, nc + n_ctx_chunks - 1 - c)


def _tri_mask4(reverse):
    ri = lax.broadcasted_iota(jnp.int32, (GLA_CHUNK, GLA_HEADS * GLA_CHUNK), 0)
    ci = lax.broadcasted_iota(jnp.int32, (GLA_CHUNK, GLA_HEADS * GLA_CHUNK), 1) % GLA_CHUNK
    return (ri <= ci) if reverse else (ri >= ci)


def _block_diag(x, rb, cb):
    x4 = jnp.concatenate([x] * GLA_HEADS, axis=0)
    r = lax.broadcasted_iota(jnp.int32, x4.shape, 0) // rb
    c = lax.broadcasted_iota(jnp.int32, x4.shape, 1) // cb
    return jnp.where(r == c, x4, jnp.zeros_like(x4))


def _diag_blocks(f, rb, cb):
    c = lax.broadcasted_iota(jnp.int32, (rb, GLA_HEADS * cb), 1) // cb
    out = None
    for h in range(GLA_HEADS):
        t = jnp.where(c == h, f[rb * h:rb * (h + 1)], 0.0)
        out = t if out is None else out + t
    return out


def gla_scan_fwd(dirs, v, n_ctx, name):
    B, L, _ = v.shape
    C, G = GLA_CHUNK, SCAN_CHUNKS
    nc = L // C
    orders = [_chunk_order(nc // G, n_ctx // C // G, rev) for rev in (False, True)]

    def body(qf, kf, ksf, tf, vf, qb, kb, ksb, tb, vb, of, ssf, ob, ssb, stf, stb):
        @pl.when(pl.program_id(1) == 0)
        def _():
            stf[...] = jnp.zeros_like(stf)
            stb[...] = jnp.zeros_like(stb)

        for sub in range(G):
            step(qf, kf, ksf, vf, tf, of, ssf, stf, False, sub)
            step(qb, kb, ksb, vb, tb, ob, ssb, stb, True, G - 1 - sub)

    def step(q_ref, k_ref, ks_ref, v_ref, tot_ref, o_ref, ss_ref, st, reverse, sub):
        rows = slice(C * sub, C * (sub + 1))
        S = st[...]
        ss_ref[0, sub] = S
        q = q_ref[0, rows]
        v = v_ref[0, rows]
        k4 = _block_diag(k_ref[0, rows], GLA_CHUNK, GLA_DK)
        v4 = _block_diag(v.astype(BF16), GLA_CHUNK, GLA_DV)
        s4 = _block_diag(S.astype(BF16), GLA_DV, GLA_DK)
        P = jnp.where(_tri_mask4(reverse), _dot_nt(q, k4), 0.0)
        o_ref[0, rows] = _dot(P.astype(BF16), v4) + _dot_nt(q, s4)
        st[...] = jnp.exp(tot_ref[0, C * sub:C * sub + 1, :]) * S + _diag_blocks(_dot(v.T.astype(BF16), ks_ref[0, rows]), GLA_DV, GLA_DK)

    in_specs, out_specs, out_shape = [], [], []
    for order in orders:
        tok = lambda n, order=order: pl.BlockSpec((1, G * C, n), lambda b, c: (b, order(c), 0))
        in_specs += [tok(GLA_KW), tok(GLA_KW), tok(GLA_KW), tok(GLA_KW), tok(GLA_WIDTH)]
        out_specs += [tok(GLA_WIDTH), pl.BlockSpec((1, G, GLA_DV, GLA_KW), lambda b, c, order=order: (b, order(c), 0, 0))]
        out_shape += [jax.ShapeDtypeStruct((B, L, GLA_WIDTH), F32), jax.ShapeDtypeStruct((B, nc, GLA_DV, GLA_KW), F32)]
    outs = pl.pallas_call(
        body, name=name, grid=(B, nc // G), in_specs=in_specs, out_specs=out_specs, out_shape=out_shape,
        scratch_shapes=[pltpu.VMEM((GLA_DV, GLA_KW), F32)] * 2,
        compiler_params=_params(("parallel", "arbitrary")),
    )(*dirs[0], v, *dirs[1], v)
    return outs[:2], outs[2:]


def gla_scan_bwd(dirs, v, do, n_ctx, name, hosted=None):
    B, L, _ = v.shape
    C, G = GLA_CHUNK, SCAN_CHUNKS
    nc = L // C
    npair = nc // G
    orders = []
    for rev in (False, True):
        fwd_order = _chunk_order(npair, n_ctx // C // G, rev)
        orders.append(lambda c, fwd_order=fwd_order: fwd_order(npair - 1 - c))

    nx = hosted.n if hosted else 0

    def body(*refs):
        qf, kf, ksf, tf, ssf, vf, dof, qb, kb, ksb, tb, ssb, vb, dob = refs[:14]
        dqf, dkf, dksf, dvf, ddf, dqb, dkb, dksb, dvb, ddb = refs[14 + nx:24 + nx]
        dstf, dstb = refs[24 + 2 * nx:26 + 2 * nx]
        if hosted:
            hosted.run(refs[14:14 + nx], refs[24 + nx:24 + 2 * nx], refs[26 + 2 * nx:], pl.program_id(0) * npair + pl.program_id(1), B * npair)

        @pl.when(pl.program_id(1) == 0)
        def _():
            dstf[...] = jnp.zeros_like(dstf)
            dstb[...] = jnp.zeros_like(dstb)

        for sub in range(G):
            step(qf, kf, ksf, vf, tf, ssf, dof, dqf, dkf, dksf, dvf, ddf, dstf, False, G - 1 - sub)
            step(qb, kb, ksb, vb, tb, ssb, dob, dqb, dkb, dksb, dvb, ddb, dstb, True, sub)

    def step(q_ref, k_ref, ks_ref, v_ref, tot_ref, ss_ref, do_ref, dq_ref, dk_ref, dks_ref, dv_ref, dd_ref, dst, reverse, sub):
        rows = slice(C * sub, C * (sub + 1))
        dSn = dst[...]
        S = ss_ref[0, sub]
        q = q_ref[0, rows]
        vb = v_ref[0, rows].astype(BF16)
        dob = do_ref[0, rows].astype(BF16)
        k4 = _block_diag(k_ref[0, rows], GLA_CHUNK, GLA_DK)
        v4 = _block_diag(vb, GLA_CHUNK, GLA_DV)
        s4 = _block_diag(S.astype(BF16), GLA_DV, GLA_DK)
        ds4 = _block_diag(dSn.astype(BF16), GLA_DV, GLA_DK)
        tri = _tri_mask4(reverse)
        P = jnp.where(tri, _dot_nt(q, k4), 0.0).astype(BF16)
        dP = jnp.where(tri, _dot_nt(dob, v4), 0.0).astype(BF16)
        dq_ref[0, rows] = _dot(dob, s4) + _dot(dP, k4)
        dk_ref[0, rows] = _diag_blocks(_dot_tn(dP, q), GLA_CHUNK, GLA_DK)
        dv_ref[0, rows] = _diag_blocks(_dot_tn(P, dob), GLA_CHUNK, GLA_DV) + _dot_nt(ks_ref[0, rows], ds4)
        dks_ref[0, rows] = _dot(vb, ds4)
        dd_ref[0, rows] = jnp.broadcast_to(jnp.sum(dSn * S, axis=0, keepdims=True), (C, GLA_KW))
        dst[...] = jnp.exp(tot_ref[0, C * sub:C * sub + 1, :]) * dSn + _diag_blocks(_dot_tn(dob, q), GLA_DV, GLA_DK)

    in_specs, out_specs, out_shape = [], [], []
    for order in orders:
        tok = lambda n, order=order: pl.BlockSpec((1, G * C, n), lambda b, c: (b, order(c), 0))
        in_specs += [tok(GLA_KW), tok(GLA_KW), tok(GLA_KW), tok(GLA_KW),
                     pl.BlockSpec((1, G, GLA_DV, GLA_KW), lambda b, c, order=order: (b, order(c), 0, 0)), tok(GLA_WIDTH), tok(GLA_WIDTH)]
        out_specs += [tok(GLA_KW), tok(GLA_KW), tok(GLA_KW), tok(GLA_WIDTH), tok(GLA_KW)]
        out_shape += [jax.ShapeDtypeStruct((B, L, GLA_KW), F32)] * 3 + [jax.ShapeDtypeStruct((B, L, GLA_WIDTH), F32), jax.ShapeDtypeStruct((B, L, GLA_KW), F32)]
    outs = pl.pallas_call(
        body, name=name, grid=(B, npair), in_specs=in_specs + (hosted.in_specs if hosted else []),
        out_specs=out_specs + (hosted.out_specs if hosted else []), out_shape=out_shape + (hosted.out_shape if hosted else []),
        scratch_shapes=[pltpu.VMEM((GLA_DV, GLA_KW), F32)] * 2 + (hosted.scratch if hosted else []),
        compiler_params=_params(("arbitrary", "arbitrary") if hosted else ("parallel", "arbitrary")),
    )(*dirs[0], v, do, *dirs[1], v, do, *(hosted.xs if hosted else []))
    return outs[:5], outs[5:10], list(outs[10:])


def gla_out_fwd(of, ob, gn, zg, name):
    T = of.shape[0]

    def body(of_ref, ob_ref, gn_ref, g_ref, y_ref):
        for h in range(GLA_HEADS):
            sl = slice(GLA_DV * h, GLA_DV * (h + 1))
            xn, _ = _rms_rows(of_ref[:, sl] + ob_ref[:, sl])
            g = g_ref[:, sl]
            y_ref[:, sl] = (xn * gn_ref[...] * (g * _sig(g))).astype(BF16)

    tok = pl.BlockSpec((TOKEN_BLOCK, GLA_WIDTH), lambda i: (i, 0))
    return pl.pallas_call(
        body, name=name, grid=(T // TOKEN_BLOCK,),
        in_specs=[tok, tok, _full(gn.shape), tok], out_specs=tok,
        out_shape=jax.ShapeDtypeStruct((T, GLA_WIDTH), BF16),
        compiler_params=_params(("parallel",)),
    )(of, ob, gn, zg)


def gla_out_bwd(of, ob, gn, zg, dy, name):
    T = of.shape[0]

    def body(of_ref, ob_ref, gn_ref, g_ref, dy_ref, do_ref, dzg_ref, dgn_ref):
        first = pl.program_id(0) == 0
        gn_v = gn_ref[...]
        dgn = None
        for h in range(GLA_HEADS):
            sl = slice(GLA_DV * h, GLA_DV * (h + 1))
            xn, r = _rms_rows(of_ref[:, sl] + ob_ref[:, sl])
            silu, dsilu = _silu_and_grad(g_ref[:, sl])
            dyv = dy_ref[:, sl]
            dzg_ref[:, sl] = (dyv * xn * gn_v * dsilu).astype(BF16)
            dn = dyv * silu
            t = jnp.sum(dn * xn, axis=0, keepdims=True)
            dgn = t if dgn is None else dgn + t
            do_ref[:, sl] = _rms_rows_bwd(dn * gn_v, xn, r)
        _acc(dgn_ref, dgn, first)

    tok = pl.BlockSpec((TOKEN_BLOCK, GLA_WIDTH), lambda i: (i, 0))
    return pl.pallas_call(
        body, name=name, grid=(T // TOKEN_BLOCK,),
        in_specs=[tok, tok, _full(gn.shape), tok, tok], out_specs=[tok, tok, _full(gn.shape)],
        out_shape=[jax.ShapeDtypeStruct((T, GLA_WIDTH), F32), jax.ShapeDtypeStruct((T, GLA_WIDTH), BF16), jax.ShapeDtypeStruct(gn.shape, F32)],
        compiler_params=_params(("arbitrary",)),
    )(of, ob, gn, zg, dy)


def merge_post_fwd(ys, zm, wbs, wo, x2, pg, ms, nb, name):
    T = x2.shape[0]

    def body(y0, y1, y2, zm_ref, w0, w1, w2, wo_ref, x_ref, pg_ref, gate_ref, xn_ref, out_ref, mg_ref):
        merged = None
        for i, (y_ref, w_ref) in enumerate(((y0, w0), (y1, w1), (y2, w2))):
            t = _sig(zm_ref[:, D_MODEL * i:D_MODEL * (i + 1)]) * _dot(y_ref[...], w_ref[...])
            merged = t if merged is None else merged + t
        mb = merged.astype(BF16)
        mg_ref[...] = mb
        out = _dot(mb, wo_ref[...])
        out_ref[...] = out
        on, _ = _rms_rows(out)
        xn_ref[...] = x_ref[...] + gate_ref[0] * (on * pg_ref[...])

    tok = lambda n: pl.BlockSpec((TOKEN_BLOCK, n), lambda i: (i, 0))
    return pl.pallas_call(
        body, name=name, grid=(T // TOKEN_BLOCK,),
        in_specs=[tok(512)] * 3 + [tok(3 * D_MODEL)] + [_full(w.shape) for w in wbs] + [_full(wo.shape), tok(D_MODEL), _full(pg.shape), _mod_spec(nb, 2)],
        out_specs=[tok(D_MODEL)] * 3,
        out_shape=[jax.ShapeDtypeStruct((T, D_MODEL), F32), jax.ShapeDtypeStruct((T, D_MODEL), F32), jax.ShapeDtypeStruct((T, D_MODEL), BF16)],
        compiler_params=_params(("parallel",), VMEM_LIMIT),
    )(*ys, zm, *wbs, wo, x2, pg, ms)


def merge_post_bwd(dxn, out, ys, zm, wbs, wo, pg, ms, nb, name):
    T = dxn.shape[0]
    nrow = ms.shape[0]
    row = _mod_row(nb)

    def body(dxn_ref, out_ref, y0, y1, y2, zm_ref, w0, w1, w2, wo_ref, pg_ref, gate_ref,
             dy0, dy1, dy2, dzm_ref, dout_ref, dp0, dp1, dp2, dgate_ref, dpg_ref):
        i = pl.program_id(0)
        dxn_v = dxn_ref[...]
        on, r = _rms_rows(out_ref[...])
        pg_v = pg_ref[...]
        _acc(dgate_ref.at[0], jnp.sum(dxn_v * on * pg_v, axis=0, keepdims=True), (i % nb) <= 1)
        dn = dxn_v * gate_ref[0]
        _acc(dpg_ref, jnp.sum(dn * on, axis=0, keepdims=True), i == 0)
        dout = _rms_rows_bwd(dn * pg_v, on, r).astype(BF16)
        dout_ref[...] = dout
        dmerged = _dot_nt(dout, wo_ref[...])
        for j, (y_ref, w_ref, dy_ref, dp_ref) in enumerate(((y0, w0, dy0, dp0), (y1, w1, dy1, dp1), (y2, w2, dy2, dp2))):
            sl = slice(D_MODEL * j, D_MODEL * (j + 1))
            g = _sig(zm_ref[:, sl])
            p = _dot(y_ref[...], w_ref[...])
            dzm_ref[:, sl] = (dmerged * p * g * (1.0 - g)).astype(BF16)
            dp = (dmerged * g).astype(BF16)
            dp_ref[...] = dp
            dy_ref[...] = _dot_nt(dp, w_ref[...])

    tok = lambda n: pl.BlockSpec((TOKEN_BLOCK, n), lambda i: (i, 0))
    return pl.pallas_call(
        body, name=name, grid=(T // TOKEN_BLOCK,),
        in_specs=[tok(D_MODEL), tok(D_MODEL)] + [tok(512)] * 3 + [tok(3 * D_MODEL)] + [_full(w.shape) for w in wbs] + [_full(wo.shape), _full(pg.shape), _mod_spec(nb, 2)],
        out_specs=[tok(512)] * 3 + [tok(3 * D_MODEL), tok(D_MODEL)] + [tok(D_MODEL)] * 3 + [pl.BlockSpec((1, 1, D_MODEL), lambda i: (row(i), 0, 0)), _full(pg.shape)],
        out_shape=[jax.ShapeDtypeStruct((T, 512), F32)] * 3 + [jax.ShapeDtypeStruct((T, 3 * D_MODEL), BF16), jax.ShapeDtypeStruct((T, D_MODEL), BF16)]
        + [jax.ShapeDtypeStruct((T, D_MODEL), BF16)] * 3 + [jax.ShapeDtypeStruct((nrow, 1, D_MODEL), F32), jax.ShapeDtypeStruct(pg.shape, F32)],
        compiler_params=_params(("arbitrary",), VMEM_LIMIT),
    )(dxn, out, *ys, zm, *wbs, wo, pg, ms)


def loss_head(y2, tgt2, nb, name):
    T = y2.shape[0]
    nlat = nb - 1

    def body(y_ref, t_ref, dy_ref, loss_ref, acc):
        i = pl.program_id(0)
        is_lat = (i % nb) > 0

        @pl.when(i == 0)
        def _():
            acc[...] = jnp.zeros_like(acc)

        @pl.when(is_lat)
        def _():
            e = y_ref[...] - t_ref[...]
            dy_ref[...] = e * (1.0 / D_MODEL)
            acc[...] += jnp.sum(e * e, axis=0, keepdims=True)

        @pl.when(jnp.logical_not(is_lat))
        def _():
            dy_ref[...] = jnp.zeros_like(dy_ref)

        @pl.when(i == pl.num_programs(0) - 1)
        def _():
            loss_ref[...] = jnp.sum(acc[...], axis=1, keepdims=True) * (0.5 / D_MODEL)

    tok = pl.BlockSpec((TOKEN_BLOCK, D_MODEL), lambda i: (i, 0))
    tgt = pl.BlockSpec((TOKEN_BLOCK, D_MODEL), lambda i: ((i // nb) * nlat + jnp.maximum(i % nb - 1, 0), 0))
    return pl.pallas_call(
        body, name=name, grid=(T // TOKEN_BLOCK,),
        in_specs=[tok, tgt], out_specs=[tok, _full((1, 1))],
        out_shape=[jax.ShapeDtypeStruct((T, D_MODEL), F32), jax.ShapeDtypeStruct((1, 1), F32)],
        scratch_shapes=[pltpu.VMEM((1, D_MODEL), F32)],
        compiler_params=_params(("arbitrary",)),
    )(y2, tgt2)


_IN_OFFS = tuple(int(o) for o in np.cumsum((0,) + IN_SIZES))
_IN_GROUPS = (("a", 0, 416, 512), ("mg", 416, 512, 512), ("px", 928, 512, 512), ("pg", 1440, 512, 512), ("gq", 1952, 256, 256),
              ("gk", 2208, 256, 256), ("gv", 2464, 512, 512), ("lr", 2976, 32, 128), ("gg", 3008, 512, 512), ("m", 3520, 3072, 3072))


def _pad_cols(w, n):
    return w if w.shape[1] == n else jnp.pad(w, ((0, 0), (0, n - w.shape[1])))


def layer_weights(w_in, w_uq, w_ukv, af_w2, ab_w2, wbm, wbp, wbg, w_out):
    W = {}
    for nm, off, n, npad in _IN_GROUPS:
        W["in_" + nm] = _pad_cols(w_in[:, off:off + n], npad)
    uq = w_uq.reshape(MLA_Q_RANK, MLA_HEADS, MLA_NOPE + MLA_ROPE)
    W["qn"] = jnp.pad(uq[:, :, :MLA_NOPE], ((0, 0), (0, 0), (0, LANES - MLA_NOPE))).reshape(MLA_Q_RANK, MLA_HEADS * LANES)
    W["qr"] = jnp.pad(uq[:, :, MLA_NOPE:], ((0, 0), (0, 0), (0, LANES - MLA_ROPE))).reshape(MLA_Q_RANK, MLA_HEADS * LANES)
    W["kv"] = w_ukv
    W["af"] = jnp.pad(af_w2, ((0, LANES - GLA_GATE_RANK), (0, 0)))
    W["ab"] = jnp.pad(ab_w2, ((GLA_GATE_RANK, LANES - 2 * GLA_GATE_RANK), (0, 0)))
    W["bm"], W["bp"], W["bg"], W["out"] = wbm, wbp, wbg, w_out
    return W


def rope_tables(L, n_ctx):
    t = np.arange(L - n_ctx)
    half = MLA_ROPE // 2
    inv = ROPE_BASE ** (-np.arange(0, half, 2, dtype=np.float32) / half)
    ang_r = (t // GRID_W).astype(np.float32)[:, None] * inv
    ang_c = (t % GRID_W).astype(np.float32)[:, None] * inv
    ang = jnp.asarray(np.concatenate([ang_r, ang_r, ang_c, ang_c], axis=-1), F32)
    cos = jnp.ones((L, LANES), F32).at[n_ctx:, :MLA_ROPE].set(jnp.cos(ang))
    sin = jnp.zeros((L, LANES), F32).at[n_ctx:, :MLA_ROPE].set(jnp.sin(ang))
    return cos, sin


def layer_fwd(x2, ms, W, P, cos, sin, B, L, n_ctx, tag, hosted=None):
    nb = L // TOKEN_BLOCK
    r3 = lambda a: a.reshape(B, L, a.shape[-1])
    r2 = lambda a: a.reshape(B * L, a.shape[-1])
    names = [g[0] for g in _IN_GROUPS[:-1]]
    h, zs = norm_in_proj(x2, P["pre"], ms, [W["in_" + n] for n in names], nb, tag + "in_proj")
    z = dict(zip(names, zs))
    (z["m"],) = mm_multi(h, [W["in_m"]], [F32], tag + "in_proj_merge")
    qn, qr, kv, kr = mla_prep_fwd(z["a"], P["qg"], P["kvg"], W["qn"], W["qr"], W["kv"], cos, sin, nb, tag + "mla_prep")
    ya, y_mla, lse, carried = attention_fwd(r3(qn), r3(qr), r3(kv), r3(kr), r3(z["mg"]), n_ctx, tag + "attention", hosted)
    y_pool = pool_fwd(r3(z["px"]), r3(z["pg"]), P["pw"], P["ps"], n_ctx, tag + "pool")
    qf, kf, ksf, tf, qb, kb, ksb, tb = gla_prep_fwd(z["lr"], z["gq"], z["gk"], W["af"], W["ab"], P["baf"], P["bab"], tag + "gla_prep")
    (of, ssf), (ob, ssb) = gla_scan_fwd([(r3(qf), r3(kf), r3(ksf), r3(tf)), (r3(qb), r3(kb), r3(ksb), r3(tb))], r3(z["gv"]), n_ctx, tag + "gla_scan")
    y_gla = gla_out_fwd(r2(of), r2(ob), P["gn"], z["gg"], tag + "gla_out")
    ys = [r2(y_mla), r2(y_pool), y_gla]
    x_new, out, merged = merge_post_fwd(ys, z["m"], [W["bm"], W["bp"], W["bg"]], W["out"], x2, P["post"], ms, nb, tag + "merge_post")
    res = dict(x2=x2, h=h, z=z, qn=qn, qr=qr, kv=kv, kr=kr, ya=ya, lse=lse, ys=ys, gla_f=(qf, kf, ksf, tf, ssf), gla_b=(qb, kb, ksb, tb, ssb),
               of=of, ob=ob, out=out, merged=merged)
    return x_new, res, carried


def layer_bwd(dxn, res, ms, W, P, cos, sin, B, L, n_ctx, tag, hosted=None, host_own=None):
    nb = L // TOKEN_BLOCK
    r3 = lambda a: a.reshape(B, L, a.shape[-1])
    r2 = lambda a: a.reshape(B * L, a.shape[-1])
    z = res["z"]
    ys = res["ys"]
    wbs = [W["bm"], W["bp"], W["bg"]]
    dy0, dy1, dy2, dzm, dout, dp0, dp1, dp2, dgate, dpost = merge_post_bwd(dxn, res["out"], ys, z["m"], wbs, W["out"], P["post"], ms, nb, tag + "merge_post_bwd")
    G = {"out": mm_dw(res["merged"], dout, tag + "dw_out"), "post": dpost}
    for nm, y, dp in zip(("bm", "bp", "bg"), ys, (dp0, dp1, dp2)):
        G[nm] = mm_dw(y, dp, tag + "dw_" + nm)
    g = {n: _natural_grad(G, n) for n in GRADS_EARLY}
    own = {}
    dz = {"m": dzm}
    do, dz["gg"], G["gn"] = gla_out_bwd(r2(res["of"]), r2(res["ob"]), P["gn"], z["gg"], dy2, tag + "gla_out_bwd")
    carrier = host_own(GRADS_EARLY, g) if host_own else None
    *grads, arrived = gla_scan_bwd([(r3(qt), r3(kt), r3(ks), r3(tot), ss) for qt, kt, ks, tot, ss in (res["gla_f"], res["gla_b"])],
                                   r3(z["gv"]), r3(do), n_ctx, tag + "gla_scan_bwd", carrier)
    own.update(zip(GRADS_EARLY, arrived))
    gf = [r2(a) for a in grads[0]]
    gb = [r2(a) for a in grads[1]]
    dz["lr"], dz["gq"], dz["gk"], G["af"], G["ab"], G["baf"], G["bab"], dz["gv"] = gla_prep_bwd(
        z["lr"], z["gq"], z["gk"], W["af"], W["ab"], P["baf"], P["bab"], gf[:3] + gf[4:], gb[:3] + gb[4:], [gf[3], gb[3]], tag + "gla_prep_bwd")
    dpx, dpg, G["pw"], G["ps"] = pool_bwd(r3(z["px"]), r3(z["pg"]), P["pw"], P["ps"], r3(dy1), n_ctx, tag + "pool_bwd")
    dz["px"], dz["pg"] = r2(dpx), r2(dpg)
    dqn, dqr, dkv, dkr, dzmg, got = attention_bwd(r3(res["qn"]), r3(res["qr"]), r3(res["kv"]), r3(res["kr"]), r3(z["mg"]), res["ya"], res["lse"],
                                                  r3(dy0), n_ctx, tag + "attention_bwd", hosted)
    dz["mg"] = r2(dzmg)
    dz["a"], G["qn"], G["qr"], G["kv"], G["qg"], G["kvg"] = mla_prep_bwd(
        r2(dqn), r2(dqr), r2(dkv), r2(dkr), z["a"], P["qg"], P["kvg"], W["qn"], W["qr"], W["kv"], cos, sin, nb, tag + "mla_prep_bwd")
    names = [grp[0] for grp in _IN_GROUPS]
    for n in names:
        G["in_" + n] = mm_dw(res["h"], dz[n], tag + "dw_in_" + n)
    g.update({n: _natural_grad(G, n) for n in GRADS_LATE})
    carrier = host_own(GRADS_LATE, g) if host_own else None
    (dx, dshift, dscale, G["pre"]), arrived = in_proj_norm_bwd([dz[n] for n in names], [W["in_" + n] for n in names], res["x2"], P["pre"], ms, dxn, nb,
                                                               tag + "in_proj_dx", carrier)
    own.update(zip(GRADS_LATE, arrived))
    g.update({n: _natural_grad(G, n) for n in GRADS_REPLICATED})
    dms = jnp.concatenate([dshift, dscale, dgate], axis=-1)
    return dx, g, dms, got, own


GRADS_EARLY = ("w_branch_mla", "w_branch_pool", "w_branch_gla", "w_out")
GRADS_LATE = ("w_in", "mla_w_uq", "mla_w_ukv", "gla_af_w2", "gla_ab_w2")
GRADS_REPLICATED = ("pre_norm", "post_norm", "mla_q_norm", "mla_kv_norm", "pool_w", "pool_scale", "gla_af_b", "gla_ab_b", "gla_norm")
_DIRECT = dict(mla_w_ukv="kv", w_branch_mla="bm", w_branch_pool="bp", w_branch_gla="bg", w_out="out", pool_w="pw")
_ROW = dict(pre_norm="pre", post_norm="post", mla_q_norm="qg", mla_kv_norm="kvg", pool_scale="ps", gla_af_b="baf", gla_ab_b="bab", gla_norm="gn")


def _natural_grad(G, name):
    if name == "w_in":
        parts = {off: G["in_" + nm][:, :n] for nm, off, n, npad in _IN_GROUPS}
        return jnp.concatenate([parts[o] for o in sorted(parts)], axis=1)
    if name == "mla_w_uq":
        gqn = G["qn"].reshape(MLA_Q_RANK, MLA_HEADS, LANES)[:, :, :MLA_NOPE]
        gqr = G["qr"].reshape(MLA_Q_RANK, MLA_HEADS, LANES)[:, :, :MLA_ROPE]
        return jnp.concatenate([gqn, gqr], axis=-1).reshape(MLA_Q_RANK, MLA_HEADS * (MLA_NOPE + MLA_ROPE))
    if name == "gla_af_w2":
        return G["af"][:GLA_GATE_RANK]
    if name == "gla_ab_w2":
        return G["ab"][GLA_GATE_RANK:2 * GLA_GATE_RANK]
    return G[_DIRECT[name]] if name in _DIRECT else G[_ROW[name]][0]


def local_step(x, c, ctx, c_ctx, small, loss_target, depth, layer_full, host_fwd=None, host_bwd=None, host_own=None):
    B, S, _ = x.shape
    n_ctx = ctx.shape[1]
    L = n_ctx + S
    nb = L // TOKEN_BLOCK
    cos, sin = rope_tables(L, n_ctx)
    x2 = jnp.concatenate([ctx, x], axis=1).reshape(B * L, D_MODEL)
    a8 = jnp.zeros((8, D_MODEL), F32).at[:B].set(c).at[B].set(c_ctx)
    Ws, Ps, mss, ress, mod_ws = [], [], [], [], []
    carried = None
    for l in range(depth):
        tag = f"l{l}_"
        full = layer_full(l, carried)
        W = layer_weights(full["w_in"], full["mla_w_uq"], full["mla_w_ukv"], full["gla_af_w2"], full["gla_ab_w2"],
                          full["w_branch_mla"], full["w_branch_pool"], full["w_branch_gla"], full["w_out"])
        P = dict(pre=small["pre_norm"][l][None], post=small["post_norm"][l][None], qg=small["mla_q_norm"][l][None], kvg=small["mla_kv_norm"][l][None],
                 pw=small["pool_w"][l].astype(BF16), ps=small["pool_scale"][l][None], baf=small["gla_af_b"][l][None], bab=small["gla_ab_b"][l][None],
                 gn=small["gla_norm"][l][None])
        mod8 = mod_fwd(a8, full["mod_w"], small["mod_b"][l][None], tag + "mod")
        ms = jnp.stack([jnp.broadcast_to(mod8[B], (B, 3 * D_MODEL)), mod8[:B]], axis=1).reshape(2 * B, 1, 3 * D_MODEL)
        x2, res, carried = layer_fwd(x2, ms, W, P, cos, sin, B, L, n_ctx, tag, host_fwd(l) if host_fwd else None)
        Ws.append(W), Ps.append(P), mss.append(ms), ress.append(res), mod_ws.append(full["mod_w"])
    dx, loss = loss_head(x2, loss_target.reshape(B * S, D_MODEL), nb, "loss_head")
    grads = [None] * depth
    delivered = [None] * depth
    dz8s = [None] * depth
    da8 = None
    for l in reversed(range(depth)):
        tag = f"l{l}_"
        hosted = host_bwd(l, grads[l + 1]) if host_bwd and l + 1 < depth else None
        dx, g, dms, got, own = layer_bwd(dx, ress[l], mss[l], Ws[l], Ps[l], cos, sin, B, L, n_ctx, tag, hosted, host_own(l) if host_own else None)
        if hosted:
            delivered[l + 1] = got
        if own:
            delivered[l] = own
        dms = dms.reshape(B, 2, 3 * D_MODEL)
        dz8s[l] = jnp.zeros((8, 3 * D_MODEL), F32).at[:B].set(dms[:, 1]).at[B].set(jnp.sum(dms[:, 0], axis=0))
        g_mod_b, da = mod_bwd(a8, mod_ws[l], dz8s[l], tag + "mod_bwd")
        da8 = da if da8 is None else da8 + da
        g["mod_b"] = g_mod_b[0]
        grads[l] = g
    grad_x = dx.reshape(B, L, D_MODEL)[:, n_ctx:]
    return loss, grad_x, grads, da8[B], delivered, (a8, dz8s)


_MESH_ID = pl.DeviceIdType.MESH
_HBM = pl.BlockSpec(memory_space=pltpu.HBM)


def _me_and_peers():
    mx, my, mc = lax.axis_index("x"), lax.axis_index("y"), lax.axis_index("c")
    peers = []
    for k in range(1, N_DEV):
        px, py, pc = mx ^ ((k >> 2) & 1), my ^ ((k >> 1) & 1), mc ^ (k & 1)
        peers.append(((px, py, pc), 4 * px + 2 * py + pc))
    return 4 * mx + 2 * my + mc, peers


def _comm_scratch(n):
    return [pltpu.SemaphoreType.DMA((n * (N_DEV - 1),)), pltpu.SemaphoreType.DMA((n * (N_DEV - 1),)), pltpu.SemaphoreType.DMA((n,))]


class _Gather:
    def __init__(self, x_refs, o_refs, send_sems, recv_sems, local_sems):
        self.x, self.o, self.send, self.recv, self.local = x_refs, o_refs, send_sems, recv_sems, local_sems
        self.n = len(x_refs)
        mx, my, mc = lax.axis_index("x"), lax.axis_index("y"), lax.axis_index("c")
        self.me, self.sibling, self.mc = (mx, my, mc), (mx, my, 1 - mc), mc
        self.chips = [(1 - mx, my), (mx, 1 - my), (1 - mx, 1 - my)]

    @staticmethod
    def out_shape(xs):
        return [jax.ShapeDtypeStruct((N_DEV,) + x.shape, x.dtype) for x in xs]

    def _copy(self, i, k, block, to, src=None):
        px, py, pc = block
        dst = self.o[i].at[4 * px + 2 * py + pc]
        sem = (N_DEV - 1) * i + k
        return pltpu.make_async_remote_copy(src_ref=dst if src is None else src, dst_ref=dst, send_sem=self.send.at[sem],
                                            recv_sem=self.recv.at[sem], device_id=to, device_id_type=_MESH_ID)

    def _mine(self, i):
        mx, my, mc = self.me
        return pltpu.make_async_copy(self.x[i], self.o[i].at[4 * mx + 2 * my + mc], self.local.at[i])

    def _first(self):
        out = []
        for i in range(self.n):
            out.append(self._copy(i, 0, self.me, self.sibling, src=self.x[i]))
            out += [self._copy(i, 1 + j, self.me, (*chip, self.mc), src=self.x[i]) for j, chip in enumerate(self.chips)]
        return out

    def _passed(self, j, i):
        return self._copy(i, 4 + j, (*self.chips[j], self.mc), self.sibling)

    def start(self):
        for i in range(self.n):
            self._mine(i).start()
        for cp in self._first():
            cp.start()

    def forward(self):
        for j, chip in enumerate(self.chips):
            for i in range(self.n):
                self._copy(i, 1 + j, (*chip, self.mc), self.me).wait_recv()
                self._passed(j, i).start()

    def finish(self):
        for i in range(self.n):
            self._copy(i, 0, self.sibling, self.me).wait_recv()
            for j, chip in enumerate(self.chips):
                self._copy(i, 4 + j, (*chip, 1 - self.mc), self.me).wait_recv()
        for cp in self._first():
            cp.wait_send()
        for j in range(len(self.chips)):
            for i in range(self.n):
                self._passed(j, i).wait_send()
        for i in range(self.n):
            self._mine(i).wait()


class _Scatter:
    def __init__(self, x_refs, o_refs, send_sems, recv_sems, local_sems):
        self.x, self.o, self.send, self.recv, self.local = x_refs, o_refs, send_sems, recv_sems, local_sems
        self.n = len(x_refs)
        self.me, self.peers = _me_and_peers()

    @staticmethod
    def out_shape(xs):
        return [jax.ShapeDtypeStruct(x.shape, x.dtype) for x in xs]

    def _copy(self, i, k, src_slot, dst_slot, to):
        sem = (N_DEV - 1) * i + k
        return pltpu.make_async_remote_copy(src_ref=self.x[i].at[src_slot], dst_ref=self.o[i].at[dst_slot], send_sem=self.send.at[sem],
                                            recv_sem=self.recv.at[sem], device_id=to, device_id_type=_MESH_ID)

    def _mine(self, i):
        return pltpu.make_async_copy(self.x[i].at[self.me], self.o[i].at[self.me], self.local.at[i])

    def _sends(self):
        return [self._copy(i, k, slot, self.me, peer) for k, (peer, slot) in enumerate(self.peers) for i in range(self.n)]

    def start(self):
        for i in range(self.n):
            self._mine(i).start()
        for cp in self._sends():
            cp.start()

    def forward(self):
        pass

    def finish(self):
        for k, (peer, slot) in enumerate(self.peers):
            for i in range(self.n):
                self._copy(i, k, slot, slot, peer).wait_recv()
        for cp in self._sends():
            cp.wait_send()
        for i in range(self.n):
            self._mine(i).wait()


class _Hosted:
    def __init__(self, kind, xs):
        self.kind, self.xs, self.n = kind, list(xs), len(xs)
        self.in_specs = [_HBM] * self.n
        self.out_specs = [_HBM] * self.n
        self.out_shape = kind.out_shape(self.xs)
        self.scratch = _comm_scratch(self.n)

    def run(self, x_refs, o_refs, sems, step, total):
        for when, phase in ((0, "start"), (total // 2, "forward"), (total - 1, "finish")):
            @pl.when(step == when)
            def _(phase=phase):
                getattr(self.kind(x_refs, o_refs, *sems), phase)()


def gather_blocks(xs, name):
    n = len(xs)

    def body(*refs):
        g = _Gather(refs[:n], refs[n:2 * n], *refs[2 * n:])
        g.start()
        g.forward()
        g.finish()

    return pl.pallas_call(
        body, name=name, in_specs=[_HBM] * n, out_specs=[_HBM] * n,
        out_shape=_Gather.out_shape(xs), scratch_shapes=_comm_scratch(n),
    )(*xs)


def reduce_adamw(slots, w, m, v, name, tr=256):
    R, C = w.shape
    nl = len(slots)
    ns = slots[0].shape[0]
    rows = R // nl
    tr = min(tr, rows)
    nbl = rows // tr
    c1 = 1.0 / (1.0 - ADAM_B1 ** ADAM_STEP)
    c2 = 1.0 / (1.0 - ADAM_B2 ** ADAM_STEP)

    def body(*refs):
        w_ref, m_ref, v_ref, g_ref, d_ref, nm_ref, nv_ref = refs[nl:]
        part = pl.program_id(0) // nbl
        g = None
        for l, s_ref in enumerate(refs[:nl]):
            gl = s_ref[0].astype(F32)
            for s in range(1, ns):
                gl = gl + s_ref[s].astype(F32)
            g = gl if g is None else jnp.where(part == l, gl, g)
        nm = ADAM_B1 * m_ref[...] + (1.0 - ADAM_B1) * g
        nv = ADAM_B2 * v_ref[...] + (1.0 - ADAM_B2) * (g * g)
        g_ref[...] = g
        nm_ref[...] = nm
        nv_ref[...] = nv
        d_ref[...] = -ADAM_LR * ((nm * c1) / (jnp.sqrt(nv * c2) + ADAM_EPS) + ADAM_WD * w_ref[...])

    blk = pl.BlockSpec((tr, C), lambda i: (i, 0))
    sspecs = [pl.BlockSpec((ns, tr, C), lambda i, l=l: (0, jnp.clip(i - l * nbl, 0, nbl - 1), 0)) for l in range(nl)]
    return pl.pallas_call(
        body, name=name, grid=(R // tr,),
        in_specs=sspecs + [blk, blk, blk], out_specs=[blk] * 4,
        out_shape=[jax.ShapeDtypeStruct((R, C), F32)] * 4,
        compiler_params=_params(("parallel",), VMEM_LIMIT),
    )(*slots, w, m, v)


ARG_WEIGHTS = ("c_ctx", "mod_w", "mod_b", "pre_norm", "post_norm", "w_in", "mla_q_norm", "mla_w_uq", "mla_kv_norm", "mla_w_ukv", "pool_w",
               "pool_scale", "gla_af_w2", "gla_af_b", "gla_ab_w2", "gla_ab_b", "gla_norm", "w_branch_mla", "w_branch_pool", "w_branch_gla", "w_out")
SHARDED = ("mod_w", "w_in", "mla_w_uq", "mla_w_ukv", "gla_af_w2", "gla_ab_w2", "w_branch_mla", "w_branch_pool", "w_branch_gla", "w_out")
ROW_SHARDED = ("w_out",)
REPLICATED = tuple(n for n in ARG_WEIGHTS if n not in SHARDED)
PACK_ROWS = 512


def _pack(parts, dtype):
    flat = jnp.concatenate([p.astype(dtype).reshape(-1) for p in parts])
    n = flat.shape[0]
    total = -(-n // (PACK_ROWS * LANES)) * (PACK_ROWS * LANES)
    return jnp.pad(flat, (0, total - n)).reshape(total // LANES, LANES)


def _unpack(buf, shapes):
    flat = buf.reshape(-1)
    out, off = [], 0
    for shp in shapes:
        n = math.prod(shp)
        out.append(flat[off:off + n].reshape(shp))
        off += n
    return out


def _gathered_to_full(g, name):
    _, r, cs = g.shape
    if name in ROW_SHARDED:
        return g.reshape(N_DEV * r, cs)
    return g.transpose(1, 0, 2).reshape(r, N_DEV * cs)


def _full_to_slots(w, name):
    if name in ROW_SHARDED:
        return w.reshape(N_DEV, w.shape[0] // N_DEV, w.shape[1])
    return w.reshape(w.shape[0], N_DEV, w.shape[1] // N_DEV).transpose(1, 0, 2)


def kernel(x, c, ctx, c_ctx, mod_w, mod_b, pre_norm, post_norm, w_in, mla_q_norm, mla_w_uq, mla_kv_norm, mla_w_ukv, pool_w, pool_scale, gla_af_w2, gla_af_b, gla_ab_w2, gla_ab_b, gla_norm, w_branch_mla, w_branch_pool, w_branch_gla, w_out, loss_target, m_c_ctx, m_mod_w, m_mod_b, m_pre_norm, m_post_norm, m_w_in, m_mla_q_norm, m_mla_w_uq, m_mla_kv_norm, m_mla_w_ukv, m_pool_w, m_pool_scale, m_gla_af_w2, m_gla_af_b, m_gla_ab_w2, m_gla_ab_b, m_gla_norm, m_w_branch_mla, m_w_branch_pool, m_w_branch_gla, m_w_out, v_c_ctx, v_mod_w, v_mod_b, v_pre_norm, v_post_norm, v_w_in, v_mla_q_norm, v_mla_w_uq, v_mla_kv_norm, v_mla_w_ukv, v_pool_w, v_pool_scale, v_gla_af_w2, v_gla_af_b, v_gla_ab_w2, v_gla_ab_b, v_gla_norm, v_w_branch_mla, v_w_branch_pool, v_w_branch_gla, v_w_out):
    local = dict(locals())
    wts = {n: local[n] for n in ARG_WEIGHTS}
    mom1 = {n: local["m_" + n] for n in ARG_WEIGHTS}
    mom2 = {n: local["v_" + n] for n in ARG_WEIGHTS}
    shard_shapes = [wts[n].shape for n in SHARDED]
    rep_shapes = [wts[n].shape for n in REPLICATED]
    kinds = ("grad", "delta", "new_m", "new_v")

    depth = w_in.shape[0]

    def shards(l):
        return [wts[n][l].astype(BF16) for n in SHARDED]

    first = gather_blocks(shards(0), "gather_weights_l0")

    def layer_full(l, carried):
        return {n: _gathered_to_full(gw, n) for n, gw in zip(SHARDED, first if l == 0 else carried)}

    def host_fwd(l):
        return _Hosted(_Gather, shards(l + 1)) if l + 1 < depth else None

    exchanged = GRADS_EARLY + GRADS_LATE

    def slots(g, names):
        return [_full_to_slots(g[n], n).astype(BF16) for n in names]

    def host_bwd(l, g_above):
        return _Hosted(_Scatter, slots(g_above, exchanged))

    def host_own(l):
        return (lambda names, g: _Hosted(_Scatter, slots(g, names))) if l == 0 else None

    small = {n: wts[n] for n in REPLICATED}
    loss, grad_x, grads, g_c_ctx, arrived, (a8, dz8s) = local_step(x, c, ctx, c_ctx, small, loss_target, depth, layer_full, host_fwd, host_bwd, host_own)
    arrived = [a if isinstance(a, dict) else dict(zip(exchanged, a)) for a in arrived]

    g = {n: (g_c_ctx if n == "c_ctx" else jnp.stack([grads[l][n] for l in range(depth)])) for n in REPLICATED}
    gathered, a_all, dz_all = gather_blocks([_pack([g[n] for n in REPLICATED], F32), a8, jnp.concatenate(dz8s, axis=0)], "gather_small_grads")
    outs = reduce_adamw([gathered], _pack([wts[n] for n in REPLICATED], F32), _pack([mom1[n] for n in REPLICATED], F32),
                        _pack([mom2[n] for n in REPLICATED], F32), "adamw_replicated")
    me = 4 * lax.axis_index("x") + 2 * lax.axis_index("y") + lax.axis_index("c")
    ncol = mod_w.shape[2]
    dz_cols = lax.dynamic_slice_in_dim(dz_all.reshape(N_DEV, depth, 8, 3 * D_MODEL), me * ncol, ncol, axis=3)
    g_mod_w = mod_dw_columns(a_all.reshape(N_DEV * 8, D_MODEL), dz_cols.transpose(1, 0, 2, 3).reshape(depth, N_DEV * 8, ncol), "mod_dw")

    res = {kind: {} for kind in kinds}
    for n, shp in zip(SHARDED, shard_shapes):
        flat = (shp[0] * shp[1], shp[2])
        parts = [g_mod_w.reshape((1,) + flat)] if n == "mod_w" else [arrived[l][n] for l in range(depth)]
        for kind, o in zip(kinds, reduce_adamw(parts, wts[n].reshape(flat), mom1[n].reshape(flat), mom2[n].reshape(flat), "adamw_" + n)):
            res[kind][n] = o.reshape(shp)
    for kind, o in zip(("grad", "delta", "new_m", "new_v"), outs):
        res[kind].update(zip(REPLICATED, _unpack(o, rep_shapes)))

    loss = lax.psum(loss[0, 0], ("x", "y", "c"))
    return (loss, grad_x, *[res[kind][n] for kind in ("grad", "delta", "new_m", "new_v") for n in ARG_WEIGHTS])
```

```python
import functools
import math

import jax
import jax.numpy as jnp
import numpy as np
from jax import lax
from jax.experimental import pallas as pl
from jax.experimental.pallas import tpu as pltpu

F32 = jnp.float32
BF16 = jnp.bfloat16

D_MODEL = 1024
NORM_EPS = 1e-6
GRID_W = 64
MLA_HEADS, MLA_Q_RANK, MLA_KV_RANK, MLA_NOPE, MLA_ROPE, MLA_V = 8, 256, 128, 64, 32, 64
MLA_WIDTH = MLA_HEADS * MLA_V
ROPE_BASE = 10000.0
ATT_SCALE = (MLA_NOPE + MLA_ROPE) ** -0.5
POOL_WINDOWS = (2, 4, 8, 16)
POOL_WIDTH, POOL_GROUP = 512, 128
GLA_HEADS, GLA_DK, GLA_DV = 4, 64, 128
GLA_KW, GLA_WIDTH = GLA_HEADS * GLA_DK, GLA_HEADS * GLA_DV
GLA_GATE_RANK, GLA_TAU, GLA_CHUNK = 16, 16.0, 64
IN_SIZES = (256, 128, 32, 512, 512, 512, 256, 256, 512, 16, 16, 512, 3 * D_MODEL)
ADAM_LR, ADAM_B1, ADAM_B2, ADAM_EPS, ADAM_WD, ADAM_STEP = 0.001, 0.9, 0.999, 1e-08, 0.01, 10
N_DEV = 8

LANES = 128
TOKEN_BLOCK = 256
WIDE_BLOCK = 768
VMEM_LIMIT = 48 * 1024 * 1024
NEG_BIG = -1e30

_NT = (((1,), (1,)), ((), ()))
_TN = (((0,), (0,)), ((), ()))


def _dot(a, b):
    return jnp.dot(a, b, preferred_element_type=F32)


def _dot_nt(a, b):
    return lax.dot_general(a, b, _NT, preferred_element_type=F32)


def _dot_tn(a, b):
    return lax.dot_general(a, b, _TN, preferred_element_type=F32)


def _params(sem=None, vmem=None):
    kw = {}
    if sem is not None:
        kw["dimension_semantics"] = sem
    if vmem is not None:
        kw["vmem_limit_bytes"] = vmem
    return pltpu.CompilerParams(**kw)


def _wide_block(rows):
    return WIDE_BLOCK if rows % WIDE_BLOCK == 0 else TOKEN_BLOCK


def _full(shape):
    n = len(shape)
    return pl.BlockSpec(shape, lambda *_: (0,) * n)


def _sig(x):
    return 1.0 / (1.0 + jnp.exp(-x))


def _silu_and_grad(x):
    s = _sig(x)
    return x * s, s * (1.0 + x * (1.0 - s))


def _acc(ref, val, first):
    @pl.when(first)
    def _():
        ref[...] = val

    @pl.when(jnp.logical_not(first))
    def _():
        ref[...] += val


def mm_multi(a, ws, dtypes, name, tm=TOKEN_BLOCK):
    M, K = a.shape
    nw = len(ws)

    def body(a_ref, *refs):
        av = a_ref[...]
        for w_ref, o_ref in zip(refs[:nw], refs[nw:]):
            o_ref[...] = _dot(av, w_ref[...]).astype(o_ref.dtype)

    return pl.pallas_call(
        body, name=name, grid=(M // tm,),
        in_specs=[pl.BlockSpec((tm, K), lambda i: (i, 0))] + [_full(w.shape) for w in ws],
        out_specs=[pl.BlockSpec((tm, w.shape[1]), lambda i: (i, 0)) for w in ws],
        out_shape=[jax.ShapeDtypeStruct((M, w.shape[1]), dt) for w, dt in zip(ws, dtypes)],
        compiler_params=_params(("parallel",), VMEM_LIMIT),
    )(a, *ws)


def mm_dw(a, dz, name, tn=1024):
    M, K = a.shape
    n = dz.shape[1]
    tn = min(tn, n)
    tk = next(t for t in (3072, 1536, 1024, 512, TOKEN_BLOCK) if M % t == 0)

    def body(a_ref, dz_ref, o_ref):
        _acc(o_ref, _dot_tn(a_ref[...], dz_ref[...]), pl.program_id(1) == 0)

    return pl.pallas_call(
        body, name=name, grid=(n // tn, M // tk),
        in_specs=[pl.BlockSpec((tk, K), lambda j, k: (k, 0)), pl.BlockSpec((tk, tn), lambda j, k: (k, j))],
        out_specs=pl.BlockSpec((K, tn), lambda j, k: (0, j)),
        out_shape=jax.ShapeDtypeStruct((K, n), F32),
        compiler_params=_params(("parallel", "arbitrary"), VMEM_LIMIT),
    )(a, dz)


def mod_fwd(a8, w, b, name):
    tn = D_MODEL

    def body(a_ref, w_ref, b_ref, o_ref):
        a = a_ref[...]
        o_ref[...] = _dot((a * _sig(a)).astype(BF16), w_ref[...]) + b_ref[...]

    return pl.pallas_call(
        body, name=name, grid=(3,),
        in_specs=[_full(a8.shape), pl.BlockSpec((D_MODEL, tn), lambda j: (0, j)), pl.BlockSpec((1, tn), lambda j: (0, j))],
        out_specs=pl.BlockSpec((8, tn), lambda j: (0, j)),
        out_shape=jax.ShapeDtypeStruct((8, 3 * D_MODEL), F32),
        compiler_params=_params(("parallel",)),
    )(a8, w, b)


def mod_bwd(a8, w, dz8, name):
    tn = D_MODEL

    def body(a_ref, w_ref, dz_ref, db_ref, da_ref):
        a = a_ref[...]
        _, dsa = _silu_and_grad(a)
        dz = dz_ref[...]
        db_ref[...] = jnp.sum(dz, axis=0, keepdims=True)
        _acc(da_ref, _dot_nt(dz.astype(BF16), w_ref[...]) * dsa, pl.program_id(0) == 0)

    return pl.pallas_call(
        body, name=name, grid=(3,),
        in_specs=[_full(a8.shape), pl.BlockSpec((D_MODEL, tn), lambda j: (0, j)), pl.BlockSpec((8, tn), lambda j: (0, j))],
        out_specs=[pl.BlockSpec((1, tn), lambda j: (0, j)), _full((8, D_MODEL))],
        out_shape=[jax.ShapeDtypeStruct((1, 3 * D_MODEL), F32), jax.ShapeDtypeStruct((8, D_MODEL), F32)],
        compiler_params=_params(("arbitrary",)),
    )(a8, w, dz8)


def mod_dw_columns(a_all, dz_cols, name):
    depth, R, n = dz_cols.shape

    def body(a_ref, dz_ref, dw_ref):
        a = a_ref[...]
        dw_ref[0] = _dot_tn((a * _sig(a)).astype(BF16), dz_ref[0].astype(BF16))

    return pl.pallas_call(
        body, name=name, grid=(depth,),
        in_specs=[_full(a_all.shape), pl.BlockSpec((1, R, n), lambda l: (l, 0, 0))],
        out_specs=pl.BlockSpec((1, D_MODEL, n), lambda l: (l, 0, 0)),
        out_shape=jax.ShapeDtypeStruct((depth, D_MODEL, n), F32),
        compiler_params=_params(("parallel",)),
    )(a_all, dz_cols)


def _mod_row(nb):
    return lambda i: 2 * (i // nb) + jnp.minimum(i % nb, 1)


def _mod_spec(nb, part):
    row = _mod_row(nb)
    return pl.BlockSpec((1, 1, D_MODEL), lambda i: (row(i), 0, part))


def norm_in_proj(x2, g, ms, ws, nb, name):
    T = x2.shape[0]
    nw = len(ws)

    def body(x_ref, g_ref, sh_ref, sc_ref, *refs):
        x = x_ref[...]
        r = lax.rsqrt(jnp.mean(x * x, axis=-1, keepdims=True) + NORM_EPS)
        h = ((x * r) * g_ref[...] * (1.0 + sc_ref[0]) + sh_ref[0]).astype(BF16)
        refs[nw][...] = h
        for w_ref, o_ref in zip(refs[:nw], refs[nw + 1:]):
            o_ref[...] = _dot(h, w_ref[...])

    tok = lambda n: pl.BlockSpec((TOKEN_BLOCK, n), lambda i: (i, 0))
    outs = pl.pallas_call(
        body, name=name, grid=(T // TOKEN_BLOCK,),
        in_specs=[tok(D_MODEL), _full((1, D_MODEL)), _mod_spec(nb, 0), _mod_spec(nb, 1)] + [_full(w.shape) for w in ws],
        out_specs=[tok(D_MODEL)] + [tok(w.shape[1]) for w in ws],
        out_shape=[jax.ShapeDtypeStruct((T, D_MODEL), BF16)] + [jax.ShapeDtypeStruct((T, w.shape[1]), F32) for w in ws],
        compiler_params=_params(("parallel",), VMEM_LIMIT),
    )(x2, g, ms, ms, *ws)
    return outs[0], outs[1:]


def in_proj_norm_bwd(dzs, ws, x2, g, ms, dxres, nb, name, hosted=None):
    T = x2.shape[0]
    nw = len(ws)
    nx = hosted.n if hosted else 0
    nrow = ms.shape[0]
    row = _mod_row(nb)
    n_in = 2 * nw + 4

    def body(*refs):
        x_ref, g_ref, sc_ref, dxr_ref = refs[2 * nw:n_in]
        dx_ref, dsh_ref, dsc_ref, dg_ref = refs[n_in + nx:n_in + nx + 4]
        i = pl.program_id(0)
        if hosted:
            hosted.run(refs[n_in:n_in + nx], refs[n_in + nx + 4:n_in + 2 * nx + 4], refs[n_in + 2 * nx + 4:], i, T // TOKEN_BLOCK)
        dh = None
        for dz_ref, w_ref in zip(refs[:nw], refs[nw:2 * nw]):
            t = _dot_nt(dz_ref[...], w_ref[...])
            dh = t if dh is None else dh + t
        x = x_ref[...]
        g = g_ref[...]
        r = lax.rsqrt(jnp.mean(x * x, axis=-1, keepdims=True) + NORM_EPS)
        xn = x * r
        du = dh * (1.0 + sc_ref[0])
        dyg = du * g
        dx_ref[...] = dxr_ref[...] + r * (dyg - xn * jnp.mean(dyg * xn, axis=-1, keepdims=True))
        first = (i % nb) <= 1
        _acc(dsh_ref.at[0], jnp.sum(dh, axis=0, keepdims=True), first)
        _acc(dsc_ref.at[0], jnp.sum(dh * xn * g, axis=0, keepdims=True), first)
        _acc(dg_ref, jnp.sum(du * xn, axis=0, keepdims=True), i == 0)

    tok = lambda n: pl.BlockSpec((TOKEN_BLOCK, n), lambda i: (i, 0))
    acc = pl.BlockSpec((1, 1, D_MODEL), lambda i: (row(i), 0, 0))
    outs = pl.pallas_call(
        body, name=name, grid=(T // TOKEN_BLOCK,),
        in_specs=[tok(dz.shape[1]) for dz in dzs] + [_full(w.shape) for w in ws] + [tok(D_MODEL), _full((1, D_MODEL)), _mod_spec(nb, 1), tok(D_MODEL)]
        + (hosted.in_specs if hosted else []),
        out_specs=[tok(D_MODEL), acc, acc, _full((1, D_MODEL))] + (hosted.out_specs if hosted else []),
        out_shape=[jax.ShapeDtypeStruct((T, D_MODEL), F32), jax.ShapeDtypeStruct((nrow, 1, D_MODEL), F32),
                   jax.ShapeDtypeStruct((nrow, 1, D_MODEL), F32), jax.ShapeDtypeStruct((1, D_MODEL), F32)] + (hosted.out_shape if hosted else []),
        scratch_shapes=hosted.scratch if hosted else [],
        compiler_params=_params(("arbitrary",), VMEM_LIMIT),
    )(*dzs, *ws, x2, g, ms, dxres, *(hosted.xs if hosted else []))
    return outs[:4], list(outs[4:])


def _rot(x):
    lane = lax.broadcasted_iota(jnp.int32, x.shape, 1)
    return jnp.where((lane % 16) < 8, -pltpu.roll(x, LANES - 8, 1), pltpu.roll(x, 8, 1))


def _rope(x, cos, sin):
    return x * cos + _rot(x) * sin


def _rope_t(dy, cos, sin):
    return dy * cos - _rot(dy * sin)


def _rms_rows(x):
    r = lax.rsqrt(jnp.mean(x * x, axis=-1, keepdims=True) + NORM_EPS)
    return x * r, r


def _rms_rows_bwd(dyg, xn, r):
    return r * (dyg - xn * jnp.mean(dyg * xn, axis=-1, keepdims=True))


def mla_prep_fwd(za, qg, kvg, wqn, wqr, wkv, cos, sin, nb, name):
    T = za.shape[0]
    W = MLA_HEADS * LANES

    def body(z_ref, qg_ref, kvg_ref, wqn_ref, wqr_ref, wkv_ref, cos_ref, sin_ref, qn_ref, qr_ref, kv_ref, kr_ref):
        z = z_ref[...]
        cos = cos_ref[...]
        sin = sin_ref[...]
        xq, _ = _rms_rows(z[:, 0:256])
        qn = (xq * qg_ref[...]).astype(BF16)
        qn_ref[...] = (_dot(qn, wqn_ref[...]) * ATT_SCALE).astype(BF16)
        qr = _dot(qn, wqr_ref[...])
        for h in range(MLA_HEADS):
            sl = slice(LANES * h, LANES * (h + 1))
            qr_ref[:, sl] = (_rope(qr[:, sl], cos, sin) * ATT_SCALE).astype(BF16)
        xkv, _ = _rms_rows(z[:, 256:384])
        kv_ref[...] = _dot((xkv * kvg_ref[...]).astype(BF16), wkv_ref[...]).astype(BF16)
        kr_ref[...] = _rope(z[:, 384:512], cos, sin).astype(BF16)

    tb = _wide_block(nb * TOKEN_BLOCK)
    npos = nb * TOKEN_BLOCK // tb
    tok = lambda n: pl.BlockSpec((tb, n), lambda i: (i, 0))
    pos = pl.BlockSpec((tb, LANES), lambda i: (i % npos, 0))
    return pl.pallas_call(
        body, name=name, grid=(T // tb,),
        in_specs=[tok(512), _full(qg.shape), _full(kvg.shape), _full(wqn.shape), _full(wqr.shape), _full(wkv.shape), pos, pos],
        out_specs=[tok(W), tok(W), tok(W), tok(LANES)],
        out_shape=[jax.ShapeDtypeStruct((T, W), BF16)] * 3 + [jax.ShapeDtypeStruct((T, LANES), BF16)],
        compiler_params=_params(("parallel",)),
    )(za, qg, kvg, wqn, wqr, wkv, cos, sin)


def mla_prep_bwd(dqn, dqr, dkv, dkr, za, qg, kvg, wqn, wqr, wkv, cos, sin, nb, name):
    T = za.shape[0]
    W = MLA_HEADS * LANES

    def body(dqn_ref, dqr_ref, dkv_ref, dkr_ref, z_ref, qg_ref, kvg_ref, wqn_ref, wqr_ref, wkv_ref, cos_ref, sin_ref,
             dz_ref, dwqn_ref, dwqr_ref, dwkv_ref, dqg_ref, dkvg_ref):
        first = pl.program_id(0) == 0
        z = z_ref[...]
        cos = cos_ref[...]
        sin = sin_ref[...]
        qg = qg_ref[...]
        kvg = kvg_ref[...]
        xq, rq = _rms_rows(z[:, 0:256])
        qn = (xq * qg).astype(BF16)
        a1 = (dqn_ref[...].astype(F32) * ATT_SCALE).astype(BF16)
        parts = []
        for h in range(MLA_HEADS):
            sl = slice(LANES * h, LANES * (h + 1))
            parts.append(_rope_t(dqr_ref[:, sl].astype(F32) * ATT_SCALE, cos, sin).astype(BF16))
        a2 = jnp.concatenate(parts, axis=1)
        dq = _dot_nt(a1, wqn_ref[...]) + _dot_nt(a2, wqr_ref[...])
        _acc(dwqn_ref, _dot_tn(qn, a1), first)
        _acc(dwqr_ref, _dot_tn(qn, a2), first)
        _acc(dqg_ref, jnp.sum(dq * xq, axis=0, keepdims=True), first)
        dz_ref[:, 0:256] = _rms_rows_bwd(dq * qg, xq, rq).astype(BF16)
        xkv, rkv = _rms_rows(z[:, 256:384])
        kvn = (xkv * kvg).astype(BF16)
        dkvb = dkv_ref[...].astype(BF16)
        dk = _dot_nt(dkvb, wkv_ref[...])
        _acc(dwkv_ref, _dot_tn(kvn, dkvb), first)
        _acc(dkvg_ref, jnp.sum(dk * xkv, axis=0, keepdims=True), first)
        dz_ref[:, 256:384] = _rms_rows_bwd(dk * kvg, xkv, rkv).astype(BF16)
        dz_ref[:, 384:512] = _rope_t(dkr_ref[...], cos, sin).astype(BF16)

    tb = _wide_block(nb * TOKEN_BLOCK)
    npos = nb * TOKEN_BLOCK // tb
    tok = lambda n: pl.BlockSpec((tb, n), lambda i: (i, 0))
    pos = pl.BlockSpec((tb, LANES), lambda i: (i % npos, 0))
    return pl.pallas_call(
        body, name=name, grid=(T // tb,),
        in_specs=[tok(W), tok(W), tok(W), tok(LANES), tok(512), _full(qg.shape), _full(kvg.shape), _full(wqn.shape),
                  _full(wqr.shape), _full(wkv.shape), pos, pos],
        out_specs=[tok(512), _full(wqn.shape), _full(wqr.shape), _full(wkv.shape), _full(qg.shape), _full(kvg.shape)],
        out_shape=[jax.ShapeDtypeStruct((T, 512), BF16), jax.ShapeDtypeStruct(wqn.shape, F32), jax.ShapeDtypeStruct(wqr.shape, F32),
                   jax.ShapeDtypeStruct(wkv.shape, F32), jax.ShapeDtypeStruct(qg.shape, F32), jax.ShapeDtypeStruct(kvg.shape, F32)],
        compiler_params=_params(("arbitrary",)),
    )(dqn, dqr, dkv, dkr, za, qg, kvg, wqn, wqr, wkv, cos, sin)


def _att_qk(qn_ref, qr_ref, kv_ref, kr, j):
    sl = slice(LANES * j, LANES * (j + 1))
    q = jnp.concatenate([qn_ref[0, :, sl], qr_ref[0, :, sl]], axis=1)
    kvj = kv_ref[0, :, sl]
    k = jnp.concatenate([kvj, kr], axis=1)
    return q, k, kvj, _dot_nt(q, k)


def _att_specs(L, lk, q0, pairs=1):
    TQ, W2 = TOKEN_BLOCK, 2 * LANES * pairs
    qspec = pl.BlockSpec((1, TQ, W2), lambda b, h, i: (b, i + q0, h))
    kvspec = pl.BlockSpec((1, lk, W2), lambda b, h, i: (b, 0, h))
    krspec = pl.BlockSpec((1, lk, LANES), lambda b, h, i: (b, 0, 0))
    gspec = pl.BlockSpec((1, TQ, LANES * pairs), lambda b, h, i: (b, i + q0, h))
    lspec = pl.BlockSpec((1, pairs, TQ, LANES), lambda b, h, i: (b, h, i + q0, 0))
    return qspec, kvspec, krspec, gspec, lspec


_ANY = pl.BlockSpec(memory_space=pl.ANY)


def attention_fwd(qn, qr, kv, kr, zg, n_ctx, name, hosted=None):
    B, L, _ = qn.shape
    TQ = TOKEN_BLOCK
    PAIRS = 2
    HP = MLA_HEADS // 2 // PAIRS
    shapes = [jax.ShapeDtypeStruct((B, L, MLA_WIDTH), F32), jax.ShapeDtypeStruct((B, L, MLA_WIDTH), BF16),
              jax.ShapeDtypeStruct((B, MLA_HEADS // 2, L, LANES), F32)]

    nx = hosted.n if hosted else 0
    NQ = L // TQ - 1

    def body(*refs):
        if hosted:
            step = (pl.program_id(0) * HP + pl.program_id(1)) * NQ + pl.program_id(2)
            hosted.run(refs[5:5 + nx], refs[8 + nx:8 + 2 * nx], refs[8 + 2 * nx:], step, B * HP * NQ)
        _fwd_step(*refs[:5], *refs[5 + nx:8 + nx])

    def body_ctx(qn_ref, qr_ref, kv_ref, kr_ref, g_ref, *rest):
        _fwd_step(qn_ref, qr_ref, kv_ref, kr_ref, g_ref, *rest[-3:])

    def _fwd_step(qn_ref, qr_ref, kv_ref, kr_ref, g_ref, ya_ref, ym_ref, lse_ref):
        kr_v = kr_ref[0]
        for pr in range(PAIRS):
            outs, lses = [], []
            for j in (2 * pr, 2 * pr + 1):
                _, _, kvj, s = _att_qk(qn_ref, qr_ref, kv_ref, kr_v, j)
                m = jnp.max(s, axis=-1, keepdims=True)
                p = jnp.exp(s - m).astype(BF16)
                lane_k = lax.broadcasted_iota(jnp.int32, kvj.shape, 1)
                o = _dot(p, jnp.where(lane_k < MLA_V, jnp.ones_like(kvj), kvj))
                l = o[:, 0:1]
                outs.append(o / l)
                lses.append(m + jnp.log(l))
            lane = lax.broadcasted_iota(jnp.int32, outs[0].shape, 1)
            y = jnp.where(lane < MLA_V, pltpu.roll(outs[0], MLA_V, 1), outs[1])
            sl = slice(LANES * pr, LANES * (pr + 1))
            ya_ref[0, :, sl] = y
            g = g_ref[0, :, sl]
            ym_ref[0, :, sl] = (y * g * _sig(g)).astype(BF16)
            lse_ref[0, pr] = jnp.where(lane < MLA_V, lses[0], lses[1])

    qspec, kvspec, krspec, gspec, lspec = _att_specs(L, L, 1, PAIRS)
    main = pl.pallas_call(
        body, name=name, grid=(B, HP, NQ),
        in_specs=[qspec, qspec, kvspec, krspec, gspec] + (hosted.in_specs if hosted else []),
        out_specs=[gspec, gspec, lspec] + (hosted.out_specs if hosted else []),
        out_shape=shapes + (hosted.out_shape if hosted else []), scratch_shapes=hosted.scratch if hosted else [],
        compiler_params=_params(("arbitrary",) * 3 if hosted else ("parallel",) * 3, VMEM_LIMIT),
    )(qn, qr, kv, kr, zg, *(hosted.xs if hosted else []))
    qspec, kvspec, krspec, gspec, lspec = _att_specs(L, n_ctx, 0, PAIRS)
    outs = pl.pallas_call(
        body_ctx, name=name + "_ctx", grid=(B, HP, 1),
        in_specs=[qspec, qspec, kvspec, krspec, gspec, _ANY, _ANY, _ANY], out_specs=[gspec, gspec, lspec], out_shape=shapes,
        input_output_aliases={5: 0, 6: 1, 7: 2},
        compiler_params=_params(("parallel", "parallel", "parallel"), VMEM_LIMIT),
    )(qn, qr, kv, kr, zg, *main[:3])
    return (*outs, list(main[3:]))


def attention_bwd(qn, qr, kv, kr, zg, ya, lse, dym, n_ctx, name, hosted=None):
    B, L, _ = qn.shape
    TQ = TOKEN_BLOCK
    PAIRS = 2
    HP = MLA_HEADS // 2 // PAIRS
    W = MLA_HEADS * LANES
    shapes = [jax.ShapeDtypeStruct((B, L, W), BF16), jax.ShapeDtypeStruct((B, L, W), BF16), jax.ShapeDtypeStruct((B, L, W), F32),
              jax.ShapeDtypeStruct((B, L, LANES), F32), jax.ShapeDtypeStruct((B, L, MLA_WIDTH), BF16)]

    nx = hosted.n if hosted else 0
    NQ = L // TQ - 1

    def body(*refs):
        if hosted:
            step = (pl.program_id(0) * HP + pl.program_id(1)) * NQ + pl.program_id(2)
            hosted.run(refs[8:8 + nx], refs[13 + nx:13 + 2 * nx], refs[13 + 2 * nx:], step, B * HP * NQ)
        dkv_ref, dkr_ref = refs[10 + nx], refs[11 + nx]

        @pl.when(pl.program_id(2) == 0)
        def _():
            dkv_ref[...] = jnp.zeros_like(dkv_ref)

        @pl.when(jnp.logical_and(pl.program_id(2) == 0, pl.program_id(1) == 0))
        def _():
            dkr_ref[...] = jnp.zeros_like(dkr_ref)

        _bwd_step(*refs[:8], *refs[8 + nx:13 + nx])

    def body_ctx(qn_ref, qr_ref, kv_ref, kr_ref, g_ref, ya_ref, lse_ref, dy_ref, dkv_in, dkr_in, a0, a1, a2,
                 dqn_ref, dqr_ref, dkv_ref, dkr_ref, dzg_ref):
        dkv_ref[...] = dkv_in[...]

        @pl.when(pl.program_id(1) == 0)
        def _():
            dkr_ref[...] = dkr_in[...]

        _bwd_step(qn_ref, qr_ref, kv_ref, kr_ref, g_ref, ya_ref, lse_ref, dy_ref, dqn_ref, dqr_ref, dkv_ref, dkr_ref, dzg_ref)

    def _bwd_step(qn_ref, qr_ref, kv_ref, kr_ref, g_ref, ya_ref, lse_ref, dy_ref, dqn_ref, dqr_ref, dkv_ref, dkr_ref, dzg_ref):
        kr_v = kr_ref[0]
        for pr in range(PAIRS):
            psl = slice(LANES * pr, LANES * (pr + 1))
            silu, dsilu = _silu_and_grad(g_ref[0, :, psl])
            dy = dy_ref[0, :, psl]
            ya_v = ya_ref[0, :, psl]
            dya = dy * silu
            dzg_ref[0, :, psl] = (dy * ya_v * dsilu).astype(BF16)
            lane = lax.broadcasted_iota(jnp.int32, dya.shape, 1)
            hi = lane >= MLA_V
            d_out = [jnp.where(hi, pltpu.roll(dya, MLA_V, 1), 0.0), jnp.where(hi, dya, 0.0)]
            prod = dya * ya_v
            drow = [jnp.sum(jnp.where(hi, 0.0, prod), axis=-1, keepdims=True), jnp.sum(jnp.where(hi, prod, 0.0), axis=-1, keepdims=True)]
            lse_v = lse_ref[0, pr]
            for jj in range(2):
                j = 2 * pr + jj
                sl = slice(LANES * j, LANES * (j + 1))
                q, k, kvj, s = _att_qk(qn_ref, qr_ref, kv_ref, kr_v, j)
                pn = jnp.exp(s - lse_v[:, MLA_V * jj:MLA_V * jj + 1])
                dob = d_out[jj].astype(BF16)
                ds = (pn * (_dot_nt(dob, kvj) - drow[jj])).astype(BF16)
                dq = _dot(ds, k)
                dqn_ref[0, :, sl] = jnp.where(hi, 0.0, dq[:, :LANES]).astype(BF16)
                dqr_ref[0, :, sl] = dq[:, LANES:].astype(BF16)
                dk = _dot_tn(ds, q)
                dkv_ref[0, :, sl] += dk[:, :LANES] + _dot_tn(pn.astype(BF16), dob)
                dkr_ref[0] += dk[:, LANES:]

    sem = _params(("parallel", "arbitrary", "arbitrary"), VMEM_LIMIT)
    qspec, kvspec, krspec, gspec, lspec = _att_specs(L, L, 1, PAIRS)
    main = pl.pallas_call(
        body, name=name, grid=(B, HP, NQ),
        in_specs=[qspec, qspec, kvspec, krspec, gspec, gspec, lspec, gspec] + (hosted.in_specs if hosted else []),
        out_specs=[qspec, qspec, kvspec, krspec, gspec] + (hosted.out_specs if hosted else []),
        out_shape=shapes + (hosted.out_shape if hosted else []), scratch_shapes=hosted.scratch if hosted else [],
        compiler_params=_params(("arbitrary",) * 3, VMEM_LIMIT) if hosted else sem,
    )(qn, qr, kv, kr, zg, ya, lse, dym, *(hosted.xs if hosted else []))
    qspec, kvspec, krspec, gspec, lspec = _att_specs(L, n_ctx, 0, PAIRS)
    outs = pl.pallas_call(
        body_ctx, name=name + "_ctx", grid=(B, HP, 1),
        in_specs=[qspec, qspec, kvspec, krspec, gspec, gspec, lspec, gspec, kvspec, krspec, _ANY, _ANY, _ANY],
        out_specs=[qspec, qspec, kvspec, krspec, gspec], out_shape=shapes,
        input_output_aliases={8: 2, 9: 3, 10: 0, 11: 1, 12: 4}, compiler_params=sem,
    )(qn, qr, kv, kr, zg, ya, lse, dym, main[2], main[3], main[0], main[1], main[4])
    return (*outs, list(main[5:]))


def _seg_bounds(rows, n_ctx, L):
    in_ctx = rows < n_ctx
    return jnp.where(in_ctx, 0, n_ctx), jnp.where(in_ctx, n_ctx, L)


def _window_sum(u, w, rows, lo, hi, mirror):
    L = u.shape[0]
    offs = range(-w // 2 + 1, w // 2 + 1) if mirror else range(-w // 2, w // 2)
    acc = None
    for d in offs:
        if d == 0:
            t = u
        else:
            src = rows + d
            t = jnp.where(jnp.logical_and(src >= lo, src < hi), pltpu.roll(u, (-d) % L, 0), 0.0)
        acc = t if acc is None else acc + t
    return acc


def _window_count(w, rows, lo, hi):
    pos = rows - lo
    return (jnp.minimum(pos + w // 2, hi - lo) - jnp.maximum(pos - w // 2, 0)).astype(F32)


def pool_fwd(px, pg, pw, ps, n_ctx, name):
    B, L, _ = px.shape

    def body(px_ref, pg_ref, pw_ref, ps_ref, y_ref):
        rows = lax.broadcasted_iota(jnp.int32, (L, POOL_GROUP), 0)
        lo, hi = _seg_bounds(rows, n_ctx, L)
        for gi, w in enumerate(POOL_WINDOWS):
            sl = slice(POOL_GROUP * gi, POOL_GROUP * (gi + 1))
            u = px_ref[0, :, sl]
            pooled = _window_sum(u, w, rows, lo, hi, False) / _window_count(w, rows, lo, hi) - u
            mixed = _dot(pooled.astype(BF16), pw_ref[gi])
            g = pg_ref[0, :, sl]
            y_ref[0, :, sl] = (mixed * ps_ref[:, sl] * (g * _sig(g))).astype(BF16)

    tok = pl.BlockSpec((1, L, POOL_WIDTH), lambda b: (b, 0, 0))
    return pl.pallas_call(
        body, name=name, grid=(B,),
        in_specs=[tok, tok, _full(pw.shape), _full(ps.shape)],
        out_specs=tok, out_shape=jax.ShapeDtypeStruct((B, L, POOL_WIDTH), BF16),
        compiler_params=_params(("parallel",), VMEM_LIMIT),
    )(px, pg, pw, ps)


def pool_bwd(px, pg, pw, ps, dy, n_ctx, name):
    B, L, _ = px.shape

    def body(px_ref, pg_ref, pw_ref, ps_ref, dy_ref, dpx_ref, dpg_ref, dpw_ref, dps_ref):
        first = pl.program_id(0) == 0
        rows = lax.broadcasted_iota(jnp.int32, (L, POOL_GROUP), 0)
        lo, hi = _seg_bounds(rows, n_ctx, L)
        for gi, w in enumerate(POOL_WINDOWS):
            sl = slice(POOL_GROUP * gi, POOL_GROUP * (gi + 1))
            u = px_ref[0, :, sl]
            cnt = _window_count(w, rows, lo, hi)
            pooled = (_window_sum(u, w, rows, lo, hi, False) / cnt - u).astype(BF16)
            mixed = _dot(pooled, pw_ref[gi])
            silu, dsilu = _silu_and_grad(pg_ref[0, :, sl])
            sc = ps_ref[:, sl]
            dyv = dy_ref[0, :, sl]
            _acc(dps_ref.at[:, sl], jnp.sum(dyv * mixed * silu, axis=0, keepdims=True), first)
            dpg_ref[0, :, sl] = (dyv * mixed * sc * dsilu).astype(BF16)
            dmixed = (dyv * sc * silu).astype(BF16)
            _acc(dpw_ref.at[gi], _dot_tn(pooled, dmixed), first)
            dpooled = _dot_nt(dmixed, pw_ref[gi])
            dpx_ref[0, :, sl] = (_window_sum(dpooled / cnt, w, rows, lo, hi, True) - dpooled).astype(BF16)

    tok = pl.BlockSpec((1, L, POOL_WIDTH), lambda b: (b, 0, 0))
    return pl.pallas_call(
        body, name=name, grid=(B,),
        in_specs=[tok, tok, _full(pw.shape), _full(ps.shape), tok],
        out_specs=[tok, tok, _full(pw.shape), _full(ps.shape)],
        out_shape=[jax.ShapeDtypeStruct((B, L, POOL_WIDTH), BF16)] * 2 + [jax.ShapeDtypeStruct(pw.shape, F32), jax.ShapeDtypeStruct(ps.shape, F32)],
        compiler_params=_params(("arbitrary",), VMEM_LIMIT),
    )(px, pg, pw, ps, dy)


_SCAN_STEPS = (1, 2, 4, 8, 16, 32)
SCAN_CHUNKS = 4


def _cum_fwd(x, r):
    for s in _SCAN_STEPS:
        x = x + jnp.where(r >= s, pltpu.roll(x, s, 0), 0.0)
    return x


def _cum_bwd(x, r):
    n = x.shape[0]
    for s in _SCAN_STEPS:
        x = x + jnp.where(r + s < GLA_CHUNK, pltpu.roll(x, n - s, 0), 0.0)
    return x


def _log_sigmoid(x):
    return jnp.minimum(x, 0.0) - jnp.log(1.0 + jnp.exp(-jnp.abs(x)))


def _gla_decays(lr, w_ref, b_ref, r, reverse):
    pre = _dot(lr, w_ref[...]) + b_ref[...]
    a = _log_sigmoid(pre) / GLA_TAU
    return pre, a, (_cum_bwd(a, r) if reverse else _cum_fwd(a, r)), _chunk_total(a)


def _chunk_total(x):
    x3 = x.reshape(x.shape[0] // GLA_CHUNK, GLA_CHUNK, x.shape[1])
    return jnp.broadcast_to(jnp.sum(x3, axis=1, keepdims=True), x3.shape).reshape(x.shape)


def gla_prep_fwd(zlr, zq, zk, waf, wab, baf, bab, name):
    T = zlr.shape[0]
    tb = _wide_block(T)

    def body(lr_ref, q_ref, k_ref, waf_ref, wab_ref, baf_ref, bab_ref, qf_ref, kf_ref, ksf_ref, tf_ref, qb_ref, kb_ref, ksb_ref, tb_ref):
        r = lax.broadcasted_iota(jnp.int32, (tb, GLA_KW), 0) % GLA_CHUNK
        lr = lr_ref[...].astype(BF16)
        q = q_ref[...] * GLA_DK ** -0.5
        k = k_ref[...]
        for rev, w_ref, b_ref, qo, ko, kso, to in ((False, waf_ref, baf_ref, qf_ref, kf_ref, ksf_ref, tf_ref),
                                                   (True, wab_ref, bab_ref, qb_ref, kb_ref, ksb_ref, tb_ref)):
            _, _, b, tot = _gla_decays(lr, w_ref, b_ref, r, rev)
            qo[...] = (q * jnp.exp(b)).astype(BF16)
            ko[...] = (k * jnp.exp(-b)).astype(BF16)
            kso[...] = (k * jnp.exp(tot - b)).astype(BF16)
            to[...] = tot

    tok = lambda n: pl.BlockSpec((tb, n), lambda i: (i, 0))
    outs = [jax.ShapeDtypeStruct((T, GLA_KW), BF16)] * 3 + [jax.ShapeDtypeStruct((T, GLA_KW), F32)]
    return pl.pallas_call(
        body, name=name, grid=(T // tb,),
        in_specs=[tok(LANES), tok(GLA_KW), tok(GLA_KW), _full(waf.shape), _full(wab.shape), _full(baf.shape), _full(bab.shape)],
        out_specs=[tok(GLA_KW)] * 8, out_shape=outs + outs,
        compiler_params=_params(("parallel",)),
    )(zlr, zq, zk, waf, wab, baf, bab)


def gla_prep_bwd(zlr, zq, zk, waf, wab, baf, bab, gf, gb, dvs, name):
    T = zlr.shape[0]
    tb = _wide_block(T)

    def body(lr_ref, q_ref, k_ref, waf_ref, wab_ref, baf_ref, bab_ref, dqf, dkf, dksf, ddf, dqb, dkb, dksb, ddb, dvf, dvb,
             dlr_ref, dq_ref, dk_ref, dwaf_ref, dwab_ref, dbaf_ref, dbab_ref, dv_ref):
        first = pl.program_id(0) == 0
        dv_ref[...] = (dvf[...] + dvb[...]).astype(BF16)
        r = lax.broadcasted_iota(jnp.int32, (tb, GLA_KW), 0) % GLA_CHUNK
        lr = lr_ref[...].astype(BF16)
        q = q_ref[...] * GLA_DK ** -0.5
        k = k_ref[...]
        dq_tot = None
        dk_tot = None
        dlr = None
        for rev, w_ref, b_ref, dqt, dkt, dks, ddec, dw_ref, db_ref in (
                (False, waf_ref, baf_ref, dqf, dkf, dksf, ddf, dwaf_ref, dbaf_ref),
                (True, wab_ref, bab_ref, dqb, dkb, dksb, ddb, dwab_ref, dbab_ref)):
            pre, _, b, tot = _gla_decays(lr, w_ref, b_ref, r, rev)
            e1 = jnp.exp(b)
            e2 = jnp.exp(-b)
            e3 = jnp.exp(tot - b)
            dqt_v = dqt[...]
            dkt_v = dkt[...]
            dks_v = dks[...]
            dq = dqt_v * e1
            dk = dkt_v * e2 + dks_v * e3
            g3 = dks_v * (k * e3)
            d_b = dqt_v * (q * e1) - dkt_v * (k * e2) - g3
            d_tot = _chunk_total(g3) + ddec[...] * jnp.exp(tot)
            da = (_cum_fwd(d_b, r) if rev else _cum_bwd(d_b, r)) + d_tot
            dpre = (da * (_sig(-pre) / GLA_TAU)).astype(BF16)
            t = _dot_nt(dpre, w_ref[...])
            dlr = t if dlr is None else dlr + t
            _acc(dw_ref, _dot_tn(lr, dpre), first)
            _acc(db_ref, jnp.sum(dpre.astype(F32), axis=0, keepdims=True), first)
            dq_tot = dq if dq_tot is None else dq_tot + dq
            dk_tot = dk if dk_tot is None else dk_tot + dk
        dlr_ref[...] = dlr.astype(BF16)
        dq_ref[...] = (dq_tot * GLA_DK ** -0.5).astype(BF16)
        dk_ref[...] = dk_tot.astype(BF16)

    tok = lambda n: pl.BlockSpec((tb, n), lambda i: (i, 0))
    return pl.pallas_call(
        body, name=name, grid=(T // tb,),
        in_specs=[tok(LANES), tok(GLA_KW), tok(GLA_KW), _full(waf.shape), _full(wab.shape), _full(baf.shape), _full(bab.shape)] + [tok(GLA_KW)] * 8
        + [tok(GLA_WIDTH)] * 2,
        out_specs=[tok(LANES), tok(GLA_KW), tok(GLA_KW), _full(waf.shape), _full(wab.shape), _full(baf.shape), _full(bab.shape), tok(GLA_WIDTH)],
        out_shape=[jax.ShapeDtypeStruct((T, LANES), BF16), jax.ShapeDtypeStruct((T, GLA_KW), BF16), jax.ShapeDtypeStruct((T, GLA_KW), BF16),
                   jax.ShapeDtypeStruct(waf.shape, F32), jax.ShapeDtypeStruct(wab.shape, F32), jax.ShapeDtypeStruct(baf.shape, F32),
                   jax.ShapeDtypeStruct(bab.shape, F32), jax.ShapeDtypeStruct((T, GLA_WIDTH), BF16)],
        compiler_params=_params(("arbitrary",)),
    )(zlr, zq, zk, waf, wab, baf, bab, *gf, *gb, *dvs)


def _chunk_order(nc, n_ctx_chunks, reverse):
    if not reverse:
        return lambda c: c
    return lambda c: jnp.where(c < n_ctx_chunks, n_ctx_chunks - 1 - c, nc + n_ctx_chunks - 1 - c)


def _tri_mask4(reverse):
    ri = lax.broadcasted_iota(jnp.int32, (GLA_CHUNK, GLA_HEADS * GLA_CHUNK), 0)
    ci = lax.broadcasted_iota(jnp.int32, (GLA_CHUNK, GLA_HEADS * GLA_CHUNK), 1) % GLA_CHUNK
    return (ri <= ci) if reverse else (ri >= ci)


def _block_diag(x, rb, cb):
    x4 = jnp.concatenate([x] * GLA_HEADS, axis=0)
    r = lax.broadcasted_iota(jnp.int32, x4.shape, 0) // rb
    c = lax.broadcasted_iota(jnp.int32, x4.shape, 1) // cb
    return jnp.where(r == c, x4, jnp.zeros_like(x4))


def _diag_blocks(f, rb, cb):
    c = lax.broadcasted_iota(jnp.int32, (rb, GLA_HEADS * cb), 1) // cb
    out = None
    for h in range(GLA_HEADS):
        t = jnp.where(c == h, f[rb * h:rb * (h + 1)], 0.0)
        out = t if out is None else out + t
    return out


def gla_scan_fwd(dirs, v, n_ctx, name):
    B, L, _ = v.shape
    C, G = GLA_CHUNK, SCAN_CHUNKS
    nc = L // C
    orders = [_chunk_order(nc // G, n_ctx // C // G, rev) for rev in (False, True)]

    def body(qf, kf, ksf, tf, vf, qb, kb, ksb, tb, vb, of, ssf, ob, ssb, stf, stb):
        @pl.when(pl.program_id(1) == 0)
        def _():
            stf[...] = jnp.zeros_like(stf)
            stb[...] = jnp.zeros_like(stb)

        for sub in range(G):
            step(qf, kf, ksf, vf, tf, of, ssf, stf, False, sub)
            step(qb, kb, ksb, vb, tb, ob, ssb, stb, True, G - 1 - sub)

    def step(q_ref, k_ref, ks_ref, v_ref, tot_ref, o_ref, ss_ref, st, reverse, sub):
        rows = slice(C * sub, C * (sub + 1))
        S = st[...]
        ss_ref[0, sub] = S
        q = q_ref[0, rows]
        v = v_ref[0, rows]
        k4 = _block_diag(k_ref[0, rows], GLA_CHUNK, GLA_DK)
        v4 = _block_diag(v.astype(BF16), GLA_CHUNK, GLA_DV)
        s4 = _block_diag(S.astype(BF16), GLA_DV, GLA_DK)
        P = jnp.where(_tri_mask4(reverse), _dot_nt(q, k4), 0.0)
        o_ref[0, rows] = _dot(P.astype(BF16), v4) + _dot_nt(q, s4)
        st[...] = jnp.exp(tot_ref[0, C * sub:C * sub + 1, :]) * S + _diag_blocks(_dot(v.T.astype(BF16), ks_ref[0, rows]), GLA_DV, GLA_DK)

    in_specs, out_specs, out_shape = [], [], []
    for order in orders:
        tok = lambda n, order=order: pl.BlockSpec((1, G * C, n), lambda b, c: (b, order(c), 0))
        in_specs += [tok(GLA_KW), tok(GLA_KW), tok(GLA_KW), tok(GLA_KW), tok(GLA_WIDTH)]
        out_specs += [tok(GLA_WIDTH), pl.BlockSpec((1, G, GLA_DV, GLA_KW), lambda b, c, order=order: (b, order(c), 0, 0))]
        out_shape += [jax.ShapeDtypeStruct((B, L, GLA_WIDTH), F32), jax.ShapeDtypeStruct((B, nc, GLA_DV, GLA_KW), F32)]
    outs = pl.pallas_call(
        body, name=name, grid=(B, nc // G), in_specs=in_specs, out_specs=out_specs, out_shape=out_shape,
        scratch_shapes=[pltpu.VMEM((GLA_DV, GLA_KW), F32)] * 2,
        compiler_params=_params(("parallel", "arbitrary")),
    )(*dirs[0], v, *dirs[1], v)
    return outs[:2], outs[2:]


def gla_scan_bwd(dirs, v, do, n_ctx, name, hosted=None):
    B, L, _ = v.shape
    C, G = GLA_CHUNK, SCAN_CHUNKS
    nc = L // C
    npair = nc // G
    orders = []
    for rev in (False, True):
        fwd_order = _chunk_order(npair, n_ctx // C // G, rev)
        orders.append(lambda c, fwd_order=fwd_order: fwd_order(npair - 1 - c))

    nx = hosted.n if hosted else 0

    def body(*refs):
        qf, kf, ksf, tf, ssf, vf, dof, qb, kb, ksb, tb, ssb, vb, dob = refs[:14]
        dqf, dkf, dksf, dvf, ddf, dqb, dkb, dksb, dvb, ddb = refs[14 + nx:24 + nx]
        dstf, dstb = refs[24 + 2 * nx:26 + 2 * nx]
        if hosted:
            hosted.run(refs[14:14 + nx], refs[24 + nx:24 + 2 * nx], refs[26 + 2 * nx:], pl.program_id(0) * npair + pl.program_id(1), B * npair)

        @pl.when(pl.program_id(1) == 0)
        def _():
            dstf[...] = jnp.zeros_like(dstf)
            dstb[...] = jnp.zeros_like(dstb)

        for sub in range(G):
            step(qf, kf, ksf, vf, tf, ssf, dof, dqf, dkf, dksf, dvf, ddf, dstf, False, G - 1 - sub)
            step(qb, kb, ksb, vb, tb, ssb, dob, dqb, dkb, dksb, dvb, ddb, dstb, True, sub)

    def step(q_ref, k_ref, ks_ref, v_ref, tot_ref, ss_ref, do_ref, dq_ref, dk_ref, dks_ref, dv_ref, dd_ref, dst, reverse, sub):
        rows = slice(C * sub, C * (sub + 1))
        dSn = dst[...]
        S = ss_ref[0, sub]
        q = q_ref[0, rows]
        vb = v_ref[0, rows].astype(BF16)
        dob = do_ref[0, rows].astype(BF16)
        k4 = _block_diag(k_ref[0, rows], GLA_CHUNK, GLA_DK)
        v4 = _block_diag(vb, GLA_CHUNK, GLA_DV)
        s4 = _block_diag(S.astype(BF16), GLA_DV, GLA_DK)
        ds4 = _block_diag(dSn.astype(BF16), GLA_DV, GLA_DK)
        tri = _tri_mask4(reverse)
        P = jnp.where(tri, _dot_nt(q, k4), 0.0).astype(BF16)
        dP = jnp.where(tri, _dot_nt(dob, v4), 0.0).astype(BF16)
        dq_ref[0, rows] = _dot(dob, s4) + _dot(dP, k4)
        dk_ref[0, rows] = _diag_blocks(_dot_tn(dP, q), GLA_CHUNK, GLA_DK)
        dv_ref[0, rows] = _diag_blocks(_dot_tn(P, dob), GLA_CHUNK, GLA_DV) + _dot_nt(ks_ref[0, rows], ds4)
        dks_ref[0, rows] = _dot(vb, ds4)
        dd_ref[0, rows] = jnp.broadcast_to(jnp.sum(dSn * S, axis=0, keepdims=True), (C, GLA_KW))
        dst[...] = jnp.exp(tot_ref[0, C * sub:C * sub + 1, :]) * dSn + _diag_blocks(_dot_tn(dob, q), GLA_DV, GLA_DK)

    in_specs, out_specs, out_shape = [], [], []
    for order in orders:
        tok = lambda n, order=order: pl.BlockSpec((1, G * C, n), lambda b, c: (b, order(c), 0))
        in_specs += [tok(GLA_KW), tok(GLA_KW), tok(GLA_KW), tok(GLA_KW),
                     pl.BlockSpec((1, G, GLA_DV, GLA_KW), lambda b, c, order=order: (b, order(c), 0, 0)), tok(GLA_WIDTH), tok(GLA_WIDTH)]
        out_specs += [tok(GLA_KW), tok(GLA_KW), tok(GLA_KW), tok(GLA_WIDTH), tok(GLA_KW)]
        out_shape += [jax.ShapeDtypeStruct((B, L, GLA_KW), F32)] * 3 + [jax.ShapeDtypeStruct((B, L, GLA_WIDTH), F32), jax.ShapeDtypeStruct((B, L, GLA_KW), F32)]
    outs = pl.pallas_call(
        body, name=name, grid=(B, npair), in_specs=in_specs + (hosted.in_specs if hosted else []),
        out_specs=out_specs + (hosted.out_specs if hosted else []), out_shape=out_shape + (hosted.out_shape if hosted else []),
        scratch_shapes=[pltpu.VMEM((GLA_DV, GLA_KW), F32)] * 2 + (hosted.scratch if hosted else []),
        compiler_params=_params(("arbitrary", "arbitrary") if hosted else ("parallel", "arbitrary")),
    )(*dirs[0], v, do, *dirs[1], v, do, *(hosted.xs if hosted else []))
    return outs[:5], outs[5:10], list(outs[10:])


def gla_out_fwd(of, ob, gn, zg, name):
    T = of.shape[0]

    def body(of_ref, ob_ref, gn_ref, g_ref, y_ref):
        for h in range(GLA_HEADS):
            sl = slice(GLA_DV * h, GLA_DV * (h + 1))
            xn, _ = _rms_rows(of_ref[:, sl] + ob_ref[:, sl])
            g = g_ref[:, sl]
            y_ref[:, sl] = (xn * gn_ref[...] * (g * _sig(g))).astype(BF16)

    tb = _wide_block(T)
    tok = pl.BlockSpec((tb, GLA_WIDTH), lambda i: (i, 0))
    return pl.pallas_call(
        body, name=name, grid=(T // tb,),
        in_specs=[tok, tok, _full(gn.shape), tok], out_specs=tok,
        out_shape=jax.ShapeDtypeStruct((T, GLA_WIDTH), BF16),
        compiler_params=_params(("parallel",)),
    )(of, ob, gn, zg)


def gla_out_bwd(of, ob, gn, zg, dy, name):
    T = of.shape[0]

    def body(of_ref, ob_ref, gn_ref, g_ref, dy_ref, do_ref, dzg_ref, dgn_ref):
        first = pl.program_id(0) == 0
        gn_v = gn_ref[...]
        dgn = None
        for h in range(GLA_HEADS):
            sl = slice(GLA_DV * h, GLA_DV * (h + 1))
            xn, r = _rms_rows(of_ref[:, sl] + ob_ref[:, sl])
            silu, dsilu = _silu_and_grad(g_ref[:, sl])
            dyv = dy_ref[:, sl]
            dzg_ref[:, sl] = (dyv * xn * gn_v * dsilu).astype(BF16)
            dn = dyv * silu
            t = jnp.sum(dn * xn, axis=0, keepdims=True)
            dgn = t if dgn is None else dgn + t
            do_ref[:, sl] = _rms_rows_bwd(dn * gn_v, xn, r)
        _acc(dgn_ref, dgn, first)

    tb = _wide_block(T)
    tok = pl.BlockSpec((tb, GLA_WIDTH), lambda i: (i, 0))
    return pl.pallas_call(
        body, name=name, grid=(T // tb,),
        in_specs=[tok, tok, _full(gn.shape), tok, tok], out_specs=[tok, tok, _full(gn.shape)],
        out_shape=[jax.ShapeDtypeStruct((T, GLA_WIDTH), F32), jax.ShapeDtypeStruct((T, GLA_WIDTH), BF16), jax.ShapeDtypeStruct(gn.shape, F32)],
        compiler_params=_params(("arbitrary",)),
    )(of, ob, gn, zg, dy)


def merge_post_fwd(ys, zm, wbs, wo, x2, pg, ms, nb, name):
    T = x2.shape[0]

    def body(y0, y1, y2, zm_ref, w0, w1, w2, wo_ref, x_ref, pg_ref, gate_ref, xn_ref, out_ref, mg_ref):
        merged = None
        for i, (y_ref, w_ref) in enumerate(((y0, w0), (y1, w1), (y2, w2))):
            t = _sig(zm_ref[:, D_MODEL * i:D_MODEL * (i + 1)].astype(F32)) * _dot(y_ref[...], w_ref[...])
            merged = t if merged is None else merged + t
        mb = merged.astype(BF16)
        mg_ref[...] = mb
        out = _dot(mb, wo_ref[...])
        out_ref[...] = out
        on, _ = _rms_rows(out)
        xn_ref[...] = x_ref[...] + gate_ref[0] * (on * pg_ref[...])

    tok = lambda n: pl.BlockSpec((TOKEN_BLOCK, n), lambda i: (i, 0))
    return pl.pallas_call(
        body, name=name, grid=(T // TOKEN_BLOCK,),
        in_specs=[tok(512)] * 3 + [tok(3 * D_MODEL)] + [_full(w.shape) for w in wbs] + [_full(wo.shape), tok(D_MODEL), _full(pg.shape), _mod_spec(nb, 2)],
        out_specs=[tok(D_MODEL)] * 3,
        out_shape=[jax.ShapeDtypeStruct((T, D_MODEL), F32), jax.ShapeDtypeStruct((T, D_MODEL), F32), jax.ShapeDtypeStruct((T, D_MODEL), BF16)],
        compiler_params=_params(("parallel",), VMEM_LIMIT),
    )(*ys, zm, *wbs, wo, x2, pg, ms)


def merge_post_bwd(dxn, out, ys, zm, wbs, wo, pg, ms, nb, name):
    T = dxn.shape[0]
    nrow = ms.shape[0]
    row = _mod_row(nb)

    def body(dxn_ref, out_ref, y0, y1, y2, zm_ref, w0, w1, w2, wo_ref, pg_ref, gate_ref,
             dy0, dy1, dy2, dzm_ref, dout_ref, dp0, dp1, dp2, dgate_ref, dpg_ref):
        i = pl.program_id(0)
        dxn_v = dxn_ref[...]
        on, r = _rms_rows(out_ref[...])
        pg_v = pg_ref[...]
        _acc(dgate_ref.at[0], jnp.sum(dxn_v * on * pg_v, axis=0, keepdims=True), (i % nb) <= 1)
        dn = dxn_v * gate_ref[0]
        _acc(dpg_ref, jnp.sum(dn * on, axis=0, keepdims=True), i == 0)
        dout = _rms_rows_bwd(dn * pg_v, on, r).astype(BF16)
        dout_ref[...] = dout
        dmerged = _dot_nt(dout, wo_ref[...])
        for j, (y_ref, w_ref, dy_ref, dp_ref) in enumerate(((y0, w0, dy0, dp0), (y1, w1, dy1, dp1), (y2, w2, dy2, dp2))):
            sl = slice(D_MODEL * j, D_MODEL * (j + 1))
            g = _sig(zm_ref[:, sl].astype(F32))
            p = _dot(y_ref[...], w_ref[...])
            dzm_ref[:, sl] = (dmerged * p * g * (1.0 - g)).astype(BF16)
            dp = (dmerged * g).astype(BF16)
            dp_ref[...] = dp
            dy_ref[...] = _dot_nt(dp, w_ref[...])

    tok = lambda n: pl.BlockSpec((TOKEN_BLOCK, n), lambda i: (i, 0))
    return pl.pallas_call(
        body, name=name, grid=(T // TOKEN_BLOCK,),
        in_specs=[tok(D_MODEL), tok(D_MODEL)] + [tok(512)] * 3 + [tok(3 * D_MODEL)] + [_full(w.shape) for w in wbs] + [_full(wo.shape), _full(pg.shape), _mod_spec(nb, 2)],
        out_specs=[tok(512)] * 3 + [tok(3 * D_MODEL), tok(D_MODEL)] + [tok(D_MODEL)] * 3 + [pl.BlockSpec((1, 1, D_MODEL), lambda i: (row(i), 0, 0)), _full(pg.shape)],
        out_shape=[jax.ShapeDtypeStruct((T, 512), F32)] * 3 + [jax.ShapeDtypeStruct((T, 3 * D_MODEL), BF16), jax.ShapeDtypeStruct((T, D_MODEL), BF16)]
        + [jax.ShapeDtypeStruct((T, D_MODEL), BF16)] * 3 + [jax.ShapeDtypeStruct((nrow, 1, D_MODEL), F32), jax.ShapeDtypeStruct(pg.shape, F32)],
        compiler_params=_params(("arbitrary",), VMEM_LIMIT),
    )(dxn, out, *ys, zm, *wbs, wo, pg, ms)


def loss_head(y2, tgt2, nb, name):
    T = y2.shape[0]
    nlat = nb - 1

    def body(y_ref, t_ref, dy_ref, loss_ref, acc):
        i = pl.program_id(0)
        is_lat = (i % nb) > 0

        @pl.when(i == 0)
        def _():
            acc[...] = jnp.zeros_like(acc)

        @pl.when(is_lat)
        def _():
            e = y_ref[...] - t_ref[...]
            dy_ref[...] = e * (1.0 / D_MODEL)
            acc[...] += jnp.sum(e * e, axis=0, keepdims=True)

        @pl.when(jnp.logical_not(is_lat))
        def _():
            dy_ref[...] = jnp.zeros_like(dy_ref)

        @pl.when(i == pl.num_programs(0) - 1)
        def _():
            loss_ref[...] = jnp.sum(acc[...], axis=1, keepdims=True) * (0.5 / D_MODEL)

    tok = pl.BlockSpec((TOKEN_BLOCK, D_MODEL), lambda i: (i, 0))
    tgt = pl.BlockSpec((TOKEN_BLOCK, D_MODEL), lambda i: ((i // nb) * nlat + jnp.maximum(i % nb - 1, 0), 0))
    return pl.pallas_call(
        body, name=name, grid=(T // TOKEN_BLOCK,),
        in_specs=[tok, tgt], out_specs=[tok, _full((1, 1))],
        out_shape=[jax.ShapeDtypeStruct((T, D_MODEL), F32), jax.ShapeDtypeStruct((1, 1), F32)],
        scratch_shapes=[pltpu.VMEM((1, D_MODEL), F32)],
        compiler_params=_params(("arbitrary",)),
    )(y2, tgt2)


_IN_OFFS = tuple(int(o) for o in np.cumsum((0,) + IN_SIZES))
_IN_GROUPS = (("a", 0, 416, 512), ("mg", 416, 512, 512), ("px", 928, 512, 512), ("pg", 1440, 512, 512), ("gq", 1952, 256, 256),
              ("gk", 2208, 256, 256), ("gv", 2464, 512, 512), ("lr", 2976, 32, 128), ("gg", 3008, 512, 512), ("m", 3520, 3072, 3072))


def _pad_cols(w, n):
    return w if w.shape[1] == n else jnp.pad(w, ((0, 0), (0, n - w.shape[1])))


def layer_weights(w_in, w_uq, w_ukv, af_w2, ab_w2, wbm, wbp, wbg, w_out):
    W = {}
    for nm, off, n, npad in _IN_GROUPS:
        W["in_" + nm] = _pad_cols(w_in[:, off:off + n], npad)
    uq = w_uq.reshape(MLA_Q_RANK, MLA_HEADS, MLA_NOPE + MLA_ROPE)
    W["qn"] = jnp.pad(uq[:, :, :MLA_NOPE], ((0, 0), (0, 0), (0, LANES - MLA_NOPE))).reshape(MLA_Q_RANK, MLA_HEADS * LANES)
    W["qr"] = jnp.pad(uq[:, :, MLA_NOPE:], ((0, 0), (0, 0), (0, LANES - MLA_ROPE))).reshape(MLA_Q_RANK, MLA_HEADS * LANES)
    W["kv"] = w_ukv
    W["af"] = jnp.pad(af_w2, ((0, LANES - GLA_GATE_RANK), (0, 0)))
    W["ab"] = jnp.pad(ab_w2, ((GLA_GATE_RANK, LANES - 2 * GLA_GATE_RANK), (0, 0)))
    W["bm"], W["bp"], W["bg"], W["out"] = wbm, wbp, wbg, w_out
    return W


def rope_tables(L, n_ctx):
    t = np.arange(L - n_ctx)
    half = MLA_ROPE // 2
    inv = ROPE_BASE ** (-np.arange(0, half, 2, dtype=np.float32) / half)
    ang_r = (t // GRID_W).astype(np.float32)[:, None] * inv
    ang_c = (t % GRID_W).astype(np.float32)[:, None] * inv
    ang = jnp.asarray(np.concatenate([ang_r, ang_r, ang_c, ang_c], axis=-1), F32)
    cos = jnp.ones((L, LANES), F32).at[n_ctx:, :MLA_ROPE].set(jnp.cos(ang))
    sin = jnp.zeros((L, LANES), F32).at[n_ctx:, :MLA_ROPE].set(jnp.sin(ang))
    return cos, sin


def layer_fwd(x2, ms, W, P, cos, sin, B, L, n_ctx, tag, hosted=None):
    nb = L // TOKEN_BLOCK
    r3 = lambda a: a.reshape(B, L, a.shape[-1])
    r2 = lambda a: a.reshape(B * L, a.shape[-1])
    names = [g[0] for g in _IN_GROUPS[:-1]]
    h, zs = norm_in_proj(x2, P["pre"], ms, [W["in_" + n] for n in names], nb, tag + "in_proj")
    z = dict(zip(names, zs))
    (z["m"],) = mm_multi(h, [W["in_m"]], [BF16], tag + "in_proj_merge")
    qn, qr, kv, kr = mla_prep_fwd(z["a"], P["qg"], P["kvg"], W["qn"], W["qr"], W["kv"], cos, sin, nb, tag + "mla_prep")
    ya, y_mla, lse, carried = attention_fwd(r3(qn), r3(qr), r3(kv), r3(kr), r3(z["mg"]), n_ctx, tag + "attention", hosted)
    y_pool = pool_fwd(r3(z["px"]), r3(z["pg"]), P["pw"], P["ps"], n_ctx, tag + "pool")
    qf, kf, ksf, tf, qb, kb, ksb, tb = gla_prep_fwd(z["lr"], z["gq"], z["gk"], W["af"], W["ab"], P["baf"], P["bab"], tag + "gla_prep")
    (of, ssf), (ob, ssb) = gla_scan_fwd([(r3(qf), r3(kf), r3(ksf), r3(tf)), (r3(qb), r3(kb), r3(ksb), r3(tb))], r3(z["gv"]), n_ctx, tag + "gla_scan")
    y_gla = gla_out_fwd(r2(of), r2(ob), P["gn"], z["gg"], tag + "gla_out")
    ys = [r2(y_mla), r2(y_pool), y_gla]
    x_new, out, merged = merge_post_fwd(ys, z["m"], [W["bm"], W["bp"], W["bg"]], W["out"], x2, P["post"], ms, nb, tag + "merge_post")
    res = dict(x2=x2, h=h, z=z, qn=qn, qr=qr, kv=kv, kr=kr, ya=ya, lse=lse, ys=ys, gla_f=(qf, kf, ksf, tf, ssf), gla_b=(qb, kb, ksb, tb, ssb),
               of=of, ob=ob, out=out, merged=merged)
    return x_new, res, carried


def layer_bwd(dxn, res, ms, W, P, cos, sin, B, L, n_ctx, tag, hosted=None, host_own=None):
    nb = L // TOKEN_BLOCK
    r3 = lambda a: a.reshape(B, L, a.shape[-1])
    r2 = lambda a: a.reshape(B * L, a.shape[-1])
    z = res["z"]
    ys = res["ys"]
    wbs = [W["bm"], W["bp"], W["bg"]]
    dy0, dy1, dy2, dzm, dout, dp0, dp1, dp2, dgate, dpost = merge_post_bwd(dxn, res["out"], ys, z["m"], wbs, W["out"], P["post"], ms, nb, tag + "merge_post_bwd")
    G = {"out": mm_dw(res["merged"], dout, tag + "dw_out"), "post": dpost}
    for nm, y, dp in zip(("bm", "bp", "bg"), ys, (dp0, dp1, dp2)):
        G[nm] = mm_dw(y, dp, tag + "dw_" + nm)
    g = {n: _natural_grad(G, n) for n in GRADS_EARLY}
    own = {}
    dz = {"m": dzm}
    do, dz["gg"], G["gn"] = gla_out_bwd(r2(res["of"]), r2(res["ob"]), P["gn"], z["gg"], dy2, tag + "gla_out_bwd")
    carrier = host_own(GRADS_EARLY, g) if host_own else None
    *grads, arrived = gla_scan_bwd([(r3(qt), r3(kt), r3(ks), r3(tot), ss) for qt, kt, ks, tot, ss in (res["gla_f"], res["gla_b"])],
                                   r3(z["gv"]), r3(do), n_ctx, tag + "gla_scan_bwd", carrier)
    own.update(zip(GRADS_EARLY, arrived))
    gf = [r2(a) for a in grads[0]]
    gb = [r2(a) for a in grads[1]]
    dz["lr"], dz["gq"], dz["gk"], G["af"], G["ab"], G["baf"], G["bab"], dz["gv"] = gla_prep_bwd(
        z["lr"], z["gq"], z["gk"], W["af"], W["ab"], P["baf"], P["bab"], gf[:3] + gf[4:], gb[:3] + gb[4:], [gf[3], gb[3]], tag + "gla_prep_bwd")
    dpx, dpg, G["pw"], G["ps"] = pool_bwd(r3(z["px"]), r3(z["pg"]), P["pw"], P["ps"], r3(dy1), n_ctx, tag + "pool_bwd")
    dz["px"], dz["pg"] = r2(dpx), r2(dpg)
    dqn, dqr, dkv, dkr, dzmg, got = attention_bwd(r3(res["qn"]), r3(res["qr"]), r3(res["kv"]), r3(res["kr"]), r3(z["mg"]), res["ya"], res["lse"],
                                                  r3(dy0), n_ctx, tag + "attention_bwd", hosted)
    dz["mg"] = r2(dzmg)
    dz["a"], G["qn"], G["qr"], G["kv"], G["qg"], G["kvg"] = mla_prep_bwd(
        r2(dqn), r2(dqr), r2(dkv), r2(dkr), z["a"], P["qg"], P["kvg"], W["qn"], W["qr"], W["kv"], cos, sin, nb, tag + "mla_prep_bwd")
    names = [grp[0] for grp in _IN_GROUPS]
    for n in names:
        G["in_" + n] = mm_dw(res["h"], dz[n], tag + "dw_in_" + n)
    g.update({n: _natural_grad(G, n) for n in GRADS_LATE})
    carrier = host_own(GRADS_LATE, g) if host_own else None
    (dx, dshift, dscale, G["pre"]), arrived = in_proj_norm_bwd([dz[n] for n in names], [W["in_" + n] for n in names], res["x2"], P["pre"], ms, dxn, nb,
                                                               tag + "in_proj_dx", carrier)
    own.update(zip(GRADS_LATE, arrived))
    g.update({n: _natural_grad(G, n) for n in GRADS_REPLICATED})
    dms = jnp.concatenate([dshift, dscale, dgate], axis=-1)
    return dx, g, dms, got, own


GRADS_EARLY = ("w_branch_mla", "w_branch_pool", "w_branch_gla", "w_out")
GRADS_LATE = ("w_in", "mla_w_uq", "mla_w_ukv", "gla_af_w2", "gla_ab_w2")
GRADS_REPLICATED = ("pre_norm", "post_norm", "mla_q_norm", "mla_kv_norm", "pool_w", "pool_scale", "gla_af_b", "gla_ab_b", "gla_norm")
_DIRECT = dict(mla_w_ukv="kv", w_branch_mla="bm", w_branch_pool="bp", w_branch_gla="bg", w_out="out", pool_w="pw")
_ROW = dict(pre_norm="pre", post_norm="post", mla_q_norm="qg", mla_kv_norm="kvg", pool_scale="ps", gla_af_b="baf", gla_ab_b="bab", gla_norm="gn")


def _natural_grad(G, name):
    if name == "w_in":
        parts = {off: G["in_" + nm][:, :n] for nm, off, n, npad in _IN_GROUPS}
        return jnp.concatenate([parts[o] for o in sorted(parts)], axis=1)
    if name == "mla_w_uq":
        gqn = G["qn"].reshape(MLA_Q_RANK, MLA_HEADS, LANES)[:, :, :MLA_NOPE]
        gqr = G["qr"].reshape(MLA_Q_RANK, MLA_HEADS, LANES)[:, :, :MLA_ROPE]
        return jnp.concatenate([gqn, gqr], axis=-1).reshape(MLA_Q_RANK, MLA_HEADS * (MLA_NOPE + MLA_ROPE))
    if name == "gla_af_w2":
        return G["af"][:GLA_GATE_RANK]
    if name == "gla_ab_w2":
        return G["ab"][GLA_GATE_RANK:2 * GLA_GATE_RANK]
    return G[_DIRECT[name]] if name in _DIRECT else G[_ROW[name]][0]


def local_step(x, c, ctx, c_ctx, small, loss_target, depth, layer_full, host_fwd=None, host_bwd=None, host_own=None):
    B, S, _ = x.shape
    n_ctx = ctx.shape[1]
    L = n_ctx + S
    nb = L // TOKEN_BLOCK
    cos, sin = rope_tables(L, n_ctx)
    x2 = jnp.concatenate([ctx, x], axis=1).reshape(B * L, D_MODEL)
    a8 = jnp.zeros((8, D_MODEL), F32).at[:B].set(c).at[B].set(c_ctx)
    Ws, Ps, mss, ress, mod_ws = [], [], [], [], []
    carried = None
    for l in range(depth):
        tag = f"l{l}_"
        full = layer_full(l, carried)
        W = layer_weights(full["w_in"], full["mla_w_uq"], full["mla_w_ukv"], full["gla_af_w2"], full["gla_ab_w2"],
                          full["w_branch_mla"], full["w_branch_pool"], full["w_branch_gla"], full["w_out"])
        P = dict(pre=small["pre_norm"][l][None], post=small["post_norm"][l][None], qg=small["mla_q_norm"][l][None], kvg=small["mla_kv_norm"][l][None],
                 pw=small["pool_w"][l].astype(BF16), ps=small["pool_scale"][l][None], baf=small["gla_af_b"][l][None], bab=small["gla_ab_b"][l][None],
                 gn=small["gla_norm"][l][None])
        mod8 = mod_fwd(a8, full["mod_w"], small["mod_b"][l][None], tag + "mod")
        ms = jnp.stack([jnp.broadcast_to(mod8[B], (B, 3 * D_MODEL)), mod8[:B]], axis=1).reshape(2 * B, 1, 3 * D_MODEL)
        x2, res, carried = layer_fwd(x2, ms, W, P, cos, sin, B, L, n_ctx, tag, host_fwd(l) if host_fwd else None)
        Ws.append(W), Ps.append(P), mss.append(ms), ress.append(res), mod_ws.append(full["mod_w"])
    dx, loss = loss_head(x2, loss_target.reshape(B * S, D_MODEL), nb, "loss_head")
    grads = [None] * depth
    delivered = [None] * depth
    dz8s = [None] * depth
    da8 = None
    for l in reversed(range(depth)):
        tag = f"l{l}_"
        hosted = host_bwd(l, grads[l + 1]) if host_bwd and l + 1 < depth else None
        dx, g, dms, got, own = layer_bwd(dx, ress[l], mss[l], Ws[l], Ps[l], cos, sin, B, L, n_ctx, tag, hosted, host_own(l) if host_own else None)
        if hosted:
            delivered[l + 1] = got
        if own:
            delivered[l] = own
        dms = dms.reshape(B, 2, 3 * D_MODEL)
        dz8s[l] = jnp.zeros((8, 3 * D_MODEL), F32).at[:B].set(dms[:, 1]).at[B].set(jnp.sum(dms[:, 0], axis=0))
        g_mod_b, da = mod_bwd(a8, mod_ws[l], dz8s[l], tag + "mod_bwd")
        da8 = da if da8 is None else da8 + da
        g["mod_b"] = g_mod_b[0]
        grads[l] = g
    grad_x = dx.reshape(B, L, D_MODEL)[:, n_ctx:]
    return loss, grad_x, grads, da8[B], delivered, (a8, dz8s)


_MESH_ID = pl.DeviceIdType.MESH
_HBM = pl.BlockSpec(memory_space=pltpu.HBM)


def _me_and_peers():
    mx, my, mc = lax.axis_index("x"), lax.axis_index("y"), lax.axis_index("c")
    peers = []
    for k in range(1, N_DEV):
        px, py, pc = mx ^ ((k >> 2) & 1), my ^ ((k >> 1) & 1), mc ^ (k & 1)
        peers.append(((px, py, pc), 4 * px + 2 * py + pc))
    return 4 * mx + 2 * my + mc, peers


def _comm_scratch(n):
    return [pltpu.SemaphoreType.DMA((n * (N_DEV - 1),)), pltpu.SemaphoreType.DMA((n * (N_DEV - 1),)), pltpu.SemaphoreType.DMA((n,))]


class _Gather:
    def __init__(self, x_refs, o_refs, send_sems, recv_sems, local_sems):
        self.x, self.o, self.send, self.recv, self.local = x_refs, o_refs, send_sems, recv_sems, local_sems
        self.n = len(x_refs)
        mx, my, mc = lax.axis_index("x"), lax.axis_index("y"), lax.axis_index("c")
        self.me, self.sibling, self.mc = (mx, my, mc), (mx, my, 1 - mc), mc
        self.chips = [(1 - mx, my), (mx, 1 - my), (1 - mx, 1 - my)]

    @staticmethod
    def out_shape(xs):
        return [jax.ShapeDtypeStruct((N_DEV,) + x.shape, x.dtype) for x in xs]

    def _copy(self, i, k, block, to, src=None):
        px, py, pc = block
        dst = self.o[i].at[4 * px + 2 * py + pc]
        sem = (N_DEV - 1) * i + k
        return pltpu.make_async_remote_copy(src_ref=dst if src is None else src, dst_ref=dst, send_sem=self.send.at[sem],
                                            recv_sem=self.recv.at[sem], device_id=to, device_id_type=_MESH_ID)

    def _mine(self, i):
        mx, my, mc = self.me
        return pltpu.make_async_copy(self.x[i], self.o[i].at[4 * mx + 2 * my + mc], self.local.at[i])

    def _first(self):
        out = []
        for i in range(self.n):
            out.append(self._copy(i, 0, self.me, self.sibling, src=self.x[i]))
            out += [self._copy(i, 1 + j, self.me, (*chip, self.mc), src=self.x[i]) for j, chip in enumerate(self.chips)]
        return out

    def _passed(self, j, i):
        return self._copy(i, 4 + j, (*self.chips[j], self.mc), self.sibling)

    def start(self):
        for i in range(self.n):
            self._mine(i).start()
        for cp in self._first():
            cp.start()

    def forward(self):
        for j, chip in enumerate(self.chips):
            for i in range(self.n):
                self._copy(i, 1 + j, (*chip, self.mc), self.me).wait_recv()
                self._passed(j, i).start()

    def finish(self):
        for i in range(self.n):
            self._copy(i, 0, self.sibling, self.me).wait_recv()
            for j, chip in enumerate(self.chips):
                self._copy(i, 4 + j, (*chip, 1 - self.mc), self.me).wait_recv()
        for cp in self._first():
            cp.wait_send()
        for j in range(len(self.chips)):
            for i in range(self.n):
                self._passed(j, i).wait_send()
        for i in range(self.n):
            self._mine(i).wait()


class _Scatter:
    def __init__(self, x_refs, o_refs, send_sems, recv_sems, local_sems):
        self.x, self.o, self.send, self.recv, self.local = x_refs, o_refs, send_sems, recv_sems, local_sems
        self.n = len(x_refs)
        self.me, self.peers = _me_and_peers()

    @staticmethod
    def out_shape(xs):
        return [jax.ShapeDtypeStruct(x.shape, x.dtype) for x in xs]

    def _copy(self, i, k, src_slot, dst_slot, to):
        sem = (N_DEV - 1) * i + k
        return pltpu.make_async_remote_copy(src_ref=self.x[i].at[src_slot], dst_ref=self.o[i].at[dst_slot], send_sem=self.send.at[sem],
                                            recv_sem=self.recv.at[sem], device_id=to, device_id_type=_MESH_ID)

    def _mine(self, i):
        return pltpu.make_async_copy(self.x[i].at[self.me], self.o[i].at[self.me], self.local.at[i])

    def _sends(self):
        return [self._copy(i, k, slot, self.me, peer) for k, (peer, slot) in enumerate(self.peers) for i in range(self.n)]

    def start(self):
        for i in range(self.n):
            self._mine(i).start()
        for cp in self._sends():
            cp.start()

    def forward(self):
        pass

    def finish(self):
        for k, (peer, slot) in enumerate(self.peers):
            for i in range(self.n):
                self._copy(i, k, slot, slot, peer).wait_recv()
        for cp in self._sends():
            cp.wait_send()
        for i in range(self.n):
            self._mine(i).wait()


class _Hosted:
    def __init__(self, kind, xs):
        self.kind, self.xs, self.n = kind, list(xs), len(xs)
        self.in_specs = [_HBM] * self.n
        self.out_specs = [_HBM] * self.n
        self.out_shape = kind.out_shape(self.xs)
        self.scratch = _comm_scratch(self.n)

    def run(self, x_refs, o_refs, sems, step, total):
        for when, phase in ((0, "start"), (total // 2, "forward"), (total - 1, "finish")):
            @pl.when(step == when)
            def _(phase=phase):
                getattr(self.kind(x_refs, o_refs, *sems), phase)()


def gather_blocks(xs, name):
    n = len(xs)

    def body(*refs):
        g = _Gather(refs[:n], refs[n:2 * n], *refs[2 * n:])
        g.start()
        g.forward()
        g.finish()

    return pl.pallas_call(
        body, name=name, in_specs=[_HBM] * n, out_specs=[_HBM] * n,
        out_shape=_Gather.out_shape(xs), scratch_shapes=_comm_scratch(n),
    )(*xs)


def reduce_adamw(slots, w, m, v, name, tr=256):
    R, C = w.shape
    nl = len(slots)
    ns = slots[0].shape[0]
    rows = R // nl
    tr = min(tr, rows)
    nbl = rows // tr
    c1 = 1.0 / (1.0 - ADAM_B1 ** ADAM_STEP)
    c2 = 1.0 / (1.0 - ADAM_B2 ** ADAM_STEP)

    def body(*refs):
        w_ref, m_ref, v_ref, g_ref, d_ref, nm_ref, nv_ref = refs[nl:]
        part = pl.program_id(0) // nbl
        g = None
        for l, s_ref in enumerate(refs[:nl]):
            gl = s_ref[0].astype(F32)
            for s in range(1, ns):
                gl = gl + s_ref[s].astype(F32)
            g = gl if g is None else jnp.where(part == l, gl, g)
        nm = ADAM_B1 * m_ref[...] + (1.0 - ADAM_B1) * g
        nv = ADAM_B2 * v_ref[...] + (1.0 - ADAM_B2) * (g * g)
        g_ref[...] = g
        nm_ref[...] = nm
        nv_ref[...] = nv
        d_ref[...] = -ADAM_LR * ((nm * c1) / (jnp.sqrt(nv * c2) + ADAM_EPS) + ADAM_WD * w_ref[...])

    blk = pl.BlockSpec((tr, C), lambda i: (i, 0))
    sspecs = [pl.BlockSpec((ns, tr, C), lambda i, l=l: (0, jnp.clip(i - l * nbl, 0, nbl - 1), 0)) for l in range(nl)]
    return pl.pallas_call(
        body, name=name, grid=(R // tr,),
        in_specs=sspecs + [blk, blk, blk], out_specs=[blk] * 4,
        out_shape=[jax.ShapeDtypeStruct((R, C), F32)] * 4,
        compiler_params=_params(("parallel",), VMEM_LIMIT),
    )(*slots, w, m, v)


ARG_WEIGHTS = ("c_ctx", "mod_w", "mod_b", "pre_norm", "post_norm", "w_in", "mla_q_norm", "mla_w_uq", "mla_kv_norm", "mla_w_ukv", "pool_w",
               "pool_scale", "gla_af_w2", "gla_af_b", "gla_ab_w2", "gla_ab_b", "gla_norm", "w_branch_mla", "w_branch_pool", "w_branch_gla", "w_out")
SHARDED = ("mod_w", "w_in", "mla_w_uq", "mla_w_ukv", "gla_af_w2", "gla_ab_w2", "w_branch_mla", "w_branch_pool", "w_branch_gla", "w_out")
ROW_SHARDED = ("w_out",)
REPLICATED = tuple(n for n in ARG_WEIGHTS if n not in SHARDED)
PACK_ROWS = 512


def _pack(parts, dtype):
    flat = jnp.concatenate([p.astype(dtype).reshape(-1) for p in parts])
    n = flat.shape[0]
    total = -(-n // (PACK_ROWS * LANES)) * (PACK_ROWS * LANES)
    return jnp.pad(flat, (0, total - n)).reshape(total // LANES, LANES)


def _unpack(buf, shapes):
    flat = buf.reshape(-1)
    out, off = [], 0
    for shp in shapes:
        n = math.prod(shp)
        out.append(flat[off:off + n].reshape(shp))
        off += n
    return out


def _gathered_to_full(g, name):
    _, r, cs = g.shape
    if name in ROW_SHARDED:
        return g.reshape(N_DEV * r, cs)
    return g.transpose(1, 0, 2).reshape(r, N_DEV * cs)


def _full_to_slots(w, name):
    if name in ROW_SHARDED:
        return w.reshape(N_DEV, w.shape[0] // N_DEV, w.shape[1])
    return w.reshape(w.shape[0], N_DEV, w.shape[1] // N_DEV).transpose(1, 0, 2)


def kernel(x, c, ctx, c_ctx, mod_w, mod_b, pre_norm, post_norm, w_in, mla_q_norm, mla_w_uq, mla_kv_norm, mla_w_ukv, pool_w, pool_scale, gla_af_w2, gla_af_b, gla_ab_w2, gla_ab_b, gla_norm, w_branch_mla, w_branch_pool, w_branch_gla, w_out, loss_target, m_c_ctx, m_mod_w, m_mod_b, m_pre_norm, m_post_norm, m_w_in, m_mla_q_norm, m_mla_w_uq, m_mla_kv_norm, m_mla_w_ukv, m_pool_w, m_pool_scale, m_gla_af_w2, m_gla_af_b, m_gla_ab_w2, m_gla_ab_b, m_gla_norm, m_w_branch_mla, m_w_branch_pool, m_w_branch_gla, m_w_out, v_c_ctx, v_mod_w, v_mod_b, v_pre_norm, v_post_norm, v_w_in, v_mla_q_norm, v_mla_w_uq, v_mla_kv_norm, v_mla_w_ukv, v_pool_w, v_pool_scale, v_gla_af_w2, v_gla_af_b, v_gla_ab_w2, v_gla_ab_b, v_gla_norm, v_w_branch_mla, v_w_branch_pool, v_w_branch_gla, v_w_out):
    local = dict(locals())
    wts = {n: local[n] for n in ARG_WEIGHTS}
    mom1 = {n: local["m_" + n] for n in ARG_WEIGHTS}
    mom2 = {n: local["v_" + n] for n in ARG_WEIGHTS}
    shard_shapes = [wts[n].shape for n in SHARDED]
    rep_shapes = [wts[n].shape for n in REPLICATED]
    kinds = ("grad", "delta", "new_m", "new_v")

    depth = w_in.shape[0]

    def shards(l):
        return [wts[n][l].astype(BF16) for n in SHARDED]

    first = gather_blocks(shards(0), "gather_weights_l0")

    def layer_full(l, carried):
        return {n: _gathered_to_full(gw, n) for n, gw in zip(SHARDED, first if l == 0 else carried)}

    def host_fwd(l):
        return _Hosted(_Gather, shards(l + 1)) if l + 1 < depth else None

    exchanged = GRADS_EARLY + GRADS_LATE

    def slots(g, names):
        return [_full_to_slots(g[n], n).astype(BF16) for n in names]

    def host_bwd(l, g_above):
        return _Hosted(_Scatter, slots(g_above, exchanged))

    def host_own(l):
        return (lambda names, g: _Hosted(_Scatter, slots(g, names))) if l == 0 else None

    small = {n: wts[n] for n in REPLICATED}
    loss, grad_x, grads, g_c_ctx, arrived, (a8, dz8s) = local_step(x, c, ctx, c_ctx, small, loss_target, depth, layer_full, host_fwd, host_bwd, host_own)
    arrived = [a if isinstance(a, dict) else dict(zip(exchanged, a)) for a in arrived]

    g = {n: (g_c_ctx if n == "c_ctx" else jnp.stack([grads[l][n] for l in range(depth)])) for n in REPLICATED}
    gathered, a_all, dz_all = gather_blocks([_pack([g[n] for n in REPLICATED], F32), a8, jnp.concatenate(dz8s, axis=0)], "gather_small_grads")
    outs = reduce_adamw([gathered], _pack([wts[n] for n in REPLICATED], F32), _pack([mom1[n] for n in REPLICATED], F32),
                        _pack([mom2[n] for n in REPLICATED], F32), "adamw_replicated")
    me = 4 * lax.axis_index("x") + 2 * lax.axis_index("y") + lax.axis_index("c")
    ncol = mod_w.shape[2]
    dz_cols = lax.dynamic_slice_in_dim(dz_all.reshape(N_DEV, depth, 8, 3 * D_MODEL), me * ncol, ncol, axis=3)
    g_mod_w = mod_dw_columns(a_all.reshape(N_DEV * 8, D_MODEL), dz_cols.transpose(1, 0, 2, 3).reshape(depth, N_DEV * 8, ncol), "mod_dw")

    res = {kind: {} for kind in kinds}
    for n, shp in zip(SHARDED, shard_shapes):
        flat = (shp[0] * shp[1], shp[2])
        parts = [g_mod_w.reshape((1,) + flat)] if n == "mod_w" else [arrived[l][n] for l in range(depth)]
        for kind, o in zip(kinds, reduce_adamw(parts, wts[n].reshape(flat), mom1[n].reshape(flat), mom2[n].reshape(flat), "adamw_" + n)):
            res[kind][n] = o.reshape(shp)
    for kind, o in zip(("grad", "delta", "new_m", "new_v"), outs):
        res[kind].update(zip(REPLICATED, _unpack(o, rep_shapes)))

    loss = lax.psum(loss[0, 0], ("x", "y", "c"))
    return (loss, grad_x, *[res[kind][n] for kind in ("grad", "delta", "new_m", "new_v") for n in ARG_WEIGHTS])
```

```python
import functools
import math

import jax
import jax.numpy as jnp
import numpy as np
from jax import lax
from jax.experimental import pallas as pl
from jax.experimental.pallas import tpu as pltpu

F32 = jnp.float32
BF16 = jnp.bfloat16

D_MODEL = 1024
NORM_EPS = 1e-6
GRID_W = 64
MLA_HEADS, MLA_Q_RANK, MLA_KV_RANK, MLA_NOPE, MLA_ROPE, MLA_V = 8, 256, 128, 64, 32, 64
MLA_WIDTH = MLA_HEADS * MLA_V
ROPE_BASE = 10000.0
ATT_SCALE = (MLA_NOPE + MLA_ROPE) ** -0.5
POOL_WINDOWS = (2, 4, 8, 16)
POOL_WIDTH, POOL_GROUP = 512, 128
GLA_HEADS, GLA_DK, GLA_DV = 4, 64, 128
GLA_KW, GLA_WIDTH = GLA_HEADS * GLA_DK, GLA_HEADS * GLA_DV
GLA_GATE_RANK, GLA_TAU, GLA_CHUNK = 16, 16.0, 64
IN_SIZES = (256, 128, 32, 512, 512, 512, 256, 256, 512, 16, 16, 512, 3 * D_MODEL)
ADAM_LR, ADAM_B1, ADAM_B2, ADAM_EPS, ADAM_WD, ADAM_STEP = 0.001, 0.9, 0.999, 1e-08, 0.01, 10
N_DEV = 8

LANES = 128
TOKEN_BLOCK = 256
WIDE_BLOCK = 768
VMEM_LIMIT = 48 * 1024 * 1024
NEG_BIG = -1e30

_NT = (((1,), (1,)), ((), ()))
_TN = (((0,), (0,)), ((), ()))


def _dot(a, b):
    return jnp.dot(a, b, preferred_element_type=F32)


def _dot_nt(a, b):
    return lax.dot_general(a, b, _NT, preferred_element_type=F32)


def _dot_tn(a, b):
    return lax.dot_general(a, b, _TN, preferred_element_type=F32)


def _params(sem=None, vmem=None):
    kw = {}
    if sem is not None:
        kw["dimension_semantics"] = sem
    if vmem is not None:
        kw["vmem_limit_bytes"] = vmem
    return pltpu.CompilerParams(**kw)


def _wide_block(rows):
    return WIDE_BLOCK if rows % WIDE_BLOCK == 0 else TOKEN_BLOCK


def _full(shape):
    n = len(shape)
    return pl.BlockSpec(shape, lambda *_: (0,) * n)


def _sig(x):
    return 1.0 / (1.0 + jnp.exp(-x))


def _silu_and_grad(x):
    s = _sig(x)
    return x * s, s * (1.0 + x * (1.0 - s))


def _acc(ref, val, first):
    @pl.when(first)
    def _():
        ref[...] = val

    @pl.when(jnp.logical_not(first))
    def _():
        ref[...] += val


def mm_multi(a, ws, dtypes, name, tm=TOKEN_BLOCK):
    M, K = a.shape
    nw = len(ws)

    def body(a_ref, *refs):
        av = a_ref[...]
        for w_ref, o_ref in zip(refs[:nw], refs[nw:]):
            o_ref[...] = _dot(av, w_ref[...]).astype(o_ref.dtype)

    return pl.pallas_call(
        body, name=name, grid=(M // tm,),
        in_specs=[pl.BlockSpec((tm, K), lambda i: (i, 0))] + [_full(w.shape) for w in ws],
        out_specs=[pl.BlockSpec((tm, w.shape[1]), lambda i: (i, 0)) for w in ws],
        out_shape=[jax.ShapeDtypeStruct((M, w.shape[1]), dt) for w, dt in zip(ws, dtypes)],
        compiler_params=_params(("parallel",), VMEM_LIMIT),
    )(a, *ws)


def mm_dw(a, dz, name, tn=1024):
    M, K = a.shape
    n = dz.shape[1]
    tn = min(tn, n)
    tk = next(t for t in (3072, 1536, 1024, 512, TOKEN_BLOCK) if M % t == 0)

    def body(a_ref, dz_ref, o_ref):
        _acc(o_ref, _dot_tn(a_ref[...], dz_ref[...]), pl.program_id(1) == 0)

    return pl.pallas_call(
        body, name=name, grid=(n // tn, M // tk),
        in_specs=[pl.BlockSpec((tk, K), lambda j, k: (k, 0)), pl.BlockSpec((tk, tn), lambda j, k: (k, j))],
        out_specs=pl.BlockSpec((K, tn), lambda j, k: (0, j)),
        out_shape=jax.ShapeDtypeStruct((K, n), F32),
        compiler_params=_params(("parallel", "arbitrary"), VMEM_LIMIT),
    )(a, dz)


def mod_fwd(a8, w, b, name):
    tn = D_MODEL

    def body(a_ref, w_ref, b_ref, o_ref):
        a = a_ref[...]
        o_ref[...] = _dot((a * _sig(a)).astype(BF16), w_ref[...]) + b_ref[...]

    return pl.pallas_call(
        body, name=name, grid=(3,),
        in_specs=[_full(a8.shape), pl.BlockSpec((D_MODEL, tn), lambda j: (0, j)), pl.BlockSpec((1, tn), lambda j: (0, j))],
        out_specs=pl.BlockSpec((8, tn), lambda j: (0, j)),
        out_shape=jax.ShapeDtypeStruct((8, 3 * D_MODEL), F32),
        compiler_params=_params(("parallel",)),
    )(a8, w, b)


def mod_bwd(a8, w, dz8, name):
    tn = D_MODEL

    def body(a_ref, w_ref, dz_ref, db_ref, da_ref):
        a = a_ref[...]
        _, dsa = _silu_and_grad(a)
        dz = dz_ref[...]
        db_ref[...] = jnp.sum(dz, axis=0, keepdims=True)
        _acc(da_ref, _dot_nt(dz.astype(BF16), w_ref[...]) * dsa, pl.program_id(0) == 0)

    return pl.pallas_call(
        body, name=name, grid=(3,),
        in_specs=[_full(a8.shape), pl.BlockSpec((D_MODEL, tn), lambda j: (0, j)), pl.BlockSpec((8, tn), lambda j: (0, j))],
        out_specs=[pl.BlockSpec((1, tn), lambda j: (0, j)), _full((8, D_MODEL))],
        out_shape=[jax.ShapeDtypeStruct((1, 3 * D_MODEL), F32), jax.ShapeDtypeStruct((8, D_MODEL), F32)],
        compiler_params=_params(("arbitrary",)),
    )(a8, w, dz8)


def mod_dw_columns(a_all, dz_cols, name):
    depth, R, n = dz_cols.shape

    def body(a_ref, dz_ref, dw_ref):
        a = a_ref[...]
        dw_ref[0] = _dot_tn((a * _sig(a)).astype(BF16), dz_ref[0].astype(BF16))

    return pl.pallas_call(
        body, name=name, grid=(depth,),
        in_specs=[_full(a_all.shape), pl.BlockSpec((1, R, n), lambda l: (l, 0, 0))],
        out_specs=pl.BlockSpec((1, D_MODEL, n), lambda l: (l, 0, 0)),
        out_shape=jax.ShapeDtypeStruct((depth, D_MODEL, n), F32),
        compiler_params=_params(("parallel",)),
    )(a_all, dz_cols)


def _mod_row(nb):
    return lambda i: 2 * (i // nb) + jnp.minimum(i % nb, 1)


def _mod_spec(nb, part):
    row = _mod_row(nb)
    return pl.BlockSpec((1, 1, D_MODEL), lambda i: (row(i), 0, part))


def norm_in_proj(x2, g, ms, ws, nb, name):
    T = x2.shape[0]
    nw = len(ws)

    def body(x_ref, g_ref, sh_ref, sc_ref, *refs):
        x = x_ref[...]
        r = lax.rsqrt(jnp.mean(x * x, axis=-1, keepdims=True) + NORM_EPS)
        h = ((x * r) * g_ref[...] * (1.0 + sc_ref[0]) + sh_ref[0]).astype(BF16)
        refs[nw][...] = h
        for w_ref, o_ref in zip(refs[:nw], refs[nw + 1:]):
            o_ref[...] = _dot(h, w_ref[...])

    tok = lambda n: pl.BlockSpec((TOKEN_BLOCK, n), lambda i: (i, 0))
    outs = pl.pallas_call(
        body, name=name, grid=(T // TOKEN_BLOCK,),
        in_specs=[tok(D_MODEL), _full((1, D_MODEL)), _mod_spec(nb, 0), _mod_spec(nb, 1)] + [_full(w.shape) for w in ws],
        out_specs=[tok(D_MODEL)] + [tok(w.shape[1]) for w in ws],
        out_shape=[jax.ShapeDtypeStruct((T, D_MODEL), BF16)] + [jax.ShapeDtypeStruct((T, w.shape[1]), F32) for w in ws],
        compiler_params=_params(("parallel",), VMEM_LIMIT),
    )(x2, g, ms, ms, *ws)
    return outs[0], outs[1:]


def in_proj_norm_bwd(dzs, ws, x2, g, ms, dxres, nb, name, hosted=None):
    T = x2.shape[0]
    nw = len(ws)
    nx = hosted.n if hosted else 0
    nrow = ms.shape[0]
    row = _mod_row(nb)
    n_in = 2 * nw + 4

    def body(*refs):
        x_ref, g_ref, sc_ref, dxr_ref = refs[2 * nw:n_in]
        dx_ref, dsh_ref, dsc_ref, dg_ref = refs[n_in + nx:n_in + nx + 4]
        i = pl.program_id(0)
        if hosted:
            hosted.run(refs[n_in:n_in + nx], refs[n_in + nx + 4:n_in + 2 * nx + 4], refs[n_in + 2 * nx + 4:], i, T // TOKEN_BLOCK)
        dh = None
        for dz_ref, w_ref in zip(refs[:nw], refs[nw:2 * nw]):
            t = _dot_nt(dz_ref[...], w_ref[...])
            dh = t if dh is None else dh + t
        x = x_ref[...]
        g = g_ref[...]
        r = lax.rsqrt(jnp.mean(x * x, axis=-1, keepdims=True) + NORM_EPS)
        xn = x * r
        du = dh * (1.0 + sc_ref[0])
        dyg = du * g
        dx_ref[...] = dxr_ref[...] + r * (dyg - xn * jnp.mean(dyg * xn, axis=-1, keepdims=True))
        first = (i % nb) <= 1
        _acc(dsh_ref.at[0], jnp.sum(dh, axis=0, keepdims=True), first)
        _acc(dsc_ref.at[0], jnp.sum(dh * xn * g, axis=0, keepdims=True), first)
        _acc(dg_ref, jnp.sum(du * xn, axis=0, keepdims=True), i == 0)

    tok = lambda n: pl.BlockSpec((TOKEN_BLOCK, n), lambda i: (i, 0))
    acc = pl.BlockSpec((1, 1, D_MODEL), lambda i: (row(i), 0, 0))
    outs = pl.pallas_call(
        body, name=name, grid=(T // TOKEN_BLOCK,),
        in_specs=[tok(dz.shape[1]) for dz in dzs] + [_full(w.shape) for w in ws] + [tok(D_MODEL), _full((1, D_MODEL)), _mod_spec(nb, 1), tok(D_MODEL)]
        + (hosted.in_specs if hosted else []),
        out_specs=[tok(D_MODEL), acc, acc, _full((1, D_MODEL))] + (hosted.out_specs if hosted else []),
        out_shape=[jax.ShapeDtypeStruct((T, D_MODEL), F32), jax.ShapeDtypeStruct((nrow, 1, D_MODEL), F32),
                   jax.ShapeDtypeStruct((nrow, 1, D_MODEL), F32), jax.ShapeDtypeStruct((1, D_MODEL), F32)] + (hosted.out_shape if hosted else []),
        scratch_shapes=hosted.scratch if hosted else [],
        compiler_params=_params(("arbitrary",), VMEM_LIMIT),
    )(*dzs, *ws, x2, g, ms, dxres, *(hosted.xs if hosted else []))
    return outs[:4], list(outs[4:])


def _rot(x):
    lane = lax.broadcasted_iota(jnp.int32, x.shape, 1)
    return jnp.where((lane % 16) < 8, -pltpu.roll(x, LANES - 8, 1), pltpu.roll(x, 8, 1))


def _rope(x, cos, sin):
    return x * cos + _rot(x) * sin


def _rope_t(dy, cos, sin):
    return dy * cos - _rot(dy * sin)


def _rms_rows(x):
    r = lax.rsqrt(jnp.mean(x * x, axis=-1, keepdims=True) + NORM_EPS)
    return x * r, r


def _rms_rows_bwd(dyg, xn, r):
    return r * (dyg - xn * jnp.mean(dyg * xn, axis=-1, keepdims=True))


def mla_prep_fwd(za, qg, kvg, wqn, wqr, wkv, cos, sin, nb, name):
    T = za.shape[0]
    W = MLA_HEADS * LANES

    def body(z_ref, qg_ref, kvg_ref, wqn_ref, wqr_ref, wkv_ref, cos_ref, sin_ref, qn_ref, qr_ref, kv_ref, kr_ref):
        z = z_ref[...]
        cos = cos_ref[...]
        sin = sin_ref[...]
        xq, _ = _rms_rows(z[:, 0:256])
        qn = (xq * qg_ref[...]).astype(BF16)
        qn_ref[...] = (_dot(qn, wqn_ref[...]) * ATT_SCALE).astype(BF16)
        qr = _dot(qn, wqr_ref[...])
        for h in range(MLA_HEADS):
            sl = slice(LANES * h, LANES * (h + 1))
            qr_ref[:, sl] = (_rope(qr[:, sl], cos, sin) * ATT_SCALE).astype(BF16)
        xkv, _ = _rms_rows(z[:, 256:384])
        kv_ref[...] = _dot((xkv * kvg_ref[...]).astype(BF16), wkv_ref[...]).astype(BF16)
        kr_ref[...] = _rope(z[:, 384:512], cos, sin).astype(BF16)

    tb = _wide_block(nb * TOKEN_BLOCK)
    npos = nb * TOKEN_BLOCK // tb
    tok = lambda n: pl.BlockSpec((tb, n), lambda i: (i, 0))
    pos = pl.BlockSpec((tb, LANES), lambda i: (i % npos, 0))
    return pl.pallas_call(
        body, name=name, grid=(T // tb,),
        in_specs=[tok(512), _full(qg.shape), _full(kvg.shape), _full(wqn.shape), _full(wqr.shape), _full(wkv.shape), pos, pos],
        out_specs=[tok(W), tok(W), tok(W), tok(LANES)],
        out_shape=[jax.ShapeDtypeStruct((T, W), BF16)] * 3 + [jax.ShapeDtypeStruct((T, LANES), BF16)],
        compiler_params=_params(("parallel",)),
    )(za, qg, kvg, wqn, wqr, wkv, cos, sin)


def mla_prep_bwd(dqn, dqr, dkv, dkr, za, qg, kvg, wqn, wqr, wkv, cos, sin, nb, name):
    T = za.shape[0]
    W = MLA_HEADS * LANES

    def body(dqn_ref, dqr_ref, dkv_ref, dkr_ref, z_ref, qg_ref, kvg_ref, wqn_ref, wqr_ref, wkv_ref, cos_ref, sin_ref,
             dz_ref, dwqn_ref, dwqr_ref, dwkv_ref, dqg_ref, dkvg_ref):
        first = pl.program_id(0) == 0
        z = z_ref[...]
        cos = cos_ref[...]
        sin = sin_ref[...]
        qg = qg_ref[...]
        kvg = kvg_ref[...]
        xq, rq = _rms_rows(z[:, 0:256])
        qn = (xq * qg).astype(BF16)
        a1 = (dqn_ref[...].astype(F32) * ATT_SCALE).astype(BF16)
        parts = []
        for h in range(MLA_HEADS):
            sl = slice(LANES * h, LANES * (h + 1))
            parts.append(_rope_t(dqr_ref[:, sl].astype(F32) * ATT_SCALE, cos, sin).astype(BF16))
        a2 = jnp.concatenate(parts, axis=1)
        dq = _dot_nt(a1, wqn_ref[...]) + _dot_nt(a2, wqr_ref[...])
        _acc(dwqn_ref, _dot_tn(qn, a1), first)
        _acc(dwqr_ref, _dot_tn(qn, a2), first)
        _acc(dqg_ref, jnp.sum(dq * xq, axis=0, keepdims=True), first)
        dz_ref[:, 0:256] = _rms_rows_bwd(dq * qg, xq, rq).astype(BF16)
        xkv, rkv = _rms_rows(z[:, 256:384])
        kvn = (xkv * kvg).astype(BF16)
        dkvb = dkv_ref[...].astype(BF16)
        dk = _dot_nt(dkvb, wkv_ref[...])
        _acc(dwkv_ref, _dot_tn(kvn, dkvb), first)
        _acc(dkvg_ref, jnp.sum(dk * xkv, axis=0, keepdims=True), first)
        dz_ref[:, 256:384] = _rms_rows_bwd(dk * kvg, xkv, rkv).astype(BF16)
        dz_ref[:, 384:512] = _rope_t(dkr_ref[...], cos, sin).astype(BF16)

    tb = _wide_block(nb * TOKEN_BLOCK)
    npos = nb * TOKEN_BLOCK // tb
    tok = lambda n: pl.BlockSpec((tb, n), lambda i: (i, 0))
    pos = pl.BlockSpec((tb, LANES), lambda i: (i % npos, 0))
    return pl.pallas_call(
        body, name=name, grid=(T // tb,),
        in_specs=[tok(W), tok(W), tok(W), tok(LANES), tok(512), _full(qg.shape), _full(kvg.shape), _full(wqn.shape),
                  _full(wqr.shape), _full(wkv.shape), pos, pos],
        out_specs=[tok(512), _full(wqn.shape), _full(wqr.shape), _full(wkv.shape), _full(qg.shape), _full(kvg.shape)],
        out_shape=[jax.ShapeDtypeStruct((T, 512), BF16), jax.ShapeDtypeStruct(wqn.shape, F32), jax.ShapeDtypeStruct(wqr.shape, F32),
                   jax.ShapeDtypeStruct(wkv.shape, F32), jax.ShapeDtypeStruct(qg.shape, F32), jax.ShapeDtypeStruct(kvg.shape, F32)],
        compiler_params=_params(("arbitrary",)),
    )(dqn, dqr, dkv, dkr, za, qg, kvg, wqn, wqr, wkv, cos, sin)


def _att_qk(qn_ref, qr_ref, kv_ref, kr, j):
    sl = slice(LANES * j, LANES * (j + 1))
    q = jnp.concatenate([qn_ref[0, :, sl], qr_ref[0, :, sl]], axis=1)
    kvj = kv_ref[0, :, sl]
    k = jnp.concatenate([kvj, kr], axis=1)
    return q, k, kvj, _dot_nt(q, k)


def _att_specs(L, lk, q0, pairs=1):
    TQ, W2 = TOKEN_BLOCK, 2 * LANES * pairs
    qspec = pl.BlockSpec((1, TQ, W2), lambda b, h, i: (b, i + q0, h))
    kvspec = pl.BlockSpec((1, lk, W2), lambda b, h, i: (b, 0, h))
    krspec = pl.BlockSpec((1, lk, LANES), lambda b, h, i: (b, 0, 0))
    gspec = pl.BlockSpec((1, TQ, LANES * pairs), lambda b, h, i: (b, i + q0, h))
    lspec = pl.BlockSpec((1, pairs, TQ, LANES), lambda b, h, i: (b, h, i + q0, 0))
    return qspec, kvspec, krspec, gspec, lspec


_ANY = pl.BlockSpec(memory_space=pl.ANY)


def attention_fwd(qn, qr, kv, kr, zg, n_ctx, name, hosted=None):
    B, L, _ = qn.shape
    TQ = TOKEN_BLOCK
    PAIRS = 2
    HP = MLA_HEADS // 2 // PAIRS
    shapes = [jax.ShapeDtypeStruct((B, L, MLA_WIDTH), F32), jax.ShapeDtypeStruct((B, L, MLA_WIDTH), BF16),
              jax.ShapeDtypeStruct((B, MLA_HEADS // 2, L, LANES), F32)]

    nx = hosted.n if hosted else 0
    NQ = L // TQ - 1

    def body(*refs):
        if hosted:
            step = (pl.program_id(0) * HP + pl.program_id(1)) * NQ + pl.program_id(2)
            hosted.run(refs[5:5 + nx], refs[8 + nx:8 + 2 * nx], refs[8 + 2 * nx:], step, B * HP * NQ)
        _fwd_step(*refs[:5], *refs[5 + nx:8 + nx])

    def body_ctx(qn_ref, qr_ref, kv_ref, kr_ref, g_ref, *rest):
        _fwd_step(qn_ref, qr_ref, kv_ref, kr_ref, g_ref, *rest[-3:])

    def _fwd_step(qn_ref, qr_ref, kv_ref, kr_ref, g_ref, ya_ref, ym_ref, lse_ref):
        kr_v = kr_ref[0]
        for pr in range(PAIRS):
            outs, lses = [], []
            for j in (2 * pr, 2 * pr + 1):
                _, _, kvj, s = _att_qk(qn_ref, qr_ref, kv_ref, kr_v, j)
                m = jnp.max(s, axis=-1, keepdims=True)
                p = jnp.exp(s - m).astype(BF16)
                lane_k = lax.broadcasted_iota(jnp.int32, kvj.shape, 1)
                o = _dot(p, jnp.where(lane_k < MLA_V, jnp.ones_like(kvj), kvj))
                l = o[:, 0:1]
                outs.append(o / l)
                lses.append(m + jnp.log(l))
            lane = lax.broadcasted_iota(jnp.int32, outs[0].shape, 1)
            y = jnp.where(lane < MLA_V, pltpu.roll(outs[0], MLA_V, 1), outs[1])
            sl = slice(LANES * pr, LANES * (pr + 1))
            ya_ref[0, :, sl] = y
            g = g_ref[0, :, sl]
            ym_ref[0, :, sl] = (y * g * _sig(g)).astype(BF16)
            lse_ref[0, pr] = jnp.where(lane < MLA_V, lses[0], lses[1])

    qspec, kvspec, krspec, gspec, lspec = _att_specs(L, L, 1, PAIRS)
    main = pl.pallas_call(
        body, name=name, grid=(B, HP, NQ),
        in_specs=[qspec, qspec, kvspec, krspec, gspec] + (hosted.in_specs if hosted else []),
        out_specs=[gspec, gspec, lspec] + (hosted.out_specs if hosted else []),
        out_shape=shapes + (hosted.out_shape if hosted else []), scratch_shapes=hosted.scratch if hosted else [],
        compiler_params=_params(("arbitrary",) * 3 if hosted else ("parallel",) * 3, VMEM_LIMIT),
    )(qn, qr, kv, kr, zg, *(hosted.xs if hosted else []))
    qspec, kvspec, krspec, gspec, lspec = _att_specs(L, n_ctx, 0, PAIRS)
    outs = pl.pallas_call(
        body_ctx, name=name + "_ctx", grid=(B, HP, 1),
        in_specs=[qspec, qspec, kvspec, krspec, gspec, _ANY, _ANY, _ANY], out_specs=[gspec, gspec, lspec], out_shape=shapes,
        input_output_aliases={5: 0, 6: 1, 7: 2},
        compiler_params=_params(("parallel", "parallel", "parallel"), VMEM_LIMIT),
    )(qn, qr, kv, kr, zg, *main[:3])
    return (*outs, list(main[3:]))


def attention_bwd(qn, qr, kv, kr, zg, ya, lse, dym, n_ctx, name, hosted=None):
    B, L, _ = qn.shape
    TQ = TOKEN_BLOCK
    PAIRS = 2
    HP = MLA_HEADS // 2 // PAIRS
    W = MLA_HEADS * LANES
    shapes = [jax.ShapeDtypeStruct((B, L, W), BF16), jax.ShapeDtypeStruct((B, L, W), BF16), jax.ShapeDtypeStruct((B, L, W), F32),
              jax.ShapeDtypeStruct((B, L, LANES), F32), jax.ShapeDtypeStruct((B, L, MLA_WIDTH), BF16)]

    nx = hosted.n if hosted else 0
    NQ = L // TQ - 1

    def body(*refs):
        if hosted:
            step = (pl.program_id(0) * HP + pl.program_id(1)) * NQ + pl.program_id(2)
            hosted.run(refs[8:8 + nx], refs[13 + nx:13 + 2 * nx], refs[13 + 2 * nx:], step, B * HP * NQ)
        dkv_ref, dkr_ref = refs[10 + nx], refs[11 + nx]

        @pl.when(pl.program_id(2) == 0)
        def _():
            dkv_ref[...] = jnp.zeros_like(dkv_ref)

        @pl.when(jnp.logical_and(pl.program_id(2) == 0, pl.program_id(1) == 0))
        def _():
            dkr_ref[...] = jnp.zeros_like(dkr_ref)

        _bwd_step(*refs[:8], *refs[8 + nx:13 + nx])

    def body_ctx(qn_ref, qr_ref, kv_ref, kr_ref, g_ref, ya_ref, lse_ref, dy_ref, dkv_in, dkr_in, a0, a1, a2,
                 dqn_ref, dqr_ref, dkv_ref, dkr_ref, dzg_ref):
        dkv_ref[...] = dkv_in[...]

        @pl.when(pl.program_id(1) == 0)
        def _():
            dkr_ref[...] = dkr_in[...]

        _bwd_step(qn_ref, qr_ref, kv_ref, kr_ref, g_ref, ya_ref, lse_ref, dy_ref, dqn_ref, dqr_ref, dkv_ref, dkr_ref, dzg_ref)

    def _bwd_step(qn_ref, qr_ref, kv_ref, kr_ref, g_ref, ya_ref, lse_ref, dy_ref, dqn_ref, dqr_ref, dkv_ref, dkr_ref, dzg_ref):
        kr_v = kr_ref[0]
        for pr in range(PAIRS):
            psl = slice(LANES * pr, LANES * (pr + 1))
            silu, dsilu = _silu_and_grad(g_ref[0, :, psl])
            dy = dy_ref[0, :, psl]
            ya_v = ya_ref[0, :, psl]
            dya = dy * silu
            dzg_ref[0, :, psl] = (dy * ya_v * dsilu).astype(BF16)
            lane = lax.broadcasted_iota(jnp.int32, dya.shape, 1)
            hi = lane >= MLA_V
            d_out = [jnp.where(hi, pltpu.roll(dya, MLA_V, 1), 0.0), jnp.where(hi, dya, 0.0)]
            prod = dya * ya_v
            drow = [jnp.sum(jnp.where(hi, 0.0, prod), axis=-1, keepdims=True), jnp.sum(jnp.where(hi, prod, 0.0), axis=-1, keepdims=True)]
            lse_v = lse_ref[0, pr]
            for jj in range(2):
                j = 2 * pr + jj
                sl = slice(LANES * j, LANES * (j + 1))
                q, k, kvj, s = _att_qk(qn_ref, qr_ref, kv_ref, kr_v, j)
                pn = jnp.exp(s - lse_v[:, MLA_V * jj:MLA_V * jj + 1])
                dob = d_out[jj].astype(BF16)
                ds = (pn * (_dot_nt(dob, kvj) - drow[jj])).astype(BF16)
                dq = _dot(ds, k)
                dqn_ref[0, :, sl] = jnp.where(hi, 0.0, dq[:, :LANES]).astype(BF16)
                dqr_ref[0, :, sl] = dq[:, LANES:].astype(BF16)
                dk = _dot_tn(ds, q)
                dkv_ref[0, :, sl] += dk[:, :LANES] + _dot_tn(pn.astype(BF16), dob)
                dkr_ref[0] += dk[:, LANES:]

    sem = _params(("parallel", "arbitrary", "arbitrary"), VMEM_LIMIT)
    qspec, kvspec, krspec, gspec, lspec = _att_specs(L, L, 1, PAIRS)
    main = pl.pallas_call(
        body, name=name, grid=(B, HP, NQ),
        in_specs=[qspec, qspec, kvspec, krspec, gspec, gspec, lspec, gspec] + (hosted.in_specs if hosted else []),
        out_specs=[qspec, qspec, kvspec, krspec, gspec] + (hosted.out_specs if hosted else []),
        out_shape=shapes + (hosted.out_shape if hosted else []), scratch_shapes=hosted.scratch if hosted else [],
        compiler_params=_params(("arbitrary",) * 3, VMEM_LIMIT) if hosted else sem,
    )(qn, qr, kv, kr, zg, ya, lse, dym, *(hosted.xs if hosted else []))
    qspec, kvspec, krspec, gspec, lspec = _att_specs(L, n_ctx, 0, PAIRS)
    outs = pl.pallas_call(
        body_ctx, name=name + "_ctx", grid=(B, HP, 1),
        in_specs=[qspec, qspec, kvspec, krspec, gspec, gspec, lspec, gspec, kvspec, krspec, _ANY, _ANY, _ANY],
        out_specs=[qspec, qspec, kvspec, krspec, gspec], out_shape=shapes,
        input_output_aliases={8: 2, 9: 3, 10: 0, 11: 1, 12: 4}, compiler_params=sem,
    )(qn, qr, kv, kr, zg, ya, lse, dym, main[2], main[3], main[0], main[1], main[4])
    return (*outs, list(main[5:]))


def _seg_bounds(rows, n_ctx, L):
    in_ctx = rows < n_ctx
    return jnp.where(in_ctx, 0, n_ctx), jnp.where(in_ctx, n_ctx, L)


def _window_sum(u, w, rows, lo, hi, mirror):
    L = u.shape[0]
    offs = range(-w // 2 + 1, w // 2 + 1) if mirror else range(-w // 2, w // 2)
    acc = None
    for d in offs:
        if d == 0:
            t = u
        else:
            src = rows + d
            t = jnp.where(jnp.logical_and(src >= lo, src < hi), pltpu.roll(u, (-d) % L, 0), 0.0)
        acc = t if acc is None else acc + t
    return acc


def _window_count(w, rows, lo, hi):
    pos = rows - lo
    return (jnp.minimum(pos + w // 2, hi - lo) - jnp.maximum(pos - w // 2, 0)).astype(F32)


def pool_fwd(px, pg, pw, ps, n_ctx, name):
    B, L, _ = px.shape

    def body(px_ref, pg_ref, pw_ref, ps_ref, y_ref):
        rows = lax.broadcasted_iota(jnp.int32, (L, POOL_GROUP), 0)
        lo, hi = _seg_bounds(rows, n_ctx, L)
        for gi, w in enumerate(POOL_WINDOWS):
            sl = slice(POOL_GROUP * gi, POOL_GROUP * (gi + 1))
            u = px_ref[0, :, sl]
            pooled = _window_sum(u, w, rows, lo, hi, False) / _window_count(w, rows, lo, hi) - u
            mixed = _dot(pooled.astype(BF16), pw_ref[gi])
            g = pg_ref[0, :, sl]
            y_ref[0, :, sl] = (mixed * ps_ref[:, sl] * (g * _sig(g))).astype(BF16)

    tok = pl.BlockSpec((1, L, POOL_WIDTH), lambda b: (b, 0, 0))
    return pl.pallas_call(
        body, name=name, grid=(B,),
        in_specs=[tok, tok, _full(pw.shape), _full(ps.shape)],
        out_specs=tok, out_shape=jax.ShapeDtypeStruct((B, L, POOL_WIDTH), BF16),
        compiler_params=_params(("parallel",), VMEM_LIMIT),
    )(px, pg, pw, ps)


def pool_bwd(px, pg, pw, ps, dy, n_ctx, name):
    B, L, _ = px.shape

    def body(px_ref, pg_ref, pw_ref, ps_ref, dy_ref, dpx_ref, dpg_ref, dpw_ref, dps_ref):
        first = pl.program_id(0) == 0
        rows = lax.broadcasted_iota(jnp.int32, (L, POOL_GROUP), 0)
        lo, hi = _seg_bounds(rows, n_ctx, L)
        for gi, w in enumerate(POOL_WINDOWS):
            sl = slice(POOL_GROUP * gi, POOL_GROUP * (gi + 1))
            u = px_ref[0, :, sl]
            cnt = _window_count(w, rows, lo, hi)
            pooled = (_window_sum(u, w, rows, lo, hi, False) / cnt - u).astype(BF16)
            mixed = _dot(pooled, pw_ref[gi])
            silu, dsilu = _silu_and_grad(pg_ref[0, :, sl])
            sc = ps_ref[:, sl]
            dyv = dy_ref[0, :, sl]
            _acc(dps_ref.at[:, sl], jnp.sum(dyv * mixed * silu, axis=0, keepdims=True), first)
            dpg_ref[0, :, sl] = (dyv * mixed * sc * dsilu).astype(BF16)
            dmixed = (dyv * sc * silu).astype(BF16)
            _acc(dpw_ref.at[gi], _dot_tn(pooled, dmixed), first)
            dpooled = _dot_nt(dmixed, pw_ref[gi])
            dpx_ref[0, :, sl] = (_window_sum(dpooled / cnt, w, rows, lo, hi, True) - dpooled).astype(BF16)

    tok = pl.BlockSpec((1, L, POOL_WIDTH), lambda b: (b, 0, 0))
    return pl.pallas_call(
        body, name=name, grid=(B,),
        in_specs=[tok, tok, _full(pw.shape), _full(ps.shape), tok],
        out_specs=[tok, tok, _full(pw.shape), _full(ps.shape)],
        out_shape=[jax.ShapeDtypeStruct((B, L, POOL_WIDTH), BF16)] * 2 + [jax.ShapeDtypeStruct(pw.shape, F32), jax.ShapeDtypeStruct(ps.shape, F32)],
        compiler_params=_params(("arbitrary",), VMEM_LIMIT),
    )(px, pg, pw, ps, dy)


_SCAN_STEPS = (1, 2, 4, 8, 16, 32)
SCAN_CHUNKS = 4


def _cum_fwd(x, r):
    for s in _SCAN_STEPS:
        x = x + jnp.where(r >= s, pltpu.roll(x, s, 0), 0.0)
    return x


def _cum_bwd(x, r):
    n = x.shape[0]
    for s in _SCAN_STEPS:
        x = x + jnp.where(r + s < GLA_CHUNK, pltpu.roll(x, n - s, 0), 0.0)
    return x


def _log_sigmoid(x):
    return jnp.minimum(x, 0.0) - jnp.log(1.0 + jnp.exp(-jnp.abs(x)))


def _gla_decays(lr, w_ref, b_ref, r, reverse):
    pre = _dot(lr, w_ref[...]) + b_ref[...]
    a = _log_sigmoid(pre) / GLA_TAU
    return pre, a, (_cum_bwd(a, r) if reverse else _cum_fwd(a, r)), _chunk_total(a)


def _chunk_total(x):
    x3 = x.reshape(x.shape[0] // GLA_CHUNK, GLA_CHUNK, x.shape[1])
    return jnp.broadcast_to(jnp.sum(x3, axis=1, keepdims=True), x3.shape).reshape(x.shape)


def gla_prep_fwd(zlr, zq, zk, waf, wab, baf, bab, name):
    T = zlr.shape[0]
    tb = _wide_block(T)

    def body(lr_ref, q_ref, k_ref, waf_ref, wab_ref, baf_ref, bab_ref, qf_ref, kf_ref, ksf_ref, tf_ref, qb_ref, kb_ref, ksb_ref, tb_ref):
        r = lax.broadcasted_iota(jnp.int32, (tb, GLA_KW), 0) % GLA_CHUNK
        lr = lr_ref[...].astype(BF16)
        q = q_ref[...] * GLA_DK ** -0.5
        k = k_ref[...]
        for rev, w_ref, b_ref, qo, ko, kso, to in ((False, waf_ref, baf_ref, qf_ref, kf_ref, ksf_ref, tf_ref),
                                                   (True, wab_ref, bab_ref, qb_ref, kb_ref, ksb_ref, tb_ref)):
            _, _, b, tot = _gla_decays(lr, w_ref, b_ref, r, rev)
            qo[...] = (q * jnp.exp(b)).astype(BF16)
            ko[...] = (k * jnp.exp(-b)).astype(BF16)
            kso[...] = (k * jnp.exp(tot - b)).astype(BF16)
            to[...] = tot

    tok = lambda n: pl.BlockSpec((tb, n), lambda i: (i, 0))
    outs = [jax.ShapeDtypeStruct((T, GLA_KW), BF16)] * 3 + [jax.ShapeDtypeStruct((T, GLA_KW), F32)]
    return pl.pallas_call(
        body, name=name, grid=(T // tb,),
        in_specs=[tok(LANES), tok(GLA_KW), tok(GLA_KW), _full(waf.shape), _full(wab.shape), _full(baf.shape), _full(bab.shape)],
        out_specs=[tok(GLA_KW)] * 8, out_shape=outs + outs,
        compiler_params=_params(("parallel",)),
    )(zlr, zq, zk, waf, wab, baf, bab)


def gla_prep_bwd(zlr, zq, zk, waf, wab, baf, bab, gf, gb, dvs, name):
    T = zlr.shape[0]
    tb = _wide_block(T)

    def body(lr_ref, q_ref, k_ref, waf_ref, wab_ref, baf_ref, bab_ref, dqf, dkf, dksf, ddf, dqb, dkb, dksb, ddb, dvf, dvb,
             dlr_ref, dq_ref, dk_ref, dwaf_ref, dwab_ref, dbaf_ref, dbab_ref, dv_ref):
        first = pl.program_id(0) == 0
        dv_ref[...] = (dvf[...] + dvb[...]).astype(BF16)
        r = lax.broadcasted_iota(jnp.int32, (tb, GLA_KW), 0) % GLA_CHUNK
        lr = lr_ref[...].astype(BF16)
        q = q_ref[...] * GLA_DK ** -0.5
        k = k_ref[...]
        dq_tot = None
        dk_tot = None
        dlr = None
        for rev, w_ref, b_ref, dqt, dkt, dks, ddec, dw_ref, db_ref in (
                (False, waf_ref, baf_ref, dqf, dkf, dksf, ddf, dwaf_ref, dbaf_ref),
                (True, wab_ref, bab_ref, dqb, dkb, dksb, ddb, dwab_ref, dbab_ref)):
            pre, _, b, tot = _gla_decays(lr, w_ref, b_ref, r, rev)
            e1 = jnp.exp(b)
            e2 = jnp.exp(-b)
            e3 = jnp.exp(tot - b)
            dqt_v = dqt[...]
            dkt_v = dkt[...]
            dks_v = dks[...]
            dq = dqt_v * e1
            dk = dkt_v * e2 + dks_v * e3
            g3 = dks_v * (k * e3)
            d_b = dqt_v * (q * e1) - dkt_v * (k * e2) - g3
            d_tot = _chunk_total(g3) + ddec[...] * jnp.exp(tot)
            da = (_cum_fwd(d_b, r) if rev else _cum_bwd(d_b, r)) + d_tot
            dpre = (da * (_sig(-pre) / GLA_TAU)).astype(BF16)
            t = _dot_nt(dpre, w_ref[...])
            dlr = t if dlr is None else dlr + t
            _acc(dw_ref, _dot_tn(lr, dpre), first)
            _acc(db_ref, jnp.sum(dpre.astype(F32), axis=0, keepdims=True), first)
            dq_tot = dq if dq_tot is None else dq_tot + dq
            dk_tot = dk if dk_tot is None else dk_tot + dk
        dlr_ref[...] = dlr.astype(BF16)
        dq_ref[...] = (dq_tot * GLA_DK ** -0.5).astype(BF16)
        dk_ref[...] = dk_tot.astype(BF16)

    tok = lambda n: pl.BlockSpec((tb, n), lambda i: (i, 0))
    return pl.pallas_call(
        body, name=name, grid=(T // tb,),
        in_specs=[tok(LANES), tok(GLA_KW), tok(GLA_KW), _full(waf.shape), _full(wab.shape), _full(baf.shape), _full(bab.shape)] + [tok(GLA_KW)] * 8
        + [tok(GLA_WIDTH)] * 2,
        out_specs=[tok(LANES), tok(GLA_KW), tok(GLA_KW), _full(waf.shape), _full(wab.shape), _full(baf.shape), _full(bab.shape), tok(GLA_WIDTH)],
        out_shape=[jax.ShapeDtypeStruct((T, LANES), BF16), jax.ShapeDtypeStruct((T, GLA_KW), BF16), jax.ShapeDtypeStruct((T, GLA_KW), BF16),
                   jax.ShapeDtypeStruct(waf.shape, F32), jax.ShapeDtypeStruct(wab.shape, F32), jax.ShapeDtypeStruct(baf.shape, F32),
                   jax.ShapeDtypeStruct(bab.shape, F32), jax.ShapeDtypeStruct((T, GLA_WIDTH), BF16)],
        compiler_params=_params(("arbitrary",)),
    )(zlr, zq, zk, waf, wab, baf, bab, *gf, *gb, *dvs)


def _chunk_order(nc, n_ctx_chunks, reverse):
    if not reverse:
        return lambda c: c
    return lambda c: jnp.where(c < n_ctx_chunks, n_ctx_chunks - 1 - c, nc + n_ctx_chunks - 1 - c)


def _tri_mask4(reverse):
    ri = lax.broadcasted_iota(jnp.int32, (GLA_CHUNK, GLA_HEADS * GLA_CHUNK), 0)
    ci = lax.broadcasted_iota(jnp.int32, (GLA_CHUNK, GLA_HEADS * GLA_CHUNK), 1) % GLA_CHUNK
    return (ri <= ci) if reverse else (ri >= ci)


def _block_diag(x, rb, cb):
    x4 = jnp.concatenate([x] * GLA_HEADS, axis=0)
    r = lax.broadcasted_iota(jnp.int32, x4.shape, 0) // rb
    c = lax.broadcasted_iota(jnp.int32, x4.shape, 1) // cb
    return jnp.where(r == c, x4, jnp.zeros_like(x4))


def _diag_blocks(f, rb, cb):
    c = lax.broadcasted_iota(jnp.int32, (rb, GLA_HEADS * cb), 1) // cb
    out = None
    for h in range(GLA_HEADS):
        t = jnp.where(c == h, f[rb * h:rb * (h + 1)], 0.0)
        out = t if out is None else out + t
    return out


def gla_scan_fwd(dirs, v, n_ctx, name):
    B, L, _ = v.shape
    C, G = GLA_CHUNK, SCAN_CHUNKS
    nc = L // C
    orders = [_chunk_order(nc // G, n_ctx // C // G, rev) for rev in (False, True)]

    def body(qf, kf, ksf, tf, vf, qb, kb, ksb, tb, vb, of, ssf, ob, ssb, stf, stb):
        @pl.when(pl.program_id(1) == 0)
        def _():
            stf[...] = jnp.zeros_like(stf)
            stb[...] = jnp.zeros_like(stb)

        for sub in range(G):
            step(qf, kf, ksf, vf, tf, of, ssf, stf, False, sub)
            step(qb, kb, ksb, vb, tb, ob, ssb, stb, True, G - 1 - sub)

    def step(q_ref, k_ref, ks_ref, v_ref, tot_ref, o_ref, ss_ref, st, reverse, sub):
        rows = slice(C * sub, C * (sub + 1))
        S = st[...]
        ss_ref[0, sub] = S
        q = q_ref[0, rows]
        v = v_ref[0, rows]
        k4 = _block_diag(k_ref[0, rows], GLA_CHUNK, GLA_DK)
        v4 = _block_diag(v.astype(BF16), GLA_CHUNK, GLA_DV)
        s4 = _block_diag(S.astype(BF16), GLA_DV, GLA_DK)
        P = jnp.where(_tri_mask4(reverse), _dot_nt(q, k4), 0.0)
        o_ref[0, rows] = _dot(P.astype(BF16), v4) + _dot_nt(q, s4)
        st[...] = jnp.exp(tot_ref[0, C * sub:C * sub + 1, :]) * S + _diag_blocks(_dot(v.T.astype(BF16), ks_ref[0, rows]), GLA_DV, GLA_DK)

    in_specs, out_specs, out_shape = [], [], []
    for order in orders:
        tok = lambda n, order=order: pl.BlockSpec((1, G * C, n), lambda b, c: (b, order(c), 0))
        in_specs += [tok(GLA_KW), tok(GLA_KW), tok(GLA_KW), tok(GLA_KW), tok(GLA_WIDTH)]
        out_specs += [tok(GLA_WIDTH), pl.BlockSpec((1, G, GLA_DV, GLA_KW), lambda b, c, order=order: (b, order(c), 0, 0))]
        out_shape += [jax.ShapeDtypeStruct((B, L, GLA_WIDTH), F32), jax.ShapeDtypeStruct((B, nc, GLA_DV, GLA_KW), F32)]
    outs = pl.pallas_call(
        body, name=name, grid=(B, nc // G), in_specs=in_specs, out_specs=out_specs, out_shape=out_shape,
        scratch_shapes=[pltpu.VMEM((GLA_DV, GLA_KW), F32)] * 2,
        compiler_params=_params(("parallel", "arbitrary")),
    )(*dirs[0], v, *dirs[1], v)
    return outs[:2], outs[2:]


def gla_scan_bwd(dirs, v, do, n_ctx, name, hosted=None):
    B, L, _ = v.shape
    C, G = GLA_CHUNK, SCAN_CHUNKS
    nc = L // C
    npair = nc // G
    orders = []
    for rev in (False, True):
        fwd_order = _chunk_order(npair, n_ctx // C // G, rev)
        orders.append(lambda c, fwd_order=fwd_order: fwd_order(npair - 1 - c))

    nx = hosted.n if hosted else 0

    def body(*refs):
        qf, kf, ksf, tf, ssf, vf, dof, qb, kb, ksb, tb, ssb, vb, dob = refs[:14]
        dqf, dkf, dksf, dvf, ddf, dqb, dkb, dksb, dvb, ddb = refs[14 + nx:24 + nx]
        dstf, dstb = refs[24 + 2 * nx:26 + 2 * nx]
        if hosted:
            hosted.run(refs[14:14 + nx], refs[24 + nx:24 + 2 * nx], refs[26 + 2 * nx:], pl.program_id(0) * npair + pl.program_id(1), B * npair)

        @pl.when(pl.program_id(1) == 0)
        def _():
            dstf[...] = jnp.zeros_like(dstf)
            dstb[...] = jnp.zeros_like(dstb)

        for sub in range(G):
            step(qf, kf, ksf, vf, tf, ssf, dof, dqf, dkf, dksf, dvf, ddf, dstf, False, G - 1 - sub)
            step(qb, kb, ksb, vb, tb, ssb, dob, dqb, dkb, dksb, dvb, ddb, dstb, True, sub)

    def step(q_ref, k_ref, ks_ref, v_ref, tot_ref, ss_ref, do_ref, dq_ref, dk_ref, dks_ref, dv_ref, dd_ref, dst, reverse, sub):
        rows = slice(C * sub, C * (sub + 1))
        dSn = dst[...]
        S = ss_ref[0, sub]
        q = q_ref[0, rows]
        vb = v_ref[0, rows].astype(BF16)
        dob = do_ref[0, rows].astype(BF16)
        k4 = _block_diag(k_ref[0, rows], GLA_CHUNK, GLA_DK)
        v4 = _block_diag(vb, GLA_CHUNK, GLA_DV)
        s4 = _block_diag(S.astype(BF16), GLA_DV, GLA_DK)
        ds4 = _block_diag(dSn.astype(BF16), GLA_DV, GLA_DK)
        tri = _tri_mask4(reverse)
        P = jnp.where(tri, _dot_nt(q, k4), 0.0).astype(BF16)
        dP = jnp.where(tri, _dot_nt(dob, v4), 0.0).astype(BF16)
        dq_ref[0, rows] = _dot(dob, s4) + _dot(dP, k4)
        dk_ref[0, rows] = _diag_blocks(_dot_tn(dP, q), GLA_CHUNK, GLA_DK)
        dv_ref[0, rows] = _diag_blocks(_dot_tn(P, dob), GLA_CHUNK, GLA_DV) + _dot_nt(ks_ref[0, rows], ds4)
        dks_ref[0, rows] = _dot(vb, ds4)
        dd_ref[0, rows] = jnp.broadcast_to(jnp.sum(dSn * S, axis=0, keepdims=True), (C, GLA_KW))
        dst[...] = jnp.exp(tot_ref[0, C * sub:C * sub + 1, :]) * dSn + _diag_blocks(_dot_tn(dob, q), GLA_DV, GLA_DK)

    in_specs, out_specs, out_shape = [], [], []
    for order in orders:
        tok = lambda n, order=order: pl.BlockSpec((1, G * C, n), lambda b, c: (b, order(c), 0))
        in_specs += [tok(GLA_KW), tok(GLA_KW), tok(GLA_KW), tok(GLA_KW),
                     pl.BlockSpec((1, G, GLA_DV, GLA_KW), lambda b, c, order=order: (b, order(c), 0, 0)), tok(GLA_WIDTH), tok(GLA_WIDTH)]
        out_specs += [tok(GLA_KW), tok(GLA_KW), tok(GLA_KW), tok(GLA_WIDTH), tok(GLA_KW)]
        out_shape += [jax.ShapeDtypeStruct((B, L, GLA_KW), F32)] * 3 + [jax.ShapeDtypeStruct((B, L, GLA_WIDTH), F32), jax.ShapeDtypeStruct((B, L, GLA_KW), F32)]
    outs = pl.pallas_call(
        body, name=name, grid=(B, npair), in_specs=in_specs + (hosted.in_specs if hosted else []),
        out_specs=out_specs + (hosted.out_specs if hosted else []), out_shape=out_shape + (hosted.out_shape if hosted else []),
        scratch_shapes=[pltpu.VMEM((GLA_DV, GLA_KW), F32)] * 2 + (hosted.scratch if hosted else []),
        compiler_params=_params(("arbitrary", "arbitrary") if hosted else ("parallel", "arbitrary")),
    )(*dirs[0], v, do, *dirs[1], v, do, *(hosted.xs if hosted else []))
    return outs[:5], outs[5:10], list(outs[10:])


def gla_out_fwd(of, ob, gn, zg, name):
    T = of.shape[0]

    def body(of_ref, ob_ref, gn_ref, g_ref, y_ref):
        for h in range(GLA_HEADS):
            sl = slice(GLA_DV * h, GLA_DV * (h + 1))
            xn, _ = _rms_rows(of_ref[:, sl] + ob_ref[:, sl])
            g = g_ref[:, sl]
            y_ref[:, sl] = (xn * gn_ref[...] * (g * _sig(g))).astype(BF16)

    tb = _wide_block(T)
    tok = pl.BlockSpec((tb, GLA_WIDTH), lambda i: (i, 0))
    return pl.pallas_call(
        body, name=name, grid=(T // tb,),
        in_specs=[tok, tok, _full(gn.shape), tok], out_specs=tok,
        out_shape=jax.ShapeDtypeStruct((T, GLA_WIDTH), BF16),
        compiler_params=_params(("parallel",)),
    )(of, ob, gn, zg)


def gla_out_bwd(of, ob, gn, zg, dy, name):
    T = of.shape[0]

    def body(of_ref, ob_ref, gn_ref, g_ref, dy_ref, do_ref, dzg_ref, dgn_ref):
        first = pl.program_id(0) == 0
        gn_v = gn_ref[...]
        dgn = None
        for h in range(GLA_HEADS):
            sl = slice(GLA_DV * h, GLA_DV * (h + 1))
            xn, r = _rms_rows(of_ref[:, sl] + ob_ref[:, sl])
            silu, dsilu = _silu_and_grad(g_ref[:, sl])
            dyv = dy_ref[:, sl]
            dzg_ref[:, sl] = (dyv * xn * gn_v * dsilu).astype(BF16)
            dn = dyv * silu
            t = jnp.sum(dn * xn, axis=0, keepdims=True)
            dgn = t if dgn is None else dgn + t
            do_ref[:, sl] = _rms_rows_bwd(dn * gn_v, xn, r)
        _acc(dgn_ref, dgn, first)

    tb = _wide_block(T)
    tok = pl.BlockSpec((tb, GLA_WIDTH), lambda i: (i, 0))
    return pl.pallas_call(
        body, name=name, grid=(T // tb,),
        in_specs=[tok, tok, _full(gn.shape), tok, tok], out_specs=[tok, tok, _full(gn.shape)],
        out_shape=[jax.ShapeDtypeStruct((T, GLA_WIDTH), F32), jax.ShapeDtypeStruct((T, GLA_WIDTH), BF16), jax.ShapeDtypeStruct(gn.shape, F32)],
        compiler_params=_params(("arbitrary",)),
    )(of, ob, gn, zg, dy)


def merge_post_fwd(ys, zm, wbs, wo, x2, pg, ms, nb, name):
    T = x2.shape[0]

    def body(y0, y1, y2, zm_ref, w0, w1, w2, wo_ref, x_ref, pg_ref, gate_ref, xn_ref, out_ref, mg_ref):
        merged = None
        for i, (y_ref, w_ref) in enumerate(((y0, w0), (y1, w1), (y2, w2))):
            t = _sig(zm_ref[:, D_MODEL * i:D_MODEL * (i + 1)].astype(F32)) * _dot(y_ref[...], w_ref[...])
            merged = t if merged is None else merged + t
        mb = merged.astype(BF16)
        mg_ref[...] = mb
        out = _dot(mb, wo_ref[...])
        out_ref[...] = out
        on, _ = _rms_rows(out)
        xn_ref[...] = x_ref[...] + gate_ref[0] * (on * pg_ref[...])

    tok = lambda n: pl.BlockSpec((TOKEN_BLOCK, n), lambda i: (i, 0))
    return pl.pallas_call(
        body, name=name, grid=(T // TOKEN_BLOCK,),
        in_specs=[tok(512)] * 3 + [tok(3 * D_MODEL)] + [_full(w.shape) for w in wbs] + [_full(wo.shape), tok(D_MODEL), _full(pg.shape), _mod_spec(nb, 2)],
        out_specs=[tok(D_MODEL)] * 3,
        out_shape=[jax.ShapeDtypeStruct((T, D_MODEL), F32), jax.ShapeDtypeStruct((T, D_MODEL), F32), jax.ShapeDtypeStruct((T, D_MODEL), BF16)],
        compiler_params=_params(("parallel",), VMEM_LIMIT),
    )(*ys, zm, *wbs, wo, x2, pg, ms)


def merge_post_bwd(dxn, out, ys, zm, wbs, wo, pg, ms, nb, name):
    T = dxn.shape[0]
    nrow = ms.shape[0]
    row = _mod_row(nb)

    def body(dxn_ref, out_ref, y0, y1, y2, zm_ref, w0, w1, w2, wo_ref, pg_ref, gate_ref,
             dy0, dy1, dy2, dzm_ref, dout_ref, dp0, dp1, dp2, dgate_ref, dpg_ref):
        i = pl.program_id(0)
        dxn_v = dxn_ref[...]
        on, r = _rms_rows(out_ref[...])
        pg_v = pg_ref[...]
        _acc(dgate_ref.at[0], jnp.sum(dxn_v * on * pg_v, axis=0, keepdims=True), (i % nb) <= 1)
        dn = dxn_v * gate_ref[0]
        _acc(dpg_ref, jnp.sum(dn * on, axis=0, keepdims=True), i == 0)
        dout = _rms_rows_bwd(dn * pg_v, on, r).astype(BF16)
        dout_ref[...] = dout
        dmerged = _dot_nt(dout, wo_ref[...])
        for j, (y_ref, w_ref, dy_ref, dp_ref) in enumerate(((y0, w0, dy0, dp0), (y1, w1, dy1, dp1), (y2, w2, dy2, dp2))):
            sl = slice(D_MODEL * j, D_MODEL * (j + 1))
            g = _sig(zm_ref[:, sl].astype(F32))
            p = _dot(y_ref[...], w_ref[...])
            dzm_ref[:, sl] = (dmerged * p * g * (1.0 - g)).astype(BF16)
            dp = (dmerged * g).astype(BF16)
            dp_ref[...] = dp
            dy_ref[...] = _dot_nt(dp, w_ref[...])

    tok = lambda n: pl.BlockSpec((TOKEN_BLOCK, n), lambda i: (i, 0))
    return pl.pallas_call(
        body, name=name, grid=(T // TOKEN_BLOCK,),
        in_specs=[tok(D_MODEL), tok(D_MODEL)] + [tok(512)] * 3 + [tok(3 * D_MODEL)] + [_full(w.shape) for w in wbs] + [_full(wo.shape), _full(pg.shape), _mod_spec(nb, 2)],
        out_specs=[tok(512)] * 3 + [tok(3 * D_MODEL), tok(D_MODEL)] + [tok(D_MODEL)] * 3 + [pl.BlockSpec((1, 1, D_MODEL), lambda i: (row(i), 0, 0)), _full(pg.shape)],
        out_shape=[jax.ShapeDtypeStruct((T, 512), F32)] * 3 + [jax.ShapeDtypeStruct((T, 3 * D_MODEL), BF16), jax.ShapeDtypeStruct((T, D_MODEL), BF16)]
        + [jax.ShapeDtypeStruct((T, D_MODEL), BF16)] * 3 + [jax.ShapeDtypeStruct((nrow, 1, D_MODEL), F32), jax.ShapeDtypeStruct(pg.shape, F32)],
        compiler_params=_params(("arbitrary",), VMEM_LIMIT),
    )(dxn, out, *ys, zm, *wbs, wo, pg, ms)


def loss_head(y2, tgt2, nb, name):
    T = y2.shape[0]
    nlat = nb - 1

    def body(y_ref, t_ref, dy_ref, loss_ref, acc):
        i = pl.program_id(0)
        is_lat = (i % nb) > 0

        @pl.when(i == 0)
        def _():
            acc[...] = jnp.zeros_like(acc)

        @pl.when(is_lat)
        def _():
            e = y_ref[...] - t_ref[...]
            dy_ref[...] = e * (1.0 / D_MODEL)
            acc[...] += jnp.sum(e * e, axis=0, keepdims=True)

        @pl.when(jnp.logical_not(is_lat))
        def _():
            dy_ref[...] = jnp.zeros_like(dy_ref)

        @pl.when(i == pl.num_programs(0) - 1)
        def _():
            loss_ref[...] = jnp.sum(acc[...], axis=1, keepdims=True) * (0.5 / D_MODEL)

    tok = pl.BlockSpec((TOKEN_BLOCK, D_MODEL), lambda i: (i, 0))
    tgt = pl.BlockSpec((TOKEN_BLOCK, D_MODEL), lambda i: ((i // nb) * nlat + jnp.maximum(i % nb - 1, 0), 0))
    return pl.pallas_call(
        body, name=name, grid=(T // TOKEN_BLOCK,),
        in_specs=[tok, tgt], out_specs=[tok, _full((1, 1))],
        out_shape=[jax.ShapeDtypeStruct((T, D_MODEL), F32), jax.ShapeDtypeStruct((1, 1), F32)],
        scratch_shapes=[pltpu.VMEM((1, D_MODEL), F32)],
        compiler_params=_params(("arbitrary",)),
    )(y2, tgt2)


_IN_OFFS = tuple(int(o) for o in np.cumsum((0,) + IN_SIZES))
_IN_GROUPS = (("a", 0, 416, 512), ("mg", 416, 512, 512), ("px", 928, 512, 512), ("pg", 1440, 512, 512), ("gq", 1952, 256, 256),
              ("gk", 2208, 256, 256), ("gv", 2464, 512, 512), ("lr", 2976, 32, 128), ("gg", 3008, 512, 512), ("m", 3520, 3072, 3072))


def _pad_cols(w, n):
    return w if w.shape[1] == n else jnp.pad(w, ((0, 0), (0, n - w.shape[1])))


def layer_weights(w_in, w_uq, w_ukv, af_w2, ab_w2, wbm, wbp, wbg, w_out):
    W = {}
    for nm, off, n, npad in _IN_GROUPS:
        W["in_" + nm] = _pad_cols(w_in[:, off:off + n], npad)
    uq = w_uq.reshape(MLA_Q_RANK, MLA_HEADS, MLA_NOPE + MLA_ROPE)
    W["qn"] = jnp.pad(uq[:, :, :MLA_NOPE], ((0, 0), (0, 0), (0, LANES - MLA_NOPE))).reshape(MLA_Q_RANK, MLA_HEADS * LANES)
    W["qr"] = jnp.pad(uq[:, :, MLA_NOPE:], ((0, 0), (0, 0), (0, LANES - MLA_ROPE))).reshape(MLA_Q_RANK, MLA_HEADS * LANES)
    W["kv"] = w_ukv
    W["af"] = jnp.pad(af_w2, ((0, LANES - GLA_GATE_RANK), (0, 0)))
    W["ab"] = jnp.pad(ab_w2, ((GLA_GATE_RANK, LANES - 2 * GLA_GATE_RANK), (0, 0)))
    W["bm"], W["bp"], W["bg"], W["out"] = wbm, wbp, wbg, w_out
    return W


def rope_tables(L, n_ctx):
    t = np.arange(L - n_ctx)
    half = MLA_ROPE // 2
    inv = ROPE_BASE ** (-np.arange(0, half, 2, dtype=np.float32) / half)
    ang_r = (t // GRID_W).astype(np.float32)[:, None] * inv
    ang_c = (t % GRID_W).astype(np.float32)[:, None] * inv
    ang = jnp.asarray(np.concatenate([ang_r, ang_r, ang_c, ang_c], axis=-1), F32)
    cos = jnp.ones((L, LANES), F32).at[n_ctx:, :MLA_ROPE].set(jnp.cos(ang))
    sin = jnp.zeros((L, LANES), F32).at[n_ctx:, :MLA_ROPE].set(jnp.sin(ang))
    return cos, sin


def layer_fwd(x2, ms, W, P, cos, sin, B, L, n_ctx, tag, hosted=None):
    nb = L // TOKEN_BLOCK
    r3 = lambda a: a.reshape(B, L, a.shape[-1])
    r2 = lambda a: a.reshape(B * L, a.shape[-1])
    names = [g[0] for g in _IN_GROUPS[:-1]]
    h, zs = norm_in_proj(x2, P["pre"], ms, [W["in_" + n] for n in names], nb, tag + "in_proj")
    z = dict(zip(names, zs))
    (z["m"],) = mm_multi(h, [W["in_m"]], [BF16], tag + "in_proj_merge", tm=_wide_block(L))
    qn, qr, kv, kr = mla_prep_fwd(z["a"], P["qg"], P["kvg"], W["qn"], W["qr"], W["kv"], cos, sin, nb, tag + "mla_prep")
    ya, y_mla, lse, carried = attention_fwd(r3(qn), r3(qr), r3(kv), r3(kr), r3(z["mg"]), n_ctx, tag + "attention", hosted)
    y_pool = pool_fwd(r3(z["px"]), r3(z["pg"]), P["pw"], P["ps"], n_ctx, tag + "pool")
    qf, kf, ksf, tf, qb, kb, ksb, tb = gla_prep_fwd(z["lr"], z["gq"], z["gk"], W["af"], W["ab"], P["baf"], P["bab"], tag + "gla_prep")
    (of, ssf), (ob, ssb) = gla_scan_fwd([(r3(qf), r3(kf), r3(ksf), r3(tf)), (r3(qb), r3(kb), r3(ksb), r3(tb))], r3(z["gv"]), n_ctx, tag + "gla_scan")
    y_gla = gla_out_fwd(r2(of), r2(ob), P["gn"], z["gg"], tag + "gla_out")
    ys = [r2(y_mla), r2(y_pool), y_gla]
    x_new, out, merged = merge_post_fwd(ys, z["m"], [W["bm"], W["bp"], W["bg"]], W["out"], x2, P["post"], ms, nb, tag + "merge_post")
    res = dict(x2=x2, h=h, z=z, qn=qn, qr=qr, kv=kv, kr=kr, ya=ya, lse=lse, ys=ys, gla_f=(qf, kf, ksf, tf, ssf), gla_b=(qb, kb, ksb, tb, ssb),
               of=of, ob=ob, out=out, merged=merged)
    return x_new, res, carried


def layer_bwd(dxn, res, ms, W, P, cos, sin, B, L, n_ctx, tag, hosted=None, host_own=None):
    nb = L // TOKEN_BLOCK
    r3 = lambda a: a.reshape(B, L, a.shape[-1])
    r2 = lambda a: a.reshape(B * L, a.shape[-1])
    z = res["z"]
    ys = res["ys"]
    wbs = [W["bm"], W["bp"], W["bg"]]
    dy0, dy1, dy2, dzm, dout, dp0, dp1, dp2, dgate, dpost = merge_post_bwd(dxn, res["out"], ys, z["m"], wbs, W["out"], P["post"], ms, nb, tag + "merge_post_bwd")
    G = {"out": mm_dw(res["merged"], dout, tag + "dw_out"), "post": dpost}
    for nm, y, dp in zip(("bm", "bp", "bg"), ys, (dp0, dp1, dp2)):
        G[nm] = mm_dw(y, dp, tag + "dw_" + nm)
    g = {n: _natural_grad(G, n) for n in GRADS_EARLY}
    own = {}
    dz = {"m": dzm}
    do, dz["gg"], G["gn"] = gla_out_bwd(r2(res["of"]), r2(res["ob"]), P["gn"], z["gg"], dy2, tag + "gla_out_bwd")
    carrier = host_own(GRADS_EARLY, g) if host_own else None
    *grads, arrived = gla_scan_bwd([(r3(qt), r3(kt), r3(ks), r3(tot), ss) for qt, kt, ks, tot, ss in (res["gla_f"], res["gla_b"])],
                                   r3(z["gv"]), r3(do), n_ctx, tag + "gla_scan_bwd", carrier)
    own.update(zip(GRADS_EARLY, arrived))
    gf = [r2(a) for a in grads[0]]
    gb = [r2(a) for a in grads[1]]
    dz["lr"], dz["gq"], dz["gk"], G["af"], G["ab"], G["baf"], G["bab"], dz["gv"] = gla_prep_bwd(
        z["lr"], z["gq"], z["gk"], W["af"], W["ab"], P["baf"], P["bab"], gf[:3] + gf[4:], gb[:3] + gb[4:], [gf[3], gb[3]], tag + "gla_prep_bwd")
    dpx, dpg, G["pw"], G["ps"] = pool_bwd(r3(z["px"]), r3(z["pg"]), P["pw"], P["ps"], r3(dy1), n_ctx, tag + "pool_bwd")
    dz["px"], dz["pg"] = r2(dpx), r2(dpg)
    dqn, dqr, dkv, dkr, dzmg, got = attention_bwd(r3(res["qn"]), r3(res["qr"]), r3(res["kv"]), r3(res["kr"]), r3(z["mg"]), res["ya"], res["lse"],
                                                  r3(dy0), n_ctx, tag + "attention_bwd", hosted)
    dz["mg"] = r2(dzmg)
    dz["a"], G["qn"], G["qr"], G["kv"], G["qg"], G["kvg"] = mla_prep_bwd(
        r2(dqn), r2(dqr), r2(dkv), r2(dkr), z["a"], P["qg"], P["kvg"], W["qn"], W["qr"], W["kv"], cos, sin, nb, tag + "mla_prep_bwd")
    names = [grp[0] for grp in _IN_GROUPS]
    for n in names:
        G["in_" + n] = mm_dw(res["h"], dz[n], tag + "dw_in_" + n)
    g.update({n: _natural_grad(G, n) for n in GRADS_LATE})
    carrier = host_own(GRADS_LATE, g) if host_own else None
    (dx, dshift, dscale, G["pre"]), arrived = in_proj_norm_bwd([dz[n] for n in names], [W["in_" + n] for n in names], res["x2"], P["pre"], ms, dxn, nb,
                                                               tag + "in_proj_dx", carrier)
    own.update(zip(GRADS_LATE, arrived))
    g.update({n: _natural_grad(G, n) for n in GRADS_REPLICATED})
    dms = jnp.concatenate([dshift, dscale, dgate], axis=-1)
    return dx, g, dms, got, own


GRADS_EARLY = ("w_branch_mla", "w_branch_pool", "w_branch_gla", "w_out")
GRADS_LATE = ("w_in", "mla_w_uq", "mla_w_ukv", "gla_af_w2", "gla_ab_w2")
GRADS_REPLICATED = ("pre_norm", "post_norm", "mla_q_norm", "mla_kv_norm", "pool_w", "pool_scale", "gla_af_b", "gla_ab_b", "gla_norm")
_DIRECT = dict(mla_w_ukv="kv", w_branch_mla="bm", w_branch_pool="bp", w_branch_gla="bg", w_out="out", pool_w="pw")
_ROW = dict(pre_norm="pre", post_norm="post", mla_q_norm="qg", mla_kv_norm="kvg", pool_scale="ps", gla_af_b="baf", gla_ab_b="bab", gla_norm="gn")


def _natural_grad(G, name):
    if name == "w_in":
        parts = {off: G["in_" + nm][:, :n] for nm, off, n, npad in _IN_GROUPS}
        return jnp.concatenate([parts[o] for o in sorted(parts)], axis=1)
    if name == "mla_w_uq":
        gqn = G["qn"].reshape(MLA_Q_RANK, MLA_HEADS, LANES)[:, :, :MLA_NOPE]
        gqr = G["qr"].reshape(MLA_Q_RANK, MLA_HEADS, LANES)[:, :, :MLA_ROPE]
        return jnp.concatenate([gqn, gqr], axis=-1).reshape(MLA_Q_RANK, MLA_HEADS * (MLA_NOPE + MLA_ROPE))
    if name == "gla_af_w2":
        return G["af"][:GLA_GATE_RANK]
    if name == "gla_ab_w2":
        return G["ab"][GLA_GATE_RANK:2 * GLA_GATE_RANK]
    return G[_DIRECT[name]] if name in _DIRECT else G[_ROW[name]][0]


def local_step(x, c, ctx, c_ctx, small, loss_target, depth, layer_full, host_fwd=None, host_bwd=None, host_own=None):
    B, S, _ = x.shape
    n_ctx = ctx.shape[1]
    L = n_ctx + S
    nb = L // TOKEN_BLOCK
    cos, sin = rope_tables(L, n_ctx)
    x2 = jnp.concatenate([ctx, x], axis=1).reshape(B * L, D_MODEL)
    a8 = jnp.zeros((8, D_MODEL), F32).at[:B].set(c).at[B].set(c_ctx)
    Ws, Ps, mss, ress, mod_ws = [], [], [], [], []
    carried = None
    for l in range(depth):
        tag = f"l{l}_"
        full = layer_full(l, carried)
        W = layer_weights(full["w_in"], full["mla_w_uq"], full["mla_w_ukv"], full["gla_af_w2"], full["gla_ab_w2"],
                          full["w_branch_mla"], full["w_branch_pool"], full["w_branch_gla"], full["w_out"])
        P = dict(pre=small["pre_norm"][l][None], post=small["post_norm"][l][None], qg=small["mla_q_norm"][l][None], kvg=small["mla_kv_norm"][l][None],
                 pw=small["pool_w"][l].astype(BF16), ps=small["pool_scale"][l][None], baf=small["gla_af_b"][l][None], bab=small["gla_ab_b"][l][None],
                 gn=small["gla_norm"][l][None])
        mod8 = mod_fwd(a8, full["mod_w"], small["mod_b"][l][None], tag + "mod")
        ms = jnp.stack([jnp.broadcast_to(mod8[B], (B, 3 * D_MODEL)), mod8[:B]], axis=1).reshape(2 * B, 1, 3 * D_MODEL)
        x2, res, carried = layer_fwd(x2, ms, W, P, cos, sin, B, L, n_ctx, tag, host_fwd(l) if host_fwd else None)
        Ws.append(W), Ps.append(P), mss.append(ms), ress.append(res), mod_ws.append(full["mod_w"])
    dx, loss = loss_head(x2, loss_target.reshape(B * S, D_MODEL), nb, "loss_head")
    grads = [None] * depth
    delivered = [None] * depth
    dz8s = [None] * depth
    da8 = None
    for l in reversed(range(depth)):
        tag = f"l{l}_"
        hosted = host_bwd(l, grads[l + 1]) if host_bwd and l + 1 < depth else None
        dx, g, dms, got, own = layer_bwd(dx, ress[l], mss[l], Ws[l], Ps[l], cos, sin, B, L, n_ctx, tag, hosted, host_own(l) if host_own else None)
        if hosted:
            delivered[l + 1] = got
        if own:
            delivered[l] = own
        dms = dms.reshape(B, 2, 3 * D_MODEL)
        dz8s[l] = jnp.zeros((8, 3 * D_MODEL), F32).at[:B].set(dms[:, 1]).at[B].set(jnp.sum(dms[:, 0], axis=0))
        g_mod_b, da = mod_bwd(a8, mod_ws[l], dz8s[l], tag + "mod_bwd")
        da8 = da if da8 is None else da8 + da
        g["mod_b"] = g_mod_b[0]
        grads[l] = g
    grad_x = dx.reshape(B, L, D_MODEL)[:, n_ctx:]
    return loss, grad_x, grads, da8[B], delivered, (a8, dz8s)


_MESH_ID = pl.DeviceIdType.MESH
_HBM = pl.BlockSpec(memory_space=pltpu.HBM)


def _me_and_peers():
    mx, my, mc = lax.axis_index("x"), lax.axis_index("y"), lax.axis_index("c")
    peers = []
    for k in range(1, N_DEV):
        px, py, pc = mx ^ ((k >> 2) & 1), my ^ ((k >> 1) & 1), mc ^ (k & 1)
        peers.append(((px, py, pc), 4 * px + 2 * py + pc))
    return 4 * mx + 2 * my + mc, peers


def _comm_scratch(n):
    return [pltpu.SemaphoreType.DMA((n * (N_DEV - 1),)), pltpu.SemaphoreType.DMA((n * (N_DEV - 1),)), pltpu.SemaphoreType.DMA((n,))]


class _Gather:
    def __init__(self, x_refs, o_refs, send_sems, recv_sems, local_sems):
        self.x, self.o, self.send, self.recv, self.local = x_refs, o_refs, send_sems, recv_sems, local_sems
        self.n = len(x_refs)
        mx, my, mc = lax.axis_index("x"), lax.axis_index("y"), lax.axis_index("c")
        self.me, self.sibling, self.mc = (mx, my, mc), (mx, my, 1 - mc), mc
        self.chips = [(1 - mx, my), (mx, 1 - my), (1 - mx, 1 - my)]

    @staticmethod
    def out_shape(xs):
        return [jax.ShapeDtypeStruct((N_DEV,) + x.shape, x.dtype) for x in xs]

    def _copy(self, i, k, block, to, src=None):
        px, py, pc = block
        dst = self.o[i].at[4 * px + 2 * py + pc]
        sem = (N_DEV - 1) * i + k
        return pltpu.make_async_remote_copy(src_ref=dst if src is None else src, dst_ref=dst, send_sem=self.send.at[sem],
                                            recv_sem=self.recv.at[sem], device_id=to, device_id_type=_MESH_ID)

    def _mine(self, i):
        mx, my, mc = self.me
        return pltpu.make_async_copy(self.x[i], self.o[i].at[4 * mx + 2 * my + mc], self.local.at[i])

    def _first(self):
        out = []
        for i in range(self.n):
            out.append(self._copy(i, 0, self.me, self.sibling, src=self.x[i]))
            out += [self._copy(i, 1 + j, self.me, (*chip, self.mc), src=self.x[i]) for j, chip in enumerate(self.chips)]
        return out

    def _passed(self, j, i):
        return self._copy(i, 4 + j, (*self.chips[j], self.mc), self.sibling)

    def start(self):
        for i in range(self.n):
            self._mine(i).start()
        for cp in self._first():
            cp.start()

    def forward(self):
        for j, chip in enumerate(self.chips):
            for i in range(self.n):
                self._copy(i, 1 + j, (*chip, self.mc), self.me).wait_recv()
                self._passed(j, i).start()

    def finish(self):
        for i in range(self.n):
            self._copy(i, 0, self.sibling, self.me).wait_recv()
            for j, chip in enumerate(self.chips):
                self._copy(i, 4 + j, (*chip, 1 - self.mc), self.me).wait_recv()
        for cp in self._first():
            cp.wait_send()
        for j in range(len(self.chips)):
            for i in range(self.n):
                self._passed(j, i).wait_send()
        for i in range(self.n):
            self._mine(i).wait()


class _Scatter:
    def __init__(self, x_refs, o_refs, send_sems, recv_sems, local_sems):
        self.x, self.o, self.send, self.recv, self.local = x_refs, o_refs, send_sems, recv_sems, local_sems
        self.n = len(x_refs)
        self.me, self.peers = _me_and_peers()

    @staticmethod
    def out_shape(xs):
        return [jax.ShapeDtypeStruct(x.shape, x.dtype) for x in xs]

    def _copy(self, i, k, src_slot, dst_slot, to):
        sem = (N_DEV - 1) * i + k
        return pltpu.make_async_remote_copy(src_ref=self.x[i].at[src_slot], dst_ref=self.o[i].at[dst_slot], send_sem=self.send.at[sem],
                                            recv_sem=self.recv.at[sem], device_id=to, device_id_type=_MESH_ID)

    def _mine(self, i):
        return pltpu.make_async_copy(self.x[i].at[self.me], self.o[i].at[self.me], self.local.at[i])

    def _sends(self):
        return [self._copy(i, k, slot, self.me, peer) for k, (peer, slot) in enumerate(self.peers) for i in range(self.n)]

    def start(self):
        for i in range(self.n):
            self._mine(i).start()
        for cp in self._sends():
            cp.start()

    def forward(self):
        pass

    def finish(self):
        for k, (peer, slot) in enumerate(self.peers):
            for i in range(self.n):
                self._copy(i, k, slot, slot, peer).wait_recv()
        for cp in self._sends():
            cp.wait_send()
        for i in range(self.n):
            self._mine(i).wait()


class _Hosted:
    def __init__(self, kind, xs):
        self.kind, self.xs, self.n = kind, list(xs), len(xs)
        self.in_specs = [_HBM] * self.n
        self.out_specs = [_HBM] * self.n
        self.out_shape = kind.out_shape(self.xs)
        self.scratch = _comm_scratch(self.n)

    def run(self, x_refs, o_refs, sems, step, total):
        for when, phase in ((0, "start"), (3 * total // 4, "forward"), (total - 1, "finish")):
            @pl.when(step == when)
            def _(phase=phase):
                getattr(self.kind(x_refs, o_refs, *sems), phase)()


def gather_blocks(xs, name):
    n = len(xs)

    def body(*refs):
        g = _Gather(refs[:n], refs[n:2 * n], *refs[2 * n:])
        g.start()
        g.forward()
        g.finish()

    return pl.pallas_call(
        body, name=name, in_specs=[_HBM] * n, out_specs=[_HBM] * n,
        out_shape=_Gather.out_shape(xs), scratch_shapes=_comm_scratch(n),
    )(*xs)


def reduce_adamw(slots, w, m, v, name, tr=256):
    R, C = w.shape
    nl = len(slots)
    ns = slots[0].shape[0]
    rows = R // nl
    tr = min(tr, rows)
    nbl = rows // tr
    c1 = 1.0 / (1.0 - ADAM_B1 ** ADAM_STEP)
    c2 = 1.0 / (1.0 - ADAM_B2 ** ADAM_STEP)

    def body(*refs):
        w_ref, m_ref, v_ref, g_ref, d_ref, nm_ref, nv_ref = refs[nl:]
        part = pl.program_id(0) // nbl
        g = None
        for l, s_ref in enumerate(refs[:nl]):
            gl = s_ref[0].astype(F32)
            for s in range(1, ns):
                gl = gl + s_ref[s].astype(F32)
            g = gl if g is None else jnp.where(part == l, gl, g)
        nm = ADAM_B1 * m_ref[...] + (1.0 - ADAM_B1) * g
        nv = ADAM_B2 * v_ref[...] + (1.0 - ADAM_B2) * (g * g)
        g_ref[...] = g
        nm_ref[...] = nm
        nv_ref[...] = nv
        d_ref[...] = -ADAM_LR * ((nm * c1) / (jnp.sqrt(nv * c2) + ADAM_EPS) + ADAM_WD * w_ref[...])

    blk = pl.BlockSpec((tr, C), lambda i: (i, 0))
    sspecs = [pl.BlockSpec((ns, tr, C), lambda i, l=l: (0, jnp.clip(i - l * nbl, 0, nbl - 1), 0)) for l in range(nl)]
    return pl.pallas_call(
        body, name=name, grid=(R // tr,),
        in_specs=sspecs + [blk, blk, blk], out_specs=[blk] * 4,
        out_shape=[jax.ShapeDtypeStruct((R, C), F32)] * 4,
        compiler_params=_params(("parallel",), VMEM_LIMIT),
    )(*slots, w, m, v)


ARG_WEIGHTS = ("c_ctx", "mod_w", "mod_b", "pre_norm", "post_norm", "w_in", "mla_q_norm", "mla_w_uq", "mla_kv_norm", "mla_w_ukv", "pool_w",
               "pool_scale", "gla_af_w2", "gla_af_b", "gla_ab_w2", "gla_ab_b", "gla_norm", "w_branch_mla", "w_branch_pool", "w_branch_gla", "w_out")
SHARDED = ("mod_w", "w_in", "mla_w_uq", "mla_w_ukv", "gla_af_w2", "gla_ab_w2", "w_branch_mla", "w_branch_pool", "w_branch_gla", "w_out")
ROW_SHARDED = ("w_out",)
REPLICATED = tuple(n for n in ARG_WEIGHTS if n not in SHARDED)
PACK_ROWS = 512


def _pack(parts, dtype):
    flat = jnp.concatenate([p.astype(dtype).reshape(-1) for p in parts])
    n = flat.shape[0]
    total = -(-n // (PACK_ROWS * LANES)) * (PACK_ROWS * LANES)
    return jnp.pad(flat, (0, total - n)).reshape(total // LANES, LANES)


def _unpack(buf, shapes):
    flat = buf.reshape(-1)
    out, off = [], 0
    for shp in shapes:
        n = math.prod(shp)
        out.append(flat[off:off + n].reshape(shp))
        off += n
    return out


def _gathered_to_full(g, name):
    _, r, cs = g.shape
    if name in ROW_SHARDED:
        return g.reshape(N_DEV * r, cs)
    return g.transpose(1, 0, 2).reshape(r, N_DEV * cs)


def _full_to_slots(w, name):
    if name in ROW_SHARDED:
        return w.reshape(N_DEV, w.shape[0] // N_DEV, w.shape[1])
    return w.reshape(w.shape[0], N_DEV, w.shape[1] // N_DEV).transpose(1, 0, 2)


def kernel(x, c, ctx, c_ctx, mod_w, mod_b, pre_norm, post_norm, w_in, mla_q_norm, mla_w_uq, mla_kv_norm, mla_w_ukv, pool_w, pool_scale, gla_af_w2, gla_af_b, gla_ab_w2, gla_ab_b, gla_norm, w_branch_mla, w_branch_pool, w_branch_gla, w_out, loss_target, m_c_ctx, m_mod_w, m_mod_b, m_pre_norm, m_post_norm, m_w_in, m_mla_q_norm, m_mla_w_uq, m_mla_kv_norm, m_mla_w_ukv, m_pool_w, m_pool_scale, m_gla_af_w2, m_gla_af_b, m_gla_ab_w2, m_gla_ab_b, m_gla_norm, m_w_branch_mla, m_w_branch_pool, m_w_branch_gla, m_w_out, v_c_ctx, v_mod_w, v_mod_b, v_pre_norm, v_post_norm, v_w_in, v_mla_q_norm, v_mla_w_uq, v_mla_kv_norm, v_mla_w_ukv, v_pool_w, v_pool_scale, v_gla_af_w2, v_gla_af_b, v_gla_ab_w2, v_gla_ab_b, v_gla_norm, v_w_branch_mla, v_w_branch_pool, v_w_branch_gla, v_w_out):
    local = dict(locals())
    wts = {n: local[n] for n in ARG_WEIGHTS}
    mom1 = {n: local["m_" + n] for n in ARG_WEIGHTS}
    mom2 = {n: local["v_" + n] for n in ARG_WEIGHTS}
    shard_shapes = [wts[n].shape for n in SHARDED]
    rep_shapes = [wts[n].shape for n in REPLICATED]
    kinds = ("grad", "delta", "new_m", "new_v")

    depth = w_in.shape[0]

    def shards(l):
        return [wts[n][l].astype(BF16) for n in SHARDED]

    first = gather_blocks(shards(0), "gather_weights_l0")

    def layer_full(l, carried):
        return {n: _gathered_to_full(gw, n) for n, gw in zip(SHARDED, first if l == 0 else carried)}

    def host_fwd(l):
        return _Hosted(_Gather, shards(l + 1)) if l + 1 < depth else None

    exchanged = GRADS_EARLY + GRADS_LATE

    def slots(g, names):
        return [_full_to_slots(g[n], n).astype(BF16) for n in names]

    def host_bwd(l, g_above):
        return _Hosted(_Scatter, slots(g_above, exchanged))

    def host_own(l):
        return (lambda names, g: _Hosted(_Scatter, slots(g, names))) if l == 0 else None

    small = {n: wts[n] for n in REPLICATED}
    loss, grad_x, grads, g_c_ctx, arrived, (a8, dz8s) = local_step(x, c, ctx, c_ctx, small, loss_target, depth, layer_full, host_fwd, host_bwd, host_own)
    arrived = [a if isinstance(a, dict) else dict(zip(exchanged, a)) for a in arrived]

    g = {n: (g_c_ctx if n == "c_ctx" else jnp.stack([grads[l][n] for l in range(depth)])) for n in REPLICATED}
    gathered, a_all, dz_all = gather_blocks([_pack([g[n] for n in REPLICATED], BF16), a8, jnp.concatenate(dz8s, axis=0)], "gather_small_grads")
    outs = reduce_adamw([gathered], _pack([wts[n] for n in REPLICATED], F32), _pack([mom1[n] for n in REPLICATED], F32),
                        _pack([mom2[n] for n in REPLICATED], F32), "adamw_replicated")
    me = 4 * lax.axis_index("x") + 2 * lax.axis_index("y") + lax.axis_index("c")
    ncol = mod_w.shape[2]
    dz_cols = lax.dynamic_slice_in_dim(dz_all.reshape(N_DEV, depth, 8, 3 * D_MODEL), me * ncol, ncol, axis=3)
    g_mod_w = mod_dw_columns(a_all.reshape(N_DEV * 8, D_MODEL), dz_cols.transpose(1, 0, 2, 3).reshape(depth, N_DEV * 8, ncol), "mod_dw")

    res = {kind: {} for kind in kinds}
    for n, shp in zip(SHARDED, shard_shapes):
        flat = (shp[0] * shp[1], shp[2])
        parts = [g_mod_w.reshape((1,) + flat)] if n == "mod_w" else [arrived[l][n] for l in range(depth)]
        for kind, o in zip(kinds, reduce_adamw(parts, wts[n].reshape(flat), mom1[n].reshape(flat), mom2[n].reshape(flat), "adamw_" + n)):
            res[kind][n] = o.reshape(shp)
    for kind, o in zip(("grad", "delta", "new_m", "new_v"), outs):
        res[kind].update(zip(REPLICATED, _unpack(o, rep_shapes)))

    loss = lax.psum(loss[0, 0], ("x", "y", "c"))
    return (loss, grad_x, *[res[kind][n] for kind in ("grad", "delta", "new_m", "new_v") for n in ARG_WEIGHTS])
```

```python
import functools
import math

import jax
import jax.numpy as jnp
import numpy as np
from jax import lax
from jax.experimental import pallas as pl
from jax.experimental.pallas import tpu as pltpu

F32 = jnp.float32
BF16 = jnp.bfloat16

D_MODEL = 1024
NORM_EPS = 1e-6
GRID_W = 64
MLA_HEADS, MLA_Q_RANK, MLA_KV_RANK, MLA_NOPE, MLA_ROPE, MLA_V = 8, 256, 128, 64, 32, 64
MLA_WIDTH = MLA_HEADS * MLA_V
ROPE_BASE = 10000.0
ATT_SCALE = (MLA_NOPE + MLA_ROPE) ** -0.5
POOL_WINDOWS = (2, 4, 8, 16)
POOL_WIDTH, POOL_GROUP = 512, 128
GLA_HEADS, GLA_DK, GLA_DV = 4, 64, 128
GLA_KW, GLA_WIDTH = GLA_HEADS * GLA_DK, GLA_HEADS * GLA_DV
GLA_GATE_RANK, GLA_TAU, GLA_CHUNK = 16, 16.0, 64
IN_SIZES = (256, 128, 32, 512, 512, 512, 256, 256, 512, 16, 16, 512, 3 * D_MODEL)
ADAM_LR, ADAM_B1, ADAM_B2, ADAM_EPS, ADAM_WD, ADAM_STEP = 0.001, 0.9, 0.999, 1e-08, 0.01, 10
N_DEV = 8

LANES = 128
TOKEN_BLOCK = 256
WIDE_BLOCK = 768
VMEM_LIMIT = 48 * 1024 * 1024
NEG_BIG = -1e30

_NT = (((1,), (1,)), ((), ()))
_TN = (((0,), (0,)), ((), ()))


def _dot(a, b):
    return jnp.dot(a, b, preferred_element_type=F32)


def _dot_nt(a, b):
    return lax.dot_general(a, b, _NT, preferred_element_type=F32)


def _dot_tn(a, b):
    return lax.dot_general(a, b, _TN, preferred_element_type=F32)


def _params(sem=None, vmem=None):
    kw = {}
    if sem is not None:
        kw["dimension_semantics"] = sem
    if vmem is not None:
        kw["vmem_limit_bytes"] = vmem
    return pltpu.CompilerParams(**kw)


def _wide_block(rows):
    return WIDE_BLOCK if rows % WIDE_BLOCK == 0 else TOKEN_BLOCK


def _full(shape):
    n = len(shape)
    return pl.BlockSpec(shape, lambda *_: (0,) * n)


def _sig(x):
    return 1.0 / (1.0 + jnp.exp(-x))


def _silu_and_grad(x):
    s = _sig(x)
    return x * s, s * (1.0 + x * (1.0 - s))


def _acc(ref, val, first):
    @pl.when(first)
    def _():
        ref[...] = val

    @pl.when(jnp.logical_not(first))
    def _():
        ref[...] += val


def mm_multi(a, ws, dtypes, name, tm=TOKEN_BLOCK):
    M, K = a.shape
    nw = len(ws)

    def body(a_ref, *refs):
        av = a_ref[...]
        for w_ref, o_ref in zip(refs[:nw], refs[nw:]):
            o_ref[...] = _dot(av, w_ref[...]).astype(o_ref.dtype)

    return pl.pallas_call(
        body, name=name, grid=(M // tm,),
        in_specs=[pl.BlockSpec((tm, K), lambda i: (i, 0))] + [_full(w.shape) for w in ws],
        out_specs=[pl.BlockSpec((tm, w.shape[1]), lambda i: (i, 0)) for w in ws],
        out_shape=[jax.ShapeDtypeStruct((M, w.shape[1]), dt) for w, dt in zip(ws, dtypes)],
        compiler_params=_params(("parallel",), VMEM_LIMIT),
    )(a, *ws)


def mm_dw(a, dz, name, tn=1024):
    M, K = a.shape
    n = dz.shape[1]
    tn = min(tn, n)
    tk = next(t for t in (3072, 1536, 1024, 512, TOKEN_BLOCK) if M % t == 0)

    def body(a_ref, dz_ref, o_ref):
        _acc(o_ref, _dot_tn(a_ref[...], dz_ref[...]), pl.program_id(1) == 0)

    return pl.pallas_call(
        body, name=name, grid=(n // tn, M // tk),
        in_specs=[pl.BlockSpec((tk, K), lambda j, k: (k, 0)), pl.BlockSpec((tk, tn), lambda j, k: (k, j))],
        out_specs=pl.BlockSpec((K, tn), lambda j, k: (0, j)),
        out_shape=jax.ShapeDtypeStruct((K, n), F32),
        compiler_params=_params(("parallel", "arbitrary"), VMEM_LIMIT),
    )(a, dz)


def mod_fwd(a8, w, b, name):
    tn = D_MODEL

    def body(a_ref, w_ref, b_ref, o_ref):
        a = a_ref[...]
        o_ref[...] = _dot((a * _sig(a)).astype(BF16), w_ref[...]) + b_ref[...]

    return pl.pallas_call(
        body, name=name, grid=(3,),
        in_specs=[_full(a8.shape), pl.BlockSpec((D_MODEL, tn), lambda j: (0, j)), pl.BlockSpec((1, tn), lambda j: (0, j))],
        out_specs=pl.BlockSpec((8, tn), lambda j: (0, j)),
        out_shape=jax.ShapeDtypeStruct((8, 3 * D_MODEL), F32),
        compiler_params=_params(("parallel",)),
    )(a8, w, b)


def mod_bwd(a8, w, dz8, name):
    tn = D_MODEL

    def body(a_ref, w_ref, dz_ref, db_ref, da_ref):
        a = a_ref[...]
        _, dsa = _silu_and_grad(a)
        dz = dz_ref[...]
        db_ref[...] = jnp.sum(dz, axis=0, keepdims=True)
        _acc(da_ref, _dot_nt(dz.astype(BF16), w_ref[...]) * dsa, pl.program_id(0) == 0)

    return pl.pallas_call(
        body, name=name, grid=(3,),
        in_specs=[_full(a8.shape), pl.BlockSpec((D_MODEL, tn), lambda j: (0, j)), pl.BlockSpec((8, tn), lambda j: (0, j))],
        out_specs=[pl.BlockSpec((1, tn), lambda j: (0, j)), _full((8, D_MODEL))],
        out_shape=[jax.ShapeDtypeStruct((1, 3 * D_MODEL), F32), jax.ShapeDtypeStruct((8, D_MODEL), F32)],
        compiler_params=_params(("arbitrary",)),
    )(a8, w, dz8)


def mod_dw_columns(a_all, dz_cols, name):
    depth, R, n = dz_cols.shape

    def body(a_ref, dz_ref, dw_ref):
        a = a_ref[...]
        dw_ref[0] = _dot_tn((a * _sig(a)).astype(BF16), dz_ref[0].astype(BF16))

    return pl.pallas_call(
        body, name=name, grid=(depth,),
        in_specs=[_full(a_all.shape), pl.BlockSpec((1, R, n), lambda l: (l, 0, 0))],
        out_specs=pl.BlockSpec((1, D_MODEL, n), lambda l: (l, 0, 0)),
        out_shape=jax.ShapeDtypeStruct((depth, D_MODEL, n), F32),
        compiler_params=_params(("parallel",)),
    )(a_all, dz_cols)


def _mod_row(nb):
    return lambda i: 2 * (i // nb) + jnp.minimum(i % nb, 1)


def _mod_spec(nb, part):
    row = _mod_row(nb)
    return pl.BlockSpec((1, 1, D_MODEL), lambda i: (row(i), 0, part))


def norm_in_proj(x2, g, ms, ws, nb, name):
    T = x2.shape[0]
    nw = len(ws)

    def body(x_ref, g_ref, sh_ref, sc_ref, *refs):
        x = x_ref[...]
        r = lax.rsqrt(jnp.mean(x * x, axis=-1, keepdims=True) + NORM_EPS)
        h = ((x * r) * g_ref[...] * (1.0 + sc_ref[0]) + sh_ref[0]).astype(BF16)
        refs[nw][...] = h
        for w_ref, o_ref in zip(refs[:nw], refs[nw + 1:]):
            o_ref[...] = _dot(h, w_ref[...])

    tok = lambda n: pl.BlockSpec((TOKEN_BLOCK, n), lambda i: (i, 0))
    outs = pl.pallas_call(
        body, name=name, grid=(T // TOKEN_BLOCK,),
        in_specs=[tok(D_MODEL), _full((1, D_MODEL)), _mod_spec(nb, 0), _mod_spec(nb, 1)] + [_full(w.shape) for w in ws],
        out_specs=[tok(D_MODEL)] + [tok(w.shape[1]) for w in ws],
        out_shape=[jax.ShapeDtypeStruct((T, D_MODEL), BF16)] + [jax.ShapeDtypeStruct((T, w.shape[1]), F32) for w in ws],
        compiler_params=_params(("parallel",), VMEM_LIMIT),
    )(x2, g, ms, ms, *ws)
    return outs[0], outs[1:]


def in_proj_norm_bwd(dzs, ws, x2, g, ms, dxres, nb, name, hosted=None, latent_only=False):
    T = x2.shape[0]
    nw = len(ws)
    nx = hosted.n if hosted else 0
    nrow = ms.shape[0]
    row = _mod_row(nb)
    n_in = 2 * nw + 4

    def body(*refs):
        x_ref, g_ref, sc_ref, dxr_ref = refs[2 * nw:n_in]
        dx_ref, dsh_ref, dsc_ref, dg_ref = refs[n_in + nx:n_in + nx + 4]
        i = pl.program_id(0)
        if hosted:
            hosted.run(refs[n_in:n_in + nx], refs[n_in + nx + 4:n_in + 2 * nx + 4], refs[n_in + 2 * nx + 4:], i, T // TOKEN_BLOCK)
        dh = None
        for dz_ref, w_ref in zip(refs[:nw], refs[nw:2 * nw]):
            t = _dot_nt(dz_ref[...], w_ref[...])
            dh = t if dh is None else dh + t
        x = x_ref[...]
        g = g_ref[...]
        r = lax.rsqrt(jnp.mean(x * x, axis=-1, keepdims=True) + NORM_EPS)
        xn = x * r
        du = dh * (1.0 + sc_ref[0])
        dyg = du * g
        dx_ref[...] = dxr_ref[...] + r * (dyg - xn * jnp.mean(dyg * xn, axis=-1, keepdims=True))
        first = (i % nb) <= 1
        _acc(dsh_ref.at[0], jnp.sum(dh, axis=0, keepdims=True), first)
        _acc(dsc_ref.at[0], jnp.sum(dh * xn * g, axis=0, keepdims=True), first)
        _acc(dg_ref, jnp.sum(du * xn, axis=0, keepdims=True), i == 0)

    tok = lambda n: pl.BlockSpec((TOKEN_BLOCK, n), lambda i: (i, 0))
    acc = pl.BlockSpec((1, 1, D_MODEL), lambda i: (row(i), 0, 0))
    if latent_only:
        dx_rows = T // nb * (nb - 1)
        dx_spec = pl.BlockSpec((TOKEN_BLOCK, D_MODEL), lambda i: ((i // nb) * (nb - 1) + jnp.maximum(i % nb - 1, 0), 0))
    else:
        dx_rows, dx_spec = T, tok(D_MODEL)
    outs = pl.pallas_call(
        body, name=name, grid=(T // TOKEN_BLOCK,),
        in_specs=[tok(dz.shape[1]) for dz in dzs] + [_full(w.shape) for w in ws] + [tok(D_MODEL), _full((1, D_MODEL)), _mod_spec(nb, 1), tok(D_MODEL)]
        + (hosted.in_specs if hosted else []),
        out_specs=[dx_spec, acc, acc, _full((1, D_MODEL))] + (hosted.out_specs if hosted else []),
        out_shape=[jax.ShapeDtypeStruct((dx_rows, D_MODEL), F32), jax.ShapeDtypeStruct((nrow, 1, D_MODEL), F32),
                   jax.ShapeDtypeStruct((nrow, 1, D_MODEL), F32), jax.ShapeDtypeStruct((1, D_MODEL), F32)] + (hosted.out_shape if hosted else []),
        scratch_shapes=hosted.scratch if hosted else [],
        compiler_params=_params(("arbitrary",), VMEM_LIMIT),
    )(*dzs, *ws, x2, g, ms, dxres, *(hosted.xs if hosted else []))
    return outs[:4], list(outs[4:])


def _rot(x):
    lane = lax.broadcasted_iota(jnp.int32, x.shape, 1)
    return jnp.where((lane % 16) < 8, -pltpu.roll(x, LANES - 8, 1), pltpu.roll(x, 8, 1))


def _rope(x, cos, sin):
    return x * cos + _rot(x) * sin


def _rope_t(dy, cos, sin):
    return dy * cos - _rot(dy * sin)


def _rms_rows(x):
    r = lax.rsqrt(jnp.mean(x * x, axis=-1, keepdims=True) + NORM_EPS)
    return x * r, r


def _rms_rows_bwd(dyg, xn, r):
    return r * (dyg - xn * jnp.mean(dyg * xn, axis=-1, keepdims=True))


def mla_prep_fwd(za, qg, kvg, wqn, wqr, wkv, cos, sin, nb, name):
    T = za.shape[0]
    W = MLA_HEADS * LANES

    def body(z_ref, qg_ref, kvg_ref, wqn_ref, wqr_ref, wkv_ref, cos_ref, sin_ref, qn_ref, qr_ref, kv_ref, kr_ref):
        z = z_ref[...]
        cos = cos_ref[...]
        sin = sin_ref[...]
        xq, _ = _rms_rows(z[:, 0:256])
        qn = (xq * qg_ref[...]).astype(BF16)
        qn_ref[...] = (_dot(qn, wqn_ref[...]) * ATT_SCALE).astype(BF16)
        qr = _dot(qn, wqr_ref[...])
        for h in range(MLA_HEADS):
            sl = slice(LANES * h, LANES * (h + 1))
            qr_ref[:, sl] = (_rope(qr[:, sl], cos, sin) * ATT_SCALE).astype(BF16)
        xkv, _ = _rms_rows(z[:, 256:384])
        kv_ref[...] = _dot((xkv * kvg_ref[...]).astype(BF16), wkv_ref[...]).astype(BF16)
        kr_ref[...] = _rope(z[:, 384:512], cos, sin).astype(BF16)

    tb = _wide_block(nb * TOKEN_BLOCK)
    npos = nb * TOKEN_BLOCK // tb
    tok = lambda n: pl.BlockSpec((tb, n), lambda i: (i, 0))
    pos = pl.BlockSpec((tb, LANES), lambda i: (i % npos, 0))
    return pl.pallas_call(
        body, name=name, grid=(T // tb,),
        in_specs=[tok(512), _full(qg.shape), _full(kvg.shape), _full(wqn.shape), _full(wqr.shape), _full(wkv.shape), pos, pos],
        out_specs=[tok(W), tok(W), tok(W), tok(LANES)],
        out_shape=[jax.ShapeDtypeStruct((T, W), BF16)] * 3 + [jax.ShapeDtypeStruct((T, LANES), BF16)],
        compiler_params=_params(("parallel",)),
    )(za, qg, kvg, wqn, wqr, wkv, cos, sin)


def mla_prep_bwd(dqn, dqr, dkv, dkr, za, qg, kvg, wqn, wqr, wkv, cos, sin, nb, name):
    T = za.shape[0]
    W = MLA_HEADS * LANES

    def body(dqn_ref, dqr_ref, dkv_ref, dkr_ref, z_ref, qg_ref, kvg_ref, wqn_ref, wqr_ref, wkv_ref, cos_ref, sin_ref,
             dz_ref, dwqn_ref, dwqr_ref, dwkv_ref, dqg_ref, dkvg_ref):
        first = pl.program_id(0) == 0
        z = z_ref[...]
        cos = cos_ref[...]
        sin = sin_ref[...]
        qg = qg_ref[...]
        kvg = kvg_ref[...]
        xq, rq = _rms_rows(z[:, 0:256])
        qn = (xq * qg).astype(BF16)
        a1 = (dqn_ref[...].astype(F32) * ATT_SCALE).astype(BF16)
        parts = []
        for h in range(MLA_HEADS):
            sl = slice(LANES * h, LANES * (h + 1))
            parts.append(_rope_t(dqr_ref[:, sl].astype(F32) * ATT_SCALE, cos, sin).astype(BF16))
        a2 = jnp.concatenate(parts, axis=1)
        dq = _dot_nt(a1, wqn_ref[...]) + _dot_nt(a2, wqr_ref[...])
        _acc(dwqn_ref, _dot_tn(qn, a1), first)
        _acc(dwqr_ref, _dot_tn(qn, a2), first)
        _acc(dqg_ref, jnp.sum(dq * xq, axis=0, keepdims=True), first)
        dz_ref[:, 0:256] = _rms_rows_bwd(dq * qg, xq, rq).astype(BF16)
        xkv, rkv = _rms_rows(z[:, 256:384])
        kvn = (xkv * kvg).astype(BF16)
        dkvb = dkv_ref[...].astype(BF16)
        dk = _dot_nt(dkvb, wkv_ref[...])
        _acc(dwkv_ref, _dot_tn(kvn, dkvb), first)
        _acc(dkvg_ref, jnp.sum(dk * xkv, axis=0, keepdims=True), first)
        dz_ref[:, 256:384] = _rms_rows_bwd(dk * kvg, xkv, rkv).astype(BF16)
        dz_ref[:, 384:512] = _rope_t(dkr_ref[...], cos, sin).astype(BF16)

    tb = _wide_block(nb * TOKEN_BLOCK)
    npos = nb * TOKEN_BLOCK // tb
    tok = lambda n: pl.BlockSpec((tb, n), lambda i: (i, 0))
    pos = pl.BlockSpec((tb, LANES), lambda i: (i % npos, 0))
    return pl.pallas_call(
        body, name=name, grid=(T // tb,),
        in_specs=[tok(W), tok(W), tok(W), tok(LANES), tok(512), _full(qg.shape), _full(kvg.shape), _full(wqn.shape),
                  _full(wqr.shape), _full(wkv.shape), pos, pos],
        out_specs=[tok(512), _full(wqn.shape), _full(wqr.shape), _full(wkv.shape), _full(qg.shape), _full(kvg.shape)],
        out_shape=[jax.ShapeDtypeStruct((T, 512), BF16), jax.ShapeDtypeStruct(wqn.shape, F32), jax.ShapeDtypeStruct(wqr.shape, F32),
                   jax.ShapeDtypeStruct(wkv.shape, F32), jax.ShapeDtypeStruct(qg.shape, F32), jax.ShapeDtypeStruct(kvg.shape, F32)],
        compiler_params=_params(("arbitrary",)),
    )(dqn, dqr, dkv, dkr, za, qg, kvg, wqn, wqr, wkv, cos, sin)


def _att_qk(qn_ref, qr_ref, kv_ref, kr, j):
    sl = slice(LANES * j, LANES * (j + 1))
    q = jnp.concatenate([qn_ref[0, :, sl], qr_ref[0, :, sl]], axis=1)
    kvj = kv_ref[0, :, sl]
    k = jnp.concatenate([kvj, kr], axis=1)
    return q, k, kvj, _dot_nt(q, k)


def _att_specs(L, lk, q0, pairs=1):
    TQ, W2 = TOKEN_BLOCK, 2 * LANES * pairs
    qspec = pl.BlockSpec((1, TQ, W2), lambda b, h, i: (b, i + q0, h))
    kvspec = pl.BlockSpec((1, lk, W2), lambda b, h, i: (b, 0, h))
    krspec = pl.BlockSpec((1, lk, LANES), lambda b, h, i: (b, 0, 0))
    gspec = pl.BlockSpec((1, TQ, LANES * pairs), lambda b, h, i: (b, i + q0, h))
    lspec = pl.BlockSpec((1, pairs, TQ, LANES), lambda b, h, i: (b, h, i + q0, 0))
    return qspec, kvspec, krspec, gspec, lspec


_ANY = pl.BlockSpec(memory_space=pl.ANY)


def attention_fwd(qn, qr, kv, kr, zg, n_ctx, name, hosted=None):
    B, L, _ = qn.shape
    TQ = TOKEN_BLOCK
    PAIRS = 2
    HP = MLA_HEADS // 2 // PAIRS
    shapes = [jax.ShapeDtypeStruct((B, L, MLA_WIDTH), F32), jax.ShapeDtypeStruct((B, L, MLA_WIDTH), BF16),
              jax.ShapeDtypeStruct((B, MLA_HEADS // 2, L, LANES), F32)]

    nx = hosted.n if hosted else 0
    NQ = L // TQ - 1

    def body(*refs):
        if hosted:
            step = (pl.program_id(0) * HP + pl.program_id(1)) * NQ + pl.program_id(2)
            hosted.run(refs[5:5 + nx], refs[8 + nx:8 + 2 * nx], refs[8 + 2 * nx:], step, B * HP * NQ)
        _fwd_step(*refs[:5], *refs[5 + nx:8 + nx])

    def body_ctx(qn_ref, qr_ref, kv_ref, kr_ref, g_ref, *rest):
        _fwd_step(qn_ref, qr_ref, kv_ref, kr_ref, g_ref, *rest[-3:])

    def _fwd_step(qn_ref, qr_ref, kv_ref, kr_ref, g_ref, ya_ref, ym_ref, lse_ref):
        kr_v = kr_ref[0]
        for pr in range(PAIRS):
            outs, lses = [], []
            for j in (2 * pr, 2 * pr + 1):
                _, _, kvj, s = _att_qk(qn_ref, qr_ref, kv_ref, kr_v, j)
                m = jnp.max(s, axis=-1, keepdims=True)
                p = jnp.exp(s - m).astype(BF16)
                lane_k = lax.broadcasted_iota(jnp.int32, kvj.shape, 1)
                o = _dot(p, jnp.where(lane_k < MLA_V, jnp.ones_like(kvj), kvj))
                l = o[:, 0:1]
                outs.append(o / l)
                lses.append(m + jnp.log(l))
            lane = lax.broadcasted_iota(jnp.int32, outs[0].shape, 1)
            y = jnp.where(lane < MLA_V, pltpu.roll(outs[0], MLA_V, 1), outs[1])
            sl = slice(LANES * pr, LANES * (pr + 1))
            ya_ref[0, :, sl] = y
            g = g_ref[0, :, sl]
            ym_ref[0, :, sl] = (y * g * _sig(g)).astype(BF16)
            lse_ref[0, pr] = jnp.where(lane < MLA_V, lses[0], lses[1])

    qspec, kvspec, krspec, gspec, lspec = _att_specs(L, L, 1, PAIRS)
    main = pl.pallas_call(
        body, name=name, grid=(B, HP, NQ),
        in_specs=[qspec, qspec, kvspec, krspec, gspec] + (hosted.in_specs if hosted else []),
        out_specs=[gspec, gspec, lspec] + (hosted.out_specs if hosted else []),
        out_shape=shapes + (hosted.out_shape if hosted else []), scratch_shapes=hosted.scratch if hosted else [],
        compiler_params=_params(("arbitrary",) * 3 if hosted else ("parallel",) * 3, VMEM_LIMIT),
    )(qn, qr, kv, kr, zg, *(hosted.xs if hosted else []))
    qspec, kvspec, krspec, gspec, lspec = _att_specs(L, n_ctx, 0, PAIRS)
    outs = pl.pallas_call(
        body_ctx, name=name + "_ctx", grid=(B, HP, 1),
        in_specs=[qspec, qspec, kvspec, krspec, gspec, _ANY, _ANY, _ANY], out_specs=[gspec, gspec, lspec], out_shape=shapes,
        input_output_aliases={5: 0, 6: 1, 7: 2},
        compiler_params=_params(("parallel", "parallel", "parallel"), VMEM_LIMIT),
    )(qn, qr, kv, kr, zg, *main[:3])
    return (*outs, list(main[3:]))


def attention_bwd(qn, qr, kv, kr, zg, ya, lse, dym, n_ctx, name, hosted=None):
    B, L, _ = qn.shape
    TQ = TOKEN_BLOCK
    PAIRS = 2
    HP = MLA_HEADS // 2 // PAIRS
    W = MLA_HEADS * LANES
    shapes = [jax.ShapeDtypeStruct((B, L, W), BF16), jax.ShapeDtypeStruct((B, L, W), BF16), jax.ShapeDtypeStruct((B, L, W), F32),
              jax.ShapeDtypeStruct((B, L, LANES), F32), jax.ShapeDtypeStruct((B, L, MLA_WIDTH), BF16)]

    nx = hosted.n if hosted else 0
    NQ = L // TQ - 1

    def body(*refs):
        if hosted:
            step = (pl.program_id(0) * HP + pl.program_id(1)) * NQ + pl.program_id(2)
            hosted.run(refs[8:8 + nx], refs[13 + nx:13 + 2 * nx], refs[13 + 2 * nx:], step, B * HP * NQ)
        dkv_ref, dkr_ref = refs[10 + nx], refs[11 + nx]

        @pl.when(pl.program_id(2) == 0)
        def _():
            dkv_ref[...] = jnp.zeros_like(dkv_ref)

        @pl.when(jnp.logical_and(pl.program_id(2) == 0, pl.program_id(1) == 0))
        def _():
            dkr_ref[...] = jnp.zeros_like(dkr_ref)

        _bwd_step(*refs[:8], *refs[8 + nx:13 + nx])

    def body_ctx(qn_ref, qr_ref, kv_ref, kr_ref, g_ref, ya_ref, lse_ref, dy_ref, dkv_in, dkr_in, a0, a1, a2,
                 dqn_ref, dqr_ref, dkv_ref, dkr_ref, dzg_ref):
        dkv_ref[...] = dkv_in[...]

        @pl.when(pl.program_id(1) == 0)
        def _():
            dkr_ref[...] = dkr_in[...]

        _bwd_step(qn_ref, qr_ref, kv_ref, kr_ref, g_ref, ya_ref, lse_ref, dy_ref, dqn_ref, dqr_ref, dkv_ref, dkr_ref, dzg_ref)

    def _bwd_step(qn_ref, qr_ref, kv_ref, kr_ref, g_ref, ya_ref, lse_ref, dy_ref, dqn_ref, dqr_ref, dkv_ref, dkr_ref, dzg_ref):
        kr_v = kr_ref[0]
        for pr in range(PAIRS):
            psl = slice(LANES * pr, LANES * (pr + 1))
            silu, dsilu = _silu_and_grad(g_ref[0, :, psl])
            dy = dy_ref[0, :, psl]
            ya_v = ya_ref[0, :, psl]
            dya = dy * silu
            dzg_ref[0, :, psl] = (dy * ya_v * dsilu).astype(BF16)
            lane = lax.broadcasted_iota(jnp.int32, dya.shape, 1)
            hi = lane >= MLA_V
            d_out = [jnp.where(hi, pltpu.roll(dya, MLA_V, 1), 0.0), jnp.where(hi, dya, 0.0)]
            prod = dya * ya_v
            drow = [jnp.sum(jnp.where(hi, 0.0, prod), axis=-1, keepdims=True), jnp.sum(jnp.where(hi, prod, 0.0), axis=-1, keepdims=True)]
            lse_v = lse_ref[0, pr]
            for jj in range(2):
                j = 2 * pr + jj
                sl = slice(LANES * j, LANES * (j + 1))
                q, k, kvj, s = _att_qk(qn_ref, qr_ref, kv_ref, kr_v, j)
                pn = jnp.exp(s - lse_v[:, MLA_V * jj:MLA_V * jj + 1])
                dob = d_out[jj].astype(BF16)
                ds = (pn * (_dot_nt(dob, kvj) - drow[jj])).astype(BF16)
                dq = _dot(ds, k)
                dqn_ref[0, :, sl] = jnp.where(hi, 0.0, dq[:, :LANES]).astype(BF16)
                dqr_ref[0, :, sl] = dq[:, LANES:].astype(BF16)
                dk = _dot_tn(ds, q)
                dkv_ref[0, :, sl] += dk[:, :LANES] + _dot_tn(pn.astype(BF16), dob)
                dkr_ref[0] += dk[:, LANES:]

    sem = _params(("parallel", "arbitrary", "arbitrary"), VMEM_LIMIT)
    qspec, kvspec, krspec, gspec, lspec = _att_specs(L, L, 1, PAIRS)
    main = pl.pallas_call(
        body, name=name, grid=(B, HP, NQ),
        in_specs=[qspec, qspec, kvspec, krspec, gspec, gspec, lspec, gspec] + (hosted.in_specs if hosted else []),
        out_specs=[qspec, qspec, kvspec, krspec, gspec] + (hosted.out_specs if hosted else []),
        out_shape=shapes + (hosted.out_shape if hosted else []), scratch_shapes=hosted.scratch if hosted else [],
        compiler_params=_params(("arbitrary",) * 3, VMEM_LIMIT) if hosted else sem,
    )(qn, qr, kv, kr, zg, ya, lse, dym, *(hosted.xs if hosted else []))
    qspec, kvspec, krspec, gspec, lspec = _att_specs(L, n_ctx, 0, PAIRS)
    outs = pl.pallas_call(
        body_ctx, name=name + "_ctx", grid=(B, HP, 1),
        in_specs=[qspec, qspec, kvspec, krspec, gspec, gspec, lspec, gspec, kvspec, krspec, _ANY, _ANY, _ANY],
        out_specs=[qspec, qspec, kvspec, krspec, gspec], out_shape=shapes,
        input_output_aliases={8: 2, 9: 3, 10: 0, 11: 1, 12: 4}, compiler_params=sem,
    )(qn, qr, kv, kr, zg, ya, lse, dym, main[2], main[3], main[0], main[1], main[4])
    return (*outs, list(main[5:]))


def _seg_bounds(rows, n_ctx, L):
    in_ctx = rows < n_ctx
    return jnp.where(in_ctx, 0, n_ctx), jnp.where(in_ctx, n_ctx, L)


def _window_sum(u, w, rows, lo, hi, mirror):
    L = u.shape[0]
    offs = range(-w // 2 + 1, w // 2 + 1) if mirror else range(-w // 2, w // 2)
    acc = None
    for d in offs:
        if d == 0:
            t = u
        else:
            src = rows + d
            t = jnp.where(jnp.logical_and(src >= lo, src < hi), pltpu.roll(u, (-d) % L, 0), 0.0)
        acc = t if acc is None else acc + t
    return acc


def _window_count(w, rows, lo, hi):
    pos = rows - lo
    return (jnp.minimum(pos + w // 2, hi - lo) - jnp.maximum(pos - w // 2, 0)).astype(F32)


def pool_fwd(px, pg, pw, ps, n_ctx, name):
    B, L, _ = px.shape

    def body(px_ref, pg_ref, pw_ref, ps_ref, y_ref):
        rows = lax.broadcasted_iota(jnp.int32, (L, POOL_GROUP), 0)
        lo, hi = _seg_bounds(rows, n_ctx, L)
        for gi, w in enumerate(POOL_WINDOWS):
            sl = slice(POOL_GROUP * gi, POOL_GROUP * (gi + 1))
            u = px_ref[0, :, sl]
            pooled = _window_sum(u, w, rows, lo, hi, False) / _window_count(w, rows, lo, hi) - u
            mixed = _dot(pooled.astype(BF16), pw_ref[gi])
            g = pg_ref[0, :, sl]
            y_ref[0, :, sl] = (mixed * ps_ref[:, sl] * (g * _sig(g))).astype(BF16)

    tok = pl.BlockSpec((1, L, POOL_WIDTH), lambda b: (b, 0, 0))
    return pl.pallas_call(
        body, name=name, grid=(B,),
        in_specs=[tok, tok, _full(pw.shape), _full(ps.shape)],
        out_specs=tok, out_shape=jax.ShapeDtypeStruct((B, L, POOL_WIDTH), BF16),
        compiler_params=_params(("parallel",), VMEM_LIMIT),
    )(px, pg, pw, ps)


def pool_bwd(px, pg, pw, ps, dy, n_ctx, name):
    B, L, _ = px.shape

    def body(px_ref, pg_ref, pw_ref, ps_ref, dy_ref, dpx_ref, dpg_ref, dpw_ref, dps_ref):
        first = pl.program_id(0) == 0
        rows = lax.broadcasted_iota(jnp.int32, (L, POOL_GROUP), 0)
        lo, hi = _seg_bounds(rows, n_ctx, L)
        for gi, w in enumerate(POOL_WINDOWS):
            sl = slice(POOL_GROUP * gi, POOL_GROUP * (gi + 1))
            u = px_ref[0, :, sl]
            cnt = _window_count(w, rows, lo, hi)
            pooled = (_window_sum(u, w, rows, lo, hi, False) / cnt - u).astype(BF16)
            mixed = _dot(pooled, pw_ref[gi])
            silu, dsilu = _silu_and_grad(pg_ref[0, :, sl])
            sc = ps_ref[:, sl]
            dyv = dy_ref[0, :, sl]
            _acc(dps_ref.at[:, sl], jnp.sum(dyv * mixed * silu, axis=0, keepdims=True), first)
            dpg_ref[0, :, sl] = (dyv * mixed * sc * dsilu).astype(BF16)
            dmixed = (dyv * sc * silu).astype(BF16)
            _acc(dpw_ref.at[gi], _dot_tn(pooled, dmixed), first)
            dpooled = _dot_nt(dmixed, pw_ref[gi])
            dpx_ref[0, :, sl] = (_window_sum(dpooled / cnt, w, rows, lo, hi, True) - dpooled).astype(BF16)

    tok = pl.BlockSpec((1, L, POOL_WIDTH), lambda b: (b, 0, 0))
    return pl.pallas_call(
        body, name=name, grid=(B,),
        in_specs=[tok, tok, _full(pw.shape), _full(ps.shape), tok],
        out_specs=[tok, tok, _full(pw.shape), _full(ps.shape)],
        out_shape=[jax.ShapeDtypeStruct((B, L, POOL_WIDTH), BF16)] * 2 + [jax.ShapeDtypeStruct(pw.shape, F32), jax.ShapeDtypeStruct(ps.shape, F32)],
        compiler_params=_params(("arbitrary",), VMEM_LIMIT),
    )(px, pg, pw, ps, dy)


_SCAN_STEPS = (1, 2, 4, 8, 16, 32)
SCAN_CHUNKS = 4


def _cum_fwd(x, r):
    for s in _SCAN_STEPS:
        x = x + jnp.where(r >= s, pltpu.roll(x, s, 0), 0.0)
    return x


def _cum_bwd(x, r):
    n = x.shape[0]
    for s in _SCAN_STEPS:
        x = x + jnp.where(r + s < GLA_CHUNK, pltpu.roll(x, n - s, 0), 0.0)
    return x


def _log_sigmoid(x):
    return jnp.minimum(x, 0.0) - jnp.log(1.0 + jnp.exp(-jnp.abs(x)))


def _gla_decays(lr, w_ref, b_ref, r, reverse):
    pre = _dot(lr, w_ref[...]) + b_ref[...]
    a = _log_sigmoid(pre) / GLA_TAU
    return pre, a, (_cum_bwd(a, r) if reverse else _cum_fwd(a, r)), _chunk_total(a)


def _chunk_total(x):
    x3 = x.reshape(x.shape[0] // GLA_CHUNK, GLA_CHUNK, x.shape[1])
    return jnp.broadcast_to(jnp.sum(x3, axis=1, keepdims=True), x3.shape).reshape(x.shape)


def gla_prep_fwd(zlr, zq, zk, waf, wab, baf, bab, name):
    T = zlr.shape[0]
    tb = _wide_block(T)

    def body(lr_ref, q_ref, k_ref, waf_ref, wab_ref, baf_ref, bab_ref, qf_ref, kf_ref, ksf_ref, tf_ref, qb_ref, kb_ref, ksb_ref, tb_ref):
        r = lax.broadcasted_iota(jnp.int32, (tb, GLA_KW), 0) % GLA_CHUNK
        lr = lr_ref[...].astype(BF16)
        q = q_ref[...] * GLA_DK ** -0.5
        k = k_ref[...]
        for rev, w_ref, b_ref, qo, ko, kso, to in ((False, waf_ref, baf_ref, qf_ref, kf_ref, ksf_ref, tf_ref),
                                                   (True, wab_ref, bab_ref, qb_ref, kb_ref, ksb_ref, tb_ref)):
            _, _, b, tot = _gla_decays(lr, w_ref, b_ref, r, rev)
            qo[...] = (q * jnp.exp(b)).astype(BF16)
            ko[...] = (k * jnp.exp(-b)).astype(BF16)
            kso[...] = (k * jnp.exp(tot - b)).astype(BF16)
            to[...] = tot

    tok = lambda n: pl.BlockSpec((tb, n), lambda i: (i, 0))
    outs = [jax.ShapeDtypeStruct((T, GLA_KW), BF16)] * 3 + [jax.ShapeDtypeStruct((T, GLA_KW), F32)]
    return pl.pallas_call(
        body, name=name, grid=(T // tb,),
        in_specs=[tok(LANES), tok(GLA_KW), tok(GLA_KW), _full(waf.shape), _full(wab.shape), _full(baf.shape), _full(bab.shape)],
        out_specs=[tok(GLA_KW)] * 8, out_shape=outs + outs,
        compiler_params=_params(("parallel",)),
    )(zlr, zq, zk, waf, wab, baf, bab)


def gla_prep_bwd(zlr, zq, zk, waf, wab, baf, bab, gf, gb, dvs, name):
    T = zlr.shape[0]
    tb = _wide_block(T)

    def body(lr_ref, q_ref, k_ref, waf_ref, wab_ref, baf_ref, bab_ref, dqf, dkf, dksf, ddf, dqb, dkb, dksb, ddb, dvf, dvb,
             dlr_ref, dq_ref, dk_ref, dwaf_ref, dwab_ref, dbaf_ref, dbab_ref, dv_ref):
        first = pl.program_id(0) == 0
        dv_ref[...] = (dvf[...] + dvb[...]).astype(BF16)
        r = lax.broadcasted_iota(jnp.int32, (tb, GLA_KW), 0) % GLA_CHUNK
        lr = lr_ref[...].astype(BF16)
        q = q_ref[...] * GLA_DK ** -0.5
        k = k_ref[...]
        dq_tot = None
        dk_tot = None
        dlr = None
        for rev, w_ref, b_ref, dqt, dkt, dks, ddec, dw_ref, db_ref in (
                (False, waf_ref, baf_ref, dqf, dkf, dksf, ddf, dwaf_ref, dbaf_ref),
                (True, wab_ref, bab_ref, dqb, dkb, dksb, ddb, dwab_ref, dbab_ref)):
            pre, _, b, tot = _gla_decays(lr, w_ref, b_ref, r, rev)
            e1 = jnp.exp(b)
            e2 = jnp.exp(-b)
            e3 = jnp.exp(tot - b)
            dqt_v = dqt[...]
            dkt_v = dkt[...]
            dks_v = dks[...]
            dq = dqt_v * e1
            dk = dkt_v * e2 + dks_v * e3
            g3 = dks_v * (k * e3)
            d_b = dqt_v * (q * e1) - dkt_v * (k * e2) - g3
            d_tot = _chunk_total(g3) + ddec[...] * jnp.exp(tot)
            da = (_cum_fwd(d_b, r) if rev else _cum_bwd(d_b, r)) + d_tot
            dpre = (da * (_sig(-pre) / GLA_TAU)).astype(BF16)
            t = _dot_nt(dpre, w_ref[...])
            dlr = t if dlr is None else dlr + t
            _acc(dw_ref, _dot_tn(lr, dpre), first)
            _acc(db_ref, jnp.sum(dpre.astype(F32), axis=0, keepdims=True), first)
            dq_tot = dq if dq_tot is None else dq_tot + dq
            dk_tot = dk if dk_tot is None else dk_tot + dk
        dlr_ref[...] = dlr.astype(BF16)
        dq_ref[...] = (dq_tot * GLA_DK ** -0.5).astype(BF16)
        dk_ref[...] = dk_tot.astype(BF16)

    tok = lambda n: pl.BlockSpec((tb, n), lambda i: (i, 0))
    return pl.pallas_call(
        body, name=name, grid=(T // tb,),
        in_specs=[tok(LANES), tok(GLA_KW), tok(GLA_KW), _full(waf.shape), _full(wab.shape), _full(baf.shape), _full(bab.shape)] + [tok(GLA_KW)] * 8
        + [tok(GLA_WIDTH)] * 2,
        out_specs=[tok(LANES), tok(GLA_KW), tok(GLA_KW), _full(waf.shape), _full(wab.shape), _full(baf.shape), _full(bab.shape), tok(GLA_WIDTH)],
        out_shape=[jax.ShapeDtypeStruct((T, LANES), BF16), jax.ShapeDtypeStruct((T, GLA_KW), BF16), jax.ShapeDtypeStruct((T, GLA_KW), BF16),
                   jax.ShapeDtypeStruct(waf.shape, F32), jax.ShapeDtypeStruct(wab.shape, F32), jax.ShapeDtypeStruct(baf.shape, F32),
                   jax.ShapeDtypeStruct(bab.shape, F32), jax.ShapeDtypeStruct((T, GLA_WIDTH), BF16)],
        compiler_params=_params(("arbitrary",)),
    )(zlr, zq, zk, waf, wab, baf, bab, *gf, *gb, *dvs)


def _chunk_order(nc, n_ctx_chunks, reverse):
    if not reverse:
        return lambda c: c
    return lambda c: jnp.where(c < n_ctx_chunks, n_ctx_chunks - 1 - c, nc + n_ctx_chunks - 1 - c)


def _tri_mask4(reverse):
    ri = lax.broadcasted_iota(jnp.int32, (GLA_CHUNK, GLA_HEADS * GLA_CHUNK), 0)
    ci = lax.broadcasted_iota(jnp.int32, (GLA_CHUNK, GLA_HEADS * GLA_CHUNK), 1) % GLA_CHUNK
    return (ri <= ci) if reverse else (ri >= ci)


def _block_diag(x, rb, cb):
    x4 = jnp.concatenate([x] * GLA_HEADS, axis=0)
    r = lax.broadcasted_iota(jnp.int32, x4.shape, 0) // rb
    c = lax.broadcasted_iota(jnp.int32, x4.shape, 1) // cb
    return jnp.where(r == c, x4, jnp.zeros_like(x4))


def _diag_blocks(f, rb, cb):
    c = lax.broadcasted_iota(jnp.int32, (rb, GLA_HEADS * cb), 1) // cb
    out = None
    for h in range(GLA_HEADS):
        t = jnp.where(c == h, f[rb * h:rb * (h + 1)], 0.0)
        out = t if out is None else out + t
    return out


def gla_scan_fwd(dirs, v, n_ctx, name):
    B, L, _ = v.shape
    C, G = GLA_CHUNK, SCAN_CHUNKS
    nc = L // C
    orders = [_chunk_order(nc // G, n_ctx // C // G, rev) for rev in (False, True)]

    def body(qf, kf, ksf, tf, vf, qb, kb, ksb, tb, vb, of, ssf, ob, ssb, stf, stb):
        @pl.when(pl.program_id(1) == 0)
        def _():
            stf[...] = jnp.zeros_like(stf)
            stb[...] = jnp.zeros_like(stb)

        for sub in range(G):
            step(qf, kf, ksf, vf, tf, of, ssf, stf, False, sub)
            step(qb, kb, ksb, vb, tb, ob, ssb, stb, True, G - 1 - sub)

    def step(q_ref, k_ref, ks_ref, v_ref, tot_ref, o_ref, ss_ref, st, reverse, sub):
        rows = slice(C * sub, C * (sub + 1))
        S = st[...]
        ss_ref[0, sub] = S
        q = q_ref[0, rows]
        v = v_ref[0, rows]
        k4 = _block_diag(k_ref[0, rows], GLA_CHUNK, GLA_DK)
        v4 = _block_diag(v.astype(BF16), GLA_CHUNK, GLA_DV)
        s4 = _block_diag(S.astype(BF16), GLA_DV, GLA_DK)
        P = jnp.where(_tri_mask4(reverse), _dot_nt(q, k4), 0.0)
        o_ref[0, rows] = _dot(P.astype(BF16), v4) + _dot_nt(q, s4)
        st[...] = jnp.exp(tot_ref[0, C * sub:C * sub + 1, :]) * S + _diag_blocks(_dot(v.T.astype(BF16), ks_ref[0, rows]), GLA_DV, GLA_DK)

    in_specs, out_specs, out_shape = [], [], []
    for order in orders:
        tok = lambda n, order=order: pl.BlockSpec((1, G * C, n), lambda b, c: (b, order(c), 0))
        in_specs += [tok(GLA_KW), tok(GLA_KW), tok(GLA_KW), tok(GLA_KW), tok(GLA_WIDTH)]
        out_specs += [tok(GLA_WIDTH), pl.BlockSpec((1, G, GLA_DV, GLA_KW), lambda b, c, order=order: (b, order(c), 0, 0))]
        out_shape += [jax.ShapeDtypeStruct((B, L, GLA_WIDTH), F32), jax.ShapeDtypeStruct((B, nc, GLA_DV, GLA_KW), F32)]
    outs = pl.pallas_call(
        body, name=name, grid=(B, nc // G), in_specs=in_specs, out_specs=out_specs, out_shape=out_shape,
        scratch_shapes=[pltpu.VMEM((GLA_DV, GLA_KW), F32)] * 2,
        compiler_params=_params(("parallel", "arbitrary")),
    )(*dirs[0], v, *dirs[1], v)
    return outs[:2], outs[2:]


def gla_scan_bwd(dirs, v, do, n_ctx, name, hosted=None):
    B, L, _ = v.shape
    C, G = GLA_CHUNK, SCAN_CHUNKS
    nc = L // C
    npair = nc // G
    orders = []
    for rev in (False, True):
        fwd_order = _chunk_order(npair, n_ctx // C // G, rev)
        orders.append(lambda c, fwd_order=fwd_order: fwd_order(npair - 1 - c))

    nx = hosted.n if hosted else 0

    def body(*refs):
        qf, kf, ksf, tf, ssf, vf, dof, qb, kb, ksb, tb, ssb, vb, dob = refs[:14]
        dqf, dkf, dksf, dvf, ddf, dqb, dkb, dksb, dvb, ddb = refs[14 + nx:24 + nx]
        dstf, dstb = refs[24 + 2 * nx:26 + 2 * nx]
        if hosted:
            hosted.run(refs[14:14 + nx], refs[24 + nx:24 + 2 * nx], refs[26 + 2 * nx:], pl.program_id(0) * npair + pl.program_id(1), B * npair)

        @pl.when(pl.program_id(1) == 0)
        def _():
            dstf[...] = jnp.zeros_like(dstf)
            dstb[...] = jnp.zeros_like(dstb)

        for sub in range(G):
            step(qf, kf, ksf, vf, tf, ssf, dof, dqf, dkf, dksf, dvf, ddf, dstf, False, G - 1 - sub)
            step(qb, kb, ksb, vb, tb, ssb, dob, dqb, dkb, dksb, dvb, ddb, dstb, True, sub)

    def step(q_ref, k_ref, ks_ref, v_ref, tot_ref, ss_ref, do_ref, dq_ref, dk_ref, dks_ref, dv_ref, dd_ref, dst, reverse, sub):
        rows = slice(C * sub, C * (sub + 1))
        dSn = dst[...]
        S = ss_ref[0, sub]
        q = q_ref[0, rows]
        vb = v_ref[0, rows].astype(BF16)
        dob = do_ref[0, rows].astype(BF16)
        k4 = _block_diag(k_ref[0, rows], GLA_CHUNK, GLA_DK)
        v4 = _block_diag(vb, GLA_CHUNK, GLA_DV)
        s4 = _block_diag(S.astype(BF16), GLA_DV, GLA_DK)
        ds4 = _block_diag(dSn.astype(BF16), GLA_DV, GLA_DK)
        tri = _tri_mask4(reverse)
        P = jnp.where(tri, _dot_nt(q, k4), 0.0).astype(BF16)
        dP = jnp.where(tri, _dot_nt(dob, v4), 0.0).astype(BF16)
        dq_ref[0, rows] = _dot(dob, s4) + _dot(dP, k4)
        dk_ref[0, rows] = _diag_blocks(_dot_tn(dP, q), GLA_CHUNK, GLA_DK)
        dv_ref[0, rows] = _diag_blocks(_dot_tn(P, dob), GLA_CHUNK, GLA_DV) + _dot_nt(ks_ref[0, rows], ds4)
        dks_ref[0, rows] = _dot(vb, ds4)
        dd_ref[0, rows] = jnp.broadcast_to(jnp.sum(dSn * S, axis=0, keepdims=True), (C, GLA_KW))
        dst[...] = jnp.exp(tot_ref[0, C * sub:C * sub + 1, :]) * dSn + _diag_blocks(_dot_tn(dob, q), GLA_DV, GLA_DK)

    in_specs, out_specs, out_shape = [], [], []
    for order in orders:
        tok = lambda n, order=order: pl.BlockSpec((1, G * C, n), lambda b, c: (b, order(c), 0))
        in_specs += [tok(GLA_KW), tok(GLA_KW), tok(GLA_KW), tok(GLA_KW),
                     pl.BlockSpec((1, G, GLA_DV, GLA_KW), lambda b, c, order=order: (b, order(c), 0, 0)), tok(GLA_WIDTH), tok(GLA_WIDTH)]
        out_specs += [tok(GLA_KW), tok(GLA_KW), tok(GLA_KW), tok(GLA_WIDTH), tok(GLA_KW)]
        out_shape += [jax.ShapeDtypeStruct((B, L, GLA_KW), F32)] * 3 + [jax.ShapeDtypeStruct((B, L, GLA_WIDTH), F32), jax.ShapeDtypeStruct((B, L, GLA_KW), F32)]
    outs = pl.pallas_call(
        body, name=name, grid=(B, npair), in_specs=in_specs + (hosted.in_specs if hosted else []),
        out_specs=out_specs + (hosted.out_specs if hosted else []), out_shape=out_shape + (hosted.out_shape if hosted else []),
        scratch_shapes=[pltpu.VMEM((GLA_DV, GLA_KW), F32)] * 2 + (hosted.scratch if hosted else []),
        compiler_params=_params(("arbitrary", "arbitrary") if hosted else ("parallel", "arbitrary")),
    )(*dirs[0], v, do, *dirs[1], v, do, *(hosted.xs if hosted else []))
    return outs[:5], outs[5:10], list(outs[10:])


def gla_out_fwd(of, ob, gn, zg, name):
    T = of.shape[0]

    def body(of_ref, ob_ref, gn_ref, g_ref, y_ref):
        for h in range(GLA_HEADS):
            sl = slice(GLA_DV * h, GLA_DV * (h + 1))
            xn, _ = _rms_rows(of_ref[:, sl] + ob_ref[:, sl])
            g = g_ref[:, sl]
            y_ref[:, sl] = (xn * gn_ref[...] * (g * _sig(g))).astype(BF16)

    tb = _wide_block(T)
    tok = pl.BlockSpec((tb, GLA_WIDTH), lambda i: (i, 0))
    return pl.pallas_call(
        body, name=name, grid=(T // tb,),
        in_specs=[tok, tok, _full(gn.shape), tok], out_specs=tok,
        out_shape=jax.ShapeDtypeStruct((T, GLA_WIDTH), BF16),
        compiler_params=_params(("parallel",)),
    )(of, ob, gn, zg)


def gla_out_bwd(of, ob, gn, zg, dy, name):
    T = of.shape[0]

    def body(of_ref, ob_ref, gn_ref, g_ref, dy_ref, do_ref, dzg_ref, dgn_ref):
        first = pl.program_id(0) == 0
        gn_v = gn_ref[...]
        dgn = None
        for h in range(GLA_HEADS):
            sl = slice(GLA_DV * h, GLA_DV * (h + 1))
            xn, r = _rms_rows(of_ref[:, sl] + ob_ref[:, sl])
            silu, dsilu = _silu_and_grad(g_ref[:, sl])
            dyv = dy_ref[:, sl]
            dzg_ref[:, sl] = (dyv * xn * gn_v * dsilu).astype(BF16)
            dn = dyv * silu
            t = jnp.sum(dn * xn, axis=0, keepdims=True)
            dgn = t if dgn is None else dgn + t
            do_ref[:, sl] = _rms_rows_bwd(dn * gn_v, xn, r)
        _acc(dgn_ref, dgn, first)

    tb = _wide_block(T)
    tok = pl.BlockSpec((tb, GLA_WIDTH), lambda i: (i, 0))
    return pl.pallas_call(
        body, name=name, grid=(T // tb,),
        in_specs=[tok, tok, _full(gn.shape), tok, tok], out_specs=[tok, tok, _full(gn.shape)],
        out_shape=[jax.ShapeDtypeStruct((T, GLA_WIDTH), F32), jax.ShapeDtypeStruct((T, GLA_WIDTH), BF16), jax.ShapeDtypeStruct(gn.shape, F32)],
        compiler_params=_params(("arbitrary",)),
    )(of, ob, gn, zg, dy)


def merge_post_fwd(ys, zm, wbs, wo, x2, pg, ms, nb, name):
    T = x2.shape[0]

    def body(y0, y1, y2, zm_ref, w0, w1, w2, wo_ref, x_ref, pg_ref, gate_ref, xn_ref, out_ref, mg_ref):
        merged = None
        for i, (y_ref, w_ref) in enumerate(((y0, w0), (y1, w1), (y2, w2))):
            t = _sig(zm_ref[:, D_MODEL * i:D_MODEL * (i + 1)].astype(F32)) * _dot(y_ref[...], w_ref[...])
            merged = t if merged is None else merged + t
        mb = merged.astype(BF16)
        mg_ref[...] = mb
        out = _dot(mb, wo_ref[...])
        out_ref[...] = out
        on, _ = _rms_rows(out)
        xn_ref[...] = x_ref[...] + gate_ref[0] * (on * pg_ref[...])

    tok = lambda n: pl.BlockSpec((TOKEN_BLOCK, n), lambda i: (i, 0))
    return pl.pallas_call(
        body, name=name, grid=(T // TOKEN_BLOCK,),
        in_specs=[tok(512)] * 3 + [tok(3 * D_MODEL)] + [_full(w.shape) for w in wbs] + [_full(wo.shape), tok(D_MODEL), _full(pg.shape), _mod_spec(nb, 2)],
        out_specs=[tok(D_MODEL)] * 3,
        out_shape=[jax.ShapeDtypeStruct((T, D_MODEL), F32), jax.ShapeDtypeStruct((T, D_MODEL), F32), jax.ShapeDtypeStruct((T, D_MODEL), BF16)],
        compiler_params=_params(("parallel",), VMEM_LIMIT),
    )(*ys, zm, *wbs, wo, x2, pg, ms)


def merge_post_bwd(dxn, out, ys, zm, wbs, wo, pg, ms, nb, name):
    T = dxn.shape[0]
    nrow = ms.shape[0]
    row = _mod_row(nb)

    def body(dxn_ref, out_ref, y0, y1, y2, zm_ref, w0, w1, w2, wo_ref, pg_ref, gate_ref,
             dy0, dy1, dy2, dzm_ref, dout_ref, dp0, dp1, dp2, dgate_ref, dpg_ref):
        i = pl.program_id(0)
        dxn_v = dxn_ref[...]
        on, r = _rms_rows(out_ref[...])
        pg_v = pg_ref[...]
        _acc(dgate_ref.at[0], jnp.sum(dxn_v * on * pg_v, axis=0, keepdims=True), (i % nb) <= 1)
        dn = dxn_v * gate_ref[0]
        _acc(dpg_ref, jnp.sum(dn * on, axis=0, keepdims=True), i == 0)
        dout = _rms_rows_bwd(dn * pg_v, on, r).astype(BF16)
        dout_ref[...] = dout
        dmerged = _dot_nt(dout, wo_ref[...])
        for j, (y_ref, w_ref, dy_ref, dp_ref) in enumerate(((y0, w0, dy0, dp0), (y1, w1, dy1, dp1), (y2, w2, dy2, dp2))):
            sl = slice(D_MODEL * j, D_MODEL * (j + 1))
            g = _sig(zm_ref[:, sl].astype(F32))
            p = _dot(y_ref[...], w_ref[...])
            dzm_ref[:, sl] = (dmerged * p * g * (1.0 - g)).astype(BF16)
            dp = (dmerged * g).astype(BF16)
            dp_ref[...] = dp
            dy_ref[...] = _dot_nt(dp, w_ref[...])

    tok = lambda n: pl.BlockSpec((TOKEN_BLOCK, n), lambda i: (i, 0))
    return pl.pallas_call(
        body, name=name, grid=(T // TOKEN_BLOCK,),
        in_specs=[tok(D_MODEL), tok(D_MODEL)] + [tok(512)] * 3 + [tok(3 * D_MODEL)] + [_full(w.shape) for w in wbs] + [_full(wo.shape), _full(pg.shape), _mod_spec(nb, 2)],
        out_specs=[tok(512)] * 3 + [tok(3 * D_MODEL), tok(D_MODEL)] + [tok(D_MODEL)] * 3 + [pl.BlockSpec((1, 1, D_MODEL), lambda i: (row(i), 0, 0)), _full(pg.shape)],
        out_shape=[jax.ShapeDtypeStruct((T, 512), F32)] * 3 + [jax.ShapeDtypeStruct((T, 3 * D_MODEL), BF16), jax.ShapeDtypeStruct((T, D_MODEL), BF16)]
        + [jax.ShapeDtypeStruct((T, D_MODEL), BF16)] * 3 + [jax.ShapeDtypeStruct((nrow, 1, D_MODEL), F32), jax.ShapeDtypeStruct(pg.shape, F32)],
        compiler_params=_params(("arbitrary",), VMEM_LIMIT),
    )(dxn, out, *ys, zm, *wbs, wo, pg, ms)


def loss_head(y2, tgt2, nb, name):
    T = y2.shape[0]
    nlat = nb - 1

    def body(y_ref, t_ref, dy_ref, loss_ref, acc):
        i = pl.program_id(0)
        is_lat = (i % nb) > 0

        @pl.when(i == 0)
        def _():
            acc[...] = jnp.zeros_like(acc)

        @pl.when(is_lat)
        def _():
            e = y_ref[...] - t_ref[...]
            dy_ref[...] = e * (1.0 / D_MODEL)
            acc[...] += jnp.sum(e * e, axis=0, keepdims=True)

        @pl.when(jnp.logical_not(is_lat))
        def _():
            dy_ref[...] = jnp.zeros_like(dy_ref)

        @pl.when(i == pl.num_programs(0) - 1)
        def _():
            loss_ref[...] = jnp.sum(acc[...], axis=1, keepdims=True) * (0.5 / D_MODEL)

    tok = pl.BlockSpec((TOKEN_BLOCK, D_MODEL), lambda i: (i, 0))
    tgt = pl.BlockSpec((TOKEN_BLOCK, D_MODEL), lambda i: ((i // nb) * nlat + jnp.maximum(i % nb - 1, 0), 0))
    return pl.pallas_call(
        body, name=name, grid=(T // TOKEN_BLOCK,),
        in_specs=[tok, tgt], out_specs=[tok, _full((1, 1))],
        out_shape=[jax.ShapeDtypeStruct((T, D_MODEL), F32), jax.ShapeDtypeStruct((1, 1), F32)],
        scratch_shapes=[pltpu.VMEM((1, D_MODEL), F32)],
        compiler_params=_params(("arbitrary",)),
    )(y2, tgt2)


_IN_OFFS = tuple(int(o) for o in np.cumsum((0,) + IN_SIZES))
_IN_GROUPS = (("a", 0, 416, 512), ("mg", 416, 512, 512), ("px", 928, 512, 512), ("pg", 1440, 512, 512), ("gq", 1952, 256, 256),
              ("gk", 2208, 256, 256), ("gv", 2464, 512, 512), ("lr", 2976, 32, 128), ("gg", 3008, 512, 512), ("m", 3520, 3072, 3072))


def _pad_cols(w, n):
    return w if w.shape[1] == n else jnp.pad(w, ((0, 0), (0, n - w.shape[1])))


def layer_weights(w_in, w_uq, w_ukv, af_w2, ab_w2, wbm, wbp, wbg, w_out):
    W = {}
    for nm, off, n, npad in _IN_GROUPS:
        W["in_" + nm] = _pad_cols(w_in[:, off:off + n], npad)
    uq = w_uq.reshape(MLA_Q_RANK, MLA_HEADS, MLA_NOPE + MLA_ROPE)
    W["qn"] = jnp.pad(uq[:, :, :MLA_NOPE], ((0, 0), (0, 0), (0, LANES - MLA_NOPE))).reshape(MLA_Q_RANK, MLA_HEADS * LANES)
    W["qr"] = jnp.pad(uq[:, :, MLA_NOPE:], ((0, 0), (0, 0), (0, LANES - MLA_ROPE))).reshape(MLA_Q_RANK, MLA_HEADS * LANES)
    W["kv"] = w_ukv
    W["af"] = jnp.pad(af_w2, ((0, LANES - GLA_GATE_RANK), (0, 0)))
    W["ab"] = jnp.pad(ab_w2, ((GLA_GATE_RANK, LANES - 2 * GLA_GATE_RANK), (0, 0)))
    W["bm"], W["bp"], W["bg"], W["out"] = wbm, wbp, wbg, w_out
    return W


def rope_tables(L, n_ctx):
    t = np.arange(L - n_ctx)
    half = MLA_ROPE // 2
    inv = ROPE_BASE ** (-np.arange(0, half, 2, dtype=np.float32) / half)
    ang_r = (t // GRID_W).astype(np.float32)[:, None] * inv
    ang_c = (t % GRID_W).astype(np.float32)[:, None] * inv
    ang = jnp.asarray(np.concatenate([ang_r, ang_r, ang_c, ang_c], axis=-1), F32)
    cos = jnp.ones((L, LANES), F32).at[n_ctx:, :MLA_ROPE].set(jnp.cos(ang))
    sin = jnp.zeros((L, LANES), F32).at[n_ctx:, :MLA_ROPE].set(jnp.sin(ang))
    return cos, sin


def layer_fwd(x2, ms, W, P, cos, sin, B, L, n_ctx, tag, hosted=None):
    nb = L // TOKEN_BLOCK
    r3 = lambda a: a.reshape(B, L, a.shape[-1])
    r2 = lambda a: a.reshape(B * L, a.shape[-1])
    names = [g[0] for g in _IN_GROUPS[:-1]]
    h, zs = norm_in_proj(x2, P["pre"], ms, [W["in_" + n] for n in names], nb, tag + "in_proj")
    z = dict(zip(names, zs))
    (z["m"],) = mm_multi(h, [W["in_m"]], [BF16], tag + "in_proj_merge", tm=_wide_block(L))
    qn, qr, kv, kr = mla_prep_fwd(z["a"], P["qg"], P["kvg"], W["qn"], W["qr"], W["kv"], cos, sin, nb, tag + "mla_prep")
    ya, y_mla, lse, carried = attention_fwd(r3(qn), r3(qr), r3(kv), r3(kr), r3(z["mg"]), n_ctx, tag + "attention", hosted)
    y_pool = pool_fwd(r3(z["px"]), r3(z["pg"]), P["pw"], P["ps"], n_ctx, tag + "pool")
    qf, kf, ksf, tf, qb, kb, ksb, tb = gla_prep_fwd(z["lr"], z["gq"], z["gk"], W["af"], W["ab"], P["baf"], P["bab"], tag + "gla_prep")
    (of, ssf), (ob, ssb) = gla_scan_fwd([(r3(qf), r3(kf), r3(ksf), r3(tf)), (r3(qb), r3(kb), r3(ksb), r3(tb))], r3(z["gv"]), n_ctx, tag + "gla_scan")
    y_gla = gla_out_fwd(r2(of), r2(ob), P["gn"], z["gg"], tag + "gla_out")
    ys = [r2(y_mla), r2(y_pool), y_gla]
    x_new, out, merged = merge_post_fwd(ys, z["m"], [W["bm"], W["bp"], W["bg"]], W["out"], x2, P["post"], ms, nb, tag + "merge_post")
    res = dict(x2=x2, h=h, z=z, qn=qn, qr=qr, kv=kv, kr=kr, ya=ya, lse=lse, ys=ys, gla_f=(qf, kf, ksf, tf, ssf), gla_b=(qb, kb, ksb, tb, ssb),
               of=of, ob=ob, out=out, merged=merged)
    return x_new, res, carried


def layer_bwd(dxn, res, ms, W, P, cos, sin, B, L, n_ctx, tag, hosted=None, host_own=None, latent_only=False):
    nb = L // TOKEN_BLOCK
    r3 = lambda a: a.reshape(B, L, a.shape[-1])
    r2 = lambda a: a.reshape(B * L, a.shape[-1])
    z = res["z"]
    ys = res["ys"]
    wbs = [W["bm"], W["bp"], W["bg"]]
    dy0, dy1, dy2, dzm, dout, dp0, dp1, dp2, dgate, dpost = merge_post_bwd(dxn, res["out"], ys, z["m"], wbs, W["out"], P["post"], ms, nb, tag + "merge_post_bwd")
    G = {"out": mm_dw(res["merged"], dout, tag + "dw_out"), "post": dpost}
    for nm, y, dp in zip(("bm", "bp", "bg"), ys, (dp0, dp1, dp2)):
        G[nm] = mm_dw(y, dp, tag + "dw_" + nm)
    g = {n: _natural_grad(G, n) for n in GRADS_EARLY}
    own = {}
    dz = {"m": dzm}
    do, dz["gg"], G["gn"] = gla_out_bwd(r2(res["of"]), r2(res["ob"]), P["gn"], z["gg"], dy2, tag + "gla_out_bwd")
    carrier = host_own(GRADS_EARLY, g) if host_own else None
    *grads, arrived = gla_scan_bwd([(r3(qt), r3(kt), r3(ks), r3(tot), ss) for qt, kt, ks, tot, ss in (res["gla_f"], res["gla_b"])],
                                   r3(z["gv"]), r3(do), n_ctx, tag + "gla_scan_bwd", carrier)
    own.update(zip(GRADS_EARLY, arrived))
    gf = [r2(a) for a in grads[0]]
    gb = [r2(a) for a in grads[1]]
    dz["lr"], dz["gq"], dz["gk"], G["af"], G["ab"], G["baf"], G["bab"], dz["gv"] = gla_prep_bwd(
        z["lr"], z["gq"], z["gk"], W["af"], W["ab"], P["baf"], P["bab"], gf[:3] + gf[4:], gb[:3] + gb[4:], [gf[3], gb[3]], tag + "gla_prep_bwd")
    dpx, dpg, G["pw"], G["ps"] = pool_bwd(r3(z["px"]), r3(z["pg"]), P["pw"], P["ps"], r3(dy1), n_ctx, tag + "pool_bwd")
    dz["px"], dz["pg"] = r2(dpx), r2(dpg)
    dqn, dqr, dkv, dkr, dzmg, got = attention_bwd(r3(res["qn"]), r3(res["qr"]), r3(res["kv"]), r3(res["kr"]), r3(z["mg"]), res["ya"], res["lse"],
                                                  r3(dy0), n_ctx, tag + "attention_bwd", hosted)
    dz["mg"] = r2(dzmg)
    dz["a"], G["qn"], G["qr"], G["kv"], G["qg"], G["kvg"] = mla_prep_bwd(
        r2(dqn), r2(dqr), r2(dkv), r2(dkr), z["a"], P["qg"], P["kvg"], W["qn"], W["qr"], W["kv"], cos, sin, nb, tag + "mla_prep_bwd")
    names = [grp[0] for grp in _IN_GROUPS]
    for n in names:
        G["in_" + n] = mm_dw(res["h"], dz[n], tag + "dw_in_" + n)
    g.update({n: _natural_grad(G, n) for n in GRADS_LATE})
    carrier = host_own(GRADS_LATE, g) if host_own else None
    (dx, dshift, dscale, G["pre"]), arrived = in_proj_norm_bwd([dz[n] for n in names], [W["in_" + n] for n in names], res["x2"], P["pre"], ms, dxn, nb,
                                                               tag + "in_proj_dx", carrier, latent_only)
    own.update(zip(GRADS_LATE, arrived))
    g.update({n: _natural_grad(G, n) for n in GRADS_REPLICATED})
    dms = jnp.concatenate([dshift, dscale, dgate], axis=-1)
    return dx, g, dms, got, own


GRADS_EARLY = ("w_branch_mla", "w_branch_pool", "w_branch_gla", "w_out")
GRADS_LATE = ("w_in", "mla_w_uq", "mla_w_ukv", "gla_af_w2", "gla_ab_w2")
GRADS_REPLICATED = ("pre_norm", "post_norm", "mla_q_norm", "mla_kv_norm", "pool_w", "pool_scale", "gla_af_b", "gla_ab_b", "gla_norm")
_DIRECT = dict(mla_w_ukv="kv", w_branch_mla="bm", w_branch_pool="bp", w_branch_gla="bg", w_out="out", pool_w="pw")
_ROW = dict(pre_norm="pre", post_norm="post", mla_q_norm="qg", mla_kv_norm="kvg", pool_scale="ps", gla_af_b="baf", gla_ab_b="bab", gla_norm="gn")


def _natural_grad(G, name):
    if name == "w_in":
        parts = {off: G["in_" + nm][:, :n] for nm, off, n, npad in _IN_GROUPS}
        return jnp.concatenate([parts[o] for o in sorted(parts)], axis=1)
    if name == "mla_w_uq":
        gqn = G["qn"].reshape(MLA_Q_RANK, MLA_HEADS, LANES)[:, :, :MLA_NOPE]
        gqr = G["qr"].reshape(MLA_Q_RANK, MLA_HEADS, LANES)[:, :, :MLA_ROPE]
        return jnp.concatenate([gqn, gqr], axis=-1).reshape(MLA_Q_RANK, MLA_HEADS * (MLA_NOPE + MLA_ROPE))
    if name == "gla_af_w2":
        return G["af"][:GLA_GATE_RANK]
    if name == "gla_ab_w2":
        return G["ab"][GLA_GATE_RANK:2 * GLA_GATE_RANK]
    return G[_DIRECT[name]] if name in _DIRECT else G[_ROW[name]][0]


def local_step(x, c, ctx, c_ctx, small, loss_target, depth, layer_full, host_fwd=None, host_bwd=None, host_own=None):
    B, S, _ = x.shape
    n_ctx = ctx.shape[1]
    L = n_ctx + S
    nb = L // TOKEN_BLOCK
    cos, sin = rope_tables(L, n_ctx)
    x2 = jnp.concatenate([ctx, x], axis=1).reshape(B * L, D_MODEL)
    a8 = jnp.zeros((8, D_MODEL), F32).at[:B].set(c).at[B].set(c_ctx)
    Ws, Ps, mss, ress, mod_ws = [], [], [], [], []
    carried = None
    for l in range(depth):
        tag = f"l{l}_"
        full = layer_full(l, carried)
        W = layer_weights(full["w_in"], full["mla_w_uq"], full["mla_w_ukv"], full["gla_af_w2"], full["gla_ab_w2"],
                          full["w_branch_mla"], full["w_branch_pool"], full["w_branch_gla"], full["w_out"])
        P = dict(pre=small["pre_norm"][l][None], post=small["post_norm"][l][None], qg=small["mla_q_norm"][l][None], kvg=small["mla_kv_norm"][l][None],
                 pw=small["pool_w"][l].astype(BF16), ps=small["pool_scale"][l][None], baf=small["gla_af_b"][l][None], bab=small["gla_ab_b"][l][None],
                 gn=small["gla_norm"][l][None])
        mod8 = mod_fwd(a8, full["mod_w"], small["mod_b"][l][None], tag + "mod")
        ms = jnp.stack([jnp.broadcast_to(mod8[B], (B, 3 * D_MODEL)), mod8[:B]], axis=1).reshape(2 * B, 1, 3 * D_MODEL)
        x2, res, carried = layer_fwd(x2, ms, W, P, cos, sin, B, L, n_ctx, tag, host_fwd(l) if host_fwd else None)
        Ws.append(W), Ps.append(P), mss.append(ms), ress.append(res), mod_ws.append(full["mod_w"])
    dx, loss = loss_head(x2, loss_target.reshape(B * S, D_MODEL), nb, "loss_head")
    grads = [None] * depth
    delivered = [None] * depth
    dz8s = [None] * depth
    da8 = None
    for l in reversed(range(depth)):
        tag = f"l{l}_"
        hosted = host_bwd(l, grads[l + 1]) if host_bwd and l + 1 < depth else None
        dx, g, dms, got, own = layer_bwd(dx, ress[l], mss[l], Ws[l], Ps[l], cos, sin, B, L, n_ctx, tag, hosted, host_own(l) if host_own else None,
                                         latent_only=l == 0)
        if hosted:
            delivered[l + 1] = got
        if own:
            delivered[l] = own
        dms = dms.reshape(B, 2, 3 * D_MODEL)
        dz8s[l] = jnp.zeros((8, 3 * D_MODEL), F32).at[:B].set(dms[:, 1]).at[B].set(jnp.sum(dms[:, 0], axis=0))
        g_mod_b, da = mod_bwd(a8, mod_ws[l], dz8s[l], tag + "mod_bwd")
        da8 = da if da8 is None else da8 + da
        g["mod_b"] = g_mod_b[0]
        grads[l] = g
    grad_x = dx.reshape(B, S, D_MODEL)
    return loss, grad_x, grads, da8[B], delivered, (a8, dz8s)


_MESH_ID = pl.DeviceIdType.MESH
_HBM = pl.BlockSpec(memory_space=pltpu.HBM)


def _me_and_peers():
    mx, my, mc = lax.axis_index("x"), lax.axis_index("y"), lax.axis_index("c")
    peers = []
    for k in range(1, N_DEV):
        px, py, pc = mx ^ ((k >> 2) & 1), my ^ ((k >> 1) & 1), mc ^ (k & 1)
        peers.append(((px, py, pc), 4 * px + 2 * py + pc))
    return 4 * mx + 2 * my + mc, peers


def _comm_scratch(n):
    return [pltpu.SemaphoreType.DMA((n * (N_DEV - 1),)), pltpu.SemaphoreType.DMA((n * (N_DEV - 1),)), pltpu.SemaphoreType.DMA((n,))]


class _Gather:
    def __init__(self, x_refs, o_refs, send_sems, recv_sems, local_sems):
        self.x, self.o, self.send, self.recv, self.local = x_refs, o_refs, send_sems, recv_sems, local_sems
        self.n = len(x_refs)
        mx, my, mc = lax.axis_index("x"), lax.axis_index("y"), lax.axis_index("c")
        self.me, self.sibling, self.mc = (mx, my, mc), (mx, my, 1 - mc), mc
        self.chips = [(1 - mx, my), (mx, 1 - my), (1 - mx, 1 - my)]

    @staticmethod
    def out_shape(xs):
        return [jax.ShapeDtypeStruct((N_DEV,) + x.shape, x.dtype) for x in xs]

    def _copy(self, i, k, block, to, src=None):
        px, py, pc = block
        dst = self.o[i].at[4 * px + 2 * py + pc]
        sem = (N_DEV - 1) * i + k
        return pltpu.make_async_remote_copy(src_ref=dst if src is None else src, dst_ref=dst, send_sem=self.send.at[sem],
                                            recv_sem=self.recv.at[sem], device_id=to, device_id_type=_MESH_ID)

    def _mine(self, i):
        mx, my, mc = self.me
        return pltpu.make_async_copy(self.x[i], self.o[i].at[4 * mx + 2 * my + mc], self.local.at[i])

    def _first(self):
        out = []
        for i in range(self.n):
            out.append(self._copy(i, 0, self.me, self.sibling, src=self.x[i]))
            out += [self._copy(i, 1 + j, self.me, (*chip, self.mc), src=self.x[i]) for j, chip in enumerate(self.chips)]
        return out

    def _passed(self, j, i):
        return self._copy(i, 4 + j, (*self.chips[j], self.mc), self.sibling)

    def start(self):
        for i in range(self.n):
            self._mine(i).start()
        for cp in self._first():
            cp.start()

    def forward(self):
        for j, chip in enumerate(self.chips):
            for i in range(self.n):
                self._copy(i, 1 + j, (*chip, self.mc), self.me).wait_recv()
                self._passed(j, i).start()

    def finish(self):
        for i in range(self.n):
            self._copy(i, 0, self.sibling, self.me).wait_recv()
            for j, chip in enumerate(self.chips):
                self._copy(i, 4 + j, (*chip, 1 - self.mc), self.me).wait_recv()
        for cp in self._first():
            cp.wait_send()
        for j in range(len(self.chips)):
            for i in range(self.n):
                self._passed(j, i).wait_send()
        for i in range(self.n):
            self._mine(i).wait()


class _Scatter:
    def __init__(self, x_refs, o_refs, send_sems, recv_sems, local_sems):
        self.x, self.o, self.send, self.recv, self.local = x_refs, o_refs, send_sems, recv_sems, local_sems
        self.n = len(x_refs)
        self.me, self.peers = _me_and_peers()

    @staticmethod
    def out_shape(xs):
        return [jax.ShapeDtypeStruct(x.shape, x.dtype) for x in xs]

    def _copy(self, i, k, src_slot, dst_slot, to):
        sem = (N_DEV - 1) * i + k
        return pltpu.make_async_remote_copy(src_ref=self.x[i].at[src_slot], dst_ref=self.o[i].at[dst_slot], send_sem=self.send.at[sem],
                                            recv_sem=self.recv.at[sem], device_id=to, device_id_type=_MESH_ID)

    def _mine(self, i):
        return pltpu.make_async_copy(self.x[i].at[self.me], self.o[i].at[self.me], self.local.at[i])

    def _sends(self):
        return [self._copy(i, k, slot, self.me, peer) for k, (peer, slot) in enumerate(self.peers) for i in range(self.n)]

    def start(self):
        for i in range(self.n):
            self._mine(i).start()
        for cp in self._sends():
            cp.start()

    def forward(self):
        pass

    def finish(self):
        for k, (peer, slot) in enumerate(self.peers):
            for i in range(self.n):
                self._copy(i, k, slot, slot, peer).wait_recv()
        for cp in self._sends():
            cp.wait_send()
        for i in range(self.n):
            self._mine(i).wait()


class _Hosted:
    def __init__(self, kind, xs):
        self.kind, self.xs, self.n = kind, list(xs), len(xs)
        self.in_specs = [_HBM] * self.n
        self.out_specs = [_HBM] * self.n
        self.out_shape = kind.out_shape(self.xs)
        self.scratch = _comm_scratch(self.n)

    def run(self, x_refs, o_refs, sems, step, total):
        for when, phase in ((0, "start"), (3 * total // 4, "forward"), (total - 1, "finish")):
            @pl.when(step == when)
            def _(phase=phase):
                getattr(self.kind(x_refs, o_refs, *sems), phase)()


def gather_blocks(xs, name):
    n = len(xs)

    def body(*refs):
        g = _Gather(refs[:n], refs[n:2 * n], *refs[2 * n:])
        g.start()
        g.forward()
        g.finish()

    return pl.pallas_call(
        body, name=name, in_specs=[_HBM] * n, out_specs=[_HBM] * n,
        out_shape=_Gather.out_shape(xs), scratch_shapes=_comm_scratch(n),
    )(*xs)


def reduce_adamw(slots, w, m, v, name, tr=256):
    nl, R, C = w.shape
    ns = slots[0].shape[0]
    tr = min(tr, R)
    c1 = 1.0 / (1.0 - ADAM_B1 ** ADAM_STEP)
    c2 = 1.0 / (1.0 - ADAM_B2 ** ADAM_STEP)

    def body(*refs):
        w_ref, m_ref, v_ref, g_ref, d_ref, nm_ref, nv_ref = refs[nl:]
        layer = pl.program_id(0)
        g = None
        for l, s_ref in enumerate(refs[:nl]):
            gl = s_ref[0].astype(F32)
            for s in range(1, ns):
                gl = gl + s_ref[s].astype(F32)
            g = gl if g is None else jnp.where(layer == l, gl, g)
        nm = ADAM_B1 * m_ref[0] + (1.0 - ADAM_B1) * g
        nv = ADAM_B2 * v_ref[0] + (1.0 - ADAM_B2) * (g * g)
        g_ref[0] = g
        nm_ref[0] = nm
        nv_ref[0] = nv
        d_ref[0] = -ADAM_LR * ((nm * c1) / (jnp.sqrt(nv * c2) + ADAM_EPS) + ADAM_WD * w_ref[0])

    blk = pl.BlockSpec((1, tr, C), lambda l, i: (l, i, 0))
    sspec = pl.BlockSpec((ns, tr, C), lambda l, i: (0, i, 0))
    return pl.pallas_call(
        body, name=name, grid=(nl, R // tr),
        in_specs=[sspec] * nl + [blk, blk, blk], out_specs=[blk] * 4,
        out_shape=[jax.ShapeDtypeStruct((nl, R, C), F32)] * 4,
        compiler_params=_params(("parallel", "parallel"), VMEM_LIMIT),
    )(*slots, w, m, v)


ARG_WEIGHTS = ("c_ctx", "mod_w", "mod_b", "pre_norm", "post_norm", "w_in", "mla_q_norm", "mla_w_uq", "mla_kv_norm", "mla_w_ukv", "pool_w",
               "pool_scale", "gla_af_w2", "gla_af_b", "gla_ab_w2", "gla_ab_b", "gla_norm", "w_branch_mla", "w_branch_pool", "w_branch_gla", "w_out")
SHARDED = ("mod_w", "w_in", "mla_w_uq", "mla_w_ukv", "gla_af_w2", "gla_ab_w2", "w_branch_mla", "w_branch_pool", "w_branch_gla", "w_out")
ROW_SHARDED = ("w_out",)
REPLICATED = tuple(n for n in ARG_WEIGHTS if n not in SHARDED)
PACK_ROWS = 512


def _pack(parts, dtype):
    flat = jnp.concatenate([p.astype(dtype).reshape(-1) for p in parts])
    n = flat.shape[0]
    total = -(-n // (PACK_ROWS * LANES)) * (PACK_ROWS * LANES)
    return jnp.pad(flat, (0, total - n)).reshape(total // LANES, LANES)


def _unpack(buf, shapes):
    flat = buf.reshape(-1)
    out, off = [], 0
    for shp in shapes:
        n = math.prod(shp)
        out.append(flat[off:off + n].reshape(shp))
        off += n
    return out


def _gathered_to_full(g, name):
    _, r, cs = g.shape
    if name in ROW_SHARDED:
        return g.reshape(N_DEV * r, cs)
    return g.transpose(1, 0, 2).reshape(r, N_DEV * cs)


def _full_to_slots(w, name):
    if name in ROW_SHARDED:
        return w.reshape(N_DEV, w.shape[0] // N_DEV, w.shape[1])
    return w.reshape(w.shape[0], N_DEV, w.shape[1] // N_DEV).transpose(1, 0, 2)


def kernel(x, c, ctx, c_ctx, mod_w, mod_b, pre_norm, post_norm, w_in, mla_q_norm, mla_w_uq, mla_kv_norm, mla_w_ukv, pool_w, pool_scale, gla_af_w2, gla_af_b, gla_ab_w2, gla_ab_b, gla_norm, w_branch_mla, w_branch_pool, w_branch_gla, w_out, loss_target, m_c_ctx, m_mod_w, m_mod_b, m_pre_norm, m_post_norm, m_w_in, m_mla_q_norm, m_mla_w_uq, m_mla_kv_norm, m_mla_w_ukv, m_pool_w, m_pool_scale, m_gla_af_w2, m_gla_af_b, m_gla_ab_w2, m_gla_ab_b, m_gla_norm, m_w_branch_mla, m_w_branch_pool, m_w_branch_gla, m_w_out, v_c_ctx, v_mod_w, v_mod_b, v_pre_norm, v_post_norm, v_w_in, v_mla_q_norm, v_mla_w_uq, v_mla_kv_norm, v_mla_w_ukv, v_pool_w, v_pool_scale, v_gla_af_w2, v_gla_af_b, v_gla_ab_w2, v_gla_ab_b, v_gla_norm, v_w_branch_mla, v_w_branch_pool, v_w_branch_gla, v_w_out):
    local = dict(locals())
    wts = {n: local[n] for n in ARG_WEIGHTS}
    mom1 = {n: local["m_" + n] for n in ARG_WEIGHTS}
    mom2 = {n: local["v_" + n] for n in ARG_WEIGHTS}
    shard_shapes = [wts[n].shape for n in SHARDED]
    rep_shapes = [wts[n].shape for n in REPLICATED]
    kinds = ("grad", "delta", "new_m", "new_v")

    depth = w_in.shape[0]

    def shards(l):
        return [wts[n][l].astype(BF16) for n in SHARDED]

    first = gather_blocks(shards(0), "gather_weights_l0")

    def layer_full(l, carried):
        return {n: _gathered_to_full(gw, n) for n, gw in zip(SHARDED, first if l == 0 else carried)}

    def host_fwd(l):
        return _Hosted(_Gather, shards(l + 1)) if l + 1 < depth else None

    exchanged = GRADS_EARLY + GRADS_LATE

    def slots(g, names):
        return [_full_to_slots(g[n], n).astype(BF16) for n in names]

    def host_bwd(l, g_above):
        return _Hosted(_Scatter, slots(g_above, exchanged))

    def host_own(l):
        return (lambda names, g: _Hosted(_Scatter, slots(g, names))) if l == 0 else None

    small = {n: wts[n] for n in REPLICATED}
    loss, grad_x, grads, g_c_ctx, arrived, (a8, dz8s) = local_step(x, c, ctx, c_ctx, small, loss_target, depth, layer_full, host_fwd, host_bwd, host_own)
    arrived = [a if isinstance(a, dict) else dict(zip(exchanged, a)) for a in arrived]

    g = {n: (g_c_ctx if n == "c_ctx" else jnp.stack([grads[l][n] for l in range(depth)])) for n in REPLICATED}
    gathered, a_all, dz_all = gather_blocks([_pack([g[n] for n in REPLICATED], BF16), a8, jnp.concatenate(dz8s, axis=0)], "gather_small_grads")
    outs = reduce_adamw([gathered], _pack([wts[n] for n in REPLICATED], F32)[None], _pack([mom1[n] for n in REPLICATED], F32)[None],
                        _pack([mom2[n] for n in REPLICATED], F32)[None], "adamw_replicated")
    me = 4 * lax.axis_index("x") + 2 * lax.axis_index("y") + lax.axis_index("c")
    ncol = mod_w.shape[2]
    dz_cols = lax.dynamic_slice_in_dim(dz_all.reshape(N_DEV, depth, 8, 3 * D_MODEL), me * ncol, ncol, axis=3)
    g_mod_w = mod_dw_columns(a_all.reshape(N_DEV * 8, D_MODEL), dz_cols.transpose(1, 0, 2, 3).reshape(depth, N_DEV * 8, ncol), "mod_dw")

    res = {kind: {} for kind in kinds}
    for n in SHARDED:
        parts = [g_mod_w[l:l + 1] if n == "mod_w" else arrived[l][n] for l in range(depth)]
        for kind, o in zip(kinds, reduce_adamw(parts, wts[n], mom1[n], mom2[n], "adamw_" + n)):
            res[kind][n] = o
    for kind, o in zip(("grad", "delta", "new_m", "new_v"), outs):
        res[kind].update(zip(REPLICATED, _unpack(o, rep_shapes)))

    loss = lax.psum(loss[0, 0], ("x", "y", "c"))
    return (loss, grad_x, *[res[kind][n] for kind in ("grad", "delta", "new_m", "new_v") for n in ARG_WEIGHTS])
```

```python
import functools
import math

import jax
import jax.numpy as jnp
import numpy as np
from jax import lax
from jax.experimental import pallas as pl
from jax.experimental.pallas import tpu as pltpu

F32 = jnp.float32
BF16 = jnp.bfloat16

D_MODEL = 1024
NORM_EPS = 1e-6
GRID_W = 64
MLA_HEADS, MLA_Q_RANK, MLA_KV_RANK, MLA_NOPE, MLA_ROPE, MLA_V = 8, 256, 128, 64, 32, 64
MLA_WIDTH = MLA_HEADS * MLA_V
ROPE_BASE = 10000.0
ATT_SCALE = (MLA_NOPE + MLA_ROPE) ** -0.5
POOL_WINDOWS = (2, 4, 8, 16)
POOL_WIDTH, POOL_GROUP = 512, 128
GLA_HEADS, GLA_DK, GLA_DV = 4, 64, 128
GLA_KW, GLA_WIDTH = GLA_HEADS * GLA_DK, GLA_HEADS * GLA_DV
GLA_GATE_RANK, GLA_TAU, GLA_CHUNK = 16, 16.0, 64
IN_SIZES = (256, 128, 32, 512, 512, 512, 256, 256, 512, 16, 16, 512, 3 * D_MODEL)
ADAM_LR, ADAM_B1, ADAM_B2, ADAM_EPS, ADAM_WD, ADAM_STEP = 0.001, 0.9, 0.999, 1e-08, 0.01, 10
N_DEV = 8

LANES = 128
TOKEN_BLOCK = 256
WIDE_BLOCKS = (1152, 768)
VMEM_LIMIT = 48 * 1024 * 1024
NEG_BIG = -1e30

_NT = (((1,), (1,)), ((), ()))
_TN = (((0,), (0,)), ((), ()))


def _dot(a, b):
    return jnp.dot(a, b, preferred_element_type=F32)


def _dot_nt(a, b):
    return lax.dot_general(a, b, _NT, preferred_element_type=F32)


def _dot_tn(a, b):
    return lax.dot_general(a, b, _TN, preferred_element_type=F32)


def _params(sem=None, vmem=None):
    kw = {}
    if sem is not None:
        kw["dimension_semantics"] = sem
    if vmem is not None:
        kw["vmem_limit_bytes"] = vmem
    return pltpu.CompilerParams(**kw)


def _wide_block(rows):
    return next(t for t in WIDE_BLOCKS + (TOKEN_BLOCK,) if rows % t == 0)


def _full(shape):
    n = len(shape)
    return pl.BlockSpec(shape, lambda *_: (0,) * n)


def _sig(x):
    return 1.0 / (1.0 + jnp.exp(-x))


def _silu_and_grad(x):
    s = _sig(x)
    return x * s, s * (1.0 + x * (1.0 - s))


def _acc(ref, val, first):
    @pl.when(first)
    def _():
        ref[...] = val

    @pl.when(jnp.logical_not(first))
    def _():
        ref[...] += val


def mm_multi(a, ws, dtypes, name, tm=TOKEN_BLOCK):
    M, K = a.shape
    nw = len(ws)

    def body(a_ref, *refs):
        av = a_ref[...]
        for w_ref, o_ref in zip(refs[:nw], refs[nw:]):
            o_ref[...] = _dot(av, w_ref[...]).astype(o_ref.dtype)

    return pl.pallas_call(
        body, name=name, grid=(M // tm,),
        in_specs=[pl.BlockSpec((tm, K), lambda i: (i, 0))] + [_full(w.shape) for w in ws],
        out_specs=[pl.BlockSpec((tm, w.shape[1]), lambda i: (i, 0)) for w in ws],
        out_shape=[jax.ShapeDtypeStruct((M, w.shape[1]), dt) for w, dt in zip(ws, dtypes)],
        compiler_params=_params(("parallel",), VMEM_LIMIT),
    )(a, *ws)


def mm_dw(a, dz, name, tn=1024):
    M, K = a.shape
    n = dz.shape[1]
    tn = min(tn, n)
    tk = next(t for t in (3072, 1536, 1024, 512, TOKEN_BLOCK) if M % t == 0)

    def body(a_ref, dz_ref, o_ref):
        _acc(o_ref, _dot_tn(a_ref[...], dz_ref[...]), pl.program_id(1) == 0)

    return pl.pallas_call(
        body, name=name, grid=(n // tn, M // tk),
        in_specs=[pl.BlockSpec((tk, K), lambda j, k: (k, 0)), pl.BlockSpec((tk, tn), lambda j, k: (k, j))],
        out_specs=pl.BlockSpec((K, tn), lambda j, k: (0, j)),
        out_shape=jax.ShapeDtypeStruct((K, n), F32),
        compiler_params=_params(("parallel", "arbitrary"), VMEM_LIMIT),
    )(a, dz)


def mod_fwd(a8, w, b, name):
    tn = D_MODEL

    def body(a_ref, w_ref, b_ref, o_ref):
        a = a_ref[...]
        o_ref[...] = _dot((a * _sig(a)).astype(BF16), w_ref[...]) + b_ref[...]

    return pl.pallas_call(
        body, name=name, grid=(3,),
        in_specs=[_full(a8.shape), pl.BlockSpec((D_MODEL, tn), lambda j: (0, j)), pl.BlockSpec((1, tn), lambda j: (0, j))],
        out_specs=pl.BlockSpec((8, tn), lambda j: (0, j)),
        out_shape=jax.ShapeDtypeStruct((8, 3 * D_MODEL), F32),
        compiler_params=_params(("parallel",)),
    )(a8, w, b)


def mod_bwd(a8, w, dz8, name):
    tn = D_MODEL

    def body(a_ref, w_ref, dz_ref, db_ref, da_ref):
        a = a_ref[...]
        _, dsa = _silu_and_grad(a)
        dz = dz_ref[...]
        db_ref[...] = jnp.sum(dz, axis=0, keepdims=True)
        _acc(da_ref, _dot_nt(dz.astype(BF16), w_ref[...]) * dsa, pl.program_id(0) == 0)

    return pl.pallas_call(
        body, name=name, grid=(3,),
        in_specs=[_full(a8.shape), pl.BlockSpec((D_MODEL, tn), lambda j: (0, j)), pl.BlockSpec((8, tn), lambda j: (0, j))],
        out_specs=[pl.BlockSpec((1, tn), lambda j: (0, j)), _full((8, D_MODEL))],
        out_shape=[jax.ShapeDtypeStruct((1, 3 * D_MODEL), F32), jax.ShapeDtypeStruct((8, D_MODEL), F32)],
        compiler_params=_params(("arbitrary",)),
    )(a8, w, dz8)


def mod_dw_columns(a_all, dz_cols, name):
    depth, R, n = dz_cols.shape

    def body(a_ref, dz_ref, dw_ref):
        a = a_ref[...]
        dw_ref[0] = _dot_tn((a * _sig(a)).astype(BF16), dz_ref[0].astype(BF16))

    return pl.pallas_call(
        body, name=name, grid=(depth,),
        in_specs=[_full(a_all.shape), pl.BlockSpec((1, R, n), lambda l: (l, 0, 0))],
        out_specs=pl.BlockSpec((1, D_MODEL, n), lambda l: (l, 0, 0)),
        out_shape=jax.ShapeDtypeStruct((depth, D_MODEL, n), F32),
        compiler_params=_params(("parallel",)),
    )(a_all, dz_cols)


def _mod_row(nb):
    return lambda i: 2 * (i // nb) + jnp.minimum(i % nb, 1)


def _mod_spec(nb, part):
    row = _mod_row(nb)
    return pl.BlockSpec((1, 1, D_MODEL), lambda i: (row(i), 0, part))


def norm_in_proj(x2, g, ms, ws, nb, name):
    T = x2.shape[0]
    nw = len(ws)

    def body(x_ref, g_ref, sh_ref, sc_ref, *refs):
        x = x_ref[...]
        r = lax.rsqrt(jnp.mean(x * x, axis=-1, keepdims=True) + NORM_EPS)
        h = ((x * r) * g_ref[...] * (1.0 + sc_ref[0]) + sh_ref[0]).astype(BF16)
        refs[nw][...] = h
        for w_ref, o_ref in zip(refs[:nw], refs[nw + 1:]):
            o_ref[...] = _dot(h, w_ref[...])

    tok = lambda n: pl.BlockSpec((TOKEN_BLOCK, n), lambda i: (i, 0))
    outs = pl.pallas_call(
        body, name=name, grid=(T // TOKEN_BLOCK,),
        in_specs=[tok(D_MODEL), _full((1, D_MODEL)), _mod_spec(nb, 0), _mod_spec(nb, 1)] + [_full(w.shape) for w in ws],
        out_specs=[tok(D_MODEL)] + [tok(w.shape[1]) for w in ws],
        out_shape=[jax.ShapeDtypeStruct((T, D_MODEL), BF16)] + [jax.ShapeDtypeStruct((T, w.shape[1]), F32) for w in ws],
        compiler_params=_params(("parallel",), VMEM_LIMIT),
    )(x2, g, ms, ms, *ws)
    return outs[0], outs[1:]


def in_proj_norm_bwd(dzs, ws, x2, g, ms, dxres, nb, name, hosted=None):
    T = x2.shape[0]
    nw = len(ws)
    nx = hosted.n if hosted else 0
    nrow = ms.shape[0]
    row = _mod_row(nb)
    n_in = 2 * nw + 4

    def body(*refs):
        x_ref, g_ref, sc_ref, dxr_ref = refs[2 * nw:n_in]
        dx_ref, dsh_ref, dsc_ref, dg_ref = refs[n_in + nx:n_in + nx + 4]
        i = pl.program_id(0)
        if hosted:
            hosted.run(refs[n_in:n_in + nx], refs[n_in + nx + 4:n_in + 2 * nx + 4], refs[n_in + 2 * nx + 4:], i, T // TOKEN_BLOCK)
        dh = None
        for dz_ref, w_ref in zip(refs[:nw], refs[nw:2 * nw]):
            t = _dot_nt(dz_ref[...], w_ref[...])
            dh = t if dh is None else dh + t
        x = x_ref[...]
        g = g_ref[...]
        r = lax.rsqrt(jnp.mean(x * x, axis=-1, keepdims=True) + NORM_EPS)
        xn = x * r
        du = dh * (1.0 + sc_ref[0])
        dyg = du * g
        dx_ref[...] = dxr_ref[...] + r * (dyg - xn * jnp.mean(dyg * xn, axis=-1, keepdims=True))
        first = (i % nb) <= 1
        _acc(dsh_ref.at[0], jnp.sum(dh, axis=0, keepdims=True), first)
        _acc(dsc_ref.at[0], jnp.sum(dh * xn * g, axis=0, keepdims=True), first)
        _acc(dg_ref, jnp.sum(du * xn, axis=0, keepdims=True), i == 0)

    tok = lambda n: pl.BlockSpec((TOKEN_BLOCK, n), lambda i: (i, 0))
    acc = pl.BlockSpec((1, 1, D_MODEL), lambda i: (row(i), 0, 0))
    outs = pl.pallas_call(
        body, name=name, grid=(T // TOKEN_BLOCK,),
        in_specs=[tok(dz.shape[1]) for dz in dzs] + [_full(w.shape) for w in ws] + [tok(D_MODEL), _full((1, D_MODEL)), _mod_spec(nb, 1), tok(D_MODEL)]
        + (hosted.in_specs if hosted else []),
        out_specs=[tok(D_MODEL), acc, acc, _full((1, D_MODEL))] + (hosted.out_specs if hosted else []),
        out_shape=[jax.ShapeDtypeStruct((T, D_MODEL), F32), jax.ShapeDtypeStruct((nrow, 1, D_MODEL), F32),
                   jax.ShapeDtypeStruct((nrow, 1, D_MODEL), F32), jax.ShapeDtypeStruct((1, D_MODEL), F32)] + (hosted.out_shape if hosted else []),
        scratch_shapes=hosted.scratch if hosted else [],
        compiler_params=_params(("arbitrary",), VMEM_LIMIT),
    )(*dzs, *ws, x2, g, ms, dxres, *(hosted.xs if hosted else []))
    return outs[:4], list(outs[4:])


def _rot(x):
    lane = lax.broadcasted_iota(jnp.int32, x.shape, 1)
    return jnp.where((lane % 16) < 8, -pltpu.roll(x, LANES - 8, 1), pltpu.roll(x, 8, 1))


def _rope(x, cos, sin):
    return x * cos + _rot(x) * sin


def _rope_t(dy, cos, sin):
    return dy * cos - _rot(dy * sin)


def _rms_rows(x):
    r = lax.rsqrt(jnp.mean(x * x, axis=-1, keepdims=True) + NORM_EPS)
    return x * r, r


def _rms_rows_bwd(dyg, xn, r):
    return r * (dyg - xn * jnp.mean(dyg * xn, axis=-1, keepdims=True))


def mla_prep_fwd(za, qg, kvg, wqn, wqr, wkv, cos, sin, nb, name):
    T = za.shape[0]
    W = MLA_HEADS * LANES

    def body(z_ref, qg_ref, kvg_ref, wqn_ref, wqr_ref, wkv_ref, cos_ref, sin_ref, qn_ref, qr_ref, kv_ref, kr_ref):
        z = z_ref[...]
        cos = cos_ref[...]
        sin = sin_ref[...]
        xq, _ = _rms_rows(z[:, 0:256])
        qn = (xq * qg_ref[...]).astype(BF16)
        qn_ref[...] = (_dot(qn, wqn_ref[...]) * ATT_SCALE).astype(BF16)
        qr = _dot(qn, wqr_ref[...])
        for h in range(MLA_HEADS):
            sl = slice(LANES * h, LANES * (h + 1))
            qr_ref[:, sl] = (_rope(qr[:, sl], cos, sin) * ATT_SCALE).astype(BF16)
        xkv, _ = _rms_rows(z[:, 256:384])
        kv_ref[...] = _dot((xkv * kvg_ref[...]).astype(BF16), wkv_ref[...]).astype(BF16)
        kr_ref[...] = _rope(z[:, 384:512], cos, sin).astype(BF16)

    tb = _wide_block(nb * TOKEN_BLOCK)
    npos = nb * TOKEN_BLOCK // tb
    tok = lambda n: pl.BlockSpec((tb, n), lambda i: (i, 0))
    pos = pl.BlockSpec((tb, LANES), lambda i: (i % npos, 0))
    return pl.pallas_call(
        body, name=name, grid=(T // tb,),
        in_specs=[tok(512), _full(qg.shape), _full(kvg.shape), _full(wqn.shape), _full(wqr.shape), _full(wkv.shape), pos, pos],
        out_specs=[tok(W), tok(W), tok(W), tok(LANES)],
        out_shape=[jax.ShapeDtypeStruct((T, W), BF16)] * 3 + [jax.ShapeDtypeStruct((T, LANES), BF16)],
        compiler_params=_params(("parallel",)),
    )(za, qg, kvg, wqn, wqr, wkv, cos, sin)


def mla_prep_bwd(dqn, dqr, dkv, dkr, za, qg, kvg, wqn, wqr, wkv, cos, sin, nb, name):
    T = za.shape[0]
    W = MLA_HEADS * LANES

    def body(dqn_ref, dqr_ref, dkv_ref, dkr_ref, z_ref, qg_ref, kvg_ref, wqn_ref, wqr_ref, wkv_ref, cos_ref, sin_ref,
             dz_ref, dwqn_ref, dwqr_ref, dwkv_ref, dqg_ref, dkvg_ref):
        first = pl.program_id(0) == 0
        z = z_ref[...]
        cos = cos_ref[...]
        sin = sin_ref[...]
        qg = qg_ref[...]
        kvg = kvg_ref[...]
        xq, rq = _rms_rows(z[:, 0:256])
        qn = (xq * qg).astype(BF16)
        a1 = (dqn_ref[...].astype(F32) * ATT_SCALE).astype(BF16)
        parts = []
        for h in range(MLA_HEADS):
            sl = slice(LANES * h, LANES * (h + 1))
            parts.append(_rope_t(dqr_ref[:, sl].astype(F32) * ATT_SCALE, cos, sin).astype(BF16))
        a2 = jnp.concatenate(parts, axis=1)
        dq = _dot_nt(a1, wqn_ref[...]) + _dot_nt(a2, wqr_ref[...])
        _acc(dwqn_ref, _dot_tn(qn, a1), first)
        _acc(dwqr_ref, _dot_tn(qn, a2), first)
        _acc(dqg_ref, jnp.sum(dq * xq, axis=0, keepdims=True), first)
        dz_ref[:, 0:256] = _rms_rows_bwd(dq * qg, xq, rq).astype(BF16)
        xkv, rkv = _rms_rows(z[:, 256:384])
        kvn = (xkv * kvg).astype(BF16)
        dkvb = dkv_ref[...].astype(BF16)
        dk = _dot_nt(dkvb, wkv_ref[...])
        _acc(dwkv_ref, _dot_tn(kvn, dkvb), first)
        _acc(dkvg_ref, jnp.sum(dk * xkv, axis=0, keepdims=True), first)
        dz_ref[:, 256:384] = _rms_rows_bwd(dk * kvg, xkv, rkv).astype(BF16)
        dz_ref[:, 384:512] = _rope_t(dkr_ref[...], cos, sin).astype(BF16)

    tb = _wide_block(nb * TOKEN_BLOCK)
    npos = nb * TOKEN_BLOCK // tb
    tok = lambda n: pl.BlockSpec((tb, n), lambda i: (i, 0))
    pos = pl.BlockSpec((tb, LANES), lambda i: (i % npos, 0))
    return pl.pallas_call(
        body, name=name, grid=(T // tb,),
        in_specs=[tok(W), tok(W), tok(W), tok(LANES), tok(512), _full(qg.shape), _full(kvg.shape), _full(wqn.shape),
                  _full(wqr.shape), _full(wkv.shape), pos, pos],
        out_specs=[tok(512), _full(wqn.shape), _full(wqr.shape), _full(wkv.shape), _full(qg.shape), _full(kvg.shape)],
        out_shape=[jax.ShapeDtypeStruct((T, 512), BF16), jax.ShapeDtypeStruct(wqn.shape, F32), jax.ShapeDtypeStruct(wqr.shape, F32),
                   jax.ShapeDtypeStruct(wkv.shape, F32), jax.ShapeDtypeStruct(qg.shape, F32), jax.ShapeDtypeStruct(kvg.shape, F32)],
        compiler_params=_params(("arbitrary",)),
    )(dqn, dqr, dkv, dkr, za, qg, kvg, wqn, wqr, wkv, cos, sin)


def _att_qk(qn_ref, qr_ref, kv_ref, kr, j):
    sl = slice(LANES * j, LANES * (j + 1))
    q = jnp.concatenate([qn_ref[0, :, sl], qr_ref[0, :, sl]], axis=1)
    kvj = kv_ref[0, :, sl]
    k = jnp.concatenate([kvj, kr], axis=1)
    return q, k, kvj, _dot_nt(q, k)


def _att_specs(L, lk, q0, pairs=1):
    TQ, W2 = TOKEN_BLOCK, 2 * LANES * pairs
    qspec = pl.BlockSpec((1, TQ, W2), lambda b, h, i: (b, i + q0, h))
    kvspec = pl.BlockSpec((1, lk, W2), lambda b, h, i: (b, 0, h))
    krspec = pl.BlockSpec((1, lk, LANES), lambda b, h, i: (b, 0, 0))
    gspec = pl.BlockSpec((1, TQ, LANES * pairs), lambda b, h, i: (b, i + q0, h))
    lspec = pl.BlockSpec((1, pairs, TQ, LANES), lambda b, h, i: (b, h, i + q0, 0))
    return qspec, kvspec, krspec, gspec, lspec


_ANY = pl.BlockSpec(memory_space=pl.ANY)


def attention_fwd(qn, qr, kv, kr, zg, n_ctx, name, hosted=None):
    B, L, _ = qn.shape
    TQ = TOKEN_BLOCK
    PAIRS = 2
    HP = MLA_HEADS // 2 // PAIRS
    shapes = [jax.ShapeDtypeStruct((B, L, MLA_WIDTH), F32), jax.ShapeDtypeStruct((B, L, MLA_WIDTH), BF16),
              jax.ShapeDtypeStruct((B, MLA_HEADS // 2, L, LANES), F32)]

    nx = hosted.n if hosted else 0
    NQ = L // TQ - 1

    def body(*refs):
        if hosted:
            step = (pl.program_id(0) * HP + pl.program_id(1)) * NQ + pl.program_id(2)
            hosted.run(refs[5:5 + nx], refs[8 + nx:8 + 2 * nx], refs[8 + 2 * nx:], step, B * HP * NQ)
        _fwd_step(*refs[:5], *refs[5 + nx:8 + nx])

    def body_ctx(qn_ref, qr_ref, kv_ref, kr_ref, g_ref, *rest):
        _fwd_step(qn_ref, qr_ref, kv_ref, kr_ref, g_ref, *rest[-3:])

    def _fwd_step(qn_ref, qr_ref, kv_ref, kr_ref, g_ref, ya_ref, ym_ref, lse_ref):
        kr_v = kr_ref[0]
        for pr in range(PAIRS):
            outs, lses = [], []
            for j in (2 * pr, 2 * pr + 1):
                _, _, kvj, s = _att_qk(qn_ref, qr_ref, kv_ref, kr_v, j)
                m = jnp.max(s, axis=-1, keepdims=True)
                p = jnp.exp(s - m).astype(BF16)
                lane_k = lax.broadcasted_iota(jnp.int32, kvj.shape, 1)
                o = _dot(p, jnp.where(lane_k < MLA_V, jnp.ones_like(kvj), kvj))
                l = o[:, 0:1]
                outs.append(o / l)
                lses.append(m + jnp.log(l))
            lane = lax.broadcasted_iota(jnp.int32, outs[0].shape, 1)
            y = jnp.where(lane < MLA_V, pltpu.roll(outs[0], MLA_V, 1), outs[1])
            sl = slice(LANES * pr, LANES * (pr + 1))
            ya_ref[0, :, sl] = y
            g = g_ref[0, :, sl]
            ym_ref[0, :, sl] = (y * g * _sig(g)).astype(BF16)
            lse_ref[0, pr] = jnp.where(lane < MLA_V, lses[0], lses[1])

    qspec, kvspec, krspec, gspec, lspec = _att_specs(L, L, 1, PAIRS)
    main = pl.pallas_call(
        body, name=name, grid=(B, HP, NQ),
        in_specs=[qspec, qspec, kvspec, krspec, gspec] + (hosted.in_specs if hosted else []),
        out_specs=[gspec, gspec, lspec] + (hosted.out_specs if hosted else []),
        out_shape=shapes + (hosted.out_shape if hosted else []), scratch_shapes=hosted.scratch if hosted else [],
        compiler_params=_params(("arbitrary",) * 3 if hosted else ("parallel",) * 3, VMEM_LIMIT),
    )(qn, qr, kv, kr, zg, *(hosted.xs if hosted else []))
    qspec, kvspec, krspec, gspec, lspec = _att_specs(L, n_ctx, 0, PAIRS)
    outs = pl.pallas_call(
        body_ctx, name=name + "_ctx", grid=(B, HP, 1),
        in_specs=[qspec, qspec, kvspec, krspec, gspec, _ANY, _ANY, _ANY], out_specs=[gspec, gspec, lspec], out_shape=shapes,
        input_output_aliases={5: 0, 6: 1, 7: 2},
        compiler_params=_params(("parallel", "parallel", "parallel"), VMEM_LIMIT),
    )(qn, qr, kv, kr, zg, *main[:3])
    return (*outs, list(main[3:]))


def attention_bwd(qn, qr, kv, kr, zg, ya, lse, dym, n_ctx, name, hosted=None):
    B, L, _ = qn.shape
    TQ = TOKEN_BLOCK
    PAIRS = 2
    HP = MLA_HEADS // 2 // PAIRS
    W = MLA_HEADS * LANES
    shapes = [jax.ShapeDtypeStruct((B, L, W), BF16), jax.ShapeDtypeStruct((B, L, W), BF16), jax.ShapeDtypeStruct((B, L, W), F32),
              jax.ShapeDtypeStruct((B, L, LANES), F32), jax.ShapeDtypeStruct((B, L, MLA_WIDTH), BF16)]

    nx = hosted.n if hosted else 0
    NQ = L // TQ - 1

    def body(*refs):
        if hosted:
            step = (pl.program_id(0) * HP + pl.program_id(1)) * NQ + pl.program_id(2)
            hosted.run(refs[8:8 + nx], refs[13 + nx:13 + 2 * nx], refs[13 + 2 * nx:], step, B * HP * NQ)
        dkv_ref, dkr_ref = refs[10 + nx], refs[11 + nx]

        @pl.when(pl.program_id(2) == 0)
        def _():
            dkv_ref[...] = jnp.zeros_like(dkv_ref)

        @pl.when(jnp.logical_and(pl.program_id(2) == 0, pl.program_id(1) == 0))
        def _():
            dkr_ref[...] = jnp.zeros_like(dkr_ref)

        _bwd_step(*refs[:8], *refs[8 + nx:13 + nx])

    def body_ctx(qn_ref, qr_ref, kv_ref, kr_ref, g_ref, ya_ref, lse_ref, dy_ref, dkv_in, dkr_in, a0, a1, a2,
                 dqn_ref, dqr_ref, dkv_ref, dkr_ref, dzg_ref):
        dkv_ref[...] = dkv_in[...]

        @pl.when(pl.program_id(1) == 0)
        def _():
            dkr_ref[...] = dkr_in[...]

        _bwd_step(qn_ref, qr_ref, kv_ref, kr_ref, g_ref, ya_ref, lse_ref, dy_ref, dqn_ref, dqr_ref, dkv_ref, dkr_ref, dzg_ref)

    def _bwd_step(qn_ref, qr_ref, kv_ref, kr_ref, g_ref, ya_ref, lse_ref, dy_ref, dqn_ref, dqr_ref, dkv_ref, dkr_ref, dzg_ref):
        kr_v = kr_ref[0]
        for pr in range(PAIRS):
            psl = slice(LANES * pr, LANES * (pr + 1))
            silu, dsilu = _silu_and_grad(g_ref[0, :, psl])
            dy = dy_ref[0, :, psl]
            ya_v = ya_ref[0, :, psl]
            dya = dy * silu
            dzg_ref[0, :, psl] = (dy * ya_v * dsilu).astype(BF16)
            lane = lax.broadcasted_iota(jnp.int32, dya.shape, 1)
            hi = lane >= MLA_V
            d_out = [jnp.where(hi, pltpu.roll(dya, MLA_V, 1), 0.0), jnp.where(hi, dya, 0.0)]
            prod = dya * ya_v
            drow = [jnp.sum(jnp.where(hi, 0.0, prod), axis=-1, keepdims=True), jnp.sum(jnp.where(hi, prod, 0.0), axis=-1, keepdims=True)]
            lse_v = lse_ref[0, pr]
            for jj in range(2):
                j = 2 * pr + jj
                sl = slice(LANES * j, LANES * (j + 1))
                q, k, kvj, s = _att_qk(qn_ref, qr_ref, kv_ref, kr_v, j)
                pn = jnp.exp(s - lse_v[:, MLA_V * jj:MLA_V * jj + 1])
                dob = d_out[jj].astype(BF16)
                ds = (pn * (_dot_nt(dob, kvj) - drow[jj])).astype(BF16)
                dq = _dot(ds, k)
                dqn_ref[0, :, sl] = jnp.where(hi, 0.0, dq[:, :LANES]).astype(BF16)
                dqr_ref[0, :, sl] = dq[:, LANES:].astype(BF16)
                dk = _dot_tn(ds, q)
                dkv_ref[0, :, sl] += dk[:, :LANES] + _dot_tn(pn.astype(BF16), dob)
                dkr_ref[0] += dk[:, LANES:]

    sem = _params(("parallel", "arbitrary", "arbitrary"), VMEM_LIMIT)
    qspec, kvspec, krspec, gspec, lspec = _att_specs(L, L, 1, PAIRS)
    main = pl.pallas_call(
        body, name=name, grid=(B, HP, NQ),
        in_specs=[qspec, qspec, kvspec, krspec, gspec, gspec, lspec, gspec] + (hosted.in_specs if hosted else []),
        out_specs=[qspec, qspec, kvspec, krspec, gspec] + (hosted.out_specs if hosted else []),
        out_shape=shapes + (hosted.out_shape if hosted else []), scratch_shapes=hosted.scratch if hosted else [],
        compiler_params=_params(("arbitrary",) * 3, VMEM_LIMIT) if hosted else sem,
    )(qn, qr, kv, kr, zg, ya, lse, dym, *(hosted.xs if hosted else []))
    qspec, kvspec, krspec, gspec, lspec = _att_specs(L, n_ctx, 0, PAIRS)
    outs = pl.pallas_call(
        body_ctx, name=name + "_ctx", grid=(B, HP, 1),
        in_specs=[qspec, qspec, kvspec, krspec, gspec, gspec, lspec, gspec, kvspec, krspec, _ANY, _ANY, _ANY],
        out_specs=[qspec, qspec, kvspec, krspec, gspec], out_shape=shapes,
        input_output_aliases={8: 2, 9: 3, 10: 0, 11: 1, 12: 4}, compiler_params=sem,
    )(qn, qr, kv, kr, zg, ya, lse, dym, main[2], main[3], main[0], main[1], main[4])
    return (*outs, list(main[5:]))


def _seg_bounds(rows, n_ctx, L):
    in_ctx = rows < n_ctx
    return jnp.where(in_ctx, 0, n_ctx), jnp.where(in_ctx, n_ctx, L)


def _seg_shift(x, k, rows, lo, hi):
    src = rows + k
    return jnp.where(jnp.logical_and(src >= lo, src < hi), pltpu.roll(x, (-k) % x.shape[0], 0), 0.0)


def _run_sum(u, m, rows, lo, hi, step):
    acc, k = u, 1
    while k < m:
        acc = acc + _seg_shift(acc, step * k, rows, lo, hi)
        k *= 2
    return acc


def _window_sum(u, w, rows, lo, hi, mirror):
    right = _run_sum(u, w // 2, rows, lo, hi, 1)
    left = _run_sum(u, w // 2, rows, lo, hi, -1)
    if mirror:
        return left + _seg_shift(right, 1, rows, lo, hi)
    return right + _seg_shift(left, -1, rows, lo, hi)


def _window_count(w, rows, lo, hi):
    pos = rows - lo
    return (jnp.minimum(pos + w // 2, hi - lo) - jnp.maximum(pos - w // 2, 0)).astype(F32)


def pool_fwd(px, pg, pw, ps, n_ctx, name):
    B, L, _ = px.shape

    def body(px_ref, pg_ref, pw_ref, ps_ref, y_ref):
        rows = lax.broadcasted_iota(jnp.int32, (L, POOL_GROUP), 0)
        lo, hi = _seg_bounds(rows, n_ctx, L)
        for gi, w in enumerate(POOL_WINDOWS):
            sl = slice(POOL_GROUP * gi, POOL_GROUP * (gi + 1))
            u = px_ref[0, :, sl]
            pooled = _window_sum(u, w, rows, lo, hi, False) / _window_count(w, rows, lo, hi) - u
            mixed = _dot(pooled.astype(BF16), pw_ref[gi])
            g = pg_ref[0, :, sl]
            y_ref[0, :, sl] = (mixed * ps_ref[:, sl] * (g * _sig(g))).astype(BF16)

    tok = pl.BlockSpec((1, L, POOL_WIDTH), lambda b: (b, 0, 0))
    return pl.pallas_call(
        body, name=name, grid=(B,),
        in_specs=[tok, tok, _full(pw.shape), _full(ps.shape)],
        out_specs=tok, out_shape=jax.ShapeDtypeStruct((B, L, POOL_WIDTH), BF16),
        compiler_params=_params(("parallel",), VMEM_LIMIT),
    )(px, pg, pw, ps)


def pool_bwd(px, pg, pw, ps, dy, n_ctx, name):
    B, L, _ = px.shape

    def body(px_ref, pg_ref, pw_ref, ps_ref, dy_ref, dpx_ref, dpg_ref, dpw_ref, dps_ref):
        first = pl.program_id(0) == 0
        rows = lax.broadcasted_iota(jnp.int32, (L, POOL_GROUP), 0)
        lo, hi = _seg_bounds(rows, n_ctx, L)
        for gi, w in enumerate(POOL_WINDOWS):
            sl = slice(POOL_GROUP * gi, POOL_GROUP * (gi + 1))
            u = px_ref[0, :, sl]
            cnt = _window_count(w, rows, lo, hi)
            pooled = (_window_sum(u, w, rows, lo, hi, False) / cnt - u).astype(BF16)
            mixed = _dot(pooled, pw_ref[gi])
            silu, dsilu = _silu_and_grad(pg_ref[0, :, sl])
            sc = ps_ref[:, sl]
            dyv = dy_ref[0, :, sl]
            _acc(dps_ref.at[:, sl], jnp.sum(dyv * mixed * silu, axis=0, keepdims=True), first)
            dpg_ref[0, :, sl] = (dyv * mixed * sc * dsilu).astype(BF16)
            dmixed = (dyv * sc * silu).astype(BF16)
            _acc(dpw_ref.at[gi], _dot_tn(pooled, dmixed), first)
            dpooled = _dot_nt(dmixed, pw_ref[gi])
            dpx_ref[0, :, sl] = (_window_sum(dpooled / cnt, w, rows, lo, hi, True) - dpooled).astype(BF16)

    tok = pl.BlockSpec((1, L, POOL_WIDTH), lambda b: (b, 0, 0))
    return pl.pallas_call(
        body, name=name, grid=(B,),
        in_specs=[tok, tok, _full(pw.shape), _full(ps.shape), tok],
        out_specs=[tok, tok, _full(pw.shape), _full(ps.shape)],
        out_shape=[jax.ShapeDtypeStruct((B, L, POOL_WIDTH), BF16)] * 2 + [jax.ShapeDtypeStruct(pw.shape, F32), jax.ShapeDtypeStruct(ps.shape, F32)],
        compiler_params=_params(("arbitrary",), VMEM_LIMIT),
    )(px, pg, pw, ps, dy)


_SCAN_STEPS = (1, 2, 4, 8, 16, 32)
SCAN_CHUNKS = 4


def _cum_fwd(x, r):
    for s in _SCAN_STEPS:
        x = x + jnp.where(r >= s, pltpu.roll(x, s, 0), 0.0)
    return x


def _cum_bwd(x, r):
    n = x.shape[0]
    for s in _SCAN_STEPS:
        x = x + jnp.where(r + s < GLA_CHUNK, pltpu.roll(x, n - s, 0), 0.0)
    return x


def _log_sigmoid(x):
    return jnp.minimum(x, 0.0) - jnp.log(1.0 + jnp.exp(-jnp.abs(x)))


def _gla_decays(lr, w_ref, b_ref, r, reverse):
    pre = _dot(lr, w_ref[...]) + b_ref[...]
    a = _log_sigmoid(pre) / GLA_TAU
    return pre, a, (_cum_bwd(a, r) if reverse else _cum_fwd(a, r)), _chunk_total(a)


def _chunk_total(x):
    x3 = x.reshape(x.shape[0] // GLA_CHUNK, GLA_CHUNK, x.shape[1])
    return jnp.broadcast_to(jnp.sum(x3, axis=1, keepdims=True), x3.shape).reshape(x.shape)


def gla_prep_fwd(zlr, zq, zk, waf, wab, baf, bab, name):
    T = zlr.shape[0]
    tb = _wide_block(T)

    def body(lr_ref, q_ref, k_ref, waf_ref, wab_ref, baf_ref, bab_ref, qf_ref, kf_ref, ksf_ref, tf_ref, qb_ref, kb_ref, ksb_ref, tb_ref):
        r = lax.broadcasted_iota(jnp.int32, (tb, GLA_KW), 0) % GLA_CHUNK
        lr = lr_ref[...].astype(BF16)
        q = q_ref[...] * GLA_DK ** -0.5
        k = k_ref[...]
        for rev, w_ref, b_ref, qo, ko, kso, to in ((False, waf_ref, baf_ref, qf_ref, kf_ref, ksf_ref, tf_ref),
                                                   (True, wab_ref, bab_ref, qb_ref, kb_ref, ksb_ref, tb_ref)):
            _, _, b, tot = _gla_decays(lr, w_ref, b_ref, r, rev)
            qo[...] = (q * jnp.exp(b)).astype(BF16)
            ko[...] = (k * jnp.exp(-b)).astype(BF16)
            kso[...] = (k * jnp.exp(tot - b)).astype(BF16)
            to[...] = tot

    tok = lambda n: pl.BlockSpec((tb, n), lambda i: (i, 0))
    outs = [jax.ShapeDtypeStruct((T, GLA_KW), BF16)] * 3 + [jax.ShapeDtypeStruct((T, GLA_KW), F32)]
    return pl.pallas_call(
        body, name=name, grid=(T // tb,),
        in_specs=[tok(LANES), tok(GLA_KW), tok(GLA_KW), _full(waf.shape), _full(wab.shape), _full(baf.shape), _full(bab.shape)],
        out_specs=[tok(GLA_KW)] * 8, out_shape=outs + outs,
        compiler_params=_params(("parallel",)),
    )(zlr, zq, zk, waf, wab, baf, bab)


def gla_prep_bwd(zlr, zq, zk, waf, wab, baf, bab, gf, gb, dvs, name):
    T = zlr.shape[0]
    tb = _wide_block(T)

    def body(lr_ref, q_ref, k_ref, waf_ref, wab_ref, baf_ref, bab_ref, dqf, dkf, dksf, ddf, dqb, dkb, dksb, ddb, dvf, dvb,
             dlr_ref, dq_ref, dk_ref, dwaf_ref, dwab_ref, dbaf_ref, dbab_ref, dv_ref):
        first = pl.program_id(0) == 0
        dv_ref[...] = (dvf[...] + dvb[...]).astype(BF16)
        r = lax.broadcasted_iota(jnp.int32, (tb, GLA_KW), 0) % GLA_CHUNK
        lr = lr_ref[...].astype(BF16)
        q = q_ref[...] * GLA_DK ** -0.5
        k = k_ref[...]
        dq_tot = None
        dk_tot = None
        dlr = None
        for rev, w_ref, b_ref, dqt, dkt, dks, ddec, dw_ref, db_ref in (
                (False, waf_ref, baf_ref, dqf, dkf, dksf, ddf, dwaf_ref, dbaf_ref),
                (True, wab_ref, bab_ref, dqb, dkb, dksb, ddb, dwab_ref, dbab_ref)):
            pre, _, b, tot = _gla_decays(lr, w_ref, b_ref, r, rev)
            e1 = jnp.exp(b)
            e2 = jnp.exp(-b)
            e3 = jnp.exp(tot - b)
            dqt_v = dqt[...]
            dkt_v = dkt[...]
            dks_v = dks[...]
            dq = dqt_v * e1
            dk = dkt_v * e2 + dks_v * e3
            g3 = dks_v * (k * e3)
            d_b = dqt_v * (q * e1) - dkt_v * (k * e2) - g3
            d_tot = _chunk_total(g3) + ddec[...] * jnp.exp(tot)
            da = (_cum_fwd(d_b, r) if rev else _cum_bwd(d_b, r)) + d_tot
            dpre = (da * (_sig(-pre) / GLA_TAU)).astype(BF16)
            t = _dot_nt(dpre, w_ref[...])
            dlr = t if dlr is None else dlr + t
            _acc(dw_ref, _dot_tn(lr, dpre), first)
            _acc(db_ref, jnp.sum(dpre.astype(F32), axis=0, keepdims=True), first)
            dq_tot = dq if dq_tot is None else dq_tot + dq
            dk_tot = dk if dk_tot is None else dk_tot + dk
        dlr_ref[...] = dlr.astype(BF16)
        dq_ref[...] = (dq_tot * GLA_DK ** -0.5).astype(BF16)
        dk_ref[...] = dk_tot.astype(BF16)

    tok = lambda n: pl.BlockSpec((tb, n), lambda i: (i, 0))
    return pl.pallas_call(
        body, name=name, grid=(T // tb,),
        in_specs=[tok(LANES), tok(GLA_KW), tok(GLA_KW), _full(waf.shape), _full(wab.shape), _full(baf.shape), _full(bab.shape)] + [tok(GLA_KW)] * 8
        + [tok(GLA_WIDTH)] * 2,
        out_specs=[tok(LANES), tok(GLA_KW), tok(GLA_KW), _full(waf.shape), _full(wab.shape), _full(baf.shape), _full(bab.shape), tok(GLA_WIDTH)],
        out_shape=[jax.ShapeDtypeStruct((T, LANES), BF16), jax.ShapeDtypeStruct((T, GLA_KW), BF16), jax.ShapeDtypeStruct((T, GLA_KW), BF16),
                   jax.ShapeDtypeStruct(waf.shape, F32), jax.ShapeDtypeStruct(wab.shape, F32), jax.ShapeDtypeStruct(baf.shape, F32),
                   jax.ShapeDtypeStruct(bab.shape, F32), jax.ShapeDtypeStruct((T, GLA_WIDTH), BF16)],
        compiler_params=_params(("arbitrary",)),
    )(zlr, zq, zk, waf, wab, baf, bab, *gf, *gb, *dvs)


def _chunk_order(nc, n_ctx_chunks, reverse):
    if not reverse:
        return lambda c: c
    return lambda c: jnp.where(c < n_ctx_chunks, n_ctx_chunks - 1 - c, nc + n_ctx_chunks - 1 - c)


def _tri_mask4(reverse):
    ri = lax.broadcasted_iota(jnp.int32, (GLA_CHUNK, GLA_HEADS * GLA_CHUNK), 0)
    ci = lax.broadcasted_iota(jnp.int32, (GLA_CHUNK, GLA_HEADS * GLA_CHUNK), 1) % GLA_CHUNK
    return (ri <= ci) if reverse else (ri >= ci)


def _block_diag(x, rb, cb):
    x4 = jnp.concatenate([x] * GLA_HEADS, axis=0)
    r = lax.broadcasted_iota(jnp.int32, x4.shape, 0) // rb
    c = lax.broadcasted_iota(jnp.int32, x4.shape, 1) // cb
    return jnp.where(r == c, x4, jnp.zeros_like(x4))


def _diag_blocks(f, rb, cb):
    c = lax.broadcasted_iota(jnp.int32, (rb, GLA_HEADS * cb), 1) // cb
    out = None
    for h in range(GLA_HEADS):
        t = jnp.where(c == h, f[rb * h:rb * (h + 1)], 0.0)
        out = t if out is None else out + t
    return out


def gla_scan_fwd(dirs, v, n_ctx, name):
    B, L, _ = v.shape
    C, G = GLA_CHUNK, SCAN_CHUNKS
    nc = L // C
    orders = [_chunk_order(nc // G, n_ctx // C // G, rev) for rev in (False, True)]

    def body(qf, kf, ksf, tf, vf, qb, kb, ksb, tb, vb, of, ssf, ob, ssb, stf, stb):
        @pl.when(pl.program_id(1) == 0)
        def _():
            stf[...] = jnp.zeros_like(stf)
            stb[...] = jnp.zeros_like(stb)

        for sub in range(G):
            step(qf, kf, ksf, vf, tf, of, ssf, stf, False, sub)
            step(qb, kb, ksb, vb, tb, ob, ssb, stb, True, G - 1 - sub)

    def step(q_ref, k_ref, ks_ref, v_ref, tot_ref, o_ref, ss_ref, st, reverse, sub):
        rows = slice(C * sub, C * (sub + 1))
        S = st[...]
        ss_ref[0, sub] = S
        q = q_ref[0, rows]
        v = v_ref[0, rows]
        k4 = _block_diag(k_ref[0, rows], GLA_CHUNK, GLA_DK)
        v4 = _block_diag(v.astype(BF16), GLA_CHUNK, GLA_DV)
        s4 = _block_diag(S.astype(BF16), GLA_DV, GLA_DK)
        P = jnp.where(_tri_mask4(reverse), _dot_nt(q, k4), 0.0)
        o_ref[0, rows] = _dot(P.astype(BF16), v4) + _dot_nt(q, s4)
        st[...] = jnp.exp(tot_ref[0, C * sub:C * sub + 1, :]) * S + _diag_blocks(_dot(v.T.astype(BF16), ks_ref[0, rows]), GLA_DV, GLA_DK)

    in_specs, out_specs, out_shape = [], [], []
    for order in orders:
        tok = lambda n, order=order: pl.BlockSpec((1, G * C, n), lambda b, c: (b, order(c), 0))
        in_specs += [tok(GLA_KW), tok(GLA_KW), tok(GLA_KW), tok(GLA_KW), tok(GLA_WIDTH)]
        out_specs += [tok(GLA_WIDTH), pl.BlockSpec((1, G, GLA_DV, GLA_KW), lambda b, c, order=order: (b, order(c), 0, 0))]
        out_shape += [jax.ShapeDtypeStruct((B, L, GLA_WIDTH), F32), jax.ShapeDtypeStruct((B, nc, GLA_DV, GLA_KW), F32)]
    outs = pl.pallas_call(
        body, name=name, grid=(B, nc // G), in_specs=in_specs, out_specs=out_specs, out_shape=out_shape,
        scratch_shapes=[pltpu.VMEM((GLA_DV, GLA_KW), F32)] * 2,
        compiler_params=_params(("parallel", "arbitrary")),
    )(*dirs[0], v, *dirs[1], v)
    return outs[:2], outs[2:]


def gla_scan_bwd(dirs, v, do, n_ctx, name, hosted=None):
    B, L, _ = v.shape
    C, G = GLA_CHUNK, SCAN_CHUNKS
    nc = L // C
    npair = nc // G
    orders = []
    for rev in (False, True):
        fwd_order = _chunk_order(npair, n_ctx // C // G, rev)
        orders.append(lambda c, fwd_order=fwd_order: fwd_order(npair - 1 - c))

    nx = hosted.n if hosted else 0

    def body(*refs):
        qf, kf, ksf, tf, ssf, vf, dof, qb, kb, ksb, tb, ssb, vb, dob = refs[:14]
        dqf, dkf, dksf, dvf, ddf, dqb, dkb, dksb, dvb, ddb = refs[14 + nx:24 + nx]
        dstf, dstb = refs[24 + 2 * nx:26 + 2 * nx]
        if hosted:
            hosted.run(refs[14:14 + nx], refs[24 + nx:24 + 2 * nx], refs[26 + 2 * nx:], pl.program_id(0) * npair + pl.program_id(1), B * npair)

        @pl.when(pl.program_id(1) == 0)
        def _():
            dstf[...] = jnp.zeros_like(dstf)
            dstb[...] = jnp.zeros_like(dstb)

        for sub in range(G):
            step(qf, kf, ksf, vf, tf, ssf, dof, dqf, dkf, dksf, dvf, ddf, dstf, False, G - 1 - sub)
            step(qb, kb, ksb, vb, tb, ssb, dob, dqb, dkb, dksb, dvb, ddb, dstb, True, sub)

    def step(q_ref, k_ref, ks_ref, v_ref, tot_ref, ss_ref, do_ref, dq_ref, dk_ref, dks_ref, dv_ref, dd_ref, dst, reverse, sub):
        rows = slice(C * sub, C * (sub + 1))
        dSn = dst[...]
        S = ss_ref[0, sub]
        q = q_ref[0, rows]
        vb = v_ref[0, rows].astype(BF16)
        dob = do_ref[0, rows].astype(BF16)
        k4 = _block_diag(k_ref[0, rows], GLA_CHUNK, GLA_DK)
        v4 = _block_diag(vb, GLA_CHUNK, GLA_DV)
        s4 = _block_diag(S.astype(BF16), GLA_DV, GLA_DK)
        ds4 = _block_diag(dSn.astype(BF16), GLA_DV, GLA_DK)
        tri = _tri_mask4(reverse)
        P = jnp.where(tri, _dot_nt(q, k4), 0.0).astype(BF16)
        dP = jnp.where(tri, _dot_nt(dob, v4), 0.0).astype(BF16)
        dq_ref[0, rows] = _dot(dob, s4) + _dot(dP, k4)
        dk_ref[0, rows] = _diag_blocks(_dot_tn(dP, q), GLA_CHUNK, GLA_DK)
        dv_ref[0, rows] = _diag_blocks(_dot_tn(P, dob), GLA_CHUNK, GLA_DV) + _dot_nt(ks_ref[0, rows], ds4)
        dks_ref[0, rows] = _dot(vb, ds4)
        dd_ref[0, rows] = jnp.broadcast_to(jnp.sum(dSn * S, axis=0, keepdims=True), (C, GLA_KW))
        dst[...] = jnp.exp(tot_ref[0, C * sub:C * sub + 1, :]) * dSn + _diag_blocks(_dot_tn(dob, q), GLA_DV, GLA_DK)

    in_specs, out_specs, out_shape = [], [], []
    for order in orders:
        tok = lambda n, order=order: pl.BlockSpec((1, G * C, n), lambda b, c: (b, order(c), 0))
        in_specs += [tok(GLA_KW), tok(GLA_KW), tok(GLA_KW), tok(GLA_KW),
                     pl.BlockSpec((1, G, GLA_DV, GLA_KW), lambda b, c, order=order: (b, order(c), 0, 0)), tok(GLA_WIDTH), tok(GLA_WIDTH)]
        out_specs += [tok(GLA_KW), tok(GLA_KW), tok(GLA_KW), tok(GLA_WIDTH), tok(GLA_KW)]
        out_shape += [jax.ShapeDtypeStruct((B, L, GLA_KW), F32)] * 3 + [jax.ShapeDtypeStruct((B, L, GLA_WIDTH), F32), jax.ShapeDtypeStruct((B, L, GLA_KW), F32)]
    outs = pl.pallas_call(
        body, name=name, grid=(B, npair), in_specs=in_specs + (hosted.in_specs if hosted else []),
        out_specs=out_specs + (hosted.out_specs if hosted else []), out_shape=out_shape + (hosted.out_shape if hosted else []),
        scratch_shapes=[pltpu.VMEM((GLA_DV, GLA_KW), F32)] * 2 + (hosted.scratch if hosted else []),
        compiler_params=_params(("arbitrary", "arbitrary") if hosted else ("parallel", "arbitrary")),
    )(*dirs[0], v, do, *dirs[1], v, do, *(hosted.xs if hosted else []))
    return outs[:5], outs[5:10], list(outs[10:])


def gla_out_fwd(of, ob, gn, zg, name):
    T = of.shape[0]

    def body(of_ref, ob_ref, gn_ref, g_ref, y_ref):
        for h in range(GLA_HEADS):
            sl = slice(GLA_DV * h, GLA_DV * (h + 1))
            xn, _ = _rms_rows(of_ref[:, sl] + ob_ref[:, sl])
            g = g_ref[:, sl]
            y_ref[:, sl] = (xn * gn_ref[...] * (g * _sig(g))).astype(BF16)

    tb = _wide_block(T)
    tok = pl.BlockSpec((tb, GLA_WIDTH), lambda i: (i, 0))
    return pl.pallas_call(
        body, name=name, grid=(T // tb,),
        in_specs=[tok, tok, _full(gn.shape), tok], out_specs=tok,
        out_shape=jax.ShapeDtypeStruct((T, GLA_WIDTH), BF16),
        compiler_params=_params(("parallel",)),
    )(of, ob, gn, zg)


def gla_out_bwd(of, ob, gn, zg, dy, name):
    T = of.shape[0]

    def body(of_ref, ob_ref, gn_ref, g_ref, dy_ref, do_ref, dzg_ref, dgn_ref):
        first = pl.program_id(0) == 0
        gn_v = gn_ref[...]
        dgn = None
        for h in range(GLA_HEADS):
            sl = slice(GLA_DV * h, GLA_DV * (h + 1))
            xn, r = _rms_rows(of_ref[:, sl] + ob_ref[:, sl])
            silu, dsilu = _silu_and_grad(g_ref[:, sl])
            dyv = dy_ref[:, sl]
            dzg_ref[:, sl] = (dyv * xn * gn_v * dsilu).astype(BF16)
            dn = dyv * silu
            t = jnp.sum(dn * xn, axis=0, keepdims=True)
            dgn = t if dgn is None else dgn + t
            do_ref[:, sl] = _rms_rows_bwd(dn * gn_v, xn, r)
        _acc(dgn_ref, dgn, first)

    tb = _wide_block(T)
    tok = pl.BlockSpec((tb, GLA_WIDTH), lambda i: (i, 0))
    return pl.pallas_call(
        body, name=name, grid=(T // tb,),
        in_specs=[tok, tok, _full(gn.shape), tok, tok], out_specs=[tok, tok, _full(gn.shape)],
        out_shape=[jax.ShapeDtypeStruct((T, GLA_WIDTH), F32), jax.ShapeDtypeStruct((T, GLA_WIDTH), BF16), jax.ShapeDtypeStruct(gn.shape, F32)],
        compiler_params=_params(("arbitrary",)),
    )(of, ob, gn, zg, dy)


def merge_post_fwd(ys, zm, wbs, wo, x2, pg, ms, nb, name):
    T = x2.shape[0]

    def body(y0, y1, y2, zm_ref, w0, w1, w2, wo_ref, x_ref, pg_ref, gate_ref, xn_ref, out_ref, mg_ref):
        merged = None
        for i, (y_ref, w_ref) in enumerate(((y0, w0), (y1, w1), (y2, w2))):
            t = _sig(zm_ref[:, D_MODEL * i:D_MODEL * (i + 1)].astype(F32)) * _dot(y_ref[...], w_ref[...])
            merged = t if merged is None else merged + t
        mb = merged.astype(BF16)
        mg_ref[...] = mb
        out = _dot(mb, wo_ref[...])
        out_ref[...] = out
        on, _ = _rms_rows(out)
        xn_ref[...] = x_ref[...] + gate_ref[0] * (on * pg_ref[...])

    tok = lambda n: pl.BlockSpec((TOKEN_BLOCK, n), lambda i: (i, 0))
    return pl.pallas_call(
        body, name=name, grid=(T // TOKEN_BLOCK,),
        in_specs=[tok(512)] * 3 + [tok(3 * D_MODEL)] + [_full(w.shape) for w in wbs] + [_full(wo.shape), tok(D_MODEL), _full(pg.shape), _mod_spec(nb, 2)],
        out_specs=[tok(D_MODEL)] * 3,
        out_shape=[jax.ShapeDtypeStruct((T, D_MODEL), F32), jax.ShapeDtypeStruct((T, D_MODEL), F32), jax.ShapeDtypeStruct((T, D_MODEL), BF16)],
        compiler_params=_params(("parallel",), VMEM_LIMIT),
    )(*ys, zm, *wbs, wo, x2, pg, ms)


def merge_post_bwd(dxn, out, ys, zm, wbs, wo, pg, ms, nb, name):
    T = dxn.shape[0]
    nrow = ms.shape[0]
    row = _mod_row(nb)

    def body(dxn_ref, out_ref, y0, y1, y2, zm_ref, w0, w1, w2, wo_ref, pg_ref, gate_ref,
             dy0, dy1, dy2, dzm_ref, dout_ref, dp0, dp1, dp2, dgate_ref, dpg_ref):
        i = pl.program_id(0)
        dxn_v = dxn_ref[...]
        on, r = _rms_rows(out_ref[...])
        pg_v = pg_ref[...]
        _acc(dgate_ref.at[0], jnp.sum(dxn_v * on * pg_v, axis=0, keepdims=True), (i % nb) <= 1)
        dn = dxn_v * gate_ref[0]
        _acc(dpg_ref, jnp.sum(dn * on, axis=0, keepdims=True), i == 0)
        dout = _rms_rows_bwd(dn * pg_v, on, r).astype(BF16)
        dout_ref[...] = dout
        dmerged = _dot_nt(dout, wo_ref[...])
        for j, (y_ref, w_ref, dy_ref, dp_ref) in enumerate(((y0, w0, dy0, dp0), (y1, w1, dy1, dp1), (y2, w2, dy2, dp2))):
            sl = slice(D_MODEL * j, D_MODEL * (j + 1))
            g = _sig(zm_ref[:, sl].astype(F32))
            p = _dot(y_ref[...], w_ref[...])
            dzm_ref[:, sl] = (dmerged * p * g * (1.0 - g)).astype(BF16)
            dp = (dmerged * g).astype(BF16)
            dp_ref[...] = dp
            dy_ref[...] = _dot_nt(dp, w_ref[...])

    tok = lambda n: pl.BlockSpec((TOKEN_BLOCK, n), lambda i: (i, 0))
    return pl.pallas_call(
        body, name=name, grid=(T // TOKEN_BLOCK,),
        in_specs=[tok(D_MODEL), tok(D_MODEL)] + [tok(512)] * 3 + [tok(3 * D_MODEL)] + [_full(w.shape) for w in wbs] + [_full(wo.shape), _full(pg.shape), _mod_spec(nb, 2)],
        out_specs=[tok(512)] * 3 + [tok(3 * D_MODEL), tok(D_MODEL)] + [tok(D_MODEL)] * 3 + [pl.BlockSpec((1, 1, D_MODEL), lambda i: (row(i), 0, 0)), _full(pg.shape)],
        out_shape=[jax.ShapeDtypeStruct((T, 512), F32)] * 3 + [jax.ShapeDtypeStruct((T, 3 * D_MODEL), BF16), jax.ShapeDtypeStruct((T, D_MODEL), BF16)]
        + [jax.ShapeDtypeStruct((T, D_MODEL), BF16)] * 3 + [jax.ShapeDtypeStruct((nrow, 1, D_MODEL), F32), jax.ShapeDtypeStruct(pg.shape, F32)],
        compiler_params=_params(("arbitrary",), VMEM_LIMIT),
    )(dxn, out, *ys, zm, *wbs, wo, pg, ms)


def loss_head(y2, tgt2, nb, name):
    T = y2.shape[0]
    nlat = nb - 1

    def body(y_ref, t_ref, dy_ref, loss_ref, acc):
        i = pl.program_id(0)
        is_lat = (i % nb) > 0

        @pl.when(i == 0)
        def _():
            acc[...] = jnp.zeros_like(acc)

        @pl.when(is_lat)
        def _():
            e = y_ref[...] - t_ref[...]
            dy_ref[...] = e * (1.0 / D_MODEL)
            acc[...] += jnp.sum(e * e, axis=0, keepdims=True)

        @pl.when(jnp.logical_not(is_lat))
        def _():
            dy_ref[...] = jnp.zeros_like(dy_ref)

        @pl.when(i == pl.num_programs(0) - 1)
        def _():
            loss_ref[...] = jnp.sum(acc[...], axis=1, keepdims=True) * (0.5 / D_MODEL)

    tok = pl.BlockSpec((TOKEN_BLOCK, D_MODEL), lambda i: (i, 0))
    tgt = pl.BlockSpec((TOKEN_BLOCK, D_MODEL), lambda i: ((i // nb) * nlat + jnp.maximum(i % nb - 1, 0), 0))
    return pl.pallas_call(
        body, name=name, grid=(T // TOKEN_BLOCK,),
        in_specs=[tok, tgt], out_specs=[tok, _full((1, 1))],
        out_shape=[jax.ShapeDtypeStruct((T, D_MODEL), F32), jax.ShapeDtypeStruct((1, 1), F32)],
        scratch_shapes=[pltpu.VMEM((1, D_MODEL), F32)],
        compiler_params=_params(("arbitrary",)),
    )(y2, tgt2)


_IN_OFFS = tuple(int(o) for o in np.cumsum((0,) + IN_SIZES))
_IN_GROUPS = (("a", 0, 416, 512), ("mg", 416, 512, 512), ("px", 928, 512, 512), ("pg", 1440, 512, 512), ("gq", 1952, 256, 256),
              ("gk", 2208, 256, 256), ("gv", 2464, 512, 512), ("lr", 2976, 32, 128), ("gg", 3008, 512, 512), ("m", 3520, 3072, 3072))


def _pad_cols(w, n):
    return w if w.shape[1] == n else jnp.pad(w, ((0, 0), (0, n - w.shape[1])))


def layer_weights(w_in, w_uq, w_ukv, af_w2, ab_w2, wbm, wbp, wbg, w_out):
    W = {}
    for nm, off, n, npad in _IN_GROUPS:
        W["in_" + nm] = _pad_cols(w_in[:, off:off + n], npad)
    uq = w_uq.reshape(MLA_Q_RANK, MLA_HEADS, MLA_NOPE + MLA_ROPE)
    W["qn"] = jnp.pad(uq[:, :, :MLA_NOPE], ((0, 0), (0, 0), (0, LANES - MLA_NOPE))).reshape(MLA_Q_RANK, MLA_HEADS * LANES)
    W["qr"] = jnp.pad(uq[:, :, MLA_NOPE:], ((0, 0), (0, 0), (0, LANES - MLA_ROPE))).reshape(MLA_Q_RANK, MLA_HEADS * LANES)
    W["kv"] = w_ukv
    W["af"] = jnp.pad(af_w2, ((0, LANES - GLA_GATE_RANK), (0, 0)))
    W["ab"] = jnp.pad(ab_w2, ((GLA_GATE_RANK, LANES - 2 * GLA_GATE_RANK), (0, 0)))
    W["bm"], W["bp"], W["bg"], W["out"] = wbm, wbp, wbg, w_out
    return W


def rope_tables(L, n_ctx):
    t = np.arange(L - n_ctx)
    half = MLA_ROPE // 2
    inv = ROPE_BASE ** (-np.arange(0, half, 2, dtype=np.float32) / half)
    ang_r = (t // GRID_W).astype(np.float32)[:, None] * inv
    ang_c = (t % GRID_W).astype(np.float32)[:, None] * inv
    ang = jnp.asarray(np.concatenate([ang_r, ang_r, ang_c, ang_c], axis=-1), F32)
    cos = jnp.ones((L, LANES), F32).at[n_ctx:, :MLA_ROPE].set(jnp.cos(ang))
    sin = jnp.zeros((L, LANES), F32).at[n_ctx:, :MLA_ROPE].set(jnp.sin(ang))
    return cos, sin


def layer_fwd(x2, ms, W, P, cos, sin, B, L, n_ctx, tag, hosted=None):
    nb = L // TOKEN_BLOCK
    r3 = lambda a: a.reshape(B, L, a.shape[-1])
    r2 = lambda a: a.reshape(B * L, a.shape[-1])
    names = [g[0] for g in _IN_GROUPS[:-1]]
    h, zs = norm_in_proj(x2, P["pre"], ms, [W["in_" + n] for n in names], nb, tag + "in_proj")
    z = dict(zip(names, zs))
    (z["m"],) = mm_multi(h, [W["in_m"]], [BF16], tag + "in_proj_merge", tm=_wide_block(L))
    qn, qr, kv, kr = mla_prep_fwd(z["a"], P["qg"], P["kvg"], W["qn"], W["qr"], W["kv"], cos, sin, nb, tag + "mla_prep")
    ya, y_mla, lse, carried = attention_fwd(r3(qn), r3(qr), r3(kv), r3(kr), r3(z["mg"]), n_ctx, tag + "attention", hosted)
    y_pool = pool_fwd(r3(z["px"]), r3(z["pg"]), P["pw"], P["ps"], n_ctx, tag + "pool")
    qf, kf, ksf, tf, qb, kb, ksb, tb = gla_prep_fwd(z["lr"], z["gq"], z["gk"], W["af"], W["ab"], P["baf"], P["bab"], tag + "gla_prep")
    (of, ssf), (ob, ssb) = gla_scan_fwd([(r3(qf), r3(kf), r3(ksf), r3(tf)), (r3(qb), r3(kb), r3(ksb), r3(tb))], r3(z["gv"]), n_ctx, tag + "gla_scan")
    y_gla = gla_out_fwd(r2(of), r2(ob), P["gn"], z["gg"], tag + "gla_out")
    ys = [r2(y_mla), r2(y_pool), y_gla]
    x_new, out, merged = merge_post_fwd(ys, z["m"], [W["bm"], W["bp"], W["bg"]], W["out"], x2, P["post"], ms, nb, tag + "merge_post")
    res = dict(x2=x2, h=h, z=z, qn=qn, qr=qr, kv=kv, kr=kr, ya=ya, lse=lse, ys=ys, gla_f=(qf, kf, ksf, tf, ssf), gla_b=(qb, kb, ksb, tb, ssb),
               of=of, ob=ob, out=out, merged=merged)
    return x_new, res, carried


def layer_bwd(dxn, res, ms, W, P, cos, sin, B, L, n_ctx, tag, hosted=None, host_own=None):
    nb = L // TOKEN_BLOCK
    r3 = lambda a: a.reshape(B, L, a.shape[-1])
    r2 = lambda a: a.reshape(B * L, a.shape[-1])
    z = res["z"]
    ys = res["ys"]
    wbs = [W["bm"], W["bp"], W["bg"]]
    dy0, dy1, dy2, dzm, dout, dp0, dp1, dp2, dgate, dpost = merge_post_bwd(dxn, res["out"], ys, z["m"], wbs, W["out"], P["post"], ms, nb, tag + "merge_post_bwd")
    G = {"out": mm_dw(res["merged"], dout, tag + "dw_out"), "post": dpost}
    for nm, y, dp in zip(("bm", "bp", "bg"), ys, (dp0, dp1, dp2)):
        G[nm] = mm_dw(y, dp, tag + "dw_" + nm)
    g = {n: _natural_grad(G, n) for n in GRADS_EARLY}
    own = {}
    dz = {"m": dzm}
    do, dz["gg"], G["gn"] = gla_out_bwd(r2(res["of"]), r2(res["ob"]), P["gn"], z["gg"], dy2, tag + "gla_out_bwd")
    carrier = host_own(GRADS_EARLY, g) if host_own else None
    *grads, arrived = gla_scan_bwd([(r3(qt), r3(kt), r3(ks), r3(tot), ss) for qt, kt, ks, tot, ss in (res["gla_f"], res["gla_b"])],
                                   r3(z["gv"]), r3(do), n_ctx, tag + "gla_scan_bwd", carrier)
    own.update(zip(GRADS_EARLY, arrived))
    gf = [r2(a) for a in grads[0]]
    gb = [r2(a) for a in grads[1]]
    dz["lr"], dz["gq"], dz["gk"], G["af"], G["ab"], G["baf"], G["bab"], dz["gv"] = gla_prep_bwd(
        z["lr"], z["gq"], z["gk"], W["af"], W["ab"], P["baf"], P["bab"], gf[:3] + gf[4:], gb[:3] + gb[4:], [gf[3], gb[3]], tag + "gla_prep_bwd")
    dpx, dpg, G["pw"], G["ps"] = pool_bwd(r3(z["px"]), r3(z["pg"]), P["pw"], P["ps"], r3(dy1), n_ctx, tag + "pool_bwd")
    dz["px"], dz["pg"] = r2(dpx), r2(dpg)
    dqn, dqr, dkv, dkr, dzmg, got = attention_bwd(r3(res["qn"]), r3(res["qr"]), r3(res["kv"]), r3(res["kr"]), r3(z["mg"]), res["ya"], res["lse"],
                                                  r3(dy0), n_ctx, tag + "attention_bwd", hosted)
    dz["mg"] = r2(dzmg)
    dz["a"], G["qn"], G["qr"], G["kv"], G["qg"], G["kvg"] = mla_prep_bwd(
        r2(dqn), r2(dqr), r2(dkv), r2(dkr), z["a"], P["qg"], P["kvg"], W["qn"], W["qr"], W["kv"], cos, sin, nb, tag + "mla_prep_bwd")
    names = [grp[0] for grp in _IN_GROUPS]
    for n in names:
        G["in_" + n] = mm_dw(res["h"], dz[n], tag + "dw_in_" + n)
    g.update({n: _natural_grad(G, n) for n in GRADS_LATE})
    carrier = host_own(GRADS_LATE, g) if host_own else None
    (dx, dshift, dscale, G["pre"]), arrived = in_proj_norm_bwd([dz[n] for n in names], [W["in_" + n] for n in names], res["x2"], P["pre"], ms, dxn, nb,
                                                               tag + "in_proj_dx", carrier)
    own.update(zip(GRADS_LATE, arrived))
    g.update({n: _natural_grad(G, n) for n in GRADS_REPLICATED})
    dms = jnp.concatenate([dshift, dscale, dgate], axis=-1)
    return dx, g, dms, got, own


GRADS_EARLY = ("w_branch_mla", "w_branch_pool", "w_branch_gla", "w_out")
GRADS_LATE = ("w_in", "mla_w_uq", "mla_w_ukv", "gla_af_w2", "gla_ab_w2")
GRADS_REPLICATED = ("pre_norm", "post_norm", "mla_q_norm", "mla_kv_norm", "pool_w", "pool_scale", "gla_af_b", "gla_ab_b", "gla_norm")
_DIRECT = dict(mla_w_ukv="kv", w_branch_mla="bm", w_branch_pool="bp", w_branch_gla="bg", w_out="out", pool_w="pw")
_ROW = dict(pre_norm="pre", post_norm="post", mla_q_norm="qg", mla_kv_norm="kvg", pool_scale="ps", gla_af_b="baf", gla_ab_b="bab", gla_norm="gn")


def _natural_grad(G, name):
    if name == "w_in":
        parts = {off: G["in_" + nm][:, :n] for nm, off, n, npad in _IN_GROUPS}
        return jnp.concatenate([parts[o] for o in sorted(parts)], axis=1)
    if name == "mla_w_uq":
        gqn = G["qn"].reshape(MLA_Q_RANK, MLA_HEADS, LANES)[:, :, :MLA_NOPE]
        gqr = G["qr"].reshape(MLA_Q_RANK, MLA_HEADS, LANES)[:, :, :MLA_ROPE]
        return jnp.concatenate([gqn, gqr], axis=-1).reshape(MLA_Q_RANK, MLA_HEADS * (MLA_NOPE + MLA_ROPE))
    if name == "gla_af_w2":
        return G["af"][:GLA_GATE_RANK]
    if name == "gla_ab_w2":
        return G["ab"][GLA_GATE_RANK:2 * GLA_GATE_RANK]
    return G[_DIRECT[name]] if name in _DIRECT else G[_ROW[name]][0]


def local_step(x, c, ctx, c_ctx, small, loss_target, depth, layer_full, host_fwd=None, host_bwd=None, host_own=None):
    B, S, _ = x.shape
    n_ctx = ctx.shape[1]
    L = n_ctx + S
    nb = L // TOKEN_BLOCK
    cos, sin = rope_tables(L, n_ctx)
    x2 = jnp.concatenate([ctx, x], axis=1).reshape(B * L, D_MODEL)
    a8 = jnp.zeros((8, D_MODEL), F32).at[:B].set(c).at[B].set(c_ctx)
    Ws, Ps, mss, ress, mod_ws = [], [], [], [], []
    carried = None
    for l in range(depth):
        tag = f"l{l}_"
        full = layer_full(l, carried)
        W = layer_weights(full["w_in"], full["mla_w_uq"], full["mla_w_ukv"], full["gla_af_w2"], full["gla_ab_w2"],
                          full["w_branch_mla"], full["w_branch_pool"], full["w_branch_gla"], full["w_out"])
        P = dict(pre=small["pre_norm"][l][None], post=small["post_norm"][l][None], qg=small["mla_q_norm"][l][None], kvg=small["mla_kv_norm"][l][None],
                 pw=small["pool_w"][l].astype(BF16), ps=small["pool_scale"][l][None], baf=small["gla_af_b"][l][None], bab=small["gla_ab_b"][l][None],
                 gn=small["gla_norm"][l][None])
        mod8 = mod_fwd(a8, full["mod_w"], small["mod_b"][l][None], tag + "mod")
        ms = jnp.stack([jnp.broadcast_to(mod8[B], (B, 3 * D_MODEL)), mod8[:B]], axis=1).reshape(2 * B, 1, 3 * D_MODEL)
        x2, res, carried = layer_fwd(x2, ms, W, P, cos, sin, B, L, n_ctx, tag, host_fwd(l) if host_fwd else None)
        Ws.append(W), Ps.append(P), mss.append(ms), ress.append(res), mod_ws.append(full["mod_w"])
    dx, loss = loss_head(x2, loss_target.reshape(B * S, D_MODEL), nb, "loss_head")
    grads = [None] * depth
    delivered = [None] * depth
    dz8s = [None] * depth
    da8 = None
    for l in reversed(range(depth)):
        tag = f"l{l}_"
        hosted = host_bwd(l, grads[l + 1]) if host_bwd and l + 1 < depth else None
        dx, g, dms, got, own = layer_bwd(dx, ress[l], mss[l], Ws[l], Ps[l], cos, sin, B, L, n_ctx, tag, hosted, host_own(l) if host_own else None)
        if hosted:
            delivered[l + 1] = got
        if own:
            delivered[l] = own
        dms = dms.reshape(B, 2, 3 * D_MODEL)
        dz8s[l] = jnp.zeros((8, 3 * D_MODEL), F32).at[:B].set(dms[:, 1]).at[B].set(jnp.sum(dms[:, 0], axis=0))
        g_mod_b, da = mod_bwd(a8, mod_ws[l], dz8s[l], tag + "mod_bwd")
        da8 = da if da8 is None else da8 + da
        g["mod_b"] = g_mod_b[0]
        grads[l] = g
    grad_x = dx.reshape(B, L, D_MODEL)[:, n_ctx:]
    return loss, grad_x, grads, da8[B], delivered, (a8, dz8s)


_MESH_ID = pl.DeviceIdType.MESH
_HBM = pl.BlockSpec(memory_space=pltpu.HBM)


def _me_and_peers():
    mx, my, mc = lax.axis_index("x"), lax.axis_index("y"), lax.axis_index("c")
    peers = []
    for k in range(1, N_DEV):
        px, py, pc = mx ^ ((k >> 2) & 1), my ^ ((k >> 1) & 1), mc ^ (k & 1)
        peers.append(((px, py, pc), 4 * px + 2 * py + pc))
    return 4 * mx + 2 * my + mc, peers


def _comm_scratch(n):
    return [pltpu.SemaphoreType.DMA((n * (N_DEV - 1),)), pltpu.SemaphoreType.DMA((n * (N_DEV - 1),)), pltpu.SemaphoreType.DMA((n,))]


class _Gather:
    def __init__(self, x_refs, o_refs, send_sems, recv_sems, local_sems):
        self.x, self.o, self.send, self.recv, self.local = x_refs, o_refs, send_sems, recv_sems, local_sems
        self.n = len(x_refs)
        mx, my, mc = lax.axis_index("x"), lax.axis_index("y"), lax.axis_index("c")
        self.me, self.sibling, self.mc = (mx, my, mc), (mx, my, 1 - mc), mc
        self.chips = [(1 - mx, my), (mx, 1 - my), (1 - mx, 1 - my)]

    @staticmethod
    def out_shape(xs):
        return [jax.ShapeDtypeStruct((N_DEV,) + x.shape, x.dtype) for x in xs]

    def _copy(self, i, k, block, to, src=None):
        px, py, pc = block
        dst = self.o[i].at[4 * px + 2 * py + pc]
        sem = (N_DEV - 1) * i + k
        return pltpu.make_async_remote_copy(src_ref=dst if src is None else src, dst_ref=dst, send_sem=self.send.at[sem],
                                            recv_sem=self.recv.at[sem], device_id=to, device_id_type=_MESH_ID)

    def _mine(self, i):
        mx, my, mc = self.me
        return pltpu.make_async_copy(self.x[i], self.o[i].at[4 * mx + 2 * my + mc], self.local.at[i])

    def _first(self):
        out = []
        for i in range(self.n):
            out.append(self._copy(i, 0, self.me, self.sibling, src=self.x[i]))
            out += [self._copy(i, 1 + j, self.me, (*chip, self.mc), src=self.x[i]) for j, chip in enumerate(self.chips)]
        return out

    def _passed(self, j, i):
        return self._copy(i, 4 + j, (*self.chips[j], self.mc), self.sibling)

    def start(self):
        for i in range(self.n):
            self._mine(i).start()
        for cp in self._first():
            cp.start()

    def forward(self):
        for j, chip in enumerate(self.chips):
            for i in range(self.n):
                self._copy(i, 1 + j, (*chip, self.mc), self.me).wait_recv()
                self._passed(j, i).start()

    def finish(self):
        for i in range(self.n):
            self._copy(i, 0, self.sibling, self.me).wait_recv()
            for j, chip in enumerate(self.chips):
                self._copy(i, 4 + j, (*chip, 1 - self.mc), self.me).wait_recv()
        for cp in self._first():
            cp.wait_send()
        for j in range(len(self.chips)):
            for i in range(self.n):
                self._passed(j, i).wait_send()
        for i in range(self.n):
            self._mine(i).wait()


class _Scatter:
    def __init__(self, x_refs, o_refs, send_sems, recv_sems, local_sems):
        self.x, self.o, self.send, self.recv, self.local = x_refs, o_refs, send_sems, recv_sems, local_sems
        self.n = len(x_refs)
        self.me, self.peers = _me_and_peers()

    @staticmethod
    def out_shape(xs):
        return [jax.ShapeDtypeStruct(x.shape, x.dtype) for x in xs]

    def _copy(self, i, k, src_slot, dst_slot, to):
        sem = (N_DEV - 1) * i + k
        return pltpu.make_async_remote_copy(src_ref=self.x[i].at[src_slot], dst_ref=self.o[i].at[dst_slot], send_sem=self.send.at[sem],
                                            recv_sem=self.recv.at[sem], device_id=to, device_id_type=_MESH_ID)

    def _mine(self, i):
        return pltpu.make_async_copy(self.x[i].at[self.me], self.o[i].at[self.me], self.local.at[i])

    def _sends(self):
        return [self._copy(i, k, slot, self.me, peer) for k, (peer, slot) in enumerate(self.peers) for i in range(self.n)]

    def start(self):
        for i in range(self.n):
            self._mine(i).start()
        for cp in self._sends():
            cp.start()

    def forward(self):
        pass

    def finish(self):
        for k, (peer, slot) in enumerate(self.peers):
            for i in range(self.n):
                self._copy(i, k, slot, slot, peer).wait_recv()
        for cp in self._sends():
            cp.wait_send()
        for i in range(self.n):
            self._mine(i).wait()


class _Hosted:
    def __init__(self, kind, xs):
        self.kind, self.xs, self.n = kind, list(xs), len(xs)
        self.in_specs = [_HBM] * self.n
        self.out_specs = [_HBM] * self.n
        self.out_shape = kind.out_shape(self.xs)
        self.scratch = _comm_scratch(self.n)

    def run(self, x_refs, o_refs, sems, step, total):
        for when, phase in ((0, "start"), (3 * total // 4, "forward"), (total - 1, "finish")):
            @pl.when(step == when)
            def _(phase=phase):
                getattr(self.kind(x_refs, o_refs, *sems), phase)()


def gather_blocks(xs, name):
    n = len(xs)

    def body(*refs):
        g = _Gather(refs[:n], refs[n:2 * n], *refs[2 * n:])
        g.start()
        g.forward()
        g.finish()

    return pl.pallas_call(
        body, name=name, in_specs=[_HBM] * n, out_specs=[_HBM] * n,
        out_shape=_Gather.out_shape(xs), scratch_shapes=_comm_scratch(n),
    )(*xs)


def reduce_adamw(slots, w, m, v, name, tr=256):
    R, C = w.shape
    nl = len(slots)
    ns = slots[0].shape[0]
    rows = R // nl
    tr = min(tr, rows)
    nbl = rows // tr
    c1 = 1.0 / (1.0 - ADAM_B1 ** ADAM_STEP)
    c2 = 1.0 / (1.0 - ADAM_B2 ** ADAM_STEP)

    def body(*refs):
        w_ref, m_ref, v_ref, g_ref, d_ref, nm_ref, nv_ref = refs[nl:]
        part = pl.program_id(0) // nbl
        g = None
        for l, s_ref in enumerate(refs[:nl]):
            gl = s_ref[0].astype(F32)
            for s in range(1, ns):
                gl = gl + s_ref[s].astype(F32)
            g = gl if g is None else jnp.where(part == l, gl, g)
        nm = ADAM_B1 * m_ref[...] + (1.0 - ADAM_B1) * g
        nv = ADAM_B2 * v_ref[...] + (1.0 - ADAM_B2) * (g * g)
        g_ref[...] = g
        nm_ref[...] = nm
        nv_ref[...] = nv
        d_ref[...] = -ADAM_LR * ((nm * c1) / (jnp.sqrt(nv * c2) + ADAM_EPS) + ADAM_WD * w_ref[...])

    blk = pl.BlockSpec((tr, C), lambda i: (i, 0))
    sspecs = [pl.BlockSpec((ns, tr, C), lambda i, l=l: (0, jnp.clip(i - l * nbl, 0, nbl - 1), 0)) for l in range(nl)]
    return pl.pallas_call(
        body, name=name, grid=(R // tr,),
        in_specs=sspecs + [blk, blk, blk], out_specs=[blk] * 4,
        out_shape=[jax.ShapeDtypeStruct((R, C), F32)] * 4,
        compiler_params=_params(("parallel",), VMEM_LIMIT),
    )(*slots, w, m, v)


ARG_WEIGHTS = ("c_ctx", "mod_w", "mod_b", "pre_norm", "post_norm", "w_in", "mla_q_norm", "mla_w_uq", "mla_kv_norm", "mla_w_ukv", "pool_w",
               "pool_scale", "gla_af_w2", "gla_af_b", "gla_ab_w2", "gla_ab_b", "gla_norm", "w_branch_mla", "w_branch_pool", "w_branch_gla", "w_out")
SHARDED = ("mod_w", "w_in", "mla_w_uq", "mla_w_ukv", "gla_af_w2", "gla_ab_w2", "w_branch_mla", "w_branch_pool", "w_branch_gla", "w_out")
ROW_SHARDED = ("w_out",)
REPLICATED = tuple(n for n in ARG_WEIGHTS if n not in SHARDED)
PACK_ROWS = 512


def _pack(parts, dtype):
    flat = jnp.concatenate([p.astype(dtype).reshape(-1) for p in parts])
    n = flat.shape[0]
    total = -(-n // (PACK_ROWS * LANES)) * (PACK_ROWS * LANES)
    return jnp.pad(flat, (0, total - n)).reshape(total // LANES, LANES)


def _unpack(buf, shapes):
    flat = buf.reshape(-1)
    out, off = [], 0
    for shp in shapes:
        n = math.prod(shp)
        out.append(flat[off:off + n].reshape(shp))
        off += n
    return out


def _gathered_to_full(g, name):
    _, r, cs = g.shape
    if name in ROW_SHARDED:
        return g.reshape(N_DEV * r, cs)
    return g.transpose(1, 0, 2).reshape(r, N_DEV * cs)


def _full_to_slots(w, name):
    if name in ROW_SHARDED:
        return w.reshape(N_DEV, w.shape[0] // N_DEV, w.shape[1])
    return w.reshape(w.shape[0], N_DEV, w.shape[1] // N_DEV).transpose(1, 0, 2)


def kernel(x, c, ctx, c_ctx, mod_w, mod_b, pre_norm, post_norm, w_in, mla_q_norm, mla_w_uq, mla_kv_norm, mla_w_ukv, pool_w, pool_scale, gla_af_w2, gla_af_b, gla_ab_w2, gla_ab_b, gla_norm, w_branch_mla, w_branch_pool, w_branch_gla, w_out, loss_target, m_c_ctx, m_mod_w, m_mod_b, m_pre_norm, m_post_norm, m_w_in, m_mla_q_norm, m_mla_w_uq, m_mla_kv_norm, m_mla_w_ukv, m_pool_w, m_pool_scale, m_gla_af_w2, m_gla_af_b, m_gla_ab_w2, m_gla_ab_b, m_gla_norm, m_w_branch_mla, m_w_branch_pool, m_w_branch_gla, m_w_out, v_c_ctx, v_mod_w, v_mod_b, v_pre_norm, v_post_norm, v_w_in, v_mla_q_norm, v_mla_w_uq, v_mla_kv_norm, v_mla_w_ukv, v_pool_w, v_pool_scale, v_gla_af_w2, v_gla_af_b, v_gla_ab_w2, v_gla_ab_b, v_gla_norm, v_w_branch_mla, v_w_branch_pool, v_w_branch_gla, v_w_out):
    local = dict(locals())
    wts = {n: local[n] for n in ARG_WEIGHTS}
    mom1 = {n: local["m_" + n] for n in ARG_WEIGHTS}
    mom2 = {n: local["v_" + n] for n in ARG_WEIGHTS}
    shard_shapes = [wts[n].shape for n in SHARDED]
    rep_shapes = [wts[n].shape for n in REPLICATED]
    kinds = ("grad", "delta", "new_m", "new_v")

    depth = w_in.shape[0]

    def shards(l):
        return [wts[n][l].astype(BF16) for n in SHARDED]

    first = gather_blocks(shards(0), "gather_weights_l0")

    def layer_full(l, carried):
        return {n: _gathered_to_full(gw, n) for n, gw in zip(SHARDED, first if l == 0 else carried)}

    def host_fwd(l):
        return _Hosted(_Gather, shards(l + 1)) if l + 1 < depth else None

    exchanged = GRADS_EARLY + GRADS_LATE

    def slots(g, names):
        return [_full_to_slots(g[n], n).astype(BF16) for n in names]

    def host_bwd(l, g_above):
        return _Hosted(_Scatter, slots(g_above, exchanged))

    def host_own(l):
        return (lambda names, g: _Hosted(_Scatter, slots(g, names))) if l == 0 else None

    small = {n: wts[n] for n in REPLICATED}
    loss, grad_x, grads, g_c_ctx, arrived, (a8, dz8s) = local_step(x, c, ctx, c_ctx, small, loss_target, depth, layer_full, host_fwd, host_bwd, host_own)
    arrived = [a if isinstance(a, dict) else dict(zip(exchanged, a)) for a in arrived]

    g = {n: (g_c_ctx if n == "c_ctx" else jnp.stack([grads[l][n] for l in range(depth)])) for n in REPLICATED}
    gathered, a_all, dz_all = gather_blocks([_pack([g[n] for n in REPLICATED], BF16), a8, jnp.concatenate(dz8s, axis=0)], "gather_small_grads")
    outs = reduce_adamw([gathered], _pack([wts[n] for n in REPLICATED], F32), _pack([mom1[n] for n in REPLICATED], F32),
                        _pack([mom2[n] for n in REPLICATED], F32), "adamw_replicated")
    me = 4 * lax.axis_index("x") + 2 * lax.axis_index("y") + lax.axis_index("c")
    ncol = mod_w.shape[2]
    dz_cols = lax.dynamic_slice_in_dim(dz_all.reshape(N_DEV, depth, 8, 3 * D_MODEL), me * ncol, ncol, axis=3)
    g_mod_w = mod_dw_columns(a_all.reshape(N_DEV * 8, D_MODEL), dz_cols.transpose(1, 0, 2, 3).reshape(depth, N_DEV * 8, ncol), "mod_dw")

    res = {kind: {} for kind in kinds}
    for n, shp in zip(SHARDED, shard_shapes):
        flat = (shp[0] * shp[1], shp[2])
        parts = [g_mod_w.reshape((1,) + flat)] if n == "mod_w" else [arrived[l][n] for l in range(depth)]
        for kind, o in zip(kinds, reduce_adamw(parts, wts[n].reshape(flat), mom1[n].reshape(flat), mom2[n].reshape(flat), "adamw_" + n)):
            res[kind][n] = o.reshape(shp)
    for kind, o in zip(("grad", "delta", "new_m", "new_v"), outs):
        res[kind].update(zip(REPLICATED, _unpack(o, rep_shapes)))

    loss = lax.psum(loss[0, 0], ("x", "y", "c"))
    return (loss, grad_x, *[res[kind][n] for kind in ("grad", "delta", "new_m", "new_v") for n in ARG_WEIGHTS])
```

```python
import functools
import math

import jax
import jax.numpy as jnp
import numpy as np
from jax import lax
from jax.experimental import pallas as pl
from jax.experimental.pallas import tpu as pltpu

F32 = jnp.float32
BF16 = jnp.bfloat16

D_MODEL = 1024
NORM_EPS = 1e-6
GRID_W = 64
MLA_HEADS, MLA_Q_RANK, MLA_KV_RANK, MLA_NOPE, MLA_ROPE, MLA_V = 8, 256, 128, 64, 32, 64
MLA_WIDTH = MLA_HEADS * MLA_V
ROPE_BASE = 10000.0
ATT_SCALE = (MLA_NOPE + MLA_ROPE) ** -0.5
POOL_WINDOWS = (2, 4, 8, 16)
POOL_WIDTH, POOL_GROUP = 512, 128
GLA_HEADS, GLA_DK, GLA_DV = 4, 64, 128
GLA_KW, GLA_WIDTH = GLA_HEADS * GLA_DK, GLA_HEADS * GLA_DV
GLA_GATE_RANK, GLA_TAU, GLA_CHUNK = 16, 16.0, 64
IN_SIZES = (256, 128, 32, 512, 512, 512, 256, 256, 512, 16, 16, 512, 3 * D_MODEL)
ADAM_LR, ADAM_B1, ADAM_B2, ADAM_EPS, ADAM_WD, ADAM_STEP = 0.001, 0.9, 0.999, 1e-08, 0.01, 10
N_DEV = 8

LANES = 128
TOKEN_BLOCK = 256
WIDE_BLOCKS = (1152, 768)
VMEM_LIMIT = 48 * 1024 * 1024
NEG_BIG = -1e30

_NT = (((1,), (1,)), ((), ()))
_TN = (((0,), (0,)), ((), ()))


def _dot(a, b):
    return jnp.dot(a, b, preferred_element_type=F32)


def _dot_nt(a, b):
    return lax.dot_general(a, b, _NT, preferred_element_type=F32)


def _dot_tn(a, b):
    return lax.dot_general(a, b, _TN, preferred_element_type=F32)


def _params(sem=None, vmem=None):
    kw = {}
    if sem is not None:
        kw["dimension_semantics"] = sem
    if vmem is not None:
        kw["vmem_limit_bytes"] = vmem
    return pltpu.CompilerParams(**kw)


def _wide_block(rows):
    return next(t for t in WIDE_BLOCKS + (TOKEN_BLOCK,) if rows % t == 0)


def _full(shape):
    n = len(shape)
    return pl.BlockSpec(shape, lambda *_: (0,) * n)


def _sig(x):
    return 1.0 / (1.0 + jnp.exp(-x))


def _silu_and_grad(x):
    s = _sig(x)
    return x * s, s * (1.0 + x * (1.0 - s))


def _acc(ref, val, first):
    @pl.when(first)
    def _():
        ref[...] = val

    @pl.when(jnp.logical_not(first))
    def _():
        ref[...] += val


def mm_multi(a, ws, dtypes, name, tm=TOKEN_BLOCK):
    M, K = a.shape
    nw = len(ws)

    def body(a_ref, *refs):
        av = a_ref[...]
        for w_ref, o_ref in zip(refs[:nw], refs[nw:]):
            o_ref[...] = _dot(av, w_ref[...]).astype(o_ref.dtype)

    return pl.pallas_call(
        body, name=name, grid=(M // tm,),
        in_specs=[pl.BlockSpec((tm, K), lambda i: (i, 0))] + [_full(w.shape) for w in ws],
        out_specs=[pl.BlockSpec((tm, w.shape[1]), lambda i: (i, 0)) for w in ws],
        out_shape=[jax.ShapeDtypeStruct((M, w.shape[1]), dt) for w, dt in zip(ws, dtypes)],
        compiler_params=_params(("parallel",), VMEM_LIMIT),
    )(a, *ws)


def mm_dw(a, dz, name, tn=1024):
    M, K = a.shape
    n = dz.shape[1]
    tn = min(tn, n)
    tk = next(t for t in (3072, 1536, 1024, 512, TOKEN_BLOCK) if M % t == 0)

    def body(a_ref, dz_ref, o_ref):
        _acc(o_ref, _dot_tn(a_ref[...], dz_ref[...]), pl.program_id(1) == 0)

    return pl.pallas_call(
        body, name=name, grid=(n // tn, M // tk),
        in_specs=[pl.BlockSpec((tk, K), lambda j, k: (k, 0)), pl.BlockSpec((tk, tn), lambda j, k: (k, j))],
        out_specs=pl.BlockSpec((K, tn), lambda j, k: (0, j)),
        out_shape=jax.ShapeDtypeStruct((K, n), F32),
        compiler_params=_params(("parallel", "arbitrary"), VMEM_LIMIT),
    )(a, dz)


def mod_fwd(a8, w, b, name):
    tn = D_MODEL

    def body(a_ref, w_ref, b_ref, o_ref):
        a = a_ref[...]
        o_ref[...] = _dot((a * _sig(a)).astype(BF16), w_ref[...]) + b_ref[...]

    return pl.pallas_call(
        body, name=name, grid=(3,),
        in_specs=[_full(a8.shape), pl.BlockSpec((D_MODEL, tn), lambda j: (0, j)), pl.BlockSpec((1, tn), lambda j: (0, j))],
        out_specs=pl.BlockSpec((8, tn), lambda j: (0, j)),
        out_shape=jax.ShapeDtypeStruct((8, 3 * D_MODEL), F32),
        compiler_params=_params(("parallel",)),
    )(a8, w, b)


def mod_bwd(a8, w, dz8, name):
    tn = D_MODEL

    def body(a_ref, w_ref, dz_ref, db_ref, da_ref):
        a = a_ref[...]
        _, dsa = _silu_and_grad(a)
        dz = dz_ref[...]
        db_ref[...] = jnp.sum(dz, axis=0, keepdims=True)
        _acc(da_ref, _dot_nt(dz.astype(BF16), w_ref[...]) * dsa, pl.program_id(0) == 0)

    return pl.pallas_call(
        body, name=name, grid=(3,),
        in_specs=[_full(a8.shape), pl.BlockSpec((D_MODEL, tn), lambda j: (0, j)), pl.BlockSpec((8, tn), lambda j: (0, j))],
        out_specs=[pl.BlockSpec((1, tn), lambda j: (0, j)), _full((8, D_MODEL))],
        out_shape=[jax.ShapeDtypeStruct((1, 3 * D_MODEL), F32), jax.ShapeDtypeStruct((8, D_MODEL), F32)],
        compiler_params=_params(("arbitrary",)),
    )(a8, w, dz8)


def mod_dw_columns(a_all, dz_cols, name):
    depth, R, n = dz_cols.shape

    def body(a_ref, dz_ref, dw_ref):
        a = a_ref[...]
        dw_ref[0] = _dot_tn((a * _sig(a)).astype(BF16), dz_ref[0].astype(BF16))

    return pl.pallas_call(
        body, name=name, grid=(depth,),
        in_specs=[_full(a_all.shape), pl.BlockSpec((1, R, n), lambda l: (l, 0, 0))],
        out_specs=pl.BlockSpec((1, D_MODEL, n), lambda l: (l, 0, 0)),
        out_shape=jax.ShapeDtypeStruct((depth, D_MODEL, n), F32),
        compiler_params=_params(("parallel",)),
    )(a_all, dz_cols)


def _mod_row(nb):
    return lambda i: 2 * (i // nb) + jnp.minimum(i % nb, 1)


def _mod_spec(nb, part):
    row = _mod_row(nb)
    return pl.BlockSpec((1, 1, D_MODEL), lambda i: (row(i), 0, part))


def norm_in_proj(x2, g, ms, ws, nb, name):
    T = x2.shape[0]
    nw = len(ws)

    def body(x_ref, g_ref, sh_ref, sc_ref, *refs):
        x = x_ref[...]
        r = lax.rsqrt(jnp.mean(x * x, axis=-1, keepdims=True) + NORM_EPS)
        h = ((x * r) * g_ref[...] * (1.0 + sc_ref[0]) + sh_ref[0]).astype(BF16)
        refs[nw][...] = h
        for w_ref, o_ref in zip(refs[:nw], refs[nw + 1:]):
            o_ref[...] = _dot(h, w_ref[...])

    tok = lambda n: pl.BlockSpec((TOKEN_BLOCK, n), lambda i: (i, 0))
    outs = pl.pallas_call(
        body, name=name, grid=(T // TOKEN_BLOCK,),
        in_specs=[tok(D_MODEL), _full((1, D_MODEL)), _mod_spec(nb, 0), _mod_spec(nb, 1)] + [_full(w.shape) for w in ws],
        out_specs=[tok(D_MODEL)] + [tok(w.shape[1]) for w in ws],
        out_shape=[jax.ShapeDtypeStruct((T, D_MODEL), BF16)] + [jax.ShapeDtypeStruct((T, w.shape[1]), F32) for w in ws],
        compiler_params=_params(("parallel",), VMEM_LIMIT),
    )(x2, g, ms, ms, *ws)
    return outs[0], outs[1:]


def in_proj_norm_bwd(dzs, ws, x2, g, ms, dxres, nb, name, hosted=None):
    T = x2.shape[0]
    nw = len(ws)
    nx = hosted.n if hosted else 0
    nrow = ms.shape[0]
    row = _mod_row(nb)
    n_in = 2 * nw + 4

    def body(*refs):
        x_ref, g_ref, sc_ref, dxr_ref = refs[2 * nw:n_in]
        dx_ref, dsh_ref, dsc_ref, dg_ref = refs[n_in + nx:n_in + nx + 4]
        i = pl.program_id(0)
        if hosted:
            hosted.run(refs[n_in:n_in + nx], refs[n_in + nx + 4:n_in + 2 * nx + 4], refs[n_in + 2 * nx + 4:], i, T // TOKEN_BLOCK)
        dh = None
        for dz_ref, w_ref in zip(refs[:nw], refs[nw:2 * nw]):
            t = _dot_nt(dz_ref[...], w_ref[...])
            dh = t if dh is None else dh + t
        x = x_ref[...]
        g = g_ref[...]
        r = lax.rsqrt(jnp.mean(x * x, axis=-1, keepdims=True) + NORM_EPS)
        xn = x * r
        du = dh * (1.0 + sc_ref[0])
        dyg = du * g
        dx_ref[...] = dxr_ref[...] + r * (dyg - xn * jnp.mean(dyg * xn, axis=-1, keepdims=True))
        first = (i % nb) <= 1
        _acc(dsh_ref.at[0], jnp.sum(dh, axis=0, keepdims=True), first)
        _acc(dsc_ref.at[0], jnp.sum(dh * xn * g, axis=0, keepdims=True), first)
        _acc(dg_ref, jnp.sum(du * xn, axis=0, keepdims=True), i == 0)

    tok = lambda n: pl.BlockSpec((TOKEN_BLOCK, n), lambda i: (i, 0))
    acc = pl.BlockSpec((1, 1, D_MODEL), lambda i: (row(i), 0, 0))
    outs = pl.pallas_call(
        body, name=name, grid=(T // TOKEN_BLOCK,),
        in_specs=[tok(dz.shape[1]) for dz in dzs] + [_full(w.shape) for w in ws] + [tok(D_MODEL), _full((1, D_MODEL)), _mod_spec(nb, 1), tok(D_MODEL)]
        + (hosted.in_specs if hosted else []),
        out_specs=[tok(D_MODEL), acc, acc, _full((1, D_MODEL))] + (hosted.out_specs if hosted else []),
        out_shape=[jax.ShapeDtypeStruct((T, D_MODEL), F32), jax.ShapeDtypeStruct((nrow, 1, D_MODEL), F32),
                   jax.ShapeDtypeStruct((nrow, 1, D_MODEL), F32), jax.ShapeDtypeStruct((1, D_MODEL), F32)] + (hosted.out_shape if hosted else []),
        scratch_shapes=hosted.scratch if hosted else [],
        compiler_params=_params(("arbitrary",), VMEM_LIMIT),
    )(*dzs, *ws, x2, g, ms, dxres, *(hosted.xs if hosted else []))
    return outs[:4], list(outs[4:])


def _rot(x):
    lane = lax.broadcasted_iota(jnp.int32, x.shape, 1)
    return jnp.where((lane % 16) < 8, -pltpu.roll(x, LANES - 8, 1), pltpu.roll(x, 8, 1))


def _rope(x, cos, sin):
    return x * cos + _rot(x) * sin


def _rope_t(dy, cos, sin):
    return dy * cos - _rot(dy * sin)


def _rms_rows(x):
    r = lax.rsqrt(jnp.mean(x * x, axis=-1, keepdims=True) + NORM_EPS)
    return x * r, r


def _rms_rows_bwd(dyg, xn, r):
    return r * (dyg - xn * jnp.mean(dyg * xn, axis=-1, keepdims=True))


def mla_prep_fwd(za, qg, kvg, wqn, wqr, wkv, cos, sin, nb, name):
    T = za.shape[0]
    W = MLA_HEADS * LANES

    def body(z_ref, qg_ref, kvg_ref, wqn_ref, wqr_ref, wkv_ref, cos_ref, sin_ref, qn_ref, qr_ref, kv_ref, kr_ref):
        z = z_ref[...]
        cos = cos_ref[...]
        sin = sin_ref[...]
        xq, _ = _rms_rows(z[:, 0:256])
        qn = (xq * qg_ref[...]).astype(BF16)
        qn_ref[...] = (_dot(qn, wqn_ref[...]) * ATT_SCALE).astype(BF16)
        qr = _dot(qn, wqr_ref[...])
        for h in range(MLA_HEADS):
            sl = slice(LANES * h, LANES * (h + 1))
            qr_ref[:, sl] = (_rope(qr[:, sl], cos, sin) * ATT_SCALE).astype(BF16)
        xkv, _ = _rms_rows(z[:, 256:384])
        kv_ref[...] = _dot((xkv * kvg_ref[...]).astype(BF16), wkv_ref[...]).astype(BF16)
        kr_ref[...] = _rope(z[:, 384:512], cos, sin).astype(BF16)

    tb = _wide_block(nb * TOKEN_BLOCK)
    npos = nb * TOKEN_BLOCK // tb
    tok = lambda n: pl.BlockSpec((tb, n), lambda i: (i, 0))
    pos = pl.BlockSpec((tb, LANES), lambda i: (i % npos, 0))
    return pl.pallas_call(
        body, name=name, grid=(T // tb,),
        in_specs=[tok(512), _full(qg.shape), _full(kvg.shape), _full(wqn.shape), _full(wqr.shape), _full(wkv.shape), pos, pos],
        out_specs=[tok(W), tok(W), tok(W), tok(LANES)],
        out_shape=[jax.ShapeDtypeStruct((T, W), BF16)] * 3 + [jax.ShapeDtypeStruct((T, LANES), BF16)],
        compiler_params=_params(("parallel",)),
    )(za, qg, kvg, wqn, wqr, wkv, cos, sin)


def mla_prep_bwd(dqn, dqr, dkv, dkr, za, qg, kvg, wqn, wqr, wkv, cos, sin, nb, name):
    T = za.shape[0]
    W = MLA_HEADS * LANES

    def body(dqn_ref, dqr_ref, dkv_ref, dkr_ref, z_ref, qg_ref, kvg_ref, wqn_ref, wqr_ref, wkv_ref, cos_ref, sin_ref,
             dz_ref, dwqn_ref, dwqr_ref, dwkv_ref, dqg_ref, dkvg_ref):
        first = pl.program_id(0) == 0
        z = z_ref[...]
        cos = cos_ref[...]
        sin = sin_ref[...]
        qg = qg_ref[...]
        kvg = kvg_ref[...]
        xq, rq = _rms_rows(z[:, 0:256])
        qn = (xq * qg).astype(BF16)
        a1 = (dqn_ref[...].astype(F32) * ATT_SCALE).astype(BF16)
        parts = []
        for h in range(MLA_HEADS):
            sl = slice(LANES * h, LANES * (h + 1))
            parts.append(_rope_t(dqr_ref[:, sl].astype(F32) * ATT_SCALE, cos, sin).astype(BF16))
        a2 = jnp.concatenate(parts, axis=1)
        dq = _dot_nt(a1, wqn_ref[...]) + _dot_nt(a2, wqr_ref[...])
        _acc(dwqn_ref, _dot_tn(qn, a1), first)
        _acc(dwqr_ref, _dot_tn(qn, a2), first)
        _acc(dqg_ref, jnp.sum(dq * xq, axis=0, keepdims=True), first)
        dz_ref[:, 0:256] = _rms_rows_bwd(dq * qg, xq, rq).astype(BF16)
        xkv, rkv = _rms_rows(z[:, 256:384])
        kvn = (xkv * kvg).astype(BF16)
        dkvb = dkv_ref[...].astype(BF16)
        dk = _dot_nt(dkvb, wkv_ref[...])
        _acc(dwkv_ref, _dot_tn(kvn, dkvb), first)
        _acc(dkvg_ref, jnp.sum(dk * xkv, axis=0, keepdims=True), first)
        dz_ref[:, 256:384] = _rms_rows_bwd(dk * kvg, xkv, rkv).astype(BF16)
        dz_ref[:, 384:512] = _rope_t(dkr_ref[...], cos, sin).astype(BF16)

    tb = _wide_block(nb * TOKEN_BLOCK)
    npos = nb * TOKEN_BLOCK // tb
    tok = lambda n: pl.BlockSpec((tb, n), lambda i: (i, 0))
    pos = pl.BlockSpec((tb, LANES), lambda i: (i % npos, 0))
    return pl.pallas_call(
        body, name=name, grid=(T // tb,),
        in_specs=[tok(W), tok(W), tok(W), tok(LANES), tok(512), _full(qg.shape), _full(kvg.shape), _full(wqn.shape),
                  _full(wqr.shape), _full(wkv.shape), pos, pos],
        out_specs=[tok(512), _full(wqn.shape), _full(wqr.shape), _full(wkv.shape), _full(qg.shape), _full(kvg.shape)],
        out_shape=[jax.ShapeDtypeStruct((T, 512), BF16), jax.ShapeDtypeStruct(wqn.shape, F32), jax.ShapeDtypeStruct(wqr.shape, F32),
                   jax.ShapeDtypeStruct(wkv.shape, F32), jax.ShapeDtypeStruct(qg.shape, F32), jax.ShapeDtypeStruct(kvg.shape, F32)],
        compiler_params=_params(("arbitrary",)),
    )(dqn, dqr, dkv, dkr, za, qg, kvg, wqn, wqr, wkv, cos, sin)


def _att_qk(qn_ref, qr_ref, kv_ref, kr, j):
    sl = slice(LANES * j, LANES * (j + 1))
    q = jnp.concatenate([qn_ref[0, :, sl], qr_ref[0, :, sl]], axis=1)
    kvj = kv_ref[0, :, sl]
    k = jnp.concatenate([kvj, kr], axis=1)
    return q, k, kvj, _dot_nt(q, k)


def _att_specs(L, lk, q0, pairs=1):
    TQ, W2 = TOKEN_BLOCK, 2 * LANES * pairs
    qspec = pl.BlockSpec((1, TQ, W2), lambda b, h, i: (b, i + q0, h))
    kvspec = pl.BlockSpec((1, lk, W2), lambda b, h, i: (b, 0, h))
    krspec = pl.BlockSpec((1, lk, LANES), lambda b, h, i: (b, 0, 0))
    gspec = pl.BlockSpec((1, TQ, LANES * pairs), lambda b, h, i: (b, i + q0, h))
    lspec = pl.BlockSpec((1, pairs, TQ, LANES), lambda b, h, i: (b, h, i + q0, 0))
    return qspec, kvspec, krspec, gspec, lspec


_ANY = pl.BlockSpec(memory_space=pl.ANY)


def attention_fwd(qn, qr, kv, kr, zg, n_ctx, name, hosted=None):
    B, L, _ = qn.shape
    TQ = TOKEN_BLOCK
    PAIRS = 2
    HP = MLA_HEADS // 2 // PAIRS
    shapes = [jax.ShapeDtypeStruct((B, L, MLA_WIDTH), F32), jax.ShapeDtypeStruct((B, L, MLA_WIDTH), BF16),
              jax.ShapeDtypeStruct((B, MLA_HEADS // 2, L, LANES), F32)]

    nx = hosted.n if hosted else 0
    NQ = L // TQ - 1

    def body(*refs):
        if hosted:
            step = (pl.program_id(0) * HP + pl.program_id(1)) * NQ + pl.program_id(2)
            hosted.run(refs[5:5 + nx], refs[8 + nx:8 + 2 * nx], refs[8 + 2 * nx:], step, B * HP * NQ)
        _fwd_step(*refs[:5], *refs[5 + nx:8 + nx])

    def body_ctx(qn_ref, qr_ref, kv_ref, kr_ref, g_ref, *rest):
        _fwd_step(qn_ref, qr_ref, kv_ref, kr_ref, g_ref, *rest[-3:])

    def _fwd_step(qn_ref, qr_ref, kv_ref, kr_ref, g_ref, ya_ref, ym_ref, lse_ref):
        kr_v = kr_ref[0]
        for pr in range(PAIRS):
            outs, lses = [], []
            for j in (2 * pr, 2 * pr + 1):
                _, _, kvj, s = _att_qk(qn_ref, qr_ref, kv_ref, kr_v, j)
                m = jnp.max(s, axis=-1, keepdims=True)
                p = jnp.exp(s - m).astype(BF16)
                lane_k = lax.broadcasted_iota(jnp.int32, kvj.shape, 1)
                o = _dot(p, jnp.where(lane_k < MLA_V, jnp.ones_like(kvj), kvj))
                l = o[:, 0:1]
                outs.append(o / l)
                lses.append(m + jnp.log(l))
            lane = lax.broadcasted_iota(jnp.int32, outs[0].shape, 1)
            y = jnp.where(lane < MLA_V, pltpu.roll(outs[0], MLA_V, 1), outs[1])
            sl = slice(LANES * pr, LANES * (pr + 1))
            ya_ref[0, :, sl] = y
            g = g_ref[0, :, sl]
            ym_ref[0, :, sl] = (y * g * _sig(g)).astype(BF16)
            lse_ref[0, pr] = jnp.where(lane < MLA_V, lses[0], lses[1])

    qspec, kvspec, krspec, gspec, lspec = _att_specs(L, L, 1, PAIRS)
    main = pl.pallas_call(
        body, name=name, grid=(B, HP, NQ),
        in_specs=[qspec, qspec, kvspec, krspec, gspec] + (hosted.in_specs if hosted else []),
        out_specs=[gspec, gspec, lspec] + (hosted.out_specs if hosted else []),
        out_shape=shapes + (hosted.out_shape if hosted else []), scratch_shapes=hosted.scratch if hosted else [],
        compiler_params=_params(("arbitrary",) * 3 if hosted else ("parallel",) * 3, VMEM_LIMIT),
    )(qn, qr, kv, kr, zg, *(hosted.xs if hosted else []))
    qspec, kvspec, krspec, gspec, lspec = _att_specs(L, n_ctx, 0, PAIRS)
    outs = pl.pallas_call(
        body_ctx, name=name + "_ctx", grid=(B, HP, 1),
        in_specs=[qspec, qspec, kvspec, krspec, gspec, _ANY, _ANY, _ANY], out_specs=[gspec, gspec, lspec], out_shape=shapes,
        input_output_aliases={5: 0, 6: 1, 7: 2},
        compiler_params=_params(("parallel", "parallel", "parallel"), VMEM_LIMIT),
    )(qn, qr, kv, kr, zg, *main[:3])
    return (*outs, list(main[3:]))


def attention_bwd(qn, qr, kv, kr, zg, ya, lse, dym, n_ctx, name, hosted=None):
    B, L, _ = qn.shape
    TQ = TOKEN_BLOCK
    PAIRS = 2
    HP = MLA_HEADS // 2 // PAIRS
    W = MLA_HEADS * LANES
    shapes = [jax.ShapeDtypeStruct((B, L, W), BF16), jax.ShapeDtypeStruct((B, L, W), BF16), jax.ShapeDtypeStruct((B, L, W), F32),
              jax.ShapeDtypeStruct((B, L, LANES), F32), jax.ShapeDtypeStruct((B, L, MLA_WIDTH), BF16)]

    nx = hosted.n if hosted else 0
    NQ = L // TQ - 1

    def body(*refs):
        if hosted:
            step = (pl.program_id(0) * HP + pl.program_id(1)) * NQ + pl.program_id(2)
            hosted.run(refs[8:8 + nx], refs[13 + nx:13 + 2 * nx], refs[13 + 2 * nx:], step, B * HP * NQ)
        dkv_ref, dkr_ref = refs[10 + nx], refs[11 + nx]

        @pl.when(pl.program_id(2) == 0)
        def _():
            dkv_ref[...] = jnp.zeros_like(dkv_ref)

        @pl.when(jnp.logical_and(pl.program_id(2) == 0, pl.program_id(1) == 0))
        def _():
            dkr_ref[...] = jnp.zeros_like(dkr_ref)

        _bwd_step(*refs[:8], *refs[8 + nx:13 + nx])

    def body_ctx(qn_ref, qr_ref, kv_ref, kr_ref, g_ref, ya_ref, lse_ref, dy_ref, dkv_in, dkr_in, a0, a1, a2,
                 dqn_ref, dqr_ref, dkv_ref, dkr_ref, dzg_ref):
        dkv_ref[...] = dkv_in[...]

        @pl.when(pl.program_id(1) == 0)
        def _():
            dkr_ref[...] = dkr_in[...]

        _bwd_step(qn_ref, qr_ref, kv_ref, kr_ref, g_ref, ya_ref, lse_ref, dy_ref, dqn_ref, dqr_ref, dkv_ref, dkr_ref, dzg_ref)

    def _bwd_step(qn_ref, qr_ref, kv_ref, kr_ref, g_ref, ya_ref, lse_ref, dy_ref, dqn_ref, dqr_ref, dkv_ref, dkr_ref, dzg_ref):
        kr_v = kr_ref[0]
        for pr in range(PAIRS):
            psl = slice(LANES * pr, LANES * (pr + 1))
            silu, dsilu = _silu_and_grad(g_ref[0, :, psl])
            dy = dy_ref[0, :, psl]
            ya_v = ya_ref[0, :, psl]
            dya = dy * silu
            dzg_ref[0, :, psl] = (dy * ya_v * dsilu).astype(BF16)
            lane = lax.broadcasted_iota(jnp.int32, dya.shape, 1)
            hi = lane >= MLA_V
            d_out = [jnp.where(hi, pltpu.roll(dya, MLA_V, 1), 0.0), jnp.where(hi, dya, 0.0)]
            prod = dya * ya_v
            drow = [jnp.sum(jnp.where(hi, 0.0, prod), axis=-1, keepdims=True), jnp.sum(jnp.where(hi, prod, 0.0), axis=-1, keepdims=True)]
            lse_v = lse_ref[0, pr]
            for jj in range(2):
                j = 2 * pr + jj
                sl = slice(LANES * j, LANES * (j + 1))
                q, k, kvj, s = _att_qk(qn_ref, qr_ref, kv_ref, kr_v, j)
                pn = jnp.exp(s - lse_v[:, MLA_V * jj:MLA_V * jj + 1])
                dob = d_out[jj].astype(BF16)
                ds = (pn * (_dot_nt(dob, kvj) - drow[jj])).astype(BF16)
                dq = _dot(ds, k)
                dqn_ref[0, :, sl] = jnp.where(hi, 0.0, dq[:, :LANES]).astype(BF16)
                dqr_ref[0, :, sl] = dq[:, LANES:].astype(BF16)
                dk = _dot_tn(ds, q)
                dkv_ref[0, :, sl] += dk[:, :LANES] + _dot_tn(pn.astype(BF16), dob)
                dkr_ref[0] += dk[:, LANES:]

    sem = _params(("parallel", "arbitrary", "arbitrary"), VMEM_LIMIT)
    qspec, kvspec, krspec, gspec, lspec = _att_specs(L, L, 1, PAIRS)
    main = pl.pallas_call(
        body, name=name, grid=(B, HP, NQ),
        in_specs=[qspec, qspec, kvspec, krspec, gspec, gspec, lspec, gspec] + (hosted.in_specs if hosted else []),
        out_specs=[qspec, qspec, kvspec, krspec, gspec] + (hosted.out_specs if hosted else []),
        out_shape=shapes + (hosted.out_shape if hosted else []), scratch_shapes=hosted.scratch if hosted else [],
        compiler_params=_params(("arbitrary",) * 3, VMEM_LIMIT) if hosted else sem,
    )(qn, qr, kv, kr, zg, ya, lse, dym, *(hosted.xs if hosted else []))
    qspec, kvspec, krspec, gspec, lspec = _att_specs(L, n_ctx, 0, PAIRS)
    outs = pl.pallas_call(
        body_ctx, name=name + "_ctx", grid=(B, HP, 1),
        in_specs=[qspec, qspec, kvspec, krspec, gspec, gspec, lspec, gspec, kvspec, krspec, _ANY, _ANY, _ANY],
        out_specs=[qspec, qspec, kvspec, krspec, gspec], out_shape=shapes,
        input_output_aliases={8: 2, 9: 3, 10: 0, 11: 1, 12: 4}, compiler_params=sem,
    )(qn, qr, kv, kr, zg, ya, lse, dym, main[2], main[3], main[0], main[1], main[4])
    return (*outs, list(main[5:]))


def _seg_bounds(rows, n_ctx, L):
    in_ctx = rows < n_ctx
    return jnp.where(in_ctx, 0, n_ctx), jnp.where(in_ctx, n_ctx, L)


def _seg_shift(x, k, rows, lo, hi):
    src = rows + k
    return jnp.where(jnp.logical_and(src >= lo, src < hi), pltpu.roll(x, (-k) % x.shape[0], 0), 0.0)


def _run_sum(u, m, rows, lo, hi, step):
    acc, k = u, 1
    while k < m:
        acc = acc + _seg_shift(acc, step * k, rows, lo, hi)
        k *= 2
    return acc


def _window_sum(u, w, rows, lo, hi, mirror):
    right = _run_sum(u, w // 2, rows, lo, hi, 1)
    left = _run_sum(u, w // 2, rows, lo, hi, -1)
    if mirror:
        return left + _seg_shift(right, 1, rows, lo, hi)
    return right + _seg_shift(left, -1, rows, lo, hi)


def _window_count(w, rows, lo, hi):
    pos = rows - lo
    return (jnp.minimum(pos + w // 2, hi - lo) - jnp.maximum(pos - w // 2, 0)).astype(F32)


def pool_fwd(px, pg, pw, ps, n_ctx, name):
    B, L, _ = px.shape

    def body(px_ref, pg_ref, pw_ref, ps_ref, y_ref):
        rows = lax.broadcasted_iota(jnp.int32, (L, POOL_GROUP), 0)
        lo, hi = _seg_bounds(rows, n_ctx, L)
        for gi, w in enumerate(POOL_WINDOWS):
            sl = slice(POOL_GROUP * gi, POOL_GROUP * (gi + 1))
            u = px_ref[0, :, sl]
            pooled = _window_sum(u, w, rows, lo, hi, False) / _window_count(w, rows, lo, hi) - u
            mixed = _dot(pooled.astype(BF16), pw_ref[gi])
            g = pg_ref[0, :, sl]
            y_ref[0, :, sl] = (mixed * ps_ref[:, sl] * (g * _sig(g))).astype(BF16)

    tok = pl.BlockSpec((1, L, POOL_WIDTH), lambda b: (b, 0, 0))
    return pl.pallas_call(
        body, name=name, grid=(B,),
        in_specs=[tok, tok, _full(pw.shape), _full(ps.shape)],
        out_specs=tok, out_shape=jax.ShapeDtypeStruct((B, L, POOL_WIDTH), BF16),
        compiler_params=_params(("parallel",), VMEM_LIMIT),
    )(px, pg, pw, ps)


def pool_bwd(px, pg, pw, ps, dy, n_ctx, name):
    B, L, _ = px.shape

    def body(px_ref, pg_ref, pw_ref, ps_ref, dy_ref, dpx_ref, dpg_ref, dpw_ref, dps_ref):
        first = pl.program_id(0) == 0
        rows = lax.broadcasted_iota(jnp.int32, (L, POOL_GROUP), 0)
        lo, hi = _seg_bounds(rows, n_ctx, L)
        for gi, w in enumerate(POOL_WINDOWS):
            sl = slice(POOL_GROUP * gi, POOL_GROUP * (gi + 1))
            u = px_ref[0, :, sl]
            cnt = _window_count(w, rows, lo, hi)
            pooled = (_window_sum(u, w, rows, lo, hi, False) / cnt - u).astype(BF16)
            mixed = _dot(pooled, pw_ref[gi])
            silu, dsilu = _silu_and_grad(pg_ref[0, :, sl])
            sc = ps_ref[:, sl]
            dyv = dy_ref[0, :, sl]
            _acc(dps_ref.at[:, sl], jnp.sum(dyv * mixed * silu, axis=0, keepdims=True), first)
            dpg_ref[0, :, sl] = (dyv * mixed * sc * dsilu).astype(BF16)
            dmixed = (dyv * sc * silu).astype(BF16)
            _acc(dpw_ref.at[gi], _dot_tn(pooled, dmixed), first)
            dpooled = _dot_nt(dmixed, pw_ref[gi])
            dpx_ref[0, :, sl] = (_window_sum(dpooled / cnt, w, rows, lo, hi, True) - dpooled).astype(BF16)

    tok = pl.BlockSpec((1, L, POOL_WIDTH), lambda b: (b, 0, 0))
    return pl.pallas_call(
        body, name=name, grid=(B,),
        in_specs=[tok, tok, _full(pw.shape), _full(ps.shape), tok],
        out_specs=[tok, tok, _full(pw.shape), _full(ps.shape)],
        out_shape=[jax.ShapeDtypeStruct((B, L, POOL_WIDTH), BF16)] * 2 + [jax.ShapeDtypeStruct(pw.shape, F32), jax.ShapeDtypeStruct(ps.shape, F32)],
        compiler_params=_params(("arbitrary",), VMEM_LIMIT),
    )(px, pg, pw, ps, dy)


_SCAN_STEPS = (1, 2, 4, 8, 16, 32)
SCAN_CHUNKS = 4


def _cum_fwd(x, r):
    for s in _SCAN_STEPS:
        x = x + jnp.where(r >= s, pltpu.roll(x, s, 0), 0.0)
    return x


def _cum_bwd(x, r):
    n = x.shape[0]
    for s in _SCAN_STEPS:
        x = x + jnp.where(r + s < GLA_CHUNK, pltpu.roll(x, n - s, 0), 0.0)
    return x


def _log_sigmoid(x):
    return jnp.minimum(x, 0.0) - jnp.log(1.0 + jnp.exp(-jnp.abs(x)))


def _gla_decays(lr, w_ref, b_ref, r, reverse):
    pre = _dot(lr, w_ref[...]) + b_ref[...]
    a = _log_sigmoid(pre) / GLA_TAU
    return pre, a, (_cum_bwd(a, r) if reverse else _cum_fwd(a, r)), _chunk_total(a)


def _chunk_total(x):
    x3 = x.reshape(x.shape[0] // GLA_CHUNK, GLA_CHUNK, x.shape[1])
    return jnp.broadcast_to(jnp.sum(x3, axis=1, keepdims=True), x3.shape).reshape(x.shape)


def gla_prep_fwd(zlr, zq, zk, waf, wab, baf, bab, name):
    T = zlr.shape[0]
    tb = _wide_block(T)

    def body(lr_ref, q_ref, k_ref, waf_ref, wab_ref, baf_ref, bab_ref, qf_ref, kf_ref, ksf_ref, tf_ref, qb_ref, kb_ref, ksb_ref, tb_ref):
        r = lax.broadcasted_iota(jnp.int32, (tb, GLA_KW), 0) % GLA_CHUNK
        lr = lr_ref[...].astype(BF16)
        q = q_ref[...] * GLA_DK ** -0.5
        k = k_ref[...]
        for rev, w_ref, b_ref, qo, ko, kso, to in ((False, waf_ref, baf_ref, qf_ref, kf_ref, ksf_ref, tf_ref),
                                                   (True, wab_ref, bab_ref, qb_ref, kb_ref, ksb_ref, tb_ref)):
            _, _, b, tot = _gla_decays(lr, w_ref, b_ref, r, rev)
            qo[...] = (q * jnp.exp(b)).astype(BF16)
            ko[...] = (k * jnp.exp(-b)).astype(BF16)
            kso[...] = (k * jnp.exp(tot - b)).astype(BF16)
            to[...] = tot

    tok = lambda n: pl.BlockSpec((tb, n), lambda i: (i, 0))
    outs = [jax.ShapeDtypeStruct((T, GLA_KW), BF16)] * 3 + [jax.ShapeDtypeStruct((T, GLA_KW), F32)]
    return pl.pallas_call(
        body, name=name, grid=(T // tb,),
        in_specs=[tok(LANES), tok(GLA_KW), tok(GLA_KW), _full(waf.shape), _full(wab.shape), _full(baf.shape), _full(bab.shape)],
        out_specs=[tok(GLA_KW)] * 8, out_shape=outs + outs,
        compiler_params=_params(("parallel",)),
    )(zlr, zq, zk, waf, wab, baf, bab)


def gla_prep_bwd(zlr, zq, zk, waf, wab, baf, bab, gf, gb, dvs, name):
    T = zlr.shape[0]
    tb = _wide_block(T)

    def body(lr_ref, q_ref, k_ref, waf_ref, wab_ref, baf_ref, bab_ref, dqf, dkf, dksf, ddf, dqb, dkb, dksb, ddb, dvf, dvb,
             dlr_ref, dq_ref, dk_ref, dwaf_ref, dwab_ref, dbaf_ref, dbab_ref, dv_ref):
        first = pl.program_id(0) == 0
        dv_ref[...] = (dvf[...] + dvb[...]).astype(BF16)
        r = lax.broadcasted_iota(jnp.int32, (tb, GLA_KW), 0) % GLA_CHUNK
        lr = lr_ref[...].astype(BF16)
        q = q_ref[...] * GLA_DK ** -0.5
        k = k_ref[...]
        dq_tot = None
        dk_tot = None
        dlr = None
        for rev, w_ref, b_ref, dqt, dkt, dks, ddec, dw_ref, db_ref in (
                (False, waf_ref, baf_ref, dqf, dkf, dksf, ddf, dwaf_ref, dbaf_ref),
                (True, wab_ref, bab_ref, dqb, dkb, dksb, ddb, dwab_ref, dbab_ref)):
            pre, _, b, tot = _gla_decays(lr, w_ref, b_ref, r, rev)
            e1 = jnp.exp(b)
            e2 = jnp.exp(-b)
            e3 = jnp.exp(tot - b)
            dqt_v = dqt[...]
            dkt_v = dkt[...]
            dks_v = dks[...]
            dq = dqt_v * e1
            dk = dkt_v * e2 + dks_v * e3
            g3 = dks_v * (k * e3)
            d_b = dqt_v * (q * e1) - dkt_v * (k * e2) - g3
            d_tot = _chunk_total(g3) + ddec[...] * jnp.exp(tot)
            da = (_cum_fwd(d_b, r) if rev else _cum_bwd(d_b, r)) + d_tot
            dpre = (da * (_sig(-pre) / GLA_TAU)).astype(BF16)
            t = _dot_nt(dpre, w_ref[...])
            dlr = t if dlr is None else dlr + t
            _acc(dw_ref, _dot_tn(lr, dpre), first)
            _acc(db_ref, jnp.sum(dpre.astype(F32), axis=0, keepdims=True), first)
            dq_tot = dq if dq_tot is None else dq_tot + dq
            dk_tot = dk if dk_tot is None else dk_tot + dk
        dlr_ref[...] = dlr.astype(BF16)
        dq_ref[...] = (dq_tot * GLA_DK ** -0.5).astype(BF16)
        dk_ref[...] = dk_tot.astype(BF16)

    tok = lambda n: pl.BlockSpec((tb, n), lambda i: (i, 0))
    return pl.pallas_call(
        body, name=name, grid=(T // tb,),
        in_specs=[tok(LANES), tok(GLA_KW), tok(GLA_KW), _full(waf.shape), _full(wab.shape), _full(baf.shape), _full(bab.shape)] + [tok(GLA_KW)] * 8
        + [tok(GLA_WIDTH)] * 2,
        out_specs=[tok(LANES), tok(GLA_KW), tok(GLA_KW), _full(waf.shape), _full(wab.shape), _full(baf.shape), _full(bab.shape), tok(GLA_WIDTH)],
        out_shape=[jax.ShapeDtypeStruct((T, LANES), BF16), jax.ShapeDtypeStruct((T, GLA_KW), BF16), jax.ShapeDtypeStruct((T, GLA_KW), BF16),
                   jax.ShapeDtypeStruct(waf.shape, F32), jax.ShapeDtypeStruct(wab.shape, F32), jax.ShapeDtypeStruct(baf.shape, F32),
                   jax.ShapeDtypeStruct(bab.shape, F32), jax.ShapeDtypeStruct((T, GLA_WIDTH), BF16)],
        compiler_params=_params(("arbitrary",)),
    )(zlr, zq, zk, waf, wab, baf, bab, *gf, *gb, *dvs)


def _chunk_order(nc, n_ctx_chunks, reverse):
    if not reverse:
        return lambda c: c
    return lambda c: jnp.where(c < n_ctx_chunks, n_ctx_chunks - 1 - c, nc + n_ctx_chunks - 1 - c)


def _tri_mask4(reverse):
    ri = lax.broadcasted_iota(jnp.int32, (GLA_CHUNK, GLA_HEADS * GLA_CHUNK), 0)
    ci = lax.broadcasted_iota(jnp.int32, (GLA_CHUNK, GLA_HEADS * GLA_CHUNK), 1) % GLA_CHUNK
    return (ri <= ci) if reverse else (ri >= ci)


def _block_diag(x, rb, cb):
    x4 = jnp.concatenate([x] * GLA_HEADS, axis=0)
    r = lax.broadcasted_iota(jnp.int32, x4.shape, 0) // rb
    c = lax.broadcasted_iota(jnp.int32, x4.shape, 1) // cb
    return jnp.where(r == c, x4, jnp.zeros_like(x4))


def _diag_blocks(f, rb, cb):
    c = lax.broadcasted_iota(jnp.int32, (rb, GLA_HEADS * cb), 1) // cb
    out = None
    for h in range(GLA_HEADS):
        t = jnp.where(c == h, f[rb * h:rb * (h + 1)], 0.0)
        out = t if out is None else out + t
    return out


def gla_scan_fwd(dirs, v, n_ctx, name):
    B, L, _ = v.shape
    C, G = GLA_CHUNK, SCAN_CHUNKS
    nc = L // C
    orders = [_chunk_order(nc // G, n_ctx // C // G, rev) for rev in (False, True)]

    def body(qf, kf, ksf, tf, vf, qb, kb, ksb, tb, vb, of, ssf, ob, ssb, stf, stb):
        @pl.when(pl.program_id(1) == 0)
        def _():
            stf[...] = jnp.zeros_like(stf)
            stb[...] = jnp.zeros_like(stb)

        for sub in range(G):
            step(qf, kf, ksf, vf, tf, of, ssf, stf, False, sub)
            step(qb, kb, ksb, vb, tb, ob, ssb, stb, True, G - 1 - sub)

    def step(q_ref, k_ref, ks_ref, v_ref, tot_ref, o_ref, ss_ref, st, reverse, sub):
        rows = slice(C * sub, C * (sub + 1))
        S = st[...]
        ss_ref[0, sub] = S
        q = q_ref[0, rows]
        v = v_ref[0, rows]
        k4 = _block_diag(k_ref[0, rows], GLA_CHUNK, GLA_DK)
        v4 = _block_diag(v.astype(BF16), GLA_CHUNK, GLA_DV)
        s4 = _block_diag(S.astype(BF16), GLA_DV, GLA_DK)
        P = jnp.where(_tri_mask4(reverse), _dot_nt(q, k4), 0.0)
        o_ref[0, rows] = _dot(P.astype(BF16), v4) + _dot_nt(q, s4)
        st[...] = jnp.exp(tot_ref[0, C * sub:C * sub + 1, :]) * S + _diag_blocks(_dot(v.T.astype(BF16), ks_ref[0, rows]), GLA_DV, GLA_DK)

    in_specs, out_specs, out_shape = [], [], []
    for order in orders:
        tok = lambda n, order=order: pl.BlockSpec((1, G * C, n), lambda b, c: (b, order(c), 0))
        in_specs += [tok(GLA_KW), tok(GLA_KW), tok(GLA_KW), tok(GLA_KW), tok(GLA_WIDTH)]
        out_specs += [tok(GLA_WIDTH), pl.BlockSpec((1, G, GLA_DV, GLA_KW), lambda b, c, order=order: (b, order(c), 0, 0))]
        out_shape += [jax.ShapeDtypeStruct((B, L, GLA_WIDTH), F32), jax.ShapeDtypeStruct((B, nc, GLA_DV, GLA_KW), F32)]
    outs = pl.pallas_call(
        body, name=name, grid=(B, nc // G), in_specs=in_specs, out_specs=out_specs, out_shape=out_shape,
        scratch_shapes=[pltpu.VMEM((GLA_DV, GLA_KW), F32)] * 2,
        compiler_params=_params(("parallel", "arbitrary")),
    )(*dirs[0], v, *dirs[1], v)
    return outs[:2], outs[2:]


def gla_scan_bwd(dirs, v, do, n_ctx, name, hosted=None):
    B, L, _ = v.shape
    C, G = GLA_CHUNK, SCAN_CHUNKS
    nc = L // C
    npair = nc // G
    orders = []
    for rev in (False, True):
        fwd_order = _chunk_order(npair, n_ctx // C // G, rev)
        orders.append(lambda c, fwd_order=fwd_order: fwd_order(npair - 1 - c))

    nx = hosted.n if hosted else 0

    def body(*refs):
        qf, kf, ksf, tf, ssf, vf, dof, qb, kb, ksb, tb, ssb, vb, dob = refs[:14]
        dqf, dkf, dksf, dvf, ddf, dqb, dkb, dksb, dvb, ddb = refs[14 + nx:24 + nx]
        dstf, dstb = refs[24 + 2 * nx:26 + 2 * nx]
        if hosted:
            hosted.run(refs[14:14 + nx], refs[24 + nx:24 + 2 * nx], refs[26 + 2 * nx:], pl.program_id(0) * npair + pl.program_id(1), B * npair)

        @pl.when(pl.program_id(1) == 0)
        def _():
            dstf[...] = jnp.zeros_like(dstf)
            dstb[...] = jnp.zeros_like(dstb)

        for sub in range(G):
            step(qf, kf, ksf, vf, tf, ssf, dof, dqf, dkf, dksf, dvf, ddf, dstf, False, G - 1 - sub)
            step(qb, kb, ksb, vb, tb, ssb, dob, dqb, dkb, dksb, dvb, ddb, dstb, True, sub)

    def step(q_ref, k_ref, ks_ref, v_ref, tot_ref, ss_ref, do_ref, dq_ref, dk_ref, dks_ref, dv_ref, dd_ref, dst, reverse, sub):
        rows = slice(C * sub, C * (sub + 1))
        dSn = dst[...]
        S = ss_ref[0, sub]
        q = q_ref[0, rows]
        vb = v_ref[0, rows].astype(BF16)
        dob = do_ref[0, rows].astype(BF16)
        k4 = _block_diag(k_ref[0, rows], GLA_CHUNK, GLA_DK)
        v4 = _block_diag(vb, GLA_CHUNK, GLA_DV)
        s4 = _block_diag(S.astype(BF16), GLA_DV, GLA_DK)
        ds4 = _block_diag(dSn.astype(BF16), GLA_DV, GLA_DK)
        tri = _tri_mask4(reverse)
        P = jnp.where(tri, _dot_nt(q, k4), 0.0).astype(BF16)
        dP = jnp.where(tri, _dot_nt(dob, v4), 0.0).astype(BF16)
        dq_ref[0, rows] = _dot(dob, s4) + _dot(dP, k4)
        dk_ref[0, rows] = _diag_blocks(_dot_tn(dP, q), GLA_CHUNK, GLA_DK)
        dv_ref[0, rows] = _diag_blocks(_dot_tn(P, dob), GLA_CHUNK, GLA_DV) + _dot_nt(ks_ref[0, rows], ds4)
        dks_ref[0, rows] = _dot(vb, ds4)
        dd_ref[0, rows] = jnp.broadcast_to(jnp.sum(dSn * S, axis=0, keepdims=True), (C, GLA_KW))
        dst[...] = jnp.exp(tot_ref[0, C * sub:C * sub + 1, :]) * dSn + _diag_blocks(_dot_tn(dob, q), GLA_DV, GLA_DK)

    in_specs, out_specs, out_shape = [], [], []
    for order in orders:
        tok = lambda n, order=order: pl.BlockSpec((1, G * C, n), lambda b, c: (b, order(c), 0))
        in_specs += [tok(GLA_KW), tok(GLA_KW), tok(GLA_KW), tok(GLA_KW),
                     pl.BlockSpec((1, G, GLA_DV, GLA_KW), lambda b, c, order=order: (b, order(c), 0, 0)), tok(GLA_WIDTH), tok(GLA_WIDTH)]
        out_specs += [tok(GLA_KW), tok(GLA_KW), tok(GLA_KW), tok(GLA_WIDTH), tok(GLA_KW)]
        out_shape += [jax.ShapeDtypeStruct((B, L, GLA_KW), F32)] * 3 + [jax.ShapeDtypeStruct((B, L, GLA_WIDTH), F32), jax.ShapeDtypeStruct((B, L, GLA_KW), F32)]
    outs = pl.pallas_call(
        body, name=name, grid=(B, npair), in_specs=in_specs + (hosted.in_specs if hosted else []),
        out_specs=out_specs + (hosted.out_specs if hosted else []), out_shape=out_shape + (hosted.out_shape if hosted else []),
        scratch_shapes=[pltpu.VMEM((GLA_DV, GLA_KW), F32)] * 2 + (hosted.scratch if hosted else []),
        compiler_params=_params(("arbitrary", "arbitrary") if hosted else ("parallel", "arbitrary")),
    )(*dirs[0], v, do, *dirs[1], v, do, *(hosted.xs if hosted else []))
    return outs[:5], outs[5:10], list(outs[10:])


def gla_out_fwd(of, ob, gn, zg, name):
    T = of.shape[0]

    def body(of_ref, ob_ref, gn_ref, g_ref, y_ref):
        for h in range(GLA_HEADS):
            sl = slice(GLA_DV * h, GLA_DV * (h + 1))
            xn, _ = _rms_rows(of_ref[:, sl] + ob_ref[:, sl])
            g = g_ref[:, sl]
            y_ref[:, sl] = (xn * gn_ref[...] * (g * _sig(g))).astype(BF16)

    tb = _wide_block(T)
    tok = pl.BlockSpec((tb, GLA_WIDTH), lambda i: (i, 0))
    return pl.pallas_call(
        body, name=name, grid=(T // tb,),
        in_specs=[tok, tok, _full(gn.shape), tok], out_specs=tok,
        out_shape=jax.ShapeDtypeStruct((T, GLA_WIDTH), BF16),
        compiler_params=_params(("parallel",)),
    )(of, ob, gn, zg)


def gla_out_bwd(of, ob, gn, zg, dy, name):
    T = of.shape[0]

    def body(of_ref, ob_ref, gn_ref, g_ref, dy_ref, do_ref, dzg_ref, dgn_ref):
        first = pl.program_id(0) == 0
        gn_v = gn_ref[...]
        dgn = None
        for h in range(GLA_HEADS):
            sl = slice(GLA_DV * h, GLA_DV * (h + 1))
            xn, r = _rms_rows(of_ref[:, sl] + ob_ref[:, sl])
            silu, dsilu = _silu_and_grad(g_ref[:, sl])
            dyv = dy_ref[:, sl]
            dzg_ref[:, sl] = (dyv * xn * gn_v * dsilu).astype(BF16)
            dn = dyv * silu
            t = jnp.sum(dn * xn, axis=0, keepdims=True)
            dgn = t if dgn is None else dgn + t
            do_ref[:, sl] = _rms_rows_bwd(dn * gn_v, xn, r)
        _acc(dgn_ref, dgn, first)

    tb = _wide_block(T)
    tok = pl.BlockSpec((tb, GLA_WIDTH), lambda i: (i, 0))
    return pl.pallas_call(
        body, name=name, grid=(T // tb,),
        in_specs=[tok, tok, _full(gn.shape), tok, tok], out_specs=[tok, tok, _full(gn.shape)],
        out_shape=[jax.ShapeDtypeStruct((T, GLA_WIDTH), F32), jax.ShapeDtypeStruct((T, GLA_WIDTH), BF16), jax.ShapeDtypeStruct(gn.shape, F32)],
        compiler_params=_params(("arbitrary",)),
    )(of, ob, gn, zg, dy)


def merge_post_fwd(ys, zm, wbs, wo, x2, pg, ms, nb, name):
    T = x2.shape[0]

    def body(y0, y1, y2, zm_ref, w0, w1, w2, wo_ref, x_ref, pg_ref, gate_ref, xn_ref, out_ref, mg_ref):
        merged = None
        for i, (y_ref, w_ref) in enumerate(((y0, w0), (y1, w1), (y2, w2))):
            t = _sig(zm_ref[:, D_MODEL * i:D_MODEL * (i + 1)].astype(F32)) * _dot(y_ref[...], w_ref[...])
            merged = t if merged is None else merged + t
        mb = merged.astype(BF16)
        mg_ref[...] = mb
        out = _dot(mb, wo_ref[...])
        out_ref[...] = out
        on, _ = _rms_rows(out)
        xn_ref[...] = x_ref[...] + gate_ref[0] * (on * pg_ref[...])

    tok = lambda n: pl.BlockSpec((TOKEN_BLOCK, n), lambda i: (i, 0))
    return pl.pallas_call(
        body, name=name, grid=(T // TOKEN_BLOCK,),
        in_specs=[tok(512)] * 3 + [tok(3 * D_MODEL)] + [_full(w.shape) for w in wbs] + [_full(wo.shape), tok(D_MODEL), _full(pg.shape), _mod_spec(nb, 2)],
        out_specs=[tok(D_MODEL)] * 3,
        out_shape=[jax.ShapeDtypeStruct((T, D_MODEL), F32), jax.ShapeDtypeStruct((T, D_MODEL), F32), jax.ShapeDtypeStruct((T, D_MODEL), BF16)],
        compiler_params=_params(("parallel",), VMEM_LIMIT),
    )(*ys, zm, *wbs, wo, x2, pg, ms)


def merge_post_bwd(dxn, out, ys, zm, wbs, wo, pg, ms, nb, name):
    T = dxn.shape[0]
    nrow = ms.shape[0]
    row = _mod_row(nb)

    def body(dxn_ref, out_ref, y0, y1, y2, zm_ref, w0, w1, w2, wo_ref, pg_ref, gate_ref,
             dy0, dy1, dy2, dzm_ref, dout_ref, dp0, dp1, dp2, dgate_ref, dpg_ref):
        i = pl.program_id(0)
        dxn_v = dxn_ref[...]
        on, r = _rms_rows(out_ref[...])
        pg_v = pg_ref[...]
        _acc(dgate_ref.at[0], jnp.sum(dxn_v * on * pg_v, axis=0, keepdims=True), (i % nb) <= 1)
        dn = dxn_v * gate_ref[0]
        _acc(dpg_ref, jnp.sum(dn * on, axis=0, keepdims=True), i == 0)
        dout = _rms_rows_bwd(dn * pg_v, on, r).astype(BF16)
        dout_ref[...] = dout
        dmerged = _dot_nt(dout, wo_ref[...])
        for j, (y_ref, w_ref, dy_ref, dp_ref) in enumerate(((y0, w0, dy0, dp0), (y1, w1, dy1, dp1), (y2, w2, dy2, dp2))):
            sl = slice(D_MODEL * j, D_MODEL * (j + 1))
            g = _sig(zm_ref[:, sl].astype(F32))
            p = _dot(y_ref[...], w_ref[...])
            dzm_ref[:, sl] = (dmerged * p * g * (1.0 - g)).astype(BF16)
            dp = (dmerged * g).astype(BF16)
            dp_ref[...] = dp
            dy_ref[...] = _dot_nt(dp, w_ref[...])

    tok = lambda n: pl.BlockSpec((TOKEN_BLOCK, n), lambda i: (i, 0))
    return pl.pallas_call(
        body, name=name, grid=(T // TOKEN_BLOCK,),
        in_specs=[tok(D_MODEL), tok(D_MODEL)] + [tok(512)] * 3 + [tok(3 * D_MODEL)] + [_full(w.shape) for w in wbs] + [_full(wo.shape), _full(pg.shape), _mod_spec(nb, 2)],
        out_specs=[tok(512)] * 3 + [tok(3 * D_MODEL), tok(D_MODEL)] + [tok(D_MODEL)] * 3 + [pl.BlockSpec((1, 1, D_MODEL), lambda i: (row(i), 0, 0)), _full(pg.shape)],
        out_shape=[jax.ShapeDtypeStruct((T, 512), F32)] * 3 + [jax.ShapeDtypeStruct((T, 3 * D_MODEL), BF16), jax.ShapeDtypeStruct((T, D_MODEL), BF16)]
        + [jax.ShapeDtypeStruct((T, D_MODEL), BF16)] * 3 + [jax.ShapeDtypeStruct((nrow, 1, D_MODEL), F32), jax.ShapeDtypeStruct(pg.shape, F32)],
        compiler_params=_params(("arbitrary",), VMEM_LIMIT),
    )(dxn, out, *ys, zm, *wbs, wo, pg, ms)


def loss_head(y2, tgt2, nb, name):
    T = y2.shape[0]
    nlat = nb - 1

    def body(y_ref, t_ref, dy_ref, loss_ref, acc):
        i = pl.program_id(0)
        is_lat = (i % nb) > 0

        @pl.when(i == 0)
        def _():
            acc[...] = jnp.zeros_like(acc)

        @pl.when(is_lat)
        def _():
            e = y_ref[...] - t_ref[...]
            dy_ref[...] = e * (1.0 / D_MODEL)
            acc[...] += jnp.sum(e * e, axis=0, keepdims=True)

        @pl.when(jnp.logical_not(is_lat))
        def _():
            dy_ref[...] = jnp.zeros_like(dy_ref)

        @pl.when(i == pl.num_programs(0) - 1)
        def _():
            loss_ref[...] = jnp.sum(acc[...], axis=1, keepdims=True) * (0.5 / D_MODEL)

    tok = pl.BlockSpec((TOKEN_BLOCK, D_MODEL), lambda i: (i, 0))
    tgt = pl.BlockSpec((TOKEN_BLOCK, D_MODEL), lambda i: ((i // nb) * nlat + jnp.maximum(i % nb - 1, 0), 0))
    return pl.pallas_call(
        body, name=name, grid=(T // TOKEN_BLOCK,),
        in_specs=[tok, tgt], out_specs=[tok, _full((1, 1))],
        out_shape=[jax.ShapeDtypeStruct((T, D_MODEL), F32), jax.ShapeDtypeStruct((1, 1), F32)],
        scratch_shapes=[pltpu.VMEM((1, D_MODEL), F32)],
        compiler_params=_params(("arbitrary",)),
    )(y2, tgt2)


_IN_OFFS = tuple(int(o) for o in np.cumsum((0,) + IN_SIZES))
_IN_GROUPS = (("a", 0, 416, 512), ("mg", 416, 512, 512), ("px", 928, 512, 512), ("pg", 1440, 512, 512), ("gq", 1952, 256, 256),
              ("gk", 2208, 256, 256), ("gv", 2464, 512, 512), ("lr", 2976, 32, 128), ("gg", 3008, 512, 512), ("m", 3520, 3072, 3072))


def _pad_cols(w, n):
    return w if w.shape[1] == n else jnp.pad(w, ((0, 0), (0, n - w.shape[1])))


def layer_weights(w_in, w_uq, w_ukv, af_w2, ab_w2, wbm, wbp, wbg, w_out):
    W = {}
    for nm, off, n, npad in _IN_GROUPS:
        W["in_" + nm] = _pad_cols(w_in[:, off:off + n], npad)
    uq = w_uq.reshape(MLA_Q_RANK, MLA_HEADS, MLA_NOPE + MLA_ROPE)
    W["qn"] = jnp.pad(uq[:, :, :MLA_NOPE], ((0, 0), (0, 0), (0, LANES - MLA_NOPE))).reshape(MLA_Q_RANK, MLA_HEADS * LANES)
    W["qr"] = jnp.pad(uq[:, :, MLA_NOPE:], ((0, 0), (0, 0), (0, LANES - MLA_ROPE))).reshape(MLA_Q_RANK, MLA_HEADS * LANES)
    W["kv"] = w_ukv
    W["af"] = jnp.pad(af_w2, ((0, LANES - GLA_GATE_RANK), (0, 0)))
    W["ab"] = jnp.pad(ab_w2, ((GLA_GATE_RANK, LANES - 2 * GLA_GATE_RANK), (0, 0)))
    W["bm"], W["bp"], W["bg"], W["out"] = wbm, wbp, wbg, w_out
    return W


def rope_tables(L, n_ctx):
    t = np.arange(L - n_ctx)
    half = MLA_ROPE // 2
    inv = ROPE_BASE ** (-np.arange(0, half, 2, dtype=np.float32) / half)
    ang_r = (t // GRID_W).astype(np.float32)[:, None] * inv
    ang_c = (t % GRID_W).astype(np.float32)[:, None] * inv
    ang = jnp.asarray(np.concatenate([ang_r, ang_r, ang_c, ang_c], axis=-1), F32)
    cos = jnp.ones((L, LANES), F32).at[n_ctx:, :MLA_ROPE].set(jnp.cos(ang))
    sin = jnp.zeros((L, LANES), F32).at[n_ctx:, :MLA_ROPE].set(jnp.sin(ang))
    return cos, sin


def layer_fwd(x2, ms, W, P, cos, sin, B, L, n_ctx, tag, hosted=None):
    nb = L // TOKEN_BLOCK
    r3 = lambda a: a.reshape(B, L, a.shape[-1])
    r2 = lambda a: a.reshape(B * L, a.shape[-1])
    names = [g[0] for g in _IN_GROUPS[:-1]]
    h, zs = norm_in_proj(x2, P["pre"], ms, [W["in_" + n] for n in names], nb, tag + "in_proj")
    z = dict(zip(names, zs))
    (z["m"],) = mm_multi(h, [W["in_m"]], [BF16], tag + "in_proj_merge", tm=_wide_block(L))
    qn, qr, kv, kr = mla_prep_fwd(z["a"], P["qg"], P["kvg"], W["qn"], W["qr"], W["kv"], cos, sin, nb, tag + "mla_prep")
    ya, y_mla, lse, carried = attention_fwd(r3(qn), r3(qr), r3(kv), r3(kr), r3(z["mg"]), n_ctx, tag + "attention", hosted)
    y_pool = pool_fwd(r3(z["px"]), r3(z["pg"]), P["pw"], P["ps"], n_ctx, tag + "pool")
    qf, kf, ksf, tf, qb, kb, ksb, tb = gla_prep_fwd(z["lr"], z["gq"], z["gk"], W["af"], W["ab"], P["baf"], P["bab"], tag + "gla_prep")
    (of, ssf), (ob, ssb) = gla_scan_fwd([(r3(qf), r3(kf), r3(ksf), r3(tf)), (r3(qb), r3(kb), r3(ksb), r3(tb))], r3(z["gv"]), n_ctx, tag + "gla_scan")
    y_gla = gla_out_fwd(r2(of), r2(ob), P["gn"], z["gg"], tag + "gla_out")
    ys = [r2(y_mla), r2(y_pool), y_gla]
    x_new, out, merged = merge_post_fwd(ys, z["m"], [W["bm"], W["bp"], W["bg"]], W["out"], x2, P["post"], ms, nb, tag + "merge_post")
    res = dict(x2=x2, h=h, z=z, qn=qn, qr=qr, kv=kv, kr=kr, ya=ya, lse=lse, ys=ys, gla_f=(qf, kf, ksf, tf, ssf), gla_b=(qb, kb, ksb, tb, ssb),
               of=of, ob=ob, out=out, merged=merged)
    return x_new, res, carried


def layer_bwd(dxn, res, ms, W, P, cos, sin, B, L, n_ctx, tag, hosted=None, host_own=None):
    nb = L // TOKEN_BLOCK
    r3 = lambda a: a.reshape(B, L, a.shape[-1])
    r2 = lambda a: a.reshape(B * L, a.shape[-1])
    z = res["z"]
    ys = res["ys"]
    wbs = [W["bm"], W["bp"], W["bg"]]
    dy0, dy1, dy2, dzm, dout, dp0, dp1, dp2, dgate, dpost = merge_post_bwd(dxn, res["out"], ys, z["m"], wbs, W["out"], P["post"], ms, nb, tag + "merge_post_bwd")
    G = {"out": mm_dw(res["merged"], dout, tag + "dw_out"), "post": dpost}
    for nm, y, dp in zip(("bm", "bp", "bg"), ys, (dp0, dp1, dp2)):
        G[nm] = mm_dw(y, dp, tag + "dw_" + nm)
    g = {n: _natural_grad(G, n) for n in GRADS_EARLY}
    own = {}
    dz = {"m": dzm}
    do, dz["gg"], G["gn"] = gla_out_bwd(r2(res["of"]), r2(res["ob"]), P["gn"], z["gg"], dy2, tag + "gla_out_bwd")
    carrier = host_own(GRADS_EARLY, g) if host_own else None
    *grads, arrived = gla_scan_bwd([(r3(qt), r3(kt), r3(ks), r3(tot), ss) for qt, kt, ks, tot, ss in (res["gla_f"], res["gla_b"])],
                                   r3(z["gv"]), r3(do), n_ctx, tag + "gla_scan_bwd", carrier)
    own.update(zip(GRADS_EARLY, arrived))
    gf = [r2(a) for a in grads[0]]
    gb = [r2(a) for a in grads[1]]
    dz["lr"], dz["gq"], dz["gk"], G["af"], G["ab"], G["baf"], G["bab"], dz["gv"] = gla_prep_bwd(
        z["lr"], z["gq"], z["gk"], W["af"], W["ab"], P["baf"], P["bab"], gf[:3] + gf[4:], gb[:3] + gb[4:], [gf[3], gb[3]], tag + "gla_prep_bwd")
    dpx, dpg, G["pw"], G["ps"] = pool_bwd(r3(z["px"]), r3(z["pg"]), P["pw"], P["ps"], r3(dy1), n_ctx, tag + "pool_bwd")
    dz["px"], dz["pg"] = r2(dpx), r2(dpg)
    dqn, dqr, dkv, dkr, dzmg, got = attention_bwd(r3(res["qn"]), r3(res["qr"]), r3(res["kv"]), r3(res["kr"]), r3(z["mg"]), res["ya"], res["lse"],
                                                  r3(dy0), n_ctx, tag + "attention_bwd", hosted)
    dz["mg"] = r2(dzmg)
    dz["a"], G["qn"], G["qr"], G["kv"], G["qg"], G["kvg"] = mla_prep_bwd(
        r2(dqn), r2(dqr), r2(dkv), r2(dkr), z["a"], P["qg"], P["kvg"], W["qn"], W["qr"], W["kv"], cos, sin, nb, tag + "mla_prep_bwd")
    names = [grp[0] for grp in _IN_GROUPS]
    for n in names:
        G["in_" + n] = mm_dw(res["h"], dz[n], tag + "dw_in_" + n)
    g.update({n: _natural_grad(G, n) for n in GRADS_LATE})
    carrier = host_own(GRADS_LATE, g) if host_own else None
    (dx, dshift, dscale, G["pre"]), arrived = in_proj_norm_bwd([dz[n] for n in names], [W["in_" + n] for n in names], res["x2"], P["pre"], ms, dxn, nb,
                                                               tag + "in_proj_dx", carrier)
    own.update(zip(GRADS_LATE, arrived))
    g.update({n: _natural_grad(G, n) for n in GRADS_REPLICATED})
    dms = jnp.concatenate([dshift, dscale, dgate], axis=-1)
    return dx, g, dms, got, own


GRADS_EARLY = ("w_branch_mla", "w_branch_pool", "w_branch_gla", "w_out")
GRADS_LATE = ("w_in", "mla_w_uq", "mla_w_ukv", "gla_af_w2", "gla_ab_w2")
GRADS_REPLICATED = ("pre_norm", "post_norm", "mla_q_norm", "mla_kv_norm", "pool_w", "pool_scale", "gla_af_b", "gla_ab_b", "gla_norm")
_DIRECT = dict(mla_w_ukv="kv", w_branch_mla="bm", w_branch_pool="bp", w_branch_gla="bg", w_out="out", pool_w="pw")
_ROW = dict(pre_norm="pre", post_norm="post", mla_q_norm="qg", mla_kv_norm="kvg", pool_scale="ps", gla_af_b="baf", gla_ab_b="bab", gla_norm="gn")


def _natural_grad(G, name):
    if name == "w_in":
        parts = {off: G["in_" + nm][:, :n] for nm, off, n, npad in _IN_GROUPS}
        return jnp.concatenate([parts[o] for o in sorted(parts)], axis=1)
    if name == "mla_w_uq":
        gqn = G["qn"].reshape(MLA_Q_RANK, MLA_HEADS, LANES)[:, :, :MLA_NOPE]
        gqr = G["qr"].reshape(MLA_Q_RANK, MLA_HEADS, LANES)[:, :, :MLA_ROPE]
        return jnp.concatenate([gqn, gqr], axis=-1).reshape(MLA_Q_RANK, MLA_HEADS * (MLA_NOPE + MLA_ROPE))
    if name == "gla_af_w2":
        return G["af"][:GLA_GATE_RANK]
    if name == "gla_ab_w2":
        return G["ab"][GLA_GATE_RANK:2 * GLA_GATE_RANK]
    return G[_DIRECT[name]] if name in _DIRECT else G[_ROW[name]][0]


def local_step(x, c, ctx, c_ctx, small, loss_target, depth, layer_full, host_fwd=None, host_bwd=None, host_own=None, join=None):
    B, S, _ = x.shape
    n_ctx = ctx.shape[1]
    L = n_ctx + S
    nb = L // TOKEN_BLOCK
    cos, sin = rope_tables(L, n_ctx)
    x2 = join(ctx, x) if join else jnp.concatenate([ctx, x], axis=1).reshape(B * L, D_MODEL)
    a8 = jnp.zeros((8, D_MODEL), F32).at[:B].set(c).at[B].set(c_ctx)
    Ws, Ps, mss, ress, mod_ws = [], [], [], [], []
    carried = None
    for l in range(depth):
        tag = f"l{l}_"
        full = layer_full(l, carried)
        W = layer_weights(full["w_in"], full["mla_w_uq"], full["mla_w_ukv"], full["gla_af_w2"], full["gla_ab_w2"],
                          full["w_branch_mla"], full["w_branch_pool"], full["w_branch_gla"], full["w_out"])
        P = dict(pre=small["pre_norm"][l][None], post=small["post_norm"][l][None], qg=small["mla_q_norm"][l][None], kvg=small["mla_kv_norm"][l][None],
                 pw=small["pool_w"][l].astype(BF16), ps=small["pool_scale"][l][None], baf=small["gla_af_b"][l][None], bab=small["gla_ab_b"][l][None],
                 gn=small["gla_norm"][l][None])
        mod8 = mod_fwd(a8, full["mod_w"], small["mod_b"][l][None], tag + "mod")
        ms = jnp.stack([jnp.broadcast_to(mod8[B], (B, 3 * D_MODEL)), mod8[:B]], axis=1).reshape(2 * B, 1, 3 * D_MODEL)
        x2, res, carried = layer_fwd(x2, ms, W, P, cos, sin, B, L, n_ctx, tag, host_fwd(l) if host_fwd else None)
        Ws.append(W), Ps.append(P), mss.append(ms), ress.append(res), mod_ws.append(full["mod_w"])
    dx, loss = loss_head(x2, loss_target.reshape(B * S, D_MODEL), nb, "loss_head")
    grads = [None] * depth
    delivered = [None] * depth
    dz8s = [None] * depth
    da8 = None
    for l in reversed(range(depth)):
        tag = f"l{l}_"
        hosted = host_bwd(l, grads[l + 1]) if host_bwd and l + 1 < depth else None
        dx, g, dms, got, own = layer_bwd(dx, ress[l], mss[l], Ws[l], Ps[l], cos, sin, B, L, n_ctx, tag, hosted, host_own(l) if host_own else None)
        if hosted:
            delivered[l + 1] = got
        if own:
            delivered[l] = own
        dms = dms.reshape(B, 2, 3 * D_MODEL)
        dz8s[l] = jnp.zeros((8, 3 * D_MODEL), F32).at[:B].set(dms[:, 1]).at[B].set(jnp.sum(dms[:, 0], axis=0))
        g_mod_b, da = mod_bwd(a8, mod_ws[l], dz8s[l], tag + "mod_bwd")
        da8 = da if da8 is None else da8 + da
        g["mod_b"] = g_mod_b[0]
        grads[l] = g
    grad_x = dx.reshape(B, L, D_MODEL)[:, n_ctx:]
    return loss, grad_x, grads, da8[B], delivered, (a8, dz8s)


_MESH_ID = pl.DeviceIdType.MESH
_HBM = pl.BlockSpec(memory_space=pltpu.HBM)


def _me_and_peers():
    mx, my, mc = lax.axis_index("x"), lax.axis_index("y"), lax.axis_index("c")
    peers = []
    for k in range(1, N_DEV):
        px, py, pc = mx ^ ((k >> 2) & 1), my ^ ((k >> 1) & 1), mc ^ (k & 1)
        peers.append(((px, py, pc), 4 * px + 2 * py + pc))
    return 4 * mx + 2 * my + mc, peers


def _comm_scratch(n):
    return [pltpu.SemaphoreType.DMA((n * (N_DEV - 1),)), pltpu.SemaphoreType.DMA((n * (N_DEV - 1),)), pltpu.SemaphoreType.DMA((n,))]


class _Gather:
    def __init__(self, x_refs, o_refs, send_sems, recv_sems, local_sems):
        self.x, self.o, self.send, self.recv, self.local = x_refs, o_refs, send_sems, recv_sems, local_sems
        self.n = len(x_refs)
        mx, my, mc = lax.axis_index("x"), lax.axis_index("y"), lax.axis_index("c")
        self.me, self.sibling, self.mc = (mx, my, mc), (mx, my, 1 - mc), mc
        self.chips = [(1 - mx, my), (mx, 1 - my), (1 - mx, 1 - my)]

    @staticmethod
    def out_shape(xs):
        return [jax.ShapeDtypeStruct((N_DEV,) + x.shape, x.dtype) for x in xs]

    def _copy(self, i, k, block, to, src=None):
        px, py, pc = block
        dst = self.o[i].at[4 * px + 2 * py + pc]
        sem = (N_DEV - 1) * i + k
        return pltpu.make_async_remote_copy(src_ref=dst if src is None else src, dst_ref=dst, send_sem=self.send.at[sem],
                                            recv_sem=self.recv.at[sem], device_id=to, device_id_type=_MESH_ID)

    def _mine(self, i):
        mx, my, mc = self.me
        return pltpu.make_async_copy(self.x[i], self.o[i].at[4 * mx + 2 * my + mc], self.local.at[i])

    def _first(self):
        out = []
        for i in range(self.n):
            out.append(self._copy(i, 0, self.me, self.sibling, src=self.x[i]))
            out += [self._copy(i, 1 + j, self.me, (*chip, self.mc), src=self.x[i]) for j, chip in enumerate(self.chips)]
        return out

    def _passed(self, j, i):
        return self._copy(i, 4 + j, (*self.chips[j], self.mc), self.sibling)

    def start(self):
        for i in range(self.n):
            self._mine(i).start()
        for cp in self._first():
            cp.start()

    def forward(self):
        for j, chip in enumerate(self.chips):
            for i in range(self.n):
                self._copy(i, 1 + j, (*chip, self.mc), self.me).wait_recv()
                self._passed(j, i).start()

    def finish(self):
        for i in range(self.n):
            self._copy(i, 0, self.sibling, self.me).wait_recv()
            for j, chip in enumerate(self.chips):
                self._copy(i, 4 + j, (*chip, 1 - self.mc), self.me).wait_recv()
        for cp in self._first():
            cp.wait_send()
        for j in range(len(self.chips)):
            for i in range(self.n):
                self._passed(j, i).wait_send()
        for i in range(self.n):
            self._mine(i).wait()


class _Scatter:
    def __init__(self, x_refs, o_refs, send_sems, recv_sems, local_sems):
        self.x, self.o, self.send, self.recv, self.local = x_refs, o_refs, send_sems, recv_sems, local_sems
        self.n = len(x_refs)
        self.me, self.peers = _me_and_peers()

    @staticmethod
    def out_shape(xs):
        return [jax.ShapeDtypeStruct(x.shape, x.dtype) for x in xs]

    def _copy(self, i, k, src_slot, dst_slot, to):
        sem = (N_DEV - 1) * i + k
        return pltpu.make_async_remote_copy(src_ref=self.x[i].at[src_slot], dst_ref=self.o[i].at[dst_slot], send_sem=self.send.at[sem],
                                            recv_sem=self.recv.at[sem], device_id=to, device_id_type=_MESH_ID)

    def _mine(self, i):
        return pltpu.make_async_copy(self.x[i].at[self.me], self.o[i].at[self.me], self.local.at[i])

    def _sends(self):
        return [self._copy(i, k, slot, self.me, peer) for k, (peer, slot) in enumerate(self.peers) for i in range(self.n)]

    def start(self):
        for i in range(self.n):
            self._mine(i).start()
        for cp in self._sends():
            cp.start()

    def forward(self):
        pass

    def finish(self):
        for k, (peer, slot) in enumerate(self.peers):
            for i in range(self.n):
                self._copy(i, k, slot, slot, peer).wait_recv()
        for cp in self._sends():
            cp.wait_send()
        for i in range(self.n):
            self._mine(i).wait()


class _Hosted:
    def __init__(self, kind, xs):
        self.kind, self.xs, self.n = kind, list(xs), len(xs)
        self.in_specs = [_HBM] * self.n
        self.out_specs = [_HBM] * self.n
        self.out_shape = kind.out_shape(self.xs)
        self.scratch = _comm_scratch(self.n)

    def run(self, x_refs, o_refs, sems, step, total):
        for when, phase in ((0, "start"), (3 * total // 4, "forward"), (total - 1, "finish")):
            @pl.when(step == when)
            def _(phase=phase):
                getattr(self.kind(x_refs, o_refs, *sems), phase)()


def gather_blocks(xs, name):
    n = len(xs)

    def body(*refs):
        g = _Gather(refs[:n], refs[n:2 * n], *refs[2 * n:])
        g.start()
        g.forward()
        g.finish()

    return pl.pallas_call(
        body, name=name, in_specs=[_HBM] * n, out_specs=[_HBM] * n,
        out_shape=_Gather.out_shape(xs), scratch_shapes=_comm_scratch(n),
    )(*xs)


def join_tokens(ctx, x, hosted, name):
    B, n_ctx, _ = ctx.shape
    S = x.shape[1]
    assert n_ctx == TOKEN_BLOCK
    nb = (n_ctx + S) // TOKEN_BLOCK
    nx = hosted.n

    def body(*refs):
        c_ref, x_ref, o_ref = refs[0], refs[1], refs[2 + nx]
        i = pl.program_id(0)
        hosted.run(refs[2:2 + nx], refs[3 + nx:3 + 2 * nx], refs[3 + 2 * nx:], i, B * nb)

        @pl.when(i % nb == 0)
        def _():
            o_ref[...] = c_ref[...]

        @pl.when(i % nb > 0)
        def _():
            o_ref[...] = x_ref[...]

    blk = (TOKEN_BLOCK, D_MODEL)
    outs = pl.pallas_call(
        body, name=name, grid=(B * nb,),
        in_specs=[pl.BlockSpec(blk, lambda i: (i // nb, 0)), pl.BlockSpec(blk, lambda i: ((i // nb) * (nb - 1) + jnp.maximum(i % nb - 1, 0), 0))]
        + hosted.in_specs,
        out_specs=[pl.BlockSpec(blk, lambda i: (i, 0))] + hosted.out_specs,
        out_shape=[jax.ShapeDtypeStruct((B * nb * TOKEN_BLOCK, D_MODEL), F32)] + hosted.out_shape,
        scratch_shapes=hosted.scratch, compiler_params=_params(("arbitrary",)),
    )(ctx.reshape(B * n_ctx, D_MODEL), x.reshape(B * S, D_MODEL), *hosted.xs)
    return outs[0], list(outs[1:])


def reduce_adamw(slots, w, m, v, name, tr=256):
    R, C = w.shape
    nl = len(slots)
    ns = slots[0].shape[0]
    rows = R // nl
    tr = min(tr, rows)
    nbl = rows // tr
    c1 = 1.0 / (1.0 - ADAM_B1 ** ADAM_STEP)
    c2 = 1.0 / (1.0 - ADAM_B2 ** ADAM_STEP)

    def body(*refs):
        w_ref, m_ref, v_ref, g_ref, d_ref, nm_ref, nv_ref = refs[nl:]
        part = pl.program_id(0) // nbl
        g = None
        for l, s_ref in enumerate(refs[:nl]):
            gl = s_ref[0].astype(F32)
            for s in range(1, ns):
                gl = gl + s_ref[s].astype(F32)
            g = gl if g is None else jnp.where(part == l, gl, g)
        nm = ADAM_B1 * m_ref[...] + (1.0 - ADAM_B1) * g
        nv = ADAM_B2 * v_ref[...] + (1.0 - ADAM_B2) * (g * g)
        g_ref[...] = g
        nm_ref[...] = nm
        nv_ref[...] = nv
        d_ref[...] = -ADAM_LR * ((nm * c1) / (jnp.sqrt(nv * c2) + ADAM_EPS) + ADAM_WD * w_ref[...])

    blk = pl.BlockSpec((tr, C), lambda i: (i, 0))
    sspecs = [pl.BlockSpec((ns, tr, C), lambda i, l=l: (0, jnp.clip(i - l * nbl, 0, nbl - 1), 0)) for l in range(nl)]
    return pl.pallas_call(
        body, name=name, grid=(R // tr,),
        in_specs=sspecs + [blk, blk, blk], out_specs=[blk] * 4,
        out_shape=[jax.ShapeDtypeStruct((R, C), F32)] * 4,
        compiler_params=_params(("parallel",), VMEM_LIMIT),
    )(*slots, w, m, v)


ARG_WEIGHTS = ("c_ctx", "mod_w", "mod_b", "pre_norm", "post_norm", "w_in", "mla_q_norm", "mla_w_uq", "mla_kv_norm", "mla_w_ukv", "pool_w",
               "pool_scale", "gla_af_w2", "gla_af_b", "gla_ab_w2", "gla_ab_b", "gla_norm", "w_branch_mla", "w_branch_pool", "w_branch_gla", "w_out")
SHARDED = ("mod_w", "w_in", "mla_w_uq", "mla_w_ukv", "gla_af_w2", "gla_ab_w2", "w_branch_mla", "w_branch_pool", "w_branch_gla", "w_out")
ROW_SHARDED = ("w_out",)
REPLICATED = tuple(n for n in ARG_WEIGHTS if n not in SHARDED)
PACK_ROWS = 512


def _pack(parts, dtype):
    flat = jnp.concatenate([p.astype(dtype).reshape(-1) for p in parts])
    n = flat.shape[0]
    total = -(-n // (PACK_ROWS * LANES)) * (PACK_ROWS * LANES)
    return jnp.pad(flat, (0, total - n)).reshape(total // LANES, LANES)


def _unpack(buf, shapes):
    flat = buf.reshape(-1)
    out, off = [], 0
    for shp in shapes:
        n = math.prod(shp)
        out.append(flat[off:off + n].reshape(shp))
        off += n
    return out


def _gathered_to_full(g, name):
    _, r, cs = g.shape
    if name in ROW_SHARDED:
        return g.reshape(N_DEV * r, cs)
    return g.transpose(1, 0, 2).reshape(r, N_DEV * cs)


def _full_to_slots(w, name):
    if name in ROW_SHARDED:
        return w.reshape(N_DEV, w.shape[0] // N_DEV, w.shape[1])
    return w.reshape(w.shape[0], N_DEV, w.shape[1] // N_DEV).transpose(1, 0, 2)


def kernel(x, c, ctx, c_ctx, mod_w, mod_b, pre_norm, post_norm, w_in, mla_q_norm, mla_w_uq, mla_kv_norm, mla_w_ukv, pool_w, pool_scale, gla_af_w2, gla_af_b, gla_ab_w2, gla_ab_b, gla_norm, w_branch_mla, w_branch_pool, w_branch_gla, w_out, loss_target, m_c_ctx, m_mod_w, m_mod_b, m_pre_norm, m_post_norm, m_w_in, m_mla_q_norm, m_mla_w_uq, m_mla_kv_norm, m_mla_w_ukv, m_pool_w, m_pool_scale, m_gla_af_w2, m_gla_af_b, m_gla_ab_w2, m_gla_ab_b, m_gla_norm, m_w_branch_mla, m_w_branch_pool, m_w_branch_gla, m_w_out, v_c_ctx, v_mod_w, v_mod_b, v_pre_norm, v_post_norm, v_w_in, v_mla_q_norm, v_mla_w_uq, v_mla_kv_norm, v_mla_w_ukv, v_pool_w, v_pool_scale, v_gla_af_w2, v_gla_af_b, v_gla_ab_w2, v_gla_ab_b, v_gla_norm, v_w_branch_mla, v_w_branch_pool, v_w_branch_gla, v_w_out):
    local = dict(locals())
    wts = {n: local[n] for n in ARG_WEIGHTS}
    mom1 = {n: local["m_" + n] for n in ARG_WEIGHTS}
    mom2 = {n: local["v_" + n] for n in ARG_WEIGHTS}
    shard_shapes = [wts[n].shape for n in SHARDED]
    rep_shapes = [wts[n].shape for n in REPLICATED]
    kinds = ("grad", "delta", "new_m", "new_v")

    depth = w_in.shape[0]

    def shards(l):
        return [wts[n][l].astype(BF16) for n in SHARDED]

    first = []

    def join(ctx_, x_):
        x2, got = join_tokens(ctx_, x_, _Hosted(_Gather, shards(0)), "join_tokens")
        first.extend(got)
        return x2

    def layer_full(l, carried):
        return {n: _gathered_to_full(gw, n) for n, gw in zip(SHARDED, first if l == 0 else carried)}

    def host_fwd(l):
        return _Hosted(_Gather, shards(l + 1)) if l + 1 < depth else None

    exchanged = GRADS_EARLY + GRADS_LATE

    def slots(g, names):
        return [_full_to_slots(g[n], n).astype(BF16) for n in names]

    def host_bwd(l, g_above):
        return _Hosted(_Scatter, slots(g_above, exchanged))

    def host_own(l):
        return (lambda names, g: _Hosted(_Scatter, slots(g, names))) if l == 0 else None

    small = {n: wts[n] for n in REPLICATED}
    loss, grad_x, grads, g_c_ctx, arrived, (a8, dz8s) = local_step(x, c, ctx, c_ctx, small, loss_target, depth, layer_full, host_fwd, host_bwd, host_own, join)
    arrived = [a if isinstance(a, dict) else dict(zip(exchanged, a)) for a in arrived]

    g = {n: (g_c_ctx if n == "c_ctx" else jnp.stack([grads[l][n] for l in range(depth)])) for n in REPLICATED}
    gathered, a_all, dz_all = gather_blocks([_pack([g[n] for n in REPLICATED], BF16), a8, jnp.concatenate(dz8s, axis=0)], "gather_small_grads")
    outs = reduce_adamw([gathered], _pack([wts[n] for n in REPLICATED], F32), _pack([mom1[n] for n in REPLICATED], F32),
                        _pack([mom2[n] for n in REPLICATED], F32), "adamw_replicated")
    me = 4 * lax.axis_index("x") + 2 * lax.axis_index("y") + lax.axis_index("c")
    ncol = mod_w.shape[2]
    dz_cols = lax.dynamic_slice_in_dim(dz_all.reshape(N_DEV, depth, 8, 3 * D_MODEL), me * ncol, ncol, axis=3)
    g_mod_w = mod_dw_columns(a_all.reshape(N_DEV * 8, D_MODEL), dz_cols.transpose(1, 0, 2, 3).reshape(depth, N_DEV * 8, ncol), "mod_dw")

    res = {kind: {} for kind in kinds}
    for n, shp in zip(SHARDED, shard_shapes):
        flat = (shp[0] * shp[1], shp[2])
        parts = [g_mod_w.reshape((1,) + flat)] if n == "mod_w" else [arrived[l][n] for l in range(depth)]
        for kind, o in zip(kinds, reduce_adamw(parts, wts[n].reshape(flat), mom1[n].reshape(flat), mom2[n].reshape(flat), "adamw_" + n)):
            res[kind][n] = o.reshape(shp)
    for kind, o in zip(("grad", "delta", "new_m", "new_v"), outs):
        res[kind].update(zip(REPLICATED, _unpack(o, rep_shapes)))

    loss = lax.psum(loss[0, 0], ("x", "y", "c"))
    return (loss, grad_x, *[res[kind][n] for kind in ("grad", "delta", "new_m", "new_v") for n in ARG_WEIGHTS])
```

```python
import functools
import math

import jax
import jax.numpy as jnp
import numpy as np
from jax import lax
from jax.experimental import pallas as pl
from jax.experimental.pallas import tpu as pltpu

F32 = jnp.float32
BF16 = jnp.bfloat16

D_MODEL = 1024
NORM_EPS = 1e-6
GRID_W = 64
MLA_HEADS, MLA_Q_RANK, MLA_KV_RANK, MLA_NOPE, MLA_ROPE, MLA_V = 8, 256, 128, 64, 32, 64
MLA_WIDTH = MLA_HEADS * MLA_V
ROPE_BASE = 10000.0
ATT_SCALE = (MLA_NOPE + MLA_ROPE) ** -0.5
POOL_WINDOWS = (2, 4, 8, 16)
POOL_WIDTH, POOL_GROUP = 512, 128
GLA_HEADS, GLA_DK, GLA_DV = 4, 64, 128
GLA_KW, GLA_WIDTH = GLA_HEADS * GLA_DK, GLA_HEADS * GLA_DV
GLA_GATE_RANK, GLA_TAU, GLA_CHUNK = 16, 16.0, 64
IN_SIZES = (256, 128, 32, 512, 512, 512, 256, 256, 512, 16, 16, 512, 3 * D_MODEL)
ADAM_LR, ADAM_B1, ADAM_B2, ADAM_EPS, ADAM_WD, ADAM_STEP = 0.001, 0.9, 0.999, 1e-08, 0.01, 10
N_DEV = 8

LANES = 128
TOKEN_BLOCK = 256
WIDE_BLOCKS = (1152, 768)
VMEM_LIMIT = 48 * 1024 * 1024
NEG_BIG = -1e30

_NT = (((1,), (1,)), ((), ()))
_TN = (((0,), (0,)), ((), ()))


def _dot(a, b):
    return jnp.dot(a, b, preferred_element_type=F32)


def _dot_nt(a, b):
    return lax.dot_general(a, b, _NT, preferred_element_type=F32)


def _dot_tn(a, b):
    return lax.dot_general(a, b, _TN, preferred_element_type=F32)


def _params(sem=None, vmem=None):
    kw = {}
    if sem is not None:
        kw["dimension_semantics"] = sem
    if vmem is not None:
        kw["vmem_limit_bytes"] = vmem
    return pltpu.CompilerParams(**kw)


def _wide_block(rows):
    return next(t for t in WIDE_BLOCKS + (TOKEN_BLOCK,) if rows % t == 0)


def _full(shape):
    n = len(shape)
    return pl.BlockSpec(shape, lambda *_: (0,) * n)


def _sig(x):
    return 1.0 / (1.0 + jnp.exp(-x))


def _silu_and_grad(x):
    s = _sig(x)
    return x * s, s * (1.0 + x * (1.0 - s))


def _acc(ref, val, first):
    @pl.when(first)
    def _():
        ref[...] = val

    @pl.when(jnp.logical_not(first))
    def _():
        ref[...] += val


def mm_multi(a, ws, dtypes, name, tm=TOKEN_BLOCK):
    M, K = a.shape
    nw = len(ws)

    def body(a_ref, *refs):
        av = a_ref[...]
        for w_ref, o_ref in zip(refs[:nw], refs[nw:]):
            o_ref[...] = _dot(av, w_ref[...]).astype(o_ref.dtype)

    return pl.pallas_call(
        body, name=name, grid=(M // tm,),
        in_specs=[pl.BlockSpec((tm, K), lambda i: (i, 0))] + [_full(w.shape) for w in ws],
        out_specs=[pl.BlockSpec((tm, w.shape[1]), lambda i: (i, 0)) for w in ws],
        out_shape=[jax.ShapeDtypeStruct((M, w.shape[1]), dt) for w, dt in zip(ws, dtypes)],
        compiler_params=_params(("parallel",), VMEM_LIMIT),
    )(a, *ws)


def mm_dw(a, dz, name, tn=1024):
    M, K = a.shape
    n = dz.shape[1]
    tn = min(tn, n)
    tk = next(t for t in (3072, 1536, 1024, 512, TOKEN_BLOCK) if M % t == 0)

    def body(a_ref, dz_ref, o_ref):
        _acc(o_ref, _dot_tn(a_ref[...], dz_ref[...]), pl.program_id(1) == 0)

    return pl.pallas_call(
        body, name=name, grid=(n // tn, M // tk),
        in_specs=[pl.BlockSpec((tk, K), lambda j, k: (k, 0)), pl.BlockSpec((tk, tn), lambda j, k: (k, j))],
        out_specs=pl.BlockSpec((K, tn), lambda j, k: (0, j)),
        out_shape=jax.ShapeDtypeStruct((K, n), F32),
        compiler_params=_params(("parallel", "arbitrary"), VMEM_LIMIT),
    )(a, dz)


def mod_fwd(a8, w, b, name):
    tn = D_MODEL

    def body(a_ref, w_ref, b_ref, o_ref):
        a = a_ref[...]
        o_ref[...] = _dot((a * _sig(a)).astype(BF16), w_ref[...]) + b_ref[...]

    return pl.pallas_call(
        body, name=name, grid=(3,),
        in_specs=[_full(a8.shape), pl.BlockSpec((D_MODEL, tn), lambda j: (0, j)), pl.BlockSpec((1, tn), lambda j: (0, j))],
        out_specs=pl.BlockSpec((8, tn), lambda j: (0, j)),
        out_shape=jax.ShapeDtypeStruct((8, 3 * D_MODEL), F32),
        compiler_params=_params(("parallel",)),
    )(a8, w, b)


def mod_bwd(a8, w, dz8, name):
    tn = D_MODEL

    def body(a_ref, w_ref, dz_ref, db_ref, da_ref):
        a = a_ref[...]
        _, dsa = _silu_and_grad(a)
        dz = dz_ref[...]
        db_ref[...] = jnp.sum(dz, axis=0, keepdims=True)
        _acc(da_ref, _dot_nt(dz.astype(BF16), w_ref[...]) * dsa, pl.program_id(0) == 0)

    return pl.pallas_call(
        body, name=name, grid=(3,),
        in_specs=[_full(a8.shape), pl.BlockSpec((D_MODEL, tn), lambda j: (0, j)), pl.BlockSpec((8, tn), lambda j: (0, j))],
        out_specs=[pl.BlockSpec((1, tn), lambda j: (0, j)), _full((8, D_MODEL))],
        out_shape=[jax.ShapeDtypeStruct((1, 3 * D_MODEL), F32), jax.ShapeDtypeStruct((8, D_MODEL), F32)],
        compiler_params=_params(("arbitrary",)),
    )(a8, w, dz8)


def mod_dw_columns(a_all, dz_cols, name):
    depth, R, n = dz_cols.shape

    def body(a_ref, dz_ref, dw_ref):
        a = a_ref[...]
        dw_ref[0] = _dot_tn((a * _sig(a)).astype(BF16), dz_ref[0].astype(BF16))

    return pl.pallas_call(
        body, name=name, grid=(depth,),
        in_specs=[_full(a_all.shape), pl.BlockSpec((1, R, n), lambda l: (l, 0, 0))],
        out_specs=pl.BlockSpec((1, D_MODEL, n), lambda l: (l, 0, 0)),
        out_shape=jax.ShapeDtypeStruct((depth, D_MODEL, n), F32),
        compiler_params=_params(("parallel",)),
    )(a_all, dz_cols)


def _mod_row(nb):
    return lambda i: 2 * (i // nb) + jnp.minimum(i % nb, 1)


def _mod_spec(nb, part):
    row = _mod_row(nb)
    return pl.BlockSpec((1, 1, D_MODEL), lambda i: (row(i), 0, part))


def norm_in_proj(x2, g, ms, ws, nb, name):
    T = x2.shape[0]
    nw = len(ws)

    def body(x_ref, g_ref, sh_ref, sc_ref, *refs):
        x = x_ref[...]
        r = lax.rsqrt(jnp.mean(x * x, axis=-1, keepdims=True) + NORM_EPS)
        h = ((x * r) * g_ref[...] * (1.0 + sc_ref[0]) + sh_ref[0]).astype(BF16)
        refs[nw][...] = h
        for w_ref, o_ref in zip(refs[:nw], refs[nw + 1:]):
            o_ref[...] = _dot(h, w_ref[...])

    tok = lambda n: pl.BlockSpec((TOKEN_BLOCK, n), lambda i: (i, 0))
    outs = pl.pallas_call(
        body, name=name, grid=(T // TOKEN_BLOCK,),
        in_specs=[tok(D_MODEL), _full((1, D_MODEL)), _mod_spec(nb, 0), _mod_spec(nb, 1)] + [_full(w.shape) for w in ws],
        out_specs=[tok(D_MODEL)] + [tok(w.shape[1]) for w in ws],
        out_shape=[jax.ShapeDtypeStruct((T, D_MODEL), BF16)] + [jax.ShapeDtypeStruct((T, w.shape[1]), F32) for w in ws],
        compiler_params=_params(("parallel",), VMEM_LIMIT),
    )(x2, g, ms, ms, *ws)
    return outs[0], outs[1:]


def in_proj_norm_bwd(dzs, ws, x2, g, ms, dxres, nb, name, hosted=None):
    T = x2.shape[0]
    nw = len(ws)
    nx = hosted.n if hosted else 0
    nrow = ms.shape[0]
    row = _mod_row(nb)
    n_in = 2 * nw + 4

    def body(*refs):
        x_ref, g_ref, sc_ref, dxr_ref = refs[2 * nw:n_in]
        dx_ref, dsh_ref, dsc_ref, dg_ref = refs[n_in + nx:n_in + nx + 4]
        i = pl.program_id(0)
        if hosted:
            hosted.run(refs[n_in:n_in + nx], refs[n_in + nx + 4:n_in + 2 * nx + 4], refs[n_in + 2 * nx + 4:], i, T // TOKEN_BLOCK)
        dh = None
        for dz_ref, w_ref in zip(refs[:nw], refs[nw:2 * nw]):
            t = _dot_nt(dz_ref[...], w_ref[...])
            dh = t if dh is None else dh + t
        x = x_ref[...]
        g = g_ref[...]
        r = lax.rsqrt(jnp.mean(x * x, axis=-1, keepdims=True) + NORM_EPS)
        xn = x * r
        du = dh * (1.0 + sc_ref[0])
        dyg = du * g
        dx_ref[...] = dxr_ref[...] + r * (dyg - xn * jnp.mean(dyg * xn, axis=-1, keepdims=True))
        first = (i % nb) <= 1
        _acc(dsh_ref.at[0], jnp.sum(dh, axis=0, keepdims=True), first)
        _acc(dsc_ref.at[0], jnp.sum(dh * xn * g, axis=0, keepdims=True), first)
        _acc(dg_ref, jnp.sum(du * xn, axis=0, keepdims=True), i == 0)

    tok = lambda n: pl.BlockSpec((TOKEN_BLOCK, n), lambda i: (i, 0))
    acc = pl.BlockSpec((1, 1, D_MODEL), lambda i: (row(i), 0, 0))
    outs = pl.pallas_call(
        body, name=name, grid=(T // TOKEN_BLOCK,),
        in_specs=[tok(dz.shape[1]) for dz in dzs] + [_full(w.shape) for w in ws] + [tok(D_MODEL), _full((1, D_MODEL)), _mod_spec(nb, 1), tok(D_MODEL)]
        + (hosted.in_specs if hosted else []),
        out_specs=[tok(D_MODEL), acc, acc, _full((1, D_MODEL))] + (hosted.out_specs if hosted else []),
        out_shape=[jax.ShapeDtypeStruct((T, D_MODEL), F32), jax.ShapeDtypeStruct((nrow, 1, D_MODEL), F32),
                   jax.ShapeDtypeStruct((nrow, 1, D_MODEL), F32), jax.ShapeDtypeStruct((1, D_MODEL), F32)] + (hosted.out_shape if hosted else []),
        scratch_shapes=hosted.scratch if hosted else [],
        compiler_params=_params(("arbitrary",), VMEM_LIMIT),
    )(*dzs, *ws, x2, g, ms, dxres, *(hosted.xs if hosted else []))
    return outs[:4], list(outs[4:])


def _rot(x):
    lane = lax.broadcasted_iota(jnp.int32, x.shape, 1)
    return jnp.where((lane % 16) < 8, -pltpu.roll(x, LANES - 8, 1), pltpu.roll(x, 8, 1))


def _rope(x, cos, sin):
    return x * cos + _rot(x) * sin


def _rope_t(dy, cos, sin):
    return dy * cos - _rot(dy * sin)


def _rms_rows(x):
    r = lax.rsqrt(jnp.mean(x * x, axis=-1, keepdims=True) + NORM_EPS)
    return x * r, r


def _rms_rows_bwd(dyg, xn, r):
    return r * (dyg - xn * jnp.mean(dyg * xn, axis=-1, keepdims=True))


def mla_prep_fwd(za, qg, kvg, wqn, wqr, wkv, cos, sin, nb, name):
    T = za.shape[0]
    W = MLA_HEADS * LANES

    def body(z_ref, qg_ref, kvg_ref, wqn_ref, wqr_ref, wkv_ref, cos_ref, sin_ref, qn_ref, qr_ref, kv_ref, kr_ref):
        z = z_ref[...]
        cos = cos_ref[...]
        sin = sin_ref[...]
        xq, _ = _rms_rows(z[:, 0:256])
        qn = (xq * qg_ref[...]).astype(BF16)
        qn_ref[...] = (_dot(qn, wqn_ref[...]) * ATT_SCALE).astype(BF16)
        qr = _dot(qn, wqr_ref[...])
        for h in range(MLA_HEADS):
            sl = slice(LANES * h, LANES * (h + 1))
            qr_ref[:, sl] = (_rope(qr[:, sl], cos, sin) * ATT_SCALE).astype(BF16)
        xkv, _ = _rms_rows(z[:, 256:384])
        kv_ref[...] = _dot((xkv * kvg_ref[...]).astype(BF16), wkv_ref[...]).astype(BF16)
        kr_ref[...] = _rope(z[:, 384:512], cos, sin).astype(BF16)

    tb = _wide_block(nb * TOKEN_BLOCK)
    npos = nb * TOKEN_BLOCK // tb
    tok = lambda n: pl.BlockSpec((tb, n), lambda i: (i, 0))
    pos = pl.BlockSpec((tb, LANES), lambda i: (i % npos, 0))
    return pl.pallas_call(
        body, name=name, grid=(T // tb,),
        in_specs=[tok(512), _full(qg.shape), _full(kvg.shape), _full(wqn.shape), _full(wqr.shape), _full(wkv.shape), pos, pos],
        out_specs=[tok(W), tok(W), tok(W), tok(LANES)],
        out_shape=[jax.ShapeDtypeStruct((T, W), BF16)] * 3 + [jax.ShapeDtypeStruct((T, LANES), BF16)],
        compiler_params=_params(("parallel",)),
    )(za, qg, kvg, wqn, wqr, wkv, cos, sin)


def mla_prep_bwd(dqn, dqr, dkv, dkr, za, qg, kvg, wqn, wqr, wkv, cos, sin, nb, name):
    T = za.shape[0]
    W = MLA_HEADS * LANES

    def body(dqn_ref, dqr_ref, dkv_ref, dkr_ref, z_ref, qg_ref, kvg_ref, wqn_ref, wqr_ref, wkv_ref, cos_ref, sin_ref,
             dz_ref, dwqn_ref, dwqr_ref, dwkv_ref, dqg_ref, dkvg_ref):
        first = pl.program_id(0) == 0
        z = z_ref[...]
        cos = cos_ref[...]
        sin = sin_ref[...]
        qg = qg_ref[...]
        kvg = kvg_ref[...]
        xq, rq = _rms_rows(z[:, 0:256])
        qn = (xq * qg).astype(BF16)
        a1 = (dqn_ref[...].astype(F32) * ATT_SCALE).astype(BF16)
        parts = []
        for h in range(MLA_HEADS):
            sl = slice(LANES * h, LANES * (h + 1))
            parts.append(_rope_t(dqr_ref[:, sl].astype(F32) * ATT_SCALE, cos, sin).astype(BF16))
        a2 = jnp.concatenate(parts, axis=1)
        dq = _dot_nt(a1, wqn_ref[...]) + _dot_nt(a2, wqr_ref[...])
        _acc(dwqn_ref, _dot_tn(qn, a1), first)
        _acc(dwqr_ref, _dot_tn(qn, a2), first)
        _acc(dqg_ref, jnp.sum(dq * xq, axis=0, keepdims=True), first)
        dz_ref[:, 0:256] = _rms_rows_bwd(dq * qg, xq, rq).astype(BF16)
        xkv, rkv = _rms_rows(z[:, 256:384])
        kvn = (xkv * kvg).astype(BF16)
        dkvb = dkv_ref[...].astype(BF16)
        dk = _dot_nt(dkvb, wkv_ref[...])
        _acc(dwkv_ref, _dot_tn(kvn, dkvb), first)
        _acc(dkvg_ref, jnp.sum(dk * xkv, axis=0, keepdims=True), first)
        dz_ref[:, 256:384] = _rms_rows_bwd(dk * kvg, xkv, rkv).astype(BF16)
        dz_ref[:, 384:512] = _rope_t(dkr_ref[...], cos, sin).astype(BF16)

    tb = _wide_block(nb * TOKEN_BLOCK)
    npos = nb * TOKEN_BLOCK // tb
    tok = lambda n: pl.BlockSpec((tb, n), lambda i: (i, 0))
    pos = pl.BlockSpec((tb, LANES), lambda i: (i % npos, 0))
    return pl.pallas_call(
        body, name=name, grid=(T // tb,),
        in_specs=[tok(W), tok(W), tok(W), tok(LANES), tok(512), _full(qg.shape), _full(kvg.shape), _full(wqn.shape),
                  _full(wqr.shape), _full(wkv.shape), pos, pos],
        out_specs=[tok(512), _full(wqn.shape), _full(wqr.shape), _full(wkv.shape), _full(qg.shape), _full(kvg.shape)],
        out_shape=[jax.ShapeDtypeStruct((T, 512), BF16), jax.ShapeDtypeStruct(wqn.shape, F32), jax.ShapeDtypeStruct(wqr.shape, F32),
                   jax.ShapeDtypeStruct(wkv.shape, F32), jax.ShapeDtypeStruct(qg.shape, F32), jax.ShapeDtypeStruct(kvg.shape, F32)],
        compiler_params=_params(("arbitrary",)),
    )(dqn, dqr, dkv, dkr, za, qg, kvg, wqn, wqr, wkv, cos, sin)


def _att_qk(qn_ref, qr_ref, kv_ref, kr, j):
    sl = slice(LANES * j, LANES * (j + 1))
    q = jnp.concatenate([qn_ref[0, :, sl], qr_ref[0, :, sl]], axis=1)
    kvj = kv_ref[0, :, sl]
    k = jnp.concatenate([kvj, kr], axis=1)
    return q, k, kvj, _dot_nt(q, k)


def _att_specs(L, lk, q0, pairs=1):
    TQ, W2 = TOKEN_BLOCK, 2 * LANES * pairs
    qspec = pl.BlockSpec((1, TQ, W2), lambda b, h, i: (b, i + q0, h))
    kvspec = pl.BlockSpec((1, lk, W2), lambda b, h, i: (b, 0, h))
    krspec = pl.BlockSpec((1, lk, LANES), lambda b, h, i: (b, 0, 0))
    gspec = pl.BlockSpec((1, TQ, LANES * pairs), lambda b, h, i: (b, i + q0, h))
    lspec = pl.BlockSpec((1, pairs, TQ, LANES), lambda b, h, i: (b, h, i + q0, 0))
    return qspec, kvspec, krspec, gspec, lspec


_ANY = pl.BlockSpec(memory_space=pl.ANY)


def attention_fwd(qn, qr, kv, kr, zg, n_ctx, name, hosted=None):
    B, L, _ = qn.shape
    TQ = TOKEN_BLOCK
    PAIRS = 2
    HP = MLA_HEADS // 2 // PAIRS
    shapes = [jax.ShapeDtypeStruct((B, L, MLA_WIDTH), F32), jax.ShapeDtypeStruct((B, L, MLA_WIDTH), BF16),
              jax.ShapeDtypeStruct((B, MLA_HEADS // 2, L, LANES), F32)]

    nx = hosted.n if hosted else 0
    NQ = L // TQ - 1

    def body(*refs):
        if hosted:
            step = (pl.program_id(0) * HP + pl.program_id(1)) * NQ + pl.program_id(2)
            hosted.run(refs[5:5 + nx], refs[8 + nx:8 + 2 * nx], refs[8 + 2 * nx:], step, B * HP * NQ)
        _fwd_step(*refs[:5], *refs[5 + nx:8 + nx])

    def body_ctx(qn_ref, qr_ref, kv_ref, kr_ref, g_ref, *rest):
        _fwd_step(qn_ref, qr_ref, kv_ref, kr_ref, g_ref, *rest[-3:])

    def _fwd_step(qn_ref, qr_ref, kv_ref, kr_ref, g_ref, ya_ref, ym_ref, lse_ref):
        kr_v = kr_ref[0]
        for pr in range(PAIRS):
            outs, lses = [], []
            for j in (2 * pr, 2 * pr + 1):
                _, _, kvj, s = _att_qk(qn_ref, qr_ref, kv_ref, kr_v, j)
                m = jnp.max(s, axis=-1, keepdims=True)
                p = jnp.exp(s - m).astype(BF16)
                lane_k = lax.broadcasted_iota(jnp.int32, kvj.shape, 1)
                o = _dot(p, jnp.where(lane_k < MLA_V, jnp.ones_like(kvj), kvj))
                l = o[:, 0:1]
                outs.append(o / l)
                lses.append(m + jnp.log(l))
            lane = lax.broadcasted_iota(jnp.int32, outs[0].shape, 1)
            y = jnp.where(lane < MLA_V, pltpu.roll(outs[0], MLA_V, 1), outs[1])
            sl = slice(LANES * pr, LANES * (pr + 1))
            ya_ref[0, :, sl] = y
            g = g_ref[0, :, sl]
            ym_ref[0, :, sl] = (y * g * _sig(g)).astype(BF16)
            lse_ref[0, pr] = jnp.where(lane < MLA_V, lses[0], lses[1])

    qspec, kvspec, krspec, gspec, lspec = _att_specs(L, L, 1, PAIRS)
    main = pl.pallas_call(
        body, name=name, grid=(B, HP, NQ),
        in_specs=[qspec, qspec, kvspec, krspec, gspec] + (hosted.in_specs if hosted else []),
        out_specs=[gspec, gspec, lspec] + (hosted.out_specs if hosted else []),
        out_shape=shapes + (hosted.out_shape if hosted else []), scratch_shapes=hosted.scratch if hosted else [],
        compiler_params=_params(("arbitrary",) * 3 if hosted else ("parallel",) * 3, VMEM_LIMIT),
    )(qn, qr, kv, kr, zg, *(hosted.xs if hosted else []))
    qspec, kvspec, krspec, gspec, lspec = _att_specs(L, n_ctx, 0, PAIRS)
    outs = pl.pallas_call(
        body_ctx, name=name + "_ctx", grid=(B, HP, 1),
        in_specs=[qspec, qspec, kvspec, krspec, gspec, _ANY, _ANY, _ANY], out_specs=[gspec, gspec, lspec], out_shape=shapes,
        input_output_aliases={5: 0, 6: 1, 7: 2},
        compiler_params=_params(("parallel", "parallel", "parallel"), VMEM_LIMIT),
    )(qn, qr, kv, kr, zg, *main[:3])
    return (*outs, list(main[3:]))


def attention_bwd(qn, qr, kv, kr, zg, ya, lse, dym, n_ctx, name, hosted=None):
    B, L, _ = qn.shape
    TQ = TOKEN_BLOCK
    PAIRS = 2
    HP = MLA_HEADS // 2 // PAIRS
    W = MLA_HEADS * LANES
    shapes = [jax.ShapeDtypeStruct((B, L, W), BF16), jax.ShapeDtypeStruct((B, L, W), BF16), jax.ShapeDtypeStruct((B, L, W), F32),
              jax.ShapeDtypeStruct((B, L, LANES), F32), jax.ShapeDtypeStruct((B, L, MLA_WIDTH), BF16)]

    nx = hosted.n if hosted else 0
    NQ = L // TQ - 1

    def body(*refs):
        if hosted:
            step = (pl.program_id(0) * HP + pl.program_id(1)) * NQ + pl.program_id(2)
            hosted.run(refs[8:8 + nx], refs[13 + nx:13 + 2 * nx], refs[13 + 2 * nx:], step, B * HP * NQ)
        dkv_ref, dkr_ref = refs[10 + nx], refs[11 + nx]

        @pl.when(pl.program_id(2) == 0)
        def _():
            dkv_ref[...] = jnp.zeros_like(dkv_ref)

        @pl.when(jnp.logical_and(pl.program_id(2) == 0, pl.program_id(1) == 0))
        def _():
            dkr_ref[...] = jnp.zeros_like(dkr_ref)

        _bwd_step(*refs[:8], *refs[8 + nx:13 + nx])

    def body_ctx(qn_ref, qr_ref, kv_ref, kr_ref, g_ref, ya_ref, lse_ref, dy_ref, dkv_in, dkr_in, a0, a1, a2,
                 dqn_ref, dqr_ref, dkv_ref, dkr_ref, dzg_ref):
        dkv_ref[...] = dkv_in[...]

        @pl.when(pl.program_id(1) == 0)
        def _():
            dkr_ref[...] = dkr_in[...]

        _bwd_step(qn_ref, qr_ref, kv_ref, kr_ref, g_ref, ya_ref, lse_ref, dy_ref, dqn_ref, dqr_ref, dkv_ref, dkr_ref, dzg_ref)

    def _bwd_step(qn_ref, qr_ref, kv_ref, kr_ref, g_ref, ya_ref, lse_ref, dy_ref, dqn_ref, dqr_ref, dkv_ref, dkr_ref, dzg_ref):
        kr_v = kr_ref[0]
        for pr in range(PAIRS):
            psl = slice(LANES * pr, LANES * (pr + 1))
            silu, dsilu = _silu_and_grad(g_ref[0, :, psl])
            dy = dy_ref[0, :, psl]
            ya_v = ya_ref[0, :, psl]
            dya = dy * silu
            dzg_ref[0, :, psl] = (dy * ya_v * dsilu).astype(BF16)
            lane = lax.broadcasted_iota(jnp.int32, dya.shape, 1)
            hi = lane >= MLA_V
            d_out = [jnp.where(hi, pltpu.roll(dya, MLA_V, 1), 0.0), jnp.where(hi, dya, 0.0)]
            prod = dya * ya_v
            drow = [jnp.sum(jnp.where(hi, 0.0, prod), axis=-1, keepdims=True), jnp.sum(jnp.where(hi, prod, 0.0), axis=-1, keepdims=True)]
            lse_v = lse_ref[0, pr]
            for jj in range(2):
                j = 2 * pr + jj
                sl = slice(LANES * j, LANES * (j + 1))
                q, k, kvj, s = _att_qk(qn_ref, qr_ref, kv_ref, kr_v, j)
                pn = jnp.exp(s - lse_v[:, MLA_V * jj:MLA_V * jj + 1])
                dob = d_out[jj].astype(BF16)
                ds = (pn * (_dot_nt(dob, kvj) - drow[jj])).astype(BF16)
                dq = _dot(ds, k)
                dqn_ref[0, :, sl] = jnp.where(hi, 0.0, dq[:, :LANES]).astype(BF16)
                dqr_ref[0, :, sl] = dq[:, LANES:].astype(BF16)
                dk = _dot_tn(ds, q)
                dkv_ref[0, :, sl] += dk[:, :LANES] + _dot_tn(pn.astype(BF16), dob)
                dkr_ref[0] += dk[:, LANES:]

    sem = _params(("parallel", "arbitrary", "arbitrary"), VMEM_LIMIT)
    qspec, kvspec, krspec, gspec, lspec = _att_specs(L, L, 1, PAIRS)
    main = pl.pallas_call(
        body, name=name, grid=(B, HP, NQ),
        in_specs=[qspec, qspec, kvspec, krspec, gspec, gspec, lspec, gspec] + (hosted.in_specs if hosted else []),
        out_specs=[qspec, qspec, kvspec, krspec, gspec] + (hosted.out_specs if hosted else []),
        out_shape=shapes + (hosted.out_shape if hosted else []), scratch_shapes=hosted.scratch if hosted else [],
        compiler_params=_params(("arbitrary",) * 3, VMEM_LIMIT) if hosted else sem,
    )(qn, qr, kv, kr, zg, ya, lse, dym, *(hosted.xs if hosted else []))
    qspec, kvspec, krspec, gspec, lspec = _att_specs(L, n_ctx, 0, PAIRS)
    outs = pl.pallas_call(
        body_ctx, name=name + "_ctx", grid=(B, HP, 1),
        in_specs=[qspec, qspec, kvspec, krspec, gspec, gspec, lspec, gspec, kvspec, krspec, _ANY, _ANY, _ANY],
        out_specs=[qspec, qspec, kvspec, krspec, gspec], out_shape=shapes,
        input_output_aliases={8: 2, 9: 3, 10: 0, 11: 1, 12: 4}, compiler_params=sem,
    )(qn, qr, kv, kr, zg, ya, lse, dym, main[2], main[3], main[0], main[1], main[4])
    return (*outs, list(main[5:]))


def _seg_bounds(rows, n_ctx, L):
    in_ctx = rows < n_ctx
    return jnp.where(in_ctx, 0, n_ctx), jnp.where(in_ctx, n_ctx, L)


def _seg_shift(x, k, rows, lo, hi):
    src = rows + k
    return jnp.where(jnp.logical_and(src >= lo, src < hi), pltpu.roll(x, (-k) % x.shape[0], 0), 0.0)


def _run_sum(u, m, rows, lo, hi, step):
    acc, k = u, 1
    while k < m:
        acc = acc + _seg_shift(acc, step * k, rows, lo, hi)
        k *= 2
    return acc


def _window_sum(u, w, rows, lo, hi, mirror):
    right = _run_sum(u, w // 2, rows, lo, hi, 1)
    left = _run_sum(u, w // 2, rows, lo, hi, -1)
    if mirror:
        return left + _seg_shift(right, 1, rows, lo, hi)
    return right + _seg_shift(left, -1, rows, lo, hi)


def _window_count(w, rows, lo, hi):
    pos = rows - lo
    return (jnp.minimum(pos + w // 2, hi - lo) - jnp.maximum(pos - w // 2, 0)).astype(F32)


def pool_fwd(px, pg, pw, ps, n_ctx, name):
    B, L, _ = px.shape

    def body(px_ref, pg_ref, pw_ref, ps_ref, y_ref):
        rows = lax.broadcasted_iota(jnp.int32, (L, POOL_GROUP), 0)
        lo, hi = _seg_bounds(rows, n_ctx, L)
        for gi, w in enumerate(POOL_WINDOWS):
            sl = slice(POOL_GROUP * gi, POOL_GROUP * (gi + 1))
            u = px_ref[0, :, sl]
            pooled = _window_sum(u, w, rows, lo, hi, False) / _window_count(w, rows, lo, hi) - u
            mixed = _dot(pooled.astype(BF16), pw_ref[gi])
            g = pg_ref[0, :, sl]
            y_ref[0, :, sl] = (mixed * ps_ref[:, sl] * (g * _sig(g))).astype(BF16)

    tok = pl.BlockSpec((1, L, POOL_WIDTH), lambda b: (b, 0, 0))
    return pl.pallas_call(
        body, name=name, grid=(B,),
        in_specs=[tok, tok, _full(pw.shape), _full(ps.shape)],
        out_specs=tok, out_shape=jax.ShapeDtypeStruct((B, L, POOL_WIDTH), BF16),
        compiler_params=_params(("parallel",), VMEM_LIMIT),
    )(px, pg, pw, ps)


def pool_bwd(px, pg, pw, ps, dy, n_ctx, name):
    B, L, _ = px.shape

    def body(px_ref, pg_ref, pw_ref, ps_ref, dy_ref, dpx_ref, dpg_ref, dpw_ref, dps_ref):
        first = pl.program_id(0) == 0
        rows = lax.broadcasted_iota(jnp.int32, (L, POOL_GROUP), 0)
        lo, hi = _seg_bounds(rows, n_ctx, L)
        for gi, w in enumerate(POOL_WINDOWS):
            sl = slice(POOL_GROUP * gi, POOL_GROUP * (gi + 1))
            u = px_ref[0, :, sl]
            cnt = _window_count(w, rows, lo, hi)
            pooled = (_window_sum(u, w, rows, lo, hi, False) / cnt - u).astype(BF16)
            mixed = _dot(pooled, pw_ref[gi])
            silu, dsilu = _silu_and_grad(pg_ref[0, :, sl])
            sc = ps_ref[:, sl]
            dyv = dy_ref[0, :, sl]
            _acc(dps_ref.at[:, sl], jnp.sum(dyv * mixed * silu, axis=0, keepdims=True), first)
            dpg_ref[0, :, sl] = (dyv * mixed * sc * dsilu).astype(BF16)
            dmixed = (dyv * sc * silu).astype(BF16)
            _acc(dpw_ref.at[gi], _dot_tn(pooled, dmixed), first)
            dpooled = _dot_nt(dmixed, pw_ref[gi])
            dpx_ref[0, :, sl] = (_window_sum(dpooled / cnt, w, rows, lo, hi, True) - dpooled).astype(BF16)

    tok = pl.BlockSpec((1, L, POOL_WIDTH), lambda b: (b, 0, 0))
    return pl.pallas_call(
        body, name=name, grid=(B,),
        in_specs=[tok, tok, _full(pw.shape), _full(ps.shape), tok],
        out_specs=[tok, tok, _full(pw.shape), _full(ps.shape)],
        out_shape=[jax.ShapeDtypeStruct((B, L, POOL_WIDTH), BF16)] * 2 + [jax.ShapeDtypeStruct(pw.shape, F32), jax.ShapeDtypeStruct(ps.shape, F32)],
        compiler_params=_params(("arbitrary",), VMEM_LIMIT),
    )(px, pg, pw, ps, dy)


_SCAN_STEPS = (1, 2, 4, 8, 16, 32)
SCAN_CHUNKS = 4


def _cum_fwd(x, r):
    for s in _SCAN_STEPS:
        x = x + jnp.where(r >= s, pltpu.roll(x, s, 0), 0.0)
    return x


def _cum_bwd(x, r):
    n = x.shape[0]
    for s in _SCAN_STEPS:
        x = x + jnp.where(r + s < GLA_CHUNK, pltpu.roll(x, n - s, 0), 0.0)
    return x


def _log_sigmoid(x):
    return jnp.minimum(x, 0.0) - jnp.log(1.0 + jnp.exp(-jnp.abs(x)))


def _gla_decays(lr, w_ref, b_ref, r, reverse):
    pre = _dot(lr, w_ref[...]) + b_ref[...]
    a = _log_sigmoid(pre) / GLA_TAU
    return pre, a, (_cum_bwd(a, r) if reverse else _cum_fwd(a, r)), _chunk_total(a)


def _chunk_total(x):
    x3 = x.reshape(x.shape[0] // GLA_CHUNK, GLA_CHUNK, x.shape[1])
    return jnp.broadcast_to(jnp.sum(x3, axis=1, keepdims=True), x3.shape).reshape(x.shape)


def gla_prep_fwd(zlr, zq, zk, waf, wab, baf, bab, name):
    T = zlr.shape[0]
    tb = _wide_block(T)

    def body(lr_ref, q_ref, k_ref, waf_ref, wab_ref, baf_ref, bab_ref, qf_ref, kf_ref, ksf_ref, tf_ref, qb_ref, kb_ref, ksb_ref, tb_ref):
        r = lax.broadcasted_iota(jnp.int32, (tb, GLA_KW), 0) % GLA_CHUNK
        lr = lr_ref[...].astype(BF16)
        q = q_ref[...] * GLA_DK ** -0.5
        k = k_ref[...]
        for rev, w_ref, b_ref, qo, ko, kso, to in ((False, waf_ref, baf_ref, qf_ref, kf_ref, ksf_ref, tf_ref),
                                                   (True, wab_ref, bab_ref, qb_ref, kb_ref, ksb_ref, tb_ref)):
            _, _, b, tot = _gla_decays(lr, w_ref, b_ref, r, rev)
            qo[...] = (q * jnp.exp(b)).astype(BF16)
            ko[...] = (k * jnp.exp(-b)).astype(BF16)
            kso[...] = (k * jnp.exp(tot - b)).astype(BF16)
            to[...] = tot

    tok = lambda n: pl.BlockSpec((tb, n), lambda i: (i, 0))
    outs = [jax.ShapeDtypeStruct((T, GLA_KW), BF16)] * 3 + [jax.ShapeDtypeStruct((T, GLA_KW), F32)]
    return pl.pallas_call(
        body, name=name, grid=(T // tb,),
        in_specs=[tok(LANES), tok(GLA_KW), tok(GLA_KW), _full(waf.shape), _full(wab.shape), _full(baf.shape), _full(bab.shape)],
        out_specs=[tok(GLA_KW)] * 8, out_shape=outs + outs,
        compiler_params=_params(("parallel",)),
    )(zlr, zq, zk, waf, wab, baf, bab)


def gla_prep_bwd(zlr, zq, zk, waf, wab, baf, bab, gf, gb, dvs, name):
    T = zlr.shape[0]
    tb = _wide_block(T)

    def body(lr_ref, q_ref, k_ref, waf_ref, wab_ref, baf_ref, bab_ref, dqf, dkf, dksf, ddf, dqb, dkb, dksb, ddb, dvf, dvb,
             dlr_ref, dq_ref, dk_ref, dwaf_ref, dwab_ref, dbaf_ref, dbab_ref, dv_ref):
        first = pl.program_id(0) == 0
        dv_ref[...] = (dvf[...] + dvb[...]).astype(BF16)
        r = lax.broadcasted_iota(jnp.int32, (tb, GLA_KW), 0) % GLA_CHUNK
        lr = lr_ref[...].astype(BF16)
        q = q_ref[...] * GLA_DK ** -0.5
        k = k_ref[...]
        dq_tot = None
        dk_tot = None
        dlr = None
        for rev, w_ref, b_ref, dqt, dkt, dks, ddec, dw_ref, db_ref in (
                (False, waf_ref, baf_ref, dqf, dkf, dksf, ddf, dwaf_ref, dbaf_ref),
                (True, wab_ref, bab_ref, dqb, dkb, dksb, ddb, dwab_ref, dbab_ref)):
            pre, _, b, tot = _gla_decays(lr, w_ref, b_ref, r, rev)
            e1 = jnp.exp(b)
            e2 = jnp.exp(-b)
            e3 = jnp.exp(tot - b)
            dqt_v = dqt[...]
            dkt_v = dkt[...]
            dks_v = dks[...]
            dq = dqt_v * e1
            dk = dkt_v * e2 + dks_v * e3
            g3 = dks_v * (k * e3)
            d_b = dqt_v * (q * e1) - dkt_v * (k * e2) - g3
            d_tot = _chunk_total(g3) + ddec[...] * jnp.exp(tot)
            da = (_cum_fwd(d_b, r) if rev else _cum_bwd(d_b, r)) + d_tot
            dpre = (da * (_sig(-pre) / GLA_TAU)).astype(BF16)
            t = _dot_nt(dpre, w_ref[...])
            dlr = t if dlr is None else dlr + t
            _acc(dw_ref, _dot_tn(lr, dpre), first)
            _acc(db_ref, jnp.sum(dpre.astype(F32), axis=0, keepdims=True), first)
            dq_tot = dq if dq_tot is None else dq_tot + dq
            dk_tot = dk if dk_tot is None else dk_tot + dk
        dlr_ref[...] = dlr.astype(BF16)
        dq_ref[...] = (dq_tot * GLA_DK ** -0.5).astype(BF16)
        dk_ref[...] = dk_tot.astype(BF16)

    tok = lambda n: pl.BlockSpec((tb, n), lambda i: (i, 0))
    return pl.pallas_call(
        body, name=name, grid=(T // tb,),
        in_specs=[tok(LANES), tok(GLA_KW), tok(GLA_KW), _full(waf.shape), _full(wab.shape), _full(baf.shape), _full(bab.shape)] + [tok(GLA_KW)] * 8
        + [tok(GLA_WIDTH)] * 2,
        out_specs=[tok(LANES), tok(GLA_KW), tok(GLA_KW), _full(waf.shape), _full(wab.shape), _full(baf.shape), _full(bab.shape), tok(GLA_WIDTH)],
        out_shape=[jax.ShapeDtypeStruct((T, LANES), BF16), jax.ShapeDtypeStruct((T, GLA_KW), BF16), jax.ShapeDtypeStruct((T, GLA_KW), BF16),
                   jax.ShapeDtypeStruct(waf.shape, F32), jax.ShapeDtypeStruct(wab.shape, F32), jax.ShapeDtypeStruct(baf.shape, F32),
                   jax.ShapeDtypeStruct(bab.shape, F32), jax.ShapeDtypeStruct((T, GLA_WIDTH), BF16)],
        compiler_params=_params(("arbitrary",)),
    )(zlr, zq, zk, waf, wab, baf, bab, *gf, *gb, *dvs)


def _chunk_order(nc, n_ctx_chunks, reverse):
    if not reverse:
        return lambda c: c
    return lambda c: jnp.where(c < n_ctx_chunks, n_ctx_chunks - 1 - c, nc + n_ctx_chunks - 1 - c)


def _tri_mask4(reverse):
    ri = lax.broadcasted_iota(jnp.int32, (GLA_CHUNK, GLA_HEADS * GLA_CHUNK), 0)
    ci = lax.broadcasted_iota(jnp.int32, (GLA_CHUNK, GLA_HEADS * GLA_CHUNK), 1) % GLA_CHUNK
    return (ri <= ci) if reverse else (ri >= ci)


def _block_diag(x, rb, cb):
    x4 = jnp.concatenate([x] * GLA_HEADS, axis=0)
    r = lax.broadcasted_iota(jnp.int32, x4.shape, 0) // rb
    c = lax.broadcasted_iota(jnp.int32, x4.shape, 1) // cb
    return jnp.where(r == c, x4, jnp.zeros_like(x4))


def _diag_blocks(f, rb, cb):
    c = lax.broadcasted_iota(jnp.int32, (rb, GLA_HEADS * cb), 1) // cb
    out = None
    for h in range(GLA_HEADS):
        t = jnp.where(c == h, f[rb * h:rb * (h + 1)], 0.0)
        out = t if out is None else out + t
    return out


def gla_scan_fwd(dirs, v, n_ctx, name):
    B, L, _ = v.shape
    C, G = GLA_CHUNK, SCAN_CHUNKS
    nc = L // C
    orders = [_chunk_order(nc // G, n_ctx // C // G, rev) for rev in (False, True)]

    def body(qf, kf, ksf, tf, vf, qb, kb, ksb, tb, vb, of, ssf, ob, ssb, stf, stb):
        @pl.when(pl.program_id(1) == 0)
        def _():
            stf[...] = jnp.zeros_like(stf)
            stb[...] = jnp.zeros_like(stb)

        for sub in range(G):
            step(qf, kf, ksf, vf, tf, of, ssf, stf, False, sub)
            step(qb, kb, ksb, vb, tb, ob, ssb, stb, True, G - 1 - sub)

    def step(q_ref, k_ref, ks_ref, v_ref, tot_ref, o_ref, ss_ref, st, reverse, sub):
        rows = slice(C * sub, C * (sub + 1))
        S = st[...]
        ss_ref[0, sub] = S
        q = q_ref[0, rows]
        v = v_ref[0, rows]
        k4 = _block_diag(k_ref[0, rows], GLA_CHUNK, GLA_DK)
        v4 = _block_diag(v.astype(BF16), GLA_CHUNK, GLA_DV)
        s4 = _block_diag(S.astype(BF16), GLA_DV, GLA_DK)
        P = jnp.where(_tri_mask4(reverse), _dot_nt(q, k4), 0.0)
        o_ref[0, rows] = _dot(P.astype(BF16), v4) + _dot_nt(q, s4)
        st[...] = jnp.exp(tot_ref[0, C * sub:C * sub + 1, :]) * S + _diag_blocks(_dot(v.T.astype(BF16), ks_ref[0, rows]), GLA_DV, GLA_DK)

    in_specs, out_specs, out_shape = [], [], []
    for order in orders:
        tok = lambda n, order=order: pl.BlockSpec((1, G * C, n), lambda b, c: (b, order(c), 0))
        in_specs += [tok(GLA_KW), tok(GLA_KW), tok(GLA_KW), tok(GLA_KW), tok(GLA_WIDTH)]
        out_specs += [tok(GLA_WIDTH), pl.BlockSpec((1, G, GLA_DV, GLA_KW), lambda b, c, order=order: (b, order(c), 0, 0))]
        out_shape += [jax.ShapeDtypeStruct((B, L, GLA_WIDTH), F32), jax.ShapeDtypeStruct((B, nc, GLA_DV, GLA_KW), F32)]
    outs = pl.pallas_call(
        body, name=name, grid=(B, nc // G), in_specs=in_specs, out_specs=out_specs, out_shape=out_shape,
        scratch_shapes=[pltpu.VMEM((GLA_DV, GLA_KW), F32)] * 2,
        compiler_params=_params(("parallel", "arbitrary")),
    )(*dirs[0], v, *dirs[1], v)
    return outs[:2], outs[2:]


def gla_scan_bwd(dirs, v, do, n_ctx, name, hosted=None):
    B, L, _ = v.shape
    C, G = GLA_CHUNK, SCAN_CHUNKS
    nc = L // C
    npair = nc // G
    orders = []
    for rev in (False, True):
        fwd_order = _chunk_order(npair, n_ctx // C // G, rev)
        orders.append(lambda c, fwd_order=fwd_order: fwd_order(npair - 1 - c))

    nx = hosted.n if hosted else 0

    def body(*refs):
        qf, kf, ksf, tf, ssf, vf, dof, qb, kb, ksb, tb, ssb, vb, dob = refs[:14]
        dqf, dkf, dksf, dvf, ddf, dqb, dkb, dksb, dvb, ddb = refs[14 + nx:24 + nx]
        dstf, dstb = refs[24 + 2 * nx:26 + 2 * nx]
        if hosted:
            hosted.run(refs[14:14 + nx], refs[24 + nx:24 + 2 * nx], refs[26 + 2 * nx:], pl.program_id(0) * npair + pl.program_id(1), B * npair)

        @pl.when(pl.program_id(1) == 0)
        def _():
            dstf[...] = jnp.zeros_like(dstf)
            dstb[...] = jnp.zeros_like(dstb)

        for sub in range(G):
            step(qf, kf, ksf, vf, tf, ssf, dof, dqf, dkf, dksf, dvf, ddf, dstf, False, G - 1 - sub)
            step(qb, kb, ksb, vb, tb, ssb, dob, dqb, dkb, dksb, dvb, ddb, dstb, True, sub)

    def step(q_ref, k_ref, ks_ref, v_ref, tot_ref, ss_ref, do_ref, dq_ref, dk_ref, dks_ref, dv_ref, dd_ref, dst, reverse, sub):
        rows = slice(C * sub, C * (sub + 1))
        dSn = dst[...]
        S = ss_ref[0, sub]
        q = q_ref[0, rows]
        vb = v_ref[0, rows].astype(BF16)
        dob = do_ref[0, rows].astype(BF16)
        k4 = _block_diag(k_ref[0, rows], GLA_CHUNK, GLA_DK)
        v4 = _block_diag(vb, GLA_CHUNK, GLA_DV)
        s4 = _block_diag(S.astype(BF16), GLA_DV, GLA_DK)
        ds4 = _block_diag(dSn.astype(BF16), GLA_DV, GLA_DK)
        tri = _tri_mask4(reverse)
        P = jnp.where(tri, _dot_nt(q, k4), 0.0).astype(BF16)
        dP = jnp.where(tri, _dot_nt(dob, v4), 0.0).astype(BF16)
        dq_ref[0, rows] = _dot(dob, s4) + _dot(dP, k4)
        dk_ref[0, rows] = _diag_blocks(_dot_tn(dP, q), GLA_CHUNK, GLA_DK)
        dv_ref[0, rows] = _diag_blocks(_dot_tn(P, dob), GLA_CHUNK, GLA_DV) + _dot_nt(ks_ref[0, rows], ds4)
        dks_ref[0, rows] = _dot(vb, ds4)
        dd_ref[0, rows] = jnp.broadcast_to(jnp.sum(dSn * S, axis=0, keepdims=True), (C, GLA_KW))
        dst[...] = jnp.exp(tot_ref[0, C * sub:C * sub + 1, :]) * dSn + _diag_blocks(_dot_tn(dob, q), GLA_DV, GLA_DK)

    in_specs, out_specs, out_shape = [], [], []
    for order in orders:
        tok = lambda n, order=order: pl.BlockSpec((1, G * C, n), lambda b, c: (b, order(c), 0))
        in_specs += [tok(GLA_KW), tok(GLA_KW), tok(GLA_KW), tok(GLA_KW),
                     pl.BlockSpec((1, G, GLA_DV, GLA_KW), lambda b, c, order=order: (b, order(c), 0, 0)), tok(GLA_WIDTH), tok(GLA_WIDTH)]
        out_specs += [tok(GLA_KW), tok(GLA_KW), tok(GLA_KW), tok(GLA_WIDTH), tok(GLA_KW)]
        out_shape += [jax.ShapeDtypeStruct((B, L, GLA_KW), F32)] * 3 + [jax.ShapeDtypeStruct((B, L, GLA_WIDTH), F32), jax.ShapeDtypeStruct((B, L, GLA_KW), F32)]
    outs = pl.pallas_call(
        body, name=name, grid=(B, npair), in_specs=in_specs + (hosted.in_specs if hosted else []),
        out_specs=out_specs + (hosted.out_specs if hosted else []), out_shape=out_shape + (hosted.out_shape if hosted else []),
        scratch_shapes=[pltpu.VMEM((GLA_DV, GLA_KW), F32)] * 2 + (hosted.scratch if hosted else []),
        compiler_params=_params(("arbitrary", "arbitrary") if hosted else ("parallel", "arbitrary")),
    )(*dirs[0], v, do, *dirs[1], v, do, *(hosted.xs if hosted else []))
    return outs[:5], outs[5:10], list(outs[10:])


def gla_out_fwd(of, ob, gn, zg, name):
    T = of.shape[0]

    def body(of_ref, ob_ref, gn_ref, g_ref, y_ref):
        for h in range(GLA_HEADS):
            sl = slice(GLA_DV * h, GLA_DV * (h + 1))
            xn, _ = _rms_rows(of_ref[:, sl] + ob_ref[:, sl])
            g = g_ref[:, sl]
            y_ref[:, sl] = (xn * gn_ref[...] * (g * _sig(g))).astype(BF16)

    tb = _wide_block(T)
    tok = pl.BlockSpec((tb, GLA_WIDTH), lambda i: (i, 0))
    return pl.pallas_call(
        body, name=name, grid=(T // tb,),
        in_specs=[tok, tok, _full(gn.shape), tok], out_specs=tok,
        out_shape=jax.ShapeDtypeStruct((T, GLA_WIDTH), BF16),
        compiler_params=_params(("parallel",)),
    )(of, ob, gn, zg)


def gla_out_bwd(of, ob, gn, zg, dy, name):
    T = of.shape[0]

    def body(of_ref, ob_ref, gn_ref, g_ref, dy_ref, do_ref, dzg_ref, dgn_ref):
        first = pl.program_id(0) == 0
        gn_v = gn_ref[...]
        dgn = None
        for h in range(GLA_HEADS):
            sl = slice(GLA_DV * h, GLA_DV * (h + 1))
            xn, r = _rms_rows(of_ref[:, sl] + ob_ref[:, sl])
            silu, dsilu = _silu_and_grad(g_ref[:, sl])
            dyv = dy_ref[:, sl]
            dzg_ref[:, sl] = (dyv * xn * gn_v * dsilu).astype(BF16)
            dn = dyv * silu
            t = jnp.sum(dn * xn, axis=0, keepdims=True)
            dgn = t if dgn is None else dgn + t
            do_ref[:, sl] = _rms_rows_bwd(dn * gn_v, xn, r)
        _acc(dgn_ref, dgn, first)

    tb = _wide_block(T)
    tok = pl.BlockSpec((tb, GLA_WIDTH), lambda i: (i, 0))
    return pl.pallas_call(
        body, name=name, grid=(T // tb,),
        in_specs=[tok, tok, _full(gn.shape), tok, tok], out_specs=[tok, tok, _full(gn.shape)],
        out_shape=[jax.ShapeDtypeStruct((T, GLA_WIDTH), F32), jax.ShapeDtypeStruct((T, GLA_WIDTH), BF16), jax.ShapeDtypeStruct(gn.shape, F32)],
        compiler_params=_params(("arbitrary",)),
    )(of, ob, gn, zg, dy)


def merge_post_fwd(ys, zm, wbs, wo, x2, pg, ms, nb, name):
    T = x2.shape[0]

    def body(y0, y1, y2, zm_ref, w0, w1, w2, wo_ref, x_ref, pg_ref, gate_ref, xn_ref, out_ref, mg_ref):
        merged = None
        for i, (y_ref, w_ref) in enumerate(((y0, w0), (y1, w1), (y2, w2))):
            t = _sig(zm_ref[:, D_MODEL * i:D_MODEL * (i + 1)].astype(F32)) * _dot(y_ref[...], w_ref[...])
            merged = t if merged is None else merged + t
        mb = merged.astype(BF16)
        mg_ref[...] = mb
        out = _dot(mb, wo_ref[...])
        out_ref[...] = out
        on, _ = _rms_rows(out)
        xn_ref[...] = x_ref[...] + gate_ref[0] * (on * pg_ref[...])

    tok = lambda n: pl.BlockSpec((TOKEN_BLOCK, n), lambda i: (i, 0))
    return pl.pallas_call(
        body, name=name, grid=(T // TOKEN_BLOCK,),
        in_specs=[tok(512)] * 3 + [tok(3 * D_MODEL)] + [_full(w.shape) for w in wbs] + [_full(wo.shape), tok(D_MODEL), _full(pg.shape), _mod_spec(nb, 2)],
        out_specs=[tok(D_MODEL)] * 3,
        out_shape=[jax.ShapeDtypeStruct((T, D_MODEL), F32), jax.ShapeDtypeStruct((T, D_MODEL), F32), jax.ShapeDtypeStruct((T, D_MODEL), BF16)],
        compiler_params=_params(("parallel",), VMEM_LIMIT),
    )(*ys, zm, *wbs, wo, x2, pg, ms)


def merge_post_bwd(dxn, out, ys, zm, wbs, wo, pg, ms, nb, name):
    T = dxn.shape[0]
    nrow = ms.shape[0]
    row = _mod_row(nb)

    def body(dxn_ref, out_ref, y0, y1, y2, zm_ref, w0, w1, w2, wo_ref, pg_ref, gate_ref,
             dy0, dy1, dy2, dzm_ref, dout_ref, dp0, dp1, dp2, dgate_ref, dpg_ref):
        i = pl.program_id(0)
        dxn_v = dxn_ref[...]
        on, r = _rms_rows(out_ref[...])
        pg_v = pg_ref[...]
        _acc(dgate_ref.at[0], jnp.sum(dxn_v * on * pg_v, axis=0, keepdims=True), (i % nb) <= 1)
        dn = dxn_v * gate_ref[0]
        _acc(dpg_ref, jnp.sum(dn * on, axis=0, keepdims=True), i == 0)
        dout = _rms_rows_bwd(dn * pg_v, on, r).astype(BF16)
        dout_ref[...] = dout
        dmerged = _dot_nt(dout, wo_ref[...])
        for j, (y_ref, w_ref, dy_ref, dp_ref) in enumerate(((y0, w0, dy0, dp0), (y1, w1, dy1, dp1), (y2, w2, dy2, dp2))):
            sl = slice(D_MODEL * j, D_MODEL * (j + 1))
            g = _sig(zm_ref[:, sl].astype(F32))
            p = _dot(y_ref[...], w_ref[...])
            dzm_ref[:, sl] = (dmerged * p * g * (1.0 - g)).astype(BF16)
            dp = (dmerged * g).astype(BF16)
            dp_ref[...] = dp
            dy_ref[...] = _dot_nt(dp, w_ref[...])

    tok = lambda n: pl.BlockSpec((TOKEN_BLOCK, n), lambda i: (i, 0))
    return pl.pallas_call(
        body, name=name, grid=(T // TOKEN_BLOCK,),
        in_specs=[tok(D_MODEL), tok(D_MODEL)] + [tok(512)] * 3 + [tok(3 * D_MODEL)] + [_full(w.shape) for w in wbs] + [_full(wo.shape), _full(pg.shape), _mod_spec(nb, 2)],
        out_specs=[tok(512)] * 3 + [tok(3 * D_MODEL), tok(D_MODEL)] + [tok(D_MODEL)] * 3 + [pl.BlockSpec((1, 1, D_MODEL), lambda i: (row(i), 0, 0)), _full(pg.shape)],
        out_shape=[jax.ShapeDtypeStruct((T, 512), F32)] * 3 + [jax.ShapeDtypeStruct((T, 3 * D_MODEL), BF16), jax.ShapeDtypeStruct((T, D_MODEL), BF16)]
        + [jax.ShapeDtypeStruct((T, D_MODEL), BF16)] * 3 + [jax.ShapeDtypeStruct((nrow, 1, D_MODEL), F32), jax.ShapeDtypeStruct(pg.shape, F32)],
        compiler_params=_params(("arbitrary",), VMEM_LIMIT),
    )(dxn, out, *ys, zm, *wbs, wo, pg, ms)


def loss_head(y2, tgt2, nb, name):
    T = y2.shape[0]
    nlat = nb - 1

    def body(y_ref, t_ref, dy_ref, loss_ref, acc):
        i = pl.program_id(0)
        is_lat = (i % nb) > 0

        @pl.when(i == 0)
        def _():
            acc[...] = jnp.zeros_like(acc)

        @pl.when(is_lat)
        def _():
            e = y_ref[...] - t_ref[...]
            dy_ref[...] = e * (1.0 / D_MODEL)
            acc[...] += jnp.sum(e * e, axis=0, keepdims=True)

        @pl.when(jnp.logical_not(is_lat))
        def _():
            dy_ref[...] = jnp.zeros_like(dy_ref)

        @pl.when(i == pl.num_programs(0) - 1)
        def _():
            loss_ref[...] = jnp.sum(acc[...], axis=1, keepdims=True) * (0.5 / D_MODEL)

    tok = pl.BlockSpec((TOKEN_BLOCK, D_MODEL), lambda i: (i, 0))
    tgt = pl.BlockSpec((TOKEN_BLOCK, D_MODEL), lambda i: ((i // nb) * nlat + jnp.maximum(i % nb - 1, 0), 0))
    return pl.pallas_call(
        body, name=name, grid=(T // TOKEN_BLOCK,),
        in_specs=[tok, tgt], out_specs=[tok, _full((1, 1))],
        out_shape=[jax.ShapeDtypeStruct((T, D_MODEL), F32), jax.ShapeDtypeStruct((1, 1), F32)],
        scratch_shapes=[pltpu.VMEM((1, D_MODEL), F32)],
        compiler_params=_params(("arbitrary",)),
    )(y2, tgt2)


_IN_OFFS = tuple(int(o) for o in np.cumsum((0,) + IN_SIZES))
_IN_GROUPS = (("a", 0, 416, 512), ("mg", 416, 512, 512), ("px", 928, 512, 512), ("pg", 1440, 512, 512), ("gq", 1952, 256, 256),
              ("gk", 2208, 256, 256), ("gv", 2464, 512, 512), ("lr", 2976, 32, 128), ("gg", 3008, 512, 512), ("m", 3520, 3072, 3072))


def _pad_cols(w, n):
    return w if w.shape[1] == n else jnp.pad(w, ((0, 0), (0, n - w.shape[1])))


def layer_weights(w_in, w_uq, w_ukv, af_w2, ab_w2, wbm, wbp, wbg, w_out):
    W = {}
    for nm, off, n, npad in _IN_GROUPS:
        W["in_" + nm] = _pad_cols(w_in[:, off:off + n], npad)
    uq = w_uq.reshape(MLA_Q_RANK, MLA_HEADS, MLA_NOPE + MLA_ROPE)
    W["qn"] = jnp.pad(uq[:, :, :MLA_NOPE], ((0, 0), (0, 0), (0, LANES - MLA_NOPE))).reshape(MLA_Q_RANK, MLA_HEADS * LANES)
    W["qr"] = jnp.pad(uq[:, :, MLA_NOPE:], ((0, 0), (0, 0), (0, LANES - MLA_ROPE))).reshape(MLA_Q_RANK, MLA_HEADS * LANES)
    W["kv"] = w_ukv
    W["af"] = jnp.pad(af_w2, ((0, LANES - GLA_GATE_RANK), (0, 0)))
    W["ab"] = jnp.pad(ab_w2, ((GLA_GATE_RANK, LANES - 2 * GLA_GATE_RANK), (0, 0)))
    W["bm"], W["bp"], W["bg"], W["out"] = wbm, wbp, wbg, w_out
    return W


def rope_tables(L, n_ctx):
    t = np.arange(L - n_ctx)
    half = MLA_ROPE // 2
    inv = ROPE_BASE ** (-np.arange(0, half, 2, dtype=np.float32) / half)
    ang_r = (t // GRID_W).astype(np.float32)[:, None] * inv
    ang_c = (t % GRID_W).astype(np.float32)[:, None] * inv
    ang = jnp.asarray(np.concatenate([ang_r, ang_r, ang_c, ang_c], axis=-1), F32)
    cos = jnp.ones((L, LANES), F32).at[n_ctx:, :MLA_ROPE].set(jnp.cos(ang))
    sin = jnp.zeros((L, LANES), F32).at[n_ctx:, :MLA_ROPE].set(jnp.sin(ang))
    return cos, sin


def layer_fwd(x2, ms, W, P, cos, sin, B, L, n_ctx, tag, hosted=None):
    nb = L // TOKEN_BLOCK
    r3 = lambda a: a.reshape(B, L, a.shape[-1])
    r2 = lambda a: a.reshape(B * L, a.shape[-1])
    names = [g[0] for g in _IN_GROUPS[:-1]]
    h, zs = norm_in_proj(x2, P["pre"], ms, [W["in_" + n] for n in names], nb, tag + "in_proj")
    z = dict(zip(names, zs))
    (z["m"],) = mm_multi(h, [W["in_m"]], [BF16], tag + "in_proj_merge", tm=_wide_block(L))
    qn, qr, kv, kr = mla_prep_fwd(z["a"], P["qg"], P["kvg"], W["qn"], W["qr"], W["kv"], cos, sin, nb, tag + "mla_prep")
    ya, y_mla, lse, carried = attention_fwd(r3(qn), r3(qr), r3(kv), r3(kr), r3(z["mg"]), n_ctx, tag + "attention", hosted)
    y_pool = pool_fwd(r3(z["px"]), r3(z["pg"]), P["pw"], P["ps"], n_ctx, tag + "pool")
    qf, kf, ksf, tf, qb, kb, ksb, tb = gla_prep_fwd(z["lr"], z["gq"], z["gk"], W["af"], W["ab"], P["baf"], P["bab"], tag + "gla_prep")
    (of, ssf), (ob, ssb) = gla_scan_fwd([(r3(qf), r3(kf), r3(ksf), r3(tf)), (r3(qb), r3(kb), r3(ksb), r3(tb))], r3(z["gv"]), n_ctx, tag + "gla_scan")
    y_gla = gla_out_fwd(r2(of), r2(ob), P["gn"], z["gg"], tag + "gla_out")
    ys = [r2(y_mla), r2(y_pool), y_gla]
    x_new, out, merged = merge_post_fwd(ys, z["m"], [W["bm"], W["bp"], W["bg"]], W["out"], x2, P["post"], ms, nb, tag + "merge_post")
    res = dict(x2=x2, h=h, z=z, qn=qn, qr=qr, kv=kv, kr=kr, ya=ya, lse=lse, ys=ys, gla_f=(qf, kf, ksf, tf, ssf), gla_b=(qb, kb, ksb, tb, ssb),
               of=of, ob=ob, out=out, merged=merged)
    return x_new, res, carried


def layer_bwd(dxn, res, ms, W, P, cos, sin, B, L, n_ctx, tag, hosted=None, host_own=None):
    nb = L // TOKEN_BLOCK
    r3 = lambda a: a.reshape(B, L, a.shape[-1])
    r2 = lambda a: a.reshape(B * L, a.shape[-1])
    z = res["z"]
    ys = res["ys"]
    wbs = [W["bm"], W["bp"], W["bg"]]
    dy0, dy1, dy2, dzm, dout, dp0, dp1, dp2, dgate, dpost = merge_post_bwd(dxn, res["out"], ys, z["m"], wbs, W["out"], P["post"], ms, nb, tag + "merge_post_bwd")
    G = {"out": mm_dw(res["merged"], dout, tag + "dw_out"), "post": dpost}
    for nm, y, dp in zip(("bm", "bp", "bg"), ys, (dp0, dp1, dp2)):
        G[nm] = mm_dw(y, dp, tag + "dw_" + nm)
    g = {n: _natural_grad(G, n) for n in GRADS_EARLY}
    own = {}
    dz = {"m": dzm}
    do, dz["gg"], G["gn"] = gla_out_bwd(r2(res["of"]), r2(res["ob"]), P["gn"], z["gg"], dy2, tag + "gla_out_bwd")
    carrier = host_own(GRADS_EARLY, g) if host_own else None
    *grads, arrived = gla_scan_bwd([(r3(qt), r3(kt), r3(ks), r3(tot), ss) for qt, kt, ks, tot, ss in (res["gla_f"], res["gla_b"])],
                                   r3(z["gv"]), r3(do), n_ctx, tag + "gla_scan_bwd", carrier)
    own.update(zip(GRADS_EARLY, arrived))
    gf = [r2(a) for a in grads[0]]
    gb = [r2(a) for a in grads[1]]
    dz["lr"], dz["gq"], dz["gk"], G["af"], G["ab"], G["baf"], G["bab"], dz["gv"] = gla_prep_bwd(
        z["lr"], z["gq"], z["gk"], W["af"], W["ab"], P["baf"], P["bab"], gf[:3] + gf[4:], gb[:3] + gb[4:], [gf[3], gb[3]], tag + "gla_prep_bwd")
    dpx, dpg, G["pw"], G["ps"] = pool_bwd(r3(z["px"]), r3(z["pg"]), P["pw"], P["ps"], r3(dy1), n_ctx, tag + "pool_bwd")
    dz["px"], dz["pg"] = r2(dpx), r2(dpg)
    dqn, dqr, dkv, dkr, dzmg, got = attention_bwd(r3(res["qn"]), r3(res["qr"]), r3(res["kv"]), r3(res["kr"]), r3(z["mg"]), res["ya"], res["lse"],
                                                  r3(dy0), n_ctx, tag + "attention_bwd", hosted)
    dz["mg"] = r2(dzmg)
    dz["a"], G["qn"], G["qr"], G["kv"], G["qg"], G["kvg"] = mla_prep_bwd(
        r2(dqn), r2(dqr), r2(dkv), r2(dkr), z["a"], P["qg"], P["kvg"], W["qn"], W["qr"], W["kv"], cos, sin, nb, tag + "mla_prep_bwd")
    names = [grp[0] for grp in _IN_GROUPS]
    for n in names:
        G["in_" + n] = mm_dw(res["h"], dz[n], tag + "dw_in_" + n)
    g.update({n: _natural_grad(G, n) for n in GRADS_LATE})
    carrier = host_own(GRADS_LATE, g) if host_own else None
    (dx, dshift, dscale, G["pre"]), arrived = in_proj_norm_bwd([dz[n] for n in names], [W["in_" + n] for n in names], res["x2"], P["pre"], ms, dxn, nb,
                                                               tag + "in_proj_dx", carrier)
    own.update(zip(GRADS_LATE, arrived))
    g.update({n: _natural_grad(G, n) for n in GRADS_REPLICATED})
    dms = jnp.concatenate([dshift, dscale, dgate], axis=-1)
    return dx, g, dms, got, own


GRADS_EARLY = ("w_branch_mla", "w_branch_pool", "w_branch_gla", "w_out")
GRADS_LATE = ("w_in", "mla_w_uq", "mla_w_ukv", "gla_af_w2", "gla_ab_w2")
GRADS_REPLICATED = ("pre_norm", "post_norm", "mla_q_norm", "mla_kv_norm", "pool_w", "pool_scale", "gla_af_b", "gla_ab_b", "gla_norm")
_DIRECT = dict(mla_w_ukv="kv", w_branch_mla="bm", w_branch_pool="bp", w_branch_gla="bg", w_out="out", pool_w="pw")
_ROW = dict(pre_norm="pre", post_norm="post", mla_q_norm="qg", mla_kv_norm="kvg", pool_scale="ps", gla_af_b="baf", gla_ab_b="bab", gla_norm="gn")


def _natural_grad(G, name):
    if name == "w_in":
        parts = {off: G["in_" + nm][:, :n] for nm, off, n, npad in _IN_GROUPS}
        return jnp.concatenate([parts[o] for o in sorted(parts)], axis=1)
    if name == "mla_w_uq":
        gqn = G["qn"].reshape(MLA_Q_RANK, MLA_HEADS, LANES)[:, :, :MLA_NOPE]
        gqr = G["qr"].reshape(MLA_Q_RANK, MLA_HEADS, LANES)[:, :, :MLA_ROPE]
        return jnp.concatenate([gqn, gqr], axis=-1).reshape(MLA_Q_RANK, MLA_HEADS * (MLA_NOPE + MLA_ROPE))
    if name == "gla_af_w2":
        return G["af"][:GLA_GATE_RANK]
    if name == "gla_ab_w2":
        return G["ab"][GLA_GATE_RANK:2 * GLA_GATE_RANK]
    return G[_DIRECT[name]] if name in _DIRECT else G[_ROW[name]][0]


def local_step(x, c, ctx, c_ctx, small, loss_target, depth, layer_full, host_fwd=None, host_bwd=None, host_own=None, join=None):
    B, S, _ = x.shape
    n_ctx = ctx.shape[1]
    L = n_ctx + S
    nb = L // TOKEN_BLOCK
    cos, sin = rope_tables(L, n_ctx)
    x2 = join(ctx, x) if join else jnp.concatenate([ctx, x], axis=1).reshape(B * L, D_MODEL)
    a8 = jnp.zeros((8, D_MODEL), F32).at[:B].set(c).at[B].set(c_ctx)
    Ws, Ps, mss, ress, mod_ws = [], [], [], [], []
    carried = None
    for l in range(depth):
        tag = f"l{l}_"
        full = layer_full(l, carried)
        W = layer_weights(full["w_in"], full["mla_w_uq"], full["mla_w_ukv"], full["gla_af_w2"], full["gla_ab_w2"],
                          full["w_branch_mla"], full["w_branch_pool"], full["w_branch_gla"], full["w_out"])
        P = dict(pre=small["pre_norm"][l][None], post=small["post_norm"][l][None], qg=small["mla_q_norm"][l][None], kvg=small["mla_kv_norm"][l][None],
                 pw=small["pool_w"][l].astype(BF16), ps=small["pool_scale"][l][None], baf=small["gla_af_b"][l][None], bab=small["gla_ab_b"][l][None],
                 gn=small["gla_norm"][l][None])
        mod8 = mod_fwd(a8, full["mod_w"], small["mod_b"][l][None], tag + "mod")
        ms = jnp.stack([jnp.broadcast_to(mod8[B], (B, 3 * D_MODEL)), mod8[:B]], axis=1).reshape(2 * B, 1, 3 * D_MODEL)
        x2, res, carried = layer_fwd(x2, ms, W, P, cos, sin, B, L, n_ctx, tag, host_fwd(l) if host_fwd else None)
        Ws.append(W), Ps.append(P), mss.append(ms), ress.append(res), mod_ws.append(full["mod_w"])
    dx, loss = loss_head(x2, loss_target.reshape(B * S, D_MODEL), nb, "loss_head")
    grads = [None] * depth
    delivered = [None] * depth
    dz8s = [None] * depth
    da8 = None
    for l in reversed(range(depth)):
        tag = f"l{l}_"
        hosted = host_bwd(l, grads[l + 1]) if host_bwd and l + 1 < depth else None
        dx, g, dms, got, own = layer_bwd(dx, ress[l], mss[l], Ws[l], Ps[l], cos, sin, B, L, n_ctx, tag, hosted, host_own(l) if host_own else None)
        if hosted:
            delivered[l + 1] = got
        if own:
            delivered[l] = own
        dms = dms.reshape(B, 2, 3 * D_MODEL)
        dz8s[l] = jnp.zeros((8, 3 * D_MODEL), F32).at[:B].set(dms[:, 1]).at[B].set(jnp.sum(dms[:, 0], axis=0))
        g_mod_b, da = mod_bwd(a8, mod_ws[l], dz8s[l], tag + "mod_bwd")
        da8 = da if da8 is None else da8 + da
        g["mod_b"] = g_mod_b[0]
        grads[l] = g
    grad_x = dx.reshape(B, L, D_MODEL)[:, n_ctx:]
    return loss, grad_x, grads, da8[B], delivered, (a8, dz8s)


_MESH_ID = pl.DeviceIdType.MESH
_HBM = pl.BlockSpec(memory_space=pltpu.HBM)


def _me_and_peers():
    mx, my, mc = lax.axis_index("x"), lax.axis_index("y"), lax.axis_index("c")
    peers = []
    for k in range(1, N_DEV):
        px, py, pc = mx ^ ((k >> 2) & 1), my ^ ((k >> 1) & 1), mc ^ (k & 1)
        peers.append(((px, py, pc), 4 * px + 2 * py + pc))
    return 4 * mx + 2 * my + mc, peers


def _comm_scratch(n):
    return [pltpu.SemaphoreType.DMA((n * (N_DEV - 1),)), pltpu.SemaphoreType.DMA((n * (N_DEV - 1),)), pltpu.SemaphoreType.DMA((n,))]


class _Gather:
    def __init__(self, x_refs, o_refs, send_sems, recv_sems, local_sems):
        self.x, self.o, self.send, self.recv, self.local = x_refs, o_refs, send_sems, recv_sems, local_sems
        self.n = len(x_refs)
        mx, my, mc = lax.axis_index("x"), lax.axis_index("y"), lax.axis_index("c")
        self.me, self.sibling, self.mc = (mx, my, mc), (mx, my, 1 - mc), mc
        self.chips = [(1 - mx, my), (mx, 1 - my), (1 - mx, 1 - my)]

    @staticmethod
    def out_shape(xs):
        return [jax.ShapeDtypeStruct((N_DEV,) + x.shape, x.dtype) for x in xs]

    def _copy(self, i, k, block, to, src=None):
        px, py, pc = block
        dst = self.o[i].at[4 * px + 2 * py + pc]
        sem = (N_DEV - 1) * i + k
        return pltpu.make_async_remote_copy(src_ref=dst if src is None else src, dst_ref=dst, send_sem=self.send.at[sem],
                                            recv_sem=self.recv.at[sem], device_id=to, device_id_type=_MESH_ID)

    def _mine(self, i):
        mx, my, mc = self.me
        return pltpu.make_async_copy(self.x[i], self.o[i].at[4 * mx + 2 * my + mc], self.local.at[i])

    def _first(self):
        out = []
        for i in range(self.n):
            out.append(self._copy(i, 0, self.me, self.sibling, src=self.x[i]))
            out += [self._copy(i, 1 + j, self.me, (*chip, self.mc), src=self.x[i]) for j, chip in enumerate(self.chips)]
        return out

    def _passed(self, j, i):
        return self._copy(i, 4 + j, (*self.chips[j], self.mc), self.sibling)

    def start(self):
        for i in range(self.n):
            self._mine(i).start()
        for cp in self._first():
            cp.start()

    def forward(self):
        for j, chip in enumerate(self.chips):
            for i in range(self.n):
                self._copy(i, 1 + j, (*chip, self.mc), self.me).wait_recv()
                self._passed(j, i).start()

    def finish(self):
        for i in range(self.n):
            self._copy(i, 0, self.sibling, self.me).wait_recv()
            for j, chip in enumerate(self.chips):
                self._copy(i, 4 + j, (*chip, 1 - self.mc), self.me).wait_recv()
        for cp in self._first():
            cp.wait_send()
        for j in range(len(self.chips)):
            for i in range(self.n):
                self._passed(j, i).wait_send()
        for i in range(self.n):
            self._mine(i).wait()


class _Scatter:
    def __init__(self, x_refs, o_refs, send_sems, recv_sems, local_sems):
        self.x, self.o, self.send, self.recv, self.local = x_refs, o_refs, send_sems, recv_sems, local_sems
        self.n = len(x_refs)
        self.me, self.peers = _me_and_peers()

    @staticmethod
    def out_shape(xs):
        return [jax.ShapeDtypeStruct(x.shape, x.dtype) for x in xs]

    def _copy(self, i, k, src_slot, dst_slot, to):
        sem = (N_DEV - 1) * i + k
        return pltpu.make_async_remote_copy(src_ref=self.x[i].at[src_slot], dst_ref=self.o[i].at[dst_slot], send_sem=self.send.at[sem],
                                            recv_sem=self.recv.at[sem], device_id=to, device_id_type=_MESH_ID)

    def _mine(self, i):
        return pltpu.make_async_copy(self.x[i].at[self.me], self.o[i].at[self.me], self.local.at[i])

    def _sends(self):
        return [self._copy(i, k, slot, self.me, peer) for k, (peer, slot) in enumerate(self.peers) for i in range(self.n)]

    def start(self):
        for i in range(self.n):
            self._mine(i).start()
        for cp in self._sends():
            cp.start()

    def forward(self):
        pass

    def finish(self):
        for k, (peer, slot) in enumerate(self.peers):
            for i in range(self.n):
                self._copy(i, k, slot, slot, peer).wait_recv()
        for cp in self._sends():
            cp.wait_send()
        for i in range(self.n):
            self._mine(i).wait()


class _Hosted:
    def __init__(self, kind, xs, late=False):
        self.kind, self.xs, self.n, self.late = kind, list(xs), len(xs), late
        self.in_specs = [_HBM] * self.n
        self.out_specs = [_HBM] * self.n
        self.out_shape = kind.out_shape(self.xs)
        self.scratch = _comm_scratch(self.n)

    def run(self, x_refs, o_refs, sems, step, total):
        for when, phase in ((0, "start"), (total - 2 if self.late else 3 * total // 4, "forward"), (total - 1, "finish")):
            @pl.when(step == when)
            def _(phase=phase):
                getattr(self.kind(x_refs, o_refs, *sems), phase)()


def gather_blocks(xs, name):
    n = len(xs)

    def body(*refs):
        g = _Gather(refs[:n], refs[n:2 * n], *refs[2 * n:])
        g.start()
        g.forward()
        g.finish()

    return pl.pallas_call(
        body, name=name, in_specs=[_HBM] * n, out_specs=[_HBM] * n,
        out_shape=_Gather.out_shape(xs), scratch_shapes=_comm_scratch(n),
    )(*xs)


def join_tokens(ctx, x, hosted, name):
    B, n_ctx, _ = ctx.shape
    S = x.shape[1]
    assert n_ctx == TOKEN_BLOCK
    nb = (n_ctx + S) // TOKEN_BLOCK
    nx = hosted.n

    def body(*refs):
        c_ref, x_ref, o_ref = refs[0], refs[1], refs[2 + nx]
        i = pl.program_id(0)
        hosted.run(refs[2:2 + nx], refs[3 + nx:3 + 2 * nx], refs[3 + 2 * nx:], i, B * nb)

        @pl.when(i % nb == 0)
        def _():
            o_ref[...] = c_ref[...]

        @pl.when(i % nb > 0)
        def _():
            o_ref[...] = x_ref[...]

    blk = (TOKEN_BLOCK, D_MODEL)
    outs = pl.pallas_call(
        body, name=name, grid=(B * nb,),
        in_specs=[pl.BlockSpec(blk, lambda i: (i // nb, 0)), pl.BlockSpec(blk, lambda i: ((i // nb) * (nb - 1) + jnp.maximum(i % nb - 1, 0), 0))]
        + hosted.in_specs,
        out_specs=[pl.BlockSpec(blk, lambda i: (i, 0))] + hosted.out_specs,
        out_shape=[jax.ShapeDtypeStruct((B * nb * TOKEN_BLOCK, D_MODEL), F32)] + hosted.out_shape,
        scratch_shapes=hosted.scratch, compiler_params=_params(("arbitrary",)),
    )(ctx.reshape(B * n_ctx, D_MODEL), x.reshape(B * S, D_MODEL), *hosted.xs)
    return outs[0], list(outs[1:])


def reduce_adamw(slots, w, m, v, name, tr=256):
    R, C = w.shape
    nl = len(slots)
    ns = slots[0].shape[0]
    rows = R // nl
    tr = min(tr, rows)
    nbl = rows // tr
    c1 = 1.0 / (1.0 - ADAM_B1 ** ADAM_STEP)
    c2 = 1.0 / (1.0 - ADAM_B2 ** ADAM_STEP)

    def body(*refs):
        w_ref, m_ref, v_ref, g_ref, d_ref, nm_ref, nv_ref = refs[nl:]
        part = pl.program_id(0) // nbl
        g = None
        for l, s_ref in enumerate(refs[:nl]):
            gl = s_ref[0].astype(F32)
            for s in range(1, ns):
                gl = gl + s_ref[s].astype(F32)
            g = gl if g is None else jnp.where(part == l, gl, g)
        nm = ADAM_B1 * m_ref[...] + (1.0 - ADAM_B1) * g
        nv = ADAM_B2 * v_ref[...] + (1.0 - ADAM_B2) * (g * g)
        g_ref[...] = g
        nm_ref[...] = nm
        nv_ref[...] = nv
        d_ref[...] = -ADAM_LR * ((nm * c1) / (jnp.sqrt(nv * c2) + ADAM_EPS) + ADAM_WD * w_ref[...])

    blk = pl.BlockSpec((tr, C), lambda i: (i, 0))
    sspecs = [pl.BlockSpec((ns, tr, C), lambda i, l=l: (0, jnp.clip(i - l * nbl, 0, nbl - 1), 0)) for l in range(nl)]
    return pl.pallas_call(
        body, name=name, grid=(R // tr,),
        in_specs=sspecs + [blk, blk, blk], out_specs=[blk] * 4,
        out_shape=[jax.ShapeDtypeStruct((R, C), F32)] * 4,
        compiler_params=_params(("parallel",), VMEM_LIMIT),
    )(*slots, w, m, v)


ARG_WEIGHTS = ("c_ctx", "mod_w", "mod_b", "pre_norm", "post_norm", "w_in", "mla_q_norm", "mla_w_uq", "mla_kv_norm", "mla_w_ukv", "pool_w",
               "pool_scale", "gla_af_w2", "gla_af_b", "gla_ab_w2", "gla_ab_b", "gla_norm", "w_branch_mla", "w_branch_pool", "w_branch_gla", "w_out")
SHARDED = ("mod_w", "w_in", "mla_w_uq", "mla_w_ukv", "gla_af_w2", "gla_ab_w2", "w_branch_mla", "w_branch_pool", "w_branch_gla", "w_out")
ROW_SHARDED = ("w_out",)
REPLICATED = tuple(n for n in ARG_WEIGHTS if n not in SHARDED)
PACK_ROWS = 512


def _pack(parts, dtype):
    flat = jnp.concatenate([p.astype(dtype).reshape(-1) for p in parts])
    n = flat.shape[0]
    total = -(-n // (PACK_ROWS * LANES)) * (PACK_ROWS * LANES)
    return jnp.pad(flat, (0, total - n)).reshape(total // LANES, LANES)


def _unpack(buf, shapes):
    flat = buf.reshape(-1)
    out, off = [], 0
    for shp in shapes:
        n = math.prod(shp)
        out.append(flat[off:off + n].reshape(shp))
        off += n
    return out


def _gathered_to_full(g, name):
    _, r, cs = g.shape
    if name in ROW_SHARDED:
        return g.reshape(N_DEV * r, cs)
    return g.transpose(1, 0, 2).reshape(r, N_DEV * cs)


def _full_to_slots(w, name):
    if name in ROW_SHARDED:
        return w.reshape(N_DEV, w.shape[0] // N_DEV, w.shape[1])
    return w.reshape(w.shape[0], N_DEV, w.shape[1] // N_DEV).transpose(1, 0, 2)


def kernel(x, c, ctx, c_ctx, mod_w, mod_b, pre_norm, post_norm, w_in, mla_q_norm, mla_w_uq, mla_kv_norm, mla_w_ukv, pool_w, pool_scale, gla_af_w2, gla_af_b, gla_ab_w2, gla_ab_b, gla_norm, w_branch_mla, w_branch_pool, w_branch_gla, w_out, loss_target, m_c_ctx, m_mod_w, m_mod_b, m_pre_norm, m_post_norm, m_w_in, m_mla_q_norm, m_mla_w_uq, m_mla_kv_norm, m_mla_w_ukv, m_pool_w, m_pool_scale, m_gla_af_w2, m_gla_af_b, m_gla_ab_w2, m_gla_ab_b, m_gla_norm, m_w_branch_mla, m_w_branch_pool, m_w_branch_gla, m_w_out, v_c_ctx, v_mod_w, v_mod_b, v_pre_norm, v_post_norm, v_w_in, v_mla_q_norm, v_mla_w_uq, v_mla_kv_norm, v_mla_w_ukv, v_pool_w, v_pool_scale, v_gla_af_w2, v_gla_af_b, v_gla_ab_w2, v_gla_ab_b, v_gla_norm, v_w_branch_mla, v_w_branch_pool, v_w_branch_gla, v_w_out):
    local = dict(locals())
    wts = {n: local[n] for n in ARG_WEIGHTS}
    mom1 = {n: local["m_" + n] for n in ARG_WEIGHTS}
    mom2 = {n: local["v_" + n] for n in ARG_WEIGHTS}
    shard_shapes = [wts[n].shape for n in SHARDED]
    rep_shapes = [wts[n].shape for n in REPLICATED]
    kinds = ("grad", "delta", "new_m", "new_v")

    depth = w_in.shape[0]

    def shards(l):
        return [wts[n][l].astype(BF16) for n in SHARDED]

    first = []

    def join(ctx_, x_):
        x2, got = join_tokens(ctx_, x_, _Hosted(_Gather, shards(0), late=True), "join_tokens")
        first.extend(got)
        return x2

    def layer_full(l, carried):
        return {n: _gathered_to_full(gw, n) for n, gw in zip(SHARDED, first if l == 0 else carried)}

    def host_fwd(l):
        return _Hosted(_Gather, shards(l + 1)) if l + 1 < depth else None

    exchanged = GRADS_EARLY + GRADS_LATE

    def slots(g, names):
        return [_full_to_slots(g[n], n).astype(BF16) for n in names]

    def host_bwd(l, g_above):
        return _Hosted(_Scatter, slots(g_above, exchanged))

    def host_own(l):
        return (lambda names, g: _Hosted(_Scatter, slots(g, names))) if l == 0 else None

    small = {n: wts[n] for n in REPLICATED}
    loss, grad_x, grads, g_c_ctx, arrived, (a8, dz8s) = local_step(x, c, ctx, c_ctx, small, loss_target, depth, layer_full, host_fwd, host_bwd, host_own, join)
    arrived = [a if isinstance(a, dict) else dict(zip(exchanged, a)) for a in arrived]

    g = {n: (g_c_ctx if n == "c_ctx" else jnp.stack([grads[l][n] for l in range(depth)])) for n in REPLICATED}
    gathered, a_all, dz_all = gather_blocks([_pack([g[n] for n in REPLICATED], BF16), a8, jnp.concatenate(dz8s, axis=0)], "gather_small_grads")
    outs = reduce_adamw([gathered], _pack([wts[n] for n in REPLICATED], F32), _pack([mom1[n] for n in REPLICATED], F32),
                        _pack([mom2[n] for n in REPLICATED], F32), "adamw_replicated")
    me = 4 * lax.axis_index("x") + 2 * lax.axis_index("y") + lax.axis_index("c")
    ncol = mod_w.shape[2]
    dz_cols = lax.dynamic_slice_in_dim(dz_all.reshape(N_DEV, depth, 8, 3 * D_MODEL), me * ncol, ncol, axis=3)
    g_mod_w = mod_dw_columns(a_all.reshape(N_DEV * 8, D_MODEL), dz_cols.transpose(1, 0, 2, 3).reshape(depth, N_DEV * 8, ncol), "mod_dw")

    res = {kind: {} for kind in kinds}
    for n, shp in zip(SHARDED, shard_shapes):
        flat = (shp[0] * shp[1], shp[2])
        parts = [g_mod_w.reshape((1,) + flat)] if n == "mod_w" else [arrived[l][n] for l in range(depth)]
        for kind, o in zip(kinds, reduce_adamw(parts, wts[n].reshape(flat), mom1[n].reshape(flat), mom2[n].reshape(flat), "adamw_" + n)):
            res[kind][n] = o.reshape(shp)
    for kind, o in zip(("grad", "delta", "new_m", "new_v"), outs):
        res[kind].update(zip(REPLICATED, _unpack(o, rep_shapes)))

    loss = lax.psum(loss[0, 0], ("x", "y", "c"))
    return (loss, grad_x, *[res[kind][n] for kind in ("grad", "delta", "new_m", "new_v") for n in ARG_WEIGHTS])
```

```python
import functools
import math

import jax
import jax.numpy as jnp
import numpy as np
from jax import lax
from jax.experimental import pallas as pl
from jax.experimental.pallas import tpu as pltpu

F32 = jnp.float32
BF16 = jnp.bfloat16

D_MODEL = 1024
NORM_EPS = 1e-6
GRID_W = 64
MLA_HEADS, MLA_Q_RANK, MLA_KV_RANK, MLA_NOPE, MLA_ROPE, MLA_V = 8, 256, 128, 64, 32, 64
MLA_WIDTH = MLA_HEADS * MLA_V
ROPE_BASE = 10000.0
ATT_SCALE = (MLA_NOPE + MLA_ROPE) ** -0.5
POOL_WINDOWS = (2, 4, 8, 16)
POOL_WIDTH, POOL_GROUP = 512, 128
GLA_HEADS, GLA_DK, GLA_DV = 4, 64, 128
GLA_KW, GLA_WIDTH = GLA_HEADS * GLA_DK, GLA_HEADS * GLA_DV
GLA_GATE_RANK, GLA_TAU, GLA_CHUNK = 16, 16.0, 64
IN_SIZES = (256, 128, 32, 512, 512, 512, 256, 256, 512, 16, 16, 512, 3 * D_MODEL)
ADAM_LR, ADAM_B1, ADAM_B2, ADAM_EPS, ADAM_WD, ADAM_STEP = 0.001, 0.9, 0.999, 1e-08, 0.01, 10
N_DEV = 8

LANES = 128
TOKEN_BLOCK = 256
WIDE_BLOCKS = (1152, 768)
VMEM_LIMIT = 48 * 1024 * 1024
NEG_BIG = -1e30

_NT = (((1,), (1,)), ((), ()))
_TN = (((0,), (0,)), ((), ()))


def _dot(a, b):
    return jnp.dot(a, b, preferred_element_type=F32)


def _dot_nt(a, b):
    return lax.dot_general(a, b, _NT, preferred_element_type=F32)


def _dot_tn(a, b):
    return lax.dot_general(a, b, _TN, preferred_element_type=F32)


def _params(sem=None, vmem=None):
    kw = {}
    if sem is not None:
        kw["dimension_semantics"] = sem
    if vmem is not None:
        kw["vmem_limit_bytes"] = vmem
    return pltpu.CompilerParams(**kw)


def _wide_block(rows):
    return next(t for t in WIDE_BLOCKS + (TOKEN_BLOCK,) if rows % t == 0)


def _full(shape):
    n = len(shape)
    return pl.BlockSpec(shape, lambda *_: (0,) * n)


def _sig(x):
    return 1.0 / (1.0 + jnp.exp(-x))


def _silu_and_grad(x):
    s = _sig(x)
    return x * s, s * (1.0 + x * (1.0 - s))


def _acc(ref, val, first):
    @pl.when(first)
    def _():
        ref[...] = val

    @pl.when(jnp.logical_not(first))
    def _():
        ref[...] += val


def mm_multi(a, ws, dtypes, name, tm=TOKEN_BLOCK):
    M, K = a.shape
    nw = len(ws)

    def body(a_ref, *refs):
        av = a_ref[...]
        for w_ref, o_ref in zip(refs[:nw], refs[nw:]):
            o_ref[...] = _dot(av, w_ref[...]).astype(o_ref.dtype)

    return pl.pallas_call(
        body, name=name, grid=(M // tm,),
        in_specs=[pl.BlockSpec((tm, K), lambda i: (i, 0))] + [_full(w.shape) for w in ws],
        out_specs=[pl.BlockSpec((tm, w.shape[1]), lambda i: (i, 0)) for w in ws],
        out_shape=[jax.ShapeDtypeStruct((M, w.shape[1]), dt) for w, dt in zip(ws, dtypes)],
        compiler_params=_params(("parallel",), VMEM_LIMIT),
    )(a, *ws)


def mm_dw(a, dz, name, tn=1024):
    M, K = a.shape
    n = dz.shape[1]
    tn = min(tn, n)
    tk = next(t for t in (3072, 1536, 1024, 512, TOKEN_BLOCK) if M % t == 0)

    def body(a_ref, dz_ref, o_ref):
        _acc(o_ref, _dot_tn(a_ref[...], dz_ref[...]), pl.program_id(1) == 0)

    return pl.pallas_call(
        body, name=name, grid=(n // tn, M // tk),
        in_specs=[pl.BlockSpec((tk, K), lambda j, k: (k, 0)), pl.BlockSpec((tk, tn), lambda j, k: (k, j))],
        out_specs=pl.BlockSpec((K, tn), lambda j, k: (0, j)),
        out_shape=jax.ShapeDtypeStruct((K, n), F32),
        compiler_params=_params(("parallel", "arbitrary"), VMEM_LIMIT),
    )(a, dz)


def mod_fwd(a8, w, b, name):
    tn = D_MODEL

    def body(a_ref, w_ref, b_ref, o_ref):
        a = a_ref[...]
        o_ref[...] = _dot((a * _sig(a)).astype(BF16), w_ref[...]) + b_ref[...]

    return pl.pallas_call(
        body, name=name, grid=(3,),
        in_specs=[_full(a8.shape), pl.BlockSpec((D_MODEL, tn), lambda j: (0, j)), pl.BlockSpec((1, tn), lambda j: (0, j))],
        out_specs=pl.BlockSpec((8, tn), lambda j: (0, j)),
        out_shape=jax.ShapeDtypeStruct((8, 3 * D_MODEL), F32),
        compiler_params=_params(("parallel",)),
    )(a8, w, b)


def mod_bwd(a8, w, dz8, name):
    tn = D_MODEL

    def body(a_ref, w_ref, dz_ref, db_ref, da_ref):
        a = a_ref[...]
        _, dsa = _silu_and_grad(a)
        dz = dz_ref[...]
        db_ref[...] = jnp.sum(dz, axis=0, keepdims=True)
        _acc(da_ref, _dot_nt(dz.astype(BF16), w_ref[...]) * dsa, pl.program_id(0) == 0)

    return pl.pallas_call(
        body, name=name, grid=(3,),
        in_specs=[_full(a8.shape), pl.BlockSpec((D_MODEL, tn), lambda j: (0, j)), pl.BlockSpec((8, tn), lambda j: (0, j))],
        out_specs=[pl.BlockSpec((1, tn), lambda j: (0, j)), _full((8, D_MODEL))],
        out_shape=[jax.ShapeDtypeStruct((1, 3 * D_MODEL), F32), jax.ShapeDtypeStruct((8, D_MODEL), F32)],
        compiler_params=_params(("arbitrary",)),
    )(a8, w, dz8)


def mod_dw_columns(a_all, dz_cols, name):
    depth, R, n = dz_cols.shape

    def body(a_ref, dz_ref, dw_ref):
        a = a_ref[...]
        dw_ref[0] = _dot_tn((a * _sig(a)).astype(BF16), dz_ref[0].astype(BF16))

    return pl.pallas_call(
        body, name=name, grid=(depth,),
        in_specs=[_full(a_all.shape), pl.BlockSpec((1, R, n), lambda l: (l, 0, 0))],
        out_specs=pl.BlockSpec((1, D_MODEL, n), lambda l: (l, 0, 0)),
        out_shape=jax.ShapeDtypeStruct((depth, D_MODEL, n), F32),
        compiler_params=_params(("parallel",)),
    )(a_all, dz_cols)


def _mod_row(nb):
    return lambda i: 2 * (i // nb) + jnp.minimum(i % nb, 1)


def _mod_spec(nb, part):
    row = _mod_row(nb)
    return pl.BlockSpec((1, 1, D_MODEL), lambda i: (row(i), 0, part))


def norm_in_proj(x2, g, ms, ws, nb, name):
    T = x2.shape[0]
    nw = len(ws)

    def body(x_ref, g_ref, sh_ref, sc_ref, *refs):
        x = x_ref[...]
        r = lax.rsqrt(jnp.mean(x * x, axis=-1, keepdims=True) + NORM_EPS)
        h = ((x * r) * g_ref[...] * (1.0 + sc_ref[0]) + sh_ref[0]).astype(BF16)
        refs[nw][...] = h
        for w_ref, o_ref in zip(refs[:nw], refs[nw + 1:]):
            o_ref[...] = _dot(h, w_ref[...])

    tok = lambda n: pl.BlockSpec((TOKEN_BLOCK, n), lambda i: (i, 0))
    outs = pl.pallas_call(
        body, name=name, grid=(T // TOKEN_BLOCK,),
        in_specs=[tok(D_MODEL), _full((1, D_MODEL)), _mod_spec(nb, 0), _mod_spec(nb, 1)] + [_full(w.shape) for w in ws],
        out_specs=[tok(D_MODEL)] + [tok(w.shape[1]) for w in ws],
        out_shape=[jax.ShapeDtypeStruct((T, D_MODEL), BF16)] + [jax.ShapeDtypeStruct((T, w.shape[1]), F32) for w in ws],
        compiler_params=_params(("parallel",), VMEM_LIMIT),
    )(x2, g, ms, ms, *ws)
    return outs[0], outs[1:]


def in_proj_norm_bwd(dzs, ws, x2, g, ms, dxres, nb, name, hosted=None):
    T = x2.shape[0]
    nw = len(ws)
    nx = hosted.n if hosted else 0
    nrow = ms.shape[0]
    row = _mod_row(nb)
    n_in = 2 * nw + 4

    def body(*refs):
        x_ref, g_ref, sc_ref, dxr_ref = refs[2 * nw:n_in]
        dx_ref, dsh_ref, dsc_ref, dg_ref = refs[n_in + nx:n_in + nx + 4]
        i = pl.program_id(0)
        if hosted:
            hosted.run(refs[n_in:n_in + nx], refs[n_in + nx + 4:n_in + 2 * nx + 4], refs[n_in + 2 * nx + 4:], i, T // TOKEN_BLOCK)
        dh = None
        for dz_ref, w_ref in zip(refs[:nw], refs[nw:2 * nw]):
            t = _dot_nt(dz_ref[...], w_ref[...])
            dh = t if dh is None else dh + t
        x = x_ref[...]
        g = g_ref[...]
        r = lax.rsqrt(jnp.mean(x * x, axis=-1, keepdims=True) + NORM_EPS)
        xn = x * r
        du = dh * (1.0 + sc_ref[0])
        dyg = du * g
        dx_ref[...] = dxr_ref[...] + r * (dyg - xn * jnp.mean(dyg * xn, axis=-1, keepdims=True))
        first = (i % nb) <= 1
        _acc(dsh_ref.at[0], jnp.sum(dh, axis=0, keepdims=True), first)
        _acc(dsc_ref.at[0], jnp.sum(dh * xn * g, axis=0, keepdims=True), first)
        _acc(dg_ref, jnp.sum(du * xn, axis=0, keepdims=True), i == 0)

    tok = lambda n: pl.BlockSpec((TOKEN_BLOCK, n), lambda i: (i, 0))
    acc = pl.BlockSpec((1, 1, D_MODEL), lambda i: (row(i), 0, 0))
    outs = pl.pallas_call(
        body, name=name, grid=(T // TOKEN_BLOCK,),
        in_specs=[tok(dz.shape[1]) for dz in dzs] + [_full(w.shape) for w in ws] + [tok(D_MODEL), _full((1, D_MODEL)), _mod_spec(nb, 1), tok(D_MODEL)]
        + (hosted.in_specs if hosted else []),
        out_specs=[tok(D_MODEL), acc, acc, _full((1, D_MODEL))] + (hosted.out_specs if hosted else []),
        out_shape=[jax.ShapeDtypeStruct((T, D_MODEL), F32), jax.ShapeDtypeStruct((nrow, 1, D_MODEL), F32),
                   jax.ShapeDtypeStruct((nrow, 1, D_MODEL), F32), jax.ShapeDtypeStruct((1, D_MODEL), F32)] + (hosted.out_shape if hosted else []),
        scratch_shapes=hosted.scratch if hosted else [],
        compiler_params=_params(("arbitrary",), VMEM_LIMIT),
    )(*dzs, *ws, x2, g, ms, dxres, *(hosted.xs if hosted else []))
    return outs[:4], list(outs[4:])


def _rot(x):
    lane = lax.broadcasted_iota(jnp.int32, x.shape, 1)
    return jnp.where((lane % 16) < 8, -pltpu.roll(x, LANES - 8, 1), pltpu.roll(x, 8, 1))


def _rope(x, cos, sin):
    return x * cos + _rot(x) * sin


def _rope_t(dy, cos, sin):
    return dy * cos - _rot(dy * sin)


def _rms_rows(x):
    r = lax.rsqrt(jnp.mean(x * x, axis=-1, keepdims=True) + NORM_EPS)
    return x * r, r


def _rms_rows_bwd(dyg, xn, r):
    return r * (dyg - xn * jnp.mean(dyg * xn, axis=-1, keepdims=True))


def mla_prep_fwd(za, qg, kvg, wqn, wqr, wkv, cos, sin, nb, name):
    T = za.shape[0]
    W = MLA_HEADS * LANES

    def body(z_ref, qg_ref, kvg_ref, wqn_ref, wqr_ref, wkv_ref, cos_ref, sin_ref, qn_ref, qr_ref, kv_ref, kr_ref):
        z = z_ref[...]
        cos = cos_ref[...]
        sin = sin_ref[...]
        xq, _ = _rms_rows(z[:, 0:256])
        qn = (xq * qg_ref[...]).astype(BF16)
        qn_ref[...] = (_dot(qn, wqn_ref[...]) * ATT_SCALE).astype(BF16)
        qr = _dot(qn, wqr_ref[...])
        for h in range(MLA_HEADS):
            sl = slice(LANES * h, LANES * (h + 1))
            qr_ref[:, sl] = (_rope(qr[:, sl], cos, sin) * ATT_SCALE).astype(BF16)
        xkv, _ = _rms_rows(z[:, 256:384])
        kv_ref[...] = _dot((xkv * kvg_ref[...]).astype(BF16), wkv_ref[...]).astype(BF16)
        kr_ref[...] = _rope(z[:, 384:512], cos, sin).astype(BF16)

    tb = _wide_block(nb * TOKEN_BLOCK)
    npos = nb * TOKEN_BLOCK // tb
    tok = lambda n: pl.BlockSpec((tb, n), lambda i: (i, 0))
    pos = pl.BlockSpec((tb, LANES), lambda i: (i % npos, 0))
    return pl.pallas_call(
        body, name=name, grid=(T // tb,),
        in_specs=[tok(512), _full(qg.shape), _full(kvg.shape), _full(wqn.shape), _full(wqr.shape), _full(wkv.shape), pos, pos],
        out_specs=[tok(W), tok(W), tok(W), tok(LANES)],
        out_shape=[jax.ShapeDtypeStruct((T, W), BF16)] * 3 + [jax.ShapeDtypeStruct((T, LANES), BF16)],
        compiler_params=_params(("parallel",)),
    )(za, qg, kvg, wqn, wqr, wkv, cos, sin)


def mla_prep_bwd(dqn, dqr, dkv, dkr, za, qg, kvg, wqn, wqr, wkv, cos, sin, nb, name):
    T = za.shape[0]
    W = MLA_HEADS * LANES

    def body(dqn_ref, dqr_ref, dkv_ref, dkr_ref, z_ref, qg_ref, kvg_ref, wqn_ref, wqr_ref, wkv_ref, cos_ref, sin_ref,
             dz_ref, dwqn_ref, dwqr_ref, dwkv_ref, dqg_ref, dkvg_ref):
        first = pl.program_id(0) == 0
        z = z_ref[...]
        cos = cos_ref[...]
        sin = sin_ref[...]
        qg = qg_ref[...]
        kvg = kvg_ref[...]
        xq, rq = _rms_rows(z[:, 0:256])
        qn = (xq * qg).astype(BF16)
        a1 = (dqn_ref[...].astype(F32) * ATT_SCALE).astype(BF16)
        parts = []
        for h in range(MLA_HEADS):
            sl = slice(LANES * h, LANES * (h + 1))
            parts.append(_rope_t(dqr_ref[:, sl].astype(F32) * ATT_SCALE, cos, sin).astype(BF16))
        a2 = jnp.concatenate(parts, axis=1)
        dq = _dot_nt(a1, wqn_ref[...]) + _dot_nt(a2, wqr_ref[...])
        _acc(dwqn_ref, _dot_tn(qn, a1), first)
        _acc(dwqr_ref, _dot_tn(qn, a2), first)
        _acc(dqg_ref, jnp.sum(dq * xq, axis=0, keepdims=True), first)
        dz_ref[:, 0:256] = _rms_rows_bwd(dq * qg, xq, rq).astype(BF16)
        xkv, rkv = _rms_rows(z[:, 256:384])
        kvn = (xkv * kvg).astype(BF16)
        dkvb = dkv_ref[...].astype(BF16)
        dk = _dot_nt(dkvb, wkv_ref[...])
        _acc(dwkv_ref, _dot_tn(kvn, dkvb), first)
        _acc(dkvg_ref, jnp.sum(dk * xkv, axis=0, keepdims=True), first)
        dz_ref[:, 256:384] = _rms_rows_bwd(dk * kvg, xkv, rkv).astype(BF16)
        dz_ref[:, 384:512] = _rope_t(dkr_ref[...], cos, sin).astype(BF16)

    tb = _wide_block(nb * TOKEN_BLOCK)
    npos = nb * TOKEN_BLOCK // tb
    tok = lambda n: pl.BlockSpec((tb, n), lambda i: (i, 0))
    pos = pl.BlockSpec((tb, LANES), lambda i: (i % npos, 0))
    return pl.pallas_call(
        body, name=name, grid=(T // tb,),
        in_specs=[tok(W), tok(W), tok(W), tok(LANES), tok(512), _full(qg.shape), _full(kvg.shape), _full(wqn.shape),
                  _full(wqr.shape), _full(wkv.shape), pos, pos],
        out_specs=[tok(512), _full(wqn.shape), _full(wqr.shape), _full(wkv.shape), _full(qg.shape), _full(kvg.shape)],
        out_shape=[jax.ShapeDtypeStruct((T, 512), BF16), jax.ShapeDtypeStruct(wqn.shape, F32), jax.ShapeDtypeStruct(wqr.shape, F32),
                   jax.ShapeDtypeStruct(wkv.shape, F32), jax.ShapeDtypeStruct(qg.shape, F32), jax.ShapeDtypeStruct(kvg.shape, F32)],
        compiler_params=_params(("arbitrary",)),
    )(dqn, dqr, dkv, dkr, za, qg, kvg, wqn, wqr, wkv, cos, sin)


def _att_qk(qn_ref, qr_ref, kv_ref, kr, j):
    sl = slice(LANES * j, LANES * (j + 1))
    q = jnp.concatenate([qn_ref[0, :, sl], qr_ref[0, :, sl]], axis=1)
    kvj = kv_ref[0, :, sl]
    k = jnp.concatenate([kvj, kr], axis=1)
    return q, k, kvj, _dot_nt(q, k)


def _att_specs(L, lk, q0, pairs=1):
    TQ, W2 = TOKEN_BLOCK, 2 * LANES * pairs
    qspec = pl.BlockSpec((1, TQ, W2), lambda b, h, i: (b, i + q0, h))
    kvspec = pl.BlockSpec((1, lk, W2), lambda b, h, i: (b, 0, h))
    krspec = pl.BlockSpec((1, lk, LANES), lambda b, h, i: (b, 0, 0))
    gspec = pl.BlockSpec((1, TQ, LANES * pairs), lambda b, h, i: (b, i + q0, h))
    lspec = pl.BlockSpec((1, pairs, TQ, LANES), lambda b, h, i: (b, h, i + q0, 0))
    return qspec, kvspec, krspec, gspec, lspec


_ANY = pl.BlockSpec(memory_space=pl.ANY)


def attention_fwd(qn, qr, kv, kr, zg, n_ctx, name, hosted=None):
    B, L, _ = qn.shape
    TQ = TOKEN_BLOCK
    PAIRS = 2
    KEY_SPLIT = 3
    HP = MLA_HEADS // 2 // PAIRS
    shapes = [jax.ShapeDtypeStruct((B, L, MLA_WIDTH), F32), jax.ShapeDtypeStruct((B, L, MLA_WIDTH), BF16),
              jax.ShapeDtypeStruct((B, MLA_HEADS // 2, L, LANES), F32)]

    nx = hosted.n if hosted else 0
    NQ = L // TQ - 1

    def body(*refs):
        if hosted:
            step = (pl.program_id(0) * HP + pl.program_id(1)) * NQ + pl.program_id(2)
            hosted.run(refs[5:5 + nx], refs[8 + nx:8 + 2 * nx], refs[8 + 2 * nx:], step, B * HP * NQ)
        _fwd_step(*refs[:5], *refs[5 + nx:8 + nx])

    def body_ctx(qn_ref, qr_ref, kv_ref, kr_ref, g_ref, *rest):
        _fwd_step(qn_ref, qr_ref, kv_ref, kr_ref, g_ref, *rest[-3:])

    def _fwd_step(qn_ref, qr_ref, kv_ref, kr_ref, g_ref, ya_ref, ym_ref, lse_ref):
        kr_v = kr_ref[0]
        nk = kr_v.shape[0]
        kparts = KEY_SPLIT if nk % (KEY_SPLIT * LANES) == 0 else 1
        for pr in range(PAIRS):
            outs, lses = [], []
            for j in (2 * pr, 2 * pr + 1):
                sl = slice(LANES * j, LANES * (j + 1))
                q = jnp.concatenate([qn_ref[0, :, sl], qr_ref[0, :, sl]], axis=1)
                m = o = None
                for part in range(kparts):
                    krows = slice(nk // kparts * part, nk // kparts * (part + 1))
                    kvj = kv_ref[0, krows, sl]
                    s = _dot_nt(q, jnp.concatenate([kvj, kr_v[krows]], axis=1))
                    mp = jnp.max(s, axis=-1, keepdims=True)
                    m_new = mp if m is None else jnp.maximum(m, mp)
                    p = jnp.exp(s - m_new).astype(BF16)
                    lane_k = lax.broadcasted_iota(jnp.int32, kvj.shape, 1)
                    op = _dot(p, jnp.where(lane_k < MLA_V, jnp.ones_like(kvj), kvj))
                    o = op if o is None else o * jnp.exp(m - m_new) + op
                    m = m_new
                l = o[:, 0:1]
                outs.append(o / l)
                lses.append(m + jnp.log(l))
            lane = lax.broadcasted_iota(jnp.int32, outs[0].shape, 1)
            y = jnp.where(lane < MLA_V, pltpu.roll(outs[0], MLA_V, 1), outs[1])
            sl = slice(LANES * pr, LANES * (pr + 1))
            ya_ref[0, :, sl] = y
            g = g_ref[0, :, sl]
            ym_ref[0, :, sl] = (y * g * _sig(g)).astype(BF16)
            lse_ref[0, pr] = jnp.where(lane < MLA_V, lses[0], lses[1])

    qspec, kvspec, krspec, gspec, lspec = _att_specs(L, L, 1, PAIRS)
    main = pl.pallas_call(
        body, name=name, grid=(B, HP, NQ),
        in_specs=[qspec, qspec, kvspec, krspec, gspec] + (hosted.in_specs if hosted else []),
        out_specs=[gspec, gspec, lspec] + (hosted.out_specs if hosted else []),
        out_shape=shapes + (hosted.out_shape if hosted else []), scratch_shapes=hosted.scratch if hosted else [],
        compiler_params=_params(("arbitrary",) * 3 if hosted else ("parallel",) * 3, VMEM_LIMIT),
    )(qn, qr, kv, kr, zg, *(hosted.xs if hosted else []))
    qspec, kvspec, krspec, gspec, lspec = _att_specs(L, n_ctx, 0, PAIRS)
    outs = pl.pallas_call(
        body_ctx, name=name + "_ctx", grid=(B, HP, 1),
        in_specs=[qspec, qspec, kvspec, krspec, gspec, _ANY, _ANY, _ANY], out_specs=[gspec, gspec, lspec], out_shape=shapes,
        input_output_aliases={5: 0, 6: 1, 7: 2},
        compiler_params=_params(("parallel", "parallel", "parallel"), VMEM_LIMIT),
    )(qn, qr, kv, kr, zg, *main[:3])
    return (*outs, list(main[3:]))


def attention_bwd(qn, qr, kv, kr, zg, ya, lse, dym, n_ctx, name, hosted=None):
    B, L, _ = qn.shape
    TQ = TOKEN_BLOCK
    PAIRS = 2
    HP = MLA_HEADS // 2 // PAIRS
    W = MLA_HEADS * LANES
    shapes = [jax.ShapeDtypeStruct((B, L, W), BF16), jax.ShapeDtypeStruct((B, L, W), BF16), jax.ShapeDtypeStruct((B, L, W), F32),
              jax.ShapeDtypeStruct((B, L, LANES), F32), jax.ShapeDtypeStruct((B, L, MLA_WIDTH), BF16)]

    nx = hosted.n if hosted else 0
    NQ = L // TQ - 1

    def body(*refs):
        if hosted:
            step = (pl.program_id(0) * HP + pl.program_id(1)) * NQ + pl.program_id(2)
            hosted.run(refs[8:8 + nx], refs[13 + nx:13 + 2 * nx], refs[13 + 2 * nx:], step, B * HP * NQ)
        dkv_ref, dkr_ref = refs[10 + nx], refs[11 + nx]

        @pl.when(pl.program_id(2) == 0)
        def _():
            dkv_ref[...] = jnp.zeros_like(dkv_ref)

        @pl.when(jnp.logical_and(pl.program_id(2) == 0, pl.program_id(1) == 0))
        def _():
            dkr_ref[...] = jnp.zeros_like(dkr_ref)

        _bwd_step(*refs[:8], *refs[8 + nx:13 + nx])

    def body_ctx(qn_ref, qr_ref, kv_ref, kr_ref, g_ref, ya_ref, lse_ref, dy_ref, dkv_in, dkr_in, a0, a1, a2,
                 dqn_ref, dqr_ref, dkv_ref, dkr_ref, dzg_ref):
        dkv_ref[...] = dkv_in[...]

        @pl.when(pl.program_id(1) == 0)
        def _():
            dkr_ref[...] = dkr_in[...]

        _bwd_step(qn_ref, qr_ref, kv_ref, kr_ref, g_ref, ya_ref, lse_ref, dy_ref, dqn_ref, dqr_ref, dkv_ref, dkr_ref, dzg_ref)

    def _bwd_step(qn_ref, qr_ref, kv_ref, kr_ref, g_ref, ya_ref, lse_ref, dy_ref, dqn_ref, dqr_ref, dkv_ref, dkr_ref, dzg_ref):
        kr_v = kr_ref[0]
        for pr in range(PAIRS):
            psl = slice(LANES * pr, LANES * (pr + 1))
            silu, dsilu = _silu_and_grad(g_ref[0, :, psl])
            dy = dy_ref[0, :, psl]
            ya_v = ya_ref[0, :, psl]
            dya = dy * silu
            dzg_ref[0, :, psl] = (dy * ya_v * dsilu).astype(BF16)
            lane = lax.broadcasted_iota(jnp.int32, dya.shape, 1)
            hi = lane >= MLA_V
            d_out = [jnp.where(hi, pltpu.roll(dya, MLA_V, 1), 0.0), jnp.where(hi, dya, 0.0)]
            prod = dya * ya_v
            drow = [jnp.sum(jnp.where(hi, 0.0, prod), axis=-1, keepdims=True), jnp.sum(jnp.where(hi, prod, 0.0), axis=-1, keepdims=True)]
            lse_v = lse_ref[0, pr]
            for jj in range(2):
                j = 2 * pr + jj
                sl = slice(LANES * j, LANES * (j + 1))
                q, k, kvj, s = _att_qk(qn_ref, qr_ref, kv_ref, kr_v, j)
                pn = jnp.exp(s - lse_v[:, MLA_V * jj:MLA_V * jj + 1])
                dob = d_out[jj].astype(BF16)
                ds = (pn * (_dot_nt(dob, kvj) - drow[jj])).astype(BF16)
                dq = _dot(ds, k)
                dqn_ref[0, :, sl] = jnp.where(hi, 0.0, dq[:, :LANES]).astype(BF16)
                dqr_ref[0, :, sl] = dq[:, LANES:].astype(BF16)
                dk = _dot_tn(ds, q)
                dkv_ref[0, :, sl] += dk[:, :LANES] + _dot_tn(pn.astype(BF16), dob)
                dkr_ref[0] += dk[:, LANES:]

    sem = _params(("parallel", "arbitrary", "arbitrary"), VMEM_LIMIT)
    qspec, kvspec, krspec, gspec, lspec = _att_specs(L, L, 1, PAIRS)
    main = pl.pallas_call(
        body, name=name, grid=(B, HP, NQ),
        in_specs=[qspec, qspec, kvspec, krspec, gspec, gspec, lspec, gspec] + (hosted.in_specs if hosted else []),
        out_specs=[qspec, qspec, kvspec, krspec, gspec] + (hosted.out_specs if hosted else []),
        out_shape=shapes + (hosted.out_shape if hosted else []), scratch_shapes=hosted.scratch if hosted else [],
        compiler_params=_params(("arbitrary",) * 3, VMEM_LIMIT) if hosted else sem,
    )(qn, qr, kv, kr, zg, ya, lse, dym, *(hosted.xs if hosted else []))
    qspec, kvspec, krspec, gspec, lspec = _att_specs(L, n_ctx, 0, PAIRS)
    outs = pl.pallas_call(
        body_ctx, name=name + "_ctx", grid=(B, HP, 1),
        in_specs=[qspec, qspec, kvspec, krspec, gspec, gspec, lspec, gspec, kvspec, krspec, _ANY, _ANY, _ANY],
        out_specs=[qspec, qspec, kvspec, krspec, gspec], out_shape=shapes,
        input_output_aliases={8: 2, 9: 3, 10: 0, 11: 1, 12: 4}, compiler_params=sem,
    )(qn, qr, kv, kr, zg, ya, lse, dym, main[2], main[3], main[0], main[1], main[4])
    return (*outs, list(main[5:]))


def _seg_bounds(rows, n_ctx, L):
    in_ctx = rows < n_ctx
    return jnp.where(in_ctx, 0, n_ctx), jnp.where(in_ctx, n_ctx, L)


def _seg_shift(x, k, rows, lo, hi):
    src = rows + k
    return jnp.where(jnp.logical_and(src >= lo, src < hi), pltpu.roll(x, (-k) % x.shape[0], 0), 0.0)


def _run_sum(u, m, rows, lo, hi, step):
    acc, k = u, 1
    while k < m:
        acc = acc + _seg_shift(acc, step * k, rows, lo, hi)
        k *= 2
    return acc


def _window_sum(u, w, rows, lo, hi, mirror):
    right = _run_sum(u, w // 2, rows, lo, hi, 1)
    left = _run_sum(u, w // 2, rows, lo, hi, -1)
    if mirror:
        return left + _seg_shift(right, 1, rows, lo, hi)
    return right + _seg_shift(left, -1, rows, lo, hi)


def _window_count(w, rows, lo, hi):
    pos = rows - lo
    return (jnp.minimum(pos + w // 2, hi - lo) - jnp.maximum(pos - w // 2, 0)).astype(F32)


def pool_fwd(px, pg, pw, ps, n_ctx, name):
    B, L, _ = px.shape

    def body(px_ref, pg_ref, pw_ref, ps_ref, y_ref):
        rows = lax.broadcasted_iota(jnp.int32, (L, POOL_GROUP), 0)
        lo, hi = _seg_bounds(rows, n_ctx, L)
        for gi, w in enumerate(POOL_WINDOWS):
            sl = slice(POOL_GROUP * gi, POOL_GROUP * (gi + 1))
            u = px_ref[0, :, sl]
            pooled = _window_sum(u, w, rows, lo, hi, False) / _window_count(w, rows, lo, hi) - u
            mixed = _dot(pooled.astype(BF16), pw_ref[gi])
            g = pg_ref[0, :, sl]
            y_ref[0, :, sl] = (mixed * ps_ref[:, sl] * (g * _sig(g))).astype(BF16)

    tok = pl.BlockSpec((1, L, POOL_WIDTH), lambda b: (b, 0, 0))
    return pl.pallas_call(
        body, name=name, grid=(B,),
        in_specs=[tok, tok, _full(pw.shape), _full(ps.shape)],
        out_specs=tok, out_shape=jax.ShapeDtypeStruct((B, L, POOL_WIDTH), BF16),
        compiler_params=_params(("parallel",), VMEM_LIMIT),
    )(px, pg, pw, ps)


def pool_bwd(px, pg, pw, ps, dy, n_ctx, name):
    B, L, _ = px.shape

    def body(px_ref, pg_ref, pw_ref, ps_ref, dy_ref, dpx_ref, dpg_ref, dpw_ref, dps_ref):
        first = pl.program_id(0) == 0
        rows = lax.broadcasted_iota(jnp.int32, (L, POOL_GROUP), 0)
        lo, hi = _seg_bounds(rows, n_ctx, L)
        for gi, w in enumerate(POOL_WINDOWS):
            sl = slice(POOL_GROUP * gi, POOL_GROUP * (gi + 1))
            u = px_ref[0, :, sl]
            cnt = _window_count(w, rows, lo, hi)
            pooled = (_window_sum(u, w, rows, lo, hi, False) / cnt - u).astype(BF16)
            mixed = _dot(pooled, pw_ref[gi])
            silu, dsilu = _silu_and_grad(pg_ref[0, :, sl])
            sc = ps_ref[:, sl]
            dyv = dy_ref[0, :, sl]
            _acc(dps_ref.at[:, sl], jnp.sum(dyv * mixed * silu, axis=0, keepdims=True), first)
            dpg_ref[0, :, sl] = (dyv * mixed * sc * dsilu).astype(BF16)
            dmixed = (dyv * sc * silu).astype(BF16)
            _acc(dpw_ref.at[gi], _dot_tn(pooled, dmixed), first)
            dpooled = _dot_nt(dmixed, pw_ref[gi])
            dpx_ref[0, :, sl] = (_window_sum(dpooled / cnt, w, rows, lo, hi, True) - dpooled).astype(BF16)

    tok = pl.BlockSpec((1, L, POOL_WIDTH), lambda b: (b, 0, 0))
    return pl.pallas_call(
        body, name=name, grid=(B,),
        in_specs=[tok, tok, _full(pw.shape), _full(ps.shape), tok],
        out_specs=[tok, tok, _full(pw.shape), _full(ps.shape)],
        out_shape=[jax.ShapeDtypeStruct((B, L, POOL_WIDTH), BF16)] * 2 + [jax.ShapeDtypeStruct(pw.shape, F32), jax.ShapeDtypeStruct(ps.shape, F32)],
        compiler_params=_params(("arbitrary",), VMEM_LIMIT),
    )(px, pg, pw, ps, dy)


_SCAN_STEPS = (1, 2, 4, 8, 16, 32)
SCAN_CHUNKS = 4


def _cum_fwd(x, r):
    for s in _SCAN_STEPS:
        x = x + jnp.where(r >= s, pltpu.roll(x, s, 0), 0.0)
    return x


def _cum_bwd(x, r):
    n = x.shape[0]
    for s in _SCAN_STEPS:
        x = x + jnp.where(r + s < GLA_CHUNK, pltpu.roll(x, n - s, 0), 0.0)
    return x


def _log_sigmoid(x):
    return jnp.minimum(x, 0.0) - jnp.log(1.0 + jnp.exp(-jnp.abs(x)))


def _gla_decays(lr, w_ref, b_ref, r, reverse):
    pre = _dot(lr, w_ref[...]) + b_ref[...]
    a = _log_sigmoid(pre) / GLA_TAU
    return pre, a, (_cum_bwd(a, r) if reverse else _cum_fwd(a, r)), _chunk_total(a)


def _chunk_total(x):
    x3 = x.reshape(x.shape[0] // GLA_CHUNK, GLA_CHUNK, x.shape[1])
    return jnp.broadcast_to(jnp.sum(x3, axis=1, keepdims=True), x3.shape).reshape(x.shape)


def gla_prep_fwd(zlr, zq, zk, waf, wab, baf, bab, name):
    T = zlr.shape[0]
    tb = _wide_block(T)

    def body(lr_ref, q_ref, k_ref, waf_ref, wab_ref, baf_ref, bab_ref, qf_ref, kf_ref, ksf_ref, tf_ref, qb_ref, kb_ref, ksb_ref, tb_ref):
        r = lax.broadcasted_iota(jnp.int32, (tb, GLA_KW), 0) % GLA_CHUNK
        lr = lr_ref[...].astype(BF16)
        q = q_ref[...] * GLA_DK ** -0.5
        k = k_ref[...]
        for rev, w_ref, b_ref, qo, ko, kso, to in ((False, waf_ref, baf_ref, qf_ref, kf_ref, ksf_ref, tf_ref),
                                                   (True, wab_ref, bab_ref, qb_ref, kb_ref, ksb_ref, tb_ref)):
            _, _, b, tot = _gla_decays(lr, w_ref, b_ref, r, rev)
            qo[...] = (q * jnp.exp(b)).astype(BF16)
            ko[...] = (k * jnp.exp(-b)).astype(BF16)
            kso[...] = (k * jnp.exp(tot - b)).astype(BF16)
            to[...] = tot

    tok = lambda n: pl.BlockSpec((tb, n), lambda i: (i, 0))
    outs = [jax.ShapeDtypeStruct((T, GLA_KW), BF16)] * 3 + [jax.ShapeDtypeStruct((T, GLA_KW), F32)]
    return pl.pallas_call(
        body, name=name, grid=(T // tb,),
        in_specs=[tok(LANES), tok(GLA_KW), tok(GLA_KW), _full(waf.shape), _full(wab.shape), _full(baf.shape), _full(bab.shape)],
        out_specs=[tok(GLA_KW)] * 8, out_shape=outs + outs,
        compiler_params=_params(("parallel",)),
    )(zlr, zq, zk, waf, wab, baf, bab)


def gla_prep_bwd(zlr, zq, zk, waf, wab, baf, bab, gf, gb, dvs, name):
    T = zlr.shape[0]
    tb = _wide_block(T)

    def body(lr_ref, q_ref, k_ref, waf_ref, wab_ref, baf_ref, bab_ref, dqf, dkf, dksf, ddf, dqb, dkb, dksb, ddb, dvf, dvb,
             dlr_ref, dq_ref, dk_ref, dwaf_ref, dwab_ref, dbaf_ref, dbab_ref, dv_ref):
        first = pl.program_id(0) == 0
        dv_ref[...] = (dvf[...] + dvb[...]).astype(BF16)
        r = lax.broadcasted_iota(jnp.int32, (tb, GLA_KW), 0) % GLA_CHUNK
        lr = lr_ref[...].astype(BF16)
        q = q_ref[...] * GLA_DK ** -0.5
        k = k_ref[...]
        dq_tot = None
        dk_tot = None
        dlr = None
        for rev, w_ref, b_ref, dqt, dkt, dks, ddec, dw_ref, db_ref in (
                (False, waf_ref, baf_ref, dqf, dkf, dksf, ddf, dwaf_ref, dbaf_ref),
                (True, wab_ref, bab_ref, dqb, dkb, dksb, ddb, dwab_ref, dbab_ref)):
            pre, _, b, tot = _gla_decays(lr, w_ref, b_ref, r, rev)
            e1 = jnp.exp(b)
            e2 = jnp.exp(-b)
            e3 = jnp.exp(tot - b)
            dqt_v = dqt[...]
            dkt_v = dkt[...]
            dks_v = dks[...]
            dq = dqt_v * e1
            dk = dkt_v * e2 + dks_v * e3
            g3 = dks_v * (k * e3)
            d_b = dqt_v * (q * e1) - dkt_v * (k * e2) - g3
            d_tot = _chunk_total(g3) + ddec[...] * jnp.exp(tot)
            da = (_cum_fwd(d_b, r) if rev else _cum_bwd(d_b, r)) + d_tot
            dpre = (da * (_sig(-pre) / GLA_TAU)).astype(BF16)
            t = _dot_nt(dpre, w_ref[...])
            dlr = t if dlr is None else dlr + t
            _acc(dw_ref, _dot_tn(lr, dpre), first)
            _acc(db_ref, jnp.sum(dpre.astype(F32), axis=0, keepdims=True), first)
            dq_tot = dq if dq_tot is None else dq_tot + dq
            dk_tot = dk if dk_tot is None else dk_tot + dk
        dlr_ref[...] = dlr.astype(BF16)
        dq_ref[...] = (dq_tot * GLA_DK ** -0.5).astype(BF16)
        dk_ref[...] = dk_tot.astype(BF16)

    tok = lambda n: pl.BlockSpec((tb, n), lambda i: (i, 0))
    return pl.pallas_call(
        body, name=name, grid=(T // tb,),
        in_specs=[tok(LANES), tok(GLA_KW), tok(GLA_KW), _full(waf.shape), _full(wab.shape), _full(baf.shape), _full(bab.shape)] + [tok(GLA_KW)] * 8
        + [tok(GLA_WIDTH)] * 2,
        out_specs=[tok(LANES), tok(GLA_KW), tok(GLA_KW), _full(waf.shape), _full(wab.shape), _full(baf.shape), _full(bab.shape), tok(GLA_WIDTH)],
        out_shape=[jax.ShapeDtypeStruct((T, LANES), BF16), jax.ShapeDtypeStruct((T, GLA_KW), BF16), jax.ShapeDtypeStruct((T, GLA_KW), BF16),
                   jax.ShapeDtypeStruct(waf.shape, F32), jax.ShapeDtypeStruct(wab.shape, F32), jax.ShapeDtypeStruct(baf.shape, F32),
                   jax.ShapeDtypeStruct(bab.shape, F32), jax.ShapeDtypeStruct((T, GLA_WIDTH), BF16)],
        compiler_params=_params(("arbitrary",)),
    )(zlr, zq, zk, waf, wab, baf, bab, *gf, *gb, *dvs)


def _chunk_order(nc, n_ctx_chunks, reverse):
    if not reverse:
        return lambda c: c
    return lambda c: jnp.where(c < n_ctx_chunks, n_ctx_chunks - 1 - c, nc + n_ctx_chunks - 1 - c)


def _tri_mask4(reverse):
    ri = lax.broadcasted_iota(jnp.int32, (GLA_CHUNK, GLA_HEADS * GLA_CHUNK), 0)
    ci = lax.broadcasted_iota(jnp.int32, (GLA_CHUNK, GLA_HEADS * GLA_CHUNK), 1) % GLA_CHUNK
    return (ri <= ci) if reverse else (ri >= ci)


def _block_diag(x, rb, cb):
    x4 = jnp.concatenate([x] * GLA_HEADS, axis=0)
    r = lax.broadcasted_iota(jnp.int32, x4.shape, 0) // rb
    c = lax.broadcasted_iota(jnp.int32, x4.shape, 1) // cb
    return jnp.where(r == c, x4, jnp.zeros_like(x4))


def _diag_blocks(f, rb, cb):
    c = lax.broadcasted_iota(jnp.int32, (rb, GLA_HEADS * cb), 1) // cb
    out = None
    for h in range(GLA_HEADS):
        t = jnp.where(c == h, f[rb * h:rb * (h + 1)], 0.0)
        out = t if out is None else out + t
    return out


def gla_scan_fwd(dirs, v, n_ctx, name):
    B, L, _ = v.shape
    C, G = GLA_CHUNK, SCAN_CHUNKS
    nc = L // C
    orders = [_chunk_order(nc // G, n_ctx // C // G, rev) for rev in (False, True)]

    def body(qf, kf, ksf, tf, vf, qb, kb, ksb, tb, vb, of, ssf, ob, ssb, stf, stb):
        @pl.when(pl.program_id(1) == 0)
        def _():
            stf[...] = jnp.zeros_like(stf)
            stb[...] = jnp.zeros_like(stb)

        for sub in range(G):
            step(qf, kf, ksf, vf, tf, of, ssf, stf, False, sub)
            step(qb, kb, ksb, vb, tb, ob, ssb, stb, True, G - 1 - sub)

    def step(q_ref, k_ref, ks_ref, v_ref, tot_ref, o_ref, ss_ref, st, reverse, sub):
        rows = slice(C * sub, C * (sub + 1))
        S = st[...]
        ss_ref[0, sub] = S
        q = q_ref[0, rows]
        v = v_ref[0, rows]
        k4 = _block_diag(k_ref[0, rows], GLA_CHUNK, GLA_DK)
        v4 = _block_diag(v.astype(BF16), GLA_CHUNK, GLA_DV)
        s4 = _block_diag(S.astype(BF16), GLA_DV, GLA_DK)
        P = jnp.where(_tri_mask4(reverse), _dot_nt(q, k4), 0.0)
        o_ref[0, rows] = _dot(P.astype(BF16), v4) + _dot_nt(q, s4)
        st[...] = jnp.exp(tot_ref[0, C * sub:C * sub + 1, :]) * S + _diag_blocks(_dot(v.T.astype(BF16), ks_ref[0, rows]), GLA_DV, GLA_DK)

    in_specs, out_specs, out_shape = [], [], []
    for order in orders:
        tok = lambda n, order=order: pl.BlockSpec((1, G * C, n), lambda b, c: (b, order(c), 0))
        in_specs += [tok(GLA_KW), tok(GLA_KW), tok(GLA_KW), tok(GLA_KW), tok(GLA_WIDTH)]
        out_specs += [tok(GLA_WIDTH), pl.BlockSpec((1, G, GLA_DV, GLA_KW), lambda b, c, order=order: (b, order(c), 0, 0))]
        out_shape += [jax.ShapeDtypeStruct((B, L, GLA_WIDTH), F32), jax.ShapeDtypeStruct((B, nc, GLA_DV, GLA_KW), F32)]
    outs = pl.pallas_call(
        body, name=name, grid=(B, nc // G), in_specs=in_specs, out_specs=out_specs, out_shape=out_shape,
        scratch_shapes=[pltpu.VMEM((GLA_DV, GLA_KW), F32)] * 2,
        compiler_params=_params(("parallel", "arbitrary")),
    )(*dirs[0], v, *dirs[1], v)
    return outs[:2], outs[2:]


def gla_scan_bwd(dirs, v, do, n_ctx, name, hosted=None):
    B, L, _ = v.shape
    C, G = GLA_CHUNK, SCAN_CHUNKS
    nc = L // C
    npair = nc // G
    orders = []
    for rev in (False, True):
        fwd_order = _chunk_order(npair, n_ctx // C // G, rev)
        orders.append(lambda c, fwd_order=fwd_order: fwd_order(npair - 1 - c))

    nx = hosted.n if hosted else 0

    def body(*refs):
        qf, kf, ksf, tf, ssf, vf, dof, qb, kb, ksb, tb, ssb, vb, dob = refs[:14]
        dqf, dkf, dksf, dvf, ddf, dqb, dkb, dksb, dvb, ddb = refs[14 + nx:24 + nx]
        dstf, dstb = refs[24 + 2 * nx:26 + 2 * nx]
        if hosted:
            hosted.run(refs[14:14 + nx], refs[24 + nx:24 + 2 * nx], refs[26 + 2 * nx:], pl.program_id(0) * npair + pl.program_id(1), B * npair)

        @pl.when(pl.program_id(1) == 0)
        def _():
            dstf[...] = jnp.zeros_like(dstf)
            dstb[...] = jnp.zeros_like(dstb)

        for sub in range(G):
            step(qf, kf, ksf, vf, tf, ssf, dof, dqf, dkf, dksf, dvf, ddf, dstf, False, G - 1 - sub)
            step(qb, kb, ksb, vb, tb, ssb, dob, dqb, dkb, dksb, dvb, ddb, dstb, True, sub)

    def step(q_ref, k_ref, ks_ref, v_ref, tot_ref, ss_ref, do_ref, dq_ref, dk_ref, dks_ref, dv_ref, dd_ref, dst, reverse, sub):
        rows = slice(C * sub, C * (sub + 1))
        dSn = dst[...]
        S = ss_ref[0, sub]
        q = q_ref[0, rows]
        vb = v_ref[0, rows].astype(BF16)
        dob = do_ref[0, rows].astype(BF16)
        k4 = _block_diag(k_ref[0, rows], GLA_CHUNK, GLA_DK)
        v4 = _block_diag(vb, GLA_CHUNK, GLA_DV)
        s4 = _block_diag(S.astype(BF16), GLA_DV, GLA_DK)
        ds4 = _block_diag(dSn.astype(BF16), GLA_DV, GLA_DK)
        tri = _tri_mask4(reverse)
        P = jnp.where(tri, _dot_nt(q, k4), 0.0).astype(BF16)
        dP = jnp.where(tri, _dot_nt(dob, v4), 0.0).astype(BF16)
        dq_ref[0, rows] = _dot(dob, s4) + _dot(dP, k4)
        dk_ref[0, rows] = _diag_blocks(_dot_tn(dP, q), GLA_CHUNK, GLA_DK)
        dv_ref[0, rows] = _diag_blocks(_dot_tn(P, dob), GLA_CHUNK, GLA_DV) + _dot_nt(ks_ref[0, rows], ds4)
        dks_ref[0, rows] = _dot(vb, ds4)
        dd_ref[0, rows] = jnp.broadcast_to(jnp.sum(dSn * S, axis=0, keepdims=True), (C, GLA_KW))
        dst[...] = jnp.exp(tot_ref[0, C * sub:C * sub + 1, :]) * dSn + _diag_blocks(_dot_tn(dob, q), GLA_DV, GLA_DK)

    in_specs, out_specs, out_shape = [], [], []
    for order in orders:
        tok = lambda n, order=order: pl.BlockSpec((1, G * C, n), lambda b, c: (b, order(c), 0))
        in_specs += [tok(GLA_KW), tok(GLA_KW), tok(GLA_KW), tok(GLA_KW),
                     pl.BlockSpec((1, G, GLA_DV, GLA_KW), lambda b, c, order=order: (b, order(c), 0, 0)), tok(GLA_WIDTH), tok(GLA_WIDTH)]
        out_specs += [tok(GLA_KW), tok(GLA_KW), tok(GLA_KW), tok(GLA_WIDTH), tok(GLA_KW)]
        out_shape += [jax.ShapeDtypeStruct((B, L, GLA_KW), F32)] * 3 + [jax.ShapeDtypeStruct((B, L, GLA_WIDTH), F32), jax.ShapeDtypeStruct((B, L, GLA_KW), F32)]
    outs = pl.pallas_call(
        body, name=name, grid=(B, npair), in_specs=in_specs + (hosted.in_specs if hosted else []),
        out_specs=out_specs + (hosted.out_specs if hosted else []), out_shape=out_shape + (hosted.out_shape if hosted else []),
        scratch_shapes=[pltpu.VMEM((GLA_DV, GLA_KW), F32)] * 2 + (hosted.scratch if hosted else []),
        compiler_params=_params(("arbitrary", "arbitrary") if hosted else ("parallel", "arbitrary")),
    )(*dirs[0], v, do, *dirs[1], v, do, *(hosted.xs if hosted else []))
    return outs[:5], outs[5:10], list(outs[10:])


def gla_out_fwd(of, ob, gn, zg, name):
    T = of.shape[0]

    def body(of_ref, ob_ref, gn_ref, g_ref, y_ref):
        for h in range(GLA_HEADS):
            sl = slice(GLA_DV * h, GLA_DV * (h + 1))
            xn, _ = _rms_rows(of_ref[:, sl] + ob_ref[:, sl])
            g = g_ref[:, sl]
            y_ref[:, sl] = (xn * gn_ref[...] * (g * _sig(g))).astype(BF16)

    tb = _wide_block(T)
    tok = pl.BlockSpec((tb, GLA_WIDTH), lambda i: (i, 0))
    return pl.pallas_call(
        body, name=name, grid=(T // tb,),
        in_specs=[tok, tok, _full(gn.shape), tok], out_specs=tok,
        out_shape=jax.ShapeDtypeStruct((T, GLA_WIDTH), BF16),
        compiler_params=_params(("parallel",)),
    )(of, ob, gn, zg)


def gla_out_bwd(of, ob, gn, zg, dy, name):
    T = of.shape[0]

    def body(of_ref, ob_ref, gn_ref, g_ref, dy_ref, do_ref, dzg_ref, dgn_ref):
        first = pl.program_id(0) == 0
        gn_v = gn_ref[...]
        dgn = None
        for h in range(GLA_HEADS):
            sl = slice(GLA_DV * h, GLA_DV * (h + 1))
            xn, r = _rms_rows(of_ref[:, sl] + ob_ref[:, sl])
            silu, dsilu = _silu_and_grad(g_ref[:, sl])
            dyv = dy_ref[:, sl]
            dzg_ref[:, sl] = (dyv * xn * gn_v * dsilu).astype(BF16)
            dn = dyv * silu
            t = jnp.sum(dn * xn, axis=0, keepdims=True)
            dgn = t if dgn is None else dgn + t
            do_ref[:, sl] = _rms_rows_bwd(dn * gn_v, xn, r)
        _acc(dgn_ref, dgn, first)

    tb = _wide_block(T)
    tok = pl.BlockSpec((tb, GLA_WIDTH), lambda i: (i, 0))
    return pl.pallas_call(
        body, name=name, grid=(T // tb,),
        in_specs=[tok, tok, _full(gn.shape), tok, tok], out_specs=[tok, tok, _full(gn.shape)],
        out_shape=[jax.ShapeDtypeStruct((T, GLA_WIDTH), F32), jax.ShapeDtypeStruct((T, GLA_WIDTH), BF16), jax.ShapeDtypeStruct(gn.shape, F32)],
        compiler_params=_params(("arbitrary",)),
    )(of, ob, gn, zg, dy)


def merge_post_fwd(ys, zm, wbs, wo, x2, pg, ms, nb, name):
    T = x2.shape[0]

    def body(y0, y1, y2, zm_ref, w0, w1, w2, wo_ref, x_ref, pg_ref, gate_ref, xn_ref, out_ref, mg_ref):
        merged = None
        for i, (y_ref, w_ref) in enumerate(((y0, w0), (y1, w1), (y2, w2))):
            t = _sig(zm_ref[:, D_MODEL * i:D_MODEL * (i + 1)].astype(F32)) * _dot(y_ref[...], w_ref[...])
            merged = t if merged is None else merged + t
        mb = merged.astype(BF16)
        mg_ref[...] = mb
        out = _dot(mb, wo_ref[...])
        out_ref[...] = out
        on, _ = _rms_rows(out)
        xn_ref[...] = x_ref[...] + gate_ref[0] * (on * pg_ref[...])

    tok = lambda n: pl.BlockSpec((TOKEN_BLOCK, n), lambda i: (i, 0))
    return pl.pallas_call(
        body, name=name, grid=(T // TOKEN_BLOCK,),
        in_specs=[tok(512)] * 3 + [tok(3 * D_MODEL)] + [_full(w.shape) for w in wbs] + [_full(wo.shape), tok(D_MODEL), _full(pg.shape), _mod_spec(nb, 2)],
        out_specs=[tok(D_MODEL)] * 3,
        out_shape=[jax.ShapeDtypeStruct((T, D_MODEL), F32), jax.ShapeDtypeStruct((T, D_MODEL), F32), jax.ShapeDtypeStruct((T, D_MODEL), BF16)],
        compiler_params=_params(("parallel",), VMEM_LIMIT),
    )(*ys, zm, *wbs, wo, x2, pg, ms)


def merge_post_bwd(dxn, out, ys, zm, wbs, wo, pg, ms, nb, name):
    T = dxn.shape[0]
    nrow = ms.shape[0]
    row = _mod_row(nb)

    def body(dxn_ref, out_ref, y0, y1, y2, zm_ref, w0, w1, w2, wo_ref, pg_ref, gate_ref,
             dy0, dy1, dy2, dzm_ref, dout_ref, dp0, dp1, dp2, dgate_ref, dpg_ref):
        i = pl.program_id(0)
        dxn_v = dxn_ref[...]
        on, r = _rms_rows(out_ref[...])
        pg_v = pg_ref[...]
        _acc(dgate_ref.at[0], jnp.sum(dxn_v * on * pg_v, axis=0, keepdims=True), (i % nb) <= 1)
        dn = dxn_v * gate_ref[0]
        _acc(dpg_ref, jnp.sum(dn * on, axis=0, keepdims=True), i == 0)
        dout = _rms_rows_bwd(dn * pg_v, on, r).astype(BF16)
        dout_ref[...] = dout
        dmerged = _dot_nt(dout, wo_ref[...])
        for j, (y_ref, w_ref, dy_ref, dp_ref) in enumerate(((y0, w0, dy0, dp0), (y1, w1, dy1, dp1), (y2, w2, dy2, dp2))):
            sl = slice(D_MODEL * j, D_MODEL * (j + 1))
            g = _sig(zm_ref[:, sl].astype(F32))
            p = _dot(y_ref[...], w_ref[...])
            dzm_ref[:, sl] = (dmerged * p * g * (1.0 - g)).astype(BF16)
            dp = (dmerged * g).astype(BF16)
            dp_ref[...] = dp
            dy_ref[...] = _dot_nt(dp, w_ref[...])

    tok = lambda n: pl.BlockSpec((TOKEN_BLOCK, n), lambda i: (i, 0))
    return pl.pallas_call(
        body, name=name, grid=(T // TOKEN_BLOCK,),
        in_specs=[tok(D_MODEL), tok(D_MODEL)] + [tok(512)] * 3 + [tok(3 * D_MODEL)] + [_full(w.shape) for w in wbs] + [_full(wo.shape), _full(pg.shape), _mod_spec(nb, 2)],
        out_specs=[tok(512)] * 3 + [tok(3 * D_MODEL), tok(D_MODEL)] + [tok(D_MODEL)] * 3 + [pl.BlockSpec((1, 1, D_MODEL), lambda i: (row(i), 0, 0)), _full(pg.shape)],
        out_shape=[jax.ShapeDtypeStruct((T, 512), F32)] * 3 + [jax.ShapeDtypeStruct((T, 3 * D_MODEL), BF16), jax.ShapeDtypeStruct((T, D_MODEL), BF16)]
        + [jax.ShapeDtypeStruct((T, D_MODEL), BF16)] * 3 + [jax.ShapeDtypeStruct((nrow, 1, D_MODEL), F32), jax.ShapeDtypeStruct(pg.shape, F32)],
        compiler_params=_params(("arbitrary",), VMEM_LIMIT),
    )(dxn, out, *ys, zm, *wbs, wo, pg, ms)


def loss_head(y2, tgt2, nb, name):
    T = y2.shape[0]
    nlat = nb - 1

    def body(y_ref, t_ref, dy_ref, loss_ref, acc):
        i = pl.program_id(0)
        is_lat = (i % nb) > 0

        @pl.when(i == 0)
        def _():
            acc[...] = jnp.zeros_like(acc)

        @pl.when(is_lat)
        def _():
            e = y_ref[...] - t_ref[...]
            dy_ref[...] = e * (1.0 / D_MODEL)
            acc[...] += jnp.sum(e * e, axis=0, keepdims=True)

        @pl.when(jnp.logical_not(is_lat))
        def _():
            dy_ref[...] = jnp.zeros_like(dy_ref)

        @pl.when(i == pl.num_programs(0) - 1)
        def _():
            loss_ref[...] = jnp.sum(acc[...], axis=1, keepdims=True) * (0.5 / D_MODEL)

    tok = pl.BlockSpec((TOKEN_BLOCK, D_MODEL), lambda i: (i, 0))
    tgt = pl.BlockSpec((TOKEN_BLOCK, D_MODEL), lambda i: ((i // nb) * nlat + jnp.maximum(i % nb - 1, 0), 0))
    return pl.pallas_call(
        body, name=name, grid=(T // TOKEN_BLOCK,),
        in_specs=[tok, tgt], out_specs=[tok, _full((1, 1))],
        out_shape=[jax.ShapeDtypeStruct((T, D_MODEL), F32), jax.ShapeDtypeStruct((1, 1), F32)],
        scratch_shapes=[pltpu.VMEM((1, D_MODEL), F32)],
        compiler_params=_params(("arbitrary",)),
    )(y2, tgt2)


_IN_OFFS = tuple(int(o) for o in np.cumsum((0,) + IN_SIZES))
_IN_GROUPS = (("a", 0, 416, 512), ("mg", 416, 512, 512), ("px", 928, 512, 512), ("pg", 1440, 512, 512), ("gq", 1952, 256, 256),
              ("gk", 2208, 256, 256), ("gv", 2464, 512, 512), ("lr", 2976, 32, 128), ("gg", 3008, 512, 512), ("m", 3520, 3072, 3072))


def _pad_cols(w, n):
    return w if w.shape[1] == n else jnp.pad(w, ((0, 0), (0, n - w.shape[1])))


def layer_weights(w_in, w_uq, w_ukv, af_w2, ab_w2, wbm, wbp, wbg, w_out):
    W = {}
    for nm, off, n, npad in _IN_GROUPS:
        W["in_" + nm] = _pad_cols(w_in[:, off:off + n], npad)
    uq = w_uq.reshape(MLA_Q_RANK, MLA_HEADS, MLA_NOPE + MLA_ROPE)
    W["qn"] = jnp.pad(uq[:, :, :MLA_NOPE], ((0, 0), (0, 0), (0, LANES - MLA_NOPE))).reshape(MLA_Q_RANK, MLA_HEADS * LANES)
    W["qr"] = jnp.pad(uq[:, :, MLA_NOPE:], ((0, 0), (0, 0), (0, LANES - MLA_ROPE))).reshape(MLA_Q_RANK, MLA_HEADS * LANES)
    W["kv"] = w_ukv
    W["af"] = jnp.pad(af_w2, ((0, LANES - GLA_GATE_RANK), (0, 0)))
    W["ab"] = jnp.pad(ab_w2, ((GLA_GATE_RANK, LANES - 2 * GLA_GATE_RANK), (0, 0)))
    W["bm"], W["bp"], W["bg"], W["out"] = wbm, wbp, wbg, w_out
    return W


def rope_tables(L, n_ctx):
    t = np.arange(L - n_ctx)
    half = MLA_ROPE // 2
    inv = ROPE_BASE ** (-np.arange(0, half, 2, dtype=np.float32) / half)
    ang_r = (t // GRID_W).astype(np.float32)[:, None] * inv
    ang_c = (t % GRID_W).astype(np.float32)[:, None] * inv
    ang = jnp.asarray(np.concatenate([ang_r, ang_r, ang_c, ang_c], axis=-1), F32)
    cos = jnp.ones((L, LANES), F32).at[n_ctx:, :MLA_ROPE].set(jnp.cos(ang))
    sin = jnp.zeros((L, LANES), F32).at[n_ctx:, :MLA_ROPE].set(jnp.sin(ang))
    return cos, sin


def layer_fwd(x2, ms, W, P, cos, sin, B, L, n_ctx, tag, hosted=None):
    nb = L // TOKEN_BLOCK
    r3 = lambda a: a.reshape(B, L, a.shape[-1])
    r2 = lambda a: a.reshape(B * L, a.shape[-1])
    names = [g[0] for g in _IN_GROUPS[:-1]]
    h, zs = norm_in_proj(x2, P["pre"], ms, [W["in_" + n] for n in names], nb, tag + "in_proj")
    z = dict(zip(names, zs))
    (z["m"],) = mm_multi(h, [W["in_m"]], [BF16], tag + "in_proj_merge", tm=_wide_block(L))
    qn, qr, kv, kr = mla_prep_fwd(z["a"], P["qg"], P["kvg"], W["qn"], W["qr"], W["kv"], cos, sin, nb, tag + "mla_prep")
    ya, y_mla, lse, carried = attention_fwd(r3(qn), r3(qr), r3(kv), r3(kr), r3(z["mg"]), n_ctx, tag + "attention", hosted)
    y_pool = pool_fwd(r3(z["px"]), r3(z["pg"]), P["pw"], P["ps"], n_ctx, tag + "pool")
    qf, kf, ksf, tf, qb, kb, ksb, tb = gla_prep_fwd(z["lr"], z["gq"], z["gk"], W["af"], W["ab"], P["baf"], P["bab"], tag + "gla_prep")
    (of, ssf), (ob, ssb) = gla_scan_fwd([(r3(qf), r3(kf), r3(ksf), r3(tf)), (r3(qb), r3(kb), r3(ksb), r3(tb))], r3(z["gv"]), n_ctx, tag + "gla_scan")
    y_gla = gla_out_fwd(r2(of), r2(ob), P["gn"], z["gg"], tag + "gla_out")
    ys = [r2(y_mla), r2(y_pool), y_gla]
    x_new, out, merged = merge_post_fwd(ys, z["m"], [W["bm"], W["bp"], W["bg"]], W["out"], x2, P["post"], ms, nb, tag + "merge_post")
    res = dict(x2=x2, h=h, z=z, qn=qn, qr=qr, kv=kv, kr=kr, ya=ya, lse=lse, ys=ys, gla_f=(qf, kf, ksf, tf, ssf), gla_b=(qb, kb, ksb, tb, ssb),
               of=of, ob=ob, out=out, merged=merged)
    return x_new, res, carried


def layer_bwd(dxn, res, ms, W, P, cos, sin, B, L, n_ctx, tag, hosted=None, host_own=None):
    nb = L // TOKEN_BLOCK
    r3 = lambda a: a.reshape(B, L, a.shape[-1])
    r2 = lambda a: a.reshape(B * L, a.shape[-1])
    z = res["z"]
    ys = res["ys"]
    wbs = [W["bm"], W["bp"], W["bg"]]
    dy0, dy1, dy2, dzm, dout, dp0, dp1, dp2, dgate, dpost = merge_post_bwd(dxn, res["out"], ys, z["m"], wbs, W["out"], P["post"], ms, nb, tag + "merge_post_bwd")
    G = {"out": mm_dw(res["merged"], dout, tag + "dw_out"), "post": dpost}
    for nm, y, dp in zip(("bm", "bp", "bg"), ys, (dp0, dp1, dp2)):
        G[nm] = mm_dw(y, dp, tag + "dw_" + nm)
    g = {n: _natural_grad(G, n) for n in GRADS_EARLY}
    own = {}
    dz = {"m": dzm}
    do, dz["gg"], G["gn"] = gla_out_bwd(r2(res["of"]), r2(res["ob"]), P["gn"], z["gg"], dy2, tag + "gla_out_bwd")
    carrier = host_own(GRADS_EARLY, g) if host_own else None
    *grads, arrived = gla_scan_bwd([(r3(qt), r3(kt), r3(ks), r3(tot), ss) for qt, kt, ks, tot, ss in (res["gla_f"], res["gla_b"])],
                                   r3(z["gv"]), r3(do), n_ctx, tag + "gla_scan_bwd", carrier)
    own.update(zip(GRADS_EARLY, arrived))
    gf = [r2(a) for a in grads[0]]
    gb = [r2(a) for a in grads[1]]
    dz["lr"], dz["gq"], dz["gk"], G["af"], G["ab"], G["baf"], G["bab"], dz["gv"] = gla_prep_bwd(
        z["lr"], z["gq"], z["gk"], W["af"], W["ab"], P["baf"], P["bab"], gf[:3] + gf[4:], gb[:3] + gb[4:], [gf[3], gb[3]], tag + "gla_prep_bwd")
    dpx, dpg, G["pw"], G["ps"] = pool_bwd(r3(z["px"]), r3(z["pg"]), P["pw"], P["ps"], r3(dy1), n_ctx, tag + "pool_bwd")
    dz["px"], dz["pg"] = r2(dpx), r2(dpg)
    dqn, dqr, dkv, dkr, dzmg, got = attention_bwd(r3(res["qn"]), r3(res["qr"]), r3(res["kv"]), r3(res["kr"]), r3(z["mg"]), res["ya"], res["lse"],
                                                  r3(dy0), n_ctx, tag + "attention_bwd", hosted)
    dz["mg"] = r2(dzmg)
    dz["a"], G["qn"], G["qr"], G["kv"], G["qg"], G["kvg"] = mla_prep_bwd(
        r2(dqn), r2(dqr), r2(dkv), r2(dkr), z["a"], P["qg"], P["kvg"], W["qn"], W["qr"], W["kv"], cos, sin, nb, tag + "mla_prep_bwd")
    names = [grp[0] for grp in _IN_GROUPS]
    for n in names:
        G["in_" + n] = mm_dw(res["h"], dz[n], tag + "dw_in_" + n)
    g.update({n: _natural_grad(G, n) for n in GRADS_LATE})
    carrier = host_own(GRADS_LATE, g) if host_own else None
    (dx, dshift, dscale, G["pre"]), arrived = in_proj_norm_bwd([dz[n] for n in names], [W["in_" + n] for n in names], res["x2"], P["pre"], ms, dxn, nb,
                                                               tag + "in_proj_dx", carrier)
    own.update(zip(GRADS_LATE, arrived))
    g.update({n: _natural_grad(G, n) for n in GRADS_REPLICATED})
    dms = jnp.concatenate([dshift, dscale, dgate], axis=-1)
    return dx, g, dms, got, own


GRADS_EARLY = ("w_branch_mla", "w_branch_pool", "w_branch_gla", "w_out")
GRADS_LATE = ("w_in", "mla_w_uq", "mla_w_ukv", "gla_af_w2", "gla_ab_w2")
GRADS_REPLICATED = ("pre_norm", "post_norm", "mla_q_norm", "mla_kv_norm", "pool_w", "pool_scale", "gla_af_b", "gla_ab_b", "gla_norm")
_DIRECT = dict(mla_w_ukv="kv", w_branch_mla="bm", w_branch_pool="bp", w_branch_gla="bg", w_out="out", pool_w="pw")
_ROW = dict(pre_norm="pre", post_norm="post", mla_q_norm="qg", mla_kv_norm="kvg", pool_scale="ps", gla_af_b="baf", gla_ab_b="bab", gla_norm="gn")


def _natural_grad(G, name):
    if name == "w_in":
        parts = {off: G["in_" + nm][:, :n] for nm, off, n, npad in _IN_GROUPS}
        return jnp.concatenate([parts[o] for o in sorted(parts)], axis=1)
    if name == "mla_w_uq":
        gqn = G["qn"].reshape(MLA_Q_RANK, MLA_HEADS, LANES)[:, :, :MLA_NOPE]
        gqr = G["qr"].reshape(MLA_Q_RANK, MLA_HEADS, LANES)[:, :, :MLA_ROPE]
        return jnp.concatenate([gqn, gqr], axis=-1).reshape(MLA_Q_RANK, MLA_HEADS * (MLA_NOPE + MLA_ROPE))
    if name == "gla_af_w2":
        return G["af"][:GLA_GATE_RANK]
    if name == "gla_ab_w2":
        return G["ab"][GLA_GATE_RANK:2 * GLA_GATE_RANK]
    return G[_DIRECT[name]] if name in _DIRECT else G[_ROW[name]][0]


def local_step(x, c, ctx, c_ctx, small, loss_target, depth, layer_full, host_fwd=None, host_bwd=None, host_own=None, join=None):
    B, S, _ = x.shape
    n_ctx = ctx.shape[1]
    L = n_ctx + S
    nb = L // TOKEN_BLOCK
    cos, sin = rope_tables(L, n_ctx)
    x2 = join(ctx, x) if join else jnp.concatenate([ctx, x], axis=1).reshape(B * L, D_MODEL)
    a8 = jnp.zeros((8, D_MODEL), F32).at[:B].set(c).at[B].set(c_ctx)
    Ws, Ps, mss, ress, mod_ws = [], [], [], [], []
    carried = None
    for l in range(depth):
        tag = f"l{l}_"
        full = layer_full(l, carried)
        W = layer_weights(full["w_in"], full["mla_w_uq"], full["mla_w_ukv"], full["gla_af_w2"], full["gla_ab_w2"],
                          full["w_branch_mla"], full["w_branch_pool"], full["w_branch_gla"], full["w_out"])
        P = dict(pre=small["pre_norm"][l][None], post=small["post_norm"][l][None], qg=small["mla_q_norm"][l][None], kvg=small["mla_kv_norm"][l][None],
                 pw=small["pool_w"][l].astype(BF16), ps=small["pool_scale"][l][None], baf=small["gla_af_b"][l][None], bab=small["gla_ab_b"][l][None],
                 gn=small["gla_norm"][l][None])
        mod8 = mod_fwd(a8, full["mod_w"], small["mod_b"][l][None], tag + "mod")
        ms = jnp.stack([jnp.broadcast_to(mod8[B], (B, 3 * D_MODEL)), mod8[:B]], axis=1).reshape(2 * B, 1, 3 * D_MODEL)
        x2, res, carried = layer_fwd(x2, ms, W, P, cos, sin, B, L, n_ctx, tag, host_fwd(l) if host_fwd else None)
        Ws.append(W), Ps.append(P), mss.append(ms), ress.append(res), mod_ws.append(full["mod_w"])
    dx, loss = loss_head(x2, loss_target.reshape(B * S, D_MODEL), nb, "loss_head")
    grads = [None] * depth
    delivered = [None] * depth
    dz8s = [None] * depth
    da8 = None
    for l in reversed(range(depth)):
        tag = f"l{l}_"
        hosted = host_bwd(l, grads[l + 1]) if host_bwd and l + 1 < depth else None
        dx, g, dms, got, own = layer_bwd(dx, ress[l], mss[l], Ws[l], Ps[l], cos, sin, B, L, n_ctx, tag, hosted, host_own(l) if host_own else None)
        if hosted:
            delivered[l + 1] = got
        if own:
            delivered[l] = own
        dms = dms.reshape(B, 2, 3 * D_MODEL)
        dz8s[l] = jnp.zeros((8, 3 * D_MODEL), F32).at[:B].set(dms[:, 1]).at[B].set(jnp.sum(dms[:, 0], axis=0))
        g_mod_b, da = mod_bwd(a8, mod_ws[l], dz8s[l], tag + "mod_bwd")
        da8 = da if da8 is None else da8 + da
        g["mod_b"] = g_mod_b[0]
        grads[l] = g
    grad_x = dx.reshape(B, L, D_MODEL)[:, n_ctx:]
    return loss, grad_x, grads, da8[B], delivered, (a8, dz8s)


_MESH_ID = pl.DeviceIdType.MESH
_HBM = pl.BlockSpec(memory_space=pltpu.HBM)


def _me_and_peers():
    mx, my, mc = lax.axis_index("x"), lax.axis_index("y"), lax.axis_index("c")
    peers = []
    for k in range(1, N_DEV):
        px, py, pc = mx ^ ((k >> 2) & 1), my ^ ((k >> 1) & 1), mc ^ (k & 1)
        peers.append(((px, py, pc), 4 * px + 2 * py + pc))
    return 4 * mx + 2 * my + mc, peers


def _comm_scratch(n):
    return [pltpu.SemaphoreType.DMA((n * (N_DEV - 1),)), pltpu.SemaphoreType.DMA((n * (N_DEV - 1),)), pltpu.SemaphoreType.DMA((n,))]


class _Gather:
    def __init__(self, x_refs, o_refs, send_sems, recv_sems, local_sems):
        self.x, self.o, self.send, self.recv, self.local = x_refs, o_refs, send_sems, recv_sems, local_sems
        self.n = len(x_refs)
        mx, my, mc = lax.axis_index("x"), lax.axis_index("y"), lax.axis_index("c")
        self.me, self.sibling, self.mc = (mx, my, mc), (mx, my, 1 - mc), mc
        self.chips = [(1 - mx, my), (mx, 1 - my), (1 - mx, 1 - my)]

    @staticmethod
    def out_shape(xs):
        return [jax.ShapeDtypeStruct((N_DEV,) + x.shape, x.dtype) for x in xs]

    def _copy(self, i, k, block, to, src=None):
        px, py, pc = block
        dst = self.o[i].at[4 * px + 2 * py + pc]
        sem = (N_DEV - 1) * i + k
        return pltpu.make_async_remote_copy(src_ref=dst if src is None else src, dst_ref=dst, send_sem=self.send.at[sem],
                                            recv_sem=self.recv.at[sem], device_id=to, device_id_type=_MESH_ID)

    def _mine(self, i):
        mx, my, mc = self.me
        return pltpu.make_async_copy(self.x[i], self.o[i].at[4 * mx + 2 * my + mc], self.local.at[i])

    def _first(self):
        out = []
        for i in range(self.n):
            out.append(self._copy(i, 0, self.me, self.sibling, src=self.x[i]))
            out += [self._copy(i, 1 + j, self.me, (*chip, self.mc), src=self.x[i]) for j, chip in enumerate(self.chips)]
        return out

    def _passed(self, j, i):
        return self._copy(i, 4 + j, (*self.chips[j], self.mc), self.sibling)

    def start(self):
        for i in range(self.n):
            self._mine(i).start()
        for cp in self._first():
            cp.start()

    def forward(self):
        for j, chip in enumerate(self.chips):
            for i in range(self.n):
                self._copy(i, 1 + j, (*chip, self.mc), self.me).wait_recv()
                self._passed(j, i).start()

    def finish(self):
        for i in range(self.n):
            self._copy(i, 0, self.sibling, self.me).wait_recv()
            for j, chip in enumerate(self.chips):
                self._copy(i, 4 + j, (*chip, 1 - self.mc), self.me).wait_recv()
        for cp in self._first():
            cp.wait_send()
        for j in range(len(self.chips)):
            for i in range(self.n):
                self._passed(j, i).wait_send()
        for i in range(self.n):
            self._mine(i).wait()


class _Scatter:
    def __init__(self, x_refs, o_refs, send_sems, recv_sems, local_sems):
        self.x, self.o, self.send, self.recv, self.local = x_refs, o_refs, send_sems, recv_sems, local_sems
        self.n = len(x_refs)
        self.me, self.peers = _me_and_peers()

    @staticmethod
    def out_shape(xs):
        return [jax.ShapeDtypeStruct(x.shape, x.dtype) for x in xs]

    def _copy(self, i, k, src_slot, dst_slot, to):
        sem = (N_DEV - 1) * i + k
        return pltpu.make_async_remote_copy(src_ref=self.x[i].at[src_slot], dst_ref=self.o[i].at[dst_slot], send_sem=self.send.at[sem],
                                            recv_sem=self.recv.at[sem], device_id=to, device_id_type=_MESH_ID)

    def _mine(self, i):
        return pltpu.make_async_copy(self.x[i].at[self.me], self.o[i].at[self.me], self.local.at[i])

    def _sends(self):
        return [self._copy(i, k, slot, self.me, peer) for k, (peer, slot) in enumerate(self.peers) for i in range(self.n)]

    def start(self):
        for i in range(self.n):
            self._mine(i).start()
        for cp in self._sends():
            cp.start()

    def forward(self):
        pass

    def finish(self):
        for k, (peer, slot) in enumerate(self.peers):
            for i in range(self.n):
                self._copy(i, k, slot, slot, peer).wait_recv()
        for cp in self._sends():
            cp.wait_send()
        for i in range(self.n):
            self._mine(i).wait()


class _Hosted:
    def __init__(self, kind, xs):
        self.kind, self.xs, self.n = kind, list(xs), len(xs)
        self.in_specs = [_HBM] * self.n
        self.out_specs = [_HBM] * self.n
        self.out_shape = kind.out_shape(self.xs)
        self.scratch = _comm_scratch(self.n)

    def run(self, x_refs, o_refs, sems, step, total):
        for when, phase in ((0, "start"), (3 * total // 4, "forward"), (total - 1, "finish")):
            @pl.when(step == when)
            def _(phase=phase):
                getattr(self.kind(x_refs, o_refs, *sems), phase)()


def gather_blocks(xs, name):
    n = len(xs)

    def body(*refs):
        g = _Gather(refs[:n], refs[n:2 * n], *refs[2 * n:])
        g.start()
        g.forward()
        g.finish()

    return pl.pallas_call(
        body, name=name, in_specs=[_HBM] * n, out_specs=[_HBM] * n,
        out_shape=_Gather.out_shape(xs), scratch_shapes=_comm_scratch(n),
    )(*xs)


def join_tokens(ctx, x, hosted, name):
    B, n_ctx, _ = ctx.shape
    S = x.shape[1]
    assert n_ctx == TOKEN_BLOCK
    nb = (n_ctx + S) // TOKEN_BLOCK
    nx = hosted.n

    def body(*refs):
        c_ref, x_ref, o_ref = refs[0], refs[1], refs[2 + nx]
        i = pl.program_id(0)
        hosted.run(refs[2:2 + nx], refs[3 + nx:3 + 2 * nx], refs[3 + 2 * nx:], i, B * nb)

        @pl.when(i % nb == 0)
        def _():
            o_ref[...] = c_ref[...]

        @pl.when(i % nb > 0)
        def _():
            o_ref[...] = x_ref[...]

    blk = (TOKEN_BLOCK, D_MODEL)
    outs = pl.pallas_call(
        body, name=name, grid=(B * nb,),
        in_specs=[pl.BlockSpec(blk, lambda i: (i // nb, 0)), pl.BlockSpec(blk, lambda i: ((i // nb) * (nb - 1) + jnp.maximum(i % nb - 1, 0), 0))]
        + hosted.in_specs,
        out_specs=[pl.BlockSpec(blk, lambda i: (i, 0))] + hosted.out_specs,
        out_shape=[jax.ShapeDtypeStruct((B * nb * TOKEN_BLOCK, D_MODEL), F32)] + hosted.out_shape,
        scratch_shapes=hosted.scratch, compiler_params=_params(("arbitrary",)),
    )(ctx.reshape(B * n_ctx, D_MODEL), x.reshape(B * S, D_MODEL), *hosted.xs)
    return outs[0], list(outs[1:])


def reduce_adamw(slots, w, m, v, name, tr=256):
    R, C = w.shape
    nl = len(slots)
    ns = slots[0].shape[0]
    rows = R // nl
    tr = min(tr, rows)
    nbl = rows // tr
    c1 = 1.0 / (1.0 - ADAM_B1 ** ADAM_STEP)
    c2 = 1.0 / (1.0 - ADAM_B2 ** ADAM_STEP)

    def body(*refs):
        w_ref, m_ref, v_ref, g_ref, d_ref, nm_ref, nv_ref = refs[nl:]
        part = pl.program_id(0) // nbl
        g = None
        for l, s_ref in enumerate(refs[:nl]):
            gl = s_ref[0].astype(F32)
            for s in range(1, ns):
                gl = gl + s_ref[s].astype(F32)
            g = gl if g is None else jnp.where(part == l, gl, g)
        nm = ADAM_B1 * m_ref[...] + (1.0 - ADAM_B1) * g
        nv = ADAM_B2 * v_ref[...] + (1.0 - ADAM_B2) * (g * g)
        g_ref[...] = g
        nm_ref[...] = nm
        nv_ref[...] = nv
        d_ref[...] = -ADAM_LR * ((nm * c1) / (jnp.sqrt(nv * c2) + ADAM_EPS) + ADAM_WD * w_ref[...])

    blk = pl.BlockSpec((tr, C), lambda i: (i, 0))
    sspecs = [pl.BlockSpec((ns, tr, C), lambda i, l=l: (0, jnp.clip(i - l * nbl, 0, nbl - 1), 0)) for l in range(nl)]
    return pl.pallas_call(
        body, name=name, grid=(R // tr,),
        in_specs=sspecs + [blk, blk, blk], out_specs=[blk] * 4,
        out_shape=[jax.ShapeDtypeStruct((R, C), F32)] * 4,
        compiler_params=_params(("parallel",), VMEM_LIMIT),
    )(*slots, w, m, v)


ARG_WEIGHTS = ("c_ctx", "mod_w", "mod_b", "pre_norm", "post_norm", "w_in", "mla_q_norm", "mla_w_uq", "mla_kv_norm", "mla_w_ukv", "pool_w",
               "pool_scale", "gla_af_w2", "gla_af_b", "gla_ab_w2", "gla_ab_b", "gla_norm", "w_branch_mla", "w_branch_pool", "w_branch_gla", "w_out")
SHARDED = ("mod_w", "w_in", "mla_w_uq", "mla_w_ukv", "gla_af_w2", "gla_ab_w2", "w_branch_mla", "w_branch_pool", "w_branch_gla", "w_out")
ROW_SHARDED = ("w_out",)
REPLICATED = tuple(n for n in ARG_WEIGHTS if n not in SHARDED)
PACK_ROWS = 512


def _pack(parts, dtype):
    flat = jnp.concatenate([p.astype(dtype).reshape(-1) for p in parts])
    n = flat.shape[0]
    total = -(-n // (PACK_ROWS * LANES)) * (PACK_ROWS * LANES)
    return jnp.pad(flat, (0, total - n)).reshape(total // LANES, LANES)


def _unpack(buf, shapes):
    flat = buf.reshape(-1)
    out, off = [], 0
    for shp in shapes:
        n = math.prod(shp)
        out.append(flat[off:off + n].reshape(shp))
        off += n
    return out


def _gathered_to_full(g, name):
    _, r, cs = g.shape
    if name in ROW_SHARDED:
        return g.reshape(N_DEV * r, cs)
    return g.transpose(1, 0, 2).reshape(r, N_DEV * cs)


def _full_to_slots(w, name):
    if name in ROW_SHARDED:
        return w.reshape(N_DEV, w.shape[0] // N_DEV, w.shape[1])
    return w.reshape(w.shape[0], N_DEV, w.shape[1] // N_DEV).transpose(1, 0, 2)


def kernel(x, c, ctx, c_ctx, mod_w, mod_b, pre_norm, post_norm, w_in, mla_q_norm, mla_w_uq, mla_kv_norm, mla_w_ukv, pool_w, pool_scale, gla_af_w2, gla_af_b, gla_ab_w2, gla_ab_b, gla_norm, w_branch_mla, w_branch_pool, w_branch_gla, w_out, loss_target, m_c_ctx, m_mod_w, m_mod_b, m_pre_norm, m_post_norm, m_w_in, m_mla_q_norm, m_mla_w_uq, m_mla_kv_norm, m_mla_w_ukv, m_pool_w, m_pool_scale, m_gla_af_w2, m_gla_af_b, m_gla_ab_w2, m_gla_ab_b, m_gla_norm, m_w_branch_mla, m_w_branch_pool, m_w_branch_gla, m_w_out, v_c_ctx, v_mod_w, v_mod_b, v_pre_norm, v_post_norm, v_w_in, v_mla_q_norm, v_mla_w_uq, v_mla_kv_norm, v_mla_w_ukv, v_pool_w, v_pool_scale, v_gla_af_w2, v_gla_af_b, v_gla_ab_w2, v_gla_ab_b, v_gla_norm, v_w_branch_mla, v_w_branch_pool, v_w_branch_gla, v_w_out):
    local = dict(locals())
    wts = {n: local[n] for n in ARG_WEIGHTS}
    mom1 = {n: local["m_" + n] for n in ARG_WEIGHTS}
    mom2 = {n: local["v_" + n] for n in ARG_WEIGHTS}
    shard_shapes = [wts[n].shape for n in SHARDED]
    rep_shapes = [wts[n].shape for n in REPLICATED]
    kinds = ("grad", "delta", "new_m", "new_v")

    depth = w_in.shape[0]

    def shards(l):
        return [wts[n][l].astype(BF16) for n in SHARDED]

    first = []

    def join(ctx_, x_):
        x2, got = join_tokens(ctx_, x_, _Hosted(_Gather, shards(0)), "join_tokens")
        first.extend(got)
        return x2

    def layer_full(l, carried):
        return {n: _gathered_to_full(gw, n) for n, gw in zip(SHARDED, first if l == 0 else carried)}

    def host_fwd(l):
        return _Hosted(_Gather, shards(l + 1)) if l + 1 < depth else None

    exchanged = GRADS_EARLY + GRADS_LATE

    def slots(g, names):
        return [_full_to_slots(g[n], n).astype(BF16) for n in names]

    def host_bwd(l, g_above):
        return _Hosted(_Scatter, slots(g_above, exchanged))

    def host_own(l):
        return (lambda names, g: _Hosted(_Scatter, slots(g, names))) if l == 0 else None

    small = {n: wts[n] for n in REPLICATED}
    loss, grad_x, grads, g_c_ctx, arrived, (a8, dz8s) = local_step(x, c, ctx, c_ctx, small, loss_target, depth, layer_full, host_fwd, host_bwd, host_own, join)
    arrived = [a if isinstance(a, dict) else dict(zip(exchanged, a)) for a in arrived]

    g = {n: (g_c_ctx if n == "c_ctx" else jnp.stack([grads[l][n] for l in range(depth)])) for n in REPLICATED}
    gathered, a_all, dz_all = gather_blocks([_pack([g[n] for n in REPLICATED], BF16), a8, jnp.concatenate(dz8s, axis=0)], "gather_small_grads")
    outs = reduce_adamw([gathered], _pack([wts[n] for n in REPLICATED], F32), _pack([mom1[n] for n in REPLICATED], F32),
                        _pack([mom2[n] for n in REPLICATED], F32), "adamw_replicated")
    me = 4 * lax.axis_index("x") + 2 * lax.axis_index("y") + lax.axis_index("c")
    ncol = mod_w.shape[2]
    dz_cols = lax.dynamic_slice_in_dim(dz_all.reshape(N_DEV, depth, 8, 3 * D_MODEL), me * ncol, ncol, axis=3)
    g_mod_w = mod_dw_columns(a_all.reshape(N_DEV * 8, D_MODEL), dz_cols.transpose(1, 0, 2, 3).reshape(depth, N_DEV * 8, ncol), "mod_dw")

    res = {kind: {} for kind in kinds}
    for n, shp in zip(SHARDED, shard_shapes):
        flat = (shp[0] * shp[1], shp[2])
        parts = [g_mod_w.reshape((1,) + flat)] if n == "mod_w" else [arrived[l][n] for l in range(depth)]
        for kind, o in zip(kinds, reduce_adamw(parts, wts[n].reshape(flat), mom1[n].reshape(flat), mom2[n].reshape(flat), "adamw_" + n)):
            res[kind][n] = o.reshape(shp)
    for kind, o in zip(("grad", "delta", "new_m", "new_v"), outs):
        res[kind].update(zip(REPLICATED, _unpack(o, rep_shapes)))

    loss = lax.psum(loss[0, 0], ("x", "y", "c"))
    return (loss, grad_x, *[res[kind][n] for kind in ("grad", "delta", "new_m", "new_v") for n in ARG_WEIGHTS])
```
